```python
import jax, jax.numpy as jnp
from jax import lax
import numpy as np

D_MODEL = 2048
BATCH = 4
SEQ = 2048
DEPTH = 4
DEC_BATCH = 128
DEC_SEQ = 1
PAST_LEN = 16384
PAGE_SIZE = 128

N_EVEN = (DEPTH + 1) // 2
N_ODD = DEPTH // 2
N_MOD = 9
D_FF = 5632
NORM_EPS = 1e-6

A_HEADS = 16
A_HEAD_DIM = 64
A_WIDTH = A_HEADS * A_HEAD_DIM
A_DECAY_RANK = 64
A_ICL_RANK = 64
A_GATE_RANK = 128
A_PROJ = 3 * A_WIDTH + A_DECAY_RANK + A_ICL_RANK + A_GATE_RANK
A_SPLITS = (A_WIDTH, 2 * A_WIDTH, 3 * A_WIDTH, 3 * A_WIDTH + A_DECAY_RANK, 3 * A_WIDTH + A_DECAY_RANK + A_ICL_RANK)
A_GN_EPS = 64e-5

B_HEADS = 4
B_KEY_DIM = 128
B_VAL_DIM = 256
B_KEY_WIDTH = B_HEADS * B_KEY_DIM
B_WIDTH = B_HEADS * B_VAL_DIM
B_ALPHA_RANK = 16
B_TAU = 16.0
B_CHUNK = 64
B_PROJ = 2 * B_KEY_WIDTH + 2 * B_WIDTH + B_ALPHA_RANK
B_SPLITS = (B_KEY_WIDTH, 2 * B_KEY_WIDTH, 2 * B_KEY_WIDTH + B_WIDTH, 2 * B_KEY_WIDTH + 2 * B_WIDTH)

EVEN_PROJ = A_PROJ + B_PROJ
MIX_WIDTH = A_WIDTH + B_WIDTH

C_WIDTH = D_MODEL
C_BLOCKS = 8
C_BLOCK = C_WIDTH // C_BLOCKS
C_CONV = 4
C_POW = 8.0

F32 = jnp.float32

kernel_name = 'hybrid_rwkv7_gla_rglru_macaron_adaln_step'


def rms_norm(x, w=None):
    xf = x.astype(F32)
    y = xf * lax.rsqrt(jnp.mean(xf * xf, axis=-1, keepdims=True) + NORM_EPS)
    if w is not None:
        y = y * w.astype(F32)
    return y.astype(x.dtype)


def modulate(x, shift, scale):
    return rms_norm(x) * (1 + scale[:, None, :]) + shift[:, None, :]


def swiglu(h, w_up, w_down):
    g, u = jnp.split(h @ w_up, 2, axis=-1)
    return (jax.nn.silu(g) * u) @ w_down


def rwkv7_mix(u, shift_prev, S0, mu, w_up, a_up, g_up, vec):
    n, T, _ = u.shape
    full = jnp.concatenate([shift_prev[:, None, :].astype(u.dtype), u], axis=1)
    new_shift = full[:, -1]
    u = u + mu * (full[:, :-1] - u)
    r, k, v, w_lo, a_lo, g_lo = [t.astype(F32) for t in jnp.split(u, A_SPLITS, axis=-1)]
    w0, a0, k_k, k_a, r_k, ln_w, ln_b = [vec[i].astype(F32) for i in range(7)]
    w = -jax.nn.softplus(-(w0 + jnp.tanh(w_lo) @ w_up.astype(F32))) - 0.5
    decay = jnp.exp(-jnp.exp(w))
    a = jax.nn.sigmoid(a0 + a_lo @ a_up.astype(F32))
    g = jax.nn.sigmoid(g_lo) @ g_up.astype(F32)
    hd = lambda t: t.reshape(n, T, A_HEADS, A_HEAD_DIM)
    kk = hd(k * k_k)
    kk = kk * lax.rsqrt(jnp.maximum(jnp.sum(kk * kk, axis=-1, keepdims=True), 1e-24))
    k = k * (1 + (a - 1) * k_a)
    r_h, k_h, v_h, a_h, w_h = hd(r), hd(k), hd(v), hd(a), hd(decay)

    def step(S, inp):
        r_t, w_t, k_t, v_t, kk_t, a_t = inp
        s_kk = jnp.einsum('bhij,bhj->bhi', S, -kk_t)
        S = (S * w_t[:, :, None, :] + s_kk[..., None] * (kk_t * a_t)[:, :, None, :]
             + v_t[..., None] * k_t[:, :, None, :])
        return S, jnp.einsum('bhij,bhj->bhi', S, r_t)

    seq = tuple(jnp.swapaxes(t, 0, 1) for t in (r_h, w_h, k_h, v_h, kk, a_h))
    S_T, y = lax.scan(step, S0.astype(F32), seq)
    y = jnp.swapaxes(y, 0, 1)
    yc = y - jnp.mean(y, axis=-1, keepdims=True)
    y = yc * lax.rsqrt(jnp.mean(yc * yc, axis=-1, keepdims=True) + A_GN_EPS)
    y = y.reshape(n, T, A_WIDTH) * ln_w + ln_b
    bonus = jnp.sum(r_h * k_h * r_k.reshape(A_HEADS, A_HEAD_DIM), axis=-1, keepdims=True) * v_h
    y = (y + bonus.reshape(n, T, A_WIDTH)) * g
    return y.astype(u.dtype), new_shift, S_T


def gla_chunked(q, k, v, log_a, S0):
    n, T = q.shape[:2]
    C = min(B_CHUNK, T)
    nc = -(-T // C)
    pad = nc * C - T

    def blocks(t):
        t = jnp.pad(t, ((0, 0), (0, pad), (0, 0), (0, 0)))
        return t.reshape(n, nc, C, B_HEADS, t.shape[-1]).transpose(1, 0, 3, 2, 4)

    qb, kb, vb = blocks(q), blocks(k), blocks(v)
    cum = jnp.cumsum(blocks(log_a), axis=3)
    causal = jnp.tril(jnp.ones((C, C), dtype=bool))[:, :, None]

    def chunk_step(S, inp):
        q_c, k_c, v_c, b_c = inp
        rel = jnp.where(causal, b_c[:, :, :, None, :] - b_c[:, :, None, :, :], -jnp.inf)
        scores = jnp.einsum('bhid,bhjd,bhijd->bhij', q_c, k_c, jnp.exp(rel))
        o = scores @ v_c + jnp.einsum('bhid,bhdv->bhiv', q_c * jnp.exp(b_c), S)
        b_end = b_c[:, :, -1:, :]
        S = (S * jnp.exp(b_end)[:, :, 0, :, None]
             + jnp.einsum('bhjd,bhjv->bhdv', k_c * jnp.exp(b_end - b_c), v_c))
        return S, o

    S_T, o = lax.scan(chunk_step, S0, (qb, kb, vb, cum))
    o = o.transpose(1, 0, 3, 2, 4).reshape(n, nc * C, B_HEADS, B_VAL_DIM)[:, :T]
    return o, S_T


def gla_mix(u, S0, alpha_up, alpha_b, norm_w):
    n, T, _ = u.shape
    q, k, v, g, a_lo = jnp.split(u.astype(F32), B_SPLITS, axis=-1)
    log_a = jax.nn.log_sigmoid(a_lo @ alpha_up.astype(F32) + alpha_b.astype(F32)) / B_TAU
    hk = lambda t: t.reshape(n, T, B_HEADS, B_KEY_DIM)
    o, S_T = gla_chunked(hk(q) * B_KEY_DIM ** -0.5, hk(k), v.reshape(n, T, B_HEADS, B_VAL_DIM),
                         hk(log_a), S0.astype(F32))
    o = rms_norm(o).reshape(n, T, B_WIDTH) * norm_w.astype(F32) * jax.nn.silu(g)
    return o.astype(u.dtype), S_T


def linear_scan(a, b, h0):
    b = b.at[:, 0].add(a[:, 0] * h0)

    def combine(lhs, rhs):
        return lhs[0] * rhs[0], rhs[0] * lhs[1] + rhs[1]

    _, h = lax.associative_scan(combine, (a, b), axis=1)
    return h


def rglru_mix(u, conv_prev, h0, conv_w, conv_b, wa, ba, wx, bx, lam):
    n, T, _ = u.shape
    gate, xb = jnp.split(u, 2, axis=-1)
    full = jnp.concatenate([conv_prev.astype(u.dtype), xb], axis=1)
    xc = conv_b + sum(full[:, i:i + T] * conv_w[i] for i in range(C_CONV))
    new_conv = full[:, T:]
    xf = xc.astype(F32)
    xblk = xf.reshape(n, T, C_BLOCKS, C_BLOCK)
    r = jax.nn.sigmoid(jnp.einsum('btnc,ncd->btnd', xblk, wa.astype(F32)).reshape(n, T, C_WIDTH) + ba.astype(F32))
    i_g = jax.nn.sigmoid(jnp.einsum('btnc,ncd->btnd', xblk, wx.astype(F32)).reshape(n, T, C_WIDTH) + bx.astype(F32))
    log_a = -C_POW * r * jax.nn.softplus(-lam.astype(F32))
    a = jnp.exp(log_a)
    b = jnp.sqrt(-jnp.expm1(2 * log_a)) * (i_g * xf)
    h = linear_scan(a, b, h0.astype(F32))
    y = jax.nn.gelu(gate.astype(F32)) * h
    return y.astype(u.dtype), new_conv, h[:, -1]


def trunk(x, c, st_shift, st_wkv, st_gla, st_conv, st_lru, w_ada, b_ada, w_ffn_up, w_ffn_down,
          w_in_even, w_out_even, a_mu, a_w_up, a_a_up, a_g_up, a_vec, b_alpha_up, b_alpha_b, b_norm_w,
          w_in_odd, w_out_odd, c_conv_w, c_conv_b, c_wa, c_ba, c_wx, c_bx, c_lam, final_norm_w):
    n = x.shape[0]
    c_act = jax.nn.silu(c)
    out_shift, out_wkv, out_gla, out_conv, out_lru = [], [], [], [], []
    for l in range(DEPTH):
        j = l // 2
        mod = (c_act @ w_ada[l] + b_ada[l]).reshape(n, N_MOD, D_MODEL)
        sh1, sc1, g1, sh2, sc2, g2, sh3, sc3, g3 = [mod[:, m] for m in range(N_MOD)]
        x = x + 0.5 * g1[:, None] * swiglu(modulate(x, sh1, sc1), w_ffn_up[l, 0], w_ffn_down[l, 0])
        h = modulate(x, sh2, sc2)
        if l % 2 == 0:
            u = h @ w_in_even[j]
            y_a, s_shift, s_wkv = rwkv7_mix(u[..., :A_PROJ], st_shift[j], st_wkv[j], a_mu[j],
                                            a_w_up[j], a_a_up[j], a_g_up[j], a_vec[j])
            y_b, s_gla = gla_mix(u[..., A_PROJ:], st_gla[j], b_alpha_up[j], b_alpha_b[j], b_norm_w[j])
            y = jnp.concatenate([y_a, y_b], axis=-1) @ w_out_even[j]
            out_shift.append(s_shift)
            out_wkv.append(s_wkv)
            out_gla.append(s_gla)
        else:
            u = h @ w_in_odd[j]
            y_c, s_conv, s_lru = rglru_mix(u, st_conv[j], st_lru[j], c_conv_w[j], c_conv_b[j],
                                           c_wa[j], c_ba[j], c_wx[j], c_bx[j], c_lam[j])
            y = y_c @ w_out_odd[j]
            out_conv.append(s_conv)
            out_lru.append(s_lru)
        x = x + g2[:, None] * y
        x = x + 0.5 * g3[:, None] * swiglu(modulate(x, sh3, sc3), w_ffn_up[l, 1], w_ffn_down[l, 1])
    y = rms_norm(x, final_norm_w)
    new_state = (jnp.stack(out_shift).astype(st_shift.dtype), jnp.stack(out_wkv).astype(st_wkv.dtype),
                 jnp.stack(out_gla).astype(st_gla.dtype), jnp.stack(out_conv).astype(st_conv.dtype),
                 jnp.stack(out_lru).astype(st_lru.dtype))
    return y, new_state


def setup_inputs(seed: int = 0) -> dict:
    key = jax.random.key(seed)
    keys = iter(jax.random.split(key, 48))

    def nrm(shape, scale):
        return jax.random.normal(next(keys), shape, F32) * scale

    def uni(shape, lo, hi):
        return jax.random.uniform(next(keys), shape, F32, lo, hi)

    D = D_MODEL
    mod_base = jnp.repeat(jnp.array([0.0, 0.0, 1.0] * 3, F32), D)
    a_base = jnp.stack([jnp.linspace(-6.5, -1.5, A_WIDTH, dtype=F32),
                        jnp.zeros((A_WIDTH,), F32),
                        jnp.full((A_WIDTH,), 0.85, F32),
                        jnp.ones((A_WIDTH,), F32),
                        jnp.full((A_WIDTH,), -0.04, F32),
                        jnp.ones((A_WIDTH,), F32),
                        jnp.zeros((A_WIDTH,), F32)])
    lru_a = uni((N_ODD, C_WIDTH), 0.9, 0.999) ** (1.0 / C_POW)
    return {
        'x_prompt': nrm((BATCH, SEQ, D), 1.0),
        'x_sample': nrm((DEC_BATCH, DEC_SEQ, D), 1.0),
        'c_prompt': nrm((BATCH, D), 1.0),
        'c_sample': nrm((DEC_BATCH, D), 1.0),
        'state_rwkv_shift': nrm((N_EVEN, DEC_BATCH, A_PROJ), 1.0),
        'state_rwkv_wkv': nrm((N_EVEN, DEC_BATCH, A_HEADS, A_HEAD_DIM, A_HEAD_DIM), 0.1),
        'state_gla': nrm((N_EVEN, DEC_BATCH, B_HEADS, B_KEY_DIM, B_VAL_DIM), 0.5),
        'state_conv': nrm((N_ODD, DEC_BATCH, C_CONV - 1, C_WIDTH), 1.0),
        'state_lru': nrm((N_ODD, DEC_BATCH, C_WIDTH), 0.5),
        'w_ada': nrm((DEPTH, D, N_MOD * D), 0.1 * D ** -0.5),
        'b_ada': mod_base[None] + nrm((DEPTH, N_MOD * D), 0.02),
        'w_ffn_up': nrm((DEPTH, 2, D, 2 * D_FF), D ** -0.5),
        'w_ffn_down': nrm((DEPTH, 2, D_FF, D), D_FF ** -0.5),
        'w_in_even': nrm((N_EVEN, D, EVEN_PROJ), D ** -0.5),
        'w_out_even': nrm((N_EVEN, MIX_WIDTH, D), MIX_WIDTH ** -0.5),
        'a_mu': uni((N_EVEN, A_PROJ), 0.0, 1.0),
        'a_w_up': nrm((N_EVEN, A_DECAY_RANK, A_WIDTH), 0.1),
        'a_a_up': nrm((N_EVEN, A_ICL_RANK, A_WIDTH), 0.1),
        'a_g_up': nrm((N_EVEN, A_GATE_RANK, A_WIDTH), A_GATE_RANK ** -0.5),
        'a_vec': a_base[None] + nrm((N_EVEN, 7, A_WIDTH), 0.02),
        'b_alpha_up': nrm((N_EVEN, B_ALPHA_RANK, B_KEY_WIDTH), B_ALPHA_RANK ** -0.5),
        'b_alpha_b': nrm((N_EVEN, B_KEY_WIDTH), 0.5),
        'b_norm_w': 1.0 + nrm((N_EVEN, B_WIDTH), 0.02),
        'w_in_odd': nrm((N_ODD, D, 2 * C_WIDTH), D ** -0.5),
        'w_out_odd': nrm((N_ODD, C_WIDTH, D), C_WIDTH ** -0.5),
        'c_conv_w': nrm((N_ODD, C_CONV, C_WIDTH), C_CONV ** -0.5),
        'c_conv_b': nrm((N_ODD, C_WIDTH), 0.02),
        'c_wa': nrm((N_ODD, C_BLOCKS, C_BLOCK, C_BLOCK), C_BLOCK ** -0.5),
        'c_ba': nrm((N_ODD, C_WIDTH), 0.02),
        'c_wx': nrm((N_ODD, C_BLOCKS, C_BLOCK, C_BLOCK), C_BLOCK ** -0.5),
        'c_bx': nrm((N_ODD, C_WIDTH), 0.02),
        'c_lam': jnp.log(lru_a) - jnp.log1p(-lru_a),
        'final_norm_w': 1.0 + nrm((D,), 0.02),
    }


def reference(x_prompt, x_sample, c_prompt, c_sample, state_rwkv_shift, state_rwkv_wkv, state_gla,
              state_conv, state_lru, w_ada, b_ada, w_ffn_up, w_ffn_down, w_in_even, w_out_even, a_mu,
              a_w_up, a_a_up, a_g_up, a_vec, b_alpha_up, b_alpha_b, b_norm_w, w_in_odd, w_out_odd,
              c_conv_w, c_conv_b, c_wa, c_ba, c_wx, c_bx, c_lam, final_norm_w):
    weights = (w_ada, b_ada, w_ffn_up, w_ffn_down, w_in_even, w_out_even, a_mu, a_w_up, a_a_up, a_g_up,
               a_vec, b_alpha_up, b_alpha_b, b_norm_w, w_in_odd, w_out_odd, c_conv_w, c_conv_b, c_wa,
               c_ba, c_wx, c_bx, c_lam, final_norm_w)
    n_p = x_prompt.shape[0]
    fresh = lambda s: jnp.zeros((s.shape[0], n_p) + s.shape[2:], s.dtype)
    y_prompt, (p_shift, p_wkv, p_gla, p_conv, p_lru) = trunk(
        x_prompt, c_prompt, fresh(state_rwkv_shift), fresh(state_rwkv_wkv), fresh(state_gla),
        fresh(state_conv), fresh(state_lru), *weights)
    y_sample, (s_shift, s_wkv, s_gla, s_conv, s_lru) = trunk(
        x_sample, c_sample, state_rwkv_shift, state_rwkv_wkv, state_gla, state_conv, state_lru, *weights)
    return (y_prompt, y_sample, p_shift, p_wkv, p_gla, p_conv, p_lru, s_shift, s_wkv, s_gla, s_conv, s_lru)
```

```python
import functools

import jax
import jax.numpy as jnp
from jax import lax
from jax.experimental import pallas as pl
from jax.experimental.pallas import tpu as pltpu

F32 = jnp.float32
BF16 = jnp.bfloat16

D_MODEL = 2048
DEPTH = 4
N_MOD = 9
D_FF = 5632
NORM_EPS = 1e-6

A_HEADS = 16
A_HEAD_DIM = 64
A_WIDTH = A_HEADS * A_HEAD_DIM
A_DECAY_RANK = 64
A_ICL_RANK = 64
A_GATE_RANK = 128
A_PROJ = 3 * A_WIDTH + A_DECAY_RANK + A_ICL_RANK + A_GATE_RANK
A_GN_EPS = 64e-5

B_HEADS = 4
B_KEY_DIM = 128
B_VAL_DIM = 256
B_KEY_WIDTH = B_HEADS * B_KEY_DIM
B_WIDTH = B_HEADS * B_VAL_DIM
B_ALPHA_RANK = 16
B_TAU = 16.0
B_PROJ = 2 * B_KEY_WIDTH + 2 * B_WIDTH + B_ALPHA_RANK
EVEN_PROJ = A_PROJ + B_PROJ

C_WIDTH = D_MODEL
C_BLOCKS = 8
C_BLOCK = C_WIDTH // C_BLOCKS
C_CONV = 4
C_POW = 8.0

CHUNK = 64
LANES = 128
VMEM_LIMIT = 56 * 1024 * 1024


def _params(*sem):
    return pltpu.CompilerParams(dimension_semantics=sem, vmem_limit_bytes=VMEM_LIMIT)


def _bdot(a, b):
    return jnp.dot(a.astype(BF16), b.astype(BF16), preferred_element_type=F32)


def _split3(x):
    hi = x.astype(BF16)
    r1 = x - hi.astype(F32)
    mid = r1.astype(BF16)
    lo = (r1 - mid.astype(F32)).astype(BF16)
    return hi, mid, lo


def _dot_exact_rhs(a_bf16, x):
    hi, mid, lo = _split3(x)
    f = lambda y: jnp.dot(a_bf16, y, preferred_element_type=F32)
    return f(hi) + f(mid) + f(lo)


def _dot_exact_lhs(x, b_bf16):
    hi, mid, lo = _split3(x)
    f = lambda y: jnp.dot(y, b_bf16, preferred_element_type=F32)
    return f(hi) + f(mid) + f(lo)


def _sigmoid(x):
    return jax.nn.sigmoid(x)


def _softplus(x):
    return jnp.maximum(x, 0.0) + jnp.log1p(jnp.exp(-jnp.abs(x)))


def _gelu_tanh(x):
    return 0.5 * x * (1.0 + jnp.tanh(0.7978845608028654 * (x + 0.044715 * (x * x * x))))


def _norm_mod(x, shift, scale):
    ms = jnp.mean(x * x, axis=-1, keepdims=True)
    return x * lax.rsqrt(ms + NORM_EPS) * (1.0 + scale) + shift


def _ada_kernel(c_ref, w_ref, b_ref, o_ref):
    c = c_ref[...]
    act = (c * _sigmoid(c)).astype(BF16)
    o_ref[...] = jnp.dot(act, w_ref[...].astype(BF16), preferred_element_type=F32) + b_ref[...]


def _ada(c_all, w_ada, b_ada):
    rows = c_all.shape[0]
    tn = 1024
    per = D_MODEL // tn
    b4 = b_ada.reshape(DEPTH, N_MOD * per, 1, tn)
    return pl.pallas_call(
        _ada_kernel,
        grid=(DEPTH, N_MOD * per),
        in_specs=[
            pl.BlockSpec((rows, D_MODEL), lambda l, j: (0, 0)),
            pl.BlockSpec((None, D_MODEL, tn), lambda l, j: (l, 0, j)),
            pl.BlockSpec((None, None, 1, tn), lambda l, j: (l, j, 0, 0)),
        ],
        out_specs=pl.BlockSpec((None, None, rows, tn), lambda l, j: (l, j // per, 0, j % per)),
        out_shape=jax.ShapeDtypeStruct((DEPTH, N_MOD, rows, D_MODEL), F32),
        compiler_params=_params("arbitrary", "arbitrary"),
        name="ada",
    )(c_all, w_ada, b4)


class _Mod:
    def __init__(self, arr, per_row, seq_len):
        self.arr = arr
        self.per_row = per_row
        self.seq_len = seq_len

    def spec(self, layer, m, tm):
        if self.per_row:
            return pl.BlockSpec((None, None, tm, D_MODEL), lambda i, *_: (layer, m, i, 0))
        per_seq = self.seq_len // tm
        return pl.BlockSpec((None, None, None, 1, D_MODEL), lambda i, *_: (layer, m, i // per_seq, 0, 0))


def _ffn_kernel(x_ref, sh_ref, sc_ref, g_ref, wg_ref, wu_ref, wd_ref, *rest, n_j, sub, final):
    if final:
        fw_ref, o_ref, h_ref = rest
    else:
        o_ref, h_ref = rest
    j = pl.program_id(1)
    tm = x_ref.shape[0]

    @pl.when(j == 0)
    def _():
        def body(s, carry):
            rows = pl.ds(pl.multiple_of(s * sub, sub), sub)
            sh = sh_ref[...] if sh_ref.shape[0] == 1 else sh_ref[rows, :]
            sc = sc_ref[...] if sc_ref.shape[0] == 1 else sc_ref[rows, :]
            h_ref[rows, :] = _norm_mod(x_ref[rows, :], sh, sc).astype(BF16)
            return carry
        lax.fori_loop(0, tm // sub, body, 0)

    h = h_ref[...]
    gt = jnp.dot(h, wg_ref[...].astype(BF16), preferred_element_type=F32)
    ut = jnp.dot(h, wu_ref[...].astype(BF16), preferred_element_type=F32)
    act = (gt * _sigmoid(gt) * ut).astype(BF16)
    part = jnp.dot(act, wd_ref[...].astype(BF16), preferred_element_type=F32)

    @pl.when(j == 0)
    def _():
        o_ref[...] = part

    @pl.when(j > 0)
    def _():
        o_ref[...] += part

    @pl.when(j == n_j - 1)
    def _():
        def body(s, carry):
            rows = pl.ds(pl.multiple_of(s * sub, sub), sub)
            g = g_ref[...] if g_ref.shape[0] == 1 else g_ref[rows, :]
            y = x_ref[rows, :] + 0.5 * g * o_ref[rows, :]
            if final:
                ms = jnp.mean(y * y, axis=-1, keepdims=True)
                y = y * lax.rsqrt(ms + NORM_EPS) * fw_ref[...]
            o_ref[rows, :] = y
            return carry
        lax.fori_loop(0, tm // sub, body, 0)


def _ffn(x, mod, layer, which, w_up, w_down, tm, tf, final_w=None):
    m_rows = x.shape[0]
    n_j = D_FF // tf
    m0 = 6 * which
    sub = min(tm, 128)
    final = final_w is not None
    in_specs = [
        pl.BlockSpec((tm, D_MODEL), lambda i, j: (i, 0), pipeline_mode=pl.Buffered(1)),
        mod.spec(layer, m0, tm), mod.spec(layer, m0 + 1, tm), mod.spec(layer, m0 + 2, tm),
        pl.BlockSpec((None, None, D_MODEL, tf), lambda i, j: (layer, which, 0, j)),
        pl.BlockSpec((None, None, D_MODEL, tf), lambda i, j: (layer, which, 0, j + n_j)),
        pl.BlockSpec((None, None, tf, D_MODEL), lambda i, j: (layer, which, j, 0)),
    ]
    args = [x, mod.arr, mod.arr, mod.arr, w_up, w_up, w_down]
    if final:
        in_specs.append(pl.BlockSpec((1, D_MODEL), lambda i, j: (0, 0)))
        args.append(final_w.reshape(1, D_MODEL))
    return pl.pallas_call(
        functools.partial(_ffn_kernel, n_j=n_j, sub=sub, final=final),
        grid=(m_rows // tm, n_j),
        in_specs=in_specs,
        out_specs=pl.BlockSpec((tm, D_MODEL), lambda i, j: (i, 0)),
        out_shape=jax.ShapeDtypeStruct((m_rows, D_MODEL), F32),
        scratch_shapes=[pltpu.VMEM((tm, D_MODEL), BF16)],
        compiler_params=_params("arbitrary", "arbitrary"),
        name="ffn",
    )(*args)


def _inproj_kernel(x_ref, sh_ref, sc_ref, w_ref, o_ref, h_ref, *, sub):
    j = pl.program_id(1)
    tm = x_ref.shape[0]

    @pl.when(j == 0)
    def _():
        def body(s, carry):
            rows = pl.ds(pl.multiple_of(s * sub, sub), sub)
            sh = sh_ref[...] if sh_ref.shape[0] == 1 else sh_ref[rows, :]
            sc = sc_ref[...] if sc_ref.shape[0] == 1 else sc_ref[rows, :]
            h_ref[rows, :] = _norm_mod(x_ref[rows, :], sh, sc).astype(BF16)
            return carry
        lax.fori_loop(0, tm // sub, body, 0)

    o_ref[...] = jnp.dot(h_ref[...], w_ref[...].astype(BF16), preferred_element_type=F32)


def _inproj(x, mod, layer, w, widx, tm, tn):
    m_rows = x.shape[0]
    n_out = w.shape[-1]
    sub = min(tm, 128)
    return pl.pallas_call(
        functools.partial(_inproj_kernel, sub=sub),
        grid=(m_rows // tm, pl.cdiv(n_out, tn)),
        in_specs=[
            pl.BlockSpec((tm, D_MODEL), lambda i, j: (i, 0)),
            mod.spec(layer, 3, tm), mod.spec(layer, 4, tm),
            pl.BlockSpec((None, D_MODEL, tn), lambda i, j: (widx, 0, j)),
        ],
        out_specs=pl.BlockSpec((tm, tn), lambda i, j: (i, j)),
        out_shape=jax.ShapeDtypeStruct((m_rows, n_out), F32),
        scratch_shapes=[pltpu.VMEM((tm, D_MODEL), BF16)],
        compiler_params=_params("arbitrary", "arbitrary"),
        name="inproj",
    )(x, mod.arr, mod.arr, w)


def _outproj_kernel(*refs, n_in):
    y_refs = refs[:n_in]
    w_refs = refs[n_in:2 * n_in]
    x_ref, g_ref, o_ref = refs[2 * n_in:]
    acc = _bdot(y_refs[0][...], w_refs[0][...])
    for y_ref, w_ref in zip(y_refs[1:], w_refs[1:]):
        acc = acc + _bdot(y_ref[...], w_ref[...])
    o_ref[...] = x_ref[...] + g_ref[...] * acc


def _outproj(ys, w, widx, x, mod, layer, tm, tn):
    m_rows = x.shape[0]
    n_in = len(ys)
    kw = ys[0].shape[1]
    gspec = mod.spec(layer, 5, tm)
    if mod.per_row:
        gate_spec = pl.BlockSpec((None, None, tm, tn), lambda i, j: (layer, 5, i, j))
    else:
        per_seq = mod.seq_len // tm
        gate_spec = pl.BlockSpec((None, None, None, 1, tn), lambda i, j: (layer, 5, i // per_seq, 0, j))
    del gspec
    in_specs = [pl.BlockSpec((tm, kw), lambda i, j: (i, 0)) for _ in ys]
    in_specs += [pl.BlockSpec((None, kw, tn), lambda i, j, k=k: (widx, k, j)) for k in range(n_in)]
    in_specs += [pl.BlockSpec((tm, tn), lambda i, j: (i, j)), gate_spec]
    return pl.pallas_call(
        functools.partial(_outproj_kernel, n_in=n_in),
        grid=(m_rows // tm, D_MODEL // tn),
        in_specs=in_specs,
        out_specs=pl.BlockSpec((tm, tn), lambda i, j: (i, j)),
        out_shape=jax.ShapeDtypeStruct((m_rows, D_MODEL), F32),
        compiler_params=_params("arbitrary", "arbitrary"),
        name="outproj",
    )(*ys, *([w] * n_in), x, mod.arr)


def _rwkv_prep_kernel(u_ref, prev_ref, mu_ref, wup_ref, aup_ref, gup_ref, vec_ref, seg_ref,
                      r_ref, lw_ref, k_ref, v_ref, kk_ref, a_ref, g_ref, bonus_ref, carry_ref, *, seq_mode):
    u = u_ref[...]
    tm = u.shape[0]
    if seq_mode:
        @pl.when(pl.program_id(1) == 0)
        def _():
            carry_ref[...] = prev_ref[...]
        rolled = pltpu.roll(u, 1, axis=0)
        row = lax.broadcasted_iota(jnp.int32, u.shape, 0)
        prev = jnp.where(row == 0, carry_ref[...], rolled)
        carry_ref[...] = u[tm - 1:tm, :]
    else:
        prev = prev_ref[...]
    xs = u + mu_ref[...] * (prev - u)
    r = xs[:, 0:A_WIDTH]
    k = xs[:, A_WIDTH:2 * A_WIDTH]
    v = xs[:, 2 * A_WIDTH:3 * A_WIDTH]
    wa_lo = xs[:, 3 * A_WIDTH:3 * A_WIDTH + LANES]
    g_lo = xs[:, 3 * A_WIDTH + LANES:]
    w0, a0, k_k, k_a, r_k = (vec_ref[i:i + 1, :] for i in range(5))
    w_raw = -_softplus(-(w0 + _bdot(jnp.tanh(wa_lo), wup_ref[...]))) - 0.5
    lw = -jnp.exp(w_raw)
    a = _sigmoid(a0 + _bdot(wa_lo, aup_ref[...]))
    g = _bdot(_sigmoid(g_lo), gup_ref[...])
    kk = k * k_k
    seg = seg_ref[...]
    kk = kk * lax.rsqrt(jnp.maximum(_dot_exact_lhs(kk * kk, seg), 1e-24))
    k2 = k * (1.0 + (a - 1.0) * k_a)
    bonus = _dot_exact_lhs(r * k2 * r_k, seg) * v
    r_ref[...] = r
    lw_ref[...] = lw
    k_ref[...] = k2
    v_ref[...] = v
    kk_ref[...] = kk
    a_ref[...] = a
    g_ref[...] = g
    bonus_ref[...] = bonus


def _rwkv_prep(u, prev, mu, wup_pad, aup_pad, g_up, vec, seg, n_seq, tm, seq_mode):
    rows = u.shape[0]
    per_seq = rows // n_seq // tm if seq_mode else 1
    grid = (n_seq, per_seq) if seq_mode else (rows // tm, 1)
    rowmap = (lambda b, t: (b * per_seq + t, 0)) if seq_mode else (lambda b, t: (b, 0))
    prev_spec = (pl.BlockSpec((None, 1, A_PROJ), lambda b, t: (b, 0, 0)) if seq_mode
                 else pl.BlockSpec((tm, A_PROJ), rowmap))
    const = lambda shape: pl.BlockSpec(shape, lambda b, t: (0,) * len(shape))
    out = jax.ShapeDtypeStruct((rows, A_WIDTH), F32)
    return pl.pallas_call(
        functools.partial(_rwkv_prep_kernel, seq_mode=seq_mode),
        grid=grid,
        in_specs=[
            pl.BlockSpec((tm, A_PROJ), rowmap), prev_spec,
            const((1, A_PROJ)), const((LANES, A_WIDTH)), const((LANES, A_WIDTH)),
            const((A_GATE_RANK, A_WIDTH)), const((8, A_WIDTH)), const((A_WIDTH, A_WIDTH)),
        ],
        out_specs=[pl.BlockSpec((tm, A_WIDTH), rowmap)] * 8,
        out_shape=[out] * 8,
        scratch_shapes=[pltpu.VMEM((1, A_PROJ), F32)],
        compiler_params=_params("arbitrary", "arbitrary"),
        name="rwkv_prep",
    )(u, prev, mu, wup_pad, aup_pad, g_up, vec, seg)


def _rwkv_post_kernel(y_ref, bonus_ref, g_ref, vec_ref, seg_ref, o_ref):
    y = y_ref[...]
    seg = seg_ref[...]
    inv_n = 1.0 / A_HEAD_DIM
    yc = y - _dot_exact_lhs(y, seg) * inv_n
    var = _dot_exact_lhs(yc * yc, seg) * inv_n
    yn = yc * lax.rsqrt(var + A_GN_EPS) * vec_ref[5:6, :] + vec_ref[6:7, :]
    o_ref[...] = (yn + bonus_ref[...]) * g_ref[...]


def _rwkv_post(y, bonus, g, vec, seg, tm):
    rows = y.shape[0]
    spec = pl.BlockSpec((tm, A_WIDTH), lambda i: (i, 0))
    return pl.pallas_call(
        _rwkv_post_kernel,
        grid=(rows // tm,),
        in_specs=[spec, spec, spec,
                  pl.BlockSpec((8, A_WIDTH), lambda i: (0, 0)),
                  pl.BlockSpec((A_WIDTH, A_WIDTH), lambda i: (0, 0))],
        out_specs=spec,
        out_shape=jax.ShapeDtypeStruct((rows, A_WIDTH), F32),
        compiler_params=_params("arbitrary"),
        name="rwkv_post",
    )(y, bonus, g, vec, seg)


def _rwkv_chunk_kernel(r_ref, lw_ref, k_ref, v_ref, kk_ref, a_ref, h0_ref, y_ref, hT_ref, h_ref):
    ci = pl.program_id(1)

    @pl.when(ci == 0)
    def _():
        h_ref[...] = h0_ref[...]

    r = r_ref[...]
    lw = lw_ref[...]
    k = k_ref[...]
    v = v_ref[...]
    kk = kk_ref[...]
    al = a_ref[...]
    nh, cs, nd = r.shape

    row = lax.broadcasted_iota(jnp.int32, (cs, cs), 0)
    col = lax.broadcasted_iota(jnp.int32, (cs, cs), 1)
    incl = (col <= row).astype(F32)
    strict = (col < row).astype(F32)
    eye_c = (col == row).astype(F32)
    nrow = lax.broadcasted_iota(jnp.int32, (nd, nd), 0)
    ncol = lax.broadcasted_iota(jnp.int32, (nd, nd), 1)
    eye_n = (nrow == ncol).astype(F32)

    def bmm(x, y):
        return jnp.einsum('hab,hbc->hac', x.astype(BF16), y.astype(BF16), preferred_element_type=F32)

    def bmm_nt(x, y):
        return jnp.einsum('han,hbn->hab', x.astype(BF16), y.astype(BF16), preferred_element_type=F32)

    def bmm_tn(x, y):
        return jnp.einsum('hca,hcb->hab', x.astype(BF16), y.astype(BF16), preferred_element_type=F32)

    tri = jnp.broadcast_to(incl.astype(BF16)[None], (nh, cs, cs))
    hi, mid, lo = _split3(lw)
    csum = lambda p: jnp.einsum('hab,hbc->hac', tri, p, preferred_element_type=F32)
    c = csum(hi) + csum(mid) + csum(lo)
    c_end = c[:, cs - 1:cs, :]
    e_neg = jnp.exp(-c)
    b_vec = kk * al
    rt = r * jnp.exp(c)
    at = -kk * jnp.exp(c - lw)
    bt = b_vec * e_neg
    kt = k * e_neg
    e_end = jnp.exp(c_end - c)
    bh = b_vec * e_end
    kh = k * e_end
    g_end = jnp.exp(c_end)

    a_ab = bmm_nt(at, bt) * strict
    a_ak = bmm_nt(at, kt) * strict
    m_b = bmm_nt(rt, bt) * incl
    m_k = bmm_nt(rt, kt) * incl

    tinv = eye_c + a_ab
    apow = a_ab
    span = 1
    while 2 * span < cs:
        apow = bmm(apow, apow)
        tinv = tinv + bmm(tinv, apow)
        span *= 2

    akv = bmm(a_ak, v)
    p = bmm(tinv, at)
    q = bmm(tinv, akv)
    p2 = rt + bmm(m_b, p)
    y0 = bmm(m_b, q) + bmm(m_k, v)
    gm = eye_n * g_end + bmm_tn(bh, p)
    hadd = bmm_tn(bh, q) + bmm_tn(kh, v)

    h = h_ref[...]
    y_ref[...] = bmm(p2, h) + y0
    h_new = bmm(gm, h) + hadd
    h_ref[...] = h_new

    @pl.when(ci == pl.num_programs(1) - 1)
    def _():
        hT_ref[...] = h_new


def _rwkv_chunk(r, lw, k, v, kk, a, h0):
    n, nh, t, nd = r.shape
    spec = pl.BlockSpec((None, nh, CHUNK, nd), lambda b, c: (b, 0, c, 0))
    hspec = pl.BlockSpec((None, nh, nd, nd), lambda b, c: (b, 0, 0, 0))
    return pl.pallas_call(
        _rwkv_chunk_kernel,
        grid=(n, t // CHUNK),
        in_specs=[spec] * 6 + [hspec],
        out_specs=[spec, hspec],
        out_shape=[jax.ShapeDtypeStruct((n, nh, t, nd), F32), jax.ShapeDtypeStruct((n, nh, nd, nd), F32)],
        scratch_shapes=[pltpu.VMEM((nh, nd, nd), F32)],
        compiler_params=_params("arbitrary", "arbitrary"),
        name="rwkv_chunk",
    )(r, lw, k, v, kk, a, h0)


def _rwkv_step_kernel(r_ref, lw_ref, k_ref, v_ref, kk_ref, a_ref, s_ref, y_ref, so_ref):
    s = s_ref[...]
    nd = s.shape[-1]
    r = r_ref[...]
    w = jnp.exp(lw_ref[...])
    k = k_ref[...]
    v = v_ref[...]
    kk = kk_ref[...]
    al = a_ref[...]
    eye = (lax.broadcasted_iota(jnp.int32, (nd, nd), 0) == lax.broadcasted_iota(jnp.int32, (nd, nd), 1)).astype(F32)
    s_kk = jnp.sum(s * (-kk), axis=-1, keepdims=True)
    v_col = jnp.sum(eye * v, axis=-1, keepdims=True)
    s_new = s * w + s_kk * (kk * al) + v_col * k
    y_col = jnp.sum(s_new * r, axis=-1, keepdims=True)
    y_ref[...] = jnp.sum(eye * y_col, axis=-2, keepdims=True)
    so_ref[...] = s_new


def _rwkv_step(r, lw, k, v, kk, a, s0, nb):
    n, nh, _, nd = r.shape
    rspec = pl.BlockSpec((nb, nh, 1, nd), lambda i: (i, 0, 0, 0))
    sspec = pl.BlockSpec((nb, nh, nd, nd), lambda i: (i, 0, 0, 0))
    return pl.pallas_call(
        _rwkv_step_kernel,
        grid=(n // nb,),
        in_specs=[rspec] * 6 + [sspec],
        out_specs=[rspec, sspec],
        out_shape=[jax.ShapeDtypeStruct((n, nh, 1, nd), F32), jax.ShapeDtypeStruct(s0.shape, F32)],
        compiler_params=_params("arbitrary"),
        name="rwkv_step",
    )(r, lw, k, v, kk, a, s0)


def _gla_log_alpha(a128, aup_ref, ab_ref):
    lane = lax.broadcasted_iota(jnp.int32, a128.shape, 1)
    a_lo = jnp.where(lane < B_ALPHA_RANK, a128, 0.0)
    x = _bdot(a_lo, aup_ref[...]) + ab_ref[...]
    return -_softplus(-x) * (1.0 / B_TAU)


def _gla_chunk_kernel(u_ref, aup_ref, ab_ref, nw_ref, s0_ref, o_ref, sT_ref, s_ref):
    ci = pl.program_id(1)

    @pl.when(ci == 0)
    def _():
        s_ref[...] = s0_ref[...]

    ub = u_ref[...]
    cs = ub.shape[0]
    q_all = ub[:, 0:B_KEY_WIDTH] * (B_KEY_DIM ** -0.5)
    k_all = ub[:, B_KEY_WIDTH:2 * B_KEY_WIDTH]
    v_all = ub[:, 2 * B_KEY_WIDTH:2 * B_KEY_WIDTH + B_WIDTH]
    g_all = ub[:, 2 * B_KEY_WIDTH + B_WIDTH:2 * B_KEY_WIDTH + 2 * B_WIDTH]
    a128 = ub[:, 2 * B_KEY_WIDTH + 2 * B_WIDTH:2 * B_KEY_WIDTH + 2 * B_WIDTH + LANES]
    log_a = _gla_log_alpha(a128, aup_ref, ab_ref)

    row = lax.broadcasted_iota(jnp.int32, (cs, cs), 0)
    col = lax.broadcasted_iota(jnp.int32, (cs, cs), 1)
    incl = col <= row
    cum = _dot_exact_rhs(incl.astype(BF16), log_a)
    e_pos = jnp.exp(cum)
    e_neg = jnp.exp(-cum)
    cum_end = cum[cs - 1:cs, :]
    e_end = jnp.exp(cum_end - cum)
    g_end = jnp.exp(cum_end)
    nk = B_KEY_DIM
    eye_k = (lax.broadcasted_iota(jnp.int32, (nk, nk), 0) == lax.broadcasted_iota(jnp.int32, (nk, nk), 1)).astype(F32)

    for h in range(B_HEADS):
        ks = slice(h * B_KEY_DIM, (h + 1) * B_KEY_DIM)
        vs = slice(h * B_VAL_DIM, (h + 1) * B_VAL_DIM)
        qe = q_all[:, ks] * e_pos[:, ks]
        kn = k_all[:, ks] * e_neg[:, ks]
        ke = k_all[:, ks] * e_end[:, ks]
        vh = v_all[:, vs]
        s = s_ref[h]
        scores = jnp.einsum('id,jd->ij', qe.astype(BF16), kn.astype(BF16), preferred_element_type=F32)
        scores = jnp.where(incl, scores, 0.0)
        o = _bdot(scores, vh) + _bdot(qe, s)
        g_col = jnp.sum(eye_k * g_end[:, ks], axis=-1, keepdims=True)
        s_ref[h] = s * g_col + jnp.einsum('jd,jv->dv', ke.astype(BF16), vh.astype(BF16),
                                           preferred_element_type=F32)
        ms = jnp.mean(o * o, axis=-1, keepdims=True)
        gh = g_all[:, vs]
        o_ref[:, vs] = o * lax.rsqrt(ms + NORM_EPS) * nw_ref[:, vs] * (gh * _sigmoid(gh))

    @pl.when(ci == pl.num_programs(1) - 1)
    def _():
        sT_ref[...] = s_ref[...]


def _gla_chunk(u, n_seq, alpha_up_pad, alpha_b, norm_w, s0):
    rows = u.shape[0]
    t = rows // n_seq
    nc = t // CHUNK
    return pl.pallas_call(
        _gla_chunk_kernel,
        grid=(n_seq, nc),
        in_specs=[
            pl.BlockSpec((CHUNK, A_PROJ), lambda b, c: (b * nc + c, 1)),
            pl.BlockSpec((LANES, B_KEY_WIDTH), lambda b, c: (0, 0)),
            pl.BlockSpec((1, B_KEY_WIDTH), lambda b, c: (0, 0)),
            pl.BlockSpec((1, B_WIDTH), lambda b, c: (0, 0)),
            pl.BlockSpec((None, B_HEADS, B_KEY_DIM, B_VAL_DIM), lambda b, c: (b, 0, 0, 0)),
        ],
        out_specs=[
            pl.BlockSpec((CHUNK, B_WIDTH), lambda b, c: (b * nc + c, 0)),
            pl.BlockSpec((None, B_HEADS, B_KEY_DIM, B_VAL_DIM), lambda b, c: (b, 0, 0, 0)),
        ],
        out_shape=[jax.ShapeDtypeStruct((rows, B_WIDTH), F32),
                   jax.ShapeDtypeStruct((n_seq, B_HEADS, B_KEY_DIM, B_VAL_DIM), F32)],
        scratch_shapes=[pltpu.VMEM((B_HEADS, B_KEY_DIM, B_VAL_DIM), F32)],
        compiler_params=_params("arbitrary", "arbitrary"),
        name="gla_chunk",
    )(u, alpha_up_pad, alpha_b, norm_w, s0)


def _gla_prep_kernel(u_ref, aup_ref, ab_ref, la_ref):
    a128 = u_ref[:, 2 * B_KEY_WIDTH + 2 * B_WIDTH:2 * B_KEY_WIDTH + 2 * B_WIDTH + LANES]
    la_ref[...] = _gla_log_alpha(a128, aup_ref, ab_ref)


def _gla_prep(u, alpha_up_pad, alpha_b):
    rows = u.shape[0]
    return pl.pallas_call(
        _gla_prep_kernel,
        grid=(1,),
        in_specs=[pl.BlockSpec((rows, A_PROJ), lambda i: (0, 1)),
                  pl.BlockSpec((LANES, B_KEY_WIDTH), lambda i: (0, 0)),
                  pl.BlockSpec((1, B_KEY_WIDTH), lambda i: (0, 0))],
        out_specs=pl.BlockSpec((rows, B_KEY_WIDTH), lambda i: (0, 0)),
        out_shape=jax.ShapeDtypeStruct((rows, B_KEY_WIDTH), F32),
        compiler_params=_params("arbitrary"),
        name="gla_prep",
    )(u, alpha_up_pad, alpha_b)


def _gla_step_kernel(q_ref, k_ref, v_ref, g_ref, la_ref, nw_ref, s_ref, o_ref, so_ref):
    s = s_ref[...]
    nk = s.shape[-2]
    q = q_ref[...] * (B_KEY_DIM ** -0.5)
    k = k_ref[...]
    v = v_ref[...]
    dec = jnp.exp(la_ref[...])
    eye = (lax.broadcasted_iota(jnp.int32, (nk, nk), 0) == lax.broadcasted_iota(jnp.int32, (nk, nk), 1)).astype(F32)
    col = lambda x: jnp.sum(eye * x, axis=-1, keepdims=True)
    qk = jnp.sum(q * k, axis=-1, keepdims=True)
    o = qk * v + jnp.sum(col(q * dec) * s, axis=-2, keepdims=True)
    so_ref[...] = s * col(dec) + col(k) * v
    ms = jnp.mean(o * o, axis=-1, keepdims=True)
    g = g_ref[...]
    o_ref[...] = o * lax.rsqrt(ms + NORM_EPS) * nw_ref[...] * (g * _sigmoid(g))


def _gla_step(q, k, v, g, la, norm_w, s0, nb):
    n = q.shape[0]
    kspec = pl.BlockSpec((nb, B_HEADS, 1, B_KEY_DIM), lambda i: (i, 0, 0, 0))
    vspec = pl.BlockSpec((nb, B_HEADS, 1, B_VAL_DIM), lambda i: (i, 0, 0, 0))
    sspec = pl.BlockSpec((nb, B_HEADS, B_KEY_DIM, B_VAL_DIM), lambda i: (i, 0, 0, 0))
    return pl.pallas_call(
        _gla_step_kernel,
        grid=(n // nb,),
        in_specs=[kspec, kspec, vspec, vspec, kspec,
                  pl.BlockSpec((B_HEADS, 1, B_VAL_DIM), lambda i: (0, 0, 0)), sspec],
        out_specs=[vspec, sspec],
        out_shape=[jax.ShapeDtypeStruct((n, B_HEADS, 1, B_VAL_DIM), F32), jax.ShapeDtypeStruct(s0.shape, F32)],
        compiler_params=_params("arbitrary"),
        name="gla_step",
    )(q, k, v, g, la, norm_w, s0)


def _lru_gates(xc, wa_ref, wx_ref, ba_ref, bx_ref, lam_ref):
    ra, ix = [], []
    for nb in range(C_BLOCKS):
        xb = xc[:, nb * C_BLOCK:(nb + 1) * C_BLOCK].astype(BF16)
        ra.append(jnp.dot(xb, wa_ref[nb].astype(BF16), preferred_element_type=F32))
        ix.append(jnp.dot(xb, wx_ref[nb].astype(BF16), preferred_element_type=F32))
    r = _sigmoid(jnp.concatenate(ra, axis=-1) + ba_ref[...])
    i_g = _sigmoid(jnp.concatenate(ix, axis=-1) + bx_ref[...])
    log_a = -C_POW * r * _softplus(-lam_ref[...])
    a = jnp.exp(log_a)
    b = jnp.sqrt(1.0 - jnp.exp(2.0 * log_a)) * (i_g * xc)
    return a, b


def _lru_seq_kernel(u_ref, cp_ref, h0_ref, cw_ref, cb_ref, wa_ref, wx_ref, ba_ref, bx_ref, lam_ref,
                    y_ref, tail_ref, hT_ref, carry_ref, h_ref, a_s, b_s):
    ti = pl.program_id(1)

    @pl.when(ti == 0)
    def _():
        carry_ref[...] = cp_ref[...]
        h_ref[...] = h0_ref[...]

    gate = u_ref[:, 0:C_WIDTH]
    xb = u_ref[:, C_WIDTH:2 * C_WIDTH]
    tm = xb.shape[0]
    carry = carry_ref[...]
    row8 = lax.broadcasted_iota(jnp.int32, (8, C_WIDTH), 0)
    xc = cb_ref[...] + cw_ref[C_CONV - 1:C_CONV, :] * xb
    for s in range(1, C_CONV):
        rolled = pltpu.roll(xb, s, axis=0)
        head = jnp.where(row8 < s, pltpu.roll(carry, s, axis=0), rolled[0:8, :])
        shifted = jnp.concatenate([head, rolled[8:, :]], axis=0)
        xc = xc + cw_ref[C_CONV - 1 - s:C_CONV - s, :] * shifted
    carry_ref[...] = xb[tm - 8:tm, :]
    tail_ref[...] = xb[tm - 8:tm, :]

    a, b = _lru_gates(xc, wa_ref, wx_ref, ba_ref, bx_ref, lam_ref)
    a_s[...] = a
    b_s[...] = b

    def body(t, h):
        h = a_s[pl.ds(t, 1), :] * h + b_s[pl.ds(t, 1), :]
        b_s[pl.ds(t, 1), :] = h
        return h

    h_last = lax.fori_loop(0, tm, body, h_ref[...], unroll=8)
    h_ref[...] = h_last
    hT_ref[...] = h_last
    y_ref[...] = _gelu_tanh(gate) * b_s[...]


def _lru_seq(u, n_seq, conv_prev8, h0, conv_w, conv_b, wa, wx, ba, bx, lam, tm):
    rows = u.shape[0]
    per_seq = rows // n_seq // tm
    const = lambda shape: pl.BlockSpec(shape, lambda b, t: (0,) * len(shape))
    return pl.pallas_call(
        _lru_seq_kernel,
        grid=(n_seq, per_seq),
        in_specs=[
            pl.BlockSpec((tm, 2 * C_WIDTH), lambda b, t: (b * per_seq + t, 0)),
            pl.BlockSpec((None, 8, C_WIDTH), lambda b, t: (b, 0, 0)),
            pl.BlockSpec((None, 1, C_WIDTH), lambda b, t: (b, 0, 0)),
            const((C_CONV, C_WIDTH)), const((1, C_WIDTH)),
            const((C_BLOCKS, C_BLOCK, C_BLOCK)), const((C_BLOCKS, C_BLOCK, C_BLOCK)),
            const((1, C_WIDTH)), const((1, C_WIDTH)), const((1, C_WIDTH)),
        ],
        out_specs=[
            pl.BlockSpec((tm, C_WIDTH), lambda b, t: (b * per_seq + t, 0)),
            pl.BlockSpec((None, 8, C_WIDTH), lambda b, t: (b, 0, 0)),
            pl.BlockSpec((None, 1, C_WIDTH), lambda b, t: (b, 0, 0)),
        ],
        out_shape=[jax.ShapeDtypeStruct((rows, C_WIDTH), F32),
                   jax.ShapeDtypeStruct((n_seq, 8, C_WIDTH), F32),
                   jax.ShapeDtypeStruct((n_seq, 1, C_WIDTH), F32)],
        scratch_shapes=[pltpu.VMEM((8, C_WIDTH), F32), pltpu.VMEM((1, C_WIDTH), F32),
                        pltpu.VMEM((tm, C_WIDTH), F32), pltpu.VMEM((tm, C_WIDTH), F32)],
        compiler_params=_params("arbitrary", "arbitrary"),
        name="lru_seq",
    )(u, conv_prev8, h0, conv_w, conv_b, wa, wx, ba, bx, lam)


def _lru_step_kernel(u_ref, cp_ref, h0_ref, cw_ref, cb_ref, wa_ref, wx_ref, ba_ref, bx_ref, lam_ref,
                     y_ref, cn_ref, h_ref):
    gate = u_ref[:, 0:C_WIDTH]
    xb = u_ref[:, C_WIDTH:2 * C_WIDTH]
    xc = cb_ref[...] + cw_ref[C_CONV - 1:C_CONV, :] * xb
    for i in range(C_CONV - 1):
        xc = xc + cw_ref[i:i + 1, :] * cp_ref[:, i * C_WIDTH:(i + 1) * C_WIDTH]
    a, b = _lru_gates(xc, wa_ref, wx_ref, ba_ref, bx_ref, lam_ref)
    h = a * h0_ref[...] + b
    h_ref[...] = h
    y_ref[...] = _gelu_tanh(gate) * h
    for i in range(C_CONV - 2):
        cn_ref[:, i * C_WIDTH:(i + 1) * C_WIDTH] = cp_ref[:, (i + 1) * C_WIDTH:(i + 2) * C_WIDTH]
    cn_ref[:, (C_CONV - 2) * C_WIDTH:(C_CONV - 1) * C_WIDTH] = xb


def _lru_step(u, conv_prev, h0, conv_w, conv_b, wa, wx, ba, bx, lam):
    rows = u.shape[0]
    full = lambda shape: pl.BlockSpec(shape, lambda i: (0,) * len(shape))
    cw = (C_CONV - 1) * C_WIDTH
    return pl.pallas_call(
        _lru_step_kernel,
        grid=(1,),
        in_specs=[full((rows, 2 * C_WIDTH)), full((rows, cw)), full((rows, C_WIDTH)),
                  full((C_CONV, C_WIDTH)), full((1, C_WIDTH)),
                  full((C_BLOCKS, C_BLOCK, C_BLOCK)), full((C_BLOCKS, C_BLOCK, C_BLOCK)),
                  full((1, C_WIDTH)), full((1, C_WIDTH)), full((1, C_WIDTH))],
        out_specs=[full((rows, C_WIDTH)), full((rows, cw)), full((rows, C_WIDTH))],
        out_shape=[jax.ShapeDtypeStruct((rows, C_WIDTH), F32),
                   jax.ShapeDtypeStruct((rows, cw), F32),
                   jax.ShapeDtypeStruct((rows, C_WIDTH), F32)],
        compiler_params=_params("arbitrary"),
        name="lru_step",
    )(u, conv_prev, h0, conv_w, conv_b, wa, wx, ba, bx, lam)


def _pad_rows(w, first, total):
    return jnp.zeros((total, w.shape[1]), w.dtype).at[first:first + w.shape[0]].set(w)


def _trunk(x, mod, n_seq, seq_len, st_shift, st_wkv, st_gla, st_conv, st_lru, wts, tm, tf):
    (w_ffn_up, w_ffn_down, w_in_even, w_out_even, a_mu, a_w_up, a_a_up, a_g_up, a_vec, b_alpha_up,
     b_alpha_b, b_norm_w, w_in_odd, w_out_odd, c_conv_w, c_conv_b, c_wa, c_ba, c_wx, c_bx, c_lam,
     final_norm_w) = wts
    rows = x.shape[0]
    prompt = seq_len > 1
    tn = 512
    seg = jnp.kron(jnp.eye(A_HEADS, dtype=F32), jnp.ones((A_HEAD_DIM, A_HEAD_DIM), F32)).astype(BF16)
    out_shift, out_wkv, out_gla, out_conv, out_lru = [], [], [], [], []
    for l in range(DEPTH):
        j = l // 2
        x = _ffn(x, mod, l, 0, w_ffn_up, w_ffn_down, tm, tf)
        if l % 2 == 0:
            u = _inproj(x, mod, l, w_in_even, j, tm, tn)
            wup_pad = _pad_rows(a_w_up[j], 0, LANES)
            aup_pad = _pad_rows(a_a_up[j], A_DECAY_RANK, LANES)
            vec8 = _pad_rows(a_vec[j], 0, 8)
            alpha_pad = _pad_rows(b_alpha_up[j], 0, LANES)
            alpha_b = b_alpha_b[j].reshape(1, B_KEY_WIDTH)
            mu = a_mu[j].reshape(1, A_PROJ)
            if prompt:
                pre = _rwkv_prep(u, st_shift[j].reshape(n_seq, 1, A_PROJ), mu, wup_pad, aup_pad, a_g_up[j],
                                 vec8, seg, n_seq, 256, True)
                r, lw, k2, v, kk, al, g, bonus = pre
                hm = lambda t: t.reshape(n_seq, seq_len, A_HEADS, A_HEAD_DIM).transpose(0, 2, 1, 3)
                h0 = jnp.swapaxes(st_wkv[j], -1, -2)
                y_hm, h_t = _rwkv_chunk(hm(r), hm(lw), hm(k2), hm(v), hm(kk), hm(al), h0)
                y_raw = y_hm.transpose(0, 2, 1, 3).reshape(rows, A_WIDTH)
                s_wkv = jnp.swapaxes(h_t, -1, -2)
                s_shift = u.reshape(n_seq, seq_len, EVEN_PROJ)[:, -1, :A_PROJ]
                y_b, s_gla = _gla_chunk(u, n_seq, alpha_pad, alpha_b, b_norm_w[j].reshape(1, B_WIDTH), st_gla[j])
            else:
                pre = _rwkv_prep(u, st_shift[j], mu, wup_pad, aup_pad, a_g_up[j], vec8, seg, rows, rows, False)
                r, lw, k2, v, kk, al, g, bonus = pre
                rw = lambda t: t.reshape(rows, A_HEADS, 1, A_HEAD_DIM)
                y4, s_wkv = _rwkv_step(rw(r), rw(lw), rw(k2), rw(v), rw(kk), rw(al), st_wkv[j], 8)
                y_raw = y4.reshape(rows, A_WIDTH)
                s_shift = u[:, :A_PROJ]
                la = _gla_prep(u, alpha_pad, alpha_b)
                ub = u[:, A_PROJ:]
                kr = lambda t: t.reshape(rows, B_HEADS, 1, B_KEY_DIM)
                vr = lambda t: t.reshape(rows, B_HEADS, 1, B_VAL_DIM)
                o4, s_gla = _gla_step(kr(ub[:, :B_KEY_WIDTH]), kr(ub[:, B_KEY_WIDTH:2 * B_KEY_WIDTH]),
                                      vr(ub[:, 2 * B_KEY_WIDTH:2 * B_KEY_WIDTH + B_WIDTH]),
                                      vr(ub[:, 2 * B_KEY_WIDTH + B_WIDTH:2 * B_KEY_WIDTH + 2 * B_WIDTH]),
                                      kr(la), b_norm_w[j].reshape(B_HEADS, 1, B_VAL_DIM), st_gla[j], 8)
                y_b = o4.reshape(rows, B_WIDTH)
            y_a = _rwkv_post(y_raw, bonus, g, vec8, seg, min(rows, 512))
            x = _outproj([y_a, y_b], w_out_even, j, x, mod, l, tm, tn)
            out_shift.append(s_shift)
            out_wkv.append(s_wkv)
            out_gla.append(s_gla)
        else:
            u = _inproj(x, mod, l, w_in_odd, j, tm, tn)
            lru_w = (c_conv_w[j], c_conv_b[j].reshape(1, C_WIDTH), c_wa[j], c_wx[j],
                     c_ba[j].reshape(1, C_WIDTH), c_bx[j].reshape(1, C_WIDTH), c_lam[j].reshape(1, C_WIDTH))
            if prompt:
                cp8 = jnp.concatenate([jnp.zeros((n_seq, 8 - (C_CONV - 1), C_WIDTH), F32), st_conv[j]], axis=1)
                y_c, tail, h_t = _lru_seq(u, n_seq, cp8, st_lru[j].reshape(n_seq, 1, C_WIDTH), *lru_w, 256)
                s_conv = tail[:, 8 - (C_CONV - 1):, :]
                s_lru = h_t.reshape(n_seq, C_WIDTH)
            else:
                y_c, cn, s_lru = _lru_step(u, st_conv[j].reshape(rows, (C_CONV - 1) * C_WIDTH), st_lru[j], *lru_w)
                s_conv = cn.reshape(rows, C_CONV - 1, C_WIDTH)
            x = _outproj([y_c], w_out_odd, j, x, mod, l, tm, tn)
            out_conv.append(s_conv)
            out_lru.append(s_lru)
        x = _ffn(x, mod, l, 1, w_ffn_up, w_ffn_down, tm, tf, final_w=final_norm_w if l == DEPTH - 1 else None)
    return x, (jnp.stack(out_shift), jnp.stack(out_wkv), jnp.stack(out_gla), jnp.stack(out_conv),
               jnp.stack(out_lru))


def kernel(x_prompt, x_sample, c_prompt, c_sample, state_rwkv_shift, state_rwkv_wkv, state_gla, state_conv, state_lru, w_ada, b_ada, w_ffn_up, w_ffn_down, w_in_even, w_out_even, a_mu, a_w_up, a_a_up, a_g_up, a_vec, b_alpha_up, b_alpha_b, b_norm_w, w_in_odd, w_out_odd, c_conv_w, c_conv_b, c_wa, c_ba, c_wx, c_bx, c_lam, final_norm_w):
    wts = (w_ffn_up, w_ffn_down, w_in_even, w_out_even, a_mu, a_w_up, a_a_up, a_g_up, a_vec, b_alpha_up,
           b_alpha_b, b_norm_w, w_in_odd, w_out_odd, c_conv_w, c_conv_b, c_wa, c_ba, c_wx, c_bx, c_lam,
           final_norm_w)
    n_p, t_p, _ = x_prompt.shape
    n_s, t_s, _ = x_sample.shape
    assert t_s == 1

    c_rows = n_p + n_s
    pad = (-c_rows) % 16
    c_all = jnp.concatenate([c_prompt, c_sample, jnp.zeros((pad, D_MODEL), F32)], axis=0)
    mod_all = _ada(c_all, w_ada, b_ada)
    mod_p = _Mod(mod_all[:, :, :n_p].reshape(DEPTH, N_MOD, n_p, 1, D_MODEL), False, t_p)
    mod_s = _Mod(mod_all[:, :, n_p:c_rows], True, 1)

    fresh = lambda s: jnp.zeros((s.shape[0], n_p) + s.shape[2:], s.dtype)
    y_p, st_p = _trunk(x_prompt.reshape(n_p * t_p, D_MODEL), mod_p, n_p, t_p,
                       fresh(state_rwkv_shift), fresh(state_rwkv_wkv), fresh(state_gla),
                       fresh(state_conv), fresh(state_lru), wts, 1024, 256)
    y_s, st_s = _trunk(x_sample.reshape(n_s, D_MODEL), mod_s, n_s, 1,
                       state_rwkv_shift, state_rwkv_wkv, state_gla, state_conv, state_lru, wts, n_s, 512)
    return (y_p.reshape(n_p, t_p, D_MODEL), y_s.reshape(n_s, 1, D_MODEL), *st_p, *st_s)
```

```python
import functools

import jax
import jax.numpy as jnp
from jax import lax
from jax.experimental import pallas as pl
from jax.experimental.pallas import tpu as pltpu

F32 = jnp.float32
BF16 = jnp.bfloat16

D_MODEL = 2048
DEPTH = 4
N_MOD = 9
D_FF = 5632
NORM_EPS = 1e-6

A_HEADS = 16
A_HEAD_DIM = 64
A_WIDTH = A_HEADS * A_HEAD_DIM
A_DECAY_RANK = 64
A_ICL_RANK = 64
A_GATE_RANK = 128
A_PROJ = 3 * A_WIDTH + A_DECAY_RANK + A_ICL_RANK + A_GATE_RANK
A_GN_EPS = 64e-5

B_HEADS = 4
B_KEY_DIM = 128
B_VAL_DIM = 256
B_KEY_WIDTH = B_HEADS * B_KEY_DIM
B_WIDTH = B_HEADS * B_VAL_DIM
B_ALPHA_RANK = 16
B_TAU = 16.0
B_PROJ = 2 * B_KEY_WIDTH + 2 * B_WIDTH + B_ALPHA_RANK
EVEN_PROJ = A_PROJ + B_PROJ

C_WIDTH = D_MODEL
C_BLOCKS = 8
C_BLOCK = C_WIDTH // C_BLOCKS
C_CONV = 4
C_POW = 8.0

CHUNK = 64
LANES = 128
VMEM_LIMIT = 56 * 1024 * 1024


def _params(*sem):
    return pltpu.CompilerParams(dimension_semantics=sem, vmem_limit_bytes=VMEM_LIMIT)


def _bdot(a, b):
    return jnp.dot(a.astype(BF16), b.astype(BF16), preferred_element_type=F32)


def _split3(x):
    hi = x.astype(BF16)
    r1 = x - hi.astype(F32)
    mid = r1.astype(BF16)
    lo = (r1 - mid.astype(F32)).astype(BF16)
    return hi, mid, lo


def _dot_exact_rhs(a_bf16, x):
    hi, mid, lo = _split3(x)
    f = lambda y: jnp.dot(a_bf16, y, preferred_element_type=F32)
    return f(hi) + f(mid) + f(lo)


def _dot_exact_lhs(x, b_bf16):
    hi, mid, lo = _split3(x)
    f = lambda y: jnp.dot(y, b_bf16, preferred_element_type=F32)
    return f(hi) + f(mid) + f(lo)


def _sigmoid(x):
    return jax.nn.sigmoid(x)


def _softplus(x):
    return jnp.maximum(x, 0.0) + jnp.log1p(jnp.exp(-jnp.abs(x)))


def _gelu_tanh(x):
    return 0.5 * x * (1.0 + jnp.tanh(0.7978845608028654 * (x + 0.044715 * (x * x * x))))


def _norm_mod(x, shift, scale):
    ms = jnp.mean(x * x, axis=-1, keepdims=True)
    return x * lax.rsqrt(ms + NORM_EPS) * (1.0 + scale) + shift


def _ada_kernel(c_ref, w_ref, b_ref, o_ref):
    c = c_ref[...]
    act = (c * _sigmoid(c)).astype(BF16)
    o_ref[...] = jnp.dot(act, w_ref[...].astype(BF16), preferred_element_type=F32) + b_ref[...]


def _ada(c_all, w_ada, b_ada):
    rows = c_all.shape[0]
    tn = 1024
    per = D_MODEL // tn
    b4 = b_ada.reshape(DEPTH, N_MOD * per, 1, tn)
    return pl.pallas_call(
        _ada_kernel,
        grid=(DEPTH, N_MOD * per),
        in_specs=[
            pl.BlockSpec((rows, D_MODEL), lambda l, j: (0, 0)),
            pl.BlockSpec((None, D_MODEL, tn), lambda l, j: (l, 0, j)),
            pl.BlockSpec((None, None, 1, tn), lambda l, j: (l, j, 0, 0)),
        ],
        out_specs=pl.BlockSpec((None, None, rows, tn), lambda l, j: (l, j // per, 0, j % per)),
        out_shape=jax.ShapeDtypeStruct((DEPTH, N_MOD, rows, D_MODEL), F32),
        compiler_params=_params("arbitrary", "arbitrary"),
        name="ada",
    )(c_all, w_ada, b4)


class _Mod:
    def __init__(self, arr, per_row, seq_len):
        self.arr = arr
        self.per_row = per_row
        self.seq_len = seq_len

    def spec(self, layer, m, tm):
        if self.per_row:
            return pl.BlockSpec((None, None, tm, D_MODEL), lambda i, *_: (layer, m, i, 0))
        per_seq = self.seq_len // tm
        return pl.BlockSpec((None, None, None, 1, D_MODEL), lambda i, *_: (layer, m, i // per_seq, 0, 0))


def _ffn_kernel(x_ref, sh_ref, sc_ref, g_ref, wg_ref, wu_ref, wd_ref, *rest, n_j, sub, final):
    if final:
        fw_ref, o_ref, h_ref = rest
    else:
        o_ref, h_ref = rest
    j = pl.program_id(1)
    tm = x_ref.shape[0]

    @pl.when(j == 0)
    def _():
        def body(s, carry):
            rows = pl.ds(pl.multiple_of(s * sub, sub), sub)
            sh = sh_ref[...] if sh_ref.shape[0] == 1 else sh_ref[rows, :]
            sc = sc_ref[...] if sc_ref.shape[0] == 1 else sc_ref[rows, :]
            h_ref[rows, :] = _norm_mod(x_ref[rows, :], sh, sc).astype(BF16)
            o_ref[rows, :] = jnp.zeros((sub, D_MODEL), F32)
            return carry
        lax.fori_loop(0, tm // sub, body, 0)

    h = h_ref[...]
    gt = jnp.dot(h, wg_ref[...], preferred_element_type=F32)
    ut = jnp.dot(h, wu_ref[...], preferred_element_type=F32)
    act = (gt * _sigmoid(gt) * ut).astype(BF16)
    o_ref[...] += jnp.dot(act, wd_ref[...], preferred_element_type=F32)

    @pl.when(j == n_j - 1)
    def _():
        def body(s, carry):
            rows = pl.ds(pl.multiple_of(s * sub, sub), sub)
            g = g_ref[...] if g_ref.shape[0] == 1 else g_ref[rows, :]
            y = x_ref[rows, :] + 0.5 * g * o_ref[rows, :]
            if final:
                ms = jnp.mean(y * y, axis=-1, keepdims=True)
                y = y * lax.rsqrt(ms + NORM_EPS) * fw_ref[...]
            o_ref[rows, :] = y
            return carry
        lax.fori_loop(0, tm // sub, body, 0)


def _ffn(x, mod, layer, which, w_up, w_down, tm, tf, final_w=None):
    m_rows = x.shape[0]
    n_j = D_FF // tf
    m0 = 6 * which
    sub = min(tm, 128)
    final = final_w is not None
    in_specs = [
        pl.BlockSpec((tm, D_MODEL), lambda i, j: (i, 0), pipeline_mode=pl.Buffered(1)),
        mod.spec(layer, m0, tm), mod.spec(layer, m0 + 1, tm), mod.spec(layer, m0 + 2, tm),
        pl.BlockSpec((None, None, D_MODEL, tf), lambda i, j: (layer, which, 0, j)),
        pl.BlockSpec((None, None, D_MODEL, tf), lambda i, j: (layer, which, 0, j + n_j)),
        pl.BlockSpec((None, None, tf, D_MODEL), lambda i, j: (layer, which, j, 0)),
    ]
    args = [x, mod.arr, mod.arr, mod.arr, w_up, w_up, w_down]
    if final:
        in_specs.append(pl.BlockSpec((1, D_MODEL), lambda i, j: (0, 0)))
        args.append(final_w.reshape(1, D_MODEL))
    return pl.pallas_call(
        functools.partial(_ffn_kernel, n_j=n_j, sub=sub, final=final),
        grid=(m_rows // tm, n_j),
        in_specs=in_specs,
        out_specs=pl.BlockSpec((tm, D_MODEL), lambda i, j: (i, 0)),
        out_shape=jax.ShapeDtypeStruct((m_rows, D_MODEL), F32),
        scratch_shapes=[pltpu.VMEM((tm, D_MODEL), BF16)],
        compiler_params=_params("arbitrary", "arbitrary"),
        name="ffn",
    )(*args)


def _inproj_kernel(x_ref, sh_ref, sc_ref, w_ref, o_ref, h_ref, *, sub):
    j = pl.program_id(1)
    tm = x_ref.shape[0]

    @pl.when(j == 0)
    def _():
        def body(s, carry):
            rows = pl.ds(pl.multiple_of(s * sub, sub), sub)
            sh = sh_ref[...] if sh_ref.shape[0] == 1 else sh_ref[rows, :]
            sc = sc_ref[...] if sc_ref.shape[0] == 1 else sc_ref[rows, :]
            h_ref[rows, :] = _norm_mod(x_ref[rows, :], sh, sc).astype(BF16)
            return carry
        lax.fori_loop(0, tm // sub, body, 0)

    o_ref[...] = jnp.dot(h_ref[...], w_ref[...], preferred_element_type=F32)


def _inproj(x, mod, layer, w, widx, tm, tn):
    m_rows = x.shape[0]
    n_out = w.shape[-1]
    sub = min(tm, 128)
    return pl.pallas_call(
        functools.partial(_inproj_kernel, sub=sub),
        grid=(m_rows // tm, pl.cdiv(n_out, tn)),
        in_specs=[
            pl.BlockSpec((tm, D_MODEL), lambda i, j: (i, 0)),
            mod.spec(layer, 3, tm), mod.spec(layer, 4, tm),
            pl.BlockSpec((None, D_MODEL, tn), lambda i, j: (widx, 0, j)),
        ],
        out_specs=pl.BlockSpec((tm, tn), lambda i, j: (i, j)),
        out_shape=jax.ShapeDtypeStruct((m_rows, n_out), F32),
        scratch_shapes=[pltpu.VMEM((tm, D_MODEL), BF16)],
        compiler_params=_params("arbitrary", "arbitrary"),
        name="inproj",
    )(x, mod.arr, mod.arr, w)


def _outproj_kernel(*refs, n_in):
    y_refs = refs[:n_in]
    w_refs = refs[n_in:2 * n_in]
    x_ref, g_ref, o_ref = refs[2 * n_in:]
    acc = _bdot(y_refs[0][...], w_refs[0][...])
    for y_ref, w_ref in zip(y_refs[1:], w_refs[1:]):
        acc = acc + _bdot(y_ref[...], w_ref[...])
    o_ref[...] = x_ref[...] + g_ref[...] * acc


def _outproj(ys, w, widx, x, mod, layer, tm):
    m_rows = x.shape[0]
    n_in = len(ys)
    kw = ys[0].shape[1]
    in_specs = [pl.BlockSpec((tm, kw), lambda i: (i, 0)) for _ in ys]
    in_specs += [pl.BlockSpec((None, kw, D_MODEL), lambda i, k=k: (widx, k, 0)) for k in range(n_in)]
    in_specs += [pl.BlockSpec((tm, D_MODEL), lambda i: (i, 0)), mod.spec(layer, 5, tm)]
    return pl.pallas_call(
        functools.partial(_outproj_kernel, n_in=n_in),
        grid=(m_rows // tm,),
        in_specs=in_specs,
        out_specs=pl.BlockSpec((tm, D_MODEL), lambda i: (i, 0)),
        out_shape=jax.ShapeDtypeStruct((m_rows, D_MODEL), F32),
        compiler_params=_params("arbitrary"),
        name="outproj",
    )(*ys, *([w] * n_in), x, mod.arr)


def _rwkv_prep_kernel(u_ref, prev_ref, mu_ref, wup_ref, aup_ref, gup_ref, vec_ref, seg_ref,
                      r_ref, lw_ref, k_ref, v_ref, kk_ref, a_ref, g_ref, bonus_ref, carry_ref, *, seq_mode):
    u = u_ref[...]
    tm = u.shape[0]
    if seq_mode:
        @pl.when(pl.program_id(1) == 0)
        def _():
            carry_ref[...] = prev_ref[...]
        rolled = pltpu.roll(u, 1, axis=0)
        row = lax.broadcasted_iota(jnp.int32, u.shape, 0)
        prev = jnp.where(row == 0, carry_ref[...], rolled)
        carry_ref[...] = u[tm - 1:tm, :]
    else:
        prev = prev_ref[...]
    xs = u + mu_ref[...] * (prev - u)
    r = xs[:, 0:A_WIDTH]
    k = xs[:, A_WIDTH:2 * A_WIDTH]
    v = xs[:, 2 * A_WIDTH:3 * A_WIDTH]
    wa_lo = xs[:, 3 * A_WIDTH:3 * A_WIDTH + LANES]
    g_lo = xs[:, 3 * A_WIDTH + LANES:]
    w0, a0, k_k, k_a, r_k = (vec_ref[i:i + 1, :] for i in range(5))
    w_raw = -_softplus(-(w0 + _bdot(jnp.tanh(wa_lo), wup_ref[...]))) - 0.5
    lw = -jnp.exp(w_raw)
    a = _sigmoid(a0 + _bdot(wa_lo, aup_ref[...]))
    g = _bdot(_sigmoid(g_lo), gup_ref[...])
    kk = k * k_k
    seg = seg_ref[...]
    kk = kk * lax.rsqrt(jnp.maximum(_dot_exact_lhs(kk * kk, seg), 1e-24))
    k2 = k * (1.0 + (a - 1.0) * k_a)
    bonus = _dot_exact_lhs(r * k2 * r_k, seg) * v
    r_ref[...] = r
    lw_ref[...] = lw
    k_ref[...] = k2
    v_ref[...] = v
    kk_ref[...] = kk
    a_ref[...] = a
    g_ref[...] = g
    bonus_ref[...] = bonus


def _rwkv_prep(u, prev, mu, wup_pad, aup_pad, g_up, vec, seg, n_seq, tm, seq_mode):
    rows = u.shape[0]
    per_seq = rows // n_seq // tm if seq_mode else 1
    grid = (n_seq, per_seq) if seq_mode else (rows // tm, 1)
    rowmap = (lambda b, t: (b * per_seq + t, 0)) if seq_mode else (lambda b, t: (b, 0))
    prev_spec = (pl.BlockSpec((None, 1, A_PROJ), lambda b, t: (b, 0, 0)) if seq_mode
                 else pl.BlockSpec((tm, A_PROJ), rowmap))
    const = lambda shape: pl.BlockSpec(shape, lambda b, t: (0,) * len(shape))
    out = jax.ShapeDtypeStruct((rows, A_WIDTH), F32)
    return pl.pallas_call(
        functools.partial(_rwkv_prep_kernel, seq_mode=seq_mode),
        grid=grid,
        in_specs=[
            pl.BlockSpec((tm, A_PROJ), rowmap), prev_spec,
            const((1, A_PROJ)), const((LANES, A_WIDTH)), const((LANES, A_WIDTH)),
            const((A_GATE_RANK, A_WIDTH)), const((8, A_WIDTH)), const((A_WIDTH, A_WIDTH)),
        ],
        out_specs=[pl.BlockSpec((tm, A_WIDTH), rowmap)] * 8,
        out_shape=[out] * 8,
        scratch_shapes=[pltpu.VMEM((1, A_PROJ), F32)],
        compiler_params=_params("arbitrary", "arbitrary"),
        name="rwkv_prep",
    )(u, prev, mu, wup_pad, aup_pad, g_up, vec, seg)


def _rwkv_post_kernel(y_ref, bonus_ref, g_ref, vec_ref, seg_ref, o_ref):
    y = y_ref[...]
    seg = seg_ref[...]
    inv_n = 1.0 / A_HEAD_DIM
    yc = y - _dot_exact_lhs(y, seg) * inv_n
    var = _dot_exact_lhs(yc * yc, seg) * inv_n
    yn = yc * lax.rsqrt(var + A_GN_EPS) * vec_ref[5:6, :] + vec_ref[6:7, :]
    o_ref[...] = (yn + bonus_ref[...]) * g_ref[...]


def _rwkv_post(y, bonus, g, vec, seg, tm):
    rows = y.shape[0]
    spec = pl.BlockSpec((tm, A_WIDTH), lambda i: (i, 0))
    return pl.pallas_call(
        _rwkv_post_kernel,
        grid=(rows // tm,),
        in_specs=[spec, spec, spec,
                  pl.BlockSpec((8, A_WIDTH), lambda i: (0, 0)),
                  pl.BlockSpec((A_WIDTH, A_WIDTH), lambda i: (0, 0))],
        out_specs=spec,
        out_shape=jax.ShapeDtypeStruct((rows, A_WIDTH), F32),
        compiler_params=_params("arbitrary"),
        name="rwkv_post",
    )(y, bonus, g, vec, seg)


def _rwkv_chunk_kernel(r_ref, lw_ref, k_ref, v_ref, kk_ref, a_ref, h0_ref, y_ref, hT_ref, h_ref):
    ci = pl.program_id(1)

    @pl.when(ci == 0)
    def _():
        h_ref[...] = h0_ref[...]

    r = r_ref[...]
    lw = lw_ref[...]
    k = k_ref[...]
    v = v_ref[...]
    kk = kk_ref[...]
    al = a_ref[...]
    nh, cs, nd = r.shape

    row = lax.broadcasted_iota(jnp.int32, (cs, cs), 0)
    col = lax.broadcasted_iota(jnp.int32, (cs, cs), 1)
    incl = (col <= row).astype(F32)
    strict = (col < row).astype(F32)
    eye_c = (col == row).astype(F32)
    nrow = lax.broadcasted_iota(jnp.int32, (nd, nd), 0)
    ncol = lax.broadcasted_iota(jnp.int32, (nd, nd), 1)
    eye_n = (nrow == ncol).astype(F32)

    def bmm(x, y):
        return jnp.einsum('hab,hbc->hac', x.astype(BF16), y.astype(BF16), preferred_element_type=F32)

    def bmm_nt(x, y):
        return jnp.einsum('han,hbn->hab', x.astype(BF16), y.astype(BF16), preferred_element_type=F32)

    def bmm_tn(x, y):
        return jnp.einsum('hca,hcb->hab', x.astype(BF16), y.astype(BF16), preferred_element_type=F32)

    tri = jnp.broadcast_to(incl.astype(BF16)[None], (nh, cs, cs))
    hi, mid, lo = _split3(lw)
    csum = lambda p: jnp.einsum('hab,hbc->hac', tri, p, preferred_element_type=F32)
    c = csum(hi) + csum(mid) + csum(lo)
    c_end = c[:, cs - 1:cs, :]
    e_neg = jnp.exp(-c)
    b_vec = kk * al
    rt = r * jnp.exp(c)
    at = -kk * jnp.exp(c - lw)
    bt = b_vec * e_neg
    kt = k * e_neg
    e_end = jnp.exp(c_end - c)
    bh = b_vec * e_end
    kh = k * e_end
    g_end = jnp.exp(c_end)

    a_ab = bmm_nt(at, bt) * strict
    a_ak = bmm_nt(at, kt) * strict
    m_b = bmm_nt(rt, bt) * incl
    m_k = bmm_nt(rt, kt) * incl

    tinv = eye_c + a_ab
    apow = a_ab
    span = 1
    while 2 * span < cs:
        apow = bmm(apow, apow)
        tinv = tinv + bmm(tinv, apow)
        span *= 2

    akv = bmm(a_ak, v)
    p = bmm(tinv, at)
    q = bmm(tinv, akv)
    p2 = rt + bmm(m_b, p)
    y0 = bmm(m_b, q) + bmm(m_k, v)
    gm = eye_n * g_end + bmm_tn(bh, p)
    hadd = bmm_tn(bh, q) + bmm_tn(kh, v)

    h = h_ref[...]
    y_ref[...] = bmm(p2, h) + y0
    h_new = bmm(gm, h) + hadd
    h_ref[...] = h_new

    @pl.when(ci == pl.num_programs(1) - 1)
    def _():
        hT_ref[...] = h_new


def _rwkv_chunk(r, lw, k, v, kk, a, h0):
    n, nh, t, nd = r.shape
    spec = pl.BlockSpec((None, nh, CHUNK, nd), lambda b, c: (b, 0, c, 0))
    hspec = pl.BlockSpec((None, nh, nd, nd), lambda b, c: (b, 0, 0, 0))
    return pl.pallas_call(
        _rwkv_chunk_kernel,
        grid=(n, t // CHUNK),
        in_specs=[spec] * 6 + [hspec],
        out_specs=[spec, hspec],
        out_shape=[jax.ShapeDtypeStruct((n, nh, t, nd), F32), jax.ShapeDtypeStruct((n, nh, nd, nd), F32)],
        scratch_shapes=[pltpu.VMEM((nh, nd, nd), F32)],
        compiler_params=_params("arbitrary", "arbitrary"),
        name="rwkv_chunk",
    )(r, lw, k, v, kk, a, h0)


def _rwkv_step_kernel(r_ref, lw_ref, k_ref, v_ref, kk_ref, a_ref, s_ref, y_ref, so_ref):
    s = s_ref[...]
    nd = s.shape[-1]
    r = r_ref[...]
    w = jnp.exp(lw_ref[...])
    k = k_ref[...]
    v = v_ref[...]
    kk = kk_ref[...]
    al = a_ref[...]
    eye = (lax.broadcasted_iota(jnp.int32, (nd, nd), 0) == lax.broadcasted_iota(jnp.int32, (nd, nd), 1)).astype(F32)
    s_kk = jnp.sum(s * (-kk), axis=-1, keepdims=True)
    v_col = jnp.sum(eye * v, axis=-1, keepdims=True)
    s_new = s * w + s_kk * (kk * al) + v_col * k
    y_col = jnp.sum(s_new * r, axis=-1, keepdims=True)
    y_ref[...] = jnp.sum(eye * y_col, axis=-2, keepdims=True)
    so_ref[...] = s_new


def _rwkv_step(r, lw, k, v, kk, a, s0, nb):
    n, nh, _, nd = r.shape
    rspec = pl.BlockSpec((nb, nh, 1, nd), lambda i: (i, 0, 0, 0))
    sspec = pl.BlockSpec((nb, nh, nd, nd), lambda i: (i, 0, 0, 0))
    return pl.pallas_call(
        _rwkv_step_kernel,
        grid=(n // nb,),
        in_specs=[rspec] * 6 + [sspec],
        out_specs=[rspec, sspec],
        out_shape=[jax.ShapeDtypeStruct((n, nh, 1, nd), F32), jax.ShapeDtypeStruct(s0.shape, F32)],
        compiler_params=_params("arbitrary"),
        name="rwkv_step",
    )(r, lw, k, v, kk, a, s0)


def _gla_log_alpha(a128, aup_ref, ab_ref):
    lane = lax.broadcasted_iota(jnp.int32, a128.shape, 1)
    a_lo = jnp.where(lane < B_ALPHA_RANK, a128, 0.0)
    x = _bdot(a_lo, aup_ref[...]) + ab_ref[...]
    return -_softplus(-x) * (1.0 / B_TAU)


def _gla_chunk_kernel(u_ref, aup_ref, ab_ref, nw_ref, s0_ref, o_ref, sT_ref, s_ref):
    ci = pl.program_id(1)

    @pl.when(ci == 0)
    def _():
        s_ref[...] = s0_ref[...]

    ub = u_ref[...]
    cs = ub.shape[0]
    q_all = ub[:, 0:B_KEY_WIDTH] * (B_KEY_DIM ** -0.5)
    k_all = ub[:, B_KEY_WIDTH:2 * B_KEY_WIDTH]
    v_all = ub[:, 2 * B_KEY_WIDTH:2 * B_KEY_WIDTH + B_WIDTH]
    g_all = ub[:, 2 * B_KEY_WIDTH + B_WIDTH:2 * B_KEY_WIDTH + 2 * B_WIDTH]
    a128 = ub[:, 2 * B_KEY_WIDTH + 2 * B_WIDTH:2 * B_KEY_WIDTH + 2 * B_WIDTH + LANES]
    log_a = _gla_log_alpha(a128, aup_ref, ab_ref)

    row = lax.broadcasted_iota(jnp.int32, (cs, cs), 0)
    col = lax.broadcasted_iota(jnp.int32, (cs, cs), 1)
    incl = col <= row
    cum = _dot_exact_rhs(incl.astype(BF16), log_a)
    e_pos = jnp.exp(cum)
    e_neg = jnp.exp(-cum)
    cum_end = cum[cs - 1:cs, :]
    e_end = jnp.exp(cum_end - cum)
    g_end = jnp.exp(cum_end)
    nk = B_KEY_DIM
    eye_k = (lax.broadcasted_iota(jnp.int32, (nk, nk), 0) == lax.broadcasted_iota(jnp.int32, (nk, nk), 1)).astype(F32)

    for h in range(B_HEADS):
        ks = slice(h * B_KEY_DIM, (h + 1) * B_KEY_DIM)
        vs = slice(h * B_VAL_DIM, (h + 1) * B_VAL_DIM)
        qe = q_all[:, ks] * e_pos[:, ks]
        kn = k_all[:, ks] * e_neg[:, ks]
        ke = k_all[:, ks] * e_end[:, ks]
        vh = v_all[:, vs]
        s = s_ref[h]
        scores = jnp.einsum('id,jd->ij', qe.astype(BF16), kn.astype(BF16), preferred_element_type=F32)
        scores = jnp.where(incl, scores, 0.0)
        o = _bdot(scores, vh) + _bdot(qe, s)
        g_col = jnp.sum(eye_k * g_end[:, ks], axis=-1, keepdims=True)
        s_ref[h] = s * g_col + jnp.einsum('jd,jv->dv', ke.astype(BF16), vh.astype(BF16),
                                           preferred_element_type=F32)
        ms = jnp.mean(o * o, axis=-1, keepdims=True)
        gh = g_all[:, vs]
        o_ref[:, vs] = o * lax.rsqrt(ms + NORM_EPS) * nw_ref[:, vs] * (gh * _sigmoid(gh))

    @pl.when(ci == pl.num_programs(1) - 1)
    def _():
        sT_ref[...] = s_ref[...]


def _gla_chunk(u, n_seq, alpha_up_pad, alpha_b, norm_w, s0):
    rows = u.shape[0]
    t = rows // n_seq
    nc = t // CHUNK
    return pl.pallas_call(
        _gla_chunk_kernel,
        grid=(n_seq, nc),
        in_specs=[
            pl.BlockSpec((CHUNK, A_PROJ), lambda b, c: (b * nc + c, 1)),
            pl.BlockSpec((LANES, B_KEY_WIDTH), lambda b, c: (0, 0)),
            pl.BlockSpec((1, B_KEY_WIDTH), lambda b, c: (0, 0)),
            pl.BlockSpec((1, B_WIDTH), lambda b, c: (0, 0)),
            pl.BlockSpec((None, B_HEADS, B_KEY_DIM, B_VAL_DIM), lambda b, c: (b, 0, 0, 0)),
        ],
        out_specs=[
            pl.BlockSpec((CHUNK, B_WIDTH), lambda b, c: (b * nc + c, 0)),
            pl.BlockSpec((None, B_HEADS, B_KEY_DIM, B_VAL_DIM), lambda b, c: (b, 0, 0, 0)),
        ],
        out_shape=[jax.ShapeDtypeStruct((rows, B_WIDTH), F32),
                   jax.ShapeDtypeStruct((n_seq, B_HEADS, B_KEY_DIM, B_VAL_DIM), F32)],
        scratch_shapes=[pltpu.VMEM((B_HEADS, B_KEY_DIM, B_VAL_DIM), F32)],
        compiler_params=_params("arbitrary", "arbitrary"),
        name="gla_chunk",
    )(u, alpha_up_pad, alpha_b, norm_w, s0)


def _gla_prep_kernel(u_ref, aup_ref, ab_ref, la_ref):
    a128 = u_ref[:, 2 * B_KEY_WIDTH + 2 * B_WIDTH:2 * B_KEY_WIDTH + 2 * B_WIDTH + LANES]
    la_ref[...] = _gla_log_alpha(a128, aup_ref, ab_ref)


def _gla_prep(u, alpha_up_pad, alpha_b):
    rows = u.shape[0]
    return pl.pallas_call(
        _gla_prep_kernel,
        grid=(1,),
        in_specs=[pl.BlockSpec((rows, A_PROJ), lambda i: (0, 1)),
                  pl.BlockSpec((LANES, B_KEY_WIDTH), lambda i: (0, 0)),
                  pl.BlockSpec((1, B_KEY_WIDTH), lambda i: (0, 0))],
        out_specs=pl.BlockSpec((rows, B_KEY_WIDTH), lambda i: (0, 0)),
        out_shape=jax.ShapeDtypeStruct((rows, B_KEY_WIDTH), F32),
        compiler_params=_params("arbitrary"),
        name="gla_prep",
    )(u, alpha_up_pad, alpha_b)


def _gla_step_kernel(q_ref, k_ref, v_ref, g_ref, la_ref, nw_ref, s_ref, o_ref, so_ref):
    s = s_ref[...]
    nk = s.shape[-2]
    q = q_ref[...] * (B_KEY_DIM ** -0.5)
    k = k_ref[...]
    v = v_ref[...]
    dec = jnp.exp(la_ref[...])
    eye = (lax.broadcasted_iota(jnp.int32, (nk, nk), 0) == lax.broadcasted_iota(jnp.int32, (nk, nk), 1)).astype(F32)
    col = lambda x: jnp.sum(eye * x, axis=-1, keepdims=True)
    qk = jnp.sum(q * k, axis=-1, keepdims=True)
    o = qk * v + jnp.sum(col(q * dec) * s, axis=-2, keepdims=True)
    so_ref[...] = s * col(dec) + col(k) * v
    ms = jnp.mean(o * o, axis=-1, keepdims=True)
    g = g_ref[...]
    o_ref[...] = o * lax.rsqrt(ms + NORM_EPS) * nw_ref[...] * (g * _sigmoid(g))


def _gla_step(q, k, v, g, la, norm_w, s0, nb):
    n = q.shape[0]
    kspec = pl.BlockSpec((nb, B_HEADS, 1, B_KEY_DIM), lambda i: (i, 0, 0, 0))
    vspec = pl.BlockSpec((nb, B_HEADS, 1, B_VAL_DIM), lambda i: (i, 0, 0, 0))
    sspec = pl.BlockSpec((nb, B_HEADS, B_KEY_DIM, B_VAL_DIM), lambda i: (i, 0, 0, 0))
    return pl.pallas_call(
        _gla_step_kernel,
        grid=(n // nb,),
        in_specs=[kspec, kspec, vspec, vspec, kspec,
                  pl.BlockSpec((B_HEADS, 1, B_VAL_DIM), lambda i: (0, 0, 0)), sspec],
        out_specs=[vspec, sspec],
        out_shape=[jax.ShapeDtypeStruct((n, B_HEADS, 1, B_VAL_DIM), F32), jax.ShapeDtypeStruct(s0.shape, F32)],
        compiler_params=_params("arbitrary"),
        name="gla_step",
    )(q, k, v, g, la, norm_w, s0)


def _lru_gates(xc, wa_ref, wx_ref, ba_ref, bx_ref, lam_ref):
    ra, ix = [], []
    for nb in range(C_BLOCKS):
        xb = xc[:, nb * C_BLOCK:(nb + 1) * C_BLOCK].astype(BF16)
        ra.append(jnp.dot(xb, wa_ref[nb].astype(BF16), preferred_element_type=F32))
        ix.append(jnp.dot(xb, wx_ref[nb].astype(BF16), preferred_element_type=F32))
    r = _sigmoid(jnp.concatenate(ra, axis=-1) + ba_ref[...])
    i_g = _sigmoid(jnp.concatenate(ix, axis=-1) + bx_ref[...])
    log_a = -C_POW * r * _softplus(-lam_ref[...])
    a = jnp.exp(log_a)
    b = jnp.sqrt(1.0 - jnp.exp(2.0 * log_a)) * (i_g * xc)
    return a, b


def _lru_seq_kernel(u_ref, cp_ref, h0_ref, cw_ref, cb_ref, wa_ref, wx_ref, ba_ref, bx_ref, lam_ref,
                    y_ref, tail_ref, hT_ref, carry_ref, h_ref, a_s, b_s):
    ti = pl.program_id(1)

    @pl.when(ti == 0)
    def _():
        carry_ref[...] = cp_ref[...]
        h_ref[...] = h0_ref[...]

    gate = u_ref[:, 0:C_WIDTH]
    xb = u_ref[:, C_WIDTH:2 * C_WIDTH]
    tm = xb.shape[0]
    carry = carry_ref[...]
    row8 = lax.broadcasted_iota(jnp.int32, (8, C_WIDTH), 0)
    xc = cb_ref[...] + cw_ref[C_CONV - 1:C_CONV, :] * xb
    for s in range(1, C_CONV):
        rolled = pltpu.roll(xb, s, axis=0)
        head = jnp.where(row8 < s, pltpu.roll(carry, s, axis=0), rolled[0:8, :])
        shifted = jnp.concatenate([head, rolled[8:, :]], axis=0)
        xc = xc + cw_ref[C_CONV - 1 - s:C_CONV - s, :] * shifted
    carry_ref[...] = xb[tm - 8:tm, :]
    tail_ref[...] = xb[tm - 8:tm, :]

    a, b = _lru_gates(xc, wa_ref, wx_ref, ba_ref, bx_ref, lam_ref)
    a_s[...] = a
    b_s[...] = b

    def body(t, h):
        h = a_s[pl.ds(t, 1), :] * h + b_s[pl.ds(t, 1), :]
        b_s[pl.ds(t, 1), :] = h
        return h

    h_last = lax.fori_loop(0, tm, body, h_ref[...], unroll=8)
    h_ref[...] = h_last
    hT_ref[...] = h_last
    y_ref[...] = _gelu_tanh(gate) * b_s[...]


def _lru_seq(u, n_seq, conv_prev8, h0, conv_w, conv_b, wa, wx, ba, bx, lam, tm):
    rows = u.shape[0]
    per_seq = rows // n_seq // tm
    const = lambda shape: pl.BlockSpec(shape, lambda b, t: (0,) * len(shape))
    return pl.pallas_call(
        _lru_seq_kernel,
        grid=(n_seq, per_seq),
        in_specs=[
            pl.BlockSpec((tm, 2 * C_WIDTH), lambda b, t: (b * per_seq + t, 0)),
            pl.BlockSpec((None, 8, C_WIDTH), lambda b, t: (b, 0, 0)),
            pl.BlockSpec((None, 1, C_WIDTH), lambda b, t: (b, 0, 0)),
            const((C_CONV, C_WIDTH)), const((1, C_WIDTH)),
            const((C_BLOCKS, C_BLOCK, C_BLOCK)), const((C_BLOCKS, C_BLOCK, C_BLOCK)),
            const((1, C_WIDTH)), const((1, C_WIDTH)), const((1, C_WIDTH)),
        ],
        out_specs=[
            pl.BlockSpec((tm, C_WIDTH), lambda b, t: (b * per_seq + t, 0)),
            pl.BlockSpec((None, 8, C_WIDTH), lambda b, t: (b, 0, 0)),
            pl.BlockSpec((None, 1, C_WIDTH), lambda b, t: (b, 0, 0)),
        ],
        out_shape=[jax.ShapeDtypeStruct((rows, C_WIDTH), F32),
                   jax.ShapeDtypeStruct((n_seq, 8, C_WIDTH), F32),
                   jax.ShapeDtypeStruct((n_seq, 1, C_WIDTH), F32)],
        scratch_shapes=[pltpu.VMEM((8, C_WIDTH), F32), pltpu.VMEM((1, C_WIDTH), F32),
                        pltpu.VMEM((tm, C_WIDTH), F32), pltpu.VMEM((tm, C_WIDTH), F32)],
        compiler_params=_params("arbitrary", "arbitrary"),
        name="lru_seq",
    )(u, conv_prev8, h0, conv_w, conv_b, wa, wx, ba, bx, lam)


def _lru_step_kernel(u_ref, cp_ref, h0_ref, cw_ref, cb_ref, wa_ref, wx_ref, ba_ref, bx_ref, lam_ref,
                     y_ref, cn_ref, h_ref):
    gate = u_ref[:, 0:C_WIDTH]
    xb = u_ref[:, C_WIDTH:2 * C_WIDTH]
    xc = cb_ref[...] + cw_ref[C_CONV - 1:C_CONV, :] * xb
    for i in range(C_CONV - 1):
        xc = xc + cw_ref[i:i + 1, :] * cp_ref[:, i * C_WIDTH:(i + 1) * C_WIDTH]
    a, b = _lru_gates(xc, wa_ref, wx_ref, ba_ref, bx_ref, lam_ref)
    h = a * h0_ref[...] + b
    h_ref[...] = h
    y_ref[...] = _gelu_tanh(gate) * h
    for i in range(C_CONV - 2):
        cn_ref[:, i * C_WIDTH:(i + 1) * C_WIDTH] = cp_ref[:, (i + 1) * C_WIDTH:(i + 2) * C_WIDTH]
    cn_ref[:, (C_CONV - 2) * C_WIDTH:(C_CONV - 1) * C_WIDTH] = xb


def _lru_step(u, conv_prev, h0, conv_w, conv_b, wa, wx, ba, bx, lam):
    rows = u.shape[0]
    full = lambda shape: pl.BlockSpec(shape, lambda i: (0,) * len(shape))
    cw = (C_CONV - 1) * C_WIDTH
    return pl.pallas_call(
        _lru_step_kernel,
        grid=(1,),
        in_specs=[full((rows, 2 * C_WIDTH)), full((rows, cw)), full((rows, C_WIDTH)),
                  full((C_CONV, C_WIDTH)), full((1, C_WIDTH)),
                  full((C_BLOCKS, C_BLOCK, C_BLOCK)), full((C_BLOCKS, C_BLOCK, C_BLOCK)),
                  full((1, C_WIDTH)), full((1, C_WIDTH)), full((1, C_WIDTH))],
        out_specs=[full((rows, C_WIDTH)), full((rows, cw)), full((rows, C_WIDTH))],
        out_shape=[jax.ShapeDtypeStruct((rows, C_WIDTH), F32),
                   jax.ShapeDtypeStruct((rows, cw), F32),
                   jax.ShapeDtypeStruct((rows, C_WIDTH), F32)],
        compiler_params=_params("arbitrary"),
        name="lru_step",
    )(u, conv_prev, h0, conv_w, conv_b, wa, wx, ba, bx, lam)


def _pad_rows(w, first, total):
    return jnp.zeros((total, w.shape[1]), w.dtype).at[first:first + w.shape[0]].set(w)


def _trunk(x, mod, n_seq, seq_len, st_shift, st_wkv, st_gla, st_conv, st_lru, wts, tm, tf):
    (w_ffn_up, w_ffn_down, w_in_even, w_out_even, a_mu, a_w_up, a_a_up, a_g_up, a_vec, b_alpha_up,
     b_alpha_b, b_norm_w, w_in_odd, w_out_odd, c_conv_w, c_conv_b, c_wa, c_ba, c_wx, c_bx, c_lam,
     final_norm_w) = wts
    rows = x.shape[0]
    prompt = seq_len > 1
    tn = 512
    seg = jnp.kron(jnp.eye(A_HEADS, dtype=F32), jnp.ones((A_HEAD_DIM, A_HEAD_DIM), F32)).astype(BF16)
    out_shift, out_wkv, out_gla, out_conv, out_lru = [], [], [], [], []
    for l in range(DEPTH):
        j = l // 2
        x = _ffn(x, mod, l, 0, w_ffn_up, w_ffn_down, tm, tf)
        if l % 2 == 0:
            u = _inproj(x, mod, l, w_in_even, j, tm, tn)
            wup_pad = _pad_rows(a_w_up[j], 0, LANES)
            aup_pad = _pad_rows(a_a_up[j], A_DECAY_RANK, LANES)
            vec8 = _pad_rows(a_vec[j], 0, 8)
            alpha_pad = _pad_rows(b_alpha_up[j], 0, LANES)
            alpha_b = b_alpha_b[j].reshape(1, B_KEY_WIDTH)
            mu = a_mu[j].reshape(1, A_PROJ)
            if prompt:
                pre = _rwkv_prep(u, st_shift[j].reshape(n_seq, 1, A_PROJ), mu, wup_pad, aup_pad, a_g_up[j],
                                 vec8, seg, n_seq, 256, True)
                r, lw, k2, v, kk, al, g, bonus = pre
                hm = lambda t: t.reshape(n_seq, seq_len, A_HEADS, A_HEAD_DIM).transpose(0, 2, 1, 3)
                h0 = jnp.swapaxes(st_wkv[j], -1, -2)
                y_hm, h_t = _rwkv_chunk(hm(r), hm(lw), hm(k2), hm(v), hm(kk), hm(al), h0)
                y_raw = y_hm.transpose(0, 2, 1, 3).reshape(rows, A_WIDTH)
                s_wkv = jnp.swapaxes(h_t, -1, -2)
                s_shift = u.reshape(n_seq, seq_len, EVEN_PROJ)[:, -1, :A_PROJ]
                y_b, s_gla = _gla_chunk(u, n_seq, alpha_pad, alpha_b, b_norm_w[j].reshape(1, B_WIDTH), st_gla[j])
            else:
                pre = _rwkv_prep(u, st_shift[j], mu, wup_pad, aup_pad, a_g_up[j], vec8, seg, rows, rows, False)
                r, lw, k2, v, kk, al, g, bonus = pre
                rw = lambda t: t.reshape(rows, A_HEADS, 1, A_HEAD_DIM)
                y4, s_wkv = _rwkv_step(rw(r), rw(lw), rw(k2), rw(v), rw(kk), rw(al), st_wkv[j], 8)
                y_raw = y4.reshape(rows, A_WIDTH)
                s_shift = u[:, :A_PROJ]
                la = _gla_prep(u, alpha_pad, alpha_b)
                ub = u[:, A_PROJ:]
                kr = lambda t: t.reshape(rows, B_HEADS, 1, B_KEY_DIM)
                vr = lambda t: t.reshape(rows, B_HEADS, 1, B_VAL_DIM)
                o4, s_gla = _gla_step(kr(ub[:, :B_KEY_WIDTH]), kr(ub[:, B_KEY_WIDTH:2 * B_KEY_WIDTH]),
                                      vr(ub[:, 2 * B_KEY_WIDTH:2 * B_KEY_WIDTH + B_WIDTH]),
                                      vr(ub[:, 2 * B_KEY_WIDTH + B_WIDTH:2 * B_KEY_WIDTH + 2 * B_WIDTH]),
                                      kr(la), b_norm_w[j].reshape(B_HEADS, 1, B_VAL_DIM), st_gla[j], 8)
                y_b = o4.reshape(rows, B_WIDTH)
            y_a = _rwkv_post(y_raw, bonus, g, vec8, seg, min(rows, 512))
            x = _outproj([y_a, y_b], w_out_even, j, x, mod, l, min(tm, 512))
            out_shift.append(s_shift)
            out_wkv.append(s_wkv)
            out_gla.append(s_gla)
        else:
            u = _inproj(x, mod, l, w_in_odd, j, tm, tn)
            lru_w = (c_conv_w[j], c_conv_b[j].reshape(1, C_WIDTH), c_wa[j], c_wx[j],
                     c_ba[j].reshape(1, C_WIDTH), c_bx[j].reshape(1, C_WIDTH), c_lam[j].reshape(1, C_WIDTH))
            if prompt:
                cp8 = jnp.concatenate([jnp.zeros((n_seq, 8 - (C_CONV - 1), C_WIDTH), F32), st_conv[j]], axis=1)
                y_c, tail, h_t = _lru_seq(u, n_seq, cp8, st_lru[j].reshape(n_seq, 1, C_WIDTH), *lru_w, 256)
                s_conv = tail[:, 8 - (C_CONV - 1):, :]
                s_lru = h_t.reshape(n_seq, C_WIDTH)
            else:
                y_c, cn, s_lru = _lru_step(u, st_conv[j].reshape(rows, (C_CONV - 1) * C_WIDTH), st_lru[j], *lru_w)
                s_conv = cn.reshape(rows, C_CONV - 1, C_WIDTH)
            x = _outproj([y_c], w_out_odd, j, x, mod, l, min(tm, 512))
            out_conv.append(s_conv)
            out_lru.append(s_lru)
        x = _ffn(x, mod, l, 1, w_ffn_up, w_ffn_down, tm, tf, final_w=final_norm_w if l == DEPTH - 1 else None)
    return x, (jnp.stack(out_shift), jnp.stack(out_wkv), jnp.stack(out_gla), jnp.stack(out_conv),
               jnp.stack(out_lru))


def kernel(x_prompt, x_sample, c_prompt, c_sample, state_rwkv_shift, state_rwkv_wkv, state_gla, state_conv, state_lru, w_ada, b_ada, w_ffn_up, w_ffn_down, w_in_even, w_out_even, a_mu, a_w_up, a_a_up, a_g_up, a_vec, b_alpha_up, b_alpha_b, b_norm_w, w_in_odd, w_out_odd, c_conv_w, c_conv_b, c_wa, c_ba, c_wx, c_bx, c_lam, final_norm_w):
    to_mxu = lambda w: w.astype(BF16)
    wts = (to_mxu(w_ffn_up), to_mxu(w_ffn_down), to_mxu(w_in_even), to_mxu(w_out_even), a_mu, a_w_up, a_a_up,
           a_g_up, a_vec, b_alpha_up, b_alpha_b, b_norm_w, to_mxu(w_in_odd), to_mxu(w_out_odd), c_conv_w,
           c_conv_b, c_wa, c_ba, c_wx, c_bx, c_lam, final_norm_w)
    n_p, t_p, _ = x_prompt.shape
    n_s, t_s, _ = x_sample.shape
    assert t_s == 1

    c_rows = n_p + n_s
    pad = (-c_rows) % 16
    c_all = jnp.concatenate([c_sample, c_prompt, jnp.zeros((pad, D_MODEL), F32)], axis=0)
    mod_all = _ada(c_all, w_ada, b_ada)
    mod_p = _Mod(mod_all[:, :, n_s:c_rows].reshape(DEPTH, N_MOD, n_p, 1, D_MODEL), False, t_p)
    mod_s = _Mod(mod_all, True, 1)

    fresh = lambda s: jnp.zeros((s.shape[0], n_p) + s.shape[2:], s.dtype)
    y_p, st_p = _trunk(x_prompt.reshape(n_p * t_p, D_MODEL), mod_p, n_p, t_p,
                       fresh(state_rwkv_shift), fresh(state_rwkv_wkv), fresh(state_gla),
                       fresh(state_conv), fresh(state_lru), wts, 1024, 512)
    y_s, st_s = _trunk(x_sample.reshape(n_s, D_MODEL), mod_s, n_s, 1,
                       state_rwkv_shift, state_rwkv_wkv, state_gla, state_conv, state_lru, wts, n_s, 512)
    return (y_p.reshape(n_p, t_p, D_MODEL), y_s.reshape(n_s, 1, D_MODEL), *st_p, *st_s)
```

```python
import functools

import jax
import jax.numpy as jnp
from jax import lax
from jax.experimental import pallas as pl
from jax.experimental.pallas import tpu as pltpu

F32 = jnp.float32
BF16 = jnp.bfloat16

D_MODEL = 2048
DEPTH = 4
N_MOD = 9
D_FF = 5632
NORM_EPS = 1e-6

A_HEADS = 16
A_HEAD_DIM = 64
A_WIDTH = A_HEADS * A_HEAD_DIM
A_DECAY_RANK = 64
A_ICL_RANK = 64
A_GATE_RANK = 128
A_PROJ = 3 * A_WIDTH + A_DECAY_RANK + A_ICL_RANK + A_GATE_RANK
A_GN_EPS = 64e-5

B_HEADS = 4
B_KEY_DIM = 128
B_VAL_DIM = 256
B_KEY_WIDTH = B_HEADS * B_KEY_DIM
B_WIDTH = B_HEADS * B_VAL_DIM
B_ALPHA_RANK = 16
B_TAU = 16.0
B_PROJ = 2 * B_KEY_WIDTH + 2 * B_WIDTH + B_ALPHA_RANK
EVEN_PROJ = A_PROJ + B_PROJ

C_WIDTH = D_MODEL
C_BLOCKS = 8
C_BLOCK = C_WIDTH // C_BLOCKS
C_CONV = 4
C_POW = 8.0

CHUNK = 64
LANES = 128
VMEM_LIMIT = 56 * 1024 * 1024


def _params(*sem):
    return pltpu.CompilerParams(dimension_semantics=sem, vmem_limit_bytes=VMEM_LIMIT)


def _bdot(a, b):
    return jnp.dot(a.astype(BF16), b.astype(BF16), preferred_element_type=F32)


def _split3(x):
    hi = x.astype(BF16)
    r1 = x - hi.astype(F32)
    mid = r1.astype(BF16)
    lo = (r1 - mid.astype(F32)).astype(BF16)
    return hi, mid, lo


def _dot_exact_rhs(a_bf16, x):
    hi, mid, lo = _split3(x)
    f = lambda y: jnp.dot(a_bf16, y, preferred_element_type=F32)
    return f(hi) + f(mid) + f(lo)


def _dot_exact_lhs(x, b_bf16):
    hi, mid, lo = _split3(x)
    f = lambda y: jnp.dot(y, b_bf16, preferred_element_type=F32)
    return f(hi) + f(mid) + f(lo)


def _sigmoid(x):
    return jax.nn.sigmoid(x)


def _softplus(x):
    return jnp.maximum(x, 0.0) + jnp.log1p(jnp.exp(-jnp.abs(x)))


def _gelu_tanh(x):
    return 0.5 * x * (1.0 + jnp.tanh(0.7978845608028654 * (x + 0.044715 * (x * x * x))))


def _norm_mod(x, shift, scale):
    ms = jnp.mean(x * x, axis=-1, keepdims=True)
    return x * lax.rsqrt(ms + NORM_EPS) * (1.0 + scale) + shift


def _ada_kernel(c_ref, w_ref, b_ref, o_ref):
    c = c_ref[...]
    act = (c * _sigmoid(c)).astype(BF16)
    o_ref[...] = jnp.dot(act, w_ref[...].astype(BF16), preferred_element_type=F32) + b_ref[...]


def _ada(c_all, w_ada, b_ada):
    rows = c_all.shape[0]
    tn = 1024
    per = D_MODEL // tn
    b4 = b_ada.reshape(DEPTH, N_MOD * per, 1, tn)
    return pl.pallas_call(
        _ada_kernel,
        grid=(DEPTH, N_MOD * per),
        in_specs=[
            pl.BlockSpec((rows, D_MODEL), lambda l, j: (0, 0)),
            pl.BlockSpec((None, D_MODEL, tn), lambda l, j: (l, 0, j)),
            pl.BlockSpec((None, None, 1, tn), lambda l, j: (l, j, 0, 0)),
        ],
        out_specs=pl.BlockSpec((None, None, rows, tn), lambda l, j: (l, j // per, 0, j % per)),
        out_shape=jax.ShapeDtypeStruct((DEPTH, N_MOD, rows, D_MODEL), F32),
        compiler_params=_params("arbitrary", "arbitrary"),
        name="ada",
    )(c_all, w_ada, b4)


class _Mod:
    def __init__(self, arr, per_row, seq_len):
        self.arr = arr
        self.per_row = per_row
        self.seq_len = seq_len

    def spec(self, layer, m, tm):
        if self.per_row:
            return pl.BlockSpec((None, None, tm, D_MODEL), lambda i, *_: (layer, m, i, 0))
        per_seq = self.seq_len // tm
        return pl.BlockSpec((None, None, None, 1, D_MODEL), lambda i, *_: (layer, m, i // per_seq, 0, 0))


def _ffn_kernel(x_ref, sh_ref, sc_ref, g_ref, wg_ref, wu_ref, wd_ref, *rest, n_j, sub, final):
    if final:
        fw_ref, o_ref, h_ref = rest
    else:
        o_ref, h_ref = rest
    j = pl.program_id(1)
    tm = x_ref.shape[0]

    @pl.when(j == 0)
    def _():
        def body(s, carry):
            rows = pl.ds(pl.multiple_of(s * sub, sub), sub)
            sh = sh_ref[...] if sh_ref.shape[0] == 1 else sh_ref[rows, :]
            sc = sc_ref[...] if sc_ref.shape[0] == 1 else sc_ref[rows, :]
            h_ref[rows, :] = _norm_mod(x_ref[rows, :], sh, sc).astype(BF16)
            o_ref[rows, :] = jnp.zeros((sub, D_MODEL), F32)
            return carry
        lax.fori_loop(0, tm // sub, body, 0)

    h = h_ref[...]
    gt = jnp.dot(h, wg_ref[...], preferred_element_type=F32)
    ut = jnp.dot(h, wu_ref[...], preferred_element_type=F32)
    act = (gt * _sigmoid(gt) * ut).astype(BF16)
    o_ref[...] += jnp.dot(act, wd_ref[...], preferred_element_type=F32)

    @pl.when(j == n_j - 1)
    def _():
        def body(s, carry):
            rows = pl.ds(pl.multiple_of(s * sub, sub), sub)
            g = g_ref[...] if g_ref.shape[0] == 1 else g_ref[rows, :]
            y = x_ref[rows, :] + 0.5 * g * o_ref[rows, :]
            if final:
                ms = jnp.mean(y * y, axis=-1, keepdims=True)
                y = y * lax.rsqrt(ms + NORM_EPS) * fw_ref[...]
            o_ref[rows, :] = y
            return carry
        lax.fori_loop(0, tm // sub, body, 0)


def _ffn(x, mod, layer, which, w_up, w_down, tm, tf, final_w=None):
    m_rows = x.shape[0]
    n_j = D_FF // tf
    m0 = 6 * which
    sub = min(tm, 128)
    final = final_w is not None
    in_specs = [
        pl.BlockSpec((tm, D_MODEL), lambda i, j: (i, 0), pipeline_mode=pl.Buffered(1)),
        mod.spec(layer, m0, tm), mod.spec(layer, m0 + 1, tm), mod.spec(layer, m0 + 2, tm),
        pl.BlockSpec((None, None, D_MODEL, tf), lambda i, j: (layer, which, 0, j)),
        pl.BlockSpec((None, None, D_MODEL, tf), lambda i, j: (layer, which, 0, j + n_j)),
        pl.BlockSpec((None, None, tf, D_MODEL), lambda i, j: (layer, which, j, 0)),
    ]
    args = [x, mod.arr, mod.arr, mod.arr, w_up, w_up, w_down]
    if final:
        in_specs.append(pl.BlockSpec((1, D_MODEL), lambda i, j: (0, 0)))
        args.append(final_w.reshape(1, D_MODEL))
    return pl.pallas_call(
        functools.partial(_ffn_kernel, n_j=n_j, sub=sub, final=final),
        grid=(m_rows // tm, n_j),
        in_specs=in_specs,
        out_specs=pl.BlockSpec((tm, D_MODEL), lambda i, j: (i, 0)),
        out_shape=jax.ShapeDtypeStruct((m_rows, D_MODEL), F32),
        scratch_shapes=[pltpu.VMEM((tm, D_MODEL), BF16)],
        compiler_params=_params("arbitrary", "arbitrary"),
        name="ffn",
    )(*args)


def _inproj_kernel(x_ref, sh_ref, sc_ref, w_ref, o_ref, h_ref, *, sub):
    j = pl.program_id(1)
    tm = x_ref.shape[0]

    @pl.when(j == 0)
    def _():
        def body(s, carry):
            rows = pl.ds(pl.multiple_of(s * sub, sub), sub)
            sh = sh_ref[...] if sh_ref.shape[0] == 1 else sh_ref[rows, :]
            sc = sc_ref[...] if sc_ref.shape[0] == 1 else sc_ref[rows, :]
            h_ref[rows, :] = _norm_mod(x_ref[rows, :], sh, sc).astype(BF16)
            return carry
        lax.fori_loop(0, tm // sub, body, 0)

    o_ref[...] = jnp.dot(h_ref[...], w_ref[...], preferred_element_type=F32)


def _inproj(x, mod, layer, w, widx, tm, tn):
    m_rows = x.shape[0]
    n_out = w.shape[-1]
    sub = min(tm, 128)
    return pl.pallas_call(
        functools.partial(_inproj_kernel, sub=sub),
        grid=(m_rows // tm, pl.cdiv(n_out, tn)),
        in_specs=[
            pl.BlockSpec((tm, D_MODEL), lambda i, j: (i, 0)),
            mod.spec(layer, 3, tm), mod.spec(layer, 4, tm),
            pl.BlockSpec((None, D_MODEL, tn), lambda i, j: (widx, 0, j)),
        ],
        out_specs=pl.BlockSpec((tm, tn), lambda i, j: (i, j)),
        out_shape=jax.ShapeDtypeStruct((m_rows, n_out), F32),
        scratch_shapes=[pltpu.VMEM((tm, D_MODEL), BF16)],
        compiler_params=_params("arbitrary", "arbitrary"),
        name="inproj",
    )(x, mod.arr, mod.arr, w)


def _outproj_kernel(*refs, n_in):
    y_refs = refs[:n_in]
    w_refs = refs[n_in:2 * n_in]
    x_ref, g_ref, o_ref = refs[2 * n_in:]
    acc = _bdot(y_refs[0][...], w_refs[0][...])
    for y_ref, w_ref in zip(y_refs[1:], w_refs[1:]):
        acc = acc + _bdot(y_ref[...], w_ref[...])
    o_ref[...] = x_ref[...] + g_ref[...] * acc


def _outproj(ys, w, widx, x, mod, layer, tm):
    m_rows = x.shape[0]
    n_in = len(ys)
    kw = ys[0].shape[1]
    in_specs = [pl.BlockSpec((tm, kw), lambda i: (i, 0)) for _ in ys]
    in_specs += [pl.BlockSpec((None, kw, D_MODEL), lambda i, k=k: (widx, k, 0)) for k in range(n_in)]
    in_specs += [pl.BlockSpec((tm, D_MODEL), lambda i: (i, 0)), mod.spec(layer, 5, tm)]
    return pl.pallas_call(
        functools.partial(_outproj_kernel, n_in=n_in),
        grid=(m_rows // tm,),
        in_specs=in_specs,
        out_specs=pl.BlockSpec((tm, D_MODEL), lambda i: (i, 0)),
        out_shape=jax.ShapeDtypeStruct((m_rows, D_MODEL), F32),
        compiler_params=_params("arbitrary"),
        name="outproj",
    )(*ys, *([w] * n_in), x, mod.arr)


def _rwkv_prep_kernel(u_ref, prev_ref, mu_ref, wup_ref, aup_ref, gup_ref, vec_ref, *rest, seq_mode):
    if seq_mode:
        r_ref, lw_ref, k_ref, v_ref, kk_ref, a_ref, g_ref, carry_ref = rest
    else:
        seg_ref, r_ref, lw_ref, k_ref, v_ref, kk_ref, a_ref, g_ref, bonus_ref = rest
    u = u_ref[...]
    tm = u.shape[0]
    if seq_mode:
        @pl.when(pl.program_id(1) == 0)
        def _():
            carry_ref[...] = prev_ref[...]
        rolled = pltpu.roll(u, 1, axis=0)
        row = lax.broadcasted_iota(jnp.int32, u.shape, 0)
        prev = jnp.where(row == 0, carry_ref[...], rolled)
        carry_ref[...] = u[tm - 1:tm, :]
    else:
        prev = prev_ref[...]
    xs = u + mu_ref[...] * (prev - u)
    r = xs[:, 0:A_WIDTH]
    k = xs[:, A_WIDTH:2 * A_WIDTH]
    v = xs[:, 2 * A_WIDTH:3 * A_WIDTH]
    wa_lo = xs[:, 3 * A_WIDTH:3 * A_WIDTH + LANES]
    g_lo = xs[:, 3 * A_WIDTH + LANES:]
    w0, a0, k_k, k_a, r_k = (vec_ref[i:i + 1, :] for i in range(5))
    w_raw = -_softplus(-(w0 + _bdot(jnp.tanh(wa_lo), wup_ref[...]))) - 0.5
    lw = -jnp.exp(w_raw)
    a = _sigmoid(a0 + _bdot(wa_lo, aup_ref[...]))
    g = _bdot(_sigmoid(g_lo), gup_ref[...])
    kk = k * k_k
    k2 = k * (1.0 + (a - 1.0) * k_a)
    if not seq_mode:
        seg = seg_ref[...]
        kk = kk * lax.rsqrt(jnp.maximum(_dot_exact_lhs(kk * kk, seg), 1e-24))
        bonus_ref[...] = _dot_exact_lhs(r * k2 * r_k, seg) * v
    r_ref[...] = r
    lw_ref[...] = lw
    k_ref[...] = k2
    v_ref[...] = v
    kk_ref[...] = kk
    a_ref[...] = a
    g_ref[...] = g


def _rwkv_prep(u, prev, mu, wup_pad, aup_pad, g_up, vec, seg, n_seq, tm, seq_mode):
    rows = u.shape[0]
    per_seq = rows // n_seq // tm if seq_mode else 1
    grid = (n_seq, per_seq) if seq_mode else (rows // tm, 1)
    rowmap = (lambda b, t: (b * per_seq + t, 0)) if seq_mode else (lambda b, t: (b, 0))
    prev_spec = (pl.BlockSpec((None, 1, A_PROJ), lambda b, t: (b, 0, 0)) if seq_mode
                 else pl.BlockSpec((tm, A_PROJ), rowmap))
    const = lambda shape: pl.BlockSpec(shape, lambda b, t: (0,) * len(shape))
    out = jax.ShapeDtypeStruct((rows, A_WIDTH), F32)
    in_specs = [pl.BlockSpec((tm, A_PROJ), rowmap), prev_spec,
                const((1, A_PROJ)), const((LANES, A_WIDTH)), const((LANES, A_WIDTH)),
                const((A_GATE_RANK, A_WIDTH)), const((8, A_WIDTH))]
    args = [u, prev, mu, wup_pad, aup_pad, g_up, vec]
    if not seq_mode:
        in_specs.append(const((A_WIDTH, A_WIDTH)))
        args.append(seg)
    n_out = 7 if seq_mode else 8
    return pl.pallas_call(
        functools.partial(_rwkv_prep_kernel, seq_mode=seq_mode),
        grid=grid,
        in_specs=in_specs,
        out_specs=[pl.BlockSpec((tm, A_WIDTH), rowmap)] * n_out,
        out_shape=[out] * n_out,
        scratch_shapes=[pltpu.VMEM((1, A_PROJ), F32)] if seq_mode else [],
        compiler_params=_params("arbitrary", "arbitrary"),
        name="rwkv_prep",
    )(*args)


def _rwkv_post_kernel(y_ref, bonus_ref, g_ref, vec_ref, seg_ref, o_ref):
    y = y_ref[...]
    seg = seg_ref[...]
    inv_n = 1.0 / A_HEAD_DIM
    yc = y - _dot_exact_lhs(y, seg) * inv_n
    var = _dot_exact_lhs(yc * yc, seg) * inv_n
    yn = yc * lax.rsqrt(var + A_GN_EPS) * vec_ref[5:6, :] + vec_ref[6:7, :]
    o_ref[...] = (yn + bonus_ref[...]) * g_ref[...]


def _rwkv_post(y, bonus, g, vec, seg, tm):
    rows = y.shape[0]
    spec = pl.BlockSpec((tm, A_WIDTH), lambda i: (i, 0))
    return pl.pallas_call(
        _rwkv_post_kernel,
        grid=(rows // tm,),
        in_specs=[spec, spec, spec,
                  pl.BlockSpec((8, A_WIDTH), lambda i: (0, 0)),
                  pl.BlockSpec((A_WIDTH, A_WIDTH), lambda i: (0, 0))],
        out_specs=spec,
        out_shape=jax.ShapeDtypeStruct((rows, A_WIDTH), F32),
        compiler_params=_params("arbitrary"),
        name="rwkv_post",
    )(y, bonus, g, vec, seg)


def _rwkv_chunk_kernel(r_ref, lw_ref, k_ref, v_ref, kk_ref, a_ref, g_ref, vec_ref, h0_ref, y_ref, hT_ref, h_ref):
    ci = pl.program_id(1)

    @pl.when(ci == 0)
    def _():
        h_ref[...] = h0_ref[...]

    cs = r_ref.shape[0]
    n2 = 2 * cs
    npair = A_WIDTH // LANES
    pairs = lambda ref: jnp.stack([ref[:, p * LANES:(p + 1) * LANES] for p in range(npair)])
    vrow = lambda i: jnp.stack([vec_ref[i:i + 1, p * LANES:(p + 1) * LANES] for p in range(npair)])
    lane = lax.broadcasted_iota(jnp.int32, (1, 1, LANES), 2)
    m0 = (lane < A_HEAD_DIM).astype(F32)
    m1 = 1.0 - m0
    split = lambda x: jnp.concatenate([x * m0, x * m1], axis=1)
    dup = lambda x: jnp.concatenate([x, x], axis=1)
    own = jnp.concatenate([jnp.broadcast_to(m0, (1, cs, LANES)), jnp.broadcast_to(m1, (1, cs, LANES))], axis=1)

    def bmm(x, y):
        return jnp.einsum('hab,hbc->hac', x.astype(BF16), y.astype(BF16), preferred_element_type=F32)

    def bmm_nt(x, y):
        return jnp.einsum('han,hbn->hab', x.astype(BF16), y.astype(BF16), preferred_element_type=F32)

    def bmm_tn(x, y):
        return jnp.einsum('hca,hcb->hab', x.astype(BF16), y.astype(BF16), preferred_element_type=F32)

    lw = pairs(lw_ref)
    row = lax.broadcasted_iota(jnp.int32, (cs, cs), 0)
    col = lax.broadcasted_iota(jnp.int32, (cs, cs), 1)
    tri = jnp.broadcast_to((col <= row).astype(BF16)[None], (npair, cs, cs))
    hi, mid, lo = _split3(lw)
    csum = lambda part: jnp.einsum('hab,hbc->hac', tri, part, preferred_element_type=F32)
    c = csum(hi) + csum(mid) + csum(lo)
    c_end = c[:, cs - 1:cs, :]
    e_pos = dup(jnp.exp(c))
    e_neg = dup(jnp.exp(-c))
    e_prev = dup(jnp.exp(c - lw))
    e_end = dup(jnp.exp(c_end - c))
    g_end = jnp.exp(c_end)

    kk = split(pairs(kk_ref))
    kk = kk * lax.rsqrt(jnp.maximum(jnp.sum(kk * kk, axis=-1, keepdims=True), 1e-24))
    b_vec = kk * dup(pairs(a_ref))
    r2 = split(pairs(r_ref))
    k2 = split(pairs(k_ref))
    v2 = split(pairs(v_ref))
    bonus = jnp.sum(r2 * k2 * vrow(4), axis=-1, keepdims=True) * v2
    rt = r2 * e_pos
    at = -kk * e_prev
    bt = b_vec * e_neg
    kt = k2 * e_neg
    bh = b_vec * e_end
    kh = k2 * e_end

    row2 = lax.broadcasted_iota(jnp.int32, (n2, n2), 0)
    col2 = lax.broadcasted_iota(jnp.int32, (n2, n2), 1)
    same = (row2 >= cs) == (col2 >= cs)
    tok_r = jnp.where(row2 >= cs, row2 - cs, row2)
    tok_c = jnp.where(col2 >= cs, col2 - cs, col2)
    strict = jnp.where(same, (tok_c < tok_r).astype(F32), 0.0)
    incl = jnp.where(same, (tok_c <= tok_r).astype(F32), 0.0)
    eye2 = (row2 == col2).astype(F32)
    nrow = lax.broadcasted_iota(jnp.int32, (LANES, LANES), 0)
    ncol = lax.broadcasted_iota(jnp.int32, (LANES, LANES), 1)
    eye_n = (nrow == ncol).astype(F32)

    a_ab = bmm_nt(at, bt) * strict
    a_ak = bmm_nt(at, kt) * strict
    m_b = bmm_nt(rt, bt) * incl
    m_k = bmm_nt(rt, kt) * incl

    tinv = eye2 + a_ab
    apow = a_ab
    span = 1
    while 2 * span < cs:
        apow = bmm(apow, apow)
        tinv = tinv + bmm(tinv, apow)
        span *= 2

    akv = bmm(a_ak, v2)
    pq = bmm(tinv, jnp.concatenate([at, akv], axis=-1))
    mpq = bmm(m_b, pq)
    p2 = rt + mpq[..., :LANES]
    y0 = mpq[..., LANES:] + bmm(m_k, v2)
    tpq = bmm_tn(bh, pq)
    gm = eye_n * g_end + tpq[..., :LANES]
    hadd = tpq[..., LANES:] + bmm_tn(kh, v2)

    h = h_ref[...]
    y2 = bmm(p2, h) + y0
    h_new = bmm(gm, h) + hadd
    h_ref[...] = h_new

    inv_n = 1.0 / A_HEAD_DIM
    yc = (y2 - jnp.sum(y2, axis=-1, keepdims=True) * inv_n) * own
    var = jnp.sum(yc * yc, axis=-1, keepdims=True) * inv_n
    z = yc * lax.rsqrt(var + A_GN_EPS) * vrow(5) + vrow(6) * own + bonus
    out = (z[:, :cs] + z[:, cs:]) * pairs(g_ref)
    for p in range(npair):
        y_ref[:, p * LANES:(p + 1) * LANES] = out[p]

    @pl.when(ci == pl.num_programs(1) - 1)
    def _():
        hT_ref[...] = h_new


def _rwkv_chunk(r, lw, k, v, kk, a, g, vec, n_seq, h0):
    rows = r.shape[0]
    nc = rows // n_seq // CHUNK
    npair = A_WIDTH // LANES
    spec = pl.BlockSpec((CHUNK, A_WIDTH), lambda b, c: (b * nc + c, 0))
    hspec = pl.BlockSpec((None, npair, LANES, LANES), lambda b, c: (b, 0, 0, 0))
    return pl.pallas_call(
        _rwkv_chunk_kernel,
        grid=(n_seq, nc),
        in_specs=[spec] * 7 + [pl.BlockSpec((8, A_WIDTH), lambda b, c: (0, 0)), hspec],
        out_specs=[spec, hspec],
        out_shape=[jax.ShapeDtypeStruct((rows, A_WIDTH), F32),
                   jax.ShapeDtypeStruct((n_seq, npair, LANES, LANES), F32)],
        scratch_shapes=[pltpu.VMEM((npair, LANES, LANES), F32)],
        compiler_params=_params("arbitrary", "arbitrary"),
        name="rwkv_chunk",
    )(r, lw, k, v, kk, a, g, vec, h0)


def _pair_states(s):
    n = s.shape[0]
    nd = A_HEAD_DIM
    h = jnp.swapaxes(s, -1, -2).reshape(n, A_HEADS // 2, 2, nd, nd)
    out = jnp.zeros((n, A_HEADS // 2, 2 * nd, 2 * nd), s.dtype)
    return out.at[:, :, :nd, :nd].set(h[:, :, 0]).at[:, :, nd:, nd:].set(h[:, :, 1])


def _unpair_states(hp):
    n = hp.shape[0]
    nd = A_HEAD_DIM
    h = jnp.stack([hp[:, :, :nd, :nd], hp[:, :, nd:, nd:]], axis=2).reshape(n, A_HEADS, nd, nd)
    return jnp.swapaxes(h, -1, -2)


def _rwkv_step_kernel(r_ref, lw_ref, k_ref, v_ref, kk_ref, a_ref, s_ref, y_ref, so_ref):
    nb = r_ref.shape[0]
    nd = A_HEAD_DIM
    eye = (lax.broadcasted_iota(jnp.int32, (nd, nd), 0) == lax.broadcasted_iota(jnp.int32, (nd, nd), 1)).astype(F32)

    def body(n, carry):
        row = pl.ds(n, 1)
        r = r_ref[row, :]
        w = jnp.exp(lw_ref[row, :])
        k = k_ref[row, :]
        v = v_ref[row, :]
        kk = kk_ref[row, :]
        b_vec = kk * a_ref[row, :]
        ys = []
        for h in range(A_HEADS):
            hs = slice(h * nd, (h + 1) * nd)
            s = s_ref[n, h]
            s_kk = jnp.sum(s * (-kk[:, hs]), axis=-1, keepdims=True)
            v_col = jnp.sum(eye * v[:, hs], axis=-1, keepdims=True)
            s_new = s * w[:, hs] + s_kk * b_vec[:, hs] + v_col * k[:, hs]
            so_ref[n, h] = s_new
            y_col = jnp.sum(s_new * r[:, hs], axis=-1, keepdims=True)
            ys.append(jnp.sum(eye * y_col, axis=0, keepdims=True))
        y_ref[row, :] = jnp.concatenate(ys, axis=-1)
        return carry

    lax.fori_loop(0, nb, body, 0)


def _rwkv_step(r, lw, k, v, kk, a, states, j, nb):
    n = r.shape[0]
    rspec = pl.BlockSpec((nb, A_WIDTH), lambda i: (i, 0))
    return pl.pallas_call(
        _rwkv_step_kernel,
        grid=(n // nb,),
        in_specs=[rspec] * 6 + [pl.BlockSpec((None, nb, A_HEADS, A_HEAD_DIM, A_HEAD_DIM),
                                             lambda i: (j, i, 0, 0, 0))],
        out_specs=[rspec, pl.BlockSpec((nb, A_HEADS, A_HEAD_DIM, A_HEAD_DIM), lambda i: (i, 0, 0, 0))],
        out_shape=[jax.ShapeDtypeStruct((n, A_WIDTH), F32), jax.ShapeDtypeStruct(states.shape[1:], F32)],
        compiler_params=_params("arbitrary"),
        name="rwkv_step",
    )(r, lw, k, v, kk, a, states)


def _gla_log_alpha(a128, aup_ref, ab_ref):
    lane = lax.broadcasted_iota(jnp.int32, a128.shape, 1)
    a_lo = jnp.where(lane < B_ALPHA_RANK, a128, 0.0)
    x = _bdot(a_lo, aup_ref[...]) + ab_ref[...]
    return -_softplus(-x) * (1.0 / B_TAU)


def _gla_chunk_kernel(u_ref, aup_ref, ab_ref, nw_ref, s0_ref, o_ref, sT_ref, s_ref):
    ci = pl.program_id(1)

    @pl.when(ci == 0)
    def _():
        s_ref[...] = s0_ref[...]

    ub = u_ref[...]
    cs = ub.shape[0]
    q_all = ub[:, 0:B_KEY_WIDTH] * (B_KEY_DIM ** -0.5)
    k_all = ub[:, B_KEY_WIDTH:2 * B_KEY_WIDTH]
    v_all = ub[:, 2 * B_KEY_WIDTH:2 * B_KEY_WIDTH + B_WIDTH]
    g_all = ub[:, 2 * B_KEY_WIDTH + B_WIDTH:2 * B_KEY_WIDTH + 2 * B_WIDTH]
    a128 = ub[:, 2 * B_KEY_WIDTH + 2 * B_WIDTH:2 * B_KEY_WIDTH + 2 * B_WIDTH + LANES]
    log_a = _gla_log_alpha(a128, aup_ref, ab_ref)

    row = lax.broadcasted_iota(jnp.int32, (cs, cs), 0)
    col = lax.broadcasted_iota(jnp.int32, (cs, cs), 1)
    incl = col <= row
    cum = _dot_exact_rhs(incl.astype(BF16), log_a)
    e_pos = jnp.exp(cum)
    e_neg = jnp.exp(-cum)
    cum_end = cum[cs - 1:cs, :]
    e_end = jnp.exp(cum_end - cum)
    g_end = jnp.exp(cum_end)
    nk = B_KEY_DIM
    eye_k = (lax.broadcasted_iota(jnp.int32, (nk, nk), 0) == lax.broadcasted_iota(jnp.int32, (nk, nk), 1)).astype(F32)

    for h in range(B_HEADS):
        ks = slice(h * B_KEY_DIM, (h + 1) * B_KEY_DIM)
        vs = slice(h * B_VAL_DIM, (h + 1) * B_VAL_DIM)
        qe = q_all[:, ks] * e_pos[:, ks]
        kn = k_all[:, ks] * e_neg[:, ks]
        ke = k_all[:, ks] * e_end[:, ks]
        vh = v_all[:, vs]
        s = s_ref[h]
        scores = jnp.einsum('id,jd->ij', qe.astype(BF16), kn.astype(BF16), preferred_element_type=F32)
        scores = jnp.where(incl, scores, 0.0)
        o = _bdot(scores, vh) + _bdot(qe, s)
        g_col = jnp.sum(eye_k * g_end[:, ks], axis=-1, keepdims=True)
        s_ref[h] = s * g_col + jnp.einsum('jd,jv->dv', ke.astype(BF16), vh.astype(BF16),
                                           preferred_element_type=F32)
        ms = jnp.mean(o * o, axis=-1, keepdims=True)
        gh = g_all[:, vs]
        o_ref[:, vs] = o * lax.rsqrt(ms + NORM_EPS) * nw_ref[:, vs] * (gh * _sigmoid(gh))

    @pl.when(ci == pl.num_programs(1) - 1)
    def _():
        sT_ref[...] = s_ref[...]


def _gla_chunk(u, n_seq, alpha_up_pad, alpha_b, norm_w, s0):
    rows = u.shape[0]
    t = rows // n_seq
    nc = t // CHUNK
    return pl.pallas_call(
        _gla_chunk_kernel,
        grid=(n_seq, nc),
        in_specs=[
            pl.BlockSpec((CHUNK, A_PROJ), lambda b, c: (b * nc + c, 1)),
            pl.BlockSpec((LANES, B_KEY_WIDTH), lambda b, c: (0, 0)),
            pl.BlockSpec((1, B_KEY_WIDTH), lambda b, c: (0, 0)),
            pl.BlockSpec((1, B_WIDTH), lambda b, c: (0, 0)),
            pl.BlockSpec((None, B_HEADS, B_KEY_DIM, B_VAL_DIM), lambda b, c: (b, 0, 0, 0)),
        ],
        out_specs=[
            pl.BlockSpec((CHUNK, B_WIDTH), lambda b, c: (b * nc + c, 0)),
            pl.BlockSpec((None, B_HEADS, B_KEY_DIM, B_VAL_DIM), lambda b, c: (b, 0, 0, 0)),
        ],
        out_shape=[jax.ShapeDtypeStruct((rows, B_WIDTH), F32),
                   jax.ShapeDtypeStruct((n_seq, B_HEADS, B_KEY_DIM, B_VAL_DIM), F32)],
        scratch_shapes=[pltpu.VMEM((B_HEADS, B_KEY_DIM, B_VAL_DIM), F32)],
        compiler_params=_params("arbitrary", "arbitrary"),
        name="gla_chunk",
    )(u, alpha_up_pad, alpha_b, norm_w, s0)


def _gla_step_kernel(u_ref, aup_ref, ab_ref, nw_ref, s_ref, o_ref, so_ref, la_ref):
    nb = u_ref.shape[0]
    nk = B_KEY_DIM
    off_k, off_v, off_g = B_KEY_WIDTH, 2 * B_KEY_WIDTH, 2 * B_KEY_WIDTH + B_WIDTH
    off_a = 2 * B_KEY_WIDTH + 2 * B_WIDTH
    la_ref[...] = _gla_log_alpha(u_ref[:, off_a:off_a + LANES], aup_ref, ab_ref)
    eye = (lax.broadcasted_iota(jnp.int32, (nk, nk), 0) == lax.broadcasted_iota(jnp.int32, (nk, nk), 1)).astype(F32)
    col = lambda x: jnp.sum(eye * x, axis=-1, keepdims=True)

    def body(n, carry):
        row = pl.ds(n, 1)
        ur = u_ref[row, :]
        dec_all = jnp.exp(la_ref[row, :])
        outs = []
        for h in range(B_HEADS):
            q = ur[:, h * nk:(h + 1) * nk] * (B_KEY_DIM ** -0.5)
            k = ur[:, off_k + h * nk:off_k + (h + 1) * nk]
            v = ur[:, off_v + h * B_VAL_DIM:off_v + (h + 1) * B_VAL_DIM]
            g = ur[:, off_g + h * B_VAL_DIM:off_g + (h + 1) * B_VAL_DIM]
            dec = dec_all[:, h * nk:(h + 1) * nk]
            s = s_ref[n, h]
            qk = jnp.sum(q * k, axis=-1, keepdims=True)
            o = qk * v + jnp.sum(col(q * dec) * s, axis=0, keepdims=True)
            so_ref[n, h] = s * col(dec) + col(k) * v
            ms = jnp.mean(o * o, axis=-1, keepdims=True)
            outs.append(o * lax.rsqrt(ms + NORM_EPS) * nw_ref[:, h * B_VAL_DIM:(h + 1) * B_VAL_DIM]
                        * (g * _sigmoid(g)))
        o_ref[row, :] = jnp.concatenate(outs, axis=-1)
        return carry

    lax.fori_loop(0, nb, body, 0)


def _gla_step(u, alpha_up_pad, alpha_b, norm_w, states, j, nb):
    n = u.shape[0]
    return pl.pallas_call(
        _gla_step_kernel,
        grid=(n // nb,),
        in_specs=[pl.BlockSpec((nb, A_PROJ), lambda i: (i, 1)),
                  pl.BlockSpec((LANES, B_KEY_WIDTH), lambda i: (0, 0)),
                  pl.BlockSpec((1, B_KEY_WIDTH), lambda i: (0, 0)),
                  pl.BlockSpec((1, B_WIDTH), lambda i: (0, 0)),
                  pl.BlockSpec((None, nb, B_HEADS, B_KEY_DIM, B_VAL_DIM), lambda i: (j, i, 0, 0, 0))],
        out_specs=[pl.BlockSpec((nb, B_WIDTH), lambda i: (i, 0)),
                   pl.BlockSpec((nb, B_HEADS, B_KEY_DIM, B_VAL_DIM), lambda i: (i, 0, 0, 0))],
        out_shape=[jax.ShapeDtypeStruct((n, B_WIDTH), F32), jax.ShapeDtypeStruct(states.shape[1:], F32)],
        scratch_shapes=[pltpu.VMEM((nb, B_KEY_WIDTH), F32)],
        compiler_params=_params("arbitrary"),
        name="gla_step",
    )(u, alpha_up_pad, alpha_b, norm_w, states)


def _lru_gates(xc, wa_ref, wx_ref, ba_ref, bx_ref, lam_ref):
    ra, ix = [], []
    for nb in range(C_BLOCKS):
        xb = xc[:, nb * C_BLOCK:(nb + 1) * C_BLOCK].astype(BF16)
        ra.append(jnp.dot(xb, wa_ref[nb].astype(BF16), preferred_element_type=F32))
        ix.append(jnp.dot(xb, wx_ref[nb].astype(BF16), preferred_element_type=F32))
    r = _sigmoid(jnp.concatenate(ra, axis=-1) + ba_ref[...])
    i_g = _sigmoid(jnp.concatenate(ix, axis=-1) + bx_ref[...])
    log_a = -C_POW * r * _softplus(-lam_ref[...])
    a = jnp.exp(log_a)
    b = jnp.sqrt(1.0 - jnp.exp(2.0 * log_a)) * (i_g * xc)
    return a, b


def _lru_seq_kernel(u_ref, cp_ref, h0_ref, cw_ref, cb_ref, wa_ref, wx_ref, ba_ref, bx_ref, lam_ref,
                    y_ref, tail_ref, hT_ref, carry_ref, h_ref, a_s, b_s):
    ti = pl.program_id(1)

    @pl.when(ti == 0)
    def _():
        carry_ref[...] = cp_ref[...]
        h_ref[...] = h0_ref[...]

    gate = u_ref[:, 0:C_WIDTH]
    xb = u_ref[:, C_WIDTH:2 * C_WIDTH]
    tm = xb.shape[0]
    carry = carry_ref[...]
    row8 = lax.broadcasted_iota(jnp.int32, (8, C_WIDTH), 0)
    xc = cb_ref[...] + cw_ref[C_CONV - 1:C_CONV, :] * xb
    for s in range(1, C_CONV):
        rolled = pltpu.roll(xb, s, axis=0)
        head = jnp.where(row8 < s, pltpu.roll(carry, s, axis=0), rolled[0:8, :])
        shifted = jnp.concatenate([head, rolled[8:, :]], axis=0)
        xc = xc + cw_ref[C_CONV - 1 - s:C_CONV - s, :] * shifted
    carry_ref[...] = xb[tm - 8:tm, :]
    tail_ref[...] = xb[tm - 8:tm, :]

    a, b = _lru_gates(xc, wa_ref, wx_ref, ba_ref, bx_ref, lam_ref)
    a_s[...] = a
    b_s[...] = b

    def body(t, h):
        h = a_s[pl.ds(t, 1), :] * h + b_s[pl.ds(t, 1), :]
        b_s[pl.ds(t, 1), :] = h
        return h

    h_last = lax.fori_loop(0, tm, body, h_ref[...], unroll=8)
    h_ref[...] = h_last
    hT_ref[...] = h_last
    y_ref[...] = _gelu_tanh(gate) * b_s[...]


def _lru_seq(u, n_seq, conv_prev8, h0, conv_w, conv_b, wa, wx, ba, bx, lam, tm):
    rows = u.shape[0]
    per_seq = rows // n_seq // tm
    const = lambda shape: pl.BlockSpec(shape, lambda b, t: (0,) * len(shape))
    return pl.pallas_call(
        _lru_seq_kernel,
        grid=(n_seq, per_seq),
        in_specs=[
            pl.BlockSpec((tm, 2 * C_WIDTH), lambda b, t: (b * per_seq + t, 0)),
            pl.BlockSpec((None, 8, C_WIDTH), lambda b, t: (b, 0, 0)),
            pl.BlockSpec((None, 1, C_WIDTH), lambda b, t: (b, 0, 0)),
            const((C_CONV, C_WIDTH)), const((1, C_WIDTH)),
            const((C_BLOCKS, C_BLOCK, C_BLOCK)), const((C_BLOCKS, C_BLOCK, C_BLOCK)),
            const((1, C_WIDTH)), const((1, C_WIDTH)), const((1, C_WIDTH)),
        ],
        out_specs=[
            pl.BlockSpec((tm, C_WIDTH), lambda b, t: (b * per_seq + t, 0)),
            pl.BlockSpec((None, 8, C_WIDTH), lambda b, t: (b, 0, 0)),
            pl.BlockSpec((None, 1, C_WIDTH), lambda b, t: (b, 0, 0)),
        ],
        out_shape=[jax.ShapeDtypeStruct((rows, C_WIDTH), F32),
                   jax.ShapeDtypeStruct((n_seq, 8, C_WIDTH), F32),
                   jax.ShapeDtypeStruct((n_seq, 1, C_WIDTH), F32)],
        scratch_shapes=[pltpu.VMEM((8, C_WIDTH), F32), pltpu.VMEM((1, C_WIDTH), F32),
                        pltpu.VMEM((tm, C_WIDTH), F32), pltpu.VMEM((tm, C_WIDTH), F32)],
        compiler_params=_params("arbitrary", "arbitrary"),
        name="lru_seq",
    )(u, conv_prev8, h0, conv_w, conv_b, wa, wx, ba, bx, lam)


def _lru_step_kernel(u_ref, cp_ref, h0_ref, cw_ref, cb_ref, wa_ref, wx_ref, ba_ref, bx_ref, lam_ref,
                     y_ref, cn_ref, h_ref):
    gate = u_ref[:, 0:C_WIDTH]
    xb = u_ref[:, C_WIDTH:2 * C_WIDTH]
    xc = cb_ref[...] + cw_ref[C_CONV - 1:C_CONV, :] * xb
    for i in range(C_CONV - 1):
        xc = xc + cw_ref[i:i + 1, :] * cp_ref[:, i * C_WIDTH:(i + 1) * C_WIDTH]
    a, b = _lru_gates(xc, wa_ref, wx_ref, ba_ref, bx_ref, lam_ref)
    h = a * h0_ref[...] + b
    h_ref[...] = h
    y_ref[...] = _gelu_tanh(gate) * h
    for i in range(C_CONV - 2):
        cn_ref[:, i * C_WIDTH:(i + 1) * C_WIDTH] = cp_ref[:, (i + 1) * C_WIDTH:(i + 2) * C_WIDTH]
    cn_ref[:, (C_CONV - 2) * C_WIDTH:(C_CONV - 1) * C_WIDTH] = xb


def _lru_step(u, conv_prev, h0, conv_w, conv_b, wa, wx, ba, bx, lam):
    rows = u.shape[0]
    full = lambda shape: pl.BlockSpec(shape, lambda i: (0,) * len(shape))
    cw = (C_CONV - 1) * C_WIDTH
    return pl.pallas_call(
        _lru_step_kernel,
        grid=(1,),
        in_specs=[full((rows, 2 * C_WIDTH)), full((rows, cw)), full((rows, C_WIDTH)),
                  full((C_CONV, C_WIDTH)), full((1, C_WIDTH)),
                  full((C_BLOCKS, C_BLOCK, C_BLOCK)), full((C_BLOCKS, C_BLOCK, C_BLOCK)),
                  full((1, C_WIDTH)), full((1, C_WIDTH)), full((1, C_WIDTH))],
        out_specs=[full((rows, C_WIDTH)), full((rows, cw)), full((rows, C_WIDTH))],
        out_shape=[jax.ShapeDtypeStruct((rows, C_WIDTH), F32),
                   jax.ShapeDtypeStruct((rows, cw), F32),
                   jax.ShapeDtypeStruct((rows, C_WIDTH), F32)],
        compiler_params=_params("arbitrary"),
        name="lru_step",
    )(u, conv_prev, h0, conv_w, conv_b, wa, wx, ba, bx, lam)


def _pad_rows(w, first, total):
    return jnp.zeros((total, w.shape[1]), w.dtype).at[first:first + w.shape[0]].set(w)


def _trunk(x, mod, n_seq, seq_len, st_shift, st_wkv, st_gla, st_conv, st_lru, wts, tm, tf):
    (w_ffn_up, w_ffn_down, w_in_even, w_out_even, a_mu, a_w_up, a_a_up, a_g_up, a_vec, b_alpha_up,
     b_alpha_b, b_norm_w, w_in_odd, w_out_odd, c_conv_w, c_conv_b, c_wa, c_ba, c_wx, c_bx, c_lam,
     final_norm_w) = wts
    rows = x.shape[0]
    prompt = seq_len > 1
    tn = 512
    seg = jnp.kron(jnp.eye(A_HEADS, dtype=F32), jnp.ones((A_HEAD_DIM, A_HEAD_DIM), F32)).astype(BF16)
    out_shift, out_wkv, out_gla, out_conv, out_lru = [], [], [], [], []
    for l in range(DEPTH):
        j = l // 2
        x = _ffn(x, mod, l, 0, w_ffn_up, w_ffn_down, tm, tf)
        if l % 2 == 0:
            u = _inproj(x, mod, l, w_in_even, j, tm, tn)
            wup_pad = _pad_rows(a_w_up[j], 0, LANES)
            aup_pad = _pad_rows(a_a_up[j], A_DECAY_RANK, LANES)
            vec8 = _pad_rows(a_vec[j], 0, 8)
            alpha_pad = _pad_rows(b_alpha_up[j], 0, LANES)
            alpha_b = b_alpha_b[j].reshape(1, B_KEY_WIDTH)
            mu = a_mu[j].reshape(1, A_PROJ)
            norm_w = b_norm_w[j].reshape(1, B_WIDTH)
            if prompt:
                r, lw, k2, v, kk, al, g = _rwkv_prep(u, st_shift[j].reshape(n_seq, 1, A_PROJ), mu, wup_pad,
                                                     aup_pad, a_g_up[j], vec8, seg, n_seq, 256, True)
                y_a, h_t = _rwkv_chunk(r, lw, k2, v, kk, al, g, vec8, n_seq, _pair_states(st_wkv[j]))
                s_wkv = _unpair_states(h_t)
                s_shift = u.reshape(n_seq, seq_len, EVEN_PROJ)[:, -1, :A_PROJ]
                y_b, s_gla = _gla_chunk(u, n_seq, alpha_pad, alpha_b, norm_w, st_gla[j])
            else:
                r, lw, k2, v, kk, al, g, bonus = _rwkv_prep(u, st_shift[j], mu, wup_pad, aup_pad, a_g_up[j],
                                                            vec8, seg, rows, rows, False)
                y_raw, s_wkv = _rwkv_step(r, lw, k2, v, kk, al, st_wkv, j, 8)
                y_a = _rwkv_post(y_raw, bonus, g, vec8, seg, rows)
                s_shift = u[:, :A_PROJ]
                y_b, s_gla = _gla_step(u, alpha_pad, alpha_b, norm_w, st_gla, j, 8)
            x = _outproj([y_a, y_b], w_out_even, j, x, mod, l, min(tm, 512))
            out_shift.append(s_shift)
            out_wkv.append(s_wkv)
            out_gla.append(s_gla)
        else:
            u = _inproj(x, mod, l, w_in_odd, j, tm, tn)
            lru_w = (c_conv_w[j], c_conv_b[j].reshape(1, C_WIDTH), c_wa[j], c_wx[j],
                     c_ba[j].reshape(1, C_WIDTH), c_bx[j].reshape(1, C_WIDTH), c_lam[j].reshape(1, C_WIDTH))
            if prompt:
                cp8 = jnp.concatenate([jnp.zeros((n_seq, 8 - (C_CONV - 1), C_WIDTH), F32), st_conv[j]], axis=1)
                y_c, tail, h_t = _lru_seq(u, n_seq, cp8, st_lru[j].reshape(n_seq, 1, C_WIDTH), *lru_w, 256)
                s_conv = tail[:, 8 - (C_CONV - 1):, :]
                s_lru = h_t.reshape(n_seq, C_WIDTH)
            else:
                y_c, cn, s_lru = _lru_step(u, st_conv[j].reshape(rows, (C_CONV - 1) * C_WIDTH), st_lru[j], *lru_w)
                s_conv = cn.reshape(rows, C_CONV - 1, C_WIDTH)
            x = _outproj([y_c], w_out_odd, j, x, mod, l, min(tm, 512))
            out_conv.append(s_conv)
            out_lru.append(s_lru)
        x = _ffn(x, mod, l, 1, w_ffn_up, w_ffn_down, tm, tf, final_w=final_norm_w if l == DEPTH - 1 else None)
    return x, (jnp.stack(out_shift), jnp.stack(out_wkv), jnp.stack(out_gla), jnp.stack(out_conv),
               jnp.stack(out_lru))


def kernel(x_prompt, x_sample, c_prompt, c_sample, state_rwkv_shift, state_rwkv_wkv, state_gla, state_conv, state_lru, w_ada, b_ada, w_ffn_up, w_ffn_down, w_in_even, w_out_even, a_mu, a_w_up, a_a_up, a_g_up, a_vec, b_alpha_up, b_alpha_b, b_norm_w, w_in_odd, w_out_odd, c_conv_w, c_conv_b, c_wa, c_ba, c_wx, c_bx, c_lam, final_norm_w):
    to_mxu = lambda w: w.astype(BF16)
    wts = (to_mxu(w_ffn_up), to_mxu(w_ffn_down), to_mxu(w_in_even), to_mxu(w_out_even), a_mu, a_w_up, a_a_up,
           a_g_up, a_vec, b_alpha_up, b_alpha_b, b_norm_w, to_mxu(w_in_odd), to_mxu(w_out_odd), c_conv_w,
           c_conv_b, c_wa, c_ba, c_wx, c_bx, c_lam, final_norm_w)
    n_p, t_p, _ = x_prompt.shape
    n_s, t_s, _ = x_sample.shape
    assert t_s == 1

    c_rows = n_p + n_s
    pad = (-c_rows) % 16
    c_all = jnp.concatenate([c_sample, c_prompt, jnp.zeros((pad, D_MODEL), F32)], axis=0)
    mod_all = _ada(c_all, w_ada, b_ada)
    mod_p = _Mod(mod_all[:, :, n_s:c_rows].reshape(DEPTH, N_MOD, n_p, 1, D_MODEL), False, t_p)
    mod_s = _Mod(mod_all, True, 1)

    fresh = lambda s: jnp.zeros((s.shape[0], n_p) + s.shape[2:], s.dtype)
    y_p, st_p = _trunk(x_prompt.reshape(n_p * t_p, D_MODEL), mod_p, n_p, t_p,
                       fresh(state_rwkv_shift), fresh(state_rwkv_wkv), fresh(state_gla),
                       fresh(state_conv), fresh(state_lru), wts, 1024, 512)
    y_s, st_s = _trunk(x_sample.reshape(n_s, D_MODEL), mod_s, n_s, 1,
                       state_rwkv_shift, state_rwkv_wkv, state_gla, state_conv, state_lru, wts, n_s, 512)
    return (y_p.reshape(n_p, t_p, D_MODEL), y_s.reshape(n_s, 1, D_MODEL), *st_p, *st_s)
```

```python
import functools

import jax
import jax.numpy as jnp
from jax import lax
from jax.experimental import pallas as pl
from jax.experimental.pallas import tpu as pltpu

F32 = jnp.float32
BF16 = jnp.bfloat16

D_MODEL = 2048
DEPTH = 4
N_MOD = 9
D_FF = 5632
NORM_EPS = 1e-6

A_HEADS = 16
A_HEAD_DIM = 64
A_WIDTH = A_HEADS * A_HEAD_DIM
A_DECAY_RANK = 64
A_ICL_RANK = 64
A_GATE_RANK = 128
A_PROJ = 3 * A_WIDTH + A_DECAY_RANK + A_ICL_RANK + A_GATE_RANK
A_GN_EPS = 64e-5

B_HEADS = 4
B_KEY_DIM = 128
B_VAL_DIM = 256
B_KEY_WIDTH = B_HEADS * B_KEY_DIM
B_WIDTH = B_HEADS * B_VAL_DIM
B_ALPHA_RANK = 16
B_TAU = 16.0
B_PROJ = 2 * B_KEY_WIDTH + 2 * B_WIDTH + B_ALPHA_RANK
EVEN_PROJ = A_PROJ + B_PROJ

C_WIDTH = D_MODEL
C_BLOCKS = 8
C_BLOCK = C_WIDTH // C_BLOCKS
C_CONV = 4
C_POW = 8.0

CHUNK = 64
LANES = 128
VMEM_LIMIT = 56 * 1024 * 1024


def _params(*sem):
    return pltpu.CompilerParams(dimension_semantics=sem, vmem_limit_bytes=VMEM_LIMIT)


def _bdot(a, b):
    return jnp.dot(a.astype(BF16), b.astype(BF16), preferred_element_type=F32)


def _split3(x):
    hi = x.astype(BF16)
    r1 = x - hi.astype(F32)
    mid = r1.astype(BF16)
    lo = (r1 - mid.astype(F32)).astype(BF16)
    return hi, mid, lo


def _dot_exact_rhs(a_bf16, x):
    hi, mid, lo = _split3(x)
    f = lambda y: jnp.dot(a_bf16, y, preferred_element_type=F32)
    return f(hi) + f(mid) + f(lo)


def _dot_exact_lhs(x, b_bf16):
    hi, mid, lo = _split3(x)
    f = lambda y: jnp.dot(y, b_bf16, preferred_element_type=F32)
    return f(hi) + f(mid) + f(lo)


def _sigmoid(x):
    return jax.nn.sigmoid(x)


def _softplus(x):
    return jnp.maximum(x, 0.0) + jnp.log1p(jnp.exp(-jnp.abs(x)))


def _gelu_tanh(x):
    return 0.5 * x * (1.0 + jnp.tanh(0.7978845608028654 * (x + 0.044715 * (x * x * x))))


def _norm_mod(x, shift, scale):
    ms = jnp.mean(x * x, axis=-1, keepdims=True)
    return x * lax.rsqrt(ms + NORM_EPS) * (1.0 + scale) + shift


def _ada_kernel(c_ref, w_ref, b_ref, o_ref):
    c = c_ref[...]
    act = (c * _sigmoid(c)).astype(BF16)
    o_ref[...] = jnp.dot(act, w_ref[...].astype(BF16), preferred_element_type=F32) + b_ref[...]


def _ada(c_all, w_ada, b_ada):
    rows = c_all.shape[0]
    tn = 1024
    per = D_MODEL // tn
    b4 = b_ada.reshape(DEPTH, N_MOD * per, 1, tn)
    return pl.pallas_call(
        _ada_kernel,
        grid=(DEPTH, N_MOD * per),
        in_specs=[
            pl.BlockSpec((rows, D_MODEL), lambda l, j: (0, 0)),
            pl.BlockSpec((None, D_MODEL, tn), lambda l, j: (l, 0, j)),
            pl.BlockSpec((None, None, 1, tn), lambda l, j: (l, j, 0, 0)),
        ],
        out_specs=pl.BlockSpec((None, None, rows, tn), lambda l, j: (l, j // per, 0, j % per)),
        out_shape=jax.ShapeDtypeStruct((DEPTH, N_MOD, rows, D_MODEL), F32),
        compiler_params=_params("arbitrary", "arbitrary"),
        name="ada",
    )(c_all, w_ada, b4)


class _Mod:
    def __init__(self, arr, per_row, seq_len):
        self.arr = arr
        self.per_row = per_row
        self.seq_len = seq_len

    def spec(self, layer, m, tm):
        if self.per_row:
            return pl.BlockSpec((None, None, tm, D_MODEL), lambda i, *_: (layer, m, i, 0))
        per_seq = self.seq_len // tm
        return pl.BlockSpec((None, None, None, 1, D_MODEL), lambda i, *_: (layer, m, i // per_seq, 0, 0))


def _ffn_kernel(x_ref, sh_ref, sc_ref, g_ref, wg_ref, wu_ref, wd_ref, *rest, n_j, sub, final):
    if final:
        fw_ref, o_ref, h_ref = rest
    else:
        o_ref, h_ref = rest
    j = pl.program_id(1)
    tm = x_ref.shape[0]

    @pl.when(j == 0)
    def _():
        def body(s, carry):
            rows = pl.ds(pl.multiple_of(s * sub, sub), sub)
            sh = sh_ref[...] if sh_ref.shape[0] == 1 else sh_ref[rows, :]
            sc = sc_ref[...] if sc_ref.shape[0] == 1 else sc_ref[rows, :]
            h_ref[rows, :] = _norm_mod(x_ref[rows, :], sh, sc).astype(BF16)
            o_ref[rows, :] = jnp.zeros((sub, D_MODEL), F32)
            return carry
        lax.fori_loop(0, tm // sub, body, 0)

    h = h_ref[...]
    gt = jnp.dot(h, wg_ref[...], preferred_element_type=F32)
    ut = jnp.dot(h, wu_ref[...], preferred_element_type=F32)
    act = (gt * _sigmoid(gt) * ut).astype(BF16)
    o_ref[...] += jnp.dot(act, wd_ref[...], preferred_element_type=F32)

    @pl.when(j == n_j - 1)
    def _():
        def body(s, carry):
            rows = pl.ds(pl.multiple_of(s * sub, sub), sub)
            g = g_ref[...] if g_ref.shape[0] == 1 else g_ref[rows, :]
            y = x_ref[rows, :] + 0.5 * g * o_ref[rows, :]
            if final:
                ms = jnp.mean(y * y, axis=-1, keepdims=True)
                y = y * lax.rsqrt(ms + NORM_EPS) * fw_ref[...]
            o_ref[rows, :] = y
            return carry
        lax.fori_loop(0, tm // sub, body, 0)


def _ffn(x, mod, layer, which, w_up, w_down, tm, tf, final_w=None):
    m_rows = x.shape[0]
    n_j = D_FF // tf
    m0 = 6 * which
    sub = min(tm, 128)
    final = final_w is not None
    in_specs = [
        pl.BlockSpec((tm, D_MODEL), lambda i, j: (i, 0), pipeline_mode=pl.Buffered(1)),
        mod.spec(layer, m0, tm), mod.spec(layer, m0 + 1, tm), mod.spec(layer, m0 + 2, tm),
        pl.BlockSpec((None, None, D_MODEL, tf), lambda i, j: (layer, which, 0, j)),
        pl.BlockSpec((None, None, D_MODEL, tf), lambda i, j: (layer, which, 0, j + n_j)),
        pl.BlockSpec((None, None, tf, D_MODEL), lambda i, j: (layer, which, j, 0)),
    ]
    args = [x, mod.arr, mod.arr, mod.arr, w_up, w_up, w_down]
    if final:
        in_specs.append(pl.BlockSpec((1, D_MODEL), lambda i, j: (0, 0)))
        args.append(final_w.reshape(1, D_MODEL))
    return pl.pallas_call(
        functools.partial(_ffn_kernel, n_j=n_j, sub=sub, final=final),
        grid=(m_rows // tm, n_j),
        in_specs=in_specs,
        out_specs=pl.BlockSpec((tm, D_MODEL), lambda i, j: (i, 0)),
        out_shape=jax.ShapeDtypeStruct((m_rows, D_MODEL), F32),
        scratch_shapes=[pltpu.VMEM((tm, D_MODEL), BF16)],
        compiler_params=_params("arbitrary", "arbitrary"),
        name="ffn",
    )(*args)


def _inproj_kernel(x_ref, sh_ref, sc_ref, w_ref, o_ref, h_ref, *, sub):
    j = pl.program_id(1)
    tm = x_ref.shape[0]

    @pl.when(j == 0)
    def _():
        def body(s, carry):
            rows = pl.ds(pl.multiple_of(s * sub, sub), sub)
            sh = sh_ref[...] if sh_ref.shape[0] == 1 else sh_ref[rows, :]
            sc = sc_ref[...] if sc_ref.shape[0] == 1 else sc_ref[rows, :]
            h_ref[rows, :] = _norm_mod(x_ref[rows, :], sh, sc).astype(BF16)
            return carry
        lax.fori_loop(0, tm // sub, body, 0)

    o_ref[...] = jnp.dot(h_ref[...], w_ref[...], preferred_element_type=F32)


def _inproj(x, mod, layer, w, widx, tm, tn):
    m_rows = x.shape[0]
    n_out = w.shape[-1]
    sub = min(tm, 128)
    return pl.pallas_call(
        functools.partial(_inproj_kernel, sub=sub),
        grid=(m_rows // tm, pl.cdiv(n_out, tn)),
        in_specs=[
            pl.BlockSpec((tm, D_MODEL), lambda i, j: (i, 0)),
            mod.spec(layer, 3, tm), mod.spec(layer, 4, tm),
            pl.BlockSpec((None, D_MODEL, tn), lambda i, j: (widx, 0, j)),
        ],
        out_specs=pl.BlockSpec((tm, tn), lambda i, j: (i, j)),
        out_shape=jax.ShapeDtypeStruct((m_rows, n_out), F32),
        scratch_shapes=[pltpu.VMEM((tm, D_MODEL), BF16)],
        compiler_params=_params("arbitrary", "arbitrary"),
        name="inproj",
    )(x, mod.arr, mod.arr, w)


def _outproj_kernel(*refs, n_in):
    y_refs = refs[:n_in]
    w_refs = refs[n_in:2 * n_in]
    x_ref, g_ref, o_ref = refs[2 * n_in:]
    acc = _bdot(y_refs[0][...], w_refs[0][...])
    for y_ref, w_ref in zip(y_refs[1:], w_refs[1:]):
        acc = acc + _bdot(y_ref[...], w_ref[...])
    o_ref[...] = x_ref[...] + g_ref[...] * acc


def _outproj(ys, w, widx, x, mod, layer, tm):
    m_rows = x.shape[0]
    n_in = len(ys)
    kw = ys[0].shape[1]
    in_specs = [pl.BlockSpec((tm, kw), lambda i: (i, 0)) for _ in ys]
    in_specs += [pl.BlockSpec((None, kw, D_MODEL), lambda i, k=k: (widx, k, 0)) for k in range(n_in)]
    in_specs += [pl.BlockSpec((tm, D_MODEL), lambda i: (i, 0)), mod.spec(layer, 5, tm)]
    return pl.pallas_call(
        functools.partial(_outproj_kernel, n_in=n_in),
        grid=(m_rows // tm,),
        in_specs=in_specs,
        out_specs=pl.BlockSpec((tm, D_MODEL), lambda i: (i, 0)),
        out_shape=jax.ShapeDtypeStruct((m_rows, D_MODEL), F32),
        compiler_params=_params("arbitrary"),
        name="outproj",
    )(*ys, *([w] * n_in), x, mod.arr)


def _rwkv_prep_kernel(u_ref, prev_ref, mu_ref, wup_ref, aup_ref, gup_ref, vec_ref, *rest, seq_mode):
    if seq_mode:
        r_ref, lw_ref, k_ref, v_ref, kk_ref, a_ref, g_ref, carry_ref = rest
    else:
        seg_ref, r_ref, lw_ref, k_ref, v_ref, kk_ref, a_ref, g_ref, bonus_ref = rest
    u = u_ref[...]
    tm = u.shape[0]
    if seq_mode:
        @pl.when(pl.program_id(1) == 0)
        def _():
            carry_ref[...] = prev_ref[...]
        rolled = pltpu.roll(u, 1, axis=0)
        row = lax.broadcasted_iota(jnp.int32, u.shape, 0)
        prev = jnp.where(row == 0, carry_ref[...], rolled)
        carry_ref[...] = u[tm - 1:tm, :]
    else:
        prev = prev_ref[...]
    xs = u + mu_ref[...] * (prev - u)
    r = xs[:, 0:A_WIDTH]
    k = xs[:, A_WIDTH:2 * A_WIDTH]
    v = xs[:, 2 * A_WIDTH:3 * A_WIDTH]
    wa_lo = xs[:, 3 * A_WIDTH:3 * A_WIDTH + LANES]
    g_lo = xs[:, 3 * A_WIDTH + LANES:]
    w0, a0, k_k, k_a, r_k = (vec_ref[i:i + 1, :] for i in range(5))
    w_raw = -_softplus(-(w0 + _bdot(jnp.tanh(wa_lo), wup_ref[...]))) - 0.5
    lw = -jnp.exp(w_raw)
    a = _sigmoid(a0 + _bdot(wa_lo, aup_ref[...]))
    g = _bdot(_sigmoid(g_lo), gup_ref[...])
    kk = k * k_k
    k2 = k * (1.0 + (a - 1.0) * k_a)
    if not seq_mode:
        seg = seg_ref[...]
        kk = kk * lax.rsqrt(jnp.maximum(_dot_exact_lhs(kk * kk, seg), 1e-24))
        bonus_ref[...] = _dot_exact_lhs(r * k2 * r_k, seg) * v
    r_ref[...] = r
    lw_ref[...] = lw
    k_ref[...] = k2
    v_ref[...] = v
    kk_ref[...] = kk
    a_ref[...] = a
    g_ref[...] = g


def _rwkv_prep(u, prev, mu, wup_pad, aup_pad, g_up, vec, seg, n_seq, tm, seq_mode):
    rows = u.shape[0]
    per_seq = rows // n_seq // tm if seq_mode else 1
    grid = (n_seq, per_seq) if seq_mode else (rows // tm, 1)
    rowmap = (lambda b, t: (b * per_seq + t, 0)) if seq_mode else (lambda b, t: (b, 0))
    prev_spec = (pl.BlockSpec((None, 1, A_PROJ), lambda b, t: (b, 0, 0)) if seq_mode
                 else pl.BlockSpec((tm, A_PROJ), rowmap))
    const = lambda shape: pl.BlockSpec(shape, lambda b, t: (0,) * len(shape))
    out = jax.ShapeDtypeStruct((rows, A_WIDTH), F32)
    in_specs = [pl.BlockSpec((tm, A_PROJ), rowmap), prev_spec,
                const((1, A_PROJ)), const((LANES, A_WIDTH)), const((LANES, A_WIDTH)),
                const((A_GATE_RANK, A_WIDTH)), const((8, A_WIDTH))]
    args = [u, prev, mu, wup_pad, aup_pad, g_up, vec]
    if not seq_mode:
        in_specs.append(const((A_WIDTH, A_WIDTH)))
        args.append(seg)
    n_out = 7 if seq_mode else 8
    return pl.pallas_call(
        functools.partial(_rwkv_prep_kernel, seq_mode=seq_mode),
        grid=grid,
        in_specs=in_specs,
        out_specs=[pl.BlockSpec((tm, A_WIDTH), rowmap)] * n_out,
        out_shape=[out] * n_out,
        scratch_shapes=[pltpu.VMEM((1, A_PROJ), F32)] if seq_mode else [],
        compiler_params=_params("arbitrary", "arbitrary"),
        name="rwkv_prep",
    )(*args)


def _rwkv_post_kernel(y_ref, bonus_ref, g_ref, vec_ref, seg_ref, o_ref):
    y = y_ref[...]
    seg = seg_ref[...]
    inv_n = 1.0 / A_HEAD_DIM
    yc = y - _dot_exact_lhs(y, seg) * inv_n
    var = _dot_exact_lhs(yc * yc, seg) * inv_n
    yn = yc * lax.rsqrt(var + A_GN_EPS) * vec_ref[5:6, :] + vec_ref[6:7, :]
    o_ref[...] = (yn + bonus_ref[...]) * g_ref[...]


def _rwkv_post(y, bonus, g, vec, seg, tm):
    rows = y.shape[0]
    spec = pl.BlockSpec((tm, A_WIDTH), lambda i: (i, 0))
    return pl.pallas_call(
        _rwkv_post_kernel,
        grid=(rows // tm,),
        in_specs=[spec, spec, spec,
                  pl.BlockSpec((8, A_WIDTH), lambda i: (0, 0)),
                  pl.BlockSpec((A_WIDTH, A_WIDTH), lambda i: (0, 0))],
        out_specs=spec,
        out_shape=jax.ShapeDtypeStruct((rows, A_WIDTH), F32),
        compiler_params=_params("arbitrary"),
        name="rwkv_post",
    )(y, bonus, g, vec, seg)


def _rwkv_chunk_kernel(r_ref, lw_ref, k_ref, v_ref, kk_ref, a_ref, g_ref, vec_ref, h0_ref, y_ref, hT_ref, h_ref):
    ci = pl.program_id(1)

    @pl.when(ci == 0)
    def _():
        h_ref[...] = h0_ref[...]

    cs = r_ref.shape[0]
    n2 = 2 * cs
    npair = A_WIDTH // LANES
    pairs = lambda ref: jnp.stack([ref[:, p * LANES:(p + 1) * LANES] for p in range(npair)])
    vrow = lambda i: jnp.stack([vec_ref[i:i + 1, p * LANES:(p + 1) * LANES] for p in range(npair)])
    lane = lax.broadcasted_iota(jnp.int32, (1, 1, LANES), 2)
    m0 = (lane < A_HEAD_DIM).astype(F32)
    m1 = 1.0 - m0
    split = lambda x: jnp.concatenate([x * m0, x * m1], axis=1)
    dup = lambda x: jnp.concatenate([x, x], axis=1)
    own = jnp.concatenate([jnp.broadcast_to(m0, (1, cs, LANES)), jnp.broadcast_to(m1, (1, cs, LANES))], axis=1)

    def bmm(x, y):
        return jnp.einsum('hab,hbc->hac', x.astype(BF16), y.astype(BF16), preferred_element_type=F32)

    def bmm_nt(x, y):
        return jnp.einsum('han,hbn->hab', x.astype(BF16), y.astype(BF16), preferred_element_type=F32)

    def bmm_tn(x, y):
        return jnp.einsum('hca,hcb->hab', x.astype(BF16), y.astype(BF16), preferred_element_type=F32)

    lw = pairs(lw_ref)
    row = lax.broadcasted_iota(jnp.int32, (cs, cs), 0)
    col = lax.broadcasted_iota(jnp.int32, (cs, cs), 1)
    tri = jnp.broadcast_to((col <= row).astype(BF16)[None], (npair, cs, cs))
    hi, mid, lo = _split3(lw)
    csum = lambda part: jnp.einsum('hab,hbc->hac', tri, part, preferred_element_type=F32)
    c = csum(hi) + csum(mid) + csum(lo)
    c_end = c[:, cs - 1:cs, :]
    e_pos = dup(jnp.exp(c))
    e_neg = dup(jnp.exp(-c))
    e_prev = dup(jnp.exp(c - lw))
    e_end = dup(jnp.exp(c_end - c))
    g_end = jnp.exp(c_end)

    kk = split(pairs(kk_ref))
    kk = kk * lax.rsqrt(jnp.maximum(jnp.sum(kk * kk, axis=-1, keepdims=True), 1e-24))
    b_vec = kk * dup(pairs(a_ref))
    r2 = split(pairs(r_ref))
    k2 = split(pairs(k_ref))
    v2 = split(pairs(v_ref))
    bonus = jnp.sum(r2 * k2 * vrow(4), axis=-1, keepdims=True) * v2
    rt = r2 * e_pos
    at = -kk * e_prev
    bt = b_vec * e_neg
    kt = k2 * e_neg
    bh = b_vec * e_end
    kh = k2 * e_end

    row2 = lax.broadcasted_iota(jnp.int32, (n2, n2), 0)
    col2 = lax.broadcasted_iota(jnp.int32, (n2, n2), 1)
    same = (row2 >= cs) == (col2 >= cs)
    tok_r = jnp.where(row2 >= cs, row2 - cs, row2)
    tok_c = jnp.where(col2 >= cs, col2 - cs, col2)
    strict = jnp.where(same, (tok_c < tok_r).astype(F32), 0.0)
    incl = jnp.where(same, (tok_c <= tok_r).astype(F32), 0.0)
    eye2 = (row2 == col2).astype(F32)
    nrow = lax.broadcasted_iota(jnp.int32, (LANES, LANES), 0)
    ncol = lax.broadcasted_iota(jnp.int32, (LANES, LANES), 1)
    eye_n = (nrow == ncol).astype(F32)

    a_ab = bmm_nt(at, bt) * strict
    a_ak = bmm_nt(at, kt) * strict
    m_b = bmm_nt(rt, bt) * incl
    m_k = bmm_nt(rt, kt) * incl

    tinv = eye2 + a_ab
    apow = a_ab
    span = 1
    while 2 * span < cs:
        apow = bmm(apow, apow)
        tinv = tinv + bmm(tinv, apow)
        span *= 2

    akv = bmm(a_ak, v2)
    pq = bmm(tinv, jnp.concatenate([at, akv], axis=-1))
    mpq = bmm(m_b, pq)
    p2 = rt + mpq[..., :LANES]
    y0 = mpq[..., LANES:] + bmm(m_k, v2)
    tpq = bmm_tn(bh, pq)
    gm = eye_n * g_end + tpq[..., :LANES]
    hadd = tpq[..., LANES:] + bmm_tn(kh, v2)

    h = h_ref[...]
    y2 = bmm(p2, h) + y0
    h_new = bmm(gm, h) + hadd
    h_ref[...] = h_new

    inv_n = 1.0 / A_HEAD_DIM
    yc = (y2 - jnp.sum(y2, axis=-1, keepdims=True) * inv_n) * own
    var = jnp.sum(yc * yc, axis=-1, keepdims=True) * inv_n
    z = yc * lax.rsqrt(var + A_GN_EPS) * vrow(5) + vrow(6) * own + bonus
    out = (z[:, :cs] + z[:, cs:]) * pairs(g_ref)
    for p in range(npair):
        y_ref[:, p * LANES:(p + 1) * LANES] = out[p]

    @pl.when(ci == pl.num_programs(1) - 1)
    def _():
        hT_ref[...] = h_new


def _rwkv_chunk(r, lw, k, v, kk, a, g, vec, n_seq, h0):
    rows = r.shape[0]
    nc = rows // n_seq // CHUNK
    npair = A_WIDTH // LANES
    spec = pl.BlockSpec((CHUNK, A_WIDTH), lambda b, c: (b * nc + c, 0))
    hspec = pl.BlockSpec((None, npair, LANES, LANES), lambda b, c: (b, 0, 0, 0))
    return pl.pallas_call(
        _rwkv_chunk_kernel,
        grid=(n_seq, nc),
        in_specs=[spec] * 7 + [pl.BlockSpec((8, A_WIDTH), lambda b, c: (0, 0)), hspec],
        out_specs=[spec, hspec],
        out_shape=[jax.ShapeDtypeStruct((rows, A_WIDTH), F32),
                   jax.ShapeDtypeStruct((n_seq, npair, LANES, LANES), F32)],
        scratch_shapes=[pltpu.VMEM((npair, LANES, LANES), F32)],
        compiler_params=_params("arbitrary", "arbitrary"),
        name="rwkv_chunk",
    )(r, lw, k, v, kk, a, g, vec, h0)


def _pair_states(s):
    n = s.shape[0]
    nd = A_HEAD_DIM
    h = jnp.swapaxes(s, -1, -2).reshape(n, A_HEADS // 2, 2, nd, nd)
    out = jnp.zeros((n, A_HEADS // 2, 2 * nd, 2 * nd), s.dtype)
    return out.at[:, :, :nd, :nd].set(h[:, :, 0]).at[:, :, nd:, nd:].set(h[:, :, 1])


def _unpair_states(hp):
    n = hp.shape[0]
    nd = A_HEAD_DIM
    h = jnp.stack([hp[:, :, :nd, :nd], hp[:, :, nd:, nd:]], axis=2).reshape(n, A_HEADS, nd, nd)
    return jnp.swapaxes(h, -1, -2)


def _rwkv_step_kernel(r_ref, lw_ref, k_ref, v_ref, kk_ref, a_ref, s_ref, y_ref, so_ref):
    nb = r_ref.shape[0]
    nd = A_HEAD_DIM
    eye = (lax.broadcasted_iota(jnp.int32, (nd, nd), 0) == lax.broadcasted_iota(jnp.int32, (nd, nd), 1))
    eye = jnp.broadcast_to(eye.astype(BF16)[None], (A_HEADS, nd, nd))
    heads = lambda x: jnp.stack([x[:, h * nd:(h + 1) * nd] for h in range(A_HEADS)])

    def body(n, carry):
        row = pl.ds(n, 1)
        r = r_ref[row, :]
        v_hi, v_mid, v_lo = (p.astype(F32) for p in _split3(v_ref[row, :]))
        kk = kk_ref[row, :]
        pad = lambda *rows: jnp.concatenate(rows + (jnp.zeros((8 - len(rows), A_WIDTH), F32),), axis=0)
        s = s_ref[n]
        x8 = heads(pad(-kk, v_hi, v_mid, v_lo)).astype(BF16)
        cols = jnp.einsum('hij,hkj->hik', jnp.concatenate([s.astype(BF16), eye], axis=1), x8,
                          preferred_element_type=F32)
        s_kk = cols[:, :nd, 0:1]
        v_col = cols[:, nd:, 1:2] + cols[:, nd:, 2:3] + cols[:, nd:, 3:4]
        s_new = (s * heads(jnp.exp(lw_ref[row, :])) + s_kk * heads(kk * a_ref[row, :])
                 + v_col * heads(k_ref[row, :]))
        so_ref[n] = s_new
        y = jnp.einsum('hkj,hij->hki', heads(pad(r)).astype(BF16), s_new.astype(BF16),
                       preferred_element_type=F32)
        y_ref[row, :] = jnp.concatenate([y[h, 0:1, :] for h in range(A_HEADS)], axis=-1)
        return carry

    lax.fori_loop(0, nb, body, 0)


def _rwkv_step(r, lw, k, v, kk, a, states, j, nb):
    n = r.shape[0]
    rspec = pl.BlockSpec((nb, A_WIDTH), lambda i: (i, 0))
    return pl.pallas_call(
        _rwkv_step_kernel,
        grid=(n // nb,),
        in_specs=[rspec] * 6 + [pl.BlockSpec((None, nb, A_HEADS, A_HEAD_DIM, A_HEAD_DIM),
                                             lambda i: (j, i, 0, 0, 0))],
        out_specs=[rspec, pl.BlockSpec((nb, A_HEADS, A_HEAD_DIM, A_HEAD_DIM), lambda i: (i, 0, 0, 0))],
        out_shape=[jax.ShapeDtypeStruct((n, A_WIDTH), F32), jax.ShapeDtypeStruct(states.shape[1:], F32)],
        compiler_params=_params("arbitrary"),
        name="rwkv_step",
    )(r, lw, k, v, kk, a, states)


def _gla_log_alpha(a128, aup_ref, ab_ref):
    lane = lax.broadcasted_iota(jnp.int32, a128.shape, 1)
    a_lo = jnp.where(lane < B_ALPHA_RANK, a128, 0.0)
    x = _bdot(a_lo, aup_ref[...]) + ab_ref[...]
    return -_softplus(-x) * (1.0 / B_TAU)


def _gla_chunk_kernel(u_ref, aup_ref, ab_ref, nw_ref, s0_ref, o_ref, sT_ref, s_ref):
    ci = pl.program_id(1)

    @pl.when(ci == 0)
    def _():
        s_ref[...] = s0_ref[...]

    ub = u_ref[...]
    cs = ub.shape[0]
    q_all = ub[:, 0:B_KEY_WIDTH] * (B_KEY_DIM ** -0.5)
    k_all = ub[:, B_KEY_WIDTH:2 * B_KEY_WIDTH]
    v_all = ub[:, 2 * B_KEY_WIDTH:2 * B_KEY_WIDTH + B_WIDTH]
    g_all = ub[:, 2 * B_KEY_WIDTH + B_WIDTH:2 * B_KEY_WIDTH + 2 * B_WIDTH]
    a128 = ub[:, 2 * B_KEY_WIDTH + 2 * B_WIDTH:2 * B_KEY_WIDTH + 2 * B_WIDTH + LANES]
    log_a = _gla_log_alpha(a128, aup_ref, ab_ref)

    row = lax.broadcasted_iota(jnp.int32, (cs, cs), 0)
    col = lax.broadcasted_iota(jnp.int32, (cs, cs), 1)
    incl = col <= row
    cum = _dot_exact_rhs(incl.astype(BF16), log_a)
    e_pos = jnp.exp(cum)
    e_neg = jnp.exp(-cum)
    cum_end = cum[cs - 1:cs, :]
    e_end = jnp.exp(cum_end - cum)
    g_end = jnp.exp(cum_end)
    nk = B_KEY_DIM
    eye_k = (lax.broadcasted_iota(jnp.int32, (nk, nk), 0) == lax.broadcasted_iota(jnp.int32, (nk, nk), 1)).astype(F32)

    for h in range(B_HEADS):
        ks = slice(h * B_KEY_DIM, (h + 1) * B_KEY_DIM)
        vs = slice(h * B_VAL_DIM, (h + 1) * B_VAL_DIM)
        qe = q_all[:, ks] * e_pos[:, ks]
        kn = k_all[:, ks] * e_neg[:, ks]
        ke = k_all[:, ks] * e_end[:, ks]
        vh = v_all[:, vs]
        s = s_ref[h]
        scores = jnp.einsum('id,jd->ij', qe.astype(BF16), kn.astype(BF16), preferred_element_type=F32)
        scores = jnp.where(incl, scores, 0.0)
        o = _bdot(scores, vh) + _bdot(qe, s)
        g_col = jnp.sum(eye_k * g_end[:, ks], axis=-1, keepdims=True)
        s_ref[h] = s * g_col + jnp.einsum('jd,jv->dv', ke.astype(BF16), vh.astype(BF16),
                                           preferred_element_type=F32)
        ms = jnp.mean(o * o, axis=-1, keepdims=True)
        gh = g_all[:, vs]
        o_ref[:, vs] = o * lax.rsqrt(ms + NORM_EPS) * nw_ref[:, vs] * (gh * _sigmoid(gh))

    @pl.when(ci == pl.num_programs(1) - 1)
    def _():
        sT_ref[...] = s_ref[...]


def _gla_chunk(u, n_seq, alpha_up_pad, alpha_b, norm_w, s0):
    rows = u.shape[0]
    t = rows // n_seq
    nc = t // CHUNK
    return pl.pallas_call(
        _gla_chunk_kernel,
        grid=(n_seq, nc),
        in_specs=[
            pl.BlockSpec((CHUNK, A_PROJ), lambda b, c: (b * nc + c, 1)),
            pl.BlockSpec((LANES, B_KEY_WIDTH), lambda b, c: (0, 0)),
            pl.BlockSpec((1, B_KEY_WIDTH), lambda b, c: (0, 0)),
            pl.BlockSpec((1, B_WIDTH), lambda b, c: (0, 0)),
            pl.BlockSpec((None, B_HEADS, B_KEY_DIM, B_VAL_DIM), lambda b, c: (b, 0, 0, 0)),
        ],
        out_specs=[
            pl.BlockSpec((CHUNK, B_WIDTH), lambda b, c: (b * nc + c, 0)),
            pl.BlockSpec((None, B_HEADS, B_KEY_DIM, B_VAL_DIM), lambda b, c: (b, 0, 0, 0)),
        ],
        out_shape=[jax.ShapeDtypeStruct((rows, B_WIDTH), F32),
                   jax.ShapeDtypeStruct((n_seq, B_HEADS, B_KEY_DIM, B_VAL_DIM), F32)],
        scratch_shapes=[pltpu.VMEM((B_HEADS, B_KEY_DIM, B_VAL_DIM), F32)],
        compiler_params=_params("arbitrary", "arbitrary"),
        name="gla_chunk",
    )(u, alpha_up_pad, alpha_b, norm_w, s0)


def _gla_step_kernel(u_ref, aup_ref, ab_ref, nw_ref, s_ref, o_ref, so_ref, la_ref):
    nb = u_ref.shape[0]
    nk = B_KEY_DIM
    off_k, off_v, off_g = B_KEY_WIDTH, 2 * B_KEY_WIDTH, 2 * B_KEY_WIDTH + B_WIDTH
    off_a = 2 * B_KEY_WIDTH + 2 * B_WIDTH
    la_ref[...] = _gla_log_alpha(u_ref[:, off_a:off_a + LANES], aup_ref, ab_ref)
    eye = (lax.broadcasted_iota(jnp.int32, (nk, nk), 0) == lax.broadcasted_iota(jnp.int32, (nk, nk), 1)).astype(F32)
    col = lambda x: jnp.sum(eye * x, axis=-1, keepdims=True)

    def body(n, carry):
        row = pl.ds(n, 1)
        ur = u_ref[row, :]
        dec_all = jnp.exp(la_ref[row, :])
        outs = []
        for h in range(B_HEADS):
            q = ur[:, h * nk:(h + 1) * nk] * (B_KEY_DIM ** -0.5)
            k = ur[:, off_k + h * nk:off_k + (h + 1) * nk]
            v = ur[:, off_v + h * B_VAL_DIM:off_v + (h + 1) * B_VAL_DIM]
            g = ur[:, off_g + h * B_VAL_DIM:off_g + (h + 1) * B_VAL_DIM]
            dec = dec_all[:, h * nk:(h + 1) * nk]
            s = s_ref[n, h]
            qk = jnp.sum(q * k, axis=-1, keepdims=True)
            o = qk * v + jnp.sum(col(q * dec) * s, axis=0, keepdims=True)
            so_ref[n, h] = s * col(dec) + col(k) * v
            ms = jnp.mean(o * o, axis=-1, keepdims=True)
            outs.append(o * lax.rsqrt(ms + NORM_EPS) * nw_ref[:, h * B_VAL_DIM:(h + 1) * B_VAL_DIM]
                        * (g * _sigmoid(g)))
        o_ref[row, :] = jnp.concatenate(outs, axis=-1)
        return carry

    lax.fori_loop(0, nb, body, 0)


def _gla_step(u, alpha_up_pad, alpha_b, norm_w, states, j, nb):
    n = u.shape[0]
    return pl.pallas_call(
        _gla_step_kernel,
        grid=(n // nb,),
        in_specs=[pl.BlockSpec((nb, A_PROJ), lambda i: (i, 1)),
                  pl.BlockSpec((LANES, B_KEY_WIDTH), lambda i: (0, 0)),
                  pl.BlockSpec((1, B_KEY_WIDTH), lambda i: (0, 0)),
                  pl.BlockSpec((1, B_WIDTH), lambda i: (0, 0)),
                  pl.BlockSpec((None, nb, B_HEADS, B_KEY_DIM, B_VAL_DIM), lambda i: (j, i, 0, 0, 0))],
        out_specs=[pl.BlockSpec((nb, B_WIDTH), lambda i: (i, 0)),
                   pl.BlockSpec((nb, B_HEADS, B_KEY_DIM, B_VAL_DIM), lambda i: (i, 0, 0, 0))],
        out_shape=[jax.ShapeDtypeStruct((n, B_WIDTH), F32), jax.ShapeDtypeStruct(states.shape[1:], F32)],
        scratch_shapes=[pltpu.VMEM((nb, B_KEY_WIDTH), F32)],
        compiler_params=_params("arbitrary"),
        name="gla_step",
    )(u, alpha_up_pad, alpha_b, norm_w, states)


def _lru_gates(xc, wa_ref, wx_ref, ba_ref, bx_ref, lam_ref):
    ra, ix = [], []
    for nb in range(C_BLOCKS):
        xb = xc[:, nb * C_BLOCK:(nb + 1) * C_BLOCK].astype(BF16)
        ra.append(jnp.dot(xb, wa_ref[nb].astype(BF16), preferred_element_type=F32))
        ix.append(jnp.dot(xb, wx_ref[nb].astype(BF16), preferred_element_type=F32))
    r = _sigmoid(jnp.concatenate(ra, axis=-1) + ba_ref[...])
    i_g = _sigmoid(jnp.concatenate(ix, axis=-1) + bx_ref[...])
    log_a = -C_POW * r * _softplus(-lam_ref[...])
    a = jnp.exp(log_a)
    b = jnp.sqrt(1.0 - a * a) * (i_g * xc)
    return a, b


def _lru_seq_kernel(u_ref, cp_ref, h0_ref, cw_ref, cb_ref, wa_ref, wx_ref, ba_ref, bx_ref, lam_ref,
                    y_ref, tail_ref, hT_ref, carry_ref, h_ref, a_s, b_s):
    ti = pl.program_id(1)

    @pl.when(ti == 0)
    def _():
        carry_ref[...] = cp_ref[...]
        h_ref[...] = h0_ref[...]

    gate = u_ref[:, 0:C_WIDTH]
    xb = u_ref[:, C_WIDTH:2 * C_WIDTH]
    tm = xb.shape[0]
    carry = carry_ref[...]
    row8 = lax.broadcasted_iota(jnp.int32, (8, C_WIDTH), 0)
    xc = cb_ref[...] + cw_ref[C_CONV - 1:C_CONV, :] * xb
    for s in range(1, C_CONV):
        rolled = pltpu.roll(xb, s, axis=0)
        head = jnp.where(row8 < s, pltpu.roll(carry, s, axis=0), rolled[0:8, :])
        shifted = jnp.concatenate([head, rolled[8:, :]], axis=0)
        xc = xc + cw_ref[C_CONV - 1 - s:C_CONV - s, :] * shifted
    carry_ref[...] = xb[tm - 8:tm, :]
    tail_ref[...] = xb[tm - 8:tm, :]

    a, b = _lru_gates(xc, wa_ref, wx_ref, ba_ref, bx_ref, lam_ref)
    a_s[...] = a
    b_s[...] = b

    def body(t, h):
        h = a_s[pl.ds(t, 1), :] * h + b_s[pl.ds(t, 1), :]
        b_s[pl.ds(t, 1), :] = h
        return h

    h_last = lax.fori_loop(0, tm, body, h_ref[...], unroll=8)
    h_ref[...] = h_last
    hT_ref[...] = h_last
    y_ref[...] = _gelu_tanh(gate) * b_s[...]


def _lru_seq(u, n_seq, conv_prev8, h0, conv_w, conv_b, wa, wx, ba, bx, lam, tm):
    rows = u.shape[0]
    per_seq = rows // n_seq // tm
    const = lambda shape: pl.BlockSpec(shape, lambda b, t: (0,) * len(shape))
    return pl.pallas_call(
        _lru_seq_kernel,
        grid=(n_seq, per_seq),
        in_specs=[
            pl.BlockSpec((tm, 2 * C_WIDTH), lambda b, t: (b * per_seq + t, 0)),
            pl.BlockSpec((None, 8, C_WIDTH), lambda b, t: (b, 0, 0)),
            pl.BlockSpec((None, 1, C_WIDTH), lambda b, t: (b, 0, 0)),
            const((C_CONV, C_WIDTH)), const((1, C_WIDTH)),
            const((C_BLOCKS, C_BLOCK, C_BLOCK)), const((C_BLOCKS, C_BLOCK, C_BLOCK)),
            const((1, C_WIDTH)), const((1, C_WIDTH)), const((1, C_WIDTH)),
        ],
        out_specs=[
            pl.BlockSpec((tm, C_WIDTH), lambda b, t: (b * per_seq + t, 0)),
            pl.BlockSpec((None, 8, C_WIDTH), lambda b, t: (b, 0, 0)),
            pl.BlockSpec((None, 1, C_WIDTH), lambda b, t: (b, 0, 0)),
        ],
        out_shape=[jax.ShapeDtypeStruct((rows, C_WIDTH), F32),
                   jax.ShapeDtypeStruct((n_seq, 8, C_WIDTH), F32),
                   jax.ShapeDtypeStruct((n_seq, 1, C_WIDTH), F32)],
        scratch_shapes=[pltpu.VMEM((8, C_WIDTH), F32), pltpu.VMEM((1, C_WIDTH), F32),
                        pltpu.VMEM((tm, C_WIDTH), F32), pltpu.VMEM((tm, C_WIDTH), F32)],
        compiler_params=_params("arbitrary", "arbitrary"),
        name="lru_seq",
    )(u, conv_prev8, h0, conv_w, conv_b, wa, wx, ba, bx, lam)


def _lru_step_kernel(u_ref, cp_ref, h0_ref, cw_ref, cb_ref, wa_ref, wx_ref, ba_ref, bx_ref, lam_ref,
                     y_ref, cn_ref, h_ref):
    gate = u_ref[:, 0:C_WIDTH]
    xb = u_ref[:, C_WIDTH:2 * C_WIDTH]
    xc = cb_ref[...] + cw_ref[C_CONV - 1:C_CONV, :] * xb
    for i in range(C_CONV - 1):
        xc = xc + cw_ref[i:i + 1, :] * cp_ref[:, i * C_WIDTH:(i + 1) * C_WIDTH]
    a, b = _lru_gates(xc, wa_ref, wx_ref, ba_ref, bx_ref, lam_ref)
    h = a * h0_ref[...] + b
    h_ref[...] = h
    y_ref[...] = _gelu_tanh(gate) * h
    for i in range(C_CONV - 2):
        cn_ref[:, i * C_WIDTH:(i + 1) * C_WIDTH] = cp_ref[:, (i + 1) * C_WIDTH:(i + 2) * C_WIDTH]
    cn_ref[:, (C_CONV - 2) * C_WIDTH:(C_CONV - 1) * C_WIDTH] = xb


def _lru_step(u, conv_prev, h0, conv_w, conv_b, wa, wx, ba, bx, lam):
    rows = u.shape[0]
    full = lambda shape: pl.BlockSpec(shape, lambda i: (0,) * len(shape))
    cw = (C_CONV - 1) * C_WIDTH
    return pl.pallas_call(
        _lru_step_kernel,
        grid=(1,),
        in_specs=[full((rows, 2 * C_WIDTH)), full((rows, cw)), full((rows, C_WIDTH)),
                  full((C_CONV, C_WIDTH)), full((1, C_WIDTH)),
                  full((C_BLOCKS, C_BLOCK, C_BLOCK)), full((C_BLOCKS, C_BLOCK, C_BLOCK)),
                  full((1, C_WIDTH)), full((1, C_WIDTH)), full((1, C_WIDTH))],
        out_specs=[full((rows, C_WIDTH)), full((rows, cw)), full((rows, C_WIDTH))],
        out_shape=[jax.ShapeDtypeStruct((rows, C_WIDTH), F32),
                   jax.ShapeDtypeStruct((rows, cw), F32),
                   jax.ShapeDtypeStruct((rows, C_WIDTH), F32)],
        compiler_params=_params("arbitrary"),
        name="lru_step",
    )(u, conv_prev, h0, conv_w, conv_b, wa, wx, ba, bx, lam)


def _pad_rows(w, first, total):
    return jnp.zeros((total, w.shape[1]), w.dtype).at[first:first + w.shape[0]].set(w)


def _trunk(x, mod, n_seq, seq_len, st_shift, st_wkv, st_gla, st_conv, st_lru, wts, tm, tf):
    (w_ffn_up, w_ffn_down, w_in_even, w_out_even, a_mu, a_w_up, a_a_up, a_g_up, a_vec, b_alpha_up,
     b_alpha_b, b_norm_w, w_in_odd, w_out_odd, c_conv_w, c_conv_b, c_wa, c_ba, c_wx, c_bx, c_lam,
     final_norm_w) = wts
    rows = x.shape[0]
    prompt = seq_len > 1
    tn_even, tn_odd = 13 * LANES, 8 * LANES
    seg = jnp.kron(jnp.eye(A_HEADS, dtype=F32), jnp.ones((A_HEAD_DIM, A_HEAD_DIM), F32)).astype(BF16)
    out_shift, out_wkv, out_gla, out_conv, out_lru = [], [], [], [], []
    for l in range(DEPTH):
        j = l // 2
        x = _ffn(x, mod, l, 0, w_ffn_up, w_ffn_down, tm, tf)
        if l % 2 == 0:
            u = _inproj(x, mod, l, w_in_even, j, tm, tn_even)
            wup_pad = _pad_rows(a_w_up[j], 0, LANES)
            aup_pad = _pad_rows(a_a_up[j], A_DECAY_RANK, LANES)
            vec8 = _pad_rows(a_vec[j], 0, 8)
            alpha_pad = _pad_rows(b_alpha_up[j], 0, LANES)
            alpha_b = b_alpha_b[j].reshape(1, B_KEY_WIDTH)
            mu = a_mu[j].reshape(1, A_PROJ)
            norm_w = b_norm_w[j].reshape(1, B_WIDTH)
            if prompt:
                r, lw, k2, v, kk, al, g = _rwkv_prep(u, st_shift[j].reshape(n_seq, 1, A_PROJ), mu, wup_pad,
                                                     aup_pad, a_g_up[j], vec8, seg, n_seq, 256, True)
                y_a, h_t = _rwkv_chunk(r, lw, k2, v, kk, al, g, vec8, n_seq, _pair_states(st_wkv[j]))
                s_wkv = _unpair_states(h_t)
                s_shift = u.reshape(n_seq, seq_len, EVEN_PROJ)[:, -1, :A_PROJ]
                y_b, s_gla = _gla_chunk(u, n_seq, alpha_pad, alpha_b, norm_w, st_gla[j])
            else:
                r, lw, k2, v, kk, al, g, bonus = _rwkv_prep(u, st_shift[j], mu, wup_pad, aup_pad, a_g_up[j],
                                                            vec8, seg, rows, rows, False)
                y_raw, s_wkv = _rwkv_step(r, lw, k2, v, kk, al, st_wkv, j, 8)
                y_a = _rwkv_post(y_raw, bonus, g, vec8, seg, rows)
                s_shift = u[:, :A_PROJ]
                y_b, s_gla = _gla_step(u, alpha_pad, alpha_b, norm_w, st_gla, j, 8)
            x = _outproj([y_a, y_b], w_out_even, j, x, mod, l, min(tm, 512))
            out_shift.append(s_shift)
            out_wkv.append(s_wkv)
            out_gla.append(s_gla)
        else:
            u = _inproj(x, mod, l, w_in_odd, j, tm, tn_odd)
            lru_w = (c_conv_w[j], c_conv_b[j].reshape(1, C_WIDTH), c_wa[j], c_wx[j],
                     c_ba[j].reshape(1, C_WIDTH), c_bx[j].reshape(1, C_WIDTH), c_lam[j].reshape(1, C_WIDTH))
            if prompt:
                cp8 = jnp.concatenate([jnp.zeros((n_seq, 8 - (C_CONV - 1), C_WIDTH), F32), st_conv[j]], axis=1)
                y_c, tail, h_t = _lru_seq(u, n_seq, cp8, st_lru[j].reshape(n_seq, 1, C_WIDTH), *lru_w, 256)
                s_conv = tail[:, 8 - (C_CONV - 1):, :]
                s_lru = h_t.reshape(n_seq, C_WIDTH)
            else:
                y_c, cn, s_lru = _lru_step(u, st_conv[j].reshape(rows, (C_CONV - 1) * C_WIDTH), st_lru[j], *lru_w)
                s_conv = cn.reshape(rows, C_CONV - 1, C_WIDTH)
            x = _outproj([y_c], w_out_odd, j, x, mod, l, min(tm, 512))
            out_conv.append(s_conv)
            out_lru.append(s_lru)
        x = _ffn(x, mod, l, 1, w_ffn_up, w_ffn_down, tm, tf, final_w=final_norm_w if l == DEPTH - 1 else None)
    return x, (jnp.stack(out_shift), jnp.stack(out_wkv), jnp.stack(out_gla), jnp.stack(out_conv),
               jnp.stack(out_lru))


def kernel(x_prompt, x_sample, c_prompt, c_sample, state_rwkv_shift, state_rwkv_wkv, state_gla, state_conv, state_lru, w_ada, b_ada, w_ffn_up, w_ffn_down, w_in_even, w_out_even, a_mu, a_w_up, a_a_up, a_g_up, a_vec, b_alpha_up, b_alpha_b, b_norm_w, w_in_odd, w_out_odd, c_conv_w, c_conv_b, c_wa, c_ba, c_wx, c_bx, c_lam, final_norm_w):
    to_mxu = lambda w: w.astype(BF16)
    wts = (to_mxu(w_ffn_up), to_mxu(w_ffn_down), to_mxu(w_in_even), to_mxu(w_out_even), a_mu, a_w_up, a_a_up,
           a_g_up, a_vec, b_alpha_up, b_alpha_b, b_norm_w, to_mxu(w_in_odd), to_mxu(w_out_odd), c_conv_w,
           c_conv_b, c_wa, c_ba, c_wx, c_bx, c_lam, final_norm_w)
    n_p, t_p, _ = x_prompt.shape
    n_s, t_s, _ = x_sample.shape
    assert t_s == 1

    c_rows = n_p + n_s
    pad = (-c_rows) % 16
    c_all = jnp.concatenate([c_sample, c_prompt, jnp.zeros((pad, D_MODEL), F32)], axis=0)
    mod_all = _ada(c_all, w_ada, b_ada)
    mod_p = _Mod(mod_all[:, :, n_s:c_rows].reshape(DEPTH, N_MOD, n_p, 1, D_MODEL), False, t_p)
    mod_s = _Mod(mod_all, True, 1)

    fresh = lambda s: jnp.zeros((s.shape[0], n_p) + s.shape[2:], s.dtype)
    y_p, st_p = _trunk(x_prompt.reshape(n_p * t_p, D_MODEL), mod_p, n_p, t_p,
                       fresh(state_rwkv_shift), fresh(state_rwkv_wkv), fresh(state_gla),
                       fresh(state_conv), fresh(state_lru), wts, 1024, 512)
    y_s, st_s = _trunk(x_sample.reshape(n_s, D_MODEL), mod_s, n_s, 1,
                       state_rwkv_shift, state_rwkv_wkv, state_gla, state_conv, state_lru, wts, n_s, 512)
    return (y_p.reshape(n_p, t_p, D_MODEL), y_s.reshape(n_s, 1, D_MODEL), *st_p, *st_s)
```

```python
import functools

import jax
import jax.numpy as jnp
from jax import lax
from jax.experimental import pallas as pl
from jax.experimental.pallas import tpu as pltpu

F32 = jnp.float32
BF16 = jnp.bfloat16

D_MODEL = 2048
DEPTH = 4
N_MOD = 9
D_FF = 5632
NORM_EPS = 1e-6

A_HEADS = 16
A_HEAD_DIM = 64
A_WIDTH = A_HEADS * A_HEAD_DIM
A_DECAY_RANK = 64
A_ICL_RANK = 64
A_GATE_RANK = 128
A_PROJ = 3 * A_WIDTH + A_DECAY_RANK + A_ICL_RANK + A_GATE_RANK
A_GN_EPS = 64e-5

B_HEADS = 4
B_KEY_DIM = 128
B_VAL_DIM = 256
B_KEY_WIDTH = B_HEADS * B_KEY_DIM
B_WIDTH = B_HEADS * B_VAL_DIM
B_ALPHA_RANK = 16
B_TAU = 16.0
B_PROJ = 2 * B_KEY_WIDTH + 2 * B_WIDTH + B_ALPHA_RANK
EVEN_PROJ = A_PROJ + B_PROJ

C_WIDTH = D_MODEL
C_BLOCKS = 8
C_BLOCK = C_WIDTH // C_BLOCKS
C_CONV = 4
C_POW = 8.0

CHUNK = 64
GLA_SUB = 8
LANES = 128
VMEM_LIMIT = 56 * 1024 * 1024


def _params(*sem):
    return pltpu.CompilerParams(dimension_semantics=sem, vmem_limit_bytes=VMEM_LIMIT)


def _bdot(a, b):
    return jnp.dot(a.astype(BF16), b.astype(BF16), preferred_element_type=F32)


def _split3(x):
    hi = x.astype(BF16)
    r1 = x - hi.astype(F32)
    mid = r1.astype(BF16)
    lo = (r1 - mid.astype(F32)).astype(BF16)
    return hi, mid, lo


def _dot_exact_rhs(a_bf16, x):
    hi, mid, lo = _split3(x)
    f = lambda y: jnp.dot(a_bf16, y, preferred_element_type=F32)
    return f(hi) + f(mid) + f(lo)


def _dot_exact_lhs(x, b_bf16):
    hi, mid, lo = _split3(x)
    f = lambda y: jnp.dot(y, b_bf16, preferred_element_type=F32)
    return f(hi) + f(mid) + f(lo)


def _sigmoid(x):
    return jax.nn.sigmoid(x)


def _softplus(x):
    return jnp.maximum(x, 0.0) + jnp.log1p(jnp.exp(-jnp.abs(x)))


def _gelu_tanh(x):
    return 0.5 * x * (1.0 + jnp.tanh(0.7978845608028654 * (x + 0.044715 * (x * x * x))))


def _norm_mod(x, shift, scale):
    ms = jnp.mean(x * x, axis=-1, keepdims=True)
    return x * lax.rsqrt(ms + NORM_EPS) * (1.0 + scale) + shift


def _ada_kernel(c_ref, w_ref, b_ref, o_ref):
    c = c_ref[...]
    act = (c * _sigmoid(c)).astype(BF16)
    o_ref[...] = jnp.dot(act, w_ref[...].astype(BF16), preferred_element_type=F32) + b_ref[...]


def _ada(c_all, w_ada, b_ada):
    rows = c_all.shape[0]
    tn = 1024
    per = D_MODEL // tn
    b4 = b_ada.reshape(DEPTH, N_MOD * per, 1, tn)
    return pl.pallas_call(
        _ada_kernel,
        grid=(DEPTH, N_MOD * per),
        in_specs=[
            pl.BlockSpec((rows, D_MODEL), lambda l, j: (0, 0)),
            pl.BlockSpec((None, D_MODEL, tn), lambda l, j: (l, 0, j)),
            pl.BlockSpec((None, None, 1, tn), lambda l, j: (l, j, 0, 0)),
        ],
        out_specs=pl.BlockSpec((None, None, rows, tn), lambda l, j: (l, j // per, 0, j % per)),
        out_shape=jax.ShapeDtypeStruct((DEPTH, N_MOD, rows, D_MODEL), F32),
        compiler_params=_params("arbitrary", "arbitrary"),
        name="ada",
    )(c_all, w_ada, b4)


class _Mod:
    def __init__(self, arr, per_row, seq_len):
        self.arr = arr
        self.per_row = per_row
        self.seq_len = seq_len

    def spec(self, layer, m, tm):
        if self.per_row:
            return pl.BlockSpec((None, None, tm, D_MODEL), lambda i, *_: (layer, m, i, 0))
        per_seq = self.seq_len // tm
        return pl.BlockSpec((None, None, None, 1, D_MODEL), lambda i, *_: (layer, m, i // per_seq, 0, 0))


def _ffn_kernel(x_ref, sh_ref, sc_ref, g_ref, wg_ref, wu_ref, wd_ref, *rest, n_j, sub, final, emit):
    rest = list(rest)
    fw_ref = rest.pop(0) if final else None
    o_ref = rest.pop(0)
    w_outs = [rest.pop(0) for _ in range(3)] if emit else None
    h_ref, = rest
    j = pl.program_id(1)
    tm = x_ref.shape[0]

    @pl.when(j == 0)
    def _():
        def body(s, carry):
            rows = pl.ds(pl.multiple_of(s * sub, sub), sub)
            sh = sh_ref[...] if sh_ref.shape[0] == 1 else sh_ref[rows, :]
            sc = sc_ref[...] if sc_ref.shape[0] == 1 else sc_ref[rows, :]
            h_ref[rows, :] = _norm_mod(x_ref[rows, :], sh, sc).astype(BF16)
            o_ref[rows, :] = jnp.zeros((sub, D_MODEL), F32)
            return carry
        lax.fori_loop(0, tm // sub, body, 0)

    wg, wu, wd = wg_ref[...], wu_ref[...], wd_ref[...]
    if emit:
        wg, wu, wd = wg.astype(BF16), wu.astype(BF16), wd.astype(BF16)
        for w_out, w in zip(w_outs, (wg, wu, wd)):
            w_out[...] = w
    h = h_ref[...]
    gt = jnp.dot(h, wg, preferred_element_type=F32)
    ut = jnp.dot(h, wu, preferred_element_type=F32)
    act = (gt * _sigmoid(gt) * ut).astype(BF16)
    o_ref[...] += jnp.dot(act, wd, preferred_element_type=F32)

    @pl.when(j == n_j - 1)
    def _():
        def body(s, carry):
            rows = pl.ds(pl.multiple_of(s * sub, sub), sub)
            g = g_ref[...] if g_ref.shape[0] == 1 else g_ref[rows, :]
            y = x_ref[rows, :] + 0.5 * g * o_ref[rows, :]
            if final:
                ms = jnp.mean(y * y, axis=-1, keepdims=True)
                y = y * lax.rsqrt(ms + NORM_EPS) * fw_ref[...]
            o_ref[rows, :] = y
            return carry
        lax.fori_loop(0, tm // sub, body, 0)


def _ffn(x, mod, layer, which, weights, tm, tf, final_w=None):
    m_rows = x.shape[0]
    n_j = D_FF // tf
    m0 = 6 * which
    sub = min(tm, 128)
    final = final_w is not None
    emit = len(weights) == 2
    up_spec = pl.BlockSpec((D_MODEL, tf), lambda i, j: (0, j))
    down_spec = pl.BlockSpec((tf, D_MODEL), lambda i, j: (j, 0))
    in_specs = [
        pl.BlockSpec((tm, D_MODEL), lambda i, j: (i, 0), pipeline_mode=pl.Buffered(1)),
        mod.spec(layer, m0, tm), mod.spec(layer, m0 + 1, tm), mod.spec(layer, m0 + 2, tm),
    ]
    if emit:
        w_up, w_down = weights
        in_specs += [pl.BlockSpec((None, None, D_MODEL, tf), lambda i, j: (layer, which, 0, j)),
                     pl.BlockSpec((None, None, D_MODEL, tf), lambda i, j: (layer, which, 0, j + n_j)),
                     pl.BlockSpec((None, None, tf, D_MODEL), lambda i, j: (layer, which, j, 0))]
        args = [x, mod.arr, mod.arr, mod.arr, w_up, w_up, w_down]
    else:
        in_specs += [up_spec, up_spec, down_spec]
        args = [x, mod.arr, mod.arr, mod.arr, *weights]
    if final:
        in_specs.append(pl.BlockSpec((1, D_MODEL), lambda i, j: (0, 0)))
        args.append(final_w.reshape(1, D_MODEL))
    out_specs = [pl.BlockSpec((tm, D_MODEL), lambda i, j: (i, 0))]
    out_shape = [jax.ShapeDtypeStruct((m_rows, D_MODEL), F32)]
    if emit:
        assert m_rows == tm
        out_specs += [up_spec, up_spec, down_spec]
        out_shape += [jax.ShapeDtypeStruct((D_MODEL, D_FF), BF16)] * 2 + [jax.ShapeDtypeStruct((D_FF, D_MODEL), BF16)]
    out = pl.pallas_call(
        functools.partial(_ffn_kernel, n_j=n_j, sub=sub, final=final, emit=emit),
        grid=(m_rows // tm, n_j),
        in_specs=in_specs,
        out_specs=out_specs,
        out_shape=out_shape,
        scratch_shapes=[pltpu.VMEM((tm, D_MODEL), BF16)],
        compiler_params=_params("arbitrary", "arbitrary"),
        name="ffn",
    )(*args)
    return (out[0], tuple(out[1:])) if emit else out[0]


def _inproj_kernel(x_ref, sh_ref, sc_ref, w_ref, o_ref, *rest, sub, emit):
    h_ref = rest[-1]
    j = pl.program_id(1)
    tm = x_ref.shape[0]

    @pl.when(j == 0)
    def _():
        def body(s, carry):
            rows = pl.ds(pl.multiple_of(s * sub, sub), sub)
            sh = sh_ref[...] if sh_ref.shape[0] == 1 else sh_ref[rows, :]
            sc = sc_ref[...] if sc_ref.shape[0] == 1 else sc_ref[rows, :]
            h_ref[rows, :] = _norm_mod(x_ref[rows, :], sh, sc).astype(BF16)
            return carry
        lax.fori_loop(0, tm // sub, body, 0)

    w = w_ref[...]
    if emit:
        w = w.astype(BF16)
        rest[0][...] = w
    o_ref[...] = jnp.dot(h_ref[...], w, preferred_element_type=F32)


def _inproj(x, mod, layer, w, widx, tm, tn):
    m_rows = x.shape[0]
    n_out = w.shape[-1]
    sub = min(tm, 128)
    emit = widx is not None
    w_spec = (pl.BlockSpec((None, D_MODEL, tn), lambda i, j: (widx, 0, j)) if emit
              else pl.BlockSpec((D_MODEL, tn), lambda i, j: (0, j)))
    out_specs = [pl.BlockSpec((tm, tn), lambda i, j: (i, j))]
    out_shape = [jax.ShapeDtypeStruct((m_rows, n_out), F32)]
    if emit:
        assert m_rows == tm
        out_specs.append(pl.BlockSpec((D_MODEL, tn), lambda i, j: (0, j)))
        out_shape.append(jax.ShapeDtypeStruct((D_MODEL, n_out), BF16))
    out = pl.pallas_call(
        functools.partial(_inproj_kernel, sub=sub, emit=emit),
        grid=(m_rows // tm, pl.cdiv(n_out, tn)),
        in_specs=[
            pl.BlockSpec((tm, D_MODEL), lambda i, j: (i, 0)),
            mod.spec(layer, 3, tm), mod.spec(layer, 4, tm),
            w_spec,
        ],
        out_specs=out_specs,
        out_shape=out_shape,
        scratch_shapes=[pltpu.VMEM((tm, D_MODEL), BF16)],
        compiler_params=_params("arbitrary", "arbitrary"),
        name="inproj",
    )(x, mod.arr, mod.arr, w)
    return tuple(out) if emit else out[0]


def _outproj_kernel(*refs, n_in, emit):
    y_refs = refs[:n_in]
    w_refs = refs[n_in:2 * n_in]
    x_ref, g_ref, o_ref = refs[2 * n_in:2 * n_in + 3]
    acc = None
    for k, (y_ref, w_ref) in enumerate(zip(y_refs, w_refs)):
        w = w_ref[...]
        if emit:
            w = w.astype(BF16)
            refs[2 * n_in + 3 + k][...] = w
        part = jnp.dot(y_ref[...].astype(BF16), w, preferred_element_type=F32)
        acc = part if acc is None else acc + part
    o_ref[...] = x_ref[...] + g_ref[...] * acc


def _outproj(ys, w, widx, x, mod, layer, tm):
    m_rows = x.shape[0]
    n_in = len(ys)
    kw = ys[0].shape[1]
    emit = widx is not None
    in_specs = [pl.BlockSpec((tm, kw), lambda i: (i, 0)) for _ in ys]
    if emit:
        assert m_rows == tm
        in_specs += [pl.BlockSpec((None, kw, D_MODEL), lambda i, k=k: (widx, k, 0)) for k in range(n_in)]
        w_args = [w] * n_in
    else:
        in_specs += [pl.BlockSpec((kw, D_MODEL), lambda i: (0, 0))] * n_in
        w_args = list(w)
    in_specs += [pl.BlockSpec((tm, D_MODEL), lambda i: (i, 0)), mod.spec(layer, 5, tm)]
    out_specs = [pl.BlockSpec((tm, D_MODEL), lambda i: (i, 0))]
    out_shape = [jax.ShapeDtypeStruct((m_rows, D_MODEL), F32)]
    if emit:
        out_specs += [pl.BlockSpec((kw, D_MODEL), lambda i: (0, 0))] * n_in
        out_shape += [jax.ShapeDtypeStruct((kw, D_MODEL), BF16)] * n_in
    out = pl.pallas_call(
        functools.partial(_outproj_kernel, n_in=n_in, emit=emit),
        grid=(m_rows // tm,),
        in_specs=in_specs,
        out_specs=out_specs,
        out_shape=out_shape,
        compiler_params=_params("arbitrary"),
        name="outproj",
    )(*ys, *w_args, x, mod.arr)
    return (out[0], tuple(out[1:])) if emit else out[0]


def _rwkv_prep_kernel(u_ref, prev_ref, mu_ref, wup_ref, aup_ref, gup_ref, vec_ref, *rest, seq_mode):
    if seq_mode:
        r_ref, lw_ref, k_ref, v_ref, kk_ref, a_ref, g_ref, carry_ref = rest
    else:
        seg_ref, r_ref, lw_ref, k_ref, v_ref, kk_ref, a_ref, g_ref, bonus_ref = rest
    u = u_ref[...]
    tm = u.shape[0]
    if seq_mode:
        @pl.when(pl.program_id(1) == 0)
        def _():
            carry_ref[...] = prev_ref[...]
        rolled = pltpu.roll(u, 1, axis=0)
        row = lax.broadcasted_iota(jnp.int32, u.shape, 0)
        prev = jnp.where(row == 0, carry_ref[...], rolled)
        carry_ref[...] = u[tm - 1:tm, :]
    else:
        prev = prev_ref[...]
    xs = u + mu_ref[...] * (prev - u)
    r = xs[:, 0:A_WIDTH]
    k = xs[:, A_WIDTH:2 * A_WIDTH]
    v = xs[:, 2 * A_WIDTH:3 * A_WIDTH]
    wa_lo = xs[:, 3 * A_WIDTH:3 * A_WIDTH + LANES]
    g_lo = xs[:, 3 * A_WIDTH + LANES:]
    w0, a0, k_k, k_a, r_k = (vec_ref[i:i + 1, :] for i in range(5))
    w_raw = -_softplus(-(w0 + _bdot(jnp.tanh(wa_lo), wup_ref[...]))) - 0.5
    lw = -jnp.exp(w_raw)
    a = _sigmoid(a0 + _bdot(wa_lo, aup_ref[...]))
    g = _bdot(_sigmoid(g_lo), gup_ref[...])
    kk = k * k_k
    k2 = k * (1.0 + (a - 1.0) * k_a)
    if not seq_mode:
        seg = seg_ref[...]
        kk = kk * lax.rsqrt(jnp.maximum(_dot_exact_lhs(kk * kk, seg), 1e-24))
        bonus_ref[...] = _dot_exact_lhs(r * k2 * r_k, seg) * v
    r_ref[...] = r
    lw_ref[...] = lw
    k_ref[...] = k2
    v_ref[...] = v
    kk_ref[...] = kk
    a_ref[...] = a
    g_ref[...] = g


def _rwkv_prep(u, prev, mu, wup_pad, aup_pad, g_up, vec, seg, n_seq, tm, seq_mode):
    rows = u.shape[0]
    per_seq = rows // n_seq // tm if seq_mode else 1
    grid = (n_seq, per_seq) if seq_mode else (rows // tm, 1)
    rowmap = (lambda b, t: (b * per_seq + t, 0)) if seq_mode else (lambda b, t: (b, 0))
    prev_spec = (pl.BlockSpec((None, 1, A_PROJ), lambda b, t: (b, 0, 0)) if seq_mode
                 else pl.BlockSpec((tm, A_PROJ), rowmap))
    const = lambda shape: pl.BlockSpec(shape, lambda b, t: (0,) * len(shape))
    out = jax.ShapeDtypeStruct((rows, A_WIDTH), F32)
    in_specs = [pl.BlockSpec((tm, A_PROJ), rowmap), prev_spec,
                const((1, A_PROJ)), const((LANES, A_WIDTH)), const((LANES, A_WIDTH)),
                const((A_GATE_RANK, A_WIDTH)), const((8, A_WIDTH))]
    args = [u, prev, mu, wup_pad, aup_pad, g_up, vec]
    if not seq_mode:
        in_specs.append(const((A_WIDTH, A_WIDTH)))
        args.append(seg)
    n_out = 7 if seq_mode else 8
    return pl.pallas_call(
        functools.partial(_rwkv_prep_kernel, seq_mode=seq_mode),
        grid=grid,
        in_specs=in_specs,
        out_specs=[pl.BlockSpec((tm, A_WIDTH), rowmap)] * n_out,
        out_shape=[out] * n_out,
        scratch_shapes=[pltpu.VMEM((1, A_PROJ), F32)] if seq_mode else [],
        compiler_params=_params("arbitrary", "arbitrary"),
        name="rwkv_prep",
    )(*args)


def _rwkv_post_kernel(y_ref, bonus_ref, g_ref, vec_ref, seg_ref, o_ref):
    y = y_ref[...]
    seg = seg_ref[...]
    inv_n = 1.0 / A_HEAD_DIM
    yc = y - _dot_exact_lhs(y, seg) * inv_n
    var = _dot_exact_lhs(yc * yc, seg) * inv_n
    yn = yc * lax.rsqrt(var + A_GN_EPS) * vec_ref[5:6, :] + vec_ref[6:7, :]
    o_ref[...] = (yn + bonus_ref[...]) * g_ref[...]


def _rwkv_post(y, bonus, g, vec, seg, tm):
    rows = y.shape[0]
    spec = pl.BlockSpec((tm, A_WIDTH), lambda i: (i, 0))
    return pl.pallas_call(
        _rwkv_post_kernel,
        grid=(rows // tm,),
        in_specs=[spec, spec, spec,
                  pl.BlockSpec((8, A_WIDTH), lambda i: (0, 0)),
                  pl.BlockSpec((A_WIDTH, A_WIDTH), lambda i: (0, 0))],
        out_specs=spec,
        out_shape=jax.ShapeDtypeStruct((rows, A_WIDTH), F32),
        compiler_params=_params("arbitrary"),
        name="rwkv_post",
    )(y, bonus, g, vec, seg)


def _rwkv_chunk_kernel(r_ref, lw_ref, k_ref, v_ref, kk_ref, a_ref, g_ref, vec_ref, h0_ref, y_ref, hT_ref, h_ref):
    ci = pl.program_id(1)

    @pl.when(ci == 0)
    def _():
        h_ref[...] = h0_ref[...]

    cs = r_ref.shape[0]
    n2 = 2 * cs
    npair = A_WIDTH // LANES
    pairs = lambda ref: jnp.stack([ref[:, p * LANES:(p + 1) * LANES] for p in range(npair)])
    vrow = lambda i: jnp.stack([vec_ref[i:i + 1, p * LANES:(p + 1) * LANES] for p in range(npair)])
    lane = lax.broadcasted_iota(jnp.int32, (1, 1, LANES), 2)
    m0 = (lane < A_HEAD_DIM).astype(F32)
    m1 = 1.0 - m0
    split = lambda x: jnp.concatenate([x * m0, x * m1], axis=1)
    dup = lambda x: jnp.concatenate([x, x], axis=1)
    own = jnp.concatenate([jnp.broadcast_to(m0, (1, cs, LANES)), jnp.broadcast_to(m1, (1, cs, LANES))], axis=1)

    def bmm(x, y):
        return jnp.einsum('hab,hbc->hac', x.astype(BF16), y.astype(BF16), preferred_element_type=F32)

    def bmm_nt(x, y):
        return jnp.einsum('han,hbn->hab', x.astype(BF16), y.astype(BF16), preferred_element_type=F32)

    def bmm_tn(x, y):
        return jnp.einsum('hca,hcb->hab', x.astype(BF16), y.astype(BF16), preferred_element_type=F32)

    lw = pairs(lw_ref)
    row = lax.broadcasted_iota(jnp.int32, (cs, cs), 0)
    col = lax.broadcasted_iota(jnp.int32, (cs, cs), 1)
    tri = jnp.broadcast_to((col <= row).astype(BF16)[None], (npair, cs, cs))
    hi, mid, lo = _split3(lw)
    csum = lambda part: jnp.einsum('hab,hbc->hac', tri, part, preferred_element_type=F32)
    c = csum(hi) + csum(mid) + csum(lo)
    c_end = c[:, cs - 1:cs, :]
    e_pos = dup(jnp.exp(c))
    e_neg = dup(jnp.exp(-c))
    e_prev = dup(jnp.exp(c - lw))
    e_end = dup(jnp.exp(c_end - c))
    g_end = jnp.exp(c_end)

    kk = split(pairs(kk_ref))
    kk = kk * lax.rsqrt(jnp.maximum(jnp.sum(kk * kk, axis=-1, keepdims=True), 1e-24))
    b_vec = kk * dup(pairs(a_ref))
    r2 = split(pairs(r_ref))
    k2 = split(pairs(k_ref))
    v2 = split(pairs(v_ref))
    bonus = jnp.sum(r2 * k2 * vrow(4), axis=-1, keepdims=True) * v2
    rt = r2 * e_pos
    at = -kk * e_prev
    bt = b_vec * e_neg
    kt = k2 * e_neg
    bh = b_vec * e_end
    kh = k2 * e_end

    row2 = lax.broadcasted_iota(jnp.int32, (n2, n2), 0)
    col2 = lax.broadcasted_iota(jnp.int32, (n2, n2), 1)
    same = (row2 >= cs) == (col2 >= cs)
    tok_r = jnp.where(row2 >= cs, row2 - cs, row2)
    tok_c = jnp.where(col2 >= cs, col2 - cs, col2)
    strict = jnp.where(same, (tok_c < tok_r).astype(F32), 0.0)
    incl = jnp.where(same, (tok_c <= tok_r).astype(F32), 0.0)
    eye2 = (row2 == col2).astype(F32)
    nrow = lax.broadcasted_iota(jnp.int32, (LANES, LANES), 0)
    ncol = lax.broadcasted_iota(jnp.int32, (LANES, LANES), 1)
    eye_n = (nrow == ncol).astype(F32)

    a_ab = bmm_nt(at, bt) * strict
    a_ak = bmm_nt(at, kt) * strict
    m_b = bmm_nt(rt, bt) * incl
    m_k = bmm_nt(rt, kt) * incl

    tinv = eye2 + a_ab
    apow = a_ab
    span = 1
    while 2 * span < cs:
        apow = bmm(apow, apow)
        tinv = tinv + bmm(tinv, apow)
        span *= 2

    akv = bmm(a_ak, v2)
    pq = bmm(tinv, jnp.concatenate([at, akv], axis=-1))
    mpq = bmm(m_b, pq)
    p2 = rt + mpq[..., :LANES]
    y0 = mpq[..., LANES:] + bmm(m_k, v2)
    tpq = bmm_tn(bh, pq)
    gm = eye_n * g_end + tpq[..., :LANES]
    hadd = tpq[..., LANES:] + bmm_tn(kh, v2)

    h = h_ref[...]
    y2 = bmm(p2, h) + y0
    h_new = bmm(gm, h) + hadd
    h_ref[...] = h_new

    inv_n = 1.0 / A_HEAD_DIM
    yc = (y2 - jnp.sum(y2, axis=-1, keepdims=True) * inv_n) * own
    var = jnp.sum(yc * yc, axis=-1, keepdims=True) * inv_n
    z = yc * lax.rsqrt(var + A_GN_EPS) * vrow(5) + vrow(6) * own + bonus
    out = (z[:, :cs] + z[:, cs:]) * pairs(g_ref)
    for p in range(npair):
        y_ref[:, p * LANES:(p + 1) * LANES] = out[p]

    @pl.when(ci == pl.num_programs(1) - 1)
    def _():
        hT_ref[...] = h_new


def _rwkv_chunk(r, lw, k, v, kk, a, g, vec, n_seq, h0):
    rows = r.shape[0]
    nc = rows // n_seq // CHUNK
    npair = A_WIDTH // LANES
    spec = pl.BlockSpec((CHUNK, A_WIDTH), lambda b, c: (b * nc + c, 0))
    hspec = pl.BlockSpec((None, npair, LANES, LANES), lambda b, c: (b, 0, 0, 0))
    return pl.pallas_call(
        _rwkv_chunk_kernel,
        grid=(n_seq, nc),
        in_specs=[spec] * 7 + [pl.BlockSpec((8, A_WIDTH), lambda b, c: (0, 0)), hspec],
        out_specs=[spec, hspec],
        out_shape=[jax.ShapeDtypeStruct((rows, A_WIDTH), F32),
                   jax.ShapeDtypeStruct((n_seq, npair, LANES, LANES), F32)],
        scratch_shapes=[pltpu.VMEM((npair, LANES, LANES), F32)],
        compiler_params=_params("arbitrary", "arbitrary"),
        name="rwkv_chunk",
    )(r, lw, k, v, kk, a, g, vec, h0)


def _pair_states(s):
    n = s.shape[0]
    nd = A_HEAD_DIM
    h = jnp.swapaxes(s, -1, -2).reshape(n, A_HEADS // 2, 2, nd, nd)
    out = jnp.zeros((n, A_HEADS // 2, 2 * nd, 2 * nd), s.dtype)
    return out.at[:, :, :nd, :nd].set(h[:, :, 0]).at[:, :, nd:, nd:].set(h[:, :, 1])


def _unpair_states(hp):
    n = hp.shape[0]
    nd = A_HEAD_DIM
    h = jnp.stack([hp[:, :, :nd, :nd], hp[:, :, nd:, nd:]], axis=2).reshape(n, A_HEADS, nd, nd)
    return jnp.swapaxes(h, -1, -2)


def _rwkv_step_kernel(r_ref, lw_ref, k_ref, v_ref, kk_ref, a_ref, s_ref, y_ref, so_ref):
    nb = r_ref.shape[0]
    nd = A_HEAD_DIM
    eye = (lax.broadcasted_iota(jnp.int32, (nd, nd), 0) == lax.broadcasted_iota(jnp.int32, (nd, nd), 1))
    eye = jnp.broadcast_to(eye.astype(BF16)[None], (A_HEADS, nd, nd))
    heads = lambda x: jnp.stack([x[:, h * nd:(h + 1) * nd] for h in range(A_HEADS)])

    def body(n, carry):
        row = pl.ds(n, 1)
        r = r_ref[row, :]
        v_hi, v_mid, v_lo = (p.astype(F32) for p in _split3(v_ref[row, :]))
        kk = kk_ref[row, :]
        pad = lambda *rows: jnp.concatenate(rows + (jnp.zeros((8 - len(rows), A_WIDTH), F32),), axis=0)
        s = s_ref[n]
        x8 = heads(pad(-kk, v_hi, v_mid, v_lo)).astype(BF16)
        cols = jnp.einsum('hij,hkj->hik', jnp.concatenate([s.astype(BF16), eye], axis=1), x8,
                          preferred_element_type=F32)
        s_kk = cols[:, :nd, 0:1]
        v_col = cols[:, nd:, 1:2] + cols[:, nd:, 2:3] + cols[:, nd:, 3:4]
        s_new = (s * heads(jnp.exp(lw_ref[row, :])) + s_kk * heads(kk * a_ref[row, :])
                 + v_col * heads(k_ref[row, :]))
        so_ref[n] = s_new
        y = jnp.einsum('hkj,hij->hki', heads(pad(r)).astype(BF16), s_new.astype(BF16),
                       preferred_element_type=F32)
        y_ref[row, :] = jnp.concatenate([y[h, 0:1, :] for h in range(A_HEADS)], axis=-1)
        return carry

    lax.fori_loop(0, nb, body, 0)


def _rwkv_step(r, lw, k, v, kk, a, states, j, nb):
    n = r.shape[0]
    rspec = pl.BlockSpec((nb, A_WIDTH), lambda i: (i, 0))
    return pl.pallas_call(
        _rwkv_step_kernel,
        grid=(n // nb,),
        in_specs=[rspec] * 6 + [pl.BlockSpec((None, nb, A_HEADS, A_HEAD_DIM, A_HEAD_DIM),
                                             lambda i: (j, i, 0, 0, 0))],
        out_specs=[rspec, pl.BlockSpec((nb, A_HEADS, A_HEAD_DIM, A_HEAD_DIM), lambda i: (i, 0, 0, 0))],
        out_shape=[jax.ShapeDtypeStruct((n, A_WIDTH), F32), jax.ShapeDtypeStruct(states.shape[1:], F32)],
        compiler_params=_params("arbitrary"),
        name="rwkv_step",
    )(r, lw, k, v, kk, a, states)


def _gla_log_alpha(a128, aup_ref, ab_ref):
    lane = lax.broadcasted_iota(jnp.int32, a128.shape, 1)
    a_lo = jnp.where(lane < B_ALPHA_RANK, a128, 0.0)
    x = _bdot(a_lo, aup_ref[...]) + ab_ref[...]
    return -_softplus(-x) * (1.0 / B_TAU)


def _gla_chunk_kernel(u_ref, aup_ref, ab_ref, nw_ref, s0_ref, o_ref, sT_ref, s_ref):
    ci = pl.program_id(1)

    @pl.when(ci == 0)
    def _():
        s_ref[...] = s0_ref[...]

    ub = u_ref[...]
    cs = ub.shape[0]
    q_all = ub[:, 0:B_KEY_WIDTH] * (B_KEY_DIM ** -0.5)
    k_all = ub[:, B_KEY_WIDTH:2 * B_KEY_WIDTH]
    v_all = ub[:, 2 * B_KEY_WIDTH:2 * B_KEY_WIDTH + B_WIDTH]
    g_all = ub[:, 2 * B_KEY_WIDTH + B_WIDTH:2 * B_KEY_WIDTH + 2 * B_WIDTH]
    a128 = ub[:, 2 * B_KEY_WIDTH + 2 * B_WIDTH:2 * B_KEY_WIDTH + 2 * B_WIDTH + LANES]
    log_a = _gla_log_alpha(a128, aup_ref, ab_ref)

    row = lax.broadcasted_iota(jnp.int32, (cs, cs), 0)
    col = lax.broadcasted_iota(jnp.int32, (cs, cs), 1)
    cum = _dot_exact_rhs((col <= row).astype(BF16), log_a)
    e_pos = jnp.exp(cum)
    cum_end = cum[cs - 1:cs, :]
    e_end = jnp.exp(cum_end - cum)
    g_end = jnp.exp(cum_end)
    nk = B_KEY_DIM
    eye_k = (lax.broadcasted_iota(jnp.int32, (nk, nk), 0) == lax.broadcasted_iota(jnp.int32, (nk, nk), 1)).astype(F32)

    nblk = cs // GLA_SUB
    kw = B_KEY_WIDTH
    cum3 = cum.reshape(nblk, GLA_SUB, kw)
    before = jnp.concatenate([jnp.zeros((1, 1, kw), F32), cum3[:nblk - 1, GLA_SUB - 1:GLA_SUB, :]], axis=0)
    q_rel = q_all * jnp.exp(cum - jnp.broadcast_to(before, (nblk, GLA_SUB, kw)).reshape(cs, kw))
    tok = lax.broadcasted_iota(jnp.int32, (cs, 1), 0)
    q_parts, k_parts = [], []
    for blk in range(1, nblk):
        first = blk * GLA_SUB
        k_parts.append(jnp.where(tok < first, k_all * jnp.exp(jnp.minimum(before[blk] - cum, 0.0)), 0.0))
        q_parts.append(jnp.where((tok >= first) & (tok < first + GLA_SUB), q_rel, 0.0))

    q3 = q_all.reshape(nblk, GLA_SUB, kw)
    k3 = k_all.reshape(nblk, GLA_SUB, kw)
    v3 = v_all.reshape(nblk, GLA_SUB, B_WIDTH)
    sub_i = lax.broadcasted_iota(jnp.int32, (1, GLA_SUB, 1), 1)
    o_diag = [jnp.zeros((nblk, GLA_SUB, B_VAL_DIM), F32) for _ in range(B_HEADS)]
    for jj in range(GLA_SUB):
        pair = q3 * k3[:, jj:jj + 1, :] * jnp.exp(jnp.minimum(cum3 - cum3[:, jj:jj + 1, :], 0.0))
        pair = jnp.where(sub_i >= jj, pair, 0.0)
        for h in range(B_HEADS):
            score = jnp.sum(pair[:, :, h * nk:(h + 1) * nk], axis=-1, keepdims=True)
            o_diag[h] = o_diag[h] + score * v3[:, jj:jj + 1, h * B_VAL_DIM:(h + 1) * B_VAL_DIM]

    for h in range(B_HEADS):
        ks = slice(h * B_KEY_DIM, (h + 1) * B_KEY_DIM)
        vs = slice(h * B_VAL_DIM, (h + 1) * B_VAL_DIM)
        qe = q_all[:, ks] * e_pos[:, ks]
        ke = k_all[:, ks] * e_end[:, ks]
        vh = v_all[:, vs]
        s = s_ref[h]
        q_cat = jnp.concatenate([part[:, ks] for part in q_parts], axis=-1).astype(BF16)
        k_cat = jnp.concatenate([part[:, ks] for part in k_parts], axis=-1).astype(BF16)
        scores = jnp.einsum('id,jd->ij', q_cat, k_cat, preferred_element_type=F32)
        o = _bdot(scores, vh) + _bdot(qe, s) + o_diag[h].reshape(cs, B_VAL_DIM)
        g_col = jnp.sum(eye_k * g_end[:, ks], axis=-1, keepdims=True)
        s_ref[h] = s * g_col + jnp.einsum('jd,jv->dv', ke.astype(BF16), vh.astype(BF16),
                                           preferred_element_type=F32)
        ms = jnp.mean(o * o, axis=-1, keepdims=True)
        gh = g_all[:, vs]
        o_ref[:, vs] = o * lax.rsqrt(ms + NORM_EPS) * nw_ref[:, vs] * (gh * _sigmoid(gh))

    @pl.when(ci == pl.num_programs(1) - 1)
    def _():
        sT_ref[...] = s_ref[...]


def _gla_chunk(u, n_seq, alpha_up_pad, alpha_b, norm_w, s0):
    rows = u.shape[0]
    t = rows // n_seq
    nc = t // CHUNK
    return pl.pallas_call(
        _gla_chunk_kernel,
        grid=(n_seq, nc),
        in_specs=[
            pl.BlockSpec((CHUNK, A_PROJ), lambda b, c: (b * nc + c, 1)),
            pl.BlockSpec((LANES, B_KEY_WIDTH), lambda b, c: (0, 0)),
            pl.BlockSpec((1, B_KEY_WIDTH), lambda b, c: (0, 0)),
            pl.BlockSpec((1, B_WIDTH), lambda b, c: (0, 0)),
            pl.BlockSpec((None, B_HEADS, B_KEY_DIM, B_VAL_DIM), lambda b, c: (b, 0, 0, 0)),
        ],
        out_specs=[
            pl.BlockSpec((CHUNK, B_WIDTH), lambda b, c: (b * nc + c, 0)),
            pl.BlockSpec((None, B_HEADS, B_KEY_DIM, B_VAL_DIM), lambda b, c: (b, 0, 0, 0)),
        ],
        out_shape=[jax.ShapeDtypeStruct((rows, B_WIDTH), F32),
                   jax.ShapeDtypeStruct((n_seq, B_HEADS, B_KEY_DIM, B_VAL_DIM), F32)],
        scratch_shapes=[pltpu.VMEM((B_HEADS, B_KEY_DIM, B_VAL_DIM), F32)],
        compiler_params=_params("arbitrary", "arbitrary"),
        name="gla_chunk",
    )(u, alpha_up_pad, alpha_b, norm_w, s0)


def _gla_step_kernel(u_ref, aup_ref, ab_ref, nw_ref, s_ref, o_ref, so_ref, la_ref):
    nb = u_ref.shape[0]
    nk = B_KEY_DIM
    off_k, off_v, off_g = B_KEY_WIDTH, 2 * B_KEY_WIDTH, 2 * B_KEY_WIDTH + B_WIDTH
    off_a = 2 * B_KEY_WIDTH + 2 * B_WIDTH
    la_ref[...] = _gla_log_alpha(u_ref[:, off_a:off_a + LANES], aup_ref, ab_ref)
    eye = (lax.broadcasted_iota(jnp.int32, (nk, nk), 0) == lax.broadcasted_iota(jnp.int32, (nk, nk), 1)).astype(F32)
    col = lambda x: jnp.sum(eye * x, axis=-1, keepdims=True)

    def body(n, carry):
        row = pl.ds(n, 1)
        ur = u_ref[row, :]
        dec_all = jnp.exp(la_ref[row, :])
        outs = []
        for h in range(B_HEADS):
            q = ur[:, h * nk:(h + 1) * nk] * (B_KEY_DIM ** -0.5)
            k = ur[:, off_k + h * nk:off_k + (h + 1) * nk]
            v = ur[:, off_v + h * B_VAL_DIM:off_v + (h + 1) * B_VAL_DIM]
            g = ur[:, off_g + h * B_VAL_DIM:off_g + (h + 1) * B_VAL_DIM]
            dec = dec_all[:, h * nk:(h + 1) * nk]
            s = s_ref[n, h]
            qk = jnp.sum(q * k, axis=-1, keepdims=True)
            o = qk * v + jnp.sum(col(q * dec) * s, axis=0, keepdims=True)
            so_ref[n, h] = s * col(dec) + col(k) * v
            ms = jnp.mean(o * o, axis=-1, keepdims=True)
            outs.append(o * lax.rsqrt(ms + NORM_EPS) * nw_ref[:, h * B_VAL_DIM:(h + 1) * B_VAL_DIM]
                        * (g * _sigmoid(g)))
        o_ref[row, :] = jnp.concatenate(outs, axis=-1)
        return carry

    lax.fori_loop(0, nb, body, 0)


def _gla_step(u, alpha_up_pad, alpha_b, norm_w, states, j, nb):
    n = u.shape[0]
    return pl.pallas_call(
        _gla_step_kernel,
        grid=(n // nb,),
        in_specs=[pl.BlockSpec((nb, A_PROJ), lambda i: (i, 1)),
                  pl.BlockSpec((LANES, B_KEY_WIDTH), lambda i: (0, 0)),
                  pl.BlockSpec((1, B_KEY_WIDTH), lambda i: (0, 0)),
                  pl.BlockSpec((1, B_WIDTH), lambda i: (0, 0)),
                  pl.BlockSpec((None, nb, B_HEADS, B_KEY_DIM, B_VAL_DIM), lambda i: (j, i, 0, 0, 0))],
        out_specs=[pl.BlockSpec((nb, B_WIDTH), lambda i: (i, 0)),
                   pl.BlockSpec((nb, B_HEADS, B_KEY_DIM, B_VAL_DIM), lambda i: (i, 0, 0, 0))],
        out_shape=[jax.ShapeDtypeStruct((n, B_WIDTH), F32), jax.ShapeDtypeStruct(states.shape[1:], F32)],
        scratch_shapes=[pltpu.VMEM((nb, B_KEY_WIDTH), F32)],
        compiler_params=_params("arbitrary"),
        name="gla_step",
    )(u, alpha_up_pad, alpha_b, norm_w, states)


def _lru_gates(xc, wa_ref, wx_ref, ba_ref, bx_ref, lam_ref):
    ra, ix = [], []
    for nb in range(C_BLOCKS):
        xb = xc[:, nb * C_BLOCK:(nb + 1) * C_BLOCK].astype(BF16)
        ra.append(jnp.dot(xb, wa_ref[nb].astype(BF16), preferred_element_type=F32))
        ix.append(jnp.dot(xb, wx_ref[nb].astype(BF16), preferred_element_type=F32))
    r = _sigmoid(jnp.concatenate(ra, axis=-1) + ba_ref[...])
    i_g = _sigmoid(jnp.concatenate(ix, axis=-1) + bx_ref[...])
    log_a = -C_POW * r * _softplus(-lam_ref[...])
    a = jnp.exp(log_a)
    b = jnp.sqrt(1.0 - a * a) * (i_g * xc)
    return a, b


def _lru_seq_kernel(u_ref, cp_ref, h0_ref, cw_ref, cb_ref, wa_ref, wx_ref, ba_ref, bx_ref, lam_ref,
                    y_ref, tail_ref, hT_ref, carry_ref, h_ref, a_s, b_s):
    ti = pl.program_id(1)

    @pl.when(ti == 0)
    def _():
        carry_ref[...] = cp_ref[...]
        h_ref[...] = h0_ref[...]

    gate = u_ref[:, 0:C_WIDTH]
    xb = u_ref[:, C_WIDTH:2 * C_WIDTH]
    tm = xb.shape[0]
    carry = carry_ref[...]
    row8 = lax.broadcasted_iota(jnp.int32, (8, C_WIDTH), 0)
    xc = cb_ref[...] + cw_ref[C_CONV - 1:C_CONV, :] * xb
    for s in range(1, C_CONV):
        rolled = pltpu.roll(xb, s, axis=0)
        head = jnp.where(row8 < s, pltpu.roll(carry, s, axis=0), rolled[0:8, :])
        shifted = jnp.concatenate([head, rolled[8:, :]], axis=0)
        xc = xc + cw_ref[C_CONV - 1 - s:C_CONV - s, :] * shifted
    carry_ref[...] = xb[tm - 8:tm, :]
    tail_ref[...] = xb[tm - 8:tm, :]

    a, b = _lru_gates(xc, wa_ref, wx_ref, ba_ref, bx_ref, lam_ref)
    a_s[...] = a
    b_s[...] = b

    def body(t, h):
        h = a_s[pl.ds(t, 1), :] * h + b_s[pl.ds(t, 1), :]
        b_s[pl.ds(t, 1), :] = h
        return h

    h_last = lax.fori_loop(0, tm, body, h_ref[...], unroll=8)
    h_ref[...] = h_last
    hT_ref[...] = h_last
    y_ref[...] = _gelu_tanh(gate) * b_s[...]


def _lru_seq(u, n_seq, conv_prev8, h0, conv_w, conv_b, wa, wx, ba, bx, lam, tm):
    rows = u.shape[0]
    per_seq = rows // n_seq // tm
    const = lambda shape: pl.BlockSpec(shape, lambda b, t: (0,) * len(shape))
    return pl.pallas_call(
        _lru_seq_kernel,
        grid=(n_seq, per_seq),
        in_specs=[
            pl.BlockSpec((tm, 2 * C_WIDTH), lambda b, t: (b * per_seq + t, 0)),
            pl.BlockSpec((None, 8, C_WIDTH), lambda b, t: (b, 0, 0)),
            pl.BlockSpec((None, 1, C_WIDTH), lambda b, t: (b, 0, 0)),
            const((C_CONV, C_WIDTH)), const((1, C_WIDTH)),
            const((C_BLOCKS, C_BLOCK, C_BLOCK)), const((C_BLOCKS, C_BLOCK, C_BLOCK)),
            const((1, C_WIDTH)), const((1, C_WIDTH)), const((1, C_WIDTH)),
        ],
        out_specs=[
            pl.BlockSpec((tm, C_WIDTH), lambda b, t: (b * per_seq + t, 0)),
            pl.BlockSpec((None, 8, C_WIDTH), lambda b, t: (b, 0, 0)),
            pl.BlockSpec((None, 1, C_WIDTH), lambda b, t: (b, 0, 0)),
        ],
        out_shape=[jax.ShapeDtypeStruct((rows, C_WIDTH), F32),
                   jax.ShapeDtypeStruct((n_seq, 8, C_WIDTH), F32),
                   jax.ShapeDtypeStruct((n_seq, 1, C_WIDTH), F32)],
        scratch_shapes=[pltpu.VMEM((8, C_WIDTH), F32), pltpu.VMEM((1, C_WIDTH), F32),
                        pltpu.VMEM((tm, C_WIDTH), F32), pltpu.VMEM((tm, C_WIDTH), F32)],
        compiler_params=_params("arbitrary", "arbitrary"),
        name="lru_seq",
    )(u, conv_prev8, h0, conv_w, conv_b, wa, wx, ba, bx, lam)


def _lru_step_kernel(u_ref, cp_ref, h0_ref, cw_ref, cb_ref, wa_ref, wx_ref, ba_ref, bx_ref, lam_ref,
                     y_ref, cn_ref, h_ref):
    gate = u_ref[:, 0:C_WIDTH]
    xb = u_ref[:, C_WIDTH:2 * C_WIDTH]
    xc = cb_ref[...] + cw_ref[C_CONV - 1:C_CONV, :] * xb
    for i in range(C_CONV - 1):
        xc = xc + cw_ref[i:i + 1, :] * cp_ref[:, i * C_WIDTH:(i + 1) * C_WIDTH]
    a, b = _lru_gates(xc, wa_ref, wx_ref, ba_ref, bx_ref, lam_ref)
    h = a * h0_ref[...] + b
    h_ref[...] = h
    y_ref[...] = _gelu_tanh(gate) * h
    for i in range(C_CONV - 2):
        cn_ref[:, i * C_WIDTH:(i + 1) * C_WIDTH] = cp_ref[:, (i + 1) * C_WIDTH:(i + 2) * C_WIDTH]
    cn_ref[:, (C_CONV - 2) * C_WIDTH:(C_CONV - 1) * C_WIDTH] = xb


def _lru_step(u, conv_prev, h0, conv_w, conv_b, wa, wx, ba, bx, lam):
    rows = u.shape[0]
    full = lambda shape: pl.BlockSpec(shape, lambda i: (0,) * len(shape))
    cw = (C_CONV - 1) * C_WIDTH
    return pl.pallas_call(
        _lru_step_kernel,
        grid=(1,),
        in_specs=[full((rows, 2 * C_WIDTH)), full((rows, cw)), full((rows, C_WIDTH)),
                  full((C_CONV, C_WIDTH)), full((1, C_WIDTH)),
                  full((C_BLOCKS, C_BLOCK, C_BLOCK)), full((C_BLOCKS, C_BLOCK, C_BLOCK)),
                  full((1, C_WIDTH)), full((1, C_WIDTH)), full((1, C_WIDTH))],
        out_specs=[full((rows, C_WIDTH)), full((rows, cw)), full((rows, C_WIDTH))],
        out_shape=[jax.ShapeDtypeStruct((rows, C_WIDTH), F32),
                   jax.ShapeDtypeStruct((rows, cw), F32),
                   jax.ShapeDtypeStruct((rows, C_WIDTH), F32)],
        compiler_params=_params("arbitrary"),
        name="lru_step",
    )(u, conv_prev, h0, conv_w, conv_b, wa, wx, ba, bx, lam)


def _pad_rows(w, first, total):
    return jnp.zeros((total, w.shape[1]), w.dtype).at[first:first + w.shape[0]].set(w)


def _trunk(x, mod, n_seq, seq_len, st_shift, st_wkv, st_gla, st_conv, st_lru, wts, mxu, tm, tf):
    (w_ffn_up, w_ffn_down, w_in_even, w_out_even, a_mu, a_w_up, a_a_up, a_g_up, a_vec, b_alpha_up,
     b_alpha_b, b_norm_w, w_in_odd, w_out_odd, c_conv_w, c_conv_b, c_wa, c_ba, c_wx, c_bx, c_lam,
     final_norm_w) = wts
    rows = x.shape[0]
    prompt = seq_len > 1
    tn_even, tn_odd = 13 * LANES, 8 * LANES
    seg = jnp.kron(jnp.eye(A_HEADS, dtype=F32), jnp.ones((A_HEAD_DIM, A_HEAD_DIM), F32)).astype(BF16)
    out_shift, out_wkv, out_gla, out_conv, out_lru = [], [], [], [], []
    for l in range(DEPTH):
        j = l // 2
        if prompt:
            x = _ffn(x, mod, l, 0, mxu['ffn', l, 0], tm, tf)
        else:
            x, mxu['ffn', l, 0] = _ffn(x, mod, l, 0, (w_ffn_up, w_ffn_down), tm, tf)
        if l % 2 == 0:
            if prompt:
                u = _inproj(x, mod, l, mxu['in', l], None, tm, tn_even)
            else:
                u, mxu['in', l] = _inproj(x, mod, l, w_in_even, j, tm, tn_even)
            wup_pad = _pad_rows(a_w_up[j], 0, LANES)
            aup_pad = _pad_rows(a_a_up[j], A_DECAY_RANK, LANES)
            vec8 = _pad_rows(a_vec[j], 0, 8)
            alpha_pad = _pad_rows(b_alpha_up[j], 0, LANES)
            alpha_b = b_alpha_b[j].reshape(1, B_KEY_WIDTH)
            mu = a_mu[j].reshape(1, A_PROJ)
            norm_w = b_norm_w[j].reshape(1, B_WIDTH)
            if prompt:
                r, lw, k2, v, kk, al, g = _rwkv_prep(u, st_shift[j].reshape(n_seq, 1, A_PROJ), mu, wup_pad,
                                                     aup_pad, a_g_up[j], vec8, seg, n_seq, 256, True)
                y_a, h_t = _rwkv_chunk(r, lw, k2, v, kk, al, g, vec8, n_seq, _pair_states(st_wkv[j]))
                s_wkv = _unpair_states(h_t)
                s_shift = u.reshape(n_seq, seq_len, EVEN_PROJ)[:, -1, :A_PROJ]
                y_b, s_gla = _gla_chunk(u, n_seq, alpha_pad, alpha_b, norm_w, st_gla[j])
            else:
                r, lw, k2, v, kk, al, g, bonus = _rwkv_prep(u, st_shift[j], mu, wup_pad, aup_pad, a_g_up[j],
                                                            vec8, seg, rows, rows, False)
                y_raw, s_wkv = _rwkv_step(r, lw, k2, v, kk, al, st_wkv, j, 8)
                y_a = _rwkv_post(y_raw, bonus, g, vec8, seg, rows)
                s_shift = u[:, :A_PROJ]
                y_b, s_gla = _gla_step(u, alpha_pad, alpha_b, norm_w, st_gla, j, 8)
            if prompt:
                x = _outproj([y_a, y_b], mxu['out', l], None, x, mod, l, min(tm, 512))
            else:
                x, mxu['out', l] = _outproj([y_a, y_b], w_out_even, j, x, mod, l, tm)
            out_shift.append(s_shift)
            out_wkv.append(s_wkv)
            out_gla.append(s_gla)
        else:
            if prompt:
                u = _inproj(x, mod, l, mxu['in', l], None, tm, tn_odd)
            else:
                u, mxu['in', l] = _inproj(x, mod, l, w_in_odd, j, tm, tn_odd)
            lru_w = (c_conv_w[j], c_conv_b[j].reshape(1, C_WIDTH), c_wa[j], c_wx[j],
                     c_ba[j].reshape(1, C_WIDTH), c_bx[j].reshape(1, C_WIDTH), c_lam[j].reshape(1, C_WIDTH))
            if prompt:
                cp8 = jnp.concatenate([jnp.zeros((n_seq, 8 - (C_CONV - 1), C_WIDTH), F32), st_conv[j]], axis=1)
                y_c, tail, h_t = _lru_seq(u, n_seq, cp8, st_lru[j].reshape(n_seq, 1, C_WIDTH), *lru_w, 256)
                s_conv = tail[:, 8 - (C_CONV - 1):, :]
                s_lru = h_t.reshape(n_seq, C_WIDTH)
            else:
                y_c, cn, s_lru = _lru_step(u, st_conv[j].reshape(rows, (C_CONV - 1) * C_WIDTH), st_lru[j], *lru_w)
                s_conv = cn.reshape(rows, C_CONV - 1, C_WIDTH)
            if prompt:
                x = _outproj([y_c], mxu['out', l], None, x, mod, l, min(tm, 512))
            else:
                x, mxu['out', l] = _outproj([y_c], w_out_odd, j, x, mod, l, tm)
            out_conv.append(s_conv)
            out_lru.append(s_lru)
        final_w = final_norm_w if l == DEPTH - 1 else None
        if prompt:
            x = _ffn(x, mod, l, 1, mxu['ffn', l, 1], tm, tf, final_w=final_w)
        else:
            x, mxu['ffn', l, 1] = _ffn(x, mod, l, 1, (w_ffn_up, w_ffn_down), tm, tf, final_w=final_w)
    return x, (jnp.stack(out_shift), jnp.stack(out_wkv), jnp.stack(out_gla), jnp.stack(out_conv),
               jnp.stack(out_lru))


def kernel(x_prompt, x_sample, c_prompt, c_sample, state_rwkv_shift, state_rwkv_wkv, state_gla, state_conv, state_lru, w_ada, b_ada, w_ffn_up, w_ffn_down, w_in_even, w_out_even, a_mu, a_w_up, a_a_up, a_g_up, a_vec, b_alpha_up, b_alpha_b, b_norm_w, w_in_odd, w_out_odd, c_conv_w, c_conv_b, c_wa, c_ba, c_wx, c_bx, c_lam, final_norm_w):
    wts = (w_ffn_up, w_ffn_down, w_in_even, w_out_even, a_mu, a_w_up, a_a_up, a_g_up, a_vec, b_alpha_up,
           b_alpha_b, b_norm_w, w_in_odd, w_out_odd, c_conv_w, c_conv_b, c_wa, c_ba, c_wx, c_bx, c_lam,
           final_norm_w)
    n_p, t_p, _ = x_prompt.shape
    n_s, t_s, _ = x_sample.shape
    assert t_s == 1

    c_rows = n_p + n_s
    pad = (-c_rows) % 16
    c_all = jnp.concatenate([c_sample, c_prompt, jnp.zeros((pad, D_MODEL), F32)], axis=0)
    mod_all = _ada(c_all, w_ada, b_ada)
    mod_p = _Mod(mod_all[:, :, n_s:c_rows].reshape(DEPTH, N_MOD, n_p, 1, D_MODEL), False, t_p)
    mod_s = _Mod(mod_all, True, 1)

    fresh = lambda s: jnp.zeros((s.shape[0], n_p) + s.shape[2:], s.dtype)
    mxu = {}
    y_s, st_s = _trunk(x_sample.reshape(n_s, D_MODEL), mod_s, n_s, 1,
                       state_rwkv_shift, state_rwkv_wkv, state_gla, state_conv, state_lru, wts, mxu, n_s, 512)
    y_p, st_p = _trunk(x_prompt.reshape(n_p * t_p, D_MODEL), mod_p, n_p, t_p,
                       fresh(state_rwkv_shift), fresh(state_rwkv_wkv), fresh(state_gla),
                       fresh(state_conv), fresh(state_lru), wts, mxu, 1024, 512)
    return (y_p.reshape(n_p, t_p, D_MODEL), y_s.reshape(n_s, 1, D_MODEL), *st_p, *st_s)
```

```python
import functools

import jax
import jax.numpy as jnp
from jax import lax
from jax.experimental import pallas as pl
from jax.experimental.pallas import tpu as pltpu

F32 = jnp.float32
BF16 = jnp.bfloat16

D_MODEL = 2048
DEPTH = 4
N_MOD = 9
D_FF = 5632
NORM_EPS = 1e-6

A_HEADS = 16
A_HEAD_DIM = 64
A_WIDTH = A_HEADS * A_HEAD_DIM
A_DECAY_RANK = 64
A_ICL_RANK = 64
A_GATE_RANK = 128
A_PROJ = 3 * A_WIDTH + A_DECAY_RANK + A_ICL_RANK + A_GATE_RANK
A_GN_EPS = 64e-5

B_HEADS = 4
B_KEY_DIM = 128
B_VAL_DIM = 256
B_KEY_WIDTH = B_HEADS * B_KEY_DIM
B_WIDTH = B_HEADS * B_VAL_DIM
B_ALPHA_RANK = 16
B_TAU = 16.0
B_PROJ = 2 * B_KEY_WIDTH + 2 * B_WIDTH + B_ALPHA_RANK
EVEN_PROJ = A_PROJ + B_PROJ

C_WIDTH = D_MODEL
C_BLOCKS = 8
C_BLOCK = C_WIDTH // C_BLOCKS
C_CONV = 4
C_POW = 8.0

CHUNK = 64
RWKV_GROUP = 8
GLA_SUB = 8
LANES = 128
VMEM_LIMIT = 56 * 1024 * 1024


def _params(*sem):
    return pltpu.CompilerParams(dimension_semantics=sem, vmem_limit_bytes=VMEM_LIMIT)


def _bdot(a, b):
    return jnp.dot(a.astype(BF16), b.astype(BF16), preferred_element_type=F32)


def _split3(x):
    hi = x.astype(BF16)
    r1 = x - hi.astype(F32)
    mid = r1.astype(BF16)
    lo = (r1 - mid.astype(F32)).astype(BF16)
    return hi, mid, lo


def _dot_exact_rhs(a_bf16, x):
    hi, mid, lo = _split3(x)
    f = lambda y: jnp.dot(a_bf16, y, preferred_element_type=F32)
    return f(hi) + f(mid) + f(lo)


def _dot_exact_lhs(x, b_bf16):
    hi, mid, lo = _split3(x)
    f = lambda y: jnp.dot(y, b_bf16, preferred_element_type=F32)
    return f(hi) + f(mid) + f(lo)


def _sigmoid(x):
    return jax.nn.sigmoid(x)


def _softplus(x):
    return jnp.maximum(x, 0.0) + jnp.log1p(jnp.exp(-jnp.abs(x)))


def _gelu_tanh(x):
    return 0.5 * x * (1.0 + jnp.tanh(0.7978845608028654 * (x + 0.044715 * (x * x * x))))


def _norm_mod(x, shift, scale):
    ms = jnp.mean(x * x, axis=-1, keepdims=True)
    return x * lax.rsqrt(ms + NORM_EPS) * (1.0 + scale) + shift


def _ada_kernel(c_ref, w_ref, b_ref, o_ref):
    c = c_ref[...]
    act = (c * _sigmoid(c)).astype(BF16)
    o_ref[...] = jnp.dot(act, w_ref[...].astype(BF16), preferred_element_type=F32) + b_ref[...]


def _ada(c_all, w_ada, b_ada):
    rows = c_all.shape[0]
    tn = 1024
    per = D_MODEL // tn
    b4 = b_ada.reshape(DEPTH, N_MOD * per, 1, tn)
    return pl.pallas_call(
        _ada_kernel,
        grid=(DEPTH, N_MOD * per),
        in_specs=[
            pl.BlockSpec((rows, D_MODEL), lambda l, j: (0, 0)),
            pl.BlockSpec((None, D_MODEL, tn), lambda l, j: (l, 0, j)),
            pl.BlockSpec((None, None, 1, tn), lambda l, j: (l, j, 0, 0)),
        ],
        out_specs=pl.BlockSpec((None, None, rows, tn), lambda l, j: (l, j // per, 0, j % per)),
        out_shape=jax.ShapeDtypeStruct((DEPTH, N_MOD, rows, D_MODEL), F32),
        compiler_params=_params("arbitrary", "arbitrary"),
        name="ada",
    )(c_all, w_ada, b4)


class _Mod:
    def __init__(self, arr, per_row, seq_len):
        self.arr = arr
        self.per_row = per_row
        self.seq_len = seq_len

    def spec(self, layer, m, tm):
        if self.per_row:
            return pl.BlockSpec((None, None, tm, D_MODEL), lambda i, *_: (layer, m, i, 0))
        per_seq = self.seq_len // tm
        return pl.BlockSpec((None, None, None, 1, D_MODEL), lambda i, *_: (layer, m, i // per_seq, 0, 0))


def _ffn_kernel(x_ref, sh_ref, sc_ref, g_ref, wg_ref, wu_ref, wd_ref, *rest, n_j, sub, final, emit):
    rest = list(rest)
    fw_ref = rest.pop(0) if final else None
    o_ref = rest.pop(0)
    w_outs = [rest.pop(0) for _ in range(3)] if emit else None
    h_ref, = rest
    j = pl.program_id(1)
    tm = x_ref.shape[0]

    @pl.when(j == 0)
    def _():
        def body(s, carry):
            rows = pl.ds(pl.multiple_of(s * sub, sub), sub)
            sh = sh_ref[...] if sh_ref.shape[0] == 1 else sh_ref[rows, :]
            sc = sc_ref[...] if sc_ref.shape[0] == 1 else sc_ref[rows, :]
            h_ref[rows, :] = _norm_mod(x_ref[rows, :], sh, sc).astype(BF16)
            o_ref[rows, :] = jnp.zeros((sub, D_MODEL), F32)
            return carry
        lax.fori_loop(0, tm // sub, body, 0, unroll=min(2, tm // sub))

    wg, wu, wd = wg_ref[...], wu_ref[...], wd_ref[...]
    if emit:
        wg, wu, wd = wg.astype(BF16), wu.astype(BF16), wd.astype(BF16)
        for w_out, w in zip(w_outs, (wg, wu, wd)):
            w_out[...] = w
    h = h_ref[...]
    gt = jnp.dot(h, wg, preferred_element_type=F32)
    ut = jnp.dot(h, wu, preferred_element_type=F32)
    act = (gt * _sigmoid(gt) * ut).astype(BF16)
    o_ref[...] += jnp.dot(act, wd, preferred_element_type=F32)

    @pl.when(j == n_j - 1)
    def _():
        def body(s, carry):
            rows = pl.ds(pl.multiple_of(s * sub, sub), sub)
            g = g_ref[...] if g_ref.shape[0] == 1 else g_ref[rows, :]
            y = x_ref[rows, :] + 0.5 * g * o_ref[rows, :]
            if final:
                ms = jnp.mean(y * y, axis=-1, keepdims=True)
                y = y * lax.rsqrt(ms + NORM_EPS) * fw_ref[...]
            o_ref[rows, :] = y
            return carry
        lax.fori_loop(0, tm // sub, body, 0, unroll=min(2, tm // sub))


def _ffn(x, mod, layer, which, weights, tm, tf, final_w=None):
    m_rows = x.shape[0]
    n_j = D_FF // tf
    m0 = 6 * which
    sub = min(tm, 128)
    final = final_w is not None
    emit = len(weights) == 2
    up_spec = pl.BlockSpec((D_MODEL, tf), lambda i, j: (0, j))
    down_spec = pl.BlockSpec((tf, D_MODEL), lambda i, j: (j, 0))
    in_specs = [
        pl.BlockSpec((tm, D_MODEL), lambda i, j: (i, 0), pipeline_mode=pl.Buffered(1)),
        mod.spec(layer, m0, tm), mod.spec(layer, m0 + 1, tm), mod.spec(layer, m0 + 2, tm),
    ]
    if emit:
        w_up, w_down = weights
        in_specs += [pl.BlockSpec((None, None, D_MODEL, tf), lambda i, j: (layer, which, 0, j)),
                     pl.BlockSpec((None, None, D_MODEL, tf), lambda i, j: (layer, which, 0, j + n_j)),
                     pl.BlockSpec((None, None, tf, D_MODEL), lambda i, j: (layer, which, j, 0))]
        args = [x, mod.arr, mod.arr, mod.arr, w_up, w_up, w_down]
    else:
        in_specs += [up_spec, up_spec, down_spec]
        args = [x, mod.arr, mod.arr, mod.arr, *weights]
    if final:
        in_specs.append(pl.BlockSpec((1, D_MODEL), lambda i, j: (0, 0)))
        args.append(final_w.reshape(1, D_MODEL))
    out_specs = [pl.BlockSpec((tm, D_MODEL), lambda i, j: (i, 0))]
    out_shape = [jax.ShapeDtypeStruct((m_rows, D_MODEL), F32)]
    if emit:
        assert m_rows == tm
        out_specs += [up_spec, up_spec, down_spec]
        out_shape += [jax.ShapeDtypeStruct((D_MODEL, D_FF), BF16)] * 2 + [jax.ShapeDtypeStruct((D_FF, D_MODEL), BF16)]
    out = pl.pallas_call(
        functools.partial(_ffn_kernel, n_j=n_j, sub=sub, final=final, emit=emit),
        grid=(m_rows // tm, n_j),
        in_specs=in_specs,
        out_specs=out_specs,
        out_shape=out_shape,
        scratch_shapes=[pltpu.VMEM((tm, D_MODEL), BF16)],
        compiler_params=_params("arbitrary", "arbitrary"),
        name="ffn",
    )(*args)
    return (out[0], tuple(out[1:])) if emit else out[0]


def _inproj_kernel(x_ref, sh_ref, sc_ref, w_ref, o_ref, *rest, sub, emit, transposed):
    h_ref = rest[-1]
    j = pl.program_id(1)
    tm = x_ref.shape[0]

    @pl.when(j == 0)
    def _():
        def body(s, carry):
            rows = pl.ds(pl.multiple_of(s * sub, sub), sub)
            sh = sh_ref[...] if sh_ref.shape[0] == 1 else sh_ref[rows, :]
            sc = sc_ref[...] if sc_ref.shape[0] == 1 else sc_ref[rows, :]
            h_ref[rows, :] = _norm_mod(x_ref[rows, :], sh, sc).astype(BF16)
            return carry
        lax.fori_loop(0, tm // sub, body, 0, unroll=min(2, tm // sub))

    w = w_ref[...]
    if transposed:
        w = w.T
    if emit:
        w = w.astype(BF16)
        rest[0][...] = w
    o_ref[...] = jnp.dot(h_ref[...], w, preferred_element_type=F32)


def _inproj(x, mod, layer, w, widx, tm, tn, transposed=False):
    m_rows = x.shape[0]
    n_out = w.shape[-2] if transposed else w.shape[-1]
    sub = min(tm, 128)
    emit = widx is not None
    if transposed:
        w_spec = pl.BlockSpec((None, tn, D_MODEL), lambda i, j: (widx, j, 0))
    elif emit:
        w_spec = pl.BlockSpec((None, D_MODEL, tn), lambda i, j: (widx, 0, j))
    else:
        w_spec = pl.BlockSpec((D_MODEL, tn), lambda i, j: (0, j))
    out_specs = [pl.BlockSpec((tm, tn), lambda i, j: (i, j))]
    out_shape = [jax.ShapeDtypeStruct((m_rows, n_out), F32)]
    if emit:
        assert m_rows == tm
        out_specs.append(pl.BlockSpec((D_MODEL, tn), lambda i, j: (0, j)))
        out_shape.append(jax.ShapeDtypeStruct((D_MODEL, n_out), BF16))
    out = pl.pallas_call(
        functools.partial(_inproj_kernel, sub=sub, emit=emit, transposed=transposed),
        grid=(m_rows // tm, pl.cdiv(n_out, tn)),
        in_specs=[
            pl.BlockSpec((tm, D_MODEL), lambda i, j: (i, 0)),
            mod.spec(layer, 3, tm), mod.spec(layer, 4, tm),
            w_spec,
        ],
        out_specs=out_specs,
        out_shape=out_shape,
        scratch_shapes=[pltpu.VMEM((tm, D_MODEL), BF16)],
        compiler_params=_params("arbitrary", "arbitrary"),
        name="inproj",
    )(x, mod.arr, mod.arr, w)
    return tuple(out) if emit else out[0]


def _outproj_kernel(*refs, n_in, emit):
    y_refs = refs[:n_in]
    w_refs = refs[n_in:2 * n_in]
    x_ref, g_ref, o_ref = refs[2 * n_in:2 * n_in + 3]
    acc = None
    for k, (y_ref, w_ref) in enumerate(zip(y_refs, w_refs)):
        w = w_ref[...]
        if emit:
            w = w.astype(BF16)
            refs[2 * n_in + 3 + k][...] = w
        part = jnp.dot(y_ref[...].astype(BF16), w, preferred_element_type=F32)
        acc = part if acc is None else acc + part
    o_ref[...] = x_ref[...] + g_ref[...] * acc


def _outproj(ys, w, widx, x, mod, layer, tm):
    m_rows = x.shape[0]
    n_in = len(ys)
    kw = ys[0].shape[1]
    emit = widx is not None
    in_specs = [pl.BlockSpec((tm, kw), lambda i: (i, 0)) for _ in ys]
    if emit:
        assert m_rows == tm
        in_specs += [pl.BlockSpec((None, kw, D_MODEL), lambda i, k=k: (widx, k, 0)) for k in range(n_in)]
        w_args = [w] * n_in
    else:
        in_specs += [pl.BlockSpec((kw, D_MODEL), lambda i: (0, 0))] * n_in
        w_args = list(w)
    in_specs += [pl.BlockSpec((tm, D_MODEL), lambda i: (i, 0)), mod.spec(layer, 5, tm)]
    out_specs = [pl.BlockSpec((tm, D_MODEL), lambda i: (i, 0))]
    out_shape = [jax.ShapeDtypeStruct((m_rows, D_MODEL), F32)]
    if emit:
        out_specs += [pl.BlockSpec((kw, D_MODEL), lambda i: (0, 0))] * n_in
        out_shape += [jax.ShapeDtypeStruct((kw, D_MODEL), BF16)] * n_in
    out = pl.pallas_call(
        functools.partial(_outproj_kernel, n_in=n_in, emit=emit),
        grid=(m_rows // tm,),
        in_specs=in_specs,
        out_specs=out_specs,
        out_shape=out_shape,
        compiler_params=_params("arbitrary"),
        name="outproj",
    )(*ys, *w_args, x, mod.arr)
    return (out[0], tuple(out[1:])) if emit else out[0]


def _rwkv_prep_kernel(u_ref, prev_ref, mu_ref, wup_ref, aup_ref, gup_ref, vec_ref, *rest, seq_mode):
    if seq_mode:
        r_ref, lw_ref, k_ref, v_ref, kk_ref, a_ref, g_ref, carry_ref = rest
    else:
        seg_ref, r_ref, lw_ref, k_ref, v_ref, kk_ref, a_ref, g_ref, bonus_ref = rest
    u = u_ref[...]
    tm = u.shape[0]
    if seq_mode:
        @pl.when(pl.program_id(1) == 0)
        def _():
            carry_ref[...] = prev_ref[...]
        rolled = pltpu.roll(u, 1, axis=0)
        row = lax.broadcasted_iota(jnp.int32, u.shape, 0)
        prev = jnp.where(row == 0, carry_ref[...], rolled)
        carry_ref[...] = u[tm - 1:tm, :]
    else:
        prev = prev_ref[...]
    xs = u + mu_ref[...] * (prev - u)
    r = xs[:, 0:A_WIDTH]
    k = xs[:, A_WIDTH:2 * A_WIDTH]
    v = xs[:, 2 * A_WIDTH:3 * A_WIDTH]
    wa_lo = xs[:, 3 * A_WIDTH:3 * A_WIDTH + LANES]
    g_lo = xs[:, 3 * A_WIDTH + LANES:]
    w0, a0, k_k, k_a, r_k = (vec_ref[i:i + 1, :] for i in range(5))
    w_raw = -_softplus(-(w0 + _bdot(jnp.tanh(wa_lo), wup_ref[...]))) - 0.5
    lw = -jnp.exp(w_raw)
    a = _sigmoid(a0 + _bdot(wa_lo, aup_ref[...]))
    g = _bdot(_sigmoid(g_lo), gup_ref[...])
    kk = k * k_k
    k2 = k * (1.0 + (a - 1.0) * k_a)
    if not seq_mode:
        seg = seg_ref[...]
        kk = kk * lax.rsqrt(jnp.maximum(_dot_exact_lhs(kk * kk, seg), 1e-24))
        bonus_ref[...] = _dot_exact_lhs(r * k2 * r_k, seg) * v
    r_ref[...] = r
    lw_ref[...] = lw
    k_ref[...] = k2
    v_ref[...] = v
    kk_ref[...] = kk
    a_ref[...] = a
    g_ref[...] = g


def _rwkv_prep(u, prev, mu, wup_pad, aup_pad, g_up, vec, seg, n_seq, tm, seq_mode):
    rows = u.shape[0]
    per_seq = rows // n_seq // tm if seq_mode else 1
    grid = (n_seq, per_seq) if seq_mode else (rows // tm, 1)
    rowmap = (lambda b, t: (b * per_seq + t, 0)) if seq_mode else (lambda b, t: (b, 0))
    prev_spec = (pl.BlockSpec((None, 1, A_PROJ), lambda b, t: (b, 0, 0)) if seq_mode
                 else pl.BlockSpec((tm, A_PROJ), rowmap))
    const = lambda shape: pl.BlockSpec(shape, lambda b, t: (0,) * len(shape))
    out = jax.ShapeDtypeStruct((rows, A_WIDTH), F32)
    in_specs = [pl.BlockSpec((tm, A_PROJ), rowmap), prev_spec,
                const((1, A_PROJ)), const((LANES, A_WIDTH)), const((LANES, A_WIDTH)),
                const((A_GATE_RANK, A_WIDTH)), const((8, A_WIDTH))]
    args = [u, prev, mu, wup_pad, aup_pad, g_up, vec]
    if not seq_mode:
        in_specs.append(const((A_WIDTH, A_WIDTH)))
        args.append(seg)
    n_out = 7 if seq_mode else 8
    return pl.pallas_call(
        functools.partial(_rwkv_prep_kernel, seq_mode=seq_mode),
        grid=grid,
        in_specs=in_specs,
        out_specs=[pl.BlockSpec((tm, A_WIDTH), rowmap)] * n_out,
        out_shape=[out] * n_out,
        scratch_shapes=[pltpu.VMEM((1, A_PROJ), F32)] if seq_mode else [],
        compiler_params=_params("arbitrary", "arbitrary"),
        name="rwkv_prep",
    )(*args)


def _rwkv_post_kernel(y_ref, bonus_ref, g_ref, vec_ref, seg_ref, o_ref):
    y = y_ref[...]
    seg = seg_ref[...]
    inv_n = 1.0 / A_HEAD_DIM
    yc = y - _dot_exact_lhs(y, seg) * inv_n
    var = _dot_exact_lhs(yc * yc, seg) * inv_n
    yn = yc * lax.rsqrt(var + A_GN_EPS) * vec_ref[5:6, :] + vec_ref[6:7, :]
    o_ref[...] = (yn + bonus_ref[...]) * g_ref[...]


def _rwkv_post(y, bonus, g, vec, seg, tm):
    rows = y.shape[0]
    spec = pl.BlockSpec((tm, A_WIDTH), lambda i: (i, 0))
    return pl.pallas_call(
        _rwkv_post_kernel,
        grid=(rows // tm,),
        in_specs=[spec, spec, spec,
                  pl.BlockSpec((8, A_WIDTH), lambda i: (0, 0)),
                  pl.BlockSpec((A_WIDTH, A_WIDTH), lambda i: (0, 0))],
        out_specs=spec,
        out_shape=jax.ShapeDtypeStruct((rows, A_WIDTH), F32),
        compiler_params=_params("arbitrary"),
        name="rwkv_post",
    )(y, bonus, g, vec, seg)


def _rwkv_chunk_kernel(r_ref, lw_ref, k_ref, v_ref, kk_ref, a_ref, g_ref, vec_ref, h0_ref, y_ref, hT_ref, h_ref):
    ci = pl.program_id(1)

    @pl.when(ci == 0)
    def _():
        h_ref[...] = h0_ref[...]

    cs = r_ref.shape[0]
    n2 = 2 * cs
    npair = RWKV_GROUP

    def group(gi, carry):
        lanes = lambda p: pl.ds(pl.multiple_of((gi * npair + p) * LANES, LANES), LANES)
        pairs = lambda ref: jnp.stack([ref[:, lanes(p)] for p in range(npair)])
        vrow = lambda i: jnp.stack([vec_ref[i:i + 1, lanes(p)] for p in range(npair)])
        lane = lax.broadcasted_iota(jnp.int32, (1, 1, LANES), 2)
        m0 = (lane < A_HEAD_DIM).astype(F32)
        m1 = 1.0 - m0
        split = lambda x: jnp.concatenate([x * m0, x * m1], axis=1)
        dup = lambda x: jnp.concatenate([x, x], axis=1)
        own = jnp.concatenate([jnp.broadcast_to(m0, (1, cs, LANES)), jnp.broadcast_to(m1, (1, cs, LANES))], axis=1)

        def bmm(x, y):
            return jnp.einsum('hab,hbc->hac', x.astype(BF16), y.astype(BF16), preferred_element_type=F32)

        def bmm_nt(x, y):
            return jnp.einsum('han,hbn->hab', x.astype(BF16), y.astype(BF16), preferred_element_type=F32)

        def bmm_tn(x, y):
            return jnp.einsum('hca,hcb->hab', x.astype(BF16), y.astype(BF16), preferred_element_type=F32)

        lw = pairs(lw_ref)
        row = lax.broadcasted_iota(jnp.int32, (cs, cs), 0)
        col = lax.broadcasted_iota(jnp.int32, (cs, cs), 1)
        tri = jnp.broadcast_to((col <= row).astype(BF16)[None], (npair, cs, cs))
        hi, mid, lo = _split3(lw)
        csum = lambda part: jnp.einsum('hab,hbc->hac', tri, part, preferred_element_type=F32)
        c = csum(hi) + csum(mid) + csum(lo)
        c_end = c[:, cs - 1:cs, :]
        e_pos = dup(jnp.exp(c))
        e_neg = dup(jnp.exp(-c))
        e_prev = dup(jnp.exp(c - lw))
        e_end = dup(jnp.exp(c_end - c))
        g_end = jnp.exp(c_end)

        kk = split(pairs(kk_ref))
        kk = kk * lax.rsqrt(jnp.maximum(jnp.sum(kk * kk, axis=-1, keepdims=True), 1e-24))
        b_vec = kk * dup(pairs(a_ref))
        r2 = split(pairs(r_ref))
        k2 = split(pairs(k_ref))
        v2 = split(pairs(v_ref))
        bonus = jnp.sum(r2 * k2 * vrow(4), axis=-1, keepdims=True) * v2
        rt = r2 * e_pos
        at = -kk * e_prev
        bt = b_vec * e_neg
        kt = k2 * e_neg
        bh = b_vec * e_end
        kh = k2 * e_end

        row2 = lax.broadcasted_iota(jnp.int32, (n2, n2), 0)
        col2 = lax.broadcasted_iota(jnp.int32, (n2, n2), 1)
        same = (row2 >= cs) == (col2 >= cs)
        tok_r = jnp.where(row2 >= cs, row2 - cs, row2)
        tok_c = jnp.where(col2 >= cs, col2 - cs, col2)
        strict = jnp.where(same, (tok_c < tok_r).astype(F32), 0.0)
        incl = jnp.where(same, (tok_c <= tok_r).astype(F32), 0.0)
        eye2 = (row2 == col2).astype(F32)
        nrow = lax.broadcasted_iota(jnp.int32, (LANES, LANES), 0)
        ncol = lax.broadcasted_iota(jnp.int32, (LANES, LANES), 1)
        eye_n = (nrow == ncol).astype(F32)

        a_ab = bmm_nt(at, bt) * strict
        a_ak = bmm_nt(at, kt) * strict
        m_b = bmm_nt(rt, bt) * incl
        m_k = bmm_nt(rt, kt) * incl

        tinv = eye2 + a_ab
        apow = a_ab
        span = 1
        while 2 * span < cs:
            apow = bmm(apow, apow)
            tinv = tinv + bmm(tinv, apow)
            span *= 2

        akv = bmm(a_ak, v2)
        pq = bmm(tinv, jnp.concatenate([at, akv], axis=-1))
        mpq = bmm(m_b, pq)
        p2 = rt + mpq[..., :LANES]
        y0 = mpq[..., LANES:] + bmm(m_k, v2)
        tpq = bmm_tn(bh, pq)
        gm = eye_n * g_end + tpq[..., :LANES]
        hadd = tpq[..., LANES:] + bmm_tn(kh, v2)

        h = h_ref[pl.ds(gi * npair, npair)]
        y2 = bmm(p2, h) + y0
        h_ref[pl.ds(gi * npair, npair)] = bmm(gm, h) + hadd

        inv_n = 1.0 / A_HEAD_DIM
        yc = (y2 - jnp.sum(y2, axis=-1, keepdims=True) * inv_n) * own
        var = jnp.sum(yc * yc, axis=-1, keepdims=True) * inv_n
        z = yc * lax.rsqrt(var + A_GN_EPS) * vrow(5) + vrow(6) * own + bonus
        out = (z[:, :cs] + z[:, cs:]) * pairs(g_ref)
        for p in range(npair):
            y_ref[:, lanes(p)] = out[p]
        return carry

    lax.fori_loop(0, A_WIDTH // LANES // npair, group, 0)

    @pl.when(ci == pl.num_programs(1) - 1)
    def _():
        hT_ref[...] = h_ref[...]


def _rwkv_chunk(r, lw, k, v, kk, a, g, vec, n_seq, h0):
    rows = r.shape[0]
    nc = rows // n_seq // CHUNK
    npair = A_WIDTH // LANES
    spec = pl.BlockSpec((CHUNK, A_WIDTH), lambda b, c: (b * nc + c, 0))
    hspec = pl.BlockSpec((None, npair, LANES, LANES), lambda b, c: (b, 0, 0, 0))
    return pl.pallas_call(
        _rwkv_chunk_kernel,
        grid=(n_seq, nc),
        in_specs=[spec] * 7 + [pl.BlockSpec((8, A_WIDTH), lambda b, c: (0, 0)), hspec],
        out_specs=[spec, hspec],
        out_shape=[jax.ShapeDtypeStruct((rows, A_WIDTH), F32),
                   jax.ShapeDtypeStruct((n_seq, npair, LANES, LANES), F32)],
        scratch_shapes=[pltpu.VMEM((npair, LANES, LANES), F32)],
        compiler_params=_params("arbitrary", "arbitrary"),
        name="rwkv_chunk",
    )(r, lw, k, v, kk, a, g, vec, h0)


def _pair_states(s):
    n = s.shape[0]
    nd = A_HEAD_DIM
    h = jnp.swapaxes(s, -1, -2).reshape(n, A_HEADS // 2, 2, nd, nd)
    out = jnp.zeros((n, A_HEADS // 2, 2 * nd, 2 * nd), s.dtype)
    return out.at[:, :, :nd, :nd].set(h[:, :, 0]).at[:, :, nd:, nd:].set(h[:, :, 1])


def _unpair_states(hp):
    n = hp.shape[0]
    nd = A_HEAD_DIM
    h = jnp.stack([hp[:, :, :nd, :nd], hp[:, :, nd:, nd:]], axis=2).reshape(n, A_HEADS, nd, nd)
    return jnp.swapaxes(h, -1, -2)


def _rwkv_step_kernel(r_ref, lw_ref, k_ref, v_ref, kk_ref, a_ref, s_ref, y_ref, so_ref,
                      r_t, w_t, k_t, v_t, nkk_t, b_t, y_t):
    h = pl.program_id(0)
    nd = A_HEAD_DIM

    @pl.when(h == 0)
    def _():
        kk = kk_ref[...]
        r_t[...] = r_ref[...].T
        w_t[...] = jnp.exp(lw_ref[...]).T
        k_t[...] = k_ref[...].T
        v_t[...] = v_ref[...].T
        nkk_t[...] = (-kk).T
        b_t[...] = (kk * a_ref[...]).T

    first = pl.multiple_of(h * nd, nd)
    rows = pl.ds(first, nd)
    r, w, k, nkk, b = r_t[rows, :], w_t[rows, :], k_t[rows, :], nkk_t[rows, :], b_t[rows, :]

    def body(i, carry):
        s = s_ref[i]
        s_kk = jnp.sum(s * nkk, axis=0, keepdims=True)
        s_new = s * w + s_kk * b + v_t[pl.ds(first + i, 1), :] * k
        so_ref[i] = s_new
        y_t[pl.ds(first + i, 1), :] = jnp.sum(s_new * r, axis=0, keepdims=True)
        return carry

    lax.fori_loop(0, nd, body, 0, unroll=4)

    @pl.when(h == pl.num_programs(0) - 1)
    def _():
        y_ref[...] = y_t[...].T


def _rwkv_step(r, lw, k, v, kk, a, states_t, j):
    n = r.shape[0]
    nd = A_HEAD_DIM
    rspec = pl.BlockSpec((n, A_WIDTH), lambda h: (0, 0))
    return pl.pallas_call(
        _rwkv_step_kernel,
        grid=(A_HEADS,),
        in_specs=[rspec] * 6 + [pl.BlockSpec((None, None, nd, nd, n), lambda h: (j, h, 0, 0, 0))],
        out_specs=[rspec, pl.BlockSpec((None, nd, nd, n), lambda h: (h, 0, 0, 0))],
        out_shape=[jax.ShapeDtypeStruct((n, A_WIDTH), F32), jax.ShapeDtypeStruct(states_t.shape[1:], F32)],
        scratch_shapes=[pltpu.VMEM((A_WIDTH, n), F32)] * 7,
        compiler_params=_params("arbitrary"),
        name="rwkv_step",
    )(r, lw, k, v, kk, a, states_t)


def _gla_log_alpha(a128, aup_ref, ab_ref):
    lane = lax.broadcasted_iota(jnp.int32, a128.shape, 1)
    a_lo = jnp.where(lane < B_ALPHA_RANK, a128, 0.0)
    x = _bdot(a_lo, aup_ref[...]) + ab_ref[...]
    return -_softplus(-x) * (1.0 / B_TAU)


def _gla_chunk_kernel(u_ref, aup_ref, ab_ref, nw_ref, s0_ref, o_ref, sT_ref, s_ref):
    ci = pl.program_id(1)

    @pl.when(ci == 0)
    def _():
        s_ref[...] = s0_ref[...]

    ub = u_ref[...]
    cs = ub.shape[0]
    q_all = ub[:, 0:B_KEY_WIDTH] * (B_KEY_DIM ** -0.5)
    k_all = ub[:, B_KEY_WIDTH:2 * B_KEY_WIDTH]
    v_all = ub[:, 2 * B_KEY_WIDTH:2 * B_KEY_WIDTH + B_WIDTH]
    g_all = ub[:, 2 * B_KEY_WIDTH + B_WIDTH:2 * B_KEY_WIDTH + 2 * B_WIDTH]
    a128 = ub[:, 2 * B_KEY_WIDTH + 2 * B_WIDTH:2 * B_KEY_WIDTH + 2 * B_WIDTH + LANES]
    log_a = _gla_log_alpha(a128, aup_ref, ab_ref)

    row = lax.broadcasted_iota(jnp.int32, (cs, cs), 0)
    col = lax.broadcasted_iota(jnp.int32, (cs, cs), 1)
    cum = _dot_exact_rhs((col <= row).astype(BF16), log_a)
    e_pos = jnp.exp(cum)
    cum_end = cum[cs - 1:cs, :]
    e_end = jnp.exp(cum_end - cum)
    g_end = jnp.exp(cum_end)
    nk = B_KEY_DIM
    eye_k = (lax.broadcasted_iota(jnp.int32, (nk, nk), 0) == lax.broadcasted_iota(jnp.int32, (nk, nk), 1)).astype(F32)

    nblk = cs // GLA_SUB
    kw = B_KEY_WIDTH
    cum3 = cum.reshape(nblk, GLA_SUB, kw)
    before = jnp.concatenate([jnp.zeros((1, 1, kw), F32), cum3[:nblk - 1, GLA_SUB - 1:GLA_SUB, :]], axis=0)
    q_rel = q_all * jnp.exp(cum - jnp.broadcast_to(before, (nblk, GLA_SUB, kw)).reshape(cs, kw))
    tok = lax.broadcasted_iota(jnp.int32, (cs, 1), 0)
    q_parts, k_parts = [], []
    for blk in range(1, nblk):
        first = blk * GLA_SUB
        k_parts.append(jnp.where(tok < first, k_all * jnp.exp(jnp.minimum(before[blk] - cum, 0.0)), 0.0))
        q_parts.append(jnp.where((tok >= first) & (tok < first + GLA_SUB), q_rel, 0.0))

    q3 = q_all.reshape(nblk, GLA_SUB, kw)
    k3 = k_all.reshape(nblk, GLA_SUB, kw)
    v3 = v_all.reshape(nblk, GLA_SUB, B_WIDTH)
    sub_i = lax.broadcasted_iota(jnp.int32, (1, GLA_SUB, 1), 1)
    o_diag = [jnp.zeros((nblk, GLA_SUB, B_VAL_DIM), F32) for _ in range(B_HEADS)]
    for jj in range(GLA_SUB):
        pair = q3 * k3[:, jj:jj + 1, :] * jnp.exp(jnp.minimum(cum3 - cum3[:, jj:jj + 1, :], 0.0))
        pair = jnp.where(sub_i >= jj, pair, 0.0)
        for h in range(B_HEADS):
            score = jnp.sum(pair[:, :, h * nk:(h + 1) * nk], axis=-1, keepdims=True)
            o_diag[h] = o_diag[h] + score * v3[:, jj:jj + 1, h * B_VAL_DIM:(h + 1) * B_VAL_DIM]

    for h in range(B_HEADS):
        ks = slice(h * B_KEY_DIM, (h + 1) * B_KEY_DIM)
        vs = slice(h * B_VAL_DIM, (h + 1) * B_VAL_DIM)
        qe = q_all[:, ks] * e_pos[:, ks]
        ke = k_all[:, ks] * e_end[:, ks]
        vh = v_all[:, vs]
        s = s_ref[h]
        q_cat = jnp.concatenate([part[:, ks] for part in q_parts], axis=-1).astype(BF16)
        k_cat = jnp.concatenate([part[:, ks] for part in k_parts], axis=-1).astype(BF16)
        scores = jnp.einsum('id,jd->ij', q_cat, k_cat, preferred_element_type=F32)
        o = _bdot(scores, vh) + _bdot(qe, s) + o_diag[h].reshape(cs, B_VAL_DIM)
        g_col = jnp.sum(eye_k * g_end[:, ks], axis=-1, keepdims=True)
        s_ref[h] = s * g_col + jnp.einsum('jd,jv->dv', ke.astype(BF16), vh.astype(BF16),
                                           preferred_element_type=F32)
        ms = jnp.mean(o * o, axis=-1, keepdims=True)
        gh = g_all[:, vs]
        o_ref[:, vs] = o * lax.rsqrt(ms + NORM_EPS) * nw_ref[:, vs] * (gh * _sigmoid(gh))

    @pl.when(ci == pl.num_programs(1) - 1)
    def _():
        sT_ref[...] = s_ref[...]


def _gla_chunk(u, n_seq, alpha_up_pad, alpha_b, norm_w, s0):
    rows = u.shape[0]
    t = rows // n_seq
    nc = t // CHUNK
    return pl.pallas_call(
        _gla_chunk_kernel,
        grid=(n_seq, nc),
        in_specs=[
            pl.BlockSpec((CHUNK, A_PROJ), lambda b, c: (b * nc + c, 1)),
            pl.BlockSpec((LANES, B_KEY_WIDTH), lambda b, c: (0, 0)),
            pl.BlockSpec((1, B_KEY_WIDTH), lambda b, c: (0, 0)),
            pl.BlockSpec((1, B_WIDTH), lambda b, c: (0, 0)),
            pl.BlockSpec((None, B_HEADS, B_KEY_DIM, B_VAL_DIM), lambda b, c: (b, 0, 0, 0)),
        ],
        out_specs=[
            pl.BlockSpec((CHUNK, B_WIDTH), lambda b, c: (b * nc + c, 0)),
            pl.BlockSpec((None, B_HEADS, B_KEY_DIM, B_VAL_DIM), lambda b, c: (b, 0, 0, 0)),
        ],
        out_shape=[jax.ShapeDtypeStruct((rows, B_WIDTH), F32),
                   jax.ShapeDtypeStruct((n_seq, B_HEADS, B_KEY_DIM, B_VAL_DIM), F32)],
        scratch_shapes=[pltpu.VMEM((B_HEADS, B_KEY_DIM, B_VAL_DIM), F32)],
        compiler_params=_params("arbitrary", "arbitrary"),
        name="gla_chunk",
    )(u, alpha_up_pad, alpha_b, norm_w, s0)


def _gla_step_kernel(u_ref, aup_ref, ab_ref, nw_ref, s_ref, o_ref, so_ref, la_ref):
    nb = u_ref.shape[0]
    nk = B_KEY_DIM
    off_k, off_v, off_g = B_KEY_WIDTH, 2 * B_KEY_WIDTH, 2 * B_KEY_WIDTH + B_WIDTH
    off_a = 2 * B_KEY_WIDTH + 2 * B_WIDTH
    la_ref[...] = _gla_log_alpha(u_ref[:, off_a:off_a + LANES], aup_ref, ab_ref)
    eye = (lax.broadcasted_iota(jnp.int32, (nk, nk), 0) == lax.broadcasted_iota(jnp.int32, (nk, nk), 1)).astype(F32)
    col = lambda x: jnp.sum(eye * x, axis=-1, keepdims=True)

    def body(n, carry):
        row = pl.ds(n, 1)
        ur = u_ref[row, :]
        dec_all = jnp.exp(la_ref[row, :])
        outs = []
        for h in range(B_HEADS):
            q = ur[:, h * nk:(h + 1) * nk] * (B_KEY_DIM ** -0.5)
            k = ur[:, off_k + h * nk:off_k + (h + 1) * nk]
            v = ur[:, off_v + h * B_VAL_DIM:off_v + (h + 1) * B_VAL_DIM]
            g = ur[:, off_g + h * B_VAL_DIM:off_g + (h + 1) * B_VAL_DIM]
            dec = dec_all[:, h * nk:(h + 1) * nk]
            s = s_ref[n, h]
            qk = jnp.sum(q * k, axis=-1, keepdims=True)
            o = qk * v + jnp.sum(col(q * dec) * s, axis=0, keepdims=True)
            so_ref[n, h] = s * col(dec) + col(k) * v
            ms = jnp.mean(o * o, axis=-1, keepdims=True)
            outs.append(o * lax.rsqrt(ms + NORM_EPS) * nw_ref[:, h * B_VAL_DIM:(h + 1) * B_VAL_DIM]
                        * (g * _sigmoid(g)))
        o_ref[row, :] = jnp.concatenate(outs, axis=-1)
        return carry

    lax.fori_loop(0, nb, body, 0)


def _gla_step(u, alpha_up_pad, alpha_b, norm_w, states, j, nb):
    n = u.shape[0]
    return pl.pallas_call(
        _gla_step_kernel,
        grid=(n // nb,),
        in_specs=[pl.BlockSpec((nb, A_PROJ), lambda i: (i, 1)),
                  pl.BlockSpec((LANES, B_KEY_WIDTH), lambda i: (0, 0)),
                  pl.BlockSpec((1, B_KEY_WIDTH), lambda i: (0, 0)),
                  pl.BlockSpec((1, B_WIDTH), lambda i: (0, 0)),
                  pl.BlockSpec((None, nb, B_HEADS, B_KEY_DIM, B_VAL_DIM), lambda i: (j, i, 0, 0, 0))],
        out_specs=[pl.BlockSpec((nb, B_WIDTH), lambda i: (i, 0)),
                   pl.BlockSpec((nb, B_HEADS, B_KEY_DIM, B_VAL_DIM), lambda i: (i, 0, 0, 0))],
        out_shape=[jax.ShapeDtypeStruct((n, B_WIDTH), F32), jax.ShapeDtypeStruct(states.shape[1:], F32)],
        scratch_shapes=[pltpu.VMEM((nb, B_KEY_WIDTH), F32)],
        compiler_params=_params("arbitrary"),
        name="gla_step",
    )(u, alpha_up_pad, alpha_b, norm_w, states)


def _lru_gates(xc, wa_ref, wx_ref, ba_ref, bx_ref, lam_ref):
    ra, ix = [], []
    for nb in range(C_BLOCKS):
        xb = xc[:, nb * C_BLOCK:(nb + 1) * C_BLOCK].astype(BF16)
        ra.append(jnp.dot(xb, wa_ref[nb].astype(BF16), preferred_element_type=F32))
        ix.append(jnp.dot(xb, wx_ref[nb].astype(BF16), preferred_element_type=F32))
    r = _sigmoid(jnp.concatenate(ra, axis=-1) + ba_ref[...])
    i_g = _sigmoid(jnp.concatenate(ix, axis=-1) + bx_ref[...])
    log_a = -C_POW * r * _softplus(-lam_ref[...])
    a = jnp.exp(log_a)
    b = jnp.sqrt(1.0 - a * a) * (i_g * xc)
    return a, b


def _lru_seq_kernel(u_ref, cp_ref, h0_ref, cw_ref, cb_ref, wa_ref, wx_ref, ba_ref, bx_ref, lam_ref,
                    y_ref, tail_ref, hT_ref, carry_ref, h_ref, a_s, b_s):
    ti = pl.program_id(1)

    @pl.when(ti == 0)
    def _():
        carry_ref[...] = cp_ref[...]
        h_ref[...] = h0_ref[...]

    gate = u_ref[:, 0:C_WIDTH]
    xb = u_ref[:, C_WIDTH:2 * C_WIDTH]
    tm = xb.shape[0]
    carry = carry_ref[...]
    row8 = lax.broadcasted_iota(jnp.int32, (8, C_WIDTH), 0)
    xc = cb_ref[...] + cw_ref[C_CONV - 1:C_CONV, :] * xb
    for s in range(1, C_CONV):
        rolled = pltpu.roll(xb, s, axis=0)
        head = jnp.where(row8 < s, pltpu.roll(carry, s, axis=0), rolled[0:8, :])
        shifted = jnp.concatenate([head, rolled[8:, :]], axis=0)
        xc = xc + cw_ref[C_CONV - 1 - s:C_CONV - s, :] * shifted
    carry_ref[...] = xb[tm - 8:tm, :]
    tail_ref[...] = xb[tm - 8:tm, :]

    a, b = _lru_gates(xc, wa_ref, wx_ref, ba_ref, bx_ref, lam_ref)
    a_s[...] = a
    b_s[...] = b

    def body(t, h):
        h = a_s[pl.ds(t, 1), :] * h + b_s[pl.ds(t, 1), :]
        b_s[pl.ds(t, 1), :] = h
        return h

    h_last = lax.fori_loop(0, tm, body, h_ref[...], unroll=8)
    h_ref[...] = h_last
    hT_ref[...] = h_last
    y_ref[...] = _gelu_tanh(gate) * b_s[...]


def _lru_seq(u, n_seq, conv_prev8, h0, conv_w, conv_b, wa, wx, ba, bx, lam, tm):
    rows = u.shape[0]
    per_seq = rows // n_seq // tm
    const = lambda shape: pl.BlockSpec(shape, lambda b, t: (0,) * len(shape))
    return pl.pallas_call(
        _lru_seq_kernel,
        grid=(n_seq, per_seq),
        in_specs=[
            pl.BlockSpec((tm, 2 * C_WIDTH), lambda b, t: (b * per_seq + t, 0)),
            pl.BlockSpec((None, 8, C_WIDTH), lambda b, t: (b, 0, 0)),
            pl.BlockSpec((None, 1, C_WIDTH), lambda b, t: (b, 0, 0)),
            const((C_CONV, C_WIDTH)), const((1, C_WIDTH)),
            const((C_BLOCKS, C_BLOCK, C_BLOCK)), const((C_BLOCKS, C_BLOCK, C_BLOCK)),
            const((1, C_WIDTH)), const((1, C_WIDTH)), const((1, C_WIDTH)),
        ],
        out_specs=[
            pl.BlockSpec((tm, C_WIDTH), lambda b, t: (b * per_seq + t, 0)),
            pl.BlockSpec((None, 8, C_WIDTH), lambda b, t: (b, 0, 0)),
            pl.BlockSpec((None, 1, C_WIDTH), lambda b, t: (b, 0, 0)),
        ],
        out_shape=[jax.ShapeDtypeStruct((rows, C_WIDTH), F32),
                   jax.ShapeDtypeStruct((n_seq, 8, C_WIDTH), F32),
                   jax.ShapeDtypeStruct((n_seq, 1, C_WIDTH), F32)],
        scratch_shapes=[pltpu.VMEM((8, C_WIDTH), F32), pltpu.VMEM((1, C_WIDTH), F32),
                        pltpu.VMEM((tm, C_WIDTH), F32), pltpu.VMEM((tm, C_WIDTH), F32)],
        compiler_params=_params("arbitrary", "arbitrary"),
        name="lru_seq",
    )(u, conv_prev8, h0, conv_w, conv_b, wa, wx, ba, bx, lam)


def _lru_step_kernel(u_ref, cp_ref, h0_ref, cw_ref, cb_ref, wa_ref, wx_ref, ba_ref, bx_ref, lam_ref,
                     y_ref, cn_ref, h_ref):
    gate = u_ref[:, 0:C_WIDTH]
    xb = u_ref[:, C_WIDTH:2 * C_WIDTH]
    xc = cb_ref[...] + cw_ref[C_CONV - 1:C_CONV, :] * xb
    for i in range(C_CONV - 1):
        xc = xc + cw_ref[i:i + 1, :] * cp_ref[i]
    a, b = _lru_gates(xc, wa_ref, wx_ref, ba_ref, bx_ref, lam_ref)
    h = a * h0_ref[...] + b
    h_ref[...] = h
    y_ref[...] = _gelu_tanh(gate) * h
    for i in range(C_CONV - 2):
        cn_ref[i] = cp_ref[i + 1]
    cn_ref[C_CONV - 2] = xb


def _lru_step(u, conv_states, h0, j, conv_w, conv_b, wa, wx, ba, bx, lam):
    rows = u.shape[0]
    full = lambda shape: pl.BlockSpec(shape, lambda i: (0,) * len(shape))
    return pl.pallas_call(
        _lru_step_kernel,
        grid=(1,),
        in_specs=[full((rows, 2 * C_WIDTH)),
                  pl.BlockSpec((None, C_CONV - 1, rows, C_WIDTH), lambda i: (j, 0, 0, 0)),
                  full((rows, C_WIDTH)),
                  full((C_CONV, C_WIDTH)), full((1, C_WIDTH)),
                  full((C_BLOCKS, C_BLOCK, C_BLOCK)), full((C_BLOCKS, C_BLOCK, C_BLOCK)),
                  full((1, C_WIDTH)), full((1, C_WIDTH)), full((1, C_WIDTH))],
        out_specs=[full((rows, C_WIDTH)), full((C_CONV - 1, rows, C_WIDTH)), full((rows, C_WIDTH))],
        out_shape=[jax.ShapeDtypeStruct((rows, C_WIDTH), F32),
                   jax.ShapeDtypeStruct((C_CONV - 1, rows, C_WIDTH), F32),
                   jax.ShapeDtypeStruct((rows, C_WIDTH), F32)],
        compiler_params=_params("arbitrary"),
        name="lru_step",
    )(u, conv_states, h0, conv_w, conv_b, wa, wx, ba, bx, lam)


def _pad_rows(w, first, total):
    return jnp.zeros((total, w.shape[1]), w.dtype).at[first:first + w.shape[0]].set(w)


def _trunk(x, mod, n_seq, seq_len, st_shift, st_wkv, st_gla, st_conv, st_lru, wts, mxu, tm, tf):
    (w_ffn_up, w_ffn_down, w_in_even, w_out_even, a_mu, a_w_up, a_a_up, a_g_up, a_vec, b_alpha_up,
     b_alpha_b, b_norm_w, w_in_odd, w_out_odd, c_conv_w, c_conv_b, c_wa, c_ba, c_wx, c_bx, c_lam,
     final_norm_w) = wts
    rows = x.shape[0]
    prompt = seq_len > 1
    tn_even, tn_odd = 13 * LANES, 8 * LANES
    seg = jnp.kron(jnp.eye(A_HEADS, dtype=F32), jnp.ones((A_HEAD_DIM, A_HEAD_DIM), F32)).astype(BF16)
    out_shift, out_wkv, out_gla, out_conv, out_lru = [], [], [], [], []
    for l in range(DEPTH):
        j = l // 2
        if prompt:
            x = _ffn(x, mod, l, 0, mxu['ffn', l, 0], tm, tf)
        else:
            x, mxu['ffn', l, 0] = _ffn(x, mod, l, 0, (w_ffn_up, w_ffn_down), tm, tf)
        if l % 2 == 0:
            if prompt:
                u = _inproj(x, mod, l, mxu['in', l], None, tm, tn_even)
            else:
                u, mxu['in', l] = _inproj(x, mod, l, jnp.swapaxes(w_in_even, 1, 2), j, tm, tn_even,
                                          transposed=True)
            wup_pad = _pad_rows(a_w_up[j], 0, LANES)
            aup_pad = _pad_rows(a_a_up[j], A_DECAY_RANK, LANES)
            vec8 = _pad_rows(a_vec[j], 0, 8)
            alpha_pad = _pad_rows(b_alpha_up[j], 0, LANES)
            alpha_b = b_alpha_b[j].reshape(1, B_KEY_WIDTH)
            mu = a_mu[j].reshape(1, A_PROJ)
            norm_w = b_norm_w[j].reshape(1, B_WIDTH)
            if prompt:
                r, lw, k2, v, kk, al, g = _rwkv_prep(u, st_shift[j].reshape(n_seq, 1, A_PROJ), mu, wup_pad,
                                                     aup_pad, a_g_up[j], vec8, seg, n_seq, 256, True)
                y_a, h_t = _rwkv_chunk(r, lw, k2, v, kk, al, g, vec8, n_seq, _pair_states(st_wkv[j]))
                s_wkv = _unpair_states(h_t)
                s_shift = u.reshape(n_seq, seq_len, EVEN_PROJ)[:, -1, :A_PROJ]
                y_b, s_gla = _gla_chunk(u, n_seq, alpha_pad, alpha_b, norm_w, st_gla[j])
            else:
                r, lw, k2, v, kk, al, g, bonus = _rwkv_prep(u, st_shift[j], mu, wup_pad, aup_pad, a_g_up[j],
                                                            vec8, seg, rows, rows, False)
                y_raw, s_wkv_t = _rwkv_step(r, lw, k2, v, kk, al, jnp.transpose(st_wkv, (0, 2, 3, 4, 1)), j)
                s_wkv = jnp.transpose(s_wkv_t, (3, 0, 1, 2))
                y_a = _rwkv_post(y_raw, bonus, g, vec8, seg, rows)
                s_shift = u[:, :A_PROJ]
                y_b, s_gla = _gla_step(u, alpha_pad, alpha_b, norm_w, st_gla, j, 8)
            if prompt:
                x = _outproj([y_a, y_b], mxu['out', l], None, x, mod, l, min(tm, 512))
            else:
                x, mxu['out', l] = _outproj([y_a, y_b], w_out_even, j, x, mod, l, tm)
            out_shift.append(s_shift)
            out_wkv.append(s_wkv)
            out_gla.append(s_gla)
        else:
            if prompt:
                u = _inproj(x, mod, l, mxu['in', l], None, tm, tn_odd)
            else:
                u, mxu['in', l] = _inproj(x, mod, l, w_in_odd, j, tm, tn_odd)
            lru_w = (c_conv_w[j], c_conv_b[j].reshape(1, C_WIDTH), c_wa[j], c_wx[j],
                     c_ba[j].reshape(1, C_WIDTH), c_bx[j].reshape(1, C_WIDTH), c_lam[j].reshape(1, C_WIDTH))
            if prompt:
                cp8 = jnp.concatenate([jnp.zeros((n_seq, 8 - (C_CONV - 1), C_WIDTH), F32), st_conv[j]], axis=1)
                y_c, tail, h_t = _lru_seq(u, n_seq, cp8, st_lru[j].reshape(n_seq, 1, C_WIDTH), *lru_w, 256)
                s_conv = tail[:, 8 - (C_CONV - 1):, :]
                s_lru = h_t.reshape(n_seq, C_WIDTH)
            else:
                y_c, cn, s_lru = _lru_step(u, jnp.swapaxes(st_conv, 1, 2), st_lru[j], j, *lru_w)
                s_conv = jnp.swapaxes(cn, 0, 1)
            if prompt:
                x = _outproj([y_c], mxu['out', l], None, x, mod, l, min(tm, 512))
            else:
                x, mxu['out', l] = _outproj([y_c], w_out_odd, j, x, mod, l, tm)
            out_conv.append(s_conv)
            out_lru.append(s_lru)
        final_w = final_norm_w if l == DEPTH - 1 else None
        if prompt:
            x = _ffn(x, mod, l, 1, mxu['ffn', l, 1], tm, tf, final_w=final_w)
        else:
            x, mxu['ffn', l, 1] = _ffn(x, mod, l, 1, (w_ffn_up, w_ffn_down), tm, tf, final_w=final_w)
    return x, (jnp.stack(out_shift), jnp.stack(out_wkv), jnp.stack(out_gla), jnp.stack(out_conv),
               jnp.stack(out_lru))


def kernel(x_prompt, x_sample, c_prompt, c_sample, state_rwkv_shift, state_rwkv_wkv, state_gla, state_conv, state_lru, w_ada, b_ada, w_ffn_up, w_ffn_down, w_in_even, w_out_even, a_mu, a_w_up, a_a_up, a_g_up, a_vec, b_alpha_up, b_alpha_b, b_norm_w, w_in_odd, w_out_odd, c_conv_w, c_conv_b, c_wa, c_ba, c_wx, c_bx, c_lam, final_norm_w):
    wts = (w_ffn_up, w_ffn_down, w_in_even, w_out_even, a_mu, a_w_up, a_a_up, a_g_up, a_vec, b_alpha_up,
           b_alpha_b, b_norm_w, w_in_odd, w_out_odd, c_conv_w, c_conv_b, c_wa, c_ba, c_wx, c_bx, c_lam,
           final_norm_w)
    n_p, t_p, _ = x_prompt.shape
    n_s, t_s, _ = x_sample.shape
    assert t_s == 1

    c_rows = n_p + n_s
    pad = (-c_rows) % 16
    c_all = jnp.concatenate([c_sample, c_prompt, jnp.zeros((pad, D_MODEL), F32)], axis=0)
    mod_all = _ada(c_all, w_ada, b_ada)
    mod_p = _Mod(mod_all[:, :, n_s:c_rows].reshape(DEPTH, N_MOD, n_p, 1, D_MODEL), False, t_p)
    mod_s = _Mod(mod_all, True, 1)

    fresh = lambda s: jnp.zeros((s.shape[0], n_p) + s.shape[2:], s.dtype)
    mxu = {}
    y_s, st_s = _trunk(x_sample.reshape(n_s, D_MODEL), mod_s, n_s, 1,
                       state_rwkv_shift, state_rwkv_wkv, state_gla, state_conv, state_lru, wts, mxu, n_s, 512)
    y_p, st_p = _trunk(x_prompt.reshape(n_p * t_p, D_MODEL), mod_p, n_p, t_p,
                       fresh(state_rwkv_shift), fresh(state_rwkv_wkv), fresh(state_gla),
                       fresh(state_conv), fresh(state_lru), wts, mxu, 1024, 512)
    return (y_p.reshape(n_p, t_p, D_MODEL), y_s.reshape(n_s, 1, D_MODEL), *st_p, *st_s)
```

```python
import functools

import jax
import jax.numpy as jnp
from jax import lax
from jax.experimental import pallas as pl
from jax.experimental.pallas import tpu as pltpu

F32 = jnp.float32
BF16 = jnp.bfloat16

D_MODEL = 2048
DEPTH = 4
N_MOD = 9
D_FF = 5632
NORM_EPS = 1e-6

A_HEADS = 16
A_HEAD_DIM = 64
A_WIDTH = A_HEADS * A_HEAD_DIM
A_DECAY_RANK = 64
A_ICL_RANK = 64
A_GATE_RANK = 128
A_PROJ = 3 * A_WIDTH + A_DECAY_RANK + A_ICL_RANK + A_GATE_RANK
A_GN_EPS = 64e-5

B_HEADS = 4
B_KEY_DIM = 128
B_VAL_DIM = 256
B_KEY_WIDTH = B_HEADS * B_KEY_DIM
B_WIDTH = B_HEADS * B_VAL_DIM
B_ALPHA_RANK = 16
B_TAU = 16.0
B_PROJ = 2 * B_KEY_WIDTH + 2 * B_WIDTH + B_ALPHA_RANK
EVEN_PROJ = A_PROJ + B_PROJ

C_WIDTH = D_MODEL
C_BLOCKS = 8
C_BLOCK = C_WIDTH // C_BLOCKS
C_CONV = 4
C_POW = 8.0

CHUNK = 64
RWKV_STEP_CHUNKS = 2
GLA_STEP_CHUNKS = 2
GLA_SUB = 8
LANES = 128
VMEM_LIMIT = 56 * 1024 * 1024


def _params(*sem):
    return pltpu.CompilerParams(dimension_semantics=sem, vmem_limit_bytes=VMEM_LIMIT)


def _bdot(a, b):
    return jnp.dot(a.astype(BF16), b.astype(BF16), preferred_element_type=F32)


def _split3(x):
    hi = x.astype(BF16)
    r1 = x - hi.astype(F32)
    mid = r1.astype(BF16)
    lo = (r1 - mid.astype(F32)).astype(BF16)
    return hi, mid, lo


def _dot_exact_rhs(a_bf16, x):
    hi, mid, lo = _split3(x)
    f = lambda y: jnp.dot(a_bf16, y, preferred_element_type=F32)
    return f(hi) + f(mid) + f(lo)


def _dot_exact_lhs(x, b_bf16):
    hi, mid, lo = _split3(x)
    f = lambda y: jnp.dot(y, b_bf16, preferred_element_type=F32)
    return f(hi) + f(mid) + f(lo)


def _sigmoid(x):
    return jax.nn.sigmoid(x)


def _softplus(x):
    return jnp.maximum(x, 0.0) + jnp.log1p(jnp.exp(-jnp.abs(x)))


def _gelu_tanh(x):
    return 0.5 * x * (1.0 + jnp.tanh(0.7978845608028654 * (x + 0.044715 * (x * x * x))))


def _norm_mod(x, shift, scale):
    ms = jnp.mean(x * x, axis=-1, keepdims=True)
    return x * lax.rsqrt(ms + NORM_EPS) * (1.0 + scale) + shift


def _ada_kernel(c_ref, w_ref, b_ref, o_ref):
    c = c_ref[...]
    act = (c * _sigmoid(c)).astype(BF16)
    o_ref[...] = jnp.dot(act, w_ref[...].astype(BF16), preferred_element_type=F32) + b_ref[...]


def _ada(c_all, w_ada, b_ada):
    rows = c_all.shape[0]
    tn = 1024
    per = D_MODEL // tn
    b4 = b_ada.reshape(DEPTH, N_MOD * per, 1, tn)
    return pl.pallas_call(
        _ada_kernel,
        grid=(DEPTH, N_MOD * per),
        in_specs=[
            pl.BlockSpec((rows, D_MODEL), lambda l, j: (0, 0)),
            pl.BlockSpec((None, D_MODEL, tn), lambda l, j: (l, 0, j)),
            pl.BlockSpec((None, None, 1, tn), lambda l, j: (l, j, 0, 0)),
        ],
        out_specs=pl.BlockSpec((None, None, rows, tn), lambda l, j: (l, j // per, 0, j % per)),
        out_shape=jax.ShapeDtypeStruct((DEPTH, N_MOD, rows, D_MODEL), F32),
        compiler_params=_params("arbitrary", "arbitrary"),
        name="ada",
    )(c_all, w_ada, b4)


class _Mod:
    def __init__(self, arr, per_row, seq_len):
        self.arr = arr
        self.per_row = per_row
        self.seq_len = seq_len

    def spec(self, layer, m, tm):
        if self.per_row:
            return pl.BlockSpec((None, None, tm, D_MODEL), lambda i, *_: (layer, m, i, 0))
        per_seq = self.seq_len // tm
        return pl.BlockSpec((None, None, None, 1, D_MODEL), lambda i, *_: (layer, m, i // per_seq, 0, 0))


def _ffn_kernel(x_ref, sh_ref, sc_ref, g_ref, wg_ref, wu_ref, wd_ref, *rest, n_j, sub, final, emit):
    rest = list(rest)
    fw_ref = rest.pop(0) if final else None
    o_ref = rest.pop(0)
    w_outs = [rest.pop(0) for _ in range(3)] if emit else None
    h_ref, = rest
    j = pl.program_id(1)
    tm = x_ref.shape[0]

    @pl.when(j == 0)
    def _():
        def body(s, carry):
            rows = pl.ds(pl.multiple_of(s * sub, sub), sub)
            sh = sh_ref[...] if sh_ref.shape[0] == 1 else sh_ref[rows, :]
            sc = sc_ref[...] if sc_ref.shape[0] == 1 else sc_ref[rows, :]
            h_ref[rows, :] = _norm_mod(x_ref[rows, :], sh, sc).astype(BF16)
            o_ref[rows, :] = jnp.zeros((sub, D_MODEL), F32)
            return carry
        lax.fori_loop(0, tm // sub, body, 0, unroll=min(2, tm // sub))

    wg, wu, wd = wg_ref[...], wu_ref[...], wd_ref[...]
    if emit:
        wg, wu, wd = wg.astype(BF16), wu.astype(BF16), wd.astype(BF16)
        for w_out, w in zip(w_outs, (wg, wu, wd)):
            w_out[...] = w
    h = h_ref[...]
    gt = jnp.dot(h, wg, preferred_element_type=F32)
    ut = jnp.dot(h, wu, preferred_element_type=F32)
    act = (gt * _sigmoid(gt) * ut).astype(BF16)
    o_ref[...] += jnp.dot(act, wd, preferred_element_type=F32)

    @pl.when(j == n_j - 1)
    def _():
        def body(s, carry):
            rows = pl.ds(pl.multiple_of(s * sub, sub), sub)
            g = g_ref[...] if g_ref.shape[0] == 1 else g_ref[rows, :]
            y = x_ref[rows, :] + 0.5 * g * o_ref[rows, :]
            if final:
                ms = jnp.mean(y * y, axis=-1, keepdims=True)
                y = y * lax.rsqrt(ms + NORM_EPS) * fw_ref[...]
            o_ref[rows, :] = y
            return carry
        lax.fori_loop(0, tm // sub, body, 0, unroll=min(2, tm // sub))


def _ffn(x, mod, layer, which, weights, tm, tf, final_w=None):
    m_rows = x.shape[0]
    n_j = D_FF // tf
    m0 = 6 * which
    sub = min(tm, 128)
    final = final_w is not None
    emit = len(weights) == 2
    up_spec = pl.BlockSpec((D_MODEL, tf), lambda i, j: (0, j))
    down_spec = pl.BlockSpec((tf, D_MODEL), lambda i, j: (j, 0))
    in_specs = [
        pl.BlockSpec((tm, D_MODEL), lambda i, j: (i, 0), pipeline_mode=pl.Buffered(1)),
        mod.spec(layer, m0, tm), mod.spec(layer, m0 + 1, tm), mod.spec(layer, m0 + 2, tm),
    ]
    if emit:
        w_up, w_down = weights
        in_specs += [pl.BlockSpec((None, None, D_MODEL, tf), lambda i, j: (layer, which, 0, j)),
                     pl.BlockSpec((None, None, D_MODEL, tf), lambda i, j: (layer, which, 0, j + n_j)),
                     pl.BlockSpec((None, None, tf, D_MODEL), lambda i, j: (layer, which, j, 0))]
        args = [x, mod.arr, mod.arr, mod.arr, w_up, w_up, w_down]
    else:
        in_specs += [up_spec, up_spec, down_spec]
        args = [x, mod.arr, mod.arr, mod.arr, *weights]
    if final:
        in_specs.append(pl.BlockSpec((1, D_MODEL), lambda i, j: (0, 0)))
        args.append(final_w.reshape(1, D_MODEL))
    out_specs = [pl.BlockSpec((tm, D_MODEL), lambda i, j: (i, 0))]
    out_shape = [jax.ShapeDtypeStruct((m_rows, D_MODEL), F32)]
    if emit:
        assert m_rows == tm
        out_specs += [up_spec, up_spec, down_spec]
        out_shape += [jax.ShapeDtypeStruct((D_MODEL, D_FF), BF16)] * 2 + [jax.ShapeDtypeStruct((D_FF, D_MODEL), BF16)]
    out = pl.pallas_call(
        functools.partial(_ffn_kernel, n_j=n_j, sub=sub, final=final, emit=emit),
        grid=(m_rows // tm, n_j),
        in_specs=in_specs,
        out_specs=out_specs,
        out_shape=out_shape,
        scratch_shapes=[pltpu.VMEM((tm, D_MODEL), BF16)],
        compiler_params=_params("arbitrary", "arbitrary"),
        name="ffn",
    )(*args)
    return (out[0], tuple(out[1:])) if emit else out[0]


def _inproj_kernel(x_ref, sh_ref, sc_ref, w_ref, o_ref, *rest, sub, emit, transposed):
    h_ref = rest[-1]
    j = pl.program_id(1)
    tm = x_ref.shape[0]

    @pl.when(j == 0)
    def _():
        def body(s, carry):
            rows = pl.ds(pl.multiple_of(s * sub, sub), sub)
            sh = sh_ref[...] if sh_ref.shape[0] == 1 else sh_ref[rows, :]
            sc = sc_ref[...] if sc_ref.shape[0] == 1 else sc_ref[rows, :]
            h_ref[rows, :] = _norm_mod(x_ref[rows, :], sh, sc).astype(BF16)
            return carry
        lax.fori_loop(0, tm // sub, body, 0, unroll=min(2, tm // sub))

    w = w_ref[...]
    if transposed:
        w = w.T
    if emit:
        w = w.astype(BF16)
        rest[0][...] = w
    o_ref[...] = jnp.dot(h_ref[...], w, preferred_element_type=F32)


def _inproj(x, mod, layer, w, widx, tm, tn, transposed=False):
    m_rows = x.shape[0]
    n_out = w.shape[-2] if transposed else w.shape[-1]
    sub = min(tm, 128)
    emit = widx is not None
    if transposed:
        w_spec = pl.BlockSpec((None, tn, D_MODEL), lambda i, j: (widx, j, 0))
    elif emit:
        w_spec = pl.BlockSpec((None, D_MODEL, tn), lambda i, j: (widx, 0, j))
    else:
        w_spec = pl.BlockSpec((D_MODEL, tn), lambda i, j: (0, j))
    out_specs = [pl.BlockSpec((tm, tn), lambda i, j: (i, j))]
    out_shape = [jax.ShapeDtypeStruct((m_rows, n_out), F32)]
    if emit:
        assert m_rows == tm
        out_specs.append(pl.BlockSpec((D_MODEL, tn), lambda i, j: (0, j)))
        out_shape.append(jax.ShapeDtypeStruct((D_MODEL, n_out), BF16))
    out = pl.pallas_call(
        functools.partial(_inproj_kernel, sub=sub, emit=emit, transposed=transposed),
        grid=(m_rows // tm, pl.cdiv(n_out, tn)),
        in_specs=[
            pl.BlockSpec((tm, D_MODEL), lambda i, j: (i, 0)),
            mod.spec(layer, 3, tm), mod.spec(layer, 4, tm),
            w_spec,
        ],
        out_specs=out_specs,
        out_shape=out_shape,
        scratch_shapes=[pltpu.VMEM((tm, D_MODEL), BF16)],
        compiler_params=_params("arbitrary", "arbitrary"),
        name="inproj",
    )(x, mod.arr, mod.arr, w)
    return tuple(out) if emit else out[0]


def _outproj_kernel(*refs, n_in, emit):
    y_refs = refs[:n_in]
    w_refs = refs[n_in:2 * n_in]
    x_ref, g_ref, o_ref = refs[2 * n_in:2 * n_in + 3]
    acc = None
    for k, (y_ref, w_ref) in enumerate(zip(y_refs, w_refs)):
        w = w_ref[...]
        if emit:
            w = w.astype(BF16)
            refs[2 * n_in + 3 + k][...] = w
        part = jnp.dot(y_ref[...].astype(BF16), w, preferred_element_type=F32)
        acc = part if acc is None else acc + part
    o_ref[...] = x_ref[...] + g_ref[...] * acc


def _outproj(ys, w, widx, x, mod, layer, tm):
    m_rows = x.shape[0]
    n_in = len(ys)
    kw = ys[0].shape[1]
    emit = widx is not None
    in_specs = [pl.BlockSpec((tm, kw), lambda i: (i, 0)) for _ in ys]
    if emit:
        assert m_rows == tm
        in_specs += [pl.BlockSpec((None, kw, D_MODEL), lambda i, k=k: (widx, k, 0)) for k in range(n_in)]
        w_args = [w] * n_in
    else:
        in_specs += [pl.BlockSpec((kw, D_MODEL), lambda i: (0, 0))] * n_in
        w_args = list(w)
    in_specs += [pl.BlockSpec((tm, D_MODEL), lambda i: (i, 0)), mod.spec(layer, 5, tm)]
    out_specs = [pl.BlockSpec((tm, D_MODEL), lambda i: (i, 0))]
    out_shape = [jax.ShapeDtypeStruct((m_rows, D_MODEL), F32)]
    if emit:
        out_specs += [pl.BlockSpec((kw, D_MODEL), lambda i: (0, 0))] * n_in
        out_shape += [jax.ShapeDtypeStruct((kw, D_MODEL), BF16)] * n_in
    out = pl.pallas_call(
        functools.partial(_outproj_kernel, n_in=n_in, emit=emit),
        grid=(m_rows // tm,),
        in_specs=in_specs,
        out_specs=out_specs,
        out_shape=out_shape,
        compiler_params=_params("arbitrary"),
        name="outproj",
    )(*ys, *w_args, x, mod.arr)
    return (out[0], tuple(out[1:])) if emit else out[0]


def _rwkv_prep_kernel(u_ref, prev_ref, mu_ref, wup_ref, aup_ref, gup_ref, vec_ref, *rest, seq_mode):
    if seq_mode:
        r_ref, lw_ref, k_ref, v_ref, kk_ref, a_ref, g_ref, carry_ref = rest
    else:
        seg_ref, r_ref, lw_ref, k_ref, v_ref, kk_ref, a_ref, g_ref, bonus_ref = rest
    u = u_ref[...]
    tm = u.shape[0]
    if seq_mode:
        @pl.when(pl.program_id(1) == 0)
        def _():
            carry_ref[...] = prev_ref[...]
        rolled = pltpu.roll(u, 1, axis=0)
        row = lax.broadcasted_iota(jnp.int32, u.shape, 0)
        prev = jnp.where(row == 0, carry_ref[...], rolled)
        carry_ref[...] = u[tm - 1:tm, :]
    else:
        prev = prev_ref[...]
    xs = u + mu_ref[...] * (prev - u)
    r = xs[:, 0:A_WIDTH]
    k = xs[:, A_WIDTH:2 * A_WIDTH]
    v = xs[:, 2 * A_WIDTH:3 * A_WIDTH]
    wa_lo = xs[:, 3 * A_WIDTH:3 * A_WIDTH + LANES]
    g_lo = xs[:, 3 * A_WIDTH + LANES:]
    w0, a0, k_k, k_a, r_k = (vec_ref[i:i + 1, :] for i in range(5))
    w_raw = -_softplus(-(w0 + _bdot(jnp.tanh(wa_lo), wup_ref[...]))) - 0.5
    lw = -jnp.exp(w_raw)
    a = _sigmoid(a0 + _bdot(wa_lo, aup_ref[...]))
    g = _bdot(_sigmoid(g_lo), gup_ref[...])
    kk = k * k_k
    k2 = k * (1.0 + (a - 1.0) * k_a)
    if not seq_mode:
        seg = seg_ref[...]
        kk = kk * lax.rsqrt(jnp.maximum(_dot_exact_lhs(kk * kk, seg), 1e-24))
        bonus_ref[...] = _dot_exact_lhs(r * k2 * r_k, seg) * v
    r_ref[...] = r
    lw_ref[...] = lw
    k_ref[...] = k2
    v_ref[...] = v
    kk_ref[...] = kk
    a_ref[...] = a
    g_ref[...] = g


def _rwkv_prep(u, prev, mu, wup_pad, aup_pad, g_up, vec, seg, n_seq, tm, seq_mode):
    rows = u.shape[0]
    per_seq = rows // n_seq // tm if seq_mode else 1
    grid = (n_seq, per_seq) if seq_mode else (rows // tm, 1)
    rowmap = (lambda b, t: (b * per_seq + t, 0)) if seq_mode else (lambda b, t: (b, 0))
    prev_spec = (pl.BlockSpec((None, 1, A_PROJ), lambda b, t: (b, 0, 0)) if seq_mode
                 else pl.BlockSpec((tm, A_PROJ), rowmap))
    const = lambda shape: pl.BlockSpec(shape, lambda b, t: (0,) * len(shape))
    out = jax.ShapeDtypeStruct((rows, A_WIDTH), F32)
    in_specs = [pl.BlockSpec((tm, A_PROJ), rowmap), prev_spec,
                const((1, A_PROJ)), const((LANES, A_WIDTH)), const((LANES, A_WIDTH)),
                const((A_GATE_RANK, A_WIDTH)), const((8, A_WIDTH))]
    args = [u, prev, mu, wup_pad, aup_pad, g_up, vec]
    if not seq_mode:
        in_specs.append(const((A_WIDTH, A_WIDTH)))
        args.append(seg)
    n_out = 7 if seq_mode else 8
    return pl.pallas_call(
        functools.partial(_rwkv_prep_kernel, seq_mode=seq_mode),
        grid=grid,
        in_specs=in_specs,
        out_specs=[pl.BlockSpec((tm, A_WIDTH), rowmap)] * n_out,
        out_shape=[out] * n_out,
        scratch_shapes=[pltpu.VMEM((1, A_PROJ), F32)] if seq_mode else [],
        compiler_params=_params("arbitrary", "arbitrary"),
        name="rwkv_prep",
    )(*args)


def _rwkv_post_kernel(y_ref, bonus_ref, g_ref, vec_ref, seg_ref, o_ref):
    y = y_ref[...]
    seg = seg_ref[...]
    inv_n = 1.0 / A_HEAD_DIM
    yc = y - _dot_exact_lhs(y, seg) * inv_n
    var = _dot_exact_lhs(yc * yc, seg) * inv_n
    yn = yc * lax.rsqrt(var + A_GN_EPS) * vec_ref[5:6, :] + vec_ref[6:7, :]
    o_ref[...] = (yn + bonus_ref[...]) * g_ref[...]


def _rwkv_post(y, bonus, g, vec, seg, tm):
    rows = y.shape[0]
    spec = pl.BlockSpec((tm, A_WIDTH), lambda i: (i, 0))
    return pl.pallas_call(
        _rwkv_post_kernel,
        grid=(rows // tm,),
        in_specs=[spec, spec, spec,
                  pl.BlockSpec((8, A_WIDTH), lambda i: (0, 0)),
                  pl.BlockSpec((A_WIDTH, A_WIDTH), lambda i: (0, 0))],
        out_specs=spec,
        out_shape=jax.ShapeDtypeStruct((rows, A_WIDTH), F32),
        compiler_params=_params("arbitrary"),
        name="rwkv_post",
    )(y, bonus, g, vec, seg)


def _rwkv_chunk_kernel(r_ref, lw_ref, k_ref, v_ref, kk_ref, a_ref, g_ref, vec_ref, h0_ref, y_ref, hT_ref, h_ref):
    ci = pl.program_id(1)

    @pl.when(ci == 0)
    def _():
        h_ref[...] = h0_ref[...]

    cs = CHUNK
    n2 = 2 * cs
    npair = A_WIDTH // LANES
    pairs = lambda x: jnp.stack([x[:, p * LANES:(p + 1) * LANES] for p in range(npair)])
    vec = vec_ref[...]
    vrow = lambda i: pairs(vec[i:i + 1, :])
    lane = lax.broadcasted_iota(jnp.int32, (1, 1, LANES), 2)
    m0 = (lane < A_HEAD_DIM).astype(F32)
    m1 = 1.0 - m0
    split = lambda x: jnp.concatenate([x * m0, x * m1], axis=1)
    dup = lambda x: jnp.concatenate([x, x], axis=1)
    own = jnp.concatenate([jnp.broadcast_to(m0, (1, cs, LANES)), jnp.broadcast_to(m1, (1, cs, LANES))], axis=1)
    bf = lambda x: x.astype(BF16)

    def bmm(x, y):
        return jnp.einsum('hab,hbc->hac', x, y, preferred_element_type=F32)

    def bmm_nt(x, y):
        return jnp.einsum('han,hbn->hab', x, y, preferred_element_type=F32)

    def bmm_tn(x, y):
        return jnp.einsum('hca,hcb->hab', x, y, preferred_element_type=F32)

    row = lax.broadcasted_iota(jnp.int32, (cs, cs), 0)
    col = lax.broadcasted_iota(jnp.int32, (cs, cs), 1)
    tri = (col <= row).astype(BF16)
    row2 = lax.broadcasted_iota(jnp.int32, (n2, n2), 0)
    col2 = lax.broadcasted_iota(jnp.int32, (n2, n2), 1)
    same = (row2 >= cs) == (col2 >= cs)
    tok_r = jnp.where(row2 >= cs, row2 - cs, row2)
    tok_c = jnp.where(col2 >= cs, col2 - cs, col2)
    strict = jnp.where(same, (tok_c < tok_r).astype(F32), 0.0)
    incl = jnp.where(same, (tok_c <= tok_r).astype(F32), 0.0)
    eye2 = (row2 == col2).astype(F32)
    nrow = lax.broadcasted_iota(jnp.int32, (LANES, LANES), 0)
    ncol = lax.broadcasted_iota(jnp.int32, (LANES, LANES), 1)
    eye_n = (nrow == ncol).astype(F32)

    h = h_ref[...]
    for sub in range(r_ref.shape[0] // cs):
        rows = slice(sub * cs, (sub + 1) * cs)
        lw_all = lw_ref[rows, :]
        c = pairs(_dot_exact_rhs(tri, lw_all))
        lw = pairs(lw_all)
        c_end = c[:, cs - 1:cs, :]
        e_pos = dup(jnp.exp(c))
        e_neg = dup(jnp.exp(-c))
        e_prev = dup(jnp.exp(c - lw))
        e_end = dup(jnp.exp(c_end - c))
        g_end = jnp.exp(c_end)

        kk = split(pairs(kk_ref[rows, :]))
        kk = kk * lax.rsqrt(jnp.maximum(jnp.sum(kk * kk, axis=-1, keepdims=True), 1e-24))
        b_vec = kk * dup(pairs(a_ref[rows, :]))
        r2 = split(pairs(r_ref[rows, :]))
        k2 = split(pairs(k_ref[rows, :]))
        v2 = split(pairs(v_ref[rows, :]))
        bonus = jnp.sum(r2 * k2 * vrow(4), axis=-1, keepdims=True) * v2
        rt = r2 * e_pos
        at = bf(-kk * e_prev)
        bt = bf(b_vec * e_neg)
        kt = bf(k2 * e_neg)
        bh = bf(b_vec * e_end)
        kh = bf(k2 * e_end)
        v2b = bf(v2)

        gram = bmm_nt(jnp.concatenate([at, bf(rt)], axis=1), jnp.concatenate([bt, kt], axis=1))
        a_ab = gram[:, :n2, :n2] * strict
        a_ak = gram[:, :n2, n2:] * strict
        m_b = gram[:, n2:, :n2] * incl
        m_k = gram[:, n2:, n2:] * incl

        tinv = eye2 + a_ab
        a_b = bf(a_ab)
        apow = bmm(a_b, a_b)
        span = 2
        while 2 * span < cs:
            ap_b = bf(apow)
            both = bmm(jnp.concatenate([bf(tinv), ap_b], axis=1), ap_b)
            tinv = tinv + both[:, :n2]
            apow = both[:, n2:]
            span *= 2
        tinv = tinv + bmm(bf(tinv), bf(apow))

        akv = bmm(bf(a_ak), v2b)
        pq = bf(bmm(bf(tinv), jnp.concatenate([at, bf(akv)], axis=-1)))
        mpq = bmm(bf(m_b), pq)
        p2 = rt + mpq[..., :LANES]
        y0 = mpq[..., LANES:] + bmm(bf(m_k), v2b)
        tpq = bmm_tn(bh, pq)
        gm = eye_n * g_end + tpq[..., :LANES]
        hadd = tpq[..., LANES:] + bmm_tn(kh, v2b)

        yh = bmm(jnp.concatenate([bf(p2), bf(gm)], axis=1), bf(h))
        y2 = yh[:, :n2] + y0
        h = yh[:, n2:] + hadd

        inv_n = 1.0 / A_HEAD_DIM
        yc = (y2 - jnp.sum(y2, axis=-1, keepdims=True) * inv_n) * own
        var = jnp.sum(yc * yc, axis=-1, keepdims=True) * inv_n
        z = yc * lax.rsqrt(var + A_GN_EPS) * vrow(5) + vrow(6) * own + bonus
        out = (z[:, :cs] + z[:, cs:]) * pairs(g_ref[rows, :])
        for p in range(npair):
            y_ref[rows, p * LANES:(p + 1) * LANES] = out[p]

    h_ref[...] = h

    @pl.when(ci == pl.num_programs(1) - 1)
    def _():
        hT_ref[...] = h


def _rwkv_chunk(r, lw, k, v, kk, a, g, vec, n_seq, h0):
    rows = r.shape[0]
    step_rows = RWKV_STEP_CHUNKS * CHUNK
    nc = rows // n_seq // step_rows
    npair = A_WIDTH // LANES
    spec = pl.BlockSpec((step_rows, A_WIDTH), lambda b, c: (b * nc + c, 0))
    hspec = pl.BlockSpec((None, npair, LANES, LANES), lambda b, c: (b, 0, 0, 0))
    return pl.pallas_call(
        _rwkv_chunk_kernel,
        grid=(n_seq, nc),
        in_specs=[spec] * 7 + [pl.BlockSpec((8, A_WIDTH), lambda b, c: (0, 0)), hspec],
        out_specs=[spec, hspec],
        out_shape=[jax.ShapeDtypeStruct((rows, A_WIDTH), F32),
                   jax.ShapeDtypeStruct((n_seq, npair, LANES, LANES), F32)],
        scratch_shapes=[pltpu.VMEM((npair, LANES, LANES), F32)],
        compiler_params=_params("arbitrary", "arbitrary"),
        name="rwkv_chunk",
    )(r, lw, k, v, kk, a, g, vec, h0)


def _pair_states(s):
    n = s.shape[0]
    nd = A_HEAD_DIM
    h = jnp.swapaxes(s, -1, -2).reshape(n, A_HEADS // 2, 2, nd, nd)
    out = jnp.zeros((n, A_HEADS // 2, 2 * nd, 2 * nd), s.dtype)
    return out.at[:, :, :nd, :nd].set(h[:, :, 0]).at[:, :, nd:, nd:].set(h[:, :, 1])


def _unpair_states(hp):
    n = hp.shape[0]
    nd = A_HEAD_DIM
    h = jnp.stack([hp[:, :, :nd, :nd], hp[:, :, nd:, nd:]], axis=2).reshape(n, A_HEADS, nd, nd)
    return jnp.swapaxes(h, -1, -2)


def _rwkv_step_kernel(r_ref, lw_ref, k_ref, v_ref, kk_ref, a_ref, s_ref, y_ref, so_ref,
                      r_t, w_t, k_t, v_t, nkk_t, b_t, y_t):
    h = pl.program_id(0)
    nd = A_HEAD_DIM

    @pl.when(h == 0)
    def _():
        kk = kk_ref[...]
        r_t[...] = r_ref[...].T
        w_t[...] = jnp.exp(lw_ref[...]).T
        k_t[...] = k_ref[...].T
        v_t[...] = v_ref[...].T
        nkk_t[...] = (-kk).T
        b_t[...] = (kk * a_ref[...]).T

    first = pl.multiple_of(h * nd, nd)
    rows = pl.ds(first, nd)
    r, w, k, nkk, b = r_t[rows, :], w_t[rows, :], k_t[rows, :], nkk_t[rows, :], b_t[rows, :]

    def body(i, carry):
        s = s_ref[i]
        s_kk = jnp.sum(s * nkk, axis=0, keepdims=True)
        s_new = s * w + s_kk * b + v_t[pl.ds(first + i, 1), :] * k
        so_ref[i] = s_new
        y_t[pl.ds(first + i, 1), :] = jnp.sum(s_new * r, axis=0, keepdims=True)
        return carry

    lax.fori_loop(0, nd, body, 0, unroll=4)

    @pl.when(h == pl.num_programs(0) - 1)
    def _():
        y_ref[...] = y_t[...].T


def _rwkv_step(r, lw, k, v, kk, a, states_t, j):
    n = r.shape[0]
    nd = A_HEAD_DIM
    rspec = pl.BlockSpec((n, A_WIDTH), lambda h: (0, 0))
    return pl.pallas_call(
        _rwkv_step_kernel,
        grid=(A_HEADS,),
        in_specs=[rspec] * 6 + [pl.BlockSpec((None, None, nd, nd, n), lambda h: (j, h, 0, 0, 0))],
        out_specs=[rspec, pl.BlockSpec((None, nd, nd, n), lambda h: (h, 0, 0, 0))],
        out_shape=[jax.ShapeDtypeStruct((n, A_WIDTH), F32), jax.ShapeDtypeStruct(states_t.shape[1:], F32)],
        scratch_shapes=[pltpu.VMEM((A_WIDTH, n), F32)] * 7,
        compiler_params=_params("arbitrary"),
        name="rwkv_step",
    )(r, lw, k, v, kk, a, states_t)


def _gla_log_alpha(a128, aup_ref, ab_ref):
    lane = lax.broadcasted_iota(jnp.int32, a128.shape, 1)
    a_lo = jnp.where(lane < B_ALPHA_RANK, a128, 0.0)
    x = _bdot(a_lo, aup_ref[...]) + ab_ref[...]
    return -_softplus(-x) * (1.0 / B_TAU)


def _gla_block(u_ref, aup_ref, ab_ref, nw_ref, o_ref, s_ref):
    ub = u_ref[...]
    cs = ub.shape[0]
    q_all = ub[:, 0:B_KEY_WIDTH] * (B_KEY_DIM ** -0.5)
    k_all = ub[:, B_KEY_WIDTH:2 * B_KEY_WIDTH]
    v_all = ub[:, 2 * B_KEY_WIDTH:2 * B_KEY_WIDTH + B_WIDTH]
    g_all = ub[:, 2 * B_KEY_WIDTH + B_WIDTH:2 * B_KEY_WIDTH + 2 * B_WIDTH]
    a128 = ub[:, 2 * B_KEY_WIDTH + 2 * B_WIDTH:2 * B_KEY_WIDTH + 2 * B_WIDTH + LANES]
    log_a = _gla_log_alpha(a128, aup_ref, ab_ref)

    row = lax.broadcasted_iota(jnp.int32, (cs, cs), 0)
    col = lax.broadcasted_iota(jnp.int32, (cs, cs), 1)
    cum = _dot_exact_rhs((col <= row).astype(BF16), log_a)
    e_pos = jnp.exp(cum)
    cum_end = cum[cs - 1:cs, :]
    e_end = jnp.exp(cum_end - cum)
    g_end = jnp.exp(cum_end)
    nk = B_KEY_DIM
    eye_k = (lax.broadcasted_iota(jnp.int32, (nk, nk), 0) == lax.broadcasted_iota(jnp.int32, (nk, nk), 1)).astype(F32)

    nblk = cs // GLA_SUB
    kw = B_KEY_WIDTH
    cum3 = cum.reshape(nblk, GLA_SUB, kw)
    before = jnp.concatenate([jnp.zeros((1, 1, kw), F32), cum3[:nblk - 1, GLA_SUB - 1:GLA_SUB, :]], axis=0)
    q_rel = q_all * jnp.exp(cum - jnp.broadcast_to(before, (nblk, GLA_SUB, kw)).reshape(cs, kw))
    tok = lax.broadcasted_iota(jnp.int32, (cs, 1), 0)
    q_parts, k_parts = [], []
    for blk in range(1, nblk):
        first = blk * GLA_SUB
        k_parts.append(jnp.where(tok < first, k_all * jnp.exp(jnp.minimum(before[blk] - cum, 0.0)), 0.0))
        q_parts.append(jnp.where((tok >= first) & (tok < first + GLA_SUB), q_rel, 0.0))

    q3 = q_all.reshape(nblk, GLA_SUB, kw)
    k3 = k_all.reshape(nblk, GLA_SUB, kw)
    v3 = v_all.reshape(nblk, GLA_SUB, B_WIDTH)
    sub_i = lax.broadcasted_iota(jnp.int32, (1, GLA_SUB, 1), 1)
    o_diag = [jnp.zeros((nblk, GLA_SUB, B_VAL_DIM), F32) for _ in range(B_HEADS)]
    for jj in range(GLA_SUB):
        pair = q3 * k3[:, jj:jj + 1, :] * jnp.exp(jnp.minimum(cum3 - cum3[:, jj:jj + 1, :], 0.0))
        pair = jnp.where(sub_i >= jj, pair, 0.0)
        for h in range(B_HEADS):
            score = jnp.sum(pair[:, :, h * nk:(h + 1) * nk], axis=-1, keepdims=True)
            o_diag[h] = o_diag[h] + score * v3[:, jj:jj + 1, h * B_VAL_DIM:(h + 1) * B_VAL_DIM]

    for h in range(B_HEADS):
        ks = slice(h * B_KEY_DIM, (h + 1) * B_KEY_DIM)
        vs = slice(h * B_VAL_DIM, (h + 1) * B_VAL_DIM)
        qe = q_all[:, ks] * e_pos[:, ks]
        ke = k_all[:, ks] * e_end[:, ks]
        vh = v_all[:, vs]
        s = s_ref[h]
        q_cat = jnp.concatenate([part[:, ks] for part in q_parts], axis=-1).astype(BF16)
        k_cat = jnp.concatenate([part[:, ks] for part in k_parts], axis=-1).astype(BF16)
        scores = jnp.einsum('id,jd->ij', q_cat, k_cat, preferred_element_type=F32)
        o = _bdot(scores, vh) + _bdot(qe, s) + o_diag[h].reshape(cs, B_VAL_DIM)
        g_col = jnp.sum(eye_k * g_end[:, ks], axis=-1, keepdims=True)
        s_ref[h] = s * g_col + jnp.einsum('jd,jv->dv', ke.astype(BF16), vh.astype(BF16),
                                           preferred_element_type=F32)
        ms = jnp.mean(o * o, axis=-1, keepdims=True)
        gh = g_all[:, vs]
        o_ref[:, vs] = o * lax.rsqrt(ms + NORM_EPS) * nw_ref[:, vs] * (gh * _sigmoid(gh))


def _gla_chunk_kernel(u_ref, aup_ref, ab_ref, nw_ref, s0_ref, o_ref, sT_ref, s_ref):
    ci = pl.program_id(1)

    @pl.when(ci == 0)
    def _():
        s_ref[...] = s0_ref[...]

    for sub in range(u_ref.shape[0] // CHUNK):
        rows = pl.ds(sub * CHUNK, CHUNK)
        _gla_block(u_ref.at[rows, :], aup_ref, ab_ref, nw_ref, o_ref.at[rows, :], s_ref)

    @pl.when(ci == pl.num_programs(1) - 1)
    def _():
        sT_ref[...] = s_ref[...]


def _gla_chunk(u, n_seq, alpha_up_pad, alpha_b, norm_w, s0):
    rows = u.shape[0]
    step_rows = GLA_STEP_CHUNKS * CHUNK
    nc = rows // n_seq // step_rows
    return pl.pallas_call(
        _gla_chunk_kernel,
        grid=(n_seq, nc),
        in_specs=[
            pl.BlockSpec((step_rows, A_PROJ), lambda b, c: (b * nc + c, 1)),
            pl.BlockSpec((LANES, B_KEY_WIDTH), lambda b, c: (0, 0)),
            pl.BlockSpec((1, B_KEY_WIDTH), lambda b, c: (0, 0)),
            pl.BlockSpec((1, B_WIDTH), lambda b, c: (0, 0)),
            pl.BlockSpec((None, B_HEADS, B_KEY_DIM, B_VAL_DIM), lambda b, c: (b, 0, 0, 0)),
        ],
        out_specs=[
            pl.BlockSpec((step_rows, B_WIDTH), lambda b, c: (b * nc + c, 0)),
            pl.BlockSpec((None, B_HEADS, B_KEY_DIM, B_VAL_DIM), lambda b, c: (b, 0, 0, 0)),
        ],
        out_shape=[jax.ShapeDtypeStruct((rows, B_WIDTH), F32),
                   jax.ShapeDtypeStruct((n_seq, B_HEADS, B_KEY_DIM, B_VAL_DIM), F32)],
        scratch_shapes=[pltpu.VMEM((B_HEADS, B_KEY_DIM, B_VAL_DIM), F32)],
        compiler_params=_params("arbitrary", "arbitrary"),
        name="gla_chunk",
    )(u, alpha_up_pad, alpha_b, norm_w, s0)


def _gla_step_kernel(u_ref, aup_ref, ab_ref, nw_ref, s_ref, o_ref, so_ref, la_ref):
    nb = u_ref.shape[0]
    nk = B_KEY_DIM
    off_k, off_v, off_g = B_KEY_WIDTH, 2 * B_KEY_WIDTH, 2 * B_KEY_WIDTH + B_WIDTH
    off_a = 2 * B_KEY_WIDTH + 2 * B_WIDTH
    la_ref[...] = _gla_log_alpha(u_ref[:, off_a:off_a + LANES], aup_ref, ab_ref)
    eye = (lax.broadcasted_iota(jnp.int32, (nk, nk), 0) == lax.broadcasted_iota(jnp.int32, (nk, nk), 1)).astype(F32)
    col = lambda x: jnp.sum(eye * x, axis=-1, keepdims=True)

    def body(n, carry):
        row = pl.ds(n, 1)
        ur = u_ref[row, :]
        dec_all = jnp.exp(la_ref[row, :])
        outs = []
        for h in range(B_HEADS):
            q = ur[:, h * nk:(h + 1) * nk] * (B_KEY_DIM ** -0.5)
            k = ur[:, off_k + h * nk:off_k + (h + 1) * nk]
            v = ur[:, off_v + h * B_VAL_DIM:off_v + (h + 1) * B_VAL_DIM]
            g = ur[:, off_g + h * B_VAL_DIM:off_g + (h + 1) * B_VAL_DIM]
            dec = dec_all[:, h * nk:(h + 1) * nk]
            s = s_ref[n, h]
            qk = jnp.sum(q * k, axis=-1, keepdims=True)
            o = qk * v + jnp.sum(col(q * dec) * s, axis=0, keepdims=True)
            so_ref[n, h] = s * col(dec) + col(k) * v
            ms = jnp.mean(o * o, axis=-1, keepdims=True)
            outs.append(o * lax.rsqrt(ms + NORM_EPS) * nw_ref[:, h * B_VAL_DIM:(h + 1) * B_VAL_DIM]
                        * (g * _sigmoid(g)))
        o_ref[row, :] = jnp.concatenate(outs, axis=-1)
        return carry

    lax.fori_loop(0, nb, body, 0)


def _gla_step(u, alpha_up_pad, alpha_b, norm_w, states, j, nb):
    n = u.shape[0]
    return pl.pallas_call(
        _gla_step_kernel,
        grid=(n // nb,),
        in_specs=[pl.BlockSpec((nb, A_PROJ), lambda i: (i, 1)),
                  pl.BlockSpec((LANES, B_KEY_WIDTH), lambda i: (0, 0)),
                  pl.BlockSpec((1, B_KEY_WIDTH), lambda i: (0, 0)),
                  pl.BlockSpec((1, B_WIDTH), lambda i: (0, 0)),
                  pl.BlockSpec((None, nb, B_HEADS, B_KEY_DIM, B_VAL_DIM), lambda i: (j, i, 0, 0, 0))],
        out_specs=[pl.BlockSpec((nb, B_WIDTH), lambda i: (i, 0)),
                   pl.BlockSpec((nb, B_HEADS, B_KEY_DIM, B_VAL_DIM), lambda i: (i, 0, 0, 0))],
        out_shape=[jax.ShapeDtypeStruct((n, B_WIDTH), F32), jax.ShapeDtypeStruct(states.shape[1:], F32)],
        scratch_shapes=[pltpu.VMEM((nb, B_KEY_WIDTH), F32)],
        compiler_params=_params("arbitrary"),
        name="gla_step",
    )(u, alpha_up_pad, alpha_b, norm_w, states)


def _lru_gates(xc, wa_ref, wx_ref, ba_ref, bx_ref, lam_ref):
    ra, ix = [], []
    for nb in range(C_BLOCKS):
        xb = xc[:, nb * C_BLOCK:(nb + 1) * C_BLOCK].astype(BF16)
        ra.append(jnp.dot(xb, wa_ref[nb].astype(BF16), preferred_element_type=F32))
        ix.append(jnp.dot(xb, wx_ref[nb].astype(BF16), preferred_element_type=F32))
    r = _sigmoid(jnp.concatenate(ra, axis=-1) + ba_ref[...])
    i_g = _sigmoid(jnp.concatenate(ix, axis=-1) + bx_ref[...])
    log_a = -C_POW * r * _softplus(-lam_ref[...])
    a = jnp.exp(log_a)
    b = jnp.sqrt(1.0 - a * a) * (i_g * xc)
    return a, b


def _lru_seq_kernel(u_ref, cp_ref, h0_ref, cw_ref, cb_ref, wa_ref, wx_ref, ba_ref, bx_ref, lam_ref,
                    y_ref, tail_ref, hT_ref, carry_ref, h_ref, a_s, b_s):
    ti = pl.program_id(1)

    @pl.when(ti == 0)
    def _():
        carry_ref[...] = cp_ref[...]
        h_ref[...] = h0_ref[...]

    gate = u_ref[:, 0:C_WIDTH]
    xb = u_ref[:, C_WIDTH:2 * C_WIDTH]
    tm = xb.shape[0]
    carry = carry_ref[...]
    row8 = lax.broadcasted_iota(jnp.int32, (8, C_WIDTH), 0)
    xc = cb_ref[...] + cw_ref[C_CONV - 1:C_CONV, :] * xb
    for s in range(1, C_CONV):
        rolled = pltpu.roll(xb, s, axis=0)
        head = jnp.where(row8 < s, pltpu.roll(carry, s, axis=0), rolled[0:8, :])
        shifted = jnp.concatenate([head, rolled[8:, :]], axis=0)
        xc = xc + cw_ref[C_CONV - 1 - s:C_CONV - s, :] * shifted
    carry_ref[...] = xb[tm - 8:tm, :]
    tail_ref[...] = xb[tm - 8:tm, :]

    a, b = _lru_gates(xc, wa_ref, wx_ref, ba_ref, bx_ref, lam_ref)
    a_s[...] = a
    b_s[...] = b

    def body(t, h):
        h = a_s[pl.ds(t, 1), :] * h + b_s[pl.ds(t, 1), :]
        b_s[pl.ds(t, 1), :] = h
        return h

    h_last = lax.fori_loop(0, tm, body, h_ref[...], unroll=8)
    h_ref[...] = h_last
    hT_ref[...] = h_last
    y_ref[...] = _gelu_tanh(gate) * b_s[...]


def _lru_seq(u, n_seq, conv_prev8, h0, conv_w, conv_b, wa, wx, ba, bx, lam, tm):
    rows = u.shape[0]
    per_seq = rows // n_seq // tm
    const = lambda shape: pl.BlockSpec(shape, lambda b, t: (0,) * len(shape))
    return pl.pallas_call(
        _lru_seq_kernel,
        grid=(n_seq, per_seq),
        in_specs=[
            pl.BlockSpec((tm, 2 * C_WIDTH), lambda b, t: (b * per_seq + t, 0)),
            pl.BlockSpec((None, 8, C_WIDTH), lambda b, t: (b, 0, 0)),
            pl.BlockSpec((None, 1, C_WIDTH), lambda b, t: (b, 0, 0)),
            const((C_CONV, C_WIDTH)), const((1, C_WIDTH)),
            const((C_BLOCKS, C_BLOCK, C_BLOCK)), const((C_BLOCKS, C_BLOCK, C_BLOCK)),
            const((1, C_WIDTH)), const((1, C_WIDTH)), const((1, C_WIDTH)),
        ],
        out_specs=[
            pl.BlockSpec((tm, C_WIDTH), lambda b, t: (b * per_seq + t, 0)),
            pl.BlockSpec((None, 8, C_WIDTH), lambda b, t: (b, 0, 0)),
            pl.BlockSpec((None, 1, C_WIDTH), lambda b, t: (b, 0, 0)),
        ],
        out_shape=[jax.ShapeDtypeStruct((rows, C_WIDTH), F32),
                   jax.ShapeDtypeStruct((n_seq, 8, C_WIDTH), F32),
                   jax.ShapeDtypeStruct((n_seq, 1, C_WIDTH), F32)],
        scratch_shapes=[pltpu.VMEM((8, C_WIDTH), F32), pltpu.VMEM((1, C_WIDTH), F32),
                        pltpu.VMEM((tm, C_WIDTH), F32), pltpu.VMEM((tm, C_WIDTH), F32)],
        compiler_params=_params("arbitrary", "arbitrary"),
        name="lru_seq",
    )(u, conv_prev8, h0, conv_w, conv_b, wa, wx, ba, bx, lam)


def _lru_step_kernel(u_ref, cp_ref, h0_ref, cw_ref, cb_ref, wa_ref, wx_ref, ba_ref, bx_ref, lam_ref,
                     y_ref, cn_ref, h_ref):
    gate = u_ref[:, 0:C_WIDTH]
    xb = u_ref[:, C_WIDTH:2 * C_WIDTH]
    xc = cb_ref[...] + cw_ref[C_CONV - 1:C_CONV, :] * xb
    for i in range(C_CONV - 1):
        xc = xc + cw_ref[i:i + 1, :] * cp_ref[i]
    a, b = _lru_gates(xc, wa_ref, wx_ref, ba_ref, bx_ref, lam_ref)
    h = a * h0_ref[...] + b
    h_ref[...] = h
    y_ref[...] = _gelu_tanh(gate) * h
    for i in range(C_CONV - 2):
        cn_ref[i] = cp_ref[i + 1]
    cn_ref[C_CONV - 2] = xb


def _lru_step(u, conv_states, h0, j, conv_w, conv_b, wa, wx, ba, bx, lam):
    rows = u.shape[0]
    full = lambda shape: pl.BlockSpec(shape, lambda i: (0,) * len(shape))
    return pl.pallas_call(
        _lru_step_kernel,
        grid=(1,),
        in_specs=[full((rows, 2 * C_WIDTH)),
                  pl.BlockSpec((None, C_CONV - 1, rows, C_WIDTH), lambda i: (j, 0, 0, 0)),
                  full((rows, C_WIDTH)),
                  full((C_CONV, C_WIDTH)), full((1, C_WIDTH)),
                  full((C_BLOCKS, C_BLOCK, C_BLOCK)), full((C_BLOCKS, C_BLOCK, C_BLOCK)),
                  full((1, C_WIDTH)), full((1, C_WIDTH)), full((1, C_WIDTH))],
        out_specs=[full((rows, C_WIDTH)), full((C_CONV - 1, rows, C_WIDTH)), full((rows, C_WIDTH))],
        out_shape=[jax.ShapeDtypeStruct((rows, C_WIDTH), F32),
                   jax.ShapeDtypeStruct((C_CONV - 1, rows, C_WIDTH), F32),
                   jax.ShapeDtypeStruct((rows, C_WIDTH), F32)],
        compiler_params=_params("arbitrary"),
        name="lru_step",
    )(u, conv_states, h0, conv_w, conv_b, wa, wx, ba, bx, lam)


def _pad_rows(w, first, total):
    return jnp.zeros((total, w.shape[1]), w.dtype).at[first:first + w.shape[0]].set(w)


def _trunk(x, mod, n_seq, seq_len, st_shift, st_wkv, st_gla, st_conv, st_lru, wts, mxu, tm, tf):
    (w_ffn_up, w_ffn_down, w_in_even, w_out_even, a_mu, a_w_up, a_a_up, a_g_up, a_vec, b_alpha_up,
     b_alpha_b, b_norm_w, w_in_odd, w_out_odd, c_conv_w, c_conv_b, c_wa, c_ba, c_wx, c_bx, c_lam,
     final_norm_w) = wts
    rows = x.shape[0]
    prompt = seq_len > 1
    tn_even, tn_odd = 13 * LANES, 8 * LANES
    seg = jnp.kron(jnp.eye(A_HEADS, dtype=F32), jnp.ones((A_HEAD_DIM, A_HEAD_DIM), F32)).astype(BF16)
    out_shift, out_wkv, out_gla, out_conv, out_lru = [], [], [], [], []
    for l in range(DEPTH):
        j = l // 2
        if prompt:
            x = _ffn(x, mod, l, 0, mxu['ffn', l, 0], tm, tf)
        else:
            x, mxu['ffn', l, 0] = _ffn(x, mod, l, 0, (w_ffn_up, w_ffn_down), tm, tf)
        if l % 2 == 0:
            if prompt:
                u = _inproj(x, mod, l, mxu['in', l], None, tm, tn_even)
            else:
                u, mxu['in', l] = _inproj(x, mod, l, jnp.swapaxes(w_in_even, 1, 2), j, tm, tn_even,
                                          transposed=True)
            wup_pad = _pad_rows(a_w_up[j], 0, LANES)
            aup_pad = _pad_rows(a_a_up[j], A_DECAY_RANK, LANES)
            vec8 = _pad_rows(a_vec[j], 0, 8)
            alpha_pad = _pad_rows(b_alpha_up[j], 0, LANES)
            alpha_b = b_alpha_b[j].reshape(1, B_KEY_WIDTH)
            mu = a_mu[j].reshape(1, A_PROJ)
            norm_w = b_norm_w[j].reshape(1, B_WIDTH)
            if prompt:
                r, lw, k2, v, kk, al, g = _rwkv_prep(u, st_shift[j].reshape(n_seq, 1, A_PROJ), mu, wup_pad,
                                                     aup_pad, a_g_up[j], vec8, seg, n_seq, 256, True)
                y_a, h_t = _rwkv_chunk(r, lw, k2, v, kk, al, g, vec8, n_seq, _pair_states(st_wkv[j]))
                s_wkv = _unpair_states(h_t)
                s_shift = u.reshape(n_seq, seq_len, EVEN_PROJ)[:, -1, :A_PROJ]
                y_b, s_gla = _gla_chunk(u, n_seq, alpha_pad, alpha_b, norm_w, st_gla[j])
            else:
                r, lw, k2, v, kk, al, g, bonus = _rwkv_prep(u, st_shift[j], mu, wup_pad, aup_pad, a_g_up[j],
                                                            vec8, seg, rows, rows, False)
                y_raw, s_wkv_t = _rwkv_step(r, lw, k2, v, kk, al, jnp.transpose(st_wkv, (0, 2, 3, 4, 1)), j)
                s_wkv = jnp.transpose(s_wkv_t, (3, 0, 1, 2))
                y_a = _rwkv_post(y_raw, bonus, g, vec8, seg, rows)
                s_shift = u[:, :A_PROJ]
                y_b, s_gla = _gla_step(u, alpha_pad, alpha_b, norm_w, st_gla, j, 8)
            if prompt:
                x = _outproj([y_a, y_b], mxu['out', l], None, x, mod, l, min(tm, 512))
            else:
                x, mxu['out', l] = _outproj([y_a, y_b], w_out_even, j, x, mod, l, tm)
            out_shift.append(s_shift)
            out_wkv.append(s_wkv)
            out_gla.append(s_gla)
        else:
            if prompt:
                u = _inproj(x, mod, l, mxu['in', l], None, tm, tn_odd)
            else:
                u, mxu['in', l] = _inproj(x, mod, l, w_in_odd, j, tm, tn_odd)
            lru_w = (c_conv_w[j], c_conv_b[j].reshape(1, C_WIDTH), c_wa[j], c_wx[j],
                     c_ba[j].reshape(1, C_WIDTH), c_bx[j].reshape(1, C_WIDTH), c_lam[j].reshape(1, C_WIDTH))
            if prompt:
                cp8 = jnp.concatenate([jnp.zeros((n_seq, 8 - (C_CONV - 1), C_WIDTH), F32), st_conv[j]], axis=1)
                y_c, tail, h_t = _lru_seq(u, n_seq, cp8, st_lru[j].reshape(n_seq, 1, C_WIDTH), *lru_w, 256)
                s_conv = tail[:, 8 - (C_CONV - 1):, :]
                s_lru = h_t.reshape(n_seq, C_WIDTH)
            else:
                y_c, cn, s_lru = _lru_step(u, jnp.swapaxes(st_conv, 1, 2), st_lru[j], j, *lru_w)
                s_conv = jnp.swapaxes(cn, 0, 1)
            if prompt:
                x = _outproj([y_c], mxu['out', l], None, x, mod, l, min(tm, 512))
            else:
                x, mxu['out', l] = _outproj([y_c], w_out_odd, j, x, mod, l, tm)
            out_conv.append(s_conv)
            out_lru.append(s_lru)
        final_w = final_norm_w if l == DEPTH - 1 else None
        if prompt:
            x = _ffn(x, mod, l, 1, mxu['ffn', l, 1], tm, tf, final_w=final_w)
        else:
            x, mxu['ffn', l, 1] = _ffn(x, mod, l, 1, (w_ffn_up, w_ffn_down), tm, tf, final_w=final_w)
    return x, (jnp.stack(out_shift), jnp.stack(out_wkv), jnp.stack(out_gla), jnp.stack(out_conv),
               jnp.stack(out_lru))


def kernel(x_prompt, x_sample, c_prompt, c_sample, state_rwkv_shift, state_rwkv_wkv, state_gla, state_conv, state_lru, w_ada, b_ada, w_ffn_up, w_ffn_down, w_in_even, w_out_even, a_mu, a_w_up, a_a_up, a_g_up, a_vec, b_alpha_up, b_alpha_b, b_norm_w, w_in_odd, w_out_odd, c_conv_w, c_conv_b, c_wa, c_ba, c_wx, c_bx, c_lam, final_norm_w):
    wts = (w_ffn_up, w_ffn_down, w_in_even, w_out_even, a_mu, a_w_up, a_a_up, a_g_up, a_vec, b_alpha_up,
           b_alpha_b, b_norm_w, w_in_odd, w_out_odd, c_conv_w, c_conv_b, c_wa, c_ba, c_wx, c_bx, c_lam,
           final_norm_w)
    n_p, t_p, _ = x_prompt.shape
    n_s, t_s, _ = x_sample.shape
    assert t_s == 1

    c_rows = n_p + n_s
    pad = (-c_rows) % 16
    c_all = jnp.concatenate([c_sample, c_prompt, jnp.zeros((pad, D_MODEL), F32)], axis=0)
    mod_all = _ada(c_all, w_ada, b_ada)
    mod_p = _Mod(mod_all[:, :, n_s:c_rows].reshape(DEPTH, N_MOD, n_p, 1, D_MODEL), False, t_p)
    mod_s = _Mod(mod_all, True, 1)

    fresh = lambda s: jnp.zeros((s.shape[0], n_p) + s.shape[2:], s.dtype)
    mxu = {}
    y_s, st_s = _trunk(x_sample.reshape(n_s, D_MODEL), mod_s, n_s, 1,
                       state_rwkv_shift, state_rwkv_wkv, state_gla, state_conv, state_lru, wts, mxu, n_s, 512)
    y_p, st_p = _trunk(x_prompt.reshape(n_p * t_p, D_MODEL), mod_p, n_p, t_p,
                       fresh(state_rwkv_shift), fresh(state_rwkv_wkv), fresh(state_gla),
                       fresh(state_conv), fresh(state_lru), wts, mxu, 1024, 512)
    return (y_p.reshape(n_p, t_p, D_MODEL), y_s.reshape(n_s, 1, D_MODEL), *st_p, *st_s)
```

```python
import functools

import jax
import jax.numpy as jnp
from jax import lax
from jax.experimental import pallas as pl
from jax.experimental.pallas import tpu as pltpu

F32 = jnp.float32
BF16 = jnp.bfloat16

D_MODEL = 2048
DEPTH = 4
N_MOD = 9
D_FF = 5632
NORM_EPS = 1e-6

A_HEADS = 16
A_HEAD_DIM = 64
A_WIDTH = A_HEADS * A_HEAD_DIM
A_DECAY_RANK = 64
A_ICL_RANK = 64
A_GATE_RANK = 128
A_PROJ = 3 * A_WIDTH + A_DECAY_RANK + A_ICL_RANK + A_GATE_RANK
A_GN_EPS = 64e-5

B_HEADS = 4
B_KEY_DIM = 128
B_VAL_DIM = 256
B_KEY_WIDTH = B_HEADS * B_KEY_DIM
B_WIDTH = B_HEADS * B_VAL_DIM
B_ALPHA_RANK = 16
B_TAU = 16.0
B_PROJ = 2 * B_KEY_WIDTH + 2 * B_WIDTH + B_ALPHA_RANK
EVEN_PROJ = A_PROJ + B_PROJ

C_WIDTH = D_MODEL
C_BLOCKS = 8
C_BLOCK = C_WIDTH // C_BLOCKS
C_CONV = 4
C_POW = 8.0

CHUNK = 64
RWKV_STEP_CHUNKS = 2
GLA_STEP_CHUNKS = 2
GLA_SUB = 8
LANES = 128
VMEM_LIMIT = 56 * 1024 * 1024


def _params(*sem):
    return pltpu.CompilerParams(dimension_semantics=sem, vmem_limit_bytes=VMEM_LIMIT)


def _bdot(a, b):
    return jnp.dot(a.astype(BF16), b.astype(BF16), preferred_element_type=F32)


def _split3(x):
    hi = x.astype(BF16)
    r1 = x - hi.astype(F32)
    mid = r1.astype(BF16)
    lo = (r1 - mid.astype(F32)).astype(BF16)
    return hi, mid, lo


def _dot_exact_rhs(a_bf16, x):
    hi, mid, lo = _split3(x)
    f = lambda y: jnp.dot(a_bf16, y, preferred_element_type=F32)
    return f(hi) + f(mid) + f(lo)


def _dot_exact_lhs(x, b_bf16):
    hi, mid, lo = _split3(x)
    f = lambda y: jnp.dot(y, b_bf16, preferred_element_type=F32)
    return f(hi) + f(mid) + f(lo)


def _sigmoid(x):
    return jax.nn.sigmoid(x)


def _softplus(x):
    return jnp.maximum(x, 0.0) + jnp.log1p(jnp.exp(-jnp.abs(x)))


def _gelu_tanh(x):
    return 0.5 * x * (1.0 + jnp.tanh(0.7978845608028654 * (x + 0.044715 * (x * x * x))))


def _norm_mod(x, shift, scale):
    ms = jnp.mean(x * x, axis=-1, keepdims=True)
    return x * lax.rsqrt(ms + NORM_EPS) * (1.0 + scale) + shift


def _ada_kernel(c_ref, w_ref, b_ref, o_ref):
    c = c_ref[...]
    act = (c * _sigmoid(c)).astype(BF16)
    o_ref[...] = jnp.dot(act, w_ref[...].astype(BF16), preferred_element_type=F32) + b_ref[...]


def _ada(c_all, w_ada, b_ada):
    rows = c_all.shape[0]
    tn = 1024
    per = D_MODEL // tn
    b4 = b_ada.reshape(DEPTH, N_MOD * per, 1, tn)
    return pl.pallas_call(
        _ada_kernel,
        grid=(DEPTH, N_MOD * per),
        in_specs=[
            pl.BlockSpec((rows, D_MODEL), lambda l, j: (0, 0)),
            pl.BlockSpec((None, D_MODEL, tn), lambda l, j: (l, 0, j)),
            pl.BlockSpec((None, None, 1, tn), lambda l, j: (l, j, 0, 0)),
        ],
        out_specs=pl.BlockSpec((None, None, rows, tn), lambda l, j: (l, j // per, 0, j % per)),
        out_shape=jax.ShapeDtypeStruct((DEPTH, N_MOD, rows, D_MODEL), F32),
        compiler_params=_params("arbitrary", "arbitrary"),
        name="ada",
    )(c_all, w_ada, b4)


class _Mod:
    def __init__(self, arr, per_row, seq_len):
        self.arr = arr
        self.per_row = per_row
        self.seq_len = seq_len

    def spec(self, layer, m, tm):
        if self.per_row:
            return pl.BlockSpec((None, None, tm, D_MODEL), lambda i, *_: (layer, m, i, 0))
        per_seq = self.seq_len // tm
        return pl.BlockSpec((None, None, None, 1, D_MODEL), lambda i, *_: (layer, m, i // per_seq, 0, 0))


def _ffn_kernel(x_ref, sh_ref, sc_ref, g_ref, wg_ref, wu_ref, wd_ref, *rest, n_j, sub, final, convert):
    rest = list(rest)
    fw_ref = rest.pop(0) if final else None
    f32_tiles = [rest.pop(0) for _ in range(2)] if convert else []
    o_ref = rest.pop(0)
    bf16_tiles = [rest.pop(0) for _ in range(2)] if convert else []
    h_ref, = rest
    j = pl.program_id(1)
    tm = x_ref.shape[0]

    @pl.when(j == 0)
    def _():
        def body(s, carry):
            rows = pl.ds(pl.multiple_of(s * sub, sub), sub)
            sh = sh_ref[...] if sh_ref.shape[0] == 1 else sh_ref[rows, :]
            sc = sc_ref[...] if sc_ref.shape[0] == 1 else sc_ref[rows, :]
            h_ref[rows, :] = _norm_mod(x_ref[rows, :], sh, sc).astype(BF16)
            o_ref[rows, :] = jnp.zeros((sub, D_MODEL), F32)
            return carry
        lax.fori_loop(0, tm // sub, body, 0, unroll=min(2, tm // sub))

    for src, dst in zip(f32_tiles, bf16_tiles):
        dst[...] = src[...].astype(BF16)

    h = h_ref[...]
    gt = jnp.dot(h, wg_ref[...], preferred_element_type=F32)
    ut = jnp.dot(h, wu_ref[...], preferred_element_type=F32)
    act = (gt * _sigmoid(gt) * ut).astype(BF16)
    o_ref[...] += jnp.dot(act, wd_ref[...], preferred_element_type=F32)

    @pl.when(j == n_j - 1)
    def _():
        def body(s, carry):
            rows = pl.ds(pl.multiple_of(s * sub, sub), sub)
            g = g_ref[...] if g_ref.shape[0] == 1 else g_ref[rows, :]
            y = x_ref[rows, :] + 0.5 * g * o_ref[rows, :]
            if final:
                ms = jnp.mean(y * y, axis=-1, keepdims=True)
                y = y * lax.rsqrt(ms + NORM_EPS) * fw_ref[...]
            o_ref[rows, :] = y
            return carry
        lax.fori_loop(0, tm // sub, body, 0, unroll=min(2, tm // sub))


def _ffn(x, mod, layer, which, weights, tm, tf, final_w=None, convert=None):
    m_rows = x.shape[0]
    n_i = m_rows // tm
    n_j = D_FF // tf
    m0 = 6 * which
    sub = min(tm, 128)
    final = final_w is not None
    in_specs = [
        pl.BlockSpec((tm, D_MODEL), lambda i, j: (i, 0), pipeline_mode=pl.Buffered(1)),
        mod.spec(layer, m0, tm), mod.spec(layer, m0 + 1, tm), mod.spec(layer, m0 + 2, tm),
        pl.BlockSpec((D_MODEL, tf), lambda i, j: (0, j)),
        pl.BlockSpec((D_MODEL, tf), lambda i, j: (0, j + n_j)),
        pl.BlockSpec((tf, D_MODEL), lambda i, j: (j, 0)),
    ]
    args = [x, mod.arr, mod.arr, mod.arr, weights[0], weights[0], weights[1]]
    if final:
        in_specs.append(pl.BlockSpec((1, D_MODEL), lambda i, j: (0, 0)))
        args.append(final_w.reshape(1, D_MODEL))
    out_specs = [pl.BlockSpec((tm, D_MODEL), lambda i, j: (i, 0))]
    out_shape = [jax.ShapeDtypeStruct((m_rows, D_MODEL), F32)]
    if convert is not None:
        w_up32, w_down32, l2, s2 = convert
        up_tile = (D_MODEL // n_i, 2 * D_FF // n_j)
        down_tile = (D_FF // n_j, D_MODEL // n_i)
        assert up_tile[0] * n_i == D_MODEL and up_tile[1] * n_j == 2 * D_FF and up_tile[1] % LANES == 0
        assert down_tile[0] * n_j == D_FF and down_tile[1] * n_i == D_MODEL and down_tile[1] % LANES == 0
        in_specs += [pl.BlockSpec((None, None) + up_tile, lambda i, j: (l2, s2, i, j)),
                     pl.BlockSpec((None, None) + down_tile, lambda i, j: (l2, s2, j, i))]
        args += [w_up32, w_down32]
        out_specs += [pl.BlockSpec(up_tile, lambda i, j: (i, j)), pl.BlockSpec(down_tile, lambda i, j: (j, i))]
        out_shape += [jax.ShapeDtypeStruct((D_MODEL, 2 * D_FF), BF16), jax.ShapeDtypeStruct((D_FF, D_MODEL), BF16)]
    out = pl.pallas_call(
        functools.partial(_ffn_kernel, n_j=n_j, sub=sub, final=final, convert=convert is not None),
        grid=(n_i, n_j),
        in_specs=in_specs,
        out_specs=out_specs,
        out_shape=out_shape,
        scratch_shapes=[pltpu.VMEM((tm, D_MODEL), BF16)],
        compiler_params=_params("arbitrary", "arbitrary"),
        name="ffn",
    )(*args)
    return (out[0], (out[1], out[2])) if convert is not None else out[0]


def _inproj_kernel(x_ref, sh_ref, sc_ref, w_ref, o_ref, *rest, sub, emit, transposed):
    h_ref = rest[-1]
    j = pl.program_id(1)
    tm = x_ref.shape[0]

    @pl.when(j == 0)
    def _():
        def body(s, carry):
            rows = pl.ds(pl.multiple_of(s * sub, sub), sub)
            sh = sh_ref[...] if sh_ref.shape[0] == 1 else sh_ref[rows, :]
            sc = sc_ref[...] if sc_ref.shape[0] == 1 else sc_ref[rows, :]
            h_ref[rows, :] = _norm_mod(x_ref[rows, :], sh, sc).astype(BF16)
            return carry
        lax.fori_loop(0, tm // sub, body, 0, unroll=min(2, tm // sub))

    w = w_ref[...]
    if transposed:
        w = w.T
    if emit:
        w = w.astype(BF16)
        rest[0][...] = w
    o_ref[...] = jnp.dot(h_ref[...], w, preferred_element_type=F32)


def _inproj(x, mod, layer, w, widx, tm, tn, transposed=False):
    m_rows = x.shape[0]
    n_out = w.shape[-2] if transposed else w.shape[-1]
    sub = min(tm, 128)
    emit = widx is not None
    if transposed:
        w_spec = pl.BlockSpec((None, tn, D_MODEL), lambda i, j: (widx, j, 0))
    elif emit:
        w_spec = pl.BlockSpec((None, D_MODEL, tn), lambda i, j: (widx, 0, j))
    else:
        w_spec = pl.BlockSpec((D_MODEL, tn), lambda i, j: (0, j))
    out_specs = [pl.BlockSpec((tm, tn), lambda i, j: (i, j))]
    out_shape = [jax.ShapeDtypeStruct((m_rows, n_out), F32)]
    if emit:
        assert m_rows == tm
        out_specs.append(pl.BlockSpec((D_MODEL, tn), lambda i, j: (0, j)))
        out_shape.append(jax.ShapeDtypeStruct((D_MODEL, n_out), BF16))
    out = pl.pallas_call(
        functools.partial(_inproj_kernel, sub=sub, emit=emit, transposed=transposed),
        grid=(m_rows // tm, pl.cdiv(n_out, tn)),
        in_specs=[
            pl.BlockSpec((tm, D_MODEL), lambda i, j: (i, 0)),
            mod.spec(layer, 3, tm), mod.spec(layer, 4, tm),
            w_spec,
        ],
        out_specs=out_specs,
        out_shape=out_shape,
        scratch_shapes=[pltpu.VMEM((tm, D_MODEL), BF16)],
        compiler_params=_params("arbitrary", "arbitrary"),
        name="inproj",
    )(x, mod.arr, mod.arr, w)
    return tuple(out) if emit else out[0]


def _outproj_kernel(*refs, n_in, emit):
    y_refs = refs[:n_in]
    w_refs = refs[n_in:2 * n_in]
    x_ref, g_ref, o_ref = refs[2 * n_in:2 * n_in + 3]
    acc = None
    for k, (y_ref, w_ref) in enumerate(zip(y_refs, w_refs)):
        w = w_ref[...]
        if emit:
            w = w.astype(BF16)
            refs[2 * n_in + 3 + k][...] = w
        part = jnp.dot(y_ref[...].astype(BF16), w, preferred_element_type=F32)
        acc = part if acc is None else acc + part
    o_ref[...] = x_ref[...] + g_ref[...] * acc


def _outproj(ys, w, widx, x, mod, layer, tm):
    m_rows = x.shape[0]
    n_in = len(ys)
    kw = ys[0].shape[1]
    emit = widx is not None
    in_specs = [pl.BlockSpec((tm, kw), lambda i: (i, 0)) for _ in ys]
    if emit:
        assert m_rows == tm
        in_specs += [pl.BlockSpec((None, kw, D_MODEL), lambda i, k=k: (widx, k, 0)) for k in range(n_in)]
        w_args = [w] * n_in
    else:
        in_specs += [pl.BlockSpec((kw, D_MODEL), lambda i: (0, 0))] * n_in
        w_args = list(w)
    in_specs += [pl.BlockSpec((tm, D_MODEL), lambda i: (i, 0)), mod.spec(layer, 5, tm)]
    out_specs = [pl.BlockSpec((tm, D_MODEL), lambda i: (i, 0))]
    out_shape = [jax.ShapeDtypeStruct((m_rows, D_MODEL), F32)]
    if emit:
        out_specs += [pl.BlockSpec((kw, D_MODEL), lambda i: (0, 0))] * n_in
        out_shape += [jax.ShapeDtypeStruct((kw, D_MODEL), BF16)] * n_in
    out = pl.pallas_call(
        functools.partial(_outproj_kernel, n_in=n_in, emit=emit),
        grid=(m_rows // tm,),
        in_specs=in_specs,
        out_specs=out_specs,
        out_shape=out_shape,
        compiler_params=_params("arbitrary"),
        name="outproj",
    )(*ys, *w_args, x, mod.arr)
    return (out[0], tuple(out[1:])) if emit else out[0]


def _rwkv_prep_kernel(u_ref, prev_ref, mu_ref, wup_ref, aup_ref, gup_ref, vec_ref, *rest, seq_mode):
    if seq_mode:
        r_ref, lw_ref, k_ref, v_ref, kk_ref, a_ref, g_ref, carry_ref = rest
    else:
        seg_ref, r_ref, lw_ref, k_ref, v_ref, kk_ref, a_ref, g_ref, bonus_ref = rest
    u = u_ref[...]
    tm = u.shape[0]
    if seq_mode:
        @pl.when(pl.program_id(1) == 0)
        def _():
            carry_ref[...] = prev_ref[...]
        rolled = pltpu.roll(u, 1, axis=0)
        row = lax.broadcasted_iota(jnp.int32, u.shape, 0)
        prev = jnp.where(row == 0, carry_ref[...], rolled)
        carry_ref[...] = u[tm - 1:tm, :]
    else:
        prev = prev_ref[...]
    xs = u + mu_ref[...] * (prev - u)
    r = xs[:, 0:A_WIDTH]
    k = xs[:, A_WIDTH:2 * A_WIDTH]
    v = xs[:, 2 * A_WIDTH:3 * A_WIDTH]
    wa_lo = xs[:, 3 * A_WIDTH:3 * A_WIDTH + LANES]
    g_lo = xs[:, 3 * A_WIDTH + LANES:]
    w0, a0, k_k, k_a, r_k = (vec_ref[i:i + 1, :] for i in range(5))
    w_raw = -_softplus(-(w0 + _bdot(jnp.tanh(wa_lo), wup_ref[...]))) - 0.5
    lw = -jnp.exp(w_raw)
    a = _sigmoid(a0 + _bdot(wa_lo, aup_ref[...]))
    g = _bdot(_sigmoid(g_lo), gup_ref[...])
    kk = k * k_k
    k2 = k * (1.0 + (a - 1.0) * k_a)
    if not seq_mode:
        seg = seg_ref[...]
        kk = kk * lax.rsqrt(jnp.maximum(_dot_exact_lhs(kk * kk, seg), 1e-24))
        bonus_ref[...] = _dot_exact_lhs(r * k2 * r_k, seg) * v
    r_ref[...] = r
    lw_ref[...] = lw
    k_ref[...] = k2
    v_ref[...] = v
    kk_ref[...] = kk
    a_ref[...] = a
    g_ref[...] = g


def _rwkv_prep(u, prev, mu, wup_pad, aup_pad, g_up, vec, seg, n_seq, tm, seq_mode):
    rows = u.shape[0]
    per_seq = rows // n_seq // tm if seq_mode else 1
    grid = (n_seq, per_seq) if seq_mode else (rows // tm, 1)
    rowmap = (lambda b, t: (b * per_seq + t, 0)) if seq_mode else (lambda b, t: (b, 0))
    prev_spec = (pl.BlockSpec((None, 1, A_PROJ), lambda b, t: (b, 0, 0)) if seq_mode
                 else pl.BlockSpec((tm, A_PROJ), rowmap))
    const = lambda shape: pl.BlockSpec(shape, lambda b, t: (0,) * len(shape))
    out = jax.ShapeDtypeStruct((rows, A_WIDTH), F32)
    in_specs = [pl.BlockSpec((tm, A_PROJ), rowmap), prev_spec,
                const((1, A_PROJ)), const((LANES, A_WIDTH)), const((LANES, A_WIDTH)),
                const((A_GATE_RANK, A_WIDTH)), const((8, A_WIDTH))]
    args = [u, prev, mu, wup_pad, aup_pad, g_up, vec]
    if not seq_mode:
        in_specs.append(const((A_WIDTH, A_WIDTH)))
        args.append(seg)
    n_out = 7 if seq_mode else 8
    return pl.pallas_call(
        functools.partial(_rwkv_prep_kernel, seq_mode=seq_mode),
        grid=grid,
        in_specs=in_specs,
        out_specs=[pl.BlockSpec((tm, A_WIDTH), rowmap)] * n_out,
        out_shape=[out] * n_out,
        scratch_shapes=[pltpu.VMEM((1, A_PROJ), F32)] if seq_mode else [],
        compiler_params=_params("arbitrary", "arbitrary"),
        name="rwkv_prep",
    )(*args)


def _rwkv_post_kernel(y_ref, bonus_ref, g_ref, vec_ref, seg_ref, o_ref):
    y = y_ref[...]
    seg = seg_ref[...]
    inv_n = 1.0 / A_HEAD_DIM
    yc = y - _dot_exact_lhs(y, seg) * inv_n
    var = _dot_exact_lhs(yc * yc, seg) * inv_n
    yn = yc * lax.rsqrt(var + A_GN_EPS) * vec_ref[5:6, :] + vec_ref[6:7, :]
    o_ref[...] = (yn + bonus_ref[...]) * g_ref[...]


def _rwkv_post(y, bonus, g, vec, seg, tm):
    rows = y.shape[0]
    spec = pl.BlockSpec((tm, A_WIDTH), lambda i: (i, 0))
    return pl.pallas_call(
        _rwkv_post_kernel,
        grid=(rows // tm,),
        in_specs=[spec, spec, spec,
                  pl.BlockSpec((8, A_WIDTH), lambda i: (0, 0)),
                  pl.BlockSpec((A_WIDTH, A_WIDTH), lambda i: (0, 0))],
        out_specs=spec,
        out_shape=jax.ShapeDtypeStruct((rows, A_WIDTH), F32),
        compiler_params=_params("arbitrary"),
        name="rwkv_post",
    )(y, bonus, g, vec, seg)


def _rwkv_chunk_kernel(r_ref, lw_ref, k_ref, v_ref, kk_ref, a_ref, g_ref, vec_ref, h0_ref, y_ref, hT_ref, h_ref):
    ci = pl.program_id(1)

    @pl.when(ci == 0)
    def _():
        h_ref[...] = h0_ref[...]

    cs = CHUNK
    n2 = 2 * cs
    npair = A_WIDTH // LANES
    pairs = lambda x: jnp.stack([x[:, p * LANES:(p + 1) * LANES] for p in range(npair)])
    vec = vec_ref[...]
    vrow = lambda i: pairs(vec[i:i + 1, :])
    lane = lax.broadcasted_iota(jnp.int32, (1, 1, LANES), 2)
    m0 = (lane < A_HEAD_DIM).astype(F32)
    m1 = 1.0 - m0
    split = lambda x: jnp.concatenate([x * m0, x * m1], axis=1)
    dup = lambda x: jnp.concatenate([x, x], axis=1)
    own = jnp.concatenate([jnp.broadcast_to(m0, (1, cs, LANES)), jnp.broadcast_to(m1, (1, cs, LANES))], axis=1)
    bf = lambda x: x.astype(BF16)

    def bmm(x, y):
        return jnp.einsum('hab,hbc->hac', x, y, preferred_element_type=F32)

    def bmm_nt(x, y):
        return jnp.einsum('han,hbn->hab', x, y, preferred_element_type=F32)

    def bmm_tn(x, y):
        return jnp.einsum('hca,hcb->hab', x, y, preferred_element_type=F32)

    row = lax.broadcasted_iota(jnp.int32, (cs, cs), 0)
    col = lax.broadcasted_iota(jnp.int32, (cs, cs), 1)
    tri = (col <= row).astype(BF16)
    row2 = lax.broadcasted_iota(jnp.int32, (n2, n2), 0)
    col2 = lax.broadcasted_iota(jnp.int32, (n2, n2), 1)
    same = (row2 >= cs) == (col2 >= cs)
    tok_r = jnp.where(row2 >= cs, row2 - cs, row2)
    tok_c = jnp.where(col2 >= cs, col2 - cs, col2)
    strict = jnp.where(same, (tok_c < tok_r).astype(F32), 0.0)
    incl = jnp.where(same, (tok_c <= tok_r).astype(F32), 0.0)
    eye2 = (row2 == col2).astype(F32)
    nrow = lax.broadcasted_iota(jnp.int32, (LANES, LANES), 0)
    ncol = lax.broadcasted_iota(jnp.int32, (LANES, LANES), 1)
    eye_n = (nrow == ncol).astype(F32)

    h = h_ref[...]
    for sub in range(r_ref.shape[0] // cs):
        rows = slice(sub * cs, (sub + 1) * cs)
        lw_all = lw_ref[rows, :]
        c = pairs(_dot_exact_rhs(tri, lw_all))
        lw = pairs(lw_all)
        c_end = c[:, cs - 1:cs, :]
        e_pos = dup(jnp.exp(c))
        e_neg = dup(jnp.exp(-c))
        e_prev = dup(jnp.exp(c - lw))
        e_end = dup(jnp.exp(c_end - c))
        g_end = jnp.exp(c_end)

        kk = split(pairs(kk_ref[rows, :]))
        kk = kk * lax.rsqrt(jnp.maximum(jnp.sum(kk * kk, axis=-1, keepdims=True), 1e-24))
        b_vec = kk * dup(pairs(a_ref[rows, :]))
        r2 = split(pairs(r_ref[rows, :]))
        k2 = split(pairs(k_ref[rows, :]))
        v2 = split(pairs(v_ref[rows, :]))
        bonus = jnp.sum(r2 * k2 * vrow(4), axis=-1, keepdims=True) * v2
        rt = r2 * e_pos
        at = bf(-kk * e_prev)
        bt = bf(b_vec * e_neg)
        kt = bf(k2 * e_neg)
        bh = bf(b_vec * e_end)
        kh = bf(k2 * e_end)
        v2b = bf(v2)

        gram = bmm_nt(jnp.concatenate([at, bf(rt)], axis=1), jnp.concatenate([bt, kt], axis=1))
        a_ab = gram[:, :n2, :n2] * strict
        a_ak = gram[:, :n2, n2:] * strict
        m_b = gram[:, n2:, :n2] * incl
        m_k = gram[:, n2:, n2:] * incl

        tinv = eye2 + a_ab
        a_b = bf(a_ab)
        apow = bmm(a_b, a_b)
        span = 2
        while 2 * span < cs:
            ap_b = bf(apow)
            both = bmm(jnp.concatenate([bf(tinv), ap_b], axis=1), ap_b)
            tinv = tinv + both[:, :n2]
            apow = both[:, n2:]
            span *= 2
        tinv = tinv + bmm(bf(tinv), bf(apow))

        akv = bmm(bf(a_ak), v2b)
        pq = bf(bmm(bf(tinv), jnp.concatenate([at, bf(akv)], axis=-1)))
        mpq = bmm(bf(m_b), pq)
        p2 = rt + mpq[..., :LANES]
        y0 = mpq[..., LANES:] + bmm(bf(m_k), v2b)
        tpq = bmm_tn(bh, pq)
        gm = eye_n * g_end + tpq[..., :LANES]
        hadd = tpq[..., LANES:] + bmm_tn(kh, v2b)

        yh = bmm(jnp.concatenate([bf(p2), bf(gm)], axis=1), bf(h))
        y2 = yh[:, :n2] + y0
        h = yh[:, n2:] + hadd

        inv_n = 1.0 / A_HEAD_DIM
        yc = (y2 - jnp.sum(y2, axis=-1, keepdims=True) * inv_n) * own
        var = jnp.sum(yc * yc, axis=-1, keepdims=True) * inv_n
        z = yc * lax.rsqrt(var + A_GN_EPS) * vrow(5) + vrow(6) * own + bonus
        out = (z[:, :cs] + z[:, cs:]) * pairs(g_ref[rows, :])
        for p in range(npair):
            y_ref[rows, p * LANES:(p + 1) * LANES] = out[p]

    h_ref[...] = h

    @pl.when(ci == pl.num_programs(1) - 1)
    def _():
        hT_ref[...] = h


def _rwkv_chunk(r, lw, k, v, kk, a, g, vec, n_seq, h0):
    rows = r.shape[0]
    step_rows = RWKV_STEP_CHUNKS * CHUNK
    nc = rows // n_seq // step_rows
    npair = A_WIDTH // LANES
    spec = pl.BlockSpec((step_rows, A_WIDTH), lambda b, c: (b * nc + c, 0))
    hspec = pl.BlockSpec((None, npair, LANES, LANES), lambda b, c: (b, 0, 0, 0))
    return pl.pallas_call(
        _rwkv_chunk_kernel,
        grid=(n_seq, nc),
        in_specs=[spec] * 7 + [pl.BlockSpec((8, A_WIDTH), lambda b, c: (0, 0)), hspec],
        out_specs=[spec, hspec],
        out_shape=[jax.ShapeDtypeStruct((rows, A_WIDTH), F32),
                   jax.ShapeDtypeStruct((n_seq, npair, LANES, LANES), F32)],
        scratch_shapes=[pltpu.VMEM((npair, LANES, LANES), F32)],
        compiler_params=_params("arbitrary", "arbitrary"),
        name="rwkv_chunk",
    )(r, lw, k, v, kk, a, g, vec, h0)


def _pair_states(s):
    n = s.shape[0]
    nd = A_HEAD_DIM
    h = jnp.swapaxes(s, -1, -2).reshape(n, A_HEADS // 2, 2, nd, nd)
    out = jnp.zeros((n, A_HEADS // 2, 2 * nd, 2 * nd), s.dtype)
    return out.at[:, :, :nd, :nd].set(h[:, :, 0]).at[:, :, nd:, nd:].set(h[:, :, 1])


def _unpair_states(hp):
    n = hp.shape[0]
    nd = A_HEAD_DIM
    h = jnp.stack([hp[:, :, :nd, :nd], hp[:, :, nd:, nd:]], axis=2).reshape(n, A_HEADS, nd, nd)
    return jnp.swapaxes(h, -1, -2)


def _rwkv_step_kernel(r_ref, lw_ref, k_ref, v_ref, kk_ref, a_ref, s_ref, y_ref, so_ref,
                      r_t, w_t, k_t, v_t, nkk_t, b_t, y_t):
    h = pl.program_id(0)
    nd = A_HEAD_DIM

    @pl.when(h == 0)
    def _():
        kk = kk_ref[...]
        r_t[...] = r_ref[...].T
        w_t[...] = jnp.exp(lw_ref[...]).T
        k_t[...] = k_ref[...].T
        v_t[...] = v_ref[...].T
        nkk_t[...] = (-kk).T
        b_t[...] = (kk * a_ref[...]).T

    first = pl.multiple_of(h * nd, nd)
    rows = pl.ds(first, nd)
    r, w, k, nkk, b = r_t[rows, :], w_t[rows, :], k_t[rows, :], nkk_t[rows, :], b_t[rows, :]

    def body(i, carry):
        s = s_ref[i]
        s_kk = jnp.sum(s * nkk, axis=0, keepdims=True)
        s_new = s * w + s_kk * b + v_t[pl.ds(first + i, 1), :] * k
        so_ref[i] = s_new
        y_t[pl.ds(first + i, 1), :] = jnp.sum(s_new * r, axis=0, keepdims=True)
        return carry

    lax.fori_loop(0, nd, body, 0, unroll=4)

    @pl.when(h == pl.num_programs(0) - 1)
    def _():
        y_ref[...] = y_t[...].T


def _rwkv_step(r, lw, k, v, kk, a, states_t, j):
    n = r.shape[0]
    nd = A_HEAD_DIM
    rspec = pl.BlockSpec((n, A_WIDTH), lambda h: (0, 0))
    return pl.pallas_call(
        _rwkv_step_kernel,
        grid=(A_HEADS,),
        in_specs=[rspec] * 6 + [pl.BlockSpec((None, None, nd, nd, n), lambda h: (j, h, 0, 0, 0))],
        out_specs=[rspec, pl.BlockSpec((None, nd, nd, n), lambda h: (h, 0, 0, 0))],
        out_shape=[jax.ShapeDtypeStruct((n, A_WIDTH), F32), jax.ShapeDtypeStruct(states_t.shape[1:], F32)],
        scratch_shapes=[pltpu.VMEM((A_WIDTH, n), F32)] * 7,
        compiler_params=_params("arbitrary"),
        name="rwkv_step",
    )(r, lw, k, v, kk, a, states_t)


def _gla_log_alpha(a128, aup_ref, ab_ref):
    lane = lax.broadcasted_iota(jnp.int32, a128.shape, 1)
    a_lo = jnp.where(lane < B_ALPHA_RANK, a128, 0.0)
    x = _bdot(a_lo, aup_ref[...]) + ab_ref[...]
    return -_softplus(-x) * (1.0 / B_TAU)


def _gla_block(u_ref, aup_ref, ab_ref, nw_ref, o_ref, s_ref):
    ub = u_ref[...]
    cs = ub.shape[0]
    q_all = ub[:, 0:B_KEY_WIDTH] * (B_KEY_DIM ** -0.5)
    k_all = ub[:, B_KEY_WIDTH:2 * B_KEY_WIDTH]
    v_all = ub[:, 2 * B_KEY_WIDTH:2 * B_KEY_WIDTH + B_WIDTH]
    g_all = ub[:, 2 * B_KEY_WIDTH + B_WIDTH:2 * B_KEY_WIDTH + 2 * B_WIDTH]
    a128 = ub[:, 2 * B_KEY_WIDTH + 2 * B_WIDTH:2 * B_KEY_WIDTH + 2 * B_WIDTH + LANES]
    log_a = _gla_log_alpha(a128, aup_ref, ab_ref)

    row = lax.broadcasted_iota(jnp.int32, (cs, cs), 0)
    col = lax.broadcasted_iota(jnp.int32, (cs, cs), 1)
    cum = _dot_exact_rhs((col <= row).astype(BF16), log_a)
    e_pos = jnp.exp(cum)
    cum_end = cum[cs - 1:cs, :]
    e_end = jnp.exp(cum_end - cum)
    g_end = jnp.exp(cum_end)
    nk = B_KEY_DIM
    eye_k = (lax.broadcasted_iota(jnp.int32, (nk, nk), 0) == lax.broadcasted_iota(jnp.int32, (nk, nk), 1)).astype(F32)

    nblk = cs // GLA_SUB
    kw = B_KEY_WIDTH
    cum3 = cum.reshape(nblk, GLA_SUB, kw)
    before = jnp.concatenate([jnp.zeros((1, 1, kw), F32), cum3[:nblk - 1, GLA_SUB - 1:GLA_SUB, :]], axis=0)
    q_rel = q_all * jnp.exp(cum - jnp.broadcast_to(before, (nblk, GLA_SUB, kw)).reshape(cs, kw))
    tok = lax.broadcasted_iota(jnp.int32, (cs, 1), 0)
    q_parts, k_parts = [], []
    for blk in range(1, nblk):
        first = blk * GLA_SUB
        k_parts.append(jnp.where(tok < first, k_all * jnp.exp(jnp.minimum(before[blk] - cum, 0.0)), 0.0))
        q_parts.append(jnp.where((tok >= first) & (tok < first + GLA_SUB), q_rel, 0.0))

    q3 = q_all.reshape(nblk, GLA_SUB, kw)
    k3 = k_all.reshape(nblk, GLA_SUB, kw)
    v3 = v_all.reshape(nblk, GLA_SUB, B_WIDTH)
    sub_i = lax.broadcasted_iota(jnp.int32, (1, GLA_SUB, 1), 1)
    o_diag = [jnp.zeros((nblk, GLA_SUB, B_VAL_DIM), F32) for _ in range(B_HEADS)]
    for jj in range(GLA_SUB):
        pair = q3 * k3[:, jj:jj + 1, :] * jnp.exp(jnp.minimum(cum3 - cum3[:, jj:jj + 1, :], 0.0))
        pair = jnp.where(sub_i >= jj, pair, 0.0)
        for h in range(B_HEADS):
            score = jnp.sum(pair[:, :, h * nk:(h + 1) * nk], axis=-1, keepdims=True)
            o_diag[h] = o_diag[h] + score * v3[:, jj:jj + 1, h * B_VAL_DIM:(h + 1) * B_VAL_DIM]

    for h in range(B_HEADS):
        ks = slice(h * B_KEY_DIM, (h + 1) * B_KEY_DIM)
        vs = slice(h * B_VAL_DIM, (h + 1) * B_VAL_DIM)
        qe = q_all[:, ks] * e_pos[:, ks]
        ke = k_all[:, ks] * e_end[:, ks]
        vh = v_all[:, vs]
        s = s_ref[h]
        q_cat = jnp.concatenate([part[:, ks] for part in q_parts], axis=-1).astype(BF16)
        k_cat = jnp.concatenate([part[:, ks] for part in k_parts], axis=-1).astype(BF16)
        scores = jnp.einsum('id,jd->ij', q_cat, k_cat, preferred_element_type=F32)
        o = _bdot(scores, vh) + _bdot(qe, s) + o_diag[h].reshape(cs, B_VAL_DIM)
        g_col = jnp.sum(eye_k * g_end[:, ks], axis=-1, keepdims=True)
        s_ref[h] = s * g_col + jnp.einsum('jd,jv->dv', ke.astype(BF16), vh.astype(BF16),
                                           preferred_element_type=F32)
        ms = jnp.mean(o * o, axis=-1, keepdims=True)
        gh = g_all[:, vs]
        o_ref[:, vs] = o * lax.rsqrt(ms + NORM_EPS) * nw_ref[:, vs] * (gh * _sigmoid(gh))


def _gla_chunk_kernel(u_ref, aup_ref, ab_ref, nw_ref, s0_ref, o_ref, sT_ref, s_ref):
    ci = pl.program_id(1)

    @pl.when(ci == 0)
    def _():
        s_ref[...] = s0_ref[...]

    for sub in range(u_ref.shape[0] // CHUNK):
        rows = pl.ds(sub * CHUNK, CHUNK)
        _gla_block(u_ref.at[rows, :], aup_ref, ab_ref, nw_ref, o_ref.at[rows, :], s_ref)

    @pl.when(ci == pl.num_programs(1) - 1)
    def _():
        sT_ref[...] = s_ref[...]


def _gla_chunk(u, n_seq, alpha_up_pad, alpha_b, norm_w, s0):
    rows = u.shape[0]
    step_rows = GLA_STEP_CHUNKS * CHUNK
    nc = rows // n_seq // step_rows
    return pl.pallas_call(
        _gla_chunk_kernel,
        grid=(n_seq, nc),
        in_specs=[
            pl.BlockSpec((step_rows, A_PROJ), lambda b, c: (b * nc + c, 1)),
            pl.BlockSpec((LANES, B_KEY_WIDTH), lambda b, c: (0, 0)),
            pl.BlockSpec((1, B_KEY_WIDTH), lambda b, c: (0, 0)),
            pl.BlockSpec((1, B_WIDTH), lambda b, c: (0, 0)),
            pl.BlockSpec((None, B_HEADS, B_KEY_DIM, B_VAL_DIM), lambda b, c: (b, 0, 0, 0)),
        ],
        out_specs=[
            pl.BlockSpec((step_rows, B_WIDTH), lambda b, c: (b * nc + c, 0)),
            pl.BlockSpec((None, B_HEADS, B_KEY_DIM, B_VAL_DIM), lambda b, c: (b, 0, 0, 0)),
        ],
        out_shape=[jax.ShapeDtypeStruct((rows, B_WIDTH), F32),
                   jax.ShapeDtypeStruct((n_seq, B_HEADS, B_KEY_DIM, B_VAL_DIM), F32)],
        scratch_shapes=[pltpu.VMEM((B_HEADS, B_KEY_DIM, B_VAL_DIM), F32)],
        compiler_params=_params("arbitrary", "arbitrary"),
        name="gla_chunk",
    )(u, alpha_up_pad, alpha_b, norm_w, s0)


def _gla_step_kernel(u_ref, aup_ref, ab_ref, nw_ref, s_ref, o_ref, so_ref, la_ref):
    nb = u_ref.shape[0]
    nk = B_KEY_DIM
    off_k, off_v, off_g = B_KEY_WIDTH, 2 * B_KEY_WIDTH, 2 * B_KEY_WIDTH + B_WIDTH
    off_a = 2 * B_KEY_WIDTH + 2 * B_WIDTH
    la_ref[...] = _gla_log_alpha(u_ref[:, off_a:off_a + LANES], aup_ref, ab_ref)
    eye = (lax.broadcasted_iota(jnp.int32, (nk, nk), 0) == lax.broadcasted_iota(jnp.int32, (nk, nk), 1)).astype(F32)
    col = lambda x: jnp.sum(eye * x, axis=-1, keepdims=True)

    def body(n, carry):
        row = pl.ds(n, 1)
        ur = u_ref[row, :]
        dec_all = jnp.exp(la_ref[row, :])
        outs = []
        for h in range(B_HEADS):
            q = ur[:, h * nk:(h + 1) * nk] * (B_KEY_DIM ** -0.5)
            k = ur[:, off_k + h * nk:off_k + (h + 1) * nk]
            v = ur[:, off_v + h * B_VAL_DIM:off_v + (h + 1) * B_VAL_DIM]
            g = ur[:, off_g + h * B_VAL_DIM:off_g + (h + 1) * B_VAL_DIM]
            dec = dec_all[:, h * nk:(h + 1) * nk]
            s = s_ref[n, h]
            qk = jnp.sum(q * k, axis=-1, keepdims=True)
            o = qk * v + jnp.sum(col(q * dec) * s, axis=0, keepdims=True)
            so_ref[n, h] = s * col(dec) + col(k) * v
            ms = jnp.mean(o * o, axis=-1, keepdims=True)
            outs.append(o * lax.rsqrt(ms + NORM_EPS) * nw_ref[:, h * B_VAL_DIM:(h + 1) * B_VAL_DIM]
                        * (g * _sigmoid(g)))
        o_ref[row, :] = jnp.concatenate(outs, axis=-1)
        return carry

    lax.fori_loop(0, nb, body, 0)


def _gla_step(u, alpha_up_pad, alpha_b, norm_w, states, j, nb):
    n = u.shape[0]
    return pl.pallas_call(
        _gla_step_kernel,
        grid=(n // nb,),
        in_specs=[pl.BlockSpec((nb, A_PROJ), lambda i: (i, 1)),
                  pl.BlockSpec((LANES, B_KEY_WIDTH), lambda i: (0, 0)),
                  pl.BlockSpec((1, B_KEY_WIDTH), lambda i: (0, 0)),
                  pl.BlockSpec((1, B_WIDTH), lambda i: (0, 0)),
                  pl.BlockSpec((None, nb, B_HEADS, B_KEY_DIM, B_VAL_DIM), lambda i: (j, i, 0, 0, 0))],
        out_specs=[pl.BlockSpec((nb, B_WIDTH), lambda i: (i, 0)),
                   pl.BlockSpec((nb, B_HEADS, B_KEY_DIM, B_VAL_DIM), lambda i: (i, 0, 0, 0))],
        out_shape=[jax.ShapeDtypeStruct((n, B_WIDTH), F32), jax.ShapeDtypeStruct(states.shape[1:], F32)],
        scratch_shapes=[pltpu.VMEM((nb, B_KEY_WIDTH), F32)],
        compiler_params=_params("arbitrary"),
        name="gla_step",
    )(u, alpha_up_pad, alpha_b, norm_w, states)


def _lru_gates(xc, wa_ref, wx_ref, ba_ref, bx_ref, lam_ref):
    ra, ix = [], []
    for nb in range(C_BLOCKS):
        xb = xc[:, nb * C_BLOCK:(nb + 1) * C_BLOCK].astype(BF16)
        ra.append(jnp.dot(xb, wa_ref[nb].astype(BF16), preferred_element_type=F32))
        ix.append(jnp.dot(xb, wx_ref[nb].astype(BF16), preferred_element_type=F32))
    r = _sigmoid(jnp.concatenate(ra, axis=-1) + ba_ref[...])
    i_g = _sigmoid(jnp.concatenate(ix, axis=-1) + bx_ref[...])
    log_a = -C_POW * r * _softplus(-lam_ref[...])
    a = jnp.exp(log_a)
    b = jnp.sqrt(1.0 - a * a) * (i_g * xc)
    return a, b


def _lru_seq_kernel(u_ref, cp_ref, h0_ref, cw_ref, cb_ref, wa_ref, wx_ref, ba_ref, bx_ref, lam_ref,
                    y_ref, tail_ref, hT_ref, carry_ref, h_ref, a_s, b_s):
    ti = pl.program_id(1)

    @pl.when(ti == 0)
    def _():
        carry_ref[...] = cp_ref[...]
        h_ref[...] = h0_ref[...]

    gate = u_ref[:, 0:C_WIDTH]
    xb = u_ref[:, C_WIDTH:2 * C_WIDTH]
    tm = xb.shape[0]
    carry = carry_ref[...]
    row8 = lax.broadcasted_iota(jnp.int32, (8, C_WIDTH), 0)
    xc = cb_ref[...] + cw_ref[C_CONV - 1:C_CONV, :] * xb
    for s in range(1, C_CONV):
        rolled = pltpu.roll(xb, s, axis=0)
        head = jnp.where(row8 < s, pltpu.roll(carry, s, axis=0), rolled[0:8, :])
        shifted = jnp.concatenate([head, rolled[8:, :]], axis=0)
        xc = xc + cw_ref[C_CONV - 1 - s:C_CONV - s, :] * shifted
    carry_ref[...] = xb[tm - 8:tm, :]
    tail_ref[...] = xb[tm - 8:tm, :]

    a, b = _lru_gates(xc, wa_ref, wx_ref, ba_ref, bx_ref, lam_ref)
    a_s[...] = a
    b_s[...] = b

    def body(t, h):
        h = a_s[pl.ds(t, 1), :] * h + b_s[pl.ds(t, 1), :]
        b_s[pl.ds(t, 1), :] = h
        return h

    h_last = lax.fori_loop(0, tm, body, h_ref[...], unroll=8)
    h_ref[...] = h_last
    hT_ref[...] = h_last
    y_ref[...] = _gelu_tanh(gate) * b_s[...]


def _lru_seq(u, n_seq, conv_prev8, h0, conv_w, conv_b, wa, wx, ba, bx, lam, tm):
    rows = u.shape[0]
    per_seq = rows // n_seq // tm
    const = lambda shape: pl.BlockSpec(shape, lambda b, t: (0,) * len(shape))
    return pl.pallas_call(
        _lru_seq_kernel,
        grid=(n_seq, per_seq),
        in_specs=[
            pl.BlockSpec((tm, 2 * C_WIDTH), lambda b, t: (b * per_seq + t, 0)),
            pl.BlockSpec((None, 8, C_WIDTH), lambda b, t: (b, 0, 0)),
            pl.BlockSpec((None, 1, C_WIDTH), lambda b, t: (b, 0, 0)),
            const((C_CONV, C_WIDTH)), const((1, C_WIDTH)),
            const((C_BLOCKS, C_BLOCK, C_BLOCK)), const((C_BLOCKS, C_BLOCK, C_BLOCK)),
            const((1, C_WIDTH)), const((1, C_WIDTH)), const((1, C_WIDTH)),
        ],
        out_specs=[
            pl.BlockSpec((tm, C_WIDTH), lambda b, t: (b * per_seq + t, 0)),
            pl.BlockSpec((None, 8, C_WIDTH), lambda b, t: (b, 0, 0)),
            pl.BlockSpec((None, 1, C_WIDTH), lambda b, t: (b, 0, 0)),
        ],
        out_shape=[jax.ShapeDtypeStruct((rows, C_WIDTH), F32),
                   jax.ShapeDtypeStruct((n_seq, 8, C_WIDTH), F32),
                   jax.ShapeDtypeStruct((n_seq, 1, C_WIDTH), F32)],
        scratch_shapes=[pltpu.VMEM((8, C_WIDTH), F32), pltpu.VMEM((1, C_WIDTH), F32),
                        pltpu.VMEM((tm, C_WIDTH), F32), pltpu.VMEM((tm, C_WIDTH), F32)],
        compiler_params=_params("arbitrary", "arbitrary"),
        name="lru_seq",
    )(u, conv_prev8, h0, conv_w, conv_b, wa, wx, ba, bx, lam)


def _lru_step_kernel(u_ref, cp_ref, h0_ref, cw_ref, cb_ref, wa_ref, wx_ref, ba_ref, bx_ref, lam_ref,
                     y_ref, cn_ref, h_ref):
    gate = u_ref[:, 0:C_WIDTH]
    xb = u_ref[:, C_WIDTH:2 * C_WIDTH]
    xc = cb_ref[...] + cw_ref[C_CONV - 1:C_CONV, :] * xb
    for i in range(C_CONV - 1):
        xc = xc + cw_ref[i:i + 1, :] * cp_ref[i]
    a, b = _lru_gates(xc, wa_ref, wx_ref, ba_ref, bx_ref, lam_ref)
    h = a * h0_ref[...] + b
    h_ref[...] = h
    y_ref[...] = _gelu_tanh(gate) * h
    for i in range(C_CONV - 2):
        cn_ref[i] = cp_ref[i + 1]
    cn_ref[C_CONV - 2] = xb


def _lru_step(u, conv_states, h0, j, conv_w, conv_b, wa, wx, ba, bx, lam):
    rows = u.shape[0]
    full = lambda shape: pl.BlockSpec(shape, lambda i: (0,) * len(shape))
    return pl.pallas_call(
        _lru_step_kernel,
        grid=(1,),
        in_specs=[full((rows, 2 * C_WIDTH)),
                  pl.BlockSpec((None, C_CONV - 1, rows, C_WIDTH), lambda i: (j, 0, 0, 0)),
                  full((rows, C_WIDTH)),
                  full((C_CONV, C_WIDTH)), full((1, C_WIDTH)),
                  full((C_BLOCKS, C_BLOCK, C_BLOCK)), full((C_BLOCKS, C_BLOCK, C_BLOCK)),
                  full((1, C_WIDTH)), full((1, C_WIDTH)), full((1, C_WIDTH))],
        out_specs=[full((rows, C_WIDTH)), full((C_CONV - 1, rows, C_WIDTH)), full((rows, C_WIDTH))],
        out_shape=[jax.ShapeDtypeStruct((rows, C_WIDTH), F32),
                   jax.ShapeDtypeStruct((C_CONV - 1, rows, C_WIDTH), F32),
                   jax.ShapeDtypeStruct((rows, C_WIDTH), F32)],
        compiler_params=_params("arbitrary"),
        name="lru_step",
    )(u, conv_states, h0, conv_w, conv_b, wa, wx, ba, bx, lam)


def _pad_rows(w, first, total):
    return jnp.zeros((total, w.shape[1]), w.dtype).at[first:first + w.shape[0]].set(w)


class _Group:
    def __init__(self, x, mod, n_seq, seq_len, states, tm, tf):
        self.x, self.mod, self.n_seq, self.seq_len, self.states, self.tm, self.tf = x, mod, n_seq, seq_len, states, tm, tf
        self.prompt = seq_len > 1
        self.out_shift, self.out_wkv, self.out_gla, self.out_conv, self.out_lru = [], [], [], [], []

    def new_states(self):
        return (jnp.stack(self.out_shift), jnp.stack(self.out_wkv), jnp.stack(self.out_gla),
                jnp.stack(self.out_conv), jnp.stack(self.out_lru))


def _mixer(grp, l, wts, mxu):
    (w_ffn_up, w_ffn_down, w_in_even, w_out_even, a_mu, a_w_up, a_a_up, a_g_up, a_vec, b_alpha_up,
     b_alpha_b, b_norm_w, w_in_odd, w_out_odd, c_conv_w, c_conv_b, c_wa, c_ba, c_wx, c_bx, c_lam,
     final_norm_w) = wts
    x, mod, n_seq, seq_len, tm, prompt = grp.x, grp.mod, grp.n_seq, grp.seq_len, grp.tm, grp.prompt
    st_shift, st_wkv, st_gla, st_conv, st_lru = grp.states
    rows = x.shape[0]
    j = l // 2
    tn_even, tn_odd = 13 * LANES, 8 * LANES
    seg = jnp.kron(jnp.eye(A_HEADS, dtype=F32), jnp.ones((A_HEAD_DIM, A_HEAD_DIM), F32)).astype(BF16)
    if l % 2 == 0:
        if prompt:
            u = _inproj(x, mod, l, mxu['in', l], None, tm, tn_even)
        else:
            u, mxu['in', l] = _inproj(x, mod, l, jnp.swapaxes(w_in_even, 1, 2), j, tm, tn_even,
                                      transposed=True)
        wup_pad = _pad_rows(a_w_up[j], 0, LANES)
        aup_pad = _pad_rows(a_a_up[j], A_DECAY_RANK, LANES)
        vec8 = _pad_rows(a_vec[j], 0, 8)
        alpha_pad = _pad_rows(b_alpha_up[j], 0, LANES)
        alpha_b = b_alpha_b[j].reshape(1, B_KEY_WIDTH)
        mu = a_mu[j].reshape(1, A_PROJ)
        norm_w = b_norm_w[j].reshape(1, B_WIDTH)
        if prompt:
            r, lw, k2, v, kk, al, g = _rwkv_prep(u, st_shift[j].reshape(n_seq, 1, A_PROJ), mu, wup_pad,
                                                 aup_pad, a_g_up[j], vec8, seg, n_seq, 256, True)
            y_a, h_t = _rwkv_chunk(r, lw, k2, v, kk, al, g, vec8, n_seq, _pair_states(st_wkv[j]))
            s_wkv = _unpair_states(h_t)
            s_shift = u.reshape(n_seq, seq_len, EVEN_PROJ)[:, -1, :A_PROJ]
            y_b, s_gla = _gla_chunk(u, n_seq, alpha_pad, alpha_b, norm_w, st_gla[j])
        else:
            r, lw, k2, v, kk, al, g, bonus = _rwkv_prep(u, st_shift[j], mu, wup_pad, aup_pad, a_g_up[j],
                                                        vec8, seg, rows, rows, False)
            y_raw, s_wkv_t = _rwkv_step(r, lw, k2, v, kk, al, jnp.transpose(st_wkv, (0, 2, 3, 4, 1)), j)
            s_wkv = jnp.transpose(s_wkv_t, (3, 0, 1, 2))
            y_a = _rwkv_post(y_raw, bonus, g, vec8, seg, rows)
            s_shift = u[:, :A_PROJ]
            y_b, s_gla = _gla_step(u, alpha_pad, alpha_b, norm_w, st_gla, j, 8)
        if prompt:
            x = _outproj([y_a, y_b], mxu['out', l], None, x, mod, l, min(tm, 512))
        else:
            x, mxu['out', l] = _outproj([y_a, y_b], w_out_even, j, x, mod, l, tm)
        grp.out_shift.append(s_shift)
        grp.out_wkv.append(s_wkv)
        grp.out_gla.append(s_gla)
    else:
        if prompt:
            u = _inproj(x, mod, l, mxu['in', l], None, tm, tn_odd)
        else:
            u, mxu['in', l] = _inproj(x, mod, l, w_in_odd, j, tm, tn_odd)
        lru_w = (c_conv_w[j], c_conv_b[j].reshape(1, C_WIDTH), c_wa[j], c_wx[j],
                 c_ba[j].reshape(1, C_WIDTH), c_bx[j].reshape(1, C_WIDTH), c_lam[j].reshape(1, C_WIDTH))
        if prompt:
            cp8 = jnp.concatenate([jnp.zeros((n_seq, 8 - (C_CONV - 1), C_WIDTH), F32), st_conv[j]], axis=1)
            y_c, tail, h_t = _lru_seq(u, n_seq, cp8, st_lru[j].reshape(n_seq, 1, C_WIDTH), *lru_w, 256)
            s_conv = tail[:, 8 - (C_CONV - 1):, :]
            s_lru = h_t.reshape(n_seq, C_WIDTH)
        else:
            y_c, cn, s_lru = _lru_step(u, jnp.swapaxes(st_conv, 1, 2), st_lru[j], j, *lru_w)
            s_conv = jnp.swapaxes(cn, 0, 1)
        if prompt:
            x = _outproj([y_c], mxu['out', l], None, x, mod, l, min(tm, 512))
        else:
            x, mxu['out', l] = _outproj([y_c], w_out_odd, j, x, mod, l, tm)
        grp.out_conv.append(s_conv)
        grp.out_lru.append(s_lru)
    grp.x = x


def _trunk(sample, prompt, wts):
    w_ffn_up, w_ffn_down, final_norm_w = wts[0], wts[1], wts[-1]
    subs = [(l, which) for l in range(DEPTH) for which in (0, 1)]
    mxu = {('ffn', 0, 0): (w_ffn_up[0, 0].astype(BF16), w_ffn_down[0, 0].astype(BF16))}
    for idx, (l, which) in enumerate(subs):
        final_w = final_norm_w if idx == len(subs) - 1 else None
        weights = mxu['ffn', l, which]
        sample.x = _ffn(sample.x, sample.mod, l, which, weights, sample.tm, sample.tf, final_w=final_w)
        if idx + 1 < len(subs):
            nxt = subs[idx + 1]
            prompt.x, mxu[('ffn',) + nxt] = _ffn(prompt.x, prompt.mod, l, which, weights, prompt.tm, prompt.tf,
                                                 convert=(w_ffn_up, w_ffn_down) + nxt)
        else:
            prompt.x = _ffn(prompt.x, prompt.mod, l, which, weights, prompt.tm, prompt.tf, final_w=final_w)
        if which == 0:
            _mixer(sample, l, wts, mxu)
            _mixer(prompt, l, wts, mxu)


def kernel(x_prompt, x_sample, c_prompt, c_sample, state_rwkv_shift, state_rwkv_wkv, state_gla, state_conv, state_lru, w_ada, b_ada, w_ffn_up, w_ffn_down, w_in_even, w_out_even, a_mu, a_w_up, a_a_up, a_g_up, a_vec, b_alpha_up, b_alpha_b, b_norm_w, w_in_odd, w_out_odd, c_conv_w, c_conv_b, c_wa, c_ba, c_wx, c_bx, c_lam, final_norm_w):
    wts = (w_ffn_up, w_ffn_down, w_in_even, w_out_even, a_mu, a_w_up, a_a_up, a_g_up, a_vec, b_alpha_up,
           b_alpha_b, b_norm_w, w_in_odd, w_out_odd, c_conv_w, c_conv_b, c_wa, c_ba, c_wx, c_bx, c_lam,
           final_norm_w)
    n_p, t_p, _ = x_prompt.shape
    n_s, t_s, _ = x_sample.shape
    assert t_s == 1

    c_rows = n_p + n_s
    pad = (-c_rows) % 16
    c_all = jnp.concatenate([c_sample, c_prompt, jnp.zeros((pad, D_MODEL), F32)], axis=0)
    mod_all = _ada(c_all, w_ada, b_ada)
    mod_p = _Mod(mod_all[:, :, n_s:c_rows].reshape(DEPTH, N_MOD, n_p, 1, D_MODEL), False, t_p)
    mod_s = _Mod(mod_all, True, 1)

    fresh = lambda s: jnp.zeros((s.shape[0], n_p) + s.shape[2:], s.dtype)
    states = (state_rwkv_shift, state_rwkv_wkv, state_gla, state_conv, state_lru)
    sample = _Group(x_sample.reshape(n_s, D_MODEL), mod_s, n_s, 1, states, n_s, 512)
    prompt = _Group(x_prompt.reshape(n_p * t_p, D_MODEL), mod_p, n_p, t_p, tuple(fresh(s) for s in states),
                    1024, 512)
    _trunk(sample, prompt, wts)
    return (prompt.x.reshape(n_p, t_p, D_MODEL), sample.x.reshape(n_s, 1, D_MODEL),
            *prompt.new_states(), *sample.new_states())
```

```python
import functools

import jax
import jax.numpy as jnp
from jax import lax
from jax.experimental import pallas as pl
from jax.experimental.pallas import tpu as pltpu

F32 = jnp.float32
BF16 = jnp.bfloat16

D_MODEL = 2048
DEPTH = 4
N_MOD = 9
D_FF = 5632
NORM_EPS = 1e-6

A_HEADS = 16
A_HEAD_DIM = 64
A_WIDTH = A_HEADS * A_HEAD_DIM
A_DECAY_RANK = 64
A_ICL_RANK = 64
A_GATE_RANK = 128
A_PROJ = 3 * A_WIDTH + A_DECAY_RANK + A_ICL_RANK + A_GATE_RANK
A_GN_EPS = 64e-5

B_HEADS = 4
B_KEY_DIM = 128
B_VAL_DIM = 256
B_KEY_WIDTH = B_HEADS * B_KEY_DIM
B_WIDTH = B_HEADS * B_VAL_DIM
B_ALPHA_RANK = 16
B_TAU = 16.0
B_PROJ = 2 * B_KEY_WIDTH + 2 * B_WIDTH + B_ALPHA_RANK
EVEN_PROJ = A_PROJ + B_PROJ

C_WIDTH = D_MODEL
C_BLOCKS = 8
C_BLOCK = C_WIDTH // C_BLOCKS
C_CONV = 4
C_POW = 8.0

CHUNK = 64
RWKV_STEP_CHUNKS = 2
GLA_STEP_CHUNKS = 2
GLA_SUB = 8
LANES = 128
VMEM_LIMIT = 56 * 1024 * 1024


def _params(*sem):
    return pltpu.CompilerParams(dimension_semantics=sem, vmem_limit_bytes=VMEM_LIMIT)


def _bdot(a, b):
    return jnp.dot(a.astype(BF16), b.astype(BF16), preferred_element_type=F32)


def _split3(x):
    hi = x.astype(BF16)
    r1 = x - hi.astype(F32)
    mid = r1.astype(BF16)
    lo = (r1 - mid.astype(F32)).astype(BF16)
    return hi, mid, lo


def _dot_exact_rhs(a_bf16, x):
    hi, mid, lo = _split3(x)
    f = lambda y: jnp.dot(a_bf16, y, preferred_element_type=F32)
    return f(hi) + f(mid) + f(lo)


def _dot_exact_lhs(x, b_bf16):
    hi, mid, lo = _split3(x)
    f = lambda y: jnp.dot(y, b_bf16, preferred_element_type=F32)
    return f(hi) + f(mid) + f(lo)


def _sigmoid(x):
    return jax.nn.sigmoid(x)


def _softplus(x):
    return jnp.maximum(x, 0.0) + jnp.log1p(jnp.exp(-jnp.abs(x)))


def _gelu_tanh(x):
    return 0.5 * x * (1.0 + jnp.tanh(0.7978845608028654 * (x + 0.044715 * (x * x * x))))


def _norm_mod(x, shift, scale):
    ms = jnp.mean(x * x, axis=-1, keepdims=True)
    return x * lax.rsqrt(ms + NORM_EPS) * (1.0 + scale) + shift


def _ada_kernel(c_ref, w_ref, b_ref, o_ref):
    c = c_ref[...]
    act = (c * _sigmoid(c)).astype(BF16)
    o_ref[...] = jnp.dot(act, w_ref[...].astype(BF16), preferred_element_type=F32) + b_ref[...]


def _ada(c_all, w_ada, b_ada):
    rows = c_all.shape[0]
    tn = 1024
    per = D_MODEL // tn
    b4 = b_ada.reshape(DEPTH, N_MOD * per, 1, tn)
    return pl.pallas_call(
        _ada_kernel,
        grid=(DEPTH, N_MOD * per),
        in_specs=[
            pl.BlockSpec((rows, D_MODEL), lambda l, j: (0, 0)),
            pl.BlockSpec((None, D_MODEL, tn), lambda l, j: (l, 0, j)),
            pl.BlockSpec((None, None, 1, tn), lambda l, j: (l, j, 0, 0)),
        ],
        out_specs=pl.BlockSpec((None, None, rows, tn), lambda l, j: (l, j // per, 0, j % per)),
        out_shape=jax.ShapeDtypeStruct((DEPTH, N_MOD, rows, D_MODEL), F32),
        compiler_params=_params("arbitrary", "arbitrary"),
        name="ada",
    )(c_all, w_ada, b4)


class _Mod:
    def __init__(self, arr, per_row, seq_len):
        self.arr = arr
        self.per_row = per_row
        self.seq_len = seq_len

    def spec(self, layer, m, tm):
        if self.per_row:
            return pl.BlockSpec((None, None, tm, D_MODEL), lambda i, *_: (layer, m, i, 0))
        per_seq = self.seq_len // tm
        return pl.BlockSpec((None, None, None, 1, D_MODEL), lambda i, *_: (layer, m, i // per_seq, 0, 0))


def _ffn_kernel(x_ref, sh_ref, sc_ref, g_ref, wg_ref, wu_ref, wd_ref, *rest, n_j, sub, final, convert):
    rest = list(rest)
    fw_ref = rest.pop(0) if final else None
    f32_tiles = [rest.pop(0) for _ in range(2)] if convert else []
    o_ref = rest.pop(0)
    bf16_tiles = [rest.pop(0) for _ in range(2)] if convert else []
    h_ref, = rest
    j = pl.program_id(1)
    tm = x_ref.shape[0]

    @pl.when(j == 0)
    def _():
        def body(s, carry):
            rows = pl.ds(pl.multiple_of(s * sub, sub), sub)
            sh = sh_ref[...] if sh_ref.shape[0] == 1 else sh_ref[rows, :]
            sc = sc_ref[...] if sc_ref.shape[0] == 1 else sc_ref[rows, :]
            h_ref[rows, :] = _norm_mod(x_ref[rows, :], sh, sc).astype(BF16)
            o_ref[rows, :] = jnp.zeros((sub, D_MODEL), F32)
            return carry
        lax.fori_loop(0, tm // sub, body, 0, unroll=min(2, tm // sub))

    for src, dst in zip(f32_tiles, bf16_tiles):
        dst[...] = src[...].astype(BF16)

    h = h_ref[...]
    gt = jnp.dot(h, wg_ref[...], preferred_element_type=F32)
    ut = jnp.dot(h, wu_ref[...], preferred_element_type=F32)
    act = (gt * _sigmoid(gt) * ut).astype(BF16)
    o_ref[...] += jnp.dot(act, wd_ref[...], preferred_element_type=F32)

    @pl.when(j == n_j - 1)
    def _():
        def body(s, carry):
            rows = pl.ds(pl.multiple_of(s * sub, sub), sub)
            g = g_ref[...] if g_ref.shape[0] == 1 else g_ref[rows, :]
            y = x_ref[rows, :] + 0.5 * g * o_ref[rows, :]
            if final:
                ms = jnp.mean(y * y, axis=-1, keepdims=True)
                y = y * lax.rsqrt(ms + NORM_EPS) * fw_ref[...]
            o_ref[rows, :] = y
            return carry
        lax.fori_loop(0, tm // sub, body, 0, unroll=min(2, tm // sub))


def _ffn(x, mod, layer, which, weights, tm, tf, final_w=None, convert=None):
    m_rows = x.shape[0]
    n_i = m_rows // tm
    n_j = D_FF // tf
    m0 = 6 * which
    sub = min(tm, 128)
    final = final_w is not None
    in_specs = [
        pl.BlockSpec((tm, D_MODEL), lambda i, j: (i, 0), pipeline_mode=pl.Buffered(1)),
        mod.spec(layer, m0, tm), mod.spec(layer, m0 + 1, tm), mod.spec(layer, m0 + 2, tm),
        pl.BlockSpec((D_MODEL, tf), lambda i, j: (0, j)),
        pl.BlockSpec((D_MODEL, tf), lambda i, j: (0, j + n_j)),
        pl.BlockSpec((tf, D_MODEL), lambda i, j: (j, 0)),
    ]
    args = [x, mod.arr, mod.arr, mod.arr, weights[0], weights[0], weights[1]]
    if final:
        in_specs.append(pl.BlockSpec((1, D_MODEL), lambda i, j: (0, 0)))
        args.append(final_w.reshape(1, D_MODEL))
    out_specs = [pl.BlockSpec((tm, D_MODEL), lambda i, j: (i, 0))]
    out_shape = [jax.ShapeDtypeStruct((m_rows, D_MODEL), F32)]
    if convert is not None:
        w_up32, w_down32, l2, s2 = convert
        up_tile = (D_MODEL // n_i, 2 * D_FF // n_j)
        down_tile = (D_FF // n_j, D_MODEL // n_i)
        assert up_tile[0] * n_i == D_MODEL and up_tile[1] * n_j == 2 * D_FF and up_tile[1] % LANES == 0
        assert down_tile[0] * n_j == D_FF and down_tile[1] * n_i == D_MODEL and down_tile[1] % LANES == 0
        in_specs += [pl.BlockSpec((None, None) + up_tile, lambda i, j: (l2, s2, i, j)),
                     pl.BlockSpec((None, None) + down_tile, lambda i, j: (l2, s2, j, i))]
        args += [w_up32, w_down32]
        out_specs += [pl.BlockSpec(up_tile, lambda i, j: (i, j)), pl.BlockSpec(down_tile, lambda i, j: (j, i))]
        out_shape += [jax.ShapeDtypeStruct((D_MODEL, 2 * D_FF), BF16), jax.ShapeDtypeStruct((D_FF, D_MODEL), BF16)]
    out = pl.pallas_call(
        functools.partial(_ffn_kernel, n_j=n_j, sub=sub, final=final, convert=convert is not None),
        grid=(n_i, n_j),
        in_specs=in_specs,
        out_specs=out_specs,
        out_shape=out_shape,
        scratch_shapes=[pltpu.VMEM((tm, D_MODEL), BF16)],
        compiler_params=_params("arbitrary", "arbitrary"),
        name="ffn",
    )(*args)
    return (out[0], (out[1], out[2])) if convert is not None else out[0]


def _inproj_kernel(x_ref, sh_ref, sc_ref, w_ref, o_ref, *rest, sub, emit, transposed):
    h_ref = rest[-1]
    j = pl.program_id(1)
    tm = x_ref.shape[0]

    @pl.when(j == 0)
    def _():
        def body(s, carry):
            rows = pl.ds(pl.multiple_of(s * sub, sub), sub)
            sh = sh_ref[...] if sh_ref.shape[0] == 1 else sh_ref[rows, :]
            sc = sc_ref[...] if sc_ref.shape[0] == 1 else sc_ref[rows, :]
            h_ref[rows, :] = _norm_mod(x_ref[rows, :], sh, sc).astype(BF16)
            return carry
        lax.fori_loop(0, tm // sub, body, 0, unroll=min(2, tm // sub))

    w = w_ref[...]
    if transposed:
        w = w.T
    if emit:
        w = w.astype(BF16)
        rest[0][...] = w
    o_ref[...] = jnp.dot(h_ref[...], w, preferred_element_type=F32)


def _inproj(x, mod, layer, w, widx, tm, tn, transposed=False):
    m_rows = x.shape[0]
    n_out = w.shape[-2] if transposed else w.shape[-1]
    sub = min(tm, 128)
    emit = widx is not None
    if transposed:
        w_spec = pl.BlockSpec((None, tn, D_MODEL), lambda i, j: (widx, j, 0))
    elif emit:
        w_spec = pl.BlockSpec((None, D_MODEL, tn), lambda i, j: (widx, 0, j))
    else:
        w_spec = pl.BlockSpec((D_MODEL, tn), lambda i, j: (0, j))
    out_specs = [pl.BlockSpec((tm, tn), lambda i, j: (i, j))]
    out_shape = [jax.ShapeDtypeStruct((m_rows, n_out), F32)]
    if emit:
        assert m_rows == tm
        out_specs.append(pl.BlockSpec((D_MODEL, tn), lambda i, j: (0, j)))
        out_shape.append(jax.ShapeDtypeStruct((D_MODEL, n_out), BF16))
    out = pl.pallas_call(
        functools.partial(_inproj_kernel, sub=sub, emit=emit, transposed=transposed),
        grid=(m_rows // tm, pl.cdiv(n_out, tn)),
        in_specs=[
            pl.BlockSpec((tm, D_MODEL), lambda i, j: (i, 0)),
            mod.spec(layer, 3, tm), mod.spec(layer, 4, tm),
            w_spec,
        ],
        out_specs=out_specs,
        out_shape=out_shape,
        scratch_shapes=[pltpu.VMEM((tm, D_MODEL), BF16)],
        compiler_params=_params("arbitrary", "arbitrary"),
        name="inproj",
    )(x, mod.arr, mod.arr, w)
    return tuple(out) if emit else out[0]


def _outproj_kernel(*refs, n_in, emit):
    y_refs = refs[:n_in]
    w_refs = refs[n_in:2 * n_in]
    x_ref, g_ref, o_ref = refs[2 * n_in:2 * n_in + 3]
    acc = None
    for k, (y_ref, w_ref) in enumerate(zip(y_refs, w_refs)):
        w = w_ref[...]
        if emit:
            w = w.astype(BF16)
            refs[2 * n_in + 3 + k][...] = w
        part = jnp.dot(y_ref[...].astype(BF16), w, preferred_element_type=F32)
        acc = part if acc is None else acc + part
    o_ref[...] = x_ref[...] + g_ref[...] * acc


def _outproj(ys, w, widx, x, mod, layer, tm):
    m_rows = x.shape[0]
    n_in = len(ys)
    kw = ys[0].shape[1]
    emit = widx is not None
    in_specs = [pl.BlockSpec((tm, kw), lambda i: (i, 0)) for _ in ys]
    if emit:
        assert m_rows == tm
        in_specs += [pl.BlockSpec((None, kw, D_MODEL), lambda i, k=k: (widx, k, 0)) for k in range(n_in)]
        w_args = [w] * n_in
    else:
        in_specs += [pl.BlockSpec((kw, D_MODEL), lambda i: (0, 0))] * n_in
        w_args = list(w)
    in_specs += [pl.BlockSpec((tm, D_MODEL), lambda i: (i, 0)), mod.spec(layer, 5, tm)]
    out_specs = [pl.BlockSpec((tm, D_MODEL), lambda i: (i, 0))]
    out_shape = [jax.ShapeDtypeStruct((m_rows, D_MODEL), F32)]
    if emit:
        out_specs += [pl.BlockSpec((kw, D_MODEL), lambda i: (0, 0))] * n_in
        out_shape += [jax.ShapeDtypeStruct((kw, D_MODEL), BF16)] * n_in
    out = pl.pallas_call(
        functools.partial(_outproj_kernel, n_in=n_in, emit=emit),
        grid=(m_rows // tm,),
        in_specs=in_specs,
        out_specs=out_specs,
        out_shape=out_shape,
        compiler_params=_params("arbitrary"),
        name="outproj",
    )(*ys, *w_args, x, mod.arr)
    return (out[0], tuple(out[1:])) if emit else out[0]


def _rwkv_prep_kernel(u_ref, prev_ref, mu_ref, wup_ref, aup_ref, gup_ref, vec_ref, *rest, seq_mode):
    if seq_mode:
        r_ref, lw_ref, k_ref, v_ref, kk_ref, a_ref, g_ref, carry_ref = rest
    else:
        seg_ref, r_ref, lw_ref, k_ref, v_ref, kk_ref, a_ref, g_ref, bonus_ref = rest
    u = u_ref[...]
    tm = u.shape[0]
    if seq_mode:
        @pl.when(pl.program_id(1) == 0)
        def _():
            carry_ref[...] = prev_ref[...]
        rolled = pltpu.roll(u, 1, axis=0)
        row = lax.broadcasted_iota(jnp.int32, u.shape, 0)
        prev = jnp.where(row == 0, carry_ref[...], rolled)
        carry_ref[...] = u[tm - 1:tm, :]
    else:
        prev = prev_ref[...]
    xs = u + mu_ref[...] * (prev - u)
    r = xs[:, 0:A_WIDTH]
    k = xs[:, A_WIDTH:2 * A_WIDTH]
    v = xs[:, 2 * A_WIDTH:3 * A_WIDTH]
    wa_lo = xs[:, 3 * A_WIDTH:3 * A_WIDTH + LANES]
    g_lo = xs[:, 3 * A_WIDTH + LANES:]
    w0, a0, k_k, k_a, r_k = (vec_ref[i:i + 1, :] for i in range(5))
    w_raw = -_softplus(-(w0 + _bdot(jnp.tanh(wa_lo), wup_ref[...]))) - 0.5
    lw = -jnp.exp(w_raw)
    a = _sigmoid(a0 + _bdot(wa_lo, aup_ref[...]))
    g = _bdot(_sigmoid(g_lo), gup_ref[...])
    kk = k * k_k
    k2 = k * (1.0 + (a - 1.0) * k_a)
    if not seq_mode:
        seg = seg_ref[...]
        kk = kk * lax.rsqrt(jnp.maximum(_dot_exact_lhs(kk * kk, seg), 1e-24))
        bonus_ref[...] = _dot_exact_lhs(r * k2 * r_k, seg) * v
    r_ref[...] = r
    lw_ref[...] = lw
    k_ref[...] = k2
    v_ref[...] = v
    kk_ref[...] = kk
    a_ref[...] = a
    g_ref[...] = g


def _rwkv_prep(u, prev, mu, wup_pad, aup_pad, g_up, vec, seg, n_seq, tm, seq_mode):
    rows = u.shape[0]
    per_seq = rows // n_seq // tm if seq_mode else 1
    grid = (n_seq, per_seq) if seq_mode else (rows // tm, 1)
    rowmap = (lambda b, t: (b * per_seq + t, 0)) if seq_mode else (lambda b, t: (b, 0))
    prev_spec = (pl.BlockSpec((None, 1, A_PROJ), lambda b, t: (b, 0, 0)) if seq_mode
                 else pl.BlockSpec((tm, A_PROJ), rowmap))
    const = lambda shape: pl.BlockSpec(shape, lambda b, t: (0,) * len(shape))
    out = jax.ShapeDtypeStruct((rows, A_WIDTH), F32)
    in_specs = [pl.BlockSpec((tm, A_PROJ), rowmap), prev_spec,
                const((1, A_PROJ)), const((LANES, A_WIDTH)), const((LANES, A_WIDTH)),
                const((A_GATE_RANK, A_WIDTH)), const((8, A_WIDTH))]
    args = [u, prev, mu, wup_pad, aup_pad, g_up, vec]
    if not seq_mode:
        in_specs.append(const((A_WIDTH, A_WIDTH)))
        args.append(seg)
    n_out = 7 if seq_mode else 8
    return pl.pallas_call(
        functools.partial(_rwkv_prep_kernel, seq_mode=seq_mode),
        grid=grid,
        in_specs=in_specs,
        out_specs=[pl.BlockSpec((tm, A_WIDTH), rowmap)] * n_out,
        out_shape=[out] * n_out,
        scratch_shapes=[pltpu.VMEM((1, A_PROJ), F32)] if seq_mode else [],
        compiler_params=_params("arbitrary", "arbitrary"),
        name="rwkv_prep",
    )(*args)


def _rwkv_post_kernel(y_ref, bonus_ref, g_ref, vec_ref, seg_ref, o_ref):
    y = y_ref[...]
    seg = seg_ref[...]
    inv_n = 1.0 / A_HEAD_DIM
    yc = y - _dot_exact_lhs(y, seg) * inv_n
    var = _dot_exact_lhs(yc * yc, seg) * inv_n
    yn = yc * lax.rsqrt(var + A_GN_EPS) * vec_ref[5:6, :] + vec_ref[6:7, :]
    o_ref[...] = (yn + bonus_ref[...]) * g_ref[...]


def _rwkv_post(y, bonus, g, vec, seg, tm):
    rows = y.shape[0]
    spec = pl.BlockSpec((tm, A_WIDTH), lambda i: (i, 0))
    return pl.pallas_call(
        _rwkv_post_kernel,
        grid=(rows // tm,),
        in_specs=[spec, spec, spec,
                  pl.BlockSpec((8, A_WIDTH), lambda i: (0, 0)),
                  pl.BlockSpec((A_WIDTH, A_WIDTH), lambda i: (0, 0))],
        out_specs=spec,
        out_shape=jax.ShapeDtypeStruct((rows, A_WIDTH), F32),
        compiler_params=_params("arbitrary"),
        name="rwkv_post",
    )(y, bonus, g, vec, seg)


def _rwkv_chunk_kernel(r_ref, lw_ref, k_ref, v_ref, kk_ref, a_ref, g_ref, vec_ref, h0_ref, y_ref, hT_ref, h_ref):
    ci = pl.program_id(1)

    @pl.when(ci == 0)
    def _():
        h_ref[...] = h0_ref[...]

    cs = CHUNK
    n2 = 2 * cs
    npair = A_WIDTH // LANES
    pairs = lambda x: jnp.stack([x[:, p * LANES:(p + 1) * LANES] for p in range(npair)])
    vec = vec_ref[...]
    vrow = lambda i: pairs(vec[i:i + 1, :])
    lane = lax.broadcasted_iota(jnp.int32, (1, 1, LANES), 2)
    m0 = (lane < A_HEAD_DIM).astype(F32)
    m1 = 1.0 - m0
    split = lambda x: jnp.concatenate([x * m0, x * m1], axis=1)
    dup = lambda x: jnp.concatenate([x, x], axis=1)
    own = jnp.concatenate([jnp.broadcast_to(m0, (1, cs, LANES)), jnp.broadcast_to(m1, (1, cs, LANES))], axis=1)
    bf = lambda x: x.astype(BF16)

    def bmm(x, y):
        return jnp.einsum('hab,hbc->hac', x, y, preferred_element_type=F32)

    def bmm_nt(x, y):
        return jnp.einsum('han,hbn->hab', x, y, preferred_element_type=F32)

    def bmm_tn(x, y):
        return jnp.einsum('hca,hcb->hab', x, y, preferred_element_type=F32)

    row = lax.broadcasted_iota(jnp.int32, (cs, cs), 0)
    col = lax.broadcasted_iota(jnp.int32, (cs, cs), 1)
    tri = (col <= row).astype(BF16)
    row2 = lax.broadcasted_iota(jnp.int32, (n2, n2), 0)
    col2 = lax.broadcasted_iota(jnp.int32, (n2, n2), 1)
    same = (row2 >= cs) == (col2 >= cs)
    tok_r = jnp.where(row2 >= cs, row2 - cs, row2)
    tok_c = jnp.where(col2 >= cs, col2 - cs, col2)
    strict = jnp.where(same, (tok_c < tok_r).astype(F32), 0.0)
    incl = jnp.where(same, (tok_c <= tok_r).astype(F32), 0.0)
    eye2 = (row2 == col2).astype(F32)
    nrow = lax.broadcasted_iota(jnp.int32, (LANES, LANES), 0)
    ncol = lax.broadcasted_iota(jnp.int32, (LANES, LANES), 1)
    eye_n = (nrow == ncol).astype(F32)

    h = h_ref[...]
    for sub in range(r_ref.shape[0] // cs):
        rows = slice(sub * cs, (sub + 1) * cs)
        lw_all = lw_ref[rows, :]
        c = pairs(_dot_exact_rhs(tri, lw_all))
        lw = pairs(lw_all)
        c_end = c[:, cs - 1:cs, :]
        e_pos = dup(jnp.exp(c))
        e_neg = dup(jnp.exp(-c))
        e_prev = dup(jnp.exp(c - lw))
        e_end = dup(jnp.exp(c_end - c))
        g_end = jnp.exp(c_end)

        kk = split(pairs(kk_ref[rows, :]))
        kk = kk * lax.rsqrt(jnp.maximum(jnp.sum(kk * kk, axis=-1, keepdims=True), 1e-24))
        b_vec = kk * dup(pairs(a_ref[rows, :]))
        r2 = split(pairs(r_ref[rows, :]))
        k2 = split(pairs(k_ref[rows, :]))
        v2 = split(pairs(v_ref[rows, :]))
        bonus = jnp.sum(r2 * k2 * vrow(4), axis=-1, keepdims=True) * v2
        rt = r2 * e_pos
        at = bf(-kk * e_prev)
        bt = bf(b_vec * e_neg)
        kt = bf(k2 * e_neg)
        bh = bf(b_vec * e_end)
        kh = bf(k2 * e_end)
        v2b = bf(v2)

        gram = bmm_nt(jnp.concatenate([at, bf(rt)], axis=1), jnp.concatenate([bt, kt], axis=1))
        a_ab = gram[:, :n2, :n2] * strict
        a_ak = gram[:, :n2, n2:] * strict
        m_b = gram[:, n2:, :n2] * incl
        m_k = gram[:, n2:, n2:] * incl

        tinv = eye2 + a_ab
        a_b = bf(a_ab)
        apow = bmm(a_b, a_b)
        span = 2
        while 2 * span < cs:
            ap_b = bf(apow)
            both = bmm(jnp.concatenate([bf(tinv), ap_b], axis=1), ap_b)
            tinv = tinv + both[:, :n2]
            apow = both[:, n2:]
            span *= 2
        tinv = tinv + bmm(bf(tinv), bf(apow))

        akv = bmm(bf(a_ak), v2b)
        pq = bf(bmm(bf(tinv), jnp.concatenate([at, bf(akv)], axis=-1)))
        mpq = bmm(bf(m_b), pq)
        p2 = rt + mpq[..., :LANES]
        y0 = mpq[..., LANES:] + bmm(bf(m_k), v2b)
        tpq = bmm_tn(bh, pq)
        gm = eye_n * g_end + tpq[..., :LANES]
        hadd = tpq[..., LANES:] + bmm_tn(kh, v2b)

        yh = bmm(jnp.concatenate([bf(p2), bf(gm)], axis=1), bf(h))
        y2 = yh[:, :n2] + y0
        h = yh[:, n2:] + hadd

        inv_n = 1.0 / A_HEAD_DIM
        yc = (y2 - jnp.sum(y2, axis=-1, keepdims=True) * inv_n) * own
        var = jnp.sum(yc * yc, axis=-1, keepdims=True) * inv_n
        z = yc * lax.rsqrt(var + A_GN_EPS) * vrow(5) + vrow(6) * own + bonus
        out = (z[:, :cs] + z[:, cs:]) * pairs(g_ref[rows, :])
        for p in range(npair):
            y_ref[rows, p * LANES:(p + 1) * LANES] = out[p]

    h_ref[...] = h

    @pl.when(ci == pl.num_programs(1) - 1)
    def _():
        hT_ref[...] = h


def _rwkv_chunk(r, lw, k, v, kk, a, g, vec, n_seq, h0):
    rows = r.shape[0]
    step_rows = RWKV_STEP_CHUNKS * CHUNK
    nc = rows // n_seq // step_rows
    npair = A_WIDTH // LANES
    spec = pl.BlockSpec((step_rows, A_WIDTH), lambda b, c: (b * nc + c, 0))
    hspec = pl.BlockSpec((None, npair, LANES, LANES), lambda b, c: (b, 0, 0, 0))
    return pl.pallas_call(
        _rwkv_chunk_kernel,
        grid=(n_seq, nc),
        in_specs=[spec] * 7 + [pl.BlockSpec((8, A_WIDTH), lambda b, c: (0, 0)), hspec],
        out_specs=[spec, hspec],
        out_shape=[jax.ShapeDtypeStruct((rows, A_WIDTH), F32),
                   jax.ShapeDtypeStruct((n_seq, npair, LANES, LANES), F32)],
        scratch_shapes=[pltpu.VMEM((npair, LANES, LANES), F32)],
        compiler_params=_params("arbitrary", "arbitrary"),
        name="rwkv_chunk",
    )(r, lw, k, v, kk, a, g, vec, h0)


def _pair_states(s):
    n = s.shape[0]
    nd = A_HEAD_DIM
    h = jnp.swapaxes(s, -1, -2).reshape(n, A_HEADS // 2, 2, nd, nd)
    out = jnp.zeros((n, A_HEADS // 2, 2 * nd, 2 * nd), s.dtype)
    return out.at[:, :, :nd, :nd].set(h[:, :, 0]).at[:, :, nd:, nd:].set(h[:, :, 1])


def _unpair_states(hp):
    n = hp.shape[0]
    nd = A_HEAD_DIM
    h = jnp.stack([hp[:, :, :nd, :nd], hp[:, :, nd:, nd:]], axis=2).reshape(n, A_HEADS, nd, nd)
    return jnp.swapaxes(h, -1, -2)


def _rwkv_step_kernel(r_ref, lw_ref, k_ref, v_ref, kk_ref, a_ref, s_ref, acc_ref, y_ref, so_ref,
                      r_t, w_t, k_t, v_t, nkk_t, b_t, y_t):
    h = pl.program_id(0)
    nd = A_HEAD_DIM

    @pl.when(h == 0)
    def _():
        kk = kk_ref[...]
        r_t[...] = r_ref[...].T
        w_t[...] = jnp.exp(lw_ref[...]).T
        k_t[...] = k_ref[...].T
        v_t[...] = v_ref[...].T
        nkk_t[...] = (-kk).T
        b_t[...] = (kk * a_ref[...]).T

    first = pl.multiple_of(h * nd, nd)
    rows = pl.ds(first, nd)
    r, w, k, nkk, b = r_t[rows, :], w_t[rows, :], k_t[rows, :], nkk_t[rows, :], b_t[rows, :]

    def body(i, carry):
        s = s_ref[i]
        s_kk = jnp.sum(s * nkk, axis=0, keepdims=True)
        s_new = s * w + s_kk * b + v_t[pl.ds(first + i, 1), :] * k
        so_ref[i] = s_new
        y_t[pl.ds(first + i, 1), :] = jnp.sum(s_new * r, axis=0, keepdims=True)
        return carry

    lax.fori_loop(0, nd, body, 0, unroll=4)

    @pl.when(h == pl.num_programs(0) - 1)
    def _():
        y_ref[...] = y_t[...].T


def _rwkv_step(r, lw, k, v, kk, a, states_t, j, new_states):
    n = r.shape[0]
    nd = A_HEAD_DIM
    rspec = pl.BlockSpec((n, A_WIDTH), lambda h: (0, 0))
    sspec = pl.BlockSpec((None, None, nd, nd, n), lambda h: (j, h, 0, 0, 0))
    return pl.pallas_call(
        _rwkv_step_kernel,
        grid=(A_HEADS,),
        in_specs=[rspec] * 6 + [sspec, pl.BlockSpec(memory_space=pl.ANY)],
        out_specs=[rspec, sspec],
        out_shape=[jax.ShapeDtypeStruct((n, A_WIDTH), F32), jax.ShapeDtypeStruct(states_t.shape, F32)],
        input_output_aliases={7: 1},
        scratch_shapes=[pltpu.VMEM((A_WIDTH, n), F32)] * 7,
        compiler_params=_params("arbitrary"),
        name="rwkv_step",
    )(r, lw, k, v, kk, a, states_t, new_states)


def _gla_log_alpha(a128, aup_ref, ab_ref):
    lane = lax.broadcasted_iota(jnp.int32, a128.shape, 1)
    a_lo = jnp.where(lane < B_ALPHA_RANK, a128, 0.0)
    x = _bdot(a_lo, aup_ref[...]) + ab_ref[...]
    return -_softplus(-x) * (1.0 / B_TAU)


def _gla_block(u_ref, aup_ref, ab_ref, nw_ref, o_ref, s_ref):
    ub = u_ref[...]
    cs = ub.shape[0]
    q_all = ub[:, 0:B_KEY_WIDTH] * (B_KEY_DIM ** -0.5)
    k_all = ub[:, B_KEY_WIDTH:2 * B_KEY_WIDTH]
    v_all = ub[:, 2 * B_KEY_WIDTH:2 * B_KEY_WIDTH + B_WIDTH]
    g_all = ub[:, 2 * B_KEY_WIDTH + B_WIDTH:2 * B_KEY_WIDTH + 2 * B_WIDTH]
    a128 = ub[:, 2 * B_KEY_WIDTH + 2 * B_WIDTH:2 * B_KEY_WIDTH + 2 * B_WIDTH + LANES]
    log_a = _gla_log_alpha(a128, aup_ref, ab_ref)

    row = lax.broadcasted_iota(jnp.int32, (cs, cs), 0)
    col = lax.broadcasted_iota(jnp.int32, (cs, cs), 1)
    cum = _dot_exact_rhs((col <= row).astype(BF16), log_a)
    e_pos = jnp.exp(cum)
    cum_end = cum[cs - 1:cs, :]
    e_end = jnp.exp(cum_end - cum)
    g_end = jnp.exp(cum_end)
    nk = B_KEY_DIM
    eye_k = (lax.broadcasted_iota(jnp.int32, (nk, nk), 0) == lax.broadcasted_iota(jnp.int32, (nk, nk), 1)).astype(F32)

    nblk = cs // GLA_SUB
    kw = B_KEY_WIDTH
    cum3 = cum.reshape(nblk, GLA_SUB, kw)
    before = jnp.concatenate([jnp.zeros((1, 1, kw), F32), cum3[:nblk - 1, GLA_SUB - 1:GLA_SUB, :]], axis=0)
    q_rel = q_all * jnp.exp(cum - jnp.broadcast_to(before, (nblk, GLA_SUB, kw)).reshape(cs, kw))
    tok = lax.broadcasted_iota(jnp.int32, (cs, 1), 0)
    q_parts, k_parts = [], []
    for blk in range(1, nblk):
        first = blk * GLA_SUB
        k_parts.append(jnp.where(tok < first, k_all * jnp.exp(jnp.minimum(before[blk] - cum, 0.0)), 0.0))
        q_parts.append(jnp.where((tok >= first) & (tok < first + GLA_SUB), q_rel, 0.0))

    q3 = q_all.reshape(nblk, GLA_SUB, kw)
    k3 = k_all.reshape(nblk, GLA_SUB, kw)
    v3 = v_all.reshape(nblk, GLA_SUB, B_WIDTH)
    sub_i = lax.broadcasted_iota(jnp.int32, (1, GLA_SUB, 1), 1)
    o_diag = [jnp.zeros((nblk, GLA_SUB, B_VAL_DIM), F32) for _ in range(B_HEADS)]
    for jj in range(GLA_SUB):
        pair = q3 * k3[:, jj:jj + 1, :] * jnp.exp(jnp.minimum(cum3 - cum3[:, jj:jj + 1, :], 0.0))
        pair = jnp.where(sub_i >= jj, pair, 0.0)
        for h in range(B_HEADS):
            score = jnp.sum(pair[:, :, h * nk:(h + 1) * nk], axis=-1, keepdims=True)
            o_diag[h] = o_diag[h] + score * v3[:, jj:jj + 1, h * B_VAL_DIM:(h + 1) * B_VAL_DIM]

    for h in range(B_HEADS):
        ks = slice(h * B_KEY_DIM, (h + 1) * B_KEY_DIM)
        vs = slice(h * B_VAL_DIM, (h + 1) * B_VAL_DIM)
        qe = q_all[:, ks] * e_pos[:, ks]
        ke = k_all[:, ks] * e_end[:, ks]
        vh = v_all[:, vs]
        s = s_ref[h]
        q_cat = jnp.concatenate([part[:, ks] for part in q_parts], axis=-1).astype(BF16)
        k_cat = jnp.concatenate([part[:, ks] for part in k_parts], axis=-1).astype(BF16)
        scores = jnp.einsum('id,jd->ij', q_cat, k_cat, preferred_element_type=F32)
        o = _bdot(scores, vh) + _bdot(qe, s) + o_diag[h].reshape(cs, B_VAL_DIM)
        g_col = jnp.sum(eye_k * g_end[:, ks], axis=-1, keepdims=True)
        s_ref[h] = s * g_col + jnp.einsum('jd,jv->dv', ke.astype(BF16), vh.astype(BF16),
                                           preferred_element_type=F32)
        ms = jnp.mean(o * o, axis=-1, keepdims=True)
        gh = g_all[:, vs]
        o_ref[:, vs] = o * lax.rsqrt(ms + NORM_EPS) * nw_ref[:, vs] * (gh * _sigmoid(gh))


def _gla_chunk_kernel(u_ref, aup_ref, ab_ref, nw_ref, s0_ref, o_ref, sT_ref, s_ref):
    ci = pl.program_id(1)

    @pl.when(ci == 0)
    def _():
        s_ref[...] = s0_ref[...]

    for sub in range(u_ref.shape[0] // CHUNK):
        rows = pl.ds(sub * CHUNK, CHUNK)
        _gla_block(u_ref.at[rows, :], aup_ref, ab_ref, nw_ref, o_ref.at[rows, :], s_ref)

    @pl.when(ci == pl.num_programs(1) - 1)
    def _():
        sT_ref[...] = s_ref[...]


def _gla_chunk(u, n_seq, alpha_up_pad, alpha_b, norm_w, s0):
    rows = u.shape[0]
    step_rows = GLA_STEP_CHUNKS * CHUNK
    nc = rows // n_seq // step_rows
    return pl.pallas_call(
        _gla_chunk_kernel,
        grid=(n_seq, nc),
        in_specs=[
            pl.BlockSpec((step_rows, A_PROJ), lambda b, c: (b * nc + c, 1)),
            pl.BlockSpec((LANES, B_KEY_WIDTH), lambda b, c: (0, 0)),
            pl.BlockSpec((1, B_KEY_WIDTH), lambda b, c: (0, 0)),
            pl.BlockSpec((1, B_WIDTH), lambda b, c: (0, 0)),
            pl.BlockSpec((None, B_HEADS, B_KEY_DIM, B_VAL_DIM), lambda b, c: (b, 0, 0, 0)),
        ],
        out_specs=[
            pl.BlockSpec((step_rows, B_WIDTH), lambda b, c: (b * nc + c, 0)),
            pl.BlockSpec((None, B_HEADS, B_KEY_DIM, B_VAL_DIM), lambda b, c: (b, 0, 0, 0)),
        ],
        out_shape=[jax.ShapeDtypeStruct((rows, B_WIDTH), F32),
                   jax.ShapeDtypeStruct((n_seq, B_HEADS, B_KEY_DIM, B_VAL_DIM), F32)],
        scratch_shapes=[pltpu.VMEM((B_HEADS, B_KEY_DIM, B_VAL_DIM), F32)],
        compiler_params=_params("arbitrary", "arbitrary"),
        name="gla_chunk",
    )(u, alpha_up_pad, alpha_b, norm_w, s0)


def _gla_step_kernel(u_ref, aup_ref, ab_ref, nw_ref, s_ref, acc_ref, o_ref, so_ref, la_ref):
    nb = u_ref.shape[0]
    nk = B_KEY_DIM
    off_k, off_v, off_g = B_KEY_WIDTH, 2 * B_KEY_WIDTH, 2 * B_KEY_WIDTH + B_WIDTH
    off_a = 2 * B_KEY_WIDTH + 2 * B_WIDTH
    la_ref[...] = _gla_log_alpha(u_ref[:, off_a:off_a + LANES], aup_ref, ab_ref)
    eye = (lax.broadcasted_iota(jnp.int32, (nk, nk), 0) == lax.broadcasted_iota(jnp.int32, (nk, nk), 1)).astype(F32)
    col = lambda x: jnp.sum(eye * x, axis=-1, keepdims=True)

    def body(n, carry):
        row = pl.ds(n, 1)
        ur = u_ref[row, :]
        dec_all = jnp.exp(la_ref[row, :])
        outs = []
        for h in range(B_HEADS):
            q = ur[:, h * nk:(h + 1) * nk] * (B_KEY_DIM ** -0.5)
            k = ur[:, off_k + h * nk:off_k + (h + 1) * nk]
            v = ur[:, off_v + h * B_VAL_DIM:off_v + (h + 1) * B_VAL_DIM]
            g = ur[:, off_g + h * B_VAL_DIM:off_g + (h + 1) * B_VAL_DIM]
            dec = dec_all[:, h * nk:(h + 1) * nk]
            s = s_ref[n, h]
            qk = jnp.sum(q * k, axis=-1, keepdims=True)
            o = qk * v + jnp.sum(col(q * dec) * s, axis=0, keepdims=True)
            so_ref[n, h] = s * col(dec) + col(k) * v
            ms = jnp.mean(o * o, axis=-1, keepdims=True)
            outs.append(o * lax.rsqrt(ms + NORM_EPS) * nw_ref[:, h * B_VAL_DIM:(h + 1) * B_VAL_DIM]
                        * (g * _sigmoid(g)))
        o_ref[row, :] = jnp.concatenate(outs, axis=-1)
        return carry

    lax.fori_loop(0, nb, body, 0)


def _gla_step(u, alpha_up_pad, alpha_b, norm_w, states, j, nb, new_states):
    n = u.shape[0]
    sspec = pl.BlockSpec((None, nb, B_HEADS, B_KEY_DIM, B_VAL_DIM), lambda i: (j, i, 0, 0, 0))
    return pl.pallas_call(
        _gla_step_kernel,
        grid=(n // nb,),
        in_specs=[pl.BlockSpec((nb, A_PROJ), lambda i: (i, 1)),
                  pl.BlockSpec((LANES, B_KEY_WIDTH), lambda i: (0, 0)),
                  pl.BlockSpec((1, B_KEY_WIDTH), lambda i: (0, 0)),
                  pl.BlockSpec((1, B_WIDTH), lambda i: (0, 0)),
                  sspec, pl.BlockSpec(memory_space=pl.ANY)],
        out_specs=[pl.BlockSpec((nb, B_WIDTH), lambda i: (i, 0)), sspec],
        out_shape=[jax.ShapeDtypeStruct((n, B_WIDTH), F32), jax.ShapeDtypeStruct(states.shape, F32)],
        input_output_aliases={5: 1},
        scratch_shapes=[pltpu.VMEM((nb, B_KEY_WIDTH), F32)],
        compiler_params=_params("arbitrary"),
        name="gla_step",
    )(u, alpha_up_pad, alpha_b, norm_w, states, new_states)


def _lru_gates(xc, wa_ref, wx_ref, ba_ref, bx_ref, lam_ref):
    ra, ix = [], []
    for nb in range(C_BLOCKS):
        xb = xc[:, nb * C_BLOCK:(nb + 1) * C_BLOCK].astype(BF16)
        ra.append(jnp.dot(xb, wa_ref[nb].astype(BF16), preferred_element_type=F32))
        ix.append(jnp.dot(xb, wx_ref[nb].astype(BF16), preferred_element_type=F32))
    r = _sigmoid(jnp.concatenate(ra, axis=-1) + ba_ref[...])
    i_g = _sigmoid(jnp.concatenate(ix, axis=-1) + bx_ref[...])
    log_a = -C_POW * r * _softplus(-lam_ref[...])
    a = jnp.exp(log_a)
    b = jnp.sqrt(1.0 - a * a) * (i_g * xc)
    return a, b


def _lru_seq_kernel(u_ref, cp_ref, h0_ref, cw_ref, cb_ref, wa_ref, wx_ref, ba_ref, bx_ref, lam_ref,
                    y_ref, tail_ref, hT_ref, carry_ref, h_ref, a_s, b_s):
    ti = pl.program_id(1)

    @pl.when(ti == 0)
    def _():
        carry_ref[...] = cp_ref[...]
        h_ref[...] = h0_ref[...]

    gate = u_ref[:, 0:C_WIDTH]
    xb = u_ref[:, C_WIDTH:2 * C_WIDTH]
    tm = xb.shape[0]
    carry = carry_ref[...]
    row8 = lax.broadcasted_iota(jnp.int32, (8, C_WIDTH), 0)
    xc = cb_ref[...] + cw_ref[C_CONV - 1:C_CONV, :] * xb
    for s in range(1, C_CONV):
        rolled = pltpu.roll(xb, s, axis=0)
        head = jnp.where(row8 < s, pltpu.roll(carry, s, axis=0), rolled[0:8, :])
        shifted = jnp.concatenate([head, rolled[8:, :]], axis=0)
        xc = xc + cw_ref[C_CONV - 1 - s:C_CONV - s, :] * shifted
    carry_ref[...] = xb[tm - 8:tm, :]
    tail_ref[...] = xb[tm - 8:tm, :]

    a, b = _lru_gates(xc, wa_ref, wx_ref, ba_ref, bx_ref, lam_ref)
    a_s[...] = a
    b_s[...] = b

    def body(t, h):
        h = a_s[pl.ds(t, 1), :] * h + b_s[pl.ds(t, 1), :]
        b_s[pl.ds(t, 1), :] = h
        return h

    h_last = lax.fori_loop(0, tm, body, h_ref[...], unroll=8)
    h_ref[...] = h_last
    hT_ref[...] = h_last
    y_ref[...] = _gelu_tanh(gate) * b_s[...]


def _lru_seq(u, n_seq, conv_prev8, h0, conv_w, conv_b, wa, wx, ba, bx, lam, tm):
    rows = u.shape[0]
    per_seq = rows // n_seq // tm
    const = lambda shape: pl.BlockSpec(shape, lambda b, t: (0,) * len(shape))
    return pl.pallas_call(
        _lru_seq_kernel,
        grid=(n_seq, per_seq),
        in_specs=[
            pl.BlockSpec((tm, 2 * C_WIDTH), lambda b, t: (b * per_seq + t, 0)),
            pl.BlockSpec((None, 8, C_WIDTH), lambda b, t: (b, 0, 0)),
            pl.BlockSpec((None, 1, C_WIDTH), lambda b, t: (b, 0, 0)),
            const((C_CONV, C_WIDTH)), const((1, C_WIDTH)),
            const((C_BLOCKS, C_BLOCK, C_BLOCK)), const((C_BLOCKS, C_BLOCK, C_BLOCK)),
            const((1, C_WIDTH)), const((1, C_WIDTH)), const((1, C_WIDTH)),
        ],
        out_specs=[
            pl.BlockSpec((tm, C_WIDTH), lambda b, t: (b * per_seq + t, 0)),
            pl.BlockSpec((None, 8, C_WIDTH), lambda b, t: (b, 0, 0)),
            pl.BlockSpec((None, 1, C_WIDTH), lambda b, t: (b, 0, 0)),
        ],
        out_shape=[jax.ShapeDtypeStruct((rows, C_WIDTH), F32),
                   jax.ShapeDtypeStruct((n_seq, 8, C_WIDTH), F32),
                   jax.ShapeDtypeStruct((n_seq, 1, C_WIDTH), F32)],
        scratch_shapes=[pltpu.VMEM((8, C_WIDTH), F32), pltpu.VMEM((1, C_WIDTH), F32),
                        pltpu.VMEM((tm, C_WIDTH), F32), pltpu.VMEM((tm, C_WIDTH), F32)],
        compiler_params=_params("arbitrary", "arbitrary"),
        name="lru_seq",
    )(u, conv_prev8, h0, conv_w, conv_b, wa, wx, ba, bx, lam)


def _lru_step_kernel(u_ref, cp_ref, h0_ref, cw_ref, cb_ref, wa_ref, wx_ref, ba_ref, bx_ref, lam_ref,
                     y_ref, cn_ref, h_ref):
    gate = u_ref[:, 0:C_WIDTH]
    xb = u_ref[:, C_WIDTH:2 * C_WIDTH]
    xc = cb_ref[...] + cw_ref[C_CONV - 1:C_CONV, :] * xb
    for i in range(C_CONV - 1):
        xc = xc + cw_ref[i:i + 1, :] * cp_ref[i]
    a, b = _lru_gates(xc, wa_ref, wx_ref, ba_ref, bx_ref, lam_ref)
    h = a * h0_ref[...] + b
    h_ref[...] = h
    y_ref[...] = _gelu_tanh(gate) * h
    for i in range(C_CONV - 2):
        cn_ref[i] = cp_ref[i + 1]
    cn_ref[C_CONV - 2] = xb


def _lru_step(u, conv_states, h0, j, conv_w, conv_b, wa, wx, ba, bx, lam):
    rows = u.shape[0]
    full = lambda shape: pl.BlockSpec(shape, lambda i: (0,) * len(shape))
    return pl.pallas_call(
        _lru_step_kernel,
        grid=(1,),
        in_specs=[full((rows, 2 * C_WIDTH)),
                  pl.BlockSpec((None, C_CONV - 1, rows, C_WIDTH), lambda i: (j, 0, 0, 0)),
                  full((rows, C_WIDTH)),
                  full((C_CONV, C_WIDTH)), full((1, C_WIDTH)),
                  full((C_BLOCKS, C_BLOCK, C_BLOCK)), full((C_BLOCKS, C_BLOCK, C_BLOCK)),
                  full((1, C_WIDTH)), full((1, C_WIDTH)), full((1, C_WIDTH))],
        out_specs=[full((rows, C_WIDTH)), full((C_CONV - 1, rows, C_WIDTH)), full((rows, C_WIDTH))],
        out_shape=[jax.ShapeDtypeStruct((rows, C_WIDTH), F32),
                   jax.ShapeDtypeStruct((C_CONV - 1, rows, C_WIDTH), F32),
                   jax.ShapeDtypeStruct((rows, C_WIDTH), F32)],
        compiler_params=_params("arbitrary"),
        name="lru_step",
    )(u, conv_states, h0, conv_w, conv_b, wa, wx, ba, bx, lam)


def _pad_rows(w, first, total):
    return jnp.zeros((total, w.shape[1]), w.dtype).at[first:first + w.shape[0]].set(w)


class _Group:
    def __init__(self, x, mod, n_seq, seq_len, states, tm, tf):
        self.x, self.mod, self.n_seq, self.seq_len, self.states, self.tm, self.tf = x, mod, n_seq, seq_len, states, tm, tf
        self.prompt = seq_len > 1
        self.out_shift, self.out_wkv, self.out_gla, self.out_conv, self.out_lru = [], [], [], [], []
        self.new_wkv_t = self.new_gla = None

    def new_states(self):
        wkv = jnp.stack(self.out_wkv) if self.new_wkv_t is None else jnp.transpose(self.new_wkv_t, (0, 4, 1, 2, 3))
        gla = jnp.stack(self.out_gla) if self.new_gla is None else self.new_gla
        return jnp.stack(self.out_shift), wkv, gla, jnp.stack(self.out_conv), jnp.stack(self.out_lru)


def _mixer(grp, l, wts, mxu):
    (w_ffn_up, w_ffn_down, w_in_even, w_out_even, a_mu, a_w_up, a_a_up, a_g_up, a_vec, b_alpha_up,
     b_alpha_b, b_norm_w, w_in_odd, w_out_odd, c_conv_w, c_conv_b, c_wa, c_ba, c_wx, c_bx, c_lam,
     final_norm_w) = wts
    x, mod, n_seq, seq_len, tm, prompt = grp.x, grp.mod, grp.n_seq, grp.seq_len, grp.tm, grp.prompt
    st_shift, st_wkv, st_gla, st_conv, st_lru = grp.states
    rows = x.shape[0]
    j = l // 2
    tn_even, tn_odd = 13 * LANES, 8 * LANES
    seg = jnp.kron(jnp.eye(A_HEADS, dtype=F32), jnp.ones((A_HEAD_DIM, A_HEAD_DIM), F32)).astype(BF16)
    if l % 2 == 0:
        if prompt:
            u = _inproj(x, mod, l, mxu['in', l], None, tm, tn_even)
        else:
            u, mxu['in', l] = _inproj(x, mod, l, jnp.swapaxes(w_in_even, 1, 2), j, tm, tn_even,
                                      transposed=True)
        wup_pad = _pad_rows(a_w_up[j], 0, LANES)
        aup_pad = _pad_rows(a_a_up[j], A_DECAY_RANK, LANES)
        vec8 = _pad_rows(a_vec[j], 0, 8)
        alpha_pad = _pad_rows(b_alpha_up[j], 0, LANES)
        alpha_b = b_alpha_b[j].reshape(1, B_KEY_WIDTH)
        mu = a_mu[j].reshape(1, A_PROJ)
        norm_w = b_norm_w[j].reshape(1, B_WIDTH)
        if prompt:
            r, lw, k2, v, kk, al, g = _rwkv_prep(u, st_shift[j].reshape(n_seq, 1, A_PROJ), mu, wup_pad,
                                                 aup_pad, a_g_up[j], vec8, seg, n_seq, 256, True)
            y_a, h_t = _rwkv_chunk(r, lw, k2, v, kk, al, g, vec8, n_seq, _pair_states(st_wkv[j]))
            s_wkv = _unpair_states(h_t)
            s_shift = u.reshape(n_seq, seq_len, EVEN_PROJ)[:, -1, :A_PROJ]
            y_b, s_gla = _gla_chunk(u, n_seq, alpha_pad, alpha_b, norm_w, st_gla[j])
        else:
            r, lw, k2, v, kk, al, g, bonus = _rwkv_prep(u, st_shift[j], mu, wup_pad, aup_pad, a_g_up[j],
                                                        vec8, seg, rows, rows, False)
            st_wkv_t = jnp.transpose(st_wkv, (0, 2, 3, 4, 1))
            y_raw, grp.new_wkv_t = _rwkv_step(r, lw, k2, v, kk, al, st_wkv_t, j,
                                              jnp.zeros_like(st_wkv_t) if grp.new_wkv_t is None else grp.new_wkv_t)
            y_a = _rwkv_post(y_raw, bonus, g, vec8, seg, rows)
            s_shift = u[:, :A_PROJ]
            y_b, grp.new_gla = _gla_step(u, alpha_pad, alpha_b, norm_w, st_gla, j, 8,
                                         jnp.zeros_like(st_gla) if grp.new_gla is None else grp.new_gla)
        if prompt:
            x = _outproj([y_a, y_b], mxu['out', l], None, x, mod, l, min(tm, 512))
        else:
            x, mxu['out', l] = _outproj([y_a, y_b], w_out_even, j, x, mod, l, tm)
        grp.out_shift.append(s_shift)
        if prompt:
            grp.out_wkv.append(s_wkv)
            grp.out_gla.append(s_gla)
    else:
        if prompt:
            u = _inproj(x, mod, l, mxu['in', l], None, tm, tn_odd)
        else:
            u, mxu['in', l] = _inproj(x, mod, l, w_in_odd, j, tm, tn_odd)
        lru_w = (c_conv_w[j], c_conv_b[j].reshape(1, C_WIDTH), c_wa[j], c_wx[j],
                 c_ba[j].reshape(1, C_WIDTH), c_bx[j].reshape(1, C_WIDTH), c_lam[j].reshape(1, C_WIDTH))
        if prompt:
            cp8 = jnp.concatenate([jnp.zeros((n_seq, 8 - (C_CONV - 1), C_WIDTH), F32), st_conv[j]], axis=1)
            y_c, tail, h_t = _lru_seq(u, n_seq, cp8, st_lru[j].reshape(n_seq, 1, C_WIDTH), *lru_w, 256)
            s_conv = tail[:, 8 - (C_CONV - 1):, :]
            s_lru = h_t.reshape(n_seq, C_WIDTH)
        else:
            y_c, cn, s_lru = _lru_step(u, jnp.swapaxes(st_conv, 1, 2), st_lru[j], j, *lru_w)
            s_conv = jnp.swapaxes(cn, 0, 1)
        if prompt:
            x = _outproj([y_c], mxu['out', l], None, x, mod, l, min(tm, 512))
        else:
            x, mxu['out', l] = _outproj([y_c], w_out_odd, j, x, mod, l, tm)
        grp.out_conv.append(s_conv)
        grp.out_lru.append(s_lru)
    grp.x = x


def _trunk(sample, prompt, wts):
    w_ffn_up, w_ffn_down, final_norm_w = wts[0], wts[1], wts[-1]
    subs = [(l, which) for l in range(DEPTH) for which in (0, 1)]
    mxu = {('ffn', 0, 0): (w_ffn_up[0, 0].astype(BF16), w_ffn_down[0, 0].astype(BF16))}
    for idx, (l, which) in enumerate(subs):
        final_w = final_norm_w if idx == len(subs) - 1 else None
        weights = mxu['ffn', l, which]
        sample.x = _ffn(sample.x, sample.mod, l, which, weights, sample.tm, sample.tf, final_w=final_w)
        if idx + 1 < len(subs):
            nxt = subs[idx + 1]
            prompt.x, mxu[('ffn',) + nxt] = _ffn(prompt.x, prompt.mod, l, which, weights, prompt.tm, prompt.tf,
                                                 convert=(w_ffn_up, w_ffn_down) + nxt)
        else:
            prompt.x = _ffn(prompt.x, prompt.mod, l, which, weights, prompt.tm, prompt.tf, final_w=final_w)
        if which == 0:
            _mixer(sample, l, wts, mxu)
            _mixer(prompt, l, wts, mxu)


def kernel(x_prompt, x_sample, c_prompt, c_sample, state_rwkv_shift, state_rwkv_wkv, state_gla, state_conv, state_lru, w_ada, b_ada, w_ffn_up, w_ffn_down, w_in_even, w_out_even, a_mu, a_w_up, a_a_up, a_g_up, a_vec, b_alpha_up, b_alpha_b, b_norm_w, w_in_odd, w_out_odd, c_conv_w, c_conv_b, c_wa, c_ba, c_wx, c_bx, c_lam, final_norm_w):
    wts = (w_ffn_up, w_ffn_down, w_in_even, w_out_even, a_mu, a_w_up, a_a_up, a_g_up, a_vec, b_alpha_up,
           b_alpha_b, b_norm_w, w_in_odd, w_out_odd, c_conv_w, c_conv_b, c_wa, c_ba, c_wx, c_bx, c_lam,
           final_norm_w)
    n_p, t_p, _ = x_prompt.shape
    n_s, t_s, _ = x_sample.shape
    assert t_s == 1

    c_rows = n_p + n_s
    pad = (-c_rows) % 16
    c_all = jnp.concatenate([c_sample, c_prompt, jnp.zeros((pad, D_MODEL), F32)], axis=0)
    mod_all = _ada(c_all, w_ada, b_ada)
    mod_p = _Mod(mod_all[:, :, n_s:c_rows].reshape(DEPTH, N_MOD, n_p, 1, D_MODEL), False, t_p)
    mod_s = _Mod(mod_all, True, 1)

    fresh = lambda s: jnp.zeros((s.shape[0], n_p) + s.shape[2:], s.dtype)
    states = (state_rwkv_shift, state_rwkv_wkv, state_gla, state_conv, state_lru)
    sample = _Group(x_sample.reshape(n_s, D_MODEL), mod_s, n_s, 1, states, n_s, 512)
    prompt = _Group(x_prompt.reshape(n_p * t_p, D_MODEL), mod_p, n_p, t_p, tuple(fresh(s) for s in states),
                    1024, 512)
    _trunk(sample, prompt, wts)
    return (prompt.x.reshape(n_p, t_p, D_MODEL), sample.x.reshape(n_s, 1, D_MODEL),
            *prompt.new_states(), *sample.new_states())
```

```python
import functools

import jax
import jax.numpy as jnp
from jax import lax
from jax.experimental import pallas as pl
from jax.experimental.pallas import tpu as pltpu

F32 = jnp.float32
BF16 = jnp.bfloat16

D_MODEL = 2048
DEPTH = 4
N_MOD = 9
D_FF = 5632
NORM_EPS = 1e-6

A_HEADS = 16
A_HEAD_DIM = 64
A_WIDTH = A_HEADS * A_HEAD_DIM
A_DECAY_RANK = 64
A_ICL_RANK = 64
A_GATE_RANK = 128
A_PROJ = 3 * A_WIDTH + A_DECAY_RANK + A_ICL_RANK + A_GATE_RANK
A_GN_EPS = 64e-5

B_HEADS = 4
B_KEY_DIM = 128
B_VAL_DIM = 256
B_KEY_WIDTH = B_HEADS * B_KEY_DIM
B_WIDTH = B_HEADS * B_VAL_DIM
B_ALPHA_RANK = 16
B_TAU = 16.0
B_PROJ = 2 * B_KEY_WIDTH + 2 * B_WIDTH + B_ALPHA_RANK
EVEN_PROJ = A_PROJ + B_PROJ

C_WIDTH = D_MODEL
C_BLOCKS = 8
C_BLOCK = C_WIDTH // C_BLOCKS
C_CONV = 4
C_POW = 8.0

LANES = 128
SUBLANES = 8
VMEM_LIMIT = 56 * 1024 * 1024

CHUNK = 64
RWKV_STEP_CHUNKS = 2
GLA_STEP_CHUNKS = 2
GLA_SUB = SUBLANES
PROMPT_ROW_TILE = 1024
OUTPROJ_ROW_TILE = 512
TOKENWISE_ROW_TILE = 256
FFN_COL_TILE = 512
EVEN_COL_TILE = 13 * LANES
ODD_COL_TILE = 8 * LANES
STEP_ROWS = SUBLANES


def _params(*sem):
    return pltpu.CompilerParams(dimension_semantics=sem, vmem_limit_bytes=VMEM_LIMIT)


def _bdot(a, b):
    return jnp.dot(a.astype(BF16), b.astype(BF16), preferred_element_type=F32)


def _split3(x):
    hi = x.astype(BF16)
    r1 = x - hi.astype(F32)
    mid = r1.astype(BF16)
    lo = (r1 - mid.astype(F32)).astype(BF16)
    return hi, mid, lo


def _dot_exact_rhs(a_bf16, x):
    hi, mid, lo = _split3(x)
    f = lambda y: jnp.dot(a_bf16, y, preferred_element_type=F32)
    return f(hi) + f(mid) + f(lo)


def _dot_exact_lhs(x, b_bf16):
    hi, mid, lo = _split3(x)
    f = lambda y: jnp.dot(y, b_bf16, preferred_element_type=F32)
    return f(hi) + f(mid) + f(lo)


def _sigmoid(x):
    return jax.nn.sigmoid(x)


def _softplus(x):
    return jnp.maximum(x, 0.0) + jnp.log1p(jnp.exp(-jnp.abs(x)))


def _gelu_tanh(x):
    return 0.5 * x * (1.0 + jnp.tanh(0.7978845608028654 * (x + 0.044715 * (x * x * x))))


def _norm_mod(x, shift, scale):
    ms = jnp.mean(x * x, axis=-1, keepdims=True)
    return x * lax.rsqrt(ms + NORM_EPS) * (1.0 + scale) + shift


def _ada_kernel(c_ref, w_ref, b_ref, o_ref):
    c = c_ref[...]
    act = (c * _sigmoid(c)).astype(BF16)
    o_ref[...] = jnp.dot(act, w_ref[...].astype(BF16), preferred_element_type=F32) + b_ref[...]


def _ada(c_all, w_ada, b_ada):
    rows = c_all.shape[0]
    tn = 1024
    per = D_MODEL // tn
    b4 = b_ada.reshape(DEPTH, N_MOD * per, 1, tn)
    return pl.pallas_call(
        _ada_kernel,
        grid=(DEPTH, N_MOD * per),
        in_specs=[
            pl.BlockSpec((rows, D_MODEL), lambda l, j: (0, 0)),
            pl.BlockSpec((None, D_MODEL, tn), lambda l, j: (l, 0, j)),
            pl.BlockSpec((None, None, 1, tn), lambda l, j: (l, j, 0, 0)),
        ],
        out_specs=pl.BlockSpec((None, None, rows, tn), lambda l, j: (l, j // per, 0, j % per)),
        out_shape=jax.ShapeDtypeStruct((DEPTH, N_MOD, rows, D_MODEL), F32),
        compiler_params=_params("arbitrary", "arbitrary"),
        name="ada",
    )(c_all, w_ada, b4)


class _Mod:
    def __init__(self, arr, per_row, seq_len):
        self.arr = arr
        self.per_row = per_row
        self.seq_len = seq_len

    def spec(self, layer, m, tm):
        if self.per_row:
            return pl.BlockSpec((None, None, tm, D_MODEL), lambda i, *_: (layer, m, i, 0))
        per_seq = self.seq_len // tm
        return pl.BlockSpec((None, None, None, 1, D_MODEL), lambda i, *_: (layer, m, i // per_seq, 0, 0))


def _ffn_kernel(x_ref, sh_ref, sc_ref, g_ref, wg_ref, wu_ref, wd_ref, *rest, n_j, sub, final, convert):
    rest = list(rest)
    fw_ref = rest.pop(0) if final else None
    f32_tiles = [rest.pop(0) for _ in range(2)] if convert else []
    o_ref = rest.pop(0)
    bf16_tiles = [rest.pop(0) for _ in range(2)] if convert else []
    h_ref, = rest
    j = pl.program_id(1)
    tm = x_ref.shape[0]

    @pl.when(j == 0)
    def _():
        def body(s, carry):
            rows = pl.ds(pl.multiple_of(s * sub, sub), sub)
            sh = sh_ref[...] if sh_ref.shape[0] == 1 else sh_ref[rows, :]
            sc = sc_ref[...] if sc_ref.shape[0] == 1 else sc_ref[rows, :]
            h_ref[rows, :] = _norm_mod(x_ref[rows, :], sh, sc).astype(BF16)
            o_ref[rows, :] = jnp.zeros((sub, D_MODEL), F32)
            return carry
        lax.fori_loop(0, tm // sub, body, 0)

    for src, dst in zip(f32_tiles, bf16_tiles):
        dst[...] = src[...].astype(BF16)

    h = h_ref[...]
    gt = jnp.dot(h, wg_ref[...], preferred_element_type=F32)
    ut = jnp.dot(h, wu_ref[...], preferred_element_type=F32)
    act = (gt * _sigmoid(gt) * ut).astype(BF16)
    o_ref[...] += jnp.dot(act, wd_ref[...], preferred_element_type=F32)

    @pl.when(j == n_j - 1)
    def _():
        def body(s, carry):
            rows = pl.ds(pl.multiple_of(s * sub, sub), sub)
            g = g_ref[...] if g_ref.shape[0] == 1 else g_ref[rows, :]
            y = x_ref[rows, :] + 0.5 * g * o_ref[rows, :]
            if final:
                ms = jnp.mean(y * y, axis=-1, keepdims=True)
                y = y * lax.rsqrt(ms + NORM_EPS) * fw_ref[...]
            o_ref[rows, :] = y
            return carry
        lax.fori_loop(0, tm // sub, body, 0)


def _ffn(x, mod, layer, which, weights, tm, tf, final_w=None, convert=None):
    m_rows = x.shape[0]
    n_i = m_rows // tm
    n_j = D_FF // tf
    m0 = 6 * which
    sub = min(tm, 128)
    final = final_w is not None
    in_specs = [
        pl.BlockSpec((tm, D_MODEL), lambda i, j: (i, 0), pipeline_mode=pl.Buffered(1)),
        mod.spec(layer, m0, tm), mod.spec(layer, m0 + 1, tm), mod.spec(layer, m0 + 2, tm),
        pl.BlockSpec((D_MODEL, tf), lambda i, j: (0, j)),
        pl.BlockSpec((D_MODEL, tf), lambda i, j: (0, j + n_j)),
        pl.BlockSpec((tf, D_MODEL), lambda i, j: (j, 0)),
    ]
    args = [x, mod.arr, mod.arr, mod.arr, weights[0], weights[0], weights[1]]
    if final:
        in_specs.append(pl.BlockSpec((1, D_MODEL), lambda i, j: (0, 0)))
        args.append(final_w.reshape(1, D_MODEL))
    out_specs = [pl.BlockSpec((tm, D_MODEL), lambda i, j: (i, 0))]
    out_shape = [jax.ShapeDtypeStruct((m_rows, D_MODEL), F32)]
    if convert is not None:
        w_up32, w_down32, l2, s2 = convert
        up_tile = (D_MODEL // n_i, 2 * D_FF // n_j)
        down_tile = (D_FF // n_j, D_MODEL // n_i)
        assert up_tile[0] * n_i == D_MODEL and up_tile[1] * n_j == 2 * D_FF and up_tile[1] % LANES == 0
        assert down_tile[0] * n_j == D_FF and down_tile[1] * n_i == D_MODEL and down_tile[1] % LANES == 0
        in_specs += [pl.BlockSpec((None, None) + up_tile, lambda i, j: (l2, s2, i, j)),
                     pl.BlockSpec((None, None) + down_tile, lambda i, j: (l2, s2, j, i))]
        args += [w_up32, w_down32]
        out_specs += [pl.BlockSpec(up_tile, lambda i, j: (i, j)), pl.BlockSpec(down_tile, lambda i, j: (j, i))]
        out_shape += [jax.ShapeDtypeStruct((D_MODEL, 2 * D_FF), BF16), jax.ShapeDtypeStruct((D_FF, D_MODEL), BF16)]
    out = pl.pallas_call(
        functools.partial(_ffn_kernel, n_j=n_j, sub=sub, final=final, convert=convert is not None),
        grid=(n_i, n_j),
        in_specs=in_specs,
        out_specs=out_specs,
        out_shape=out_shape,
        scratch_shapes=[pltpu.VMEM((tm, D_MODEL), BF16)],
        compiler_params=_params("arbitrary", "arbitrary"),
        name="ffn",
    )(*args)
    return (out[0], (out[1], out[2])) if convert is not None else out[0]


def _inproj_kernel(x_ref, sh_ref, sc_ref, w_ref, o_ref, *rest, sub, emit, transposed):
    h_ref = rest[-1]
    j = pl.program_id(1)
    tm = x_ref.shape[0]

    @pl.when(j == 0)
    def _():
        def body(s, carry):
            rows = pl.ds(pl.multiple_of(s * sub, sub), sub)
            sh = sh_ref[...] if sh_ref.shape[0] == 1 else sh_ref[rows, :]
            sc = sc_ref[...] if sc_ref.shape[0] == 1 else sc_ref[rows, :]
            h_ref[rows, :] = _norm_mod(x_ref[rows, :], sh, sc).astype(BF16)
            return carry
        lax.fori_loop(0, tm // sub, body, 0)

    w = w_ref[...]
    if transposed:
        w = w.T
    if emit:
        w = w.astype(BF16)
        rest[0][...] = w
    o_ref[...] = jnp.dot(h_ref[...], w, preferred_element_type=F32)


def _inproj(x, mod, layer, w, widx, tm, tn, transposed=False):
    m_rows = x.shape[0]
    n_out = w.shape[-2] if transposed else w.shape[-1]
    sub = min(tm, 128)
    emit = widx is not None
    if transposed:
        w_spec = pl.BlockSpec((None, tn, D_MODEL), lambda i, j: (widx, j, 0))
    elif emit:
        w_spec = pl.BlockSpec((None, D_MODEL, tn), lambda i, j: (widx, 0, j))
    else:
        w_spec = pl.BlockSpec((D_MODEL, tn), lambda i, j: (0, j))
    out_specs = [pl.BlockSpec((tm, tn), lambda i, j: (i, j))]
    out_shape = [jax.ShapeDtypeStruct((m_rows, n_out), F32)]
    if emit:
        assert m_rows == tm
        out_specs.append(pl.BlockSpec((D_MODEL, tn), lambda i, j: (0, j)))
        out_shape.append(jax.ShapeDtypeStruct((D_MODEL, n_out), BF16))
    out = pl.pallas_call(
        functools.partial(_inproj_kernel, sub=sub, emit=emit, transposed=transposed),
        grid=(m_rows // tm, pl.cdiv(n_out, tn)),
        in_specs=[
            pl.BlockSpec((tm, D_MODEL), lambda i, j: (i, 0)),
            mod.spec(layer, 3, tm), mod.spec(layer, 4, tm),
            w_spec,
        ],
        out_specs=out_specs,
        out_shape=out_shape,
        scratch_shapes=[pltpu.VMEM((tm, D_MODEL), BF16)],
        compiler_params=_params("arbitrary", "arbitrary"),
        name="inproj",
    )(x, mod.arr, mod.arr, w)
    return tuple(out) if emit else out[0]


def _outproj_kernel(*refs, n_in, emit):
    y_refs = refs[:n_in]
    w_refs = refs[n_in:2 * n_in]
    x_ref, g_ref, o_ref = refs[2 * n_in:2 * n_in + 3]
    acc = None
    for k, (y_ref, w_ref) in enumerate(zip(y_refs, w_refs)):
        w = w_ref[...]
        if emit:
            w = w.astype(BF16)
            refs[2 * n_in + 3 + k][...] = w
        part = jnp.dot(y_ref[...].astype(BF16), w, preferred_element_type=F32)
        acc = part if acc is None else acc + part
    o_ref[...] = x_ref[...] + g_ref[...] * acc


def _outproj(ys, w, widx, x, mod, layer, tm):
    m_rows = x.shape[0]
    n_in = len(ys)
    kw = ys[0].shape[1]
    emit = widx is not None
    in_specs = [pl.BlockSpec((tm, kw), lambda i: (i, 0)) for _ in ys]
    if emit:
        assert m_rows == tm
        in_specs += [pl.BlockSpec((None, kw, D_MODEL), lambda i, k=k: (widx, k, 0)) for k in range(n_in)]
        w_args = [w] * n_in
    else:
        in_specs += [pl.BlockSpec((kw, D_MODEL), lambda i: (0, 0))] * n_in
        w_args = list(w)
    in_specs += [pl.BlockSpec((tm, D_MODEL), lambda i: (i, 0)), mod.spec(layer, 5, tm)]
    out_specs = [pl.BlockSpec((tm, D_MODEL), lambda i: (i, 0))]
    out_shape = [jax.ShapeDtypeStruct((m_rows, D_MODEL), F32)]
    if emit:
        out_specs += [pl.BlockSpec((kw, D_MODEL), lambda i: (0, 0))] * n_in
        out_shape += [jax.ShapeDtypeStruct((kw, D_MODEL), BF16)] * n_in
    out = pl.pallas_call(
        functools.partial(_outproj_kernel, n_in=n_in, emit=emit),
        grid=(m_rows // tm,),
        in_specs=in_specs,
        out_specs=out_specs,
        out_shape=out_shape,
        compiler_params=_params("arbitrary"),
        name="outproj",
    )(*ys, *w_args, x, mod.arr)
    return (out[0], tuple(out[1:])) if emit else out[0]


def _rwkv_prep_kernel(u_ref, prev_ref, mu_ref, wup_ref, aup_ref, gup_ref, vec_ref, *rest, seq_mode):
    if seq_mode:
        r_ref, lw_ref, k_ref, v_ref, kk_ref, a_ref, g_ref, carry_ref = rest
    else:
        seg_ref, r_ref, lw_ref, k_ref, v_ref, kk_ref, a_ref, g_ref, bonus_ref = rest
    u = u_ref[...]
    tm = u.shape[0]
    if seq_mode:
        @pl.when(pl.program_id(1) == 0)
        def _():
            carry_ref[...] = prev_ref[...]
        rolled = pltpu.roll(u, 1, axis=0)
        row = lax.broadcasted_iota(jnp.int32, u.shape, 0)
        prev = jnp.where(row == 0, carry_ref[...], rolled)
        carry_ref[...] = u[tm - 1:tm, :]
    else:
        prev = prev_ref[...]
    xs = u + mu_ref[...] * (prev - u)
    r = xs[:, 0:A_WIDTH]
    k = xs[:, A_WIDTH:2 * A_WIDTH]
    v = xs[:, 2 * A_WIDTH:3 * A_WIDTH]
    wa_lo = xs[:, 3 * A_WIDTH:3 * A_WIDTH + LANES]
    g_lo = xs[:, 3 * A_WIDTH + LANES:]
    w0, a0, k_k, k_a, r_k = (vec_ref[i:i + 1, :] for i in range(5))
    w_raw = -_softplus(-(w0 + _bdot(jnp.tanh(wa_lo), wup_ref[...]))) - 0.5
    lw = -jnp.exp(w_raw)
    a = _sigmoid(a0 + _bdot(wa_lo, aup_ref[...]))
    g = _bdot(_sigmoid(g_lo), gup_ref[...])
    kk = k * k_k
    k2 = k * (1.0 + (a - 1.0) * k_a)
    if not seq_mode:
        seg = seg_ref[...]
        kk = kk * lax.rsqrt(jnp.maximum(_dot_exact_lhs(kk * kk, seg), 1e-24))
        bonus_ref[...] = _dot_exact_lhs(r * k2 * r_k, seg) * v
    r_ref[...] = r
    lw_ref[...] = lw
    k_ref[...] = k2
    v_ref[...] = v
    kk_ref[...] = kk
    a_ref[...] = a
    g_ref[...] = g


def _rwkv_prep(u, prev, mu, wup_pad, aup_pad, g_up, vec, seg, n_seq, tm, seq_mode):
    rows = u.shape[0]
    per_seq = rows // n_seq // tm if seq_mode else 1
    grid = (n_seq, per_seq) if seq_mode else (rows // tm, 1)
    rowmap = (lambda b, t: (b * per_seq + t, 0)) if seq_mode else (lambda b, t: (b, 0))
    prev_spec = (pl.BlockSpec((None, 1, A_PROJ), lambda b, t: (b, 0, 0)) if seq_mode
                 else pl.BlockSpec((tm, A_PROJ), rowmap))
    const = lambda shape: pl.BlockSpec(shape, lambda b, t: (0,) * len(shape))
    out = jax.ShapeDtypeStruct((rows, A_WIDTH), F32)
    in_specs = [pl.BlockSpec((tm, A_PROJ), rowmap), prev_spec,
                const((1, A_PROJ)), const((LANES, A_WIDTH)), const((LANES, A_WIDTH)),
                const((A_GATE_RANK, A_WIDTH)), const((8, A_WIDTH))]
    args = [u, prev, mu, wup_pad, aup_pad, g_up, vec]
    if not seq_mode:
        in_specs.append(const((A_WIDTH, A_WIDTH)))
        args.append(seg)
    n_out = 7 if seq_mode else 8
    return pl.pallas_call(
        functools.partial(_rwkv_prep_kernel, seq_mode=seq_mode),
        grid=grid,
        in_specs=in_specs,
        out_specs=[pl.BlockSpec((tm, A_WIDTH), rowmap)] * n_out,
        out_shape=[out] * n_out,
        scratch_shapes=[pltpu.VMEM((1, A_PROJ), F32)] if seq_mode else [],
        compiler_params=_params("arbitrary", "arbitrary"),
        name="rwkv_prep",
    )(*args)


def _rwkv_post_kernel(y_ref, bonus_ref, g_ref, vec_ref, seg_ref, o_ref):
    y = y_ref[...]
    seg = seg_ref[...]
    inv_n = 1.0 / A_HEAD_DIM
    yc = y - _dot_exact_lhs(y, seg) * inv_n
    var = _dot_exact_lhs(yc * yc, seg) * inv_n
    yn = yc * lax.rsqrt(var + A_GN_EPS) * vec_ref[5:6, :] + vec_ref[6:7, :]
    o_ref[...] = (yn + bonus_ref[...]) * g_ref[...]


def _rwkv_post(y, bonus, g, vec, seg, tm):
    rows = y.shape[0]
    spec = pl.BlockSpec((tm, A_WIDTH), lambda i: (i, 0))
    return pl.pallas_call(
        _rwkv_post_kernel,
        grid=(rows // tm,),
        in_specs=[spec, spec, spec,
                  pl.BlockSpec((8, A_WIDTH), lambda i: (0, 0)),
                  pl.BlockSpec((A_WIDTH, A_WIDTH), lambda i: (0, 0))],
        out_specs=spec,
        out_shape=jax.ShapeDtypeStruct((rows, A_WIDTH), F32),
        compiler_params=_params("arbitrary"),
        name="rwkv_post",
    )(y, bonus, g, vec, seg)


def _rwkv_chunk_kernel(r_ref, lw_ref, k_ref, v_ref, kk_ref, a_ref, g_ref, vec_ref, h0_ref, y_ref, hT_ref, h_ref):
    ci = pl.program_id(1)

    @pl.when(ci == 0)
    def _():
        h_ref[...] = h0_ref[...]

    cs = CHUNK
    n2 = 2 * cs
    npair = A_WIDTH // LANES
    pairs = lambda x: jnp.stack([x[:, p * LANES:(p + 1) * LANES] for p in range(npair)])
    vec = vec_ref[...]
    vrow = lambda i: pairs(vec[i:i + 1, :])
    lane = lax.broadcasted_iota(jnp.int32, (1, 1, LANES), 2)
    m0 = (lane < A_HEAD_DIM).astype(F32)
    m1 = 1.0 - m0
    split = lambda x: jnp.concatenate([x * m0, x * m1], axis=1)
    dup = lambda x: jnp.concatenate([x, x], axis=1)
    own = jnp.concatenate([jnp.broadcast_to(m0, (1, cs, LANES)), jnp.broadcast_to(m1, (1, cs, LANES))], axis=1)
    bf = lambda x: x.astype(BF16)

    def bmm(x, y):
        return jnp.einsum('hab,hbc->hac', x, y, preferred_element_type=F32)

    def bmm_nt(x, y):
        return jnp.einsum('han,hbn->hab', x, y, preferred_element_type=F32)

    def bmm_tn(x, y):
        return jnp.einsum('hca,hcb->hab', x, y, preferred_element_type=F32)

    row = lax.broadcasted_iota(jnp.int32, (cs, cs), 0)
    col = lax.broadcasted_iota(jnp.int32, (cs, cs), 1)
    tri = (col <= row).astype(BF16)
    row2 = lax.broadcasted_iota(jnp.int32, (n2, n2), 0)
    col2 = lax.broadcasted_iota(jnp.int32, (n2, n2), 1)
    same = (row2 >= cs) == (col2 >= cs)
    tok_r = jnp.where(row2 >= cs, row2 - cs, row2)
    tok_c = jnp.where(col2 >= cs, col2 - cs, col2)
    strict = jnp.where(same, (tok_c < tok_r).astype(F32), 0.0)
    incl = jnp.where(same, (tok_c <= tok_r).astype(F32), 0.0)
    eye2 = (row2 == col2).astype(F32)
    nrow = lax.broadcasted_iota(jnp.int32, (LANES, LANES), 0)
    ncol = lax.broadcasted_iota(jnp.int32, (LANES, LANES), 1)
    eye_n = (nrow == ncol).astype(F32)

    h = h_ref[...]
    for sub in range(r_ref.shape[0] // cs):
        rows = slice(sub * cs, (sub + 1) * cs)
        lw_all = lw_ref[rows, :]
        c = pairs(_dot_exact_rhs(tri, lw_all))
        lw = pairs(lw_all)
        c_end = c[:, cs - 1:cs, :]
        e_pos = dup(jnp.exp(c))
        e_neg = dup(jnp.exp(-c))
        e_prev = dup(jnp.exp(c - lw))
        e_end = dup(jnp.exp(c_end - c))
        g_end = jnp.exp(c_end)

        kk = split(pairs(kk_ref[rows, :]))
        kk = kk * lax.rsqrt(jnp.maximum(jnp.sum(kk * kk, axis=-1, keepdims=True), 1e-24))
        b_vec = kk * dup(pairs(a_ref[rows, :]))
        r2 = split(pairs(r_ref[rows, :]))
        k2 = split(pairs(k_ref[rows, :]))
        v2 = split(pairs(v_ref[rows, :]))
        bonus = jnp.sum(r2 * k2 * vrow(4), axis=-1, keepdims=True) * v2
        rt = r2 * e_pos
        at = bf(-kk * e_prev)
        bt = bf(b_vec * e_neg)
        kt = bf(k2 * e_neg)
        bh = bf(b_vec * e_end)
        kh = bf(k2 * e_end)
        v2b = bf(v2)

        gram = bmm_nt(jnp.concatenate([at, bf(rt)], axis=1), jnp.concatenate([bt, kt], axis=1))
        a_ab = gram[:, :n2, :n2] * strict
        a_ak = gram[:, :n2, n2:] * strict
        m_b = gram[:, n2:, :n2] * incl
        m_k = gram[:, n2:, n2:] * incl

        tinv = eye2 + a_ab
        a_b = bf(a_ab)
        apow = bmm(a_b, a_b)
        span = 2
        while 2 * span < cs:
            ap_b = bf(apow)
            both = bmm(jnp.concatenate([bf(tinv), ap_b], axis=1), ap_b)
            tinv = tinv + both[:, :n2]
            apow = both[:, n2:]
            span *= 2
        tinv = tinv + bmm(bf(tinv), bf(apow))

        akv = bmm(bf(a_ak), v2b)
        pq = bf(bmm(bf(tinv), jnp.concatenate([at, bf(akv)], axis=-1)))
        mpq = bmm(bf(m_b), pq)
        p2 = rt + mpq[..., :LANES]
        y0 = mpq[..., LANES:] + bmm(bf(m_k), v2b)
        tpq = bmm_tn(bh, pq)
        gm = eye_n * g_end + tpq[..., :LANES]
        hadd = tpq[..., LANES:] + bmm_tn(kh, v2b)

        yh = bmm(jnp.concatenate([bf(p2), bf(gm)], axis=1), bf(h))
        y2 = yh[:, :n2] + y0
        h = yh[:, n2:] + hadd

        inv_n = 1.0 / A_HEAD_DIM
        yc = (y2 - jnp.sum(y2, axis=-1, keepdims=True) * inv_n) * own
        var = jnp.sum(yc * yc, axis=-1, keepdims=True) * inv_n
        z = yc * lax.rsqrt(var + A_GN_EPS) * vrow(5) + vrow(6) * own + bonus
        out = (z[:, :cs] + z[:, cs:]) * pairs(g_ref[rows, :])
        for p in range(npair):
            y_ref[rows, p * LANES:(p + 1) * LANES] = out[p]

    h_ref[...] = h

    @pl.when(ci == pl.num_programs(1) - 1)
    def _():
        hT_ref[...] = h


def _rwkv_chunk(r, lw, k, v, kk, a, g, vec, n_seq, h0):
    rows = r.shape[0]
    step_rows = RWKV_STEP_CHUNKS * CHUNK
    nc = rows // n_seq // step_rows
    npair = A_WIDTH // LANES
    spec = pl.BlockSpec((step_rows, A_WIDTH), lambda b, c: (b * nc + c, 0))
    hspec = pl.BlockSpec((None, npair, LANES, LANES), lambda b, c: (b, 0, 0, 0))
    return pl.pallas_call(
        _rwkv_chunk_kernel,
        grid=(n_seq, nc),
        in_specs=[spec] * 7 + [pl.BlockSpec((8, A_WIDTH), lambda b, c: (0, 0)), hspec],
        out_specs=[spec, hspec],
        out_shape=[jax.ShapeDtypeStruct((rows, A_WIDTH), F32),
                   jax.ShapeDtypeStruct((n_seq, npair, LANES, LANES), F32)],
        scratch_shapes=[pltpu.VMEM((npair, LANES, LANES), F32)],
        compiler_params=_params("arbitrary", "arbitrary"),
        name="rwkv_chunk",
    )(r, lw, k, v, kk, a, g, vec, h0)


def _pair_states(s):
    n = s.shape[0]
    nd = A_HEAD_DIM
    h = jnp.swapaxes(s, -1, -2).reshape(n, A_HEADS // 2, 2, nd, nd)
    out = jnp.zeros((n, A_HEADS // 2, 2 * nd, 2 * nd), s.dtype)
    return out.at[:, :, :nd, :nd].set(h[:, :, 0]).at[:, :, nd:, nd:].set(h[:, :, 1])


def _unpair_states(hp):
    n = hp.shape[0]
    nd = A_HEAD_DIM
    h = jnp.stack([hp[:, :, :nd, :nd], hp[:, :, nd:, nd:]], axis=2).reshape(n, A_HEADS, nd, nd)
    return jnp.swapaxes(h, -1, -2)


def _rwkv_step_kernel(r_ref, lw_ref, k_ref, v_ref, kk_ref, a_ref, s_ref, acc_ref, y_ref, so_ref,
                      r_t, w_t, k_t, v_t, nkk_t, b_t, y_t):
    h = pl.program_id(0)
    nd = A_HEAD_DIM

    @pl.when(h == 0)
    def _():
        kk = kk_ref[...]
        r_t[...] = r_ref[...].T
        w_t[...] = jnp.exp(lw_ref[...]).T
        k_t[...] = k_ref[...].T
        v_t[...] = v_ref[...].T
        nkk_t[...] = (-kk).T
        b_t[...] = (kk * a_ref[...]).T

    first = pl.multiple_of(h * nd, nd)
    rows = pl.ds(first, nd)
    r, w, k, nkk, b = r_t[rows, :], w_t[rows, :], k_t[rows, :], nkk_t[rows, :], b_t[rows, :]

    def body(i, carry):
        s = s_ref[i]
        s_kk = jnp.sum(s * nkk, axis=0, keepdims=True)
        s_new = s * w + s_kk * b + v_t[pl.ds(first + i, 1), :] * k
        so_ref[i] = s_new
        y_t[pl.ds(first + i, 1), :] = jnp.sum(s_new * r, axis=0, keepdims=True)
        return carry

    lax.fori_loop(0, nd, body, 0, unroll=4)

    @pl.when(h == pl.num_programs(0) - 1)
    def _():
        y_ref[...] = y_t[...].T


def _rwkv_step(r, lw, k, v, kk, a, states_t, j, new_states):
    n = r.shape[0]
    nd = A_HEAD_DIM
    rspec = pl.BlockSpec((n, A_WIDTH), lambda h: (0, 0))
    sspec = pl.BlockSpec((None, None, nd, nd, n), lambda h: (j, h, 0, 0, 0))
    return pl.pallas_call(
        _rwkv_step_kernel,
        grid=(A_HEADS,),
        in_specs=[rspec] * 6 + [sspec, pl.BlockSpec(memory_space=pl.ANY)],
        out_specs=[rspec, sspec],
        out_shape=[jax.ShapeDtypeStruct((n, A_WIDTH), F32), jax.ShapeDtypeStruct(states_t.shape, F32)],
        input_output_aliases={7: 1},
        scratch_shapes=[pltpu.VMEM((A_WIDTH, n), F32)] * 7,
        compiler_params=_params("arbitrary"),
        name="rwkv_step",
    )(r, lw, k, v, kk, a, states_t, new_states)


def _gla_log_alpha(a128, aup_ref, ab_ref):
    lane = lax.broadcasted_iota(jnp.int32, a128.shape, 1)
    a_lo = jnp.where(lane < B_ALPHA_RANK, a128, 0.0)
    x = _bdot(a_lo, aup_ref[...]) + ab_ref[...]
    return -_softplus(-x) * (1.0 / B_TAU)


def _gla_block(u_ref, aup_ref, ab_ref, nw_ref, o_ref, s_ref):
    ub = u_ref[...]
    cs = ub.shape[0]
    q_all = ub[:, 0:B_KEY_WIDTH] * (B_KEY_DIM ** -0.5)
    k_all = ub[:, B_KEY_WIDTH:2 * B_KEY_WIDTH]
    v_all = ub[:, 2 * B_KEY_WIDTH:2 * B_KEY_WIDTH + B_WIDTH]
    g_all = ub[:, 2 * B_KEY_WIDTH + B_WIDTH:2 * B_KEY_WIDTH + 2 * B_WIDTH]
    a128 = ub[:, 2 * B_KEY_WIDTH + 2 * B_WIDTH:2 * B_KEY_WIDTH + 2 * B_WIDTH + LANES]
    log_a = _gla_log_alpha(a128, aup_ref, ab_ref)

    row = lax.broadcasted_iota(jnp.int32, (cs, cs), 0)
    col = lax.broadcasted_iota(jnp.int32, (cs, cs), 1)
    cum = _dot_exact_rhs((col <= row).astype(BF16), log_a)
    e_pos = jnp.exp(cum)
    cum_end = cum[cs - 1:cs, :]
    e_end = jnp.exp(cum_end - cum)
    g_end = jnp.exp(cum_end)
    nk = B_KEY_DIM
    eye_k = (lax.broadcasted_iota(jnp.int32, (nk, nk), 0) == lax.broadcasted_iota(jnp.int32, (nk, nk), 1)).astype(F32)

    nblk = cs // GLA_SUB
    kw = B_KEY_WIDTH
    cum3 = cum.reshape(nblk, GLA_SUB, kw)
    before = jnp.concatenate([jnp.zeros((1, 1, kw), F32), cum3[:nblk - 1, GLA_SUB - 1:GLA_SUB, :]], axis=0)
    q_rel = q_all * jnp.exp(cum - jnp.broadcast_to(before, (nblk, GLA_SUB, kw)).reshape(cs, kw))
    tok = lax.broadcasted_iota(jnp.int32, (cs, 1), 0)
    q_parts, k_parts = [], []
    for blk in range(1, nblk):
        first = blk * GLA_SUB
        k_parts.append(jnp.where(tok < first, k_all * jnp.exp(jnp.minimum(before[blk] - cum, 0.0)), 0.0))
        q_parts.append(jnp.where((tok >= first) & (tok < first + GLA_SUB), q_rel, 0.0))

    q3 = q_all.reshape(nblk, GLA_SUB, kw)
    k3 = k_all.reshape(nblk, GLA_SUB, kw)
    v3 = v_all.reshape(nblk, GLA_SUB, B_WIDTH)
    sub_i = lax.broadcasted_iota(jnp.int32, (1, GLA_SUB, 1), 1)
    o_diag = [jnp.zeros((nblk, GLA_SUB, B_VAL_DIM), F32) for _ in range(B_HEADS)]
    for jj in range(GLA_SUB):
        pair = q3 * k3[:, jj:jj + 1, :] * jnp.exp(jnp.minimum(cum3 - cum3[:, jj:jj + 1, :], 0.0))
        pair = jnp.where(sub_i >= jj, pair, 0.0)
        for h in range(B_HEADS):
            score = jnp.sum(pair[:, :, h * nk:(h + 1) * nk], axis=-1, keepdims=True)
            o_diag[h] = o_diag[h] + score * v3[:, jj:jj + 1, h * B_VAL_DIM:(h + 1) * B_VAL_DIM]

    for h in range(B_HEADS):
        ks = slice(h * B_KEY_DIM, (h + 1) * B_KEY_DIM)
        vs = slice(h * B_VAL_DIM, (h + 1) * B_VAL_DIM)
        qe = q_all[:, ks] * e_pos[:, ks]
        ke = k_all[:, ks] * e_end[:, ks]
        vh = v_all[:, vs]
        s = s_ref[h]
        q_cat = jnp.concatenate([part[:, ks] for part in q_parts], axis=-1).astype(BF16)
        k_cat = jnp.concatenate([part[:, ks] for part in k_parts], axis=-1).astype(BF16)
        scores = jnp.einsum('id,jd->ij', q_cat, k_cat, preferred_element_type=F32)
        o = _bdot(scores, vh) + _bdot(qe, s) + o_diag[h].reshape(cs, B_VAL_DIM)
        g_col = jnp.sum(eye_k * g_end[:, ks], axis=-1, keepdims=True)
        s_ref[h] = s * g_col + jnp.einsum('jd,jv->dv', ke.astype(BF16), vh.astype(BF16),
                                           preferred_element_type=F32)
        ms = jnp.mean(o * o, axis=-1, keepdims=True)
        gh = g_all[:, vs]
        o_ref[:, vs] = o * lax.rsqrt(ms + NORM_EPS) * nw_ref[:, vs] * (gh * _sigmoid(gh))


def _gla_chunk_kernel(u_ref, aup_ref, ab_ref, nw_ref, s0_ref, o_ref, sT_ref, s_ref):
    ci = pl.program_id(1)

    @pl.when(ci == 0)
    def _():
        s_ref[...] = s0_ref[...]

    for sub in range(u_ref.shape[0] // CHUNK):
        rows = pl.ds(sub * CHUNK, CHUNK)
        _gla_block(u_ref.at[rows, :], aup_ref, ab_ref, nw_ref, o_ref.at[rows, :], s_ref)

    @pl.when(ci == pl.num_programs(1) - 1)
    def _():
        sT_ref[...] = s_ref[...]


def _gla_chunk(u, n_seq, alpha_up_pad, alpha_b, norm_w, s0):
    rows = u.shape[0]
    step_rows = GLA_STEP_CHUNKS * CHUNK
    nc = rows // n_seq // step_rows
    return pl.pallas_call(
        _gla_chunk_kernel,
        grid=(n_seq, nc),
        in_specs=[
            pl.BlockSpec((step_rows, A_PROJ), lambda b, c: (b * nc + c, 1)),
            pl.BlockSpec((LANES, B_KEY_WIDTH), lambda b, c: (0, 0)),
            pl.BlockSpec((1, B_KEY_WIDTH), lambda b, c: (0, 0)),
            pl.BlockSpec((1, B_WIDTH), lambda b, c: (0, 0)),
            pl.BlockSpec((None, B_HEADS, B_KEY_DIM, B_VAL_DIM), lambda b, c: (b, 0, 0, 0)),
        ],
        out_specs=[
            pl.BlockSpec((step_rows, B_WIDTH), lambda b, c: (b * nc + c, 0)),
            pl.BlockSpec((None, B_HEADS, B_KEY_DIM, B_VAL_DIM), lambda b, c: (b, 0, 0, 0)),
        ],
        out_shape=[jax.ShapeDtypeStruct((rows, B_WIDTH), F32),
                   jax.ShapeDtypeStruct((n_seq, B_HEADS, B_KEY_DIM, B_VAL_DIM), F32)],
        scratch_shapes=[pltpu.VMEM((B_HEADS, B_KEY_DIM, B_VAL_DIM), F32)],
        compiler_params=_params("arbitrary", "arbitrary"),
        name="gla_chunk",
    )(u, alpha_up_pad, alpha_b, norm_w, s0)


def _gla_step_kernel(u_ref, aup_ref, ab_ref, nw_ref, s_ref, acc_ref, o_ref, so_ref, la_ref):
    nb = u_ref.shape[0]
    nk = B_KEY_DIM
    off_k, off_v, off_g = B_KEY_WIDTH, 2 * B_KEY_WIDTH, 2 * B_KEY_WIDTH + B_WIDTH
    off_a = 2 * B_KEY_WIDTH + 2 * B_WIDTH
    la_ref[...] = _gla_log_alpha(u_ref[:, off_a:off_a + LANES], aup_ref, ab_ref)
    eye = (lax.broadcasted_iota(jnp.int32, (nk, nk), 0) == lax.broadcasted_iota(jnp.int32, (nk, nk), 1)).astype(F32)
    col = lambda x: jnp.sum(eye * x, axis=-1, keepdims=True)

    def body(n, carry):
        row = pl.ds(n, 1)
        ur = u_ref[row, :]
        dec_all = jnp.exp(la_ref[row, :])
        outs = []
        for h in range(B_HEADS):
            q = ur[:, h * nk:(h + 1) * nk] * (B_KEY_DIM ** -0.5)
            k = ur[:, off_k + h * nk:off_k + (h + 1) * nk]
            v = ur[:, off_v + h * B_VAL_DIM:off_v + (h + 1) * B_VAL_DIM]
            g = ur[:, off_g + h * B_VAL_DIM:off_g + (h + 1) * B_VAL_DIM]
            dec = dec_all[:, h * nk:(h + 1) * nk]
            s = s_ref[n, h]
            qk = jnp.sum(q * k, axis=-1, keepdims=True)
            o = qk * v + jnp.sum(col(q * dec) * s, axis=0, keepdims=True)
            so_ref[n, h] = s * col(dec) + col(k) * v
            ms = jnp.mean(o * o, axis=-1, keepdims=True)
            outs.append(o * lax.rsqrt(ms + NORM_EPS) * nw_ref[:, h * B_VAL_DIM:(h + 1) * B_VAL_DIM]
                        * (g * _sigmoid(g)))
        o_ref[row, :] = jnp.concatenate(outs, axis=-1)
        return carry

    lax.fori_loop(0, nb, body, 0)


def _gla_step(u, alpha_up_pad, alpha_b, norm_w, states, j, nb, new_states):
    n = u.shape[0]
    sspec = pl.BlockSpec((None, nb, B_HEADS, B_KEY_DIM, B_VAL_DIM), lambda i: (j, i, 0, 0, 0))
    return pl.pallas_call(
        _gla_step_kernel,
        grid=(n // nb,),
        in_specs=[pl.BlockSpec((nb, A_PROJ), lambda i: (i, 1)),
                  pl.BlockSpec((LANES, B_KEY_WIDTH), lambda i: (0, 0)),
                  pl.BlockSpec((1, B_KEY_WIDTH), lambda i: (0, 0)),
                  pl.BlockSpec((1, B_WIDTH), lambda i: (0, 0)),
                  sspec, pl.BlockSpec(memory_space=pl.ANY)],
        out_specs=[pl.BlockSpec((nb, B_WIDTH), lambda i: (i, 0)), sspec],
        out_shape=[jax.ShapeDtypeStruct((n, B_WIDTH), F32), jax.ShapeDtypeStruct(states.shape, F32)],
        input_output_aliases={5: 1},
        scratch_shapes=[pltpu.VMEM((nb, B_KEY_WIDTH), F32)],
        compiler_params=_params("arbitrary"),
        name="gla_step",
    )(u, alpha_up_pad, alpha_b, norm_w, states, new_states)


def _lru_gates(xc, wa_ref, wx_ref, ba_ref, bx_ref, lam_ref):
    ra, ix = [], []
    for nb in range(C_BLOCKS):
        xb = xc[:, nb * C_BLOCK:(nb + 1) * C_BLOCK].astype(BF16)
        ra.append(jnp.dot(xb, wa_ref[nb].astype(BF16), preferred_element_type=F32))
        ix.append(jnp.dot(xb, wx_ref[nb].astype(BF16), preferred_element_type=F32))
    r = _sigmoid(jnp.concatenate(ra, axis=-1) + ba_ref[...])
    i_g = _sigmoid(jnp.concatenate(ix, axis=-1) + bx_ref[...])
    log_a = -C_POW * r * _softplus(-lam_ref[...])
    a = jnp.exp(log_a)
    b = jnp.sqrt(1.0 - a * a) * (i_g * xc)
    return a, b


def _lru_seq_kernel(u_ref, cp_ref, h0_ref, cw_ref, cb_ref, wa_ref, wx_ref, ba_ref, bx_ref, lam_ref,
                    y_ref, tail_ref, hT_ref, carry_ref, h_ref, a_s, b_s):
    ti = pl.program_id(1)

    @pl.when(ti == 0)
    def _():
        carry_ref[...] = cp_ref[...]
        h_ref[...] = h0_ref[...]

    gate = u_ref[:, 0:C_WIDTH]
    xb = u_ref[:, C_WIDTH:2 * C_WIDTH]
    tm = xb.shape[0]
    carry = carry_ref[...]
    row8 = lax.broadcasted_iota(jnp.int32, (8, C_WIDTH), 0)
    xc = cb_ref[...] + cw_ref[C_CONV - 1:C_CONV, :] * xb
    for s in range(1, C_CONV):
        rolled = pltpu.roll(xb, s, axis=0)
        head = jnp.where(row8 < s, pltpu.roll(carry, s, axis=0), rolled[0:8, :])
        shifted = jnp.concatenate([head, rolled[8:, :]], axis=0)
        xc = xc + cw_ref[C_CONV - 1 - s:C_CONV - s, :] * shifted
    carry_ref[...] = xb[tm - 8:tm, :]
    tail_ref[...] = xb[tm - 8:tm, :]

    a, b = _lru_gates(xc, wa_ref, wx_ref, ba_ref, bx_ref, lam_ref)
    a_s[...] = a
    b_s[...] = b

    def body(t, h):
        h = a_s[pl.ds(t, 1), :] * h + b_s[pl.ds(t, 1), :]
        b_s[pl.ds(t, 1), :] = h
        return h

    h_last = lax.fori_loop(0, tm, body, h_ref[...], unroll=8)
    h_ref[...] = h_last
    hT_ref[...] = h_last
    y_ref[...] = _gelu_tanh(gate) * b_s[...]


def _lru_seq(u, n_seq, conv_prev8, h0, conv_w, conv_b, wa, wx, ba, bx, lam, tm):
    rows = u.shape[0]
    per_seq = rows // n_seq // tm
    const = lambda shape: pl.BlockSpec(shape, lambda b, t: (0,) * len(shape))
    return pl.pallas_call(
        _lru_seq_kernel,
        grid=(n_seq, per_seq),
        in_specs=[
            pl.BlockSpec((tm, 2 * C_WIDTH), lambda b, t: (b * per_seq + t, 0)),
            pl.BlockSpec((None, 8, C_WIDTH), lambda b, t: (b, 0, 0)),
            pl.BlockSpec((None, 1, C_WIDTH), lambda b, t: (b, 0, 0)),
            const((C_CONV, C_WIDTH)), const((1, C_WIDTH)),
            const((C_BLOCKS, C_BLOCK, C_BLOCK)), const((C_BLOCKS, C_BLOCK, C_BLOCK)),
            const((1, C_WIDTH)), const((1, C_WIDTH)), const((1, C_WIDTH)),
        ],
        out_specs=[
            pl.BlockSpec((tm, C_WIDTH), lambda b, t: (b * per_seq + t, 0)),
            pl.BlockSpec((None, 8, C_WIDTH), lambda b, t: (b, 0, 0)),
            pl.BlockSpec((None, 1, C_WIDTH), lambda b, t: (b, 0, 0)),
        ],
        out_shape=[jax.ShapeDtypeStruct((rows, C_WIDTH), F32),
                   jax.ShapeDtypeStruct((n_seq, 8, C_WIDTH), F32),
                   jax.ShapeDtypeStruct((n_seq, 1, C_WIDTH), F32)],
        scratch_shapes=[pltpu.VMEM((8, C_WIDTH), F32), pltpu.VMEM((1, C_WIDTH), F32),
                        pltpu.VMEM((tm, C_WIDTH), F32), pltpu.VMEM((tm, C_WIDTH), F32)],
        compiler_params=_params("arbitrary", "arbitrary"),
        name="lru_seq",
    )(u, conv_prev8, h0, conv_w, conv_b, wa, wx, ba, bx, lam)


def _lru_step_kernel(u_ref, cp_ref, h0_ref, cw_ref, cb_ref, wa_ref, wx_ref, ba_ref, bx_ref, lam_ref,
                     y_ref, cn_ref, h_ref):
    gate = u_ref[:, 0:C_WIDTH]
    xb = u_ref[:, C_WIDTH:2 * C_WIDTH]
    xc = cb_ref[...] + cw_ref[C_CONV - 1:C_CONV, :] * xb
    for i in range(C_CONV - 1):
        xc = xc + cw_ref[i:i + 1, :] * cp_ref[i]
    a, b = _lru_gates(xc, wa_ref, wx_ref, ba_ref, bx_ref, lam_ref)
    h = a * h0_ref[...] + b
    h_ref[...] = h
    y_ref[...] = _gelu_tanh(gate) * h
    for i in range(C_CONV - 2):
        cn_ref[i] = cp_ref[i + 1]
    cn_ref[C_CONV - 2] = xb


def _lru_step(u, conv_states, h0, j, conv_w, conv_b, wa, wx, ba, bx, lam):
    rows = u.shape[0]
    full = lambda shape: pl.BlockSpec(shape, lambda i: (0,) * len(shape))
    return pl.pallas_call(
        _lru_step_kernel,
        grid=(1,),
        in_specs=[full((rows, 2 * C_WIDTH)),
                  pl.BlockSpec((None, C_CONV - 1, rows, C_WIDTH), lambda i: (j, 0, 0, 0)),
                  full((rows, C_WIDTH)),
                  full((C_CONV, C_WIDTH)), full((1, C_WIDTH)),
                  full((C_BLOCKS, C_BLOCK, C_BLOCK)), full((C_BLOCKS, C_BLOCK, C_BLOCK)),
                  full((1, C_WIDTH)), full((1, C_WIDTH)), full((1, C_WIDTH))],
        out_specs=[full((rows, C_WIDTH)), full((C_CONV - 1, rows, C_WIDTH)), full((rows, C_WIDTH))],
        out_shape=[jax.ShapeDtypeStruct((rows, C_WIDTH), F32),
                   jax.ShapeDtypeStruct((C_CONV - 1, rows, C_WIDTH), F32),
                   jax.ShapeDtypeStruct((rows, C_WIDTH), F32)],
        compiler_params=_params("arbitrary"),
        name="lru_step",
    )(u, conv_states, h0, conv_w, conv_b, wa, wx, ba, bx, lam)


def _pad_rows(w, first, total):
    return jnp.zeros((total, w.shape[1]), w.dtype).at[first:first + w.shape[0]].set(w)


class _Group:
    def __init__(self, x, mod, n_seq, seq_len, states, tm, tf):
        self.x, self.mod, self.n_seq, self.seq_len, self.states, self.tm, self.tf = x, mod, n_seq, seq_len, states, tm, tf
        self.prompt = seq_len > 1
        self.out_shift, self.out_wkv, self.out_gla, self.out_conv, self.out_lru = [], [], [], [], []
        self.new_wkv_t = self.new_gla = None

    def new_states(self):
        wkv = jnp.stack(self.out_wkv) if self.new_wkv_t is None else jnp.transpose(self.new_wkv_t, (0, 4, 1, 2, 3))
        gla = jnp.stack(self.out_gla) if self.new_gla is None else self.new_gla
        return jnp.stack(self.out_shift), wkv, gla, jnp.stack(self.out_conv), jnp.stack(self.out_lru)


def _mixer(grp, l, wts, mxu):
    (w_ffn_up, w_ffn_down, w_in_even, w_out_even, a_mu, a_w_up, a_a_up, a_g_up, a_vec, b_alpha_up,
     b_alpha_b, b_norm_w, w_in_odd, w_out_odd, c_conv_w, c_conv_b, c_wa, c_ba, c_wx, c_bx, c_lam,
     final_norm_w) = wts
    x, mod, n_seq, seq_len, tm, prompt = grp.x, grp.mod, grp.n_seq, grp.seq_len, grp.tm, grp.prompt
    st_shift, st_wkv, st_gla, st_conv, st_lru = grp.states
    rows = x.shape[0]
    j = l // 2
    tn_even, tn_odd = EVEN_COL_TILE, ODD_COL_TILE
    seg = jnp.kron(jnp.eye(A_HEADS, dtype=F32), jnp.ones((A_HEAD_DIM, A_HEAD_DIM), F32)).astype(BF16)
    if l % 2 == 0:
        if prompt:
            u = _inproj(x, mod, l, mxu['in', l], None, tm, tn_even)
        else:
            u, mxu['in', l] = _inproj(x, mod, l, jnp.swapaxes(w_in_even, 1, 2), j, tm, tn_even,
                                      transposed=True)
        wup_pad = _pad_rows(a_w_up[j], 0, LANES)
        aup_pad = _pad_rows(a_a_up[j], A_DECAY_RANK, LANES)
        vec8 = _pad_rows(a_vec[j], 0, SUBLANES)
        alpha_pad = _pad_rows(b_alpha_up[j], 0, LANES)
        alpha_b = b_alpha_b[j].reshape(1, B_KEY_WIDTH)
        mu = a_mu[j].reshape(1, A_PROJ)
        norm_w = b_norm_w[j].reshape(1, B_WIDTH)
        if prompt:
            r, lw, k2, v, kk, al, g = _rwkv_prep(u, st_shift[j].reshape(n_seq, 1, A_PROJ), mu, wup_pad,
                                                 aup_pad, a_g_up[j], vec8, seg, n_seq, TOKENWISE_ROW_TILE, True)
            y_a, h_t = _rwkv_chunk(r, lw, k2, v, kk, al, g, vec8, n_seq, _pair_states(st_wkv[j]))
            s_wkv = _unpair_states(h_t)
            s_shift = u.reshape(n_seq, seq_len, EVEN_PROJ)[:, -1, :A_PROJ]
            y_b, s_gla = _gla_chunk(u, n_seq, alpha_pad, alpha_b, norm_w, st_gla[j])
        else:
            r, lw, k2, v, kk, al, g, bonus = _rwkv_prep(u, st_shift[j], mu, wup_pad, aup_pad, a_g_up[j],
                                                        vec8, seg, rows, rows, False)
            st_wkv_t = jnp.transpose(st_wkv, (0, 2, 3, 4, 1))
            y_raw, grp.new_wkv_t = _rwkv_step(r, lw, k2, v, kk, al, st_wkv_t, j,
                                              jnp.zeros_like(st_wkv_t) if grp.new_wkv_t is None else grp.new_wkv_t)
            y_a = _rwkv_post(y_raw, bonus, g, vec8, seg, rows)
            s_shift = u[:, :A_PROJ]
            y_b, grp.new_gla = _gla_step(u, alpha_pad, alpha_b, norm_w, st_gla, j, STEP_ROWS,
                                         jnp.zeros_like(st_gla) if grp.new_gla is None else grp.new_gla)
        if prompt:
            x = _outproj([y_a, y_b], mxu['out', l], None, x, mod, l, OUTPROJ_ROW_TILE)
        else:
            x, mxu['out', l] = _outproj([y_a, y_b], w_out_even, j, x, mod, l, tm)
        grp.out_shift.append(s_shift)
        if prompt:
            grp.out_wkv.append(s_wkv)
            grp.out_gla.append(s_gla)
    else:
        if prompt:
            u = _inproj(x, mod, l, mxu['in', l], None, tm, tn_odd)
        else:
            u, mxu['in', l] = _inproj(x, mod, l, w_in_odd, j, tm, tn_odd)
        lru_w = (c_conv_w[j], c_conv_b[j].reshape(1, C_WIDTH), c_wa[j], c_wx[j],
                 c_ba[j].reshape(1, C_WIDTH), c_bx[j].reshape(1, C_WIDTH), c_lam[j].reshape(1, C_WIDTH))
        if prompt:
            cp8 = jnp.concatenate([jnp.zeros((n_seq, 8 - (C_CONV - 1), C_WIDTH), F32), st_conv[j]], axis=1)
            y_c, tail, h_t = _lru_seq(u, n_seq, cp8, st_lru[j].reshape(n_seq, 1, C_WIDTH), *lru_w,
                                      TOKENWISE_ROW_TILE)
            s_conv = tail[:, 8 - (C_CONV - 1):, :]
            s_lru = h_t.reshape(n_seq, C_WIDTH)
        else:
            y_c, cn, s_lru = _lru_step(u, jnp.swapaxes(st_conv, 1, 2), st_lru[j], j, *lru_w)
            s_conv = jnp.swapaxes(cn, 0, 1)
        if prompt:
            x = _outproj([y_c], mxu['out', l], None, x, mod, l, OUTPROJ_ROW_TILE)
        else:
            x, mxu['out', l] = _outproj([y_c], w_out_odd, j, x, mod, l, tm)
        grp.out_conv.append(s_conv)
        grp.out_lru.append(s_lru)
    grp.x = x


def _trunk(sample, prompt, wts):
    w_ffn_up, w_ffn_down, final_norm_w = wts[0], wts[1], wts[-1]
    subs = [(l, which) for l in range(DEPTH) for which in (0, 1)]
    mxu = {('ffn', 0, 0): (w_ffn_up[0, 0].astype(BF16), w_ffn_down[0, 0].astype(BF16))}
    for idx, (l, which) in enumerate(subs):
        final_w = final_norm_w if idx == len(subs) - 1 else None
        weights = mxu['ffn', l, which]
        sample.x = _ffn(sample.x, sample.mod, l, which, weights, sample.tm, sample.tf, final_w=final_w)
        if idx + 1 < len(subs):
            nxt = subs[idx + 1]
            prompt.x, mxu[('ffn',) + nxt] = _ffn(prompt.x, prompt.mod, l, which, weights, prompt.tm, prompt.tf,
                                                 convert=(w_ffn_up, w_ffn_down) + nxt)
        else:
            prompt.x = _ffn(prompt.x, prompt.mod, l, which, weights, prompt.tm, prompt.tf, final_w=final_w)
        if which == 0:
            _mixer(sample, l, wts, mxu)
            _mixer(prompt, l, wts, mxu)


def kernel(x_prompt, x_sample, c_prompt, c_sample, state_rwkv_shift, state_rwkv_wkv, state_gla, state_conv, state_lru, w_ada, b_ada, w_ffn_up, w_ffn_down, w_in_even, w_out_even, a_mu, a_w_up, a_a_up, a_g_up, a_vec, b_alpha_up, b_alpha_b, b_norm_w, w_in_odd, w_out_odd, c_conv_w, c_conv_b, c_wa, c_ba, c_wx, c_bx, c_lam, final_norm_w):
    wts = (w_ffn_up, w_ffn_down, w_in_even, w_out_even, a_mu, a_w_up, a_a_up, a_g_up, a_vec, b_alpha_up,
           b_alpha_b, b_norm_w, w_in_odd, w_out_odd, c_conv_w, c_conv_b, c_wa, c_ba, c_wx, c_bx, c_lam,
           final_norm_w)
    n_p, t_p, _ = x_prompt.shape
    n_s, t_s, _ = x_sample.shape
    assert t_s == 1

    c_rows = n_p + n_s
    pad = (-c_rows) % 16
    c_all = jnp.concatenate([c_sample, c_prompt, jnp.zeros((pad, D_MODEL), F32)], axis=0)
    mod_all = _ada(c_all, w_ada, b_ada)
    mod_p = _Mod(mod_all[:, :, n_s:c_rows].reshape(DEPTH, N_MOD, n_p, 1, D_MODEL), False, t_p)
    mod_s = _Mod(mod_all, True, 1)

    fresh = lambda s: jnp.zeros((s.shape[0], n_p) + s.shape[2:], s.dtype)
    states = (state_rwkv_shift, state_rwkv_wkv, state_gla, state_conv, state_lru)
    sample = _Group(x_sample.reshape(n_s, D_MODEL), mod_s, n_s, 1, states, n_s, FFN_COL_TILE)
    prompt = _Group(x_prompt.reshape(n_p * t_p, D_MODEL), mod_p, n_p, t_p, tuple(fresh(s) for s in states),
                    PROMPT_ROW_TILE, FFN_COL_TILE)
    _trunk(sample, prompt, wts)
    return (prompt.x.reshape(n_p, t_p, D_MODEL), sample.x.reshape(n_s, 1, D_MODEL),
            *prompt.new_states(), *sample.new_states())
```

```python
import functools

import jax
import jax.numpy as jnp
from jax import lax
from jax.experimental import pallas as pl
from jax.experimental.pallas import tpu as pltpu

F32 = jnp.float32
BF16 = jnp.bfloat16

D_MODEL = 2048
DEPTH = 4
N_MOD = 9
D_FF = 5632
NORM_EPS = 1e-6

A_HEADS = 16
A_HEAD_DIM = 64
A_WIDTH = A_HEADS * A_HEAD_DIM
A_DECAY_RANK = 64
A_ICL_RANK = 64
A_GATE_RANK = 128
A_PROJ = 3 * A_WIDTH + A_DECAY_RANK + A_ICL_RANK + A_GATE_RANK
A_GN_EPS = 64e-5

B_HEADS = 4
B_KEY_DIM = 128
B_VAL_DIM = 256
B_KEY_WIDTH = B_HEADS * B_KEY_DIM
B_WIDTH = B_HEADS * B_VAL_DIM
B_ALPHA_RANK = 16
B_TAU = 16.0
B_PROJ = 2 * B_KEY_WIDTH + 2 * B_WIDTH + B_ALPHA_RANK
EVEN_PROJ = A_PROJ + B_PROJ

C_WIDTH = D_MODEL
C_BLOCKS = 8
C_BLOCK = C_WIDTH // C_BLOCKS
C_CONV = 4
C_POW = 8.0

LANES = 128
SUBLANES = 8
VMEM_LIMIT = 56 * 1024 * 1024

CHUNK = 64
RWKV_STEP_CHUNKS = 2
GLA_STEP_CHUNKS = 2
GLA_SUB = SUBLANES
PROMPT_ROW_TILE = 1024
OUTPROJ_ROW_TILE = 512
TOKENWISE_ROW_TILE = 256
FFN_COL_TILE = 512
EVEN_COL_TILE = 13 * LANES
ODD_COL_TILE = 8 * LANES
STEP_ROWS = SUBLANES
ADA_COL_TILE = 2048
NORM_ROWS = 128


def _params(*sem):
    return pltpu.CompilerParams(dimension_semantics=sem, vmem_limit_bytes=VMEM_LIMIT)


def _bdot(a, b):
    return jnp.dot(a.astype(BF16), b.astype(BF16), preferred_element_type=F32)


def _split3(x):
    hi = x.astype(BF16)
    r1 = x - hi.astype(F32)
    mid = r1.astype(BF16)
    lo = (r1 - mid.astype(F32)).astype(BF16)
    return hi, mid, lo


def _dot_exact_rhs(a_bf16, x):
    hi, mid, lo = _split3(x)
    f = lambda y: jnp.dot(a_bf16, y, preferred_element_type=F32)
    return f(hi) + f(mid) + f(lo)


def _dot_exact_lhs(x, b_bf16):
    hi, mid, lo = _split3(x)
    f = lambda y: jnp.dot(y, b_bf16, preferred_element_type=F32)
    return f(hi) + f(mid) + f(lo)


def _sigmoid(x):
    return jax.nn.sigmoid(x)


def _softplus(x):
    return jnp.maximum(x, 0.0) + jnp.log1p(jnp.exp(-jnp.abs(x)))


def _gelu_tanh(x):
    return 0.5 * x * (1.0 + jnp.tanh(0.7978845608028654 * (x + 0.044715 * (x * x * x))))


def _norm_mod(x, shift, scale):
    ms = jnp.mean(x * x, axis=-1, keepdims=True)
    return x * lax.rsqrt(ms + NORM_EPS) * (1.0 + scale) + shift


def _ada_kernel(c_ref, w_ref, b_ref, o_ref):
    c = c_ref[...]
    act = (c * _sigmoid(c)).astype(BF16)
    o_ref[...] = jnp.dot(act, w_ref[...].astype(BF16), preferred_element_type=F32) + b_ref[...]


def _ada(c_all, w_ada, b_ada):
    rows = c_all.shape[0]
    tn = ADA_COL_TILE
    per = D_MODEL // tn
    b4 = b_ada.reshape(DEPTH, N_MOD * per, 1, tn)
    return pl.pallas_call(
        _ada_kernel,
        grid=(DEPTH, N_MOD * per),
        in_specs=[
            pl.BlockSpec((rows, D_MODEL), lambda l, j: (0, 0)),
            pl.BlockSpec((None, D_MODEL, tn), lambda l, j: (l, 0, j)),
            pl.BlockSpec((None, None, 1, tn), lambda l, j: (l, j, 0, 0)),
        ],
        out_specs=pl.BlockSpec((None, None, rows, tn), lambda l, j: (l, j // per, 0, j % per)),
        out_shape=jax.ShapeDtypeStruct((DEPTH, N_MOD, rows, D_MODEL), F32),
        compiler_params=_params("arbitrary", "arbitrary"),
        name="ada",
    )(c_all, w_ada, b4)


class _Mod:
    def __init__(self, arr, per_row, seq_len):
        self.arr = arr
        self.per_row = per_row
        self.seq_len = seq_len

    def spec(self, layer, m, tm):
        if self.per_row:
            return pl.BlockSpec((None, None, tm, D_MODEL), lambda i, *_: (layer, m, i, 0))
        per_seq = self.seq_len // tm
        return pl.BlockSpec((None, None, None, 1, D_MODEL), lambda i, *_: (layer, m, i // per_seq, 0, 0))


def _ffn_kernel(x_ref, sh_ref, sc_ref, g_ref, wg_ref, wu_ref, wd_ref, *rest, n_j, sub, final, convert):
    rest = list(rest)
    fw_ref = rest.pop(0) if final else None
    f32_tiles = [rest.pop(0) for _ in range(2)] if convert else []
    o_ref = rest.pop(0)
    bf16_tiles = [rest.pop(0) for _ in range(2)] if convert else []
    h_ref, = rest
    j = pl.program_id(1)
    tm = x_ref.shape[0]

    @pl.when(j == 0)
    def _():
        def body(s, carry):
            rows = pl.ds(pl.multiple_of(s * sub, sub), sub)
            sh = sh_ref[...] if sh_ref.shape[0] == 1 else sh_ref[rows, :]
            sc = sc_ref[...] if sc_ref.shape[0] == 1 else sc_ref[rows, :]
            h_ref[rows, :] = _norm_mod(x_ref[rows, :], sh, sc).astype(BF16)
            o_ref[rows, :] = jnp.zeros((sub, D_MODEL), F32)
            return carry
        lax.fori_loop(0, tm // sub, body, 0)

    h = h_ref[...]
    gt = jnp.dot(h, wg_ref[...], preferred_element_type=F32)
    ut = jnp.dot(h, wu_ref[...], preferred_element_type=F32)
    act = (gt * _sigmoid(gt) * ut).astype(BF16)
    o_ref[...] += jnp.dot(act, wd_ref[...], preferred_element_type=F32)

    for src, dst in zip(f32_tiles, bf16_tiles):
        dst[...] = src[...].astype(BF16)

    @pl.when(j == n_j - 1)
    def _():
        def body(s, carry):
            rows = pl.ds(pl.multiple_of(s * sub, sub), sub)
            g = g_ref[...] if g_ref.shape[0] == 1 else g_ref[rows, :]
            y = x_ref[rows, :] + 0.5 * g * o_ref[rows, :]
            if final:
                ms = jnp.mean(y * y, axis=-1, keepdims=True)
                y = y * lax.rsqrt(ms + NORM_EPS) * fw_ref[...]
            o_ref[rows, :] = y
            return carry
        lax.fori_loop(0, tm // sub, body, 0)


def _ffn(x, mod, layer, which, weights, tm, tf, final_w=None, convert=None):
    m_rows = x.shape[0]
    n_i = m_rows // tm
    n_j = D_FF // tf
    m0 = 6 * which
    sub = min(tm, NORM_ROWS)
    final = final_w is not None
    in_specs = [
        pl.BlockSpec((tm, D_MODEL), lambda i, j: (i, 0), pipeline_mode=pl.Buffered(1)),
        mod.spec(layer, m0, tm), mod.spec(layer, m0 + 1, tm), mod.spec(layer, m0 + 2, tm),
        pl.BlockSpec((D_MODEL, tf), lambda i, j: (0, j)),
        pl.BlockSpec((D_MODEL, tf), lambda i, j: (0, j + n_j)),
        pl.BlockSpec((tf, D_MODEL), lambda i, j: (j, 0)),
    ]
    args = [x, mod.arr, mod.arr, mod.arr, weights[0], weights[0], weights[1]]
    if final:
        in_specs.append(pl.BlockSpec((1, D_MODEL), lambda i, j: (0, 0)))
        args.append(final_w.reshape(1, D_MODEL))
    out_specs = [pl.BlockSpec((tm, D_MODEL), lambda i, j: (i, 0))]
    out_shape = [jax.ShapeDtypeStruct((m_rows, D_MODEL), F32)]
    if convert is not None:
        w_up32, w_down32, l2, s2 = convert
        up_tile = (D_MODEL // n_i, 2 * D_FF // n_j)
        down_tile = (D_FF // n_j, D_MODEL // n_i)
        assert up_tile[0] * n_i == D_MODEL and up_tile[1] * n_j == 2 * D_FF and up_tile[1] % LANES == 0
        assert down_tile[0] * n_j == D_FF and down_tile[1] * n_i == D_MODEL and down_tile[1] % LANES == 0
        in_specs += [pl.BlockSpec((None, None) + up_tile, lambda i, j: (l2, s2, i, j)),
                     pl.BlockSpec((None, None) + down_tile, lambda i, j: (l2, s2, j, i))]
        args += [w_up32, w_down32]
        out_specs += [pl.BlockSpec(up_tile, lambda i, j: (i, j)), pl.BlockSpec(down_tile, lambda i, j: (j, i))]
        out_shape += [jax.ShapeDtypeStruct((D_MODEL, 2 * D_FF), BF16), jax.ShapeDtypeStruct((D_FF, D_MODEL), BF16)]
    out = pl.pallas_call(
        functools.partial(_ffn_kernel, n_j=n_j, sub=sub, final=final, convert=convert is not None),
        grid=(n_i, n_j),
        in_specs=in_specs,
        out_specs=out_specs,
        out_shape=out_shape,
        scratch_shapes=[pltpu.VMEM((tm, D_MODEL), BF16)],
        compiler_params=_params("arbitrary", "arbitrary"),
        name="ffn",
    )(*args)
    return (out[0], (out[1], out[2])) if convert is not None else out[0]


def _inproj_kernel(x_ref, sh_ref, sc_ref, w_ref, o_ref, *rest, sub, emit, transposed):
    h_ref = rest[-1]
    j = pl.program_id(1)
    tm = x_ref.shape[0]

    @pl.when(j == 0)
    def _():
        def body(s, carry):
            rows = pl.ds(pl.multiple_of(s * sub, sub), sub)
            sh = sh_ref[...] if sh_ref.shape[0] == 1 else sh_ref[rows, :]
            sc = sc_ref[...] if sc_ref.shape[0] == 1 else sc_ref[rows, :]
            h_ref[rows, :] = _norm_mod(x_ref[rows, :], sh, sc).astype(BF16)
            return carry
        lax.fori_loop(0, tm // sub, body, 0)

    w = w_ref[...]
    if transposed:
        w = w.T
    if emit:
        w = w.astype(BF16)
        rest[0][...] = w
    o_ref[...] = jnp.dot(h_ref[...], w, preferred_element_type=F32)


def _inproj(x, mod, layer, w, widx, tm, tn, transposed=False):
    m_rows = x.shape[0]
    n_out = w.shape[-2] if transposed else w.shape[-1]
    sub = min(tm, NORM_ROWS)
    emit = widx is not None
    if transposed:
        w_spec = pl.BlockSpec((None, tn, D_MODEL), lambda i, j: (widx, j, 0))
    elif emit:
        w_spec = pl.BlockSpec((None, D_MODEL, tn), lambda i, j: (widx, 0, j))
    else:
        w_spec = pl.BlockSpec((D_MODEL, tn), lambda i, j: (0, j))
    out_specs = [pl.BlockSpec((tm, tn), lambda i, j: (i, j))]
    out_shape = [jax.ShapeDtypeStruct((m_rows, n_out), F32)]
    if emit:
        assert m_rows == tm
        out_specs.append(pl.BlockSpec((D_MODEL, tn), lambda i, j: (0, j)))
        out_shape.append(jax.ShapeDtypeStruct((D_MODEL, n_out), BF16))
    out = pl.pallas_call(
        functools.partial(_inproj_kernel, sub=sub, emit=emit, transposed=transposed),
        grid=(m_rows // tm, pl.cdiv(n_out, tn)),
        in_specs=[
            pl.BlockSpec((tm, D_MODEL), lambda i, j: (i, 0)),
            mod.spec(layer, 3, tm), mod.spec(layer, 4, tm),
            w_spec,
        ],
        out_specs=out_specs,
        out_shape=out_shape,
        scratch_shapes=[pltpu.VMEM((tm, D_MODEL), BF16)],
        compiler_params=_params("arbitrary", "arbitrary"),
        name="inproj",
    )(x, mod.arr, mod.arr, w)
    return tuple(out) if emit else out[0]


def _outproj_kernel(*refs, n_in, emit):
    y_refs = refs[:n_in]
    w_refs = refs[n_in:2 * n_in]
    x_ref, g_ref, o_ref = refs[2 * n_in:2 * n_in + 3]
    acc = None
    for k, (y_ref, w_ref) in enumerate(zip(y_refs, w_refs)):
        w = w_ref[...]
        if emit:
            w = w.astype(BF16)
            refs[2 * n_in + 3 + k][...] = w
        part = jnp.dot(y_ref[...].astype(BF16), w, preferred_element_type=F32)
        acc = part if acc is None else acc + part
    o_ref[...] = x_ref[...] + g_ref[...] * acc


def _outproj(ys, w, widx, x, mod, layer, tm):
    m_rows = x.shape[0]
    n_in = len(ys)
    kw = ys[0].shape[1]
    emit = widx is not None
    in_specs = [pl.BlockSpec((tm, kw), lambda i: (i, 0)) for _ in ys]
    if emit:
        assert m_rows == tm
        in_specs += [pl.BlockSpec((None, kw, D_MODEL), lambda i, k=k: (widx, k, 0)) for k in range(n_in)]
        w_args = [w] * n_in
    else:
        in_specs += [pl.BlockSpec((kw, D_MODEL), lambda i: (0, 0))] * n_in
        w_args = list(w)
    in_specs += [pl.BlockSpec((tm, D_MODEL), lambda i: (i, 0)), mod.spec(layer, 5, tm)]
    out_specs = [pl.BlockSpec((tm, D_MODEL), lambda i: (i, 0))]
    out_shape = [jax.ShapeDtypeStruct((m_rows, D_MODEL), F32)]
    if emit:
        out_specs += [pl.BlockSpec((kw, D_MODEL), lambda i: (0, 0))] * n_in
        out_shape += [jax.ShapeDtypeStruct((kw, D_MODEL), BF16)] * n_in
    out = pl.pallas_call(
        functools.partial(_outproj_kernel, n_in=n_in, emit=emit),
        grid=(m_rows // tm,),
        in_specs=in_specs,
        out_specs=out_specs,
        out_shape=out_shape,
        compiler_params=_params("arbitrary"),
        name="outproj",
    )(*ys, *w_args, x, mod.arr)
    return (out[0], tuple(out[1:])) if emit else out[0]


def _rwkv_prep_kernel(u_ref, prev_ref, mu_ref, wup_ref, aup_ref, gup_ref, vec_ref, *rest, seq_mode):
    if seq_mode:
        r_ref, lw_ref, k_ref, v_ref, kk_ref, a_ref, g_ref, carry_ref = rest
    else:
        seg_ref, r_ref, lw_ref, k_ref, v_ref, kk_ref, a_ref, g_ref, bonus_ref = rest
    u = u_ref[...]
    tm = u.shape[0]
    if seq_mode:
        @pl.when(pl.program_id(1) == 0)
        def _():
            carry_ref[...] = prev_ref[...]
        rolled = pltpu.roll(u, 1, axis=0)
        row = lax.broadcasted_iota(jnp.int32, u.shape, 0)
        prev = jnp.where(row == 0, carry_ref[...], rolled)
        carry_ref[...] = u[tm - 1:tm, :]
    else:
        prev = prev_ref[...]
    xs = u + mu_ref[...] * (prev - u)
    r = xs[:, 0:A_WIDTH]
    k = xs[:, A_WIDTH:2 * A_WIDTH]
    v = xs[:, 2 * A_WIDTH:3 * A_WIDTH]
    wa_lo = xs[:, 3 * A_WIDTH:3 * A_WIDTH + LANES]
    g_lo = xs[:, 3 * A_WIDTH + LANES:]
    w0, a0, k_k, k_a, r_k = (vec_ref[i:i + 1, :] for i in range(5))
    w_raw = -_softplus(-(w0 + _bdot(jnp.tanh(wa_lo), wup_ref[...]))) - 0.5
    lw = -jnp.exp(w_raw)
    a = _sigmoid(a0 + _bdot(wa_lo, aup_ref[...]))
    g = _bdot(_sigmoid(g_lo), gup_ref[...])
    kk = k * k_k
    k2 = k * (1.0 + (a - 1.0) * k_a)
    if not seq_mode:
        seg = seg_ref[...]
        kk = kk * lax.rsqrt(jnp.maximum(_dot_exact_lhs(kk * kk, seg), 1e-24))
        bonus_ref[...] = _dot_exact_lhs(r * k2 * r_k, seg) * v
    r_ref[...] = r
    lw_ref[...] = lw
    k_ref[...] = k2
    v_ref[...] = v
    kk_ref[...] = kk
    a_ref[...] = a
    g_ref[...] = g


def _rwkv_prep(u, prev, mu, wup_pad, aup_pad, g_up, vec, seg, n_seq, tm, seq_mode):
    rows = u.shape[0]
    per_seq = rows // n_seq // tm if seq_mode else 1
    grid = (n_seq, per_seq) if seq_mode else (rows // tm, 1)
    rowmap = (lambda b, t: (b * per_seq + t, 0)) if seq_mode else (lambda b, t: (b, 0))
    prev_spec = (pl.BlockSpec((None, 1, A_PROJ), lambda b, t: (b, 0, 0)) if seq_mode
                 else pl.BlockSpec((tm, A_PROJ), rowmap))
    const = lambda shape: pl.BlockSpec(shape, lambda b, t: (0,) * len(shape))
    out = jax.ShapeDtypeStruct((rows, A_WIDTH), F32)
    in_specs = [pl.BlockSpec((tm, A_PROJ), rowmap), prev_spec,
                const((1, A_PROJ)), const((LANES, A_WIDTH)), const((LANES, A_WIDTH)),
                const((A_GATE_RANK, A_WIDTH)), const((SUBLANES, A_WIDTH))]
    args = [u, prev, mu, wup_pad, aup_pad, g_up, vec]
    if not seq_mode:
        in_specs.append(const((A_WIDTH, A_WIDTH)))
        args.append(seg)
    n_out = 7 if seq_mode else 8
    return pl.pallas_call(
        functools.partial(_rwkv_prep_kernel, seq_mode=seq_mode),
        grid=grid,
        in_specs=in_specs,
        out_specs=[pl.BlockSpec((tm, A_WIDTH), rowmap)] * n_out,
        out_shape=[out] * n_out,
        scratch_shapes=[pltpu.VMEM((1, A_PROJ), F32)] if seq_mode else [],
        compiler_params=_params("arbitrary", "arbitrary"),
        name="rwkv_prep",
    )(*args)


def _rwkv_post_kernel(y_ref, bonus_ref, g_ref, vec_ref, seg_ref, o_ref):
    y = y_ref[...]
    seg = seg_ref[...]
    inv_n = 1.0 / A_HEAD_DIM
    yc = y - _dot_exact_lhs(y, seg) * inv_n
    var = _dot_exact_lhs(yc * yc, seg) * inv_n
    yn = yc * lax.rsqrt(var + A_GN_EPS) * vec_ref[5:6, :] + vec_ref[6:7, :]
    o_ref[...] = (yn + bonus_ref[...]) * g_ref[...]


def _rwkv_post(y, bonus, g, vec, seg, tm):
    rows = y.shape[0]
    spec = pl.BlockSpec((tm, A_WIDTH), lambda i: (i, 0))
    return pl.pallas_call(
        _rwkv_post_kernel,
        grid=(rows // tm,),
        in_specs=[spec, spec, spec,
                  pl.BlockSpec((8, A_WIDTH), lambda i: (0, 0)),
                  pl.BlockSpec((A_WIDTH, A_WIDTH), lambda i: (0, 0))],
        out_specs=spec,
        out_shape=jax.ShapeDtypeStruct((rows, A_WIDTH), F32),
        compiler_params=_params("arbitrary"),
        name="rwkv_post",
    )(y, bonus, g, vec, seg)


def _rwkv_chunk_kernel(r_ref, lw_ref, k_ref, v_ref, kk_ref, a_ref, g_ref, vec_ref, h0_ref, y_ref, hT_ref, h_ref):
    ci = pl.program_id(1)

    @pl.when(ci == 0)
    def _():
        h_ref[...] = h0_ref[...]

    cs = CHUNK
    n2 = 2 * cs
    npair = A_WIDTH // LANES
    pairs = lambda x: jnp.stack([x[:, p * LANES:(p + 1) * LANES] for p in range(npair)])
    vec = vec_ref[...]
    vrow = lambda i: pairs(vec[i:i + 1, :])
    lane = lax.broadcasted_iota(jnp.int32, (1, 1, LANES), 2)
    m0 = (lane < A_HEAD_DIM).astype(F32)
    m1 = 1.0 - m0
    split = lambda x: jnp.concatenate([x * m0, x * m1], axis=1)
    dup = lambda x: jnp.concatenate([x, x], axis=1)
    own = jnp.concatenate([jnp.broadcast_to(m0, (1, cs, LANES)), jnp.broadcast_to(m1, (1, cs, LANES))], axis=1)
    bf = lambda x: x.astype(BF16)

    def bmm(x, y):
        return jnp.einsum('hab,hbc->hac', x, y, preferred_element_type=F32)

    def bmm_nt(x, y):
        return jnp.einsum('han,hbn->hab', x, y, preferred_element_type=F32)

    def bmm_tn(x, y):
        return jnp.einsum('hca,hcb->hab', x, y, preferred_element_type=F32)

    row = lax.broadcasted_iota(jnp.int32, (cs, cs), 0)
    col = lax.broadcasted_iota(jnp.int32, (cs, cs), 1)
    tri = (col <= row).astype(BF16)
    row2 = lax.broadcasted_iota(jnp.int32, (n2, n2), 0)
    col2 = lax.broadcasted_iota(jnp.int32, (n2, n2), 1)
    same = (row2 >= cs) == (col2 >= cs)
    tok_r = jnp.where(row2 >= cs, row2 - cs, row2)
    tok_c = jnp.where(col2 >= cs, col2 - cs, col2)
    strict = jnp.where(same, (tok_c < tok_r).astype(F32), 0.0)
    incl = jnp.where(same, (tok_c <= tok_r).astype(F32), 0.0)
    eye2 = (row2 == col2).astype(F32)
    nrow = lax.broadcasted_iota(jnp.int32, (LANES, LANES), 0)
    ncol = lax.broadcasted_iota(jnp.int32, (LANES, LANES), 1)
    eye_n = (nrow == ncol).astype(F32)

    h = h_ref[...]
    for sub in range(r_ref.shape[0] // cs):
        rows = slice(sub * cs, (sub + 1) * cs)
        lw_all = lw_ref[rows, :]
        c = pairs(_dot_exact_rhs(tri, lw_all))
        lw = pairs(lw_all)
        c_end = c[:, cs - 1:cs, :]
        e_pos = dup(jnp.exp(c))
        e_neg = dup(jnp.exp(-c))
        e_prev = dup(jnp.exp(c - lw))
        e_end = dup(jnp.exp(c_end - c))
        g_end = jnp.exp(c_end)

        kk = split(pairs(kk_ref[rows, :]))
        kk = kk * lax.rsqrt(jnp.maximum(jnp.sum(kk * kk, axis=-1, keepdims=True), 1e-24))
        b_vec = kk * dup(pairs(a_ref[rows, :]))
        r2 = split(pairs(r_ref[rows, :]))
        k2 = split(pairs(k_ref[rows, :]))
        v2 = split(pairs(v_ref[rows, :]))
        bonus = jnp.sum(r2 * k2 * vrow(4), axis=-1, keepdims=True) * v2
        rt = r2 * e_pos
        at = bf(-kk * e_prev)
        bt = bf(b_vec * e_neg)
        kt = bf(k2 * e_neg)
        bh = bf(b_vec * e_end)
        kh = bf(k2 * e_end)
        v2b = bf(v2)

        gram = bmm_nt(jnp.concatenate([at, bf(rt)], axis=1), jnp.concatenate([bt, kt], axis=1))
        a_ab = gram[:, :n2, :n2] * strict
        a_ak = gram[:, :n2, n2:] * strict
        m_b = gram[:, n2:, :n2] * incl
        m_k = gram[:, n2:, n2:] * incl

        tinv = eye2 + a_ab
        a_b = bf(a_ab)
        apow = bmm(a_b, a_b)
        span = 2
        while 2 * span < cs:
            ap_b = bf(apow)
            both = bmm(jnp.concatenate([bf(tinv), ap_b], axis=1), ap_b)
            tinv = tinv + both[:, :n2]
            apow = both[:, n2:]
            span *= 2
        tinv = tinv + bmm(bf(tinv), bf(apow))

        akv = bmm(bf(a_ak), v2b)
        pq = bf(bmm(bf(tinv), jnp.concatenate([at, bf(akv)], axis=-1)))
        mpq = bmm(bf(m_b), pq)
        p2 = rt + mpq[..., :LANES]
        y0 = mpq[..., LANES:] + bmm(bf(m_k), v2b)
        tpq = bmm_tn(bh, pq)
        gm = eye_n * g_end + tpq[..., :LANES]
        hadd = tpq[..., LANES:] + bmm_tn(kh, v2b)

        yh = bmm(jnp.concatenate([bf(p2), bf(gm)], axis=1), bf(h))
        y2 = yh[:, :n2] + y0
        h = yh[:, n2:] + hadd

        inv_n = 1.0 / A_HEAD_DIM
        yc = (y2 - jnp.sum(y2, axis=-1, keepdims=True) * inv_n) * own
        var = jnp.sum(yc * yc, axis=-1, keepdims=True) * inv_n
        z = yc * lax.rsqrt(var + A_GN_EPS) * vrow(5) + vrow(6) * own + bonus
        out = (z[:, :cs] + z[:, cs:]) * pairs(g_ref[rows, :])
        for p in range(npair):
            y_ref[rows, p * LANES:(p + 1) * LANES] = out[p]

    h_ref[...] = h

    @pl.when(ci == pl.num_programs(1) - 1)
    def _():
        hT_ref[...] = h


def _rwkv_chunk(r, lw, k, v, kk, a, g, vec, n_seq, h0):
    rows = r.shape[0]
    step_rows = RWKV_STEP_CHUNKS * CHUNK
    nc = rows // n_seq // step_rows
    npair = A_WIDTH // LANES
    spec = pl.BlockSpec((step_rows, A_WIDTH), lambda b, c: (b * nc + c, 0))
    hspec = pl.BlockSpec((None, npair, LANES, LANES), lambda b, c: (b, 0, 0, 0))
    return pl.pallas_call(
        _rwkv_chunk_kernel,
        grid=(n_seq, nc),
        in_specs=[spec] * 7 + [pl.BlockSpec((SUBLANES, A_WIDTH), lambda b, c: (0, 0)), hspec],
        out_specs=[spec, hspec],
        out_shape=[jax.ShapeDtypeStruct((rows, A_WIDTH), F32),
                   jax.ShapeDtypeStruct((n_seq, npair, LANES, LANES), F32)],
        scratch_shapes=[pltpu.VMEM((npair, LANES, LANES), F32)],
        compiler_params=_params("arbitrary", "arbitrary"),
        name="rwkv_chunk",
    )(r, lw, k, v, kk, a, g, vec, h0)


def _pair_states(s):
    n = s.shape[0]
    nd = A_HEAD_DIM
    h = jnp.swapaxes(s, -1, -2).reshape(n, A_HEADS // 2, 2, nd, nd)
    out = jnp.zeros((n, A_HEADS // 2, 2 * nd, 2 * nd), s.dtype)
    return out.at[:, :, :nd, :nd].set(h[:, :, 0]).at[:, :, nd:, nd:].set(h[:, :, 1])


def _unpair_states(hp):
    n = hp.shape[0]
    nd = A_HEAD_DIM
    h = jnp.stack([hp[:, :, :nd, :nd], hp[:, :, nd:, nd:]], axis=2).reshape(n, A_HEADS, nd, nd)
    return jnp.swapaxes(h, -1, -2)


def _rwkv_step_kernel(r_ref, lw_ref, k_ref, v_ref, kk_ref, a_ref, s_ref, acc_ref, y_ref, so_ref,
                      r_t, w_t, k_t, v_t, nkk_t, b_t, y_t):
    h = pl.program_id(0)
    nd = A_HEAD_DIM

    @pl.when(h == 0)
    def _():
        kk = kk_ref[...]
        r_t[...] = r_ref[...].T
        w_t[...] = jnp.exp(lw_ref[...]).T
        k_t[...] = k_ref[...].T
        v_t[...] = v_ref[...].T
        nkk_t[...] = (-kk).T
        b_t[...] = (kk * a_ref[...]).T

    first = pl.multiple_of(h * nd, nd)
    rows = pl.ds(first, nd)
    r, w, k, nkk, b = r_t[rows, :], w_t[rows, :], k_t[rows, :], nkk_t[rows, :], b_t[rows, :]

    def body(i, carry):
        s = s_ref[i]
        s_kk = jnp.sum(s * nkk, axis=0, keepdims=True)
        s_new = s * w + s_kk * b + v_t[pl.ds(first + i, 1), :] * k
        so_ref[i] = s_new
        y_t[pl.ds(first + i, 1), :] = jnp.sum(s_new * r, axis=0, keepdims=True)
        return carry

    lax.fori_loop(0, nd, body, 0, unroll=4)

    @pl.when(h == pl.num_programs(0) - 1)
    def _():
        y_ref[...] = y_t[...].T


def _rwkv_step(r, lw, k, v, kk, a, states_t, j, new_states):
    n = r.shape[0]
    nd = A_HEAD_DIM
    rspec = pl.BlockSpec((n, A_WIDTH), lambda h: (0, 0))
    sspec = pl.BlockSpec((None, None, nd, nd, n), lambda h: (j, h, 0, 0, 0))
    return pl.pallas_call(
        _rwkv_step_kernel,
        grid=(A_HEADS,),
        in_specs=[rspec] * 6 + [sspec, pl.BlockSpec(memory_space=pl.ANY)],
        out_specs=[rspec, sspec],
        out_shape=[jax.ShapeDtypeStruct((n, A_WIDTH), F32), jax.ShapeDtypeStruct(states_t.shape, F32)],
        input_output_aliases={7: 1},
        scratch_shapes=[pltpu.VMEM((A_WIDTH, n), F32)] * 7,
        compiler_params=_params("arbitrary"),
        name="rwkv_step",
    )(r, lw, k, v, kk, a, states_t, new_states)


def _gla_log_alpha(a128, aup_ref, ab_ref):
    lane = lax.broadcasted_iota(jnp.int32, a128.shape, 1)
    a_lo = jnp.where(lane < B_ALPHA_RANK, a128, 0.0)
    x = _bdot(a_lo, aup_ref[...]) + ab_ref[...]
    return -_softplus(-x) * (1.0 / B_TAU)


def _gla_block(u_ref, aup_ref, ab_ref, nw_ref, o_ref, s_ref):
    ub = u_ref[...]
    cs = ub.shape[0]
    q_all = ub[:, 0:B_KEY_WIDTH] * (B_KEY_DIM ** -0.5)
    k_all = ub[:, B_KEY_WIDTH:2 * B_KEY_WIDTH]
    v_all = ub[:, 2 * B_KEY_WIDTH:2 * B_KEY_WIDTH + B_WIDTH]
    g_all = ub[:, 2 * B_KEY_WIDTH + B_WIDTH:2 * B_KEY_WIDTH + 2 * B_WIDTH]
    a128 = ub[:, 2 * B_KEY_WIDTH + 2 * B_WIDTH:2 * B_KEY_WIDTH + 2 * B_WIDTH + LANES]
    log_a = _gla_log_alpha(a128, aup_ref, ab_ref)

    row = lax.broadcasted_iota(jnp.int32, (cs, cs), 0)
    col = lax.broadcasted_iota(jnp.int32, (cs, cs), 1)
    cum = _dot_exact_rhs((col <= row).astype(BF16), log_a)
    e_pos = jnp.exp(cum)
    cum_end = cum[cs - 1:cs, :]
    e_end = jnp.exp(cum_end - cum)
    g_end = jnp.exp(cum_end)
    nk = B_KEY_DIM
    eye_k = (lax.broadcasted_iota(jnp.int32, (nk, nk), 0) == lax.broadcasted_iota(jnp.int32, (nk, nk), 1)).astype(F32)

    nblk = cs // GLA_SUB
    kw = B_KEY_WIDTH
    cum3 = cum.reshape(nblk, GLA_SUB, kw)
    before = jnp.concatenate([jnp.zeros((1, 1, kw), F32), cum3[:nblk - 1, GLA_SUB - 1:GLA_SUB, :]], axis=0)
    q_rel = q_all * jnp.exp(cum - jnp.broadcast_to(before, (nblk, GLA_SUB, kw)).reshape(cs, kw))
    tok = lax.broadcasted_iota(jnp.int32, (cs, 1), 0)
    q_parts, k_parts = [], []
    for blk in range(1, nblk):
        first = blk * GLA_SUB
        k_parts.append(jnp.where(tok < first, k_all * jnp.exp(jnp.minimum(before[blk] - cum, 0.0)), 0.0))
        q_parts.append(jnp.where((tok >= first) & (tok < first + GLA_SUB), q_rel, 0.0))

    q3 = q_all.reshape(nblk, GLA_SUB, kw)
    k3 = k_all.reshape(nblk, GLA_SUB, kw)
    v3 = v_all.reshape(nblk, GLA_SUB, B_WIDTH)
    sub_i = lax.broadcasted_iota(jnp.int32, (1, GLA_SUB, 1), 1)
    o_diag = [jnp.zeros((nblk, GLA_SUB, B_VAL_DIM), F32) for _ in range(B_HEADS)]
    for jj in range(GLA_SUB):
        pair = q3 * k3[:, jj:jj + 1, :] * jnp.exp(jnp.minimum(cum3 - cum3[:, jj:jj + 1, :], 0.0))
        pair = jnp.where(sub_i >= jj, pair, 0.0)
        for h in range(B_HEADS):
            score = jnp.sum(pair[:, :, h * nk:(h + 1) * nk], axis=-1, keepdims=True)
            o_diag[h] = o_diag[h] + score * v3[:, jj:jj + 1, h * B_VAL_DIM:(h + 1) * B_VAL_DIM]

    for h in range(B_HEADS):
        ks = slice(h * B_KEY_DIM, (h + 1) * B_KEY_DIM)
        vs = slice(h * B_VAL_DIM, (h + 1) * B_VAL_DIM)
        qe = q_all[:, ks] * e_pos[:, ks]
        ke = k_all[:, ks] * e_end[:, ks]
        vh = v_all[:, vs]
        s = s_ref[h]
        q_cat = jnp.concatenate([part[:, ks] for part in q_parts], axis=-1).astype(BF16)
        k_cat = jnp.concatenate([part[:, ks] for part in k_parts], axis=-1).astype(BF16)
        scores = jnp.einsum('id,jd->ij', q_cat, k_cat, preferred_element_type=F32)
        o = _bdot(scores, vh) + _bdot(qe, s) + o_diag[h].reshape(cs, B_VAL_DIM)
        g_col = jnp.sum(eye_k * g_end[:, ks], axis=-1, keepdims=True)
        s_ref[h] = s * g_col + jnp.einsum('jd,jv->dv', ke.astype(BF16), vh.astype(BF16),
                                           preferred_element_type=F32)
        ms = jnp.mean(o * o, axis=-1, keepdims=True)
        gh = g_all[:, vs]
        o_ref[:, vs] = o * lax.rsqrt(ms + NORM_EPS) * nw_ref[:, vs] * (gh * _sigmoid(gh))


def _gla_chunk_kernel(u_ref, aup_ref, ab_ref, nw_ref, s0_ref, o_ref, sT_ref, s_ref):
    ci = pl.program_id(1)

    @pl.when(ci == 0)
    def _():
        s_ref[...] = s0_ref[...]

    for sub in range(u_ref.shape[0] // CHUNK):
        rows = pl.ds(sub * CHUNK, CHUNK)
        _gla_block(u_ref.at[rows, :], aup_ref, ab_ref, nw_ref, o_ref.at[rows, :], s_ref)

    @pl.when(ci == pl.num_programs(1) - 1)
    def _():
        sT_ref[...] = s_ref[...]


def _gla_chunk(u, n_seq, alpha_up_pad, alpha_b, norm_w, s0):
    rows = u.shape[0]
    step_rows = GLA_STEP_CHUNKS * CHUNK
    nc = rows // n_seq // step_rows
    return pl.pallas_call(
        _gla_chunk_kernel,
        grid=(n_seq, nc),
        in_specs=[
            pl.BlockSpec((step_rows, A_PROJ), lambda b, c: (b * nc + c, 1)),
            pl.BlockSpec((LANES, B_KEY_WIDTH), lambda b, c: (0, 0)),
            pl.BlockSpec((1, B_KEY_WIDTH), lambda b, c: (0, 0)),
            pl.BlockSpec((1, B_WIDTH), lambda b, c: (0, 0)),
            pl.BlockSpec((None, B_HEADS, B_KEY_DIM, B_VAL_DIM), lambda b, c: (b, 0, 0, 0)),
        ],
        out_specs=[
            pl.BlockSpec((step_rows, B_WIDTH), lambda b, c: (b * nc + c, 0)),
            pl.BlockSpec((None, B_HEADS, B_KEY_DIM, B_VAL_DIM), lambda b, c: (b, 0, 0, 0)),
        ],
        out_shape=[jax.ShapeDtypeStruct((rows, B_WIDTH), F32),
                   jax.ShapeDtypeStruct((n_seq, B_HEADS, B_KEY_DIM, B_VAL_DIM), F32)],
        scratch_shapes=[pltpu.VMEM((B_HEADS, B_KEY_DIM, B_VAL_DIM), F32)],
        compiler_params=_params("arbitrary", "arbitrary"),
        name="gla_chunk",
    )(u, alpha_up_pad, alpha_b, norm_w, s0)


def _gla_step_kernel(u_ref, aup_ref, ab_ref, nw_ref, s_ref, acc_ref, o_ref, so_ref, la_ref):
    nb = u_ref.shape[0]
    nk = B_KEY_DIM
    off_k, off_v, off_g = B_KEY_WIDTH, 2 * B_KEY_WIDTH, 2 * B_KEY_WIDTH + B_WIDTH
    off_a = 2 * B_KEY_WIDTH + 2 * B_WIDTH
    la_ref[...] = _gla_log_alpha(u_ref[:, off_a:off_a + LANES], aup_ref, ab_ref)
    eye = (lax.broadcasted_iota(jnp.int32, (nk, nk), 0) == lax.broadcasted_iota(jnp.int32, (nk, nk), 1)).astype(F32)
    col = lambda x: jnp.sum(eye * x, axis=-1, keepdims=True)

    def body(n, carry):
        row = pl.ds(n, 1)
        ur = u_ref[row, :]
        dec_all = jnp.exp(la_ref[row, :])
        outs = []
        for h in range(B_HEADS):
            q = ur[:, h * nk:(h + 1) * nk] * (B_KEY_DIM ** -0.5)
            k = ur[:, off_k + h * nk:off_k + (h + 1) * nk]
            v = ur[:, off_v + h * B_VAL_DIM:off_v + (h + 1) * B_VAL_DIM]
            g = ur[:, off_g + h * B_VAL_DIM:off_g + (h + 1) * B_VAL_DIM]
            dec = dec_all[:, h * nk:(h + 1) * nk]
            s = s_ref[n, h]
            qk = jnp.sum(q * k, axis=-1, keepdims=True)
            o = qk * v + jnp.sum(col(q * dec) * s, axis=0, keepdims=True)
            so_ref[n, h] = s * col(dec) + col(k) * v
            ms = jnp.mean(o * o, axis=-1, keepdims=True)
            outs.append(o * lax.rsqrt(ms + NORM_EPS) * nw_ref[:, h * B_VAL_DIM:(h + 1) * B_VAL_DIM]
                        * (g * _sigmoid(g)))
        o_ref[row, :] = jnp.concatenate(outs, axis=-1)
        return carry

    lax.fori_loop(0, nb, body, 0)


def _gla_step(u, alpha_up_pad, alpha_b, norm_w, states, j, nb, new_states):
    n = u.shape[0]
    sspec = pl.BlockSpec((None, nb, B_HEADS, B_KEY_DIM, B_VAL_DIM), lambda i: (j, i, 0, 0, 0))
    return pl.pallas_call(
        _gla_step_kernel,
        grid=(n // nb,),
        in_specs=[pl.BlockSpec((nb, A_PROJ), lambda i: (i, 1)),
                  pl.BlockSpec((LANES, B_KEY_WIDTH), lambda i: (0, 0)),
                  pl.BlockSpec((1, B_KEY_WIDTH), lambda i: (0, 0)),
                  pl.BlockSpec((1, B_WIDTH), lambda i: (0, 0)),
                  sspec, pl.BlockSpec(memory_space=pl.ANY)],
        out_specs=[pl.BlockSpec((nb, B_WIDTH), lambda i: (i, 0)), sspec],
        out_shape=[jax.ShapeDtypeStruct((n, B_WIDTH), F32), jax.ShapeDtypeStruct(states.shape, F32)],
        input_output_aliases={5: 1},
        scratch_shapes=[pltpu.VMEM((nb, B_KEY_WIDTH), F32)],
        compiler_params=_params("arbitrary"),
        name="gla_step",
    )(u, alpha_up_pad, alpha_b, norm_w, states, new_states)


def _lru_gates(xc, wa_ref, wx_ref, ba_ref, bx_ref, lam_ref):
    ra, ix = [], []
    for nb in range(C_BLOCKS):
        xb = xc[:, nb * C_BLOCK:(nb + 1) * C_BLOCK].astype(BF16)
        ra.append(jnp.dot(xb, wa_ref[nb].astype(BF16), preferred_element_type=F32))
        ix.append(jnp.dot(xb, wx_ref[nb].astype(BF16), preferred_element_type=F32))
    r = _sigmoid(jnp.concatenate(ra, axis=-1) + ba_ref[...])
    i_g = _sigmoid(jnp.concatenate(ix, axis=-1) + bx_ref[...])
    log_a = -C_POW * r * _softplus(-lam_ref[...])
    a = jnp.exp(log_a)
    b = jnp.sqrt(1.0 - a * a) * (i_g * xc)
    return a, b


def _lru_seq_kernel(u_ref, cp_ref, h0_ref, cw_ref, cb_ref, wa_ref, wx_ref, ba_ref, bx_ref, lam_ref,
                    y_ref, tail_ref, hT_ref, carry_ref, h_ref, a_s, b_s):
    ti = pl.program_id(1)

    @pl.when(ti == 0)
    def _():
        carry_ref[...] = cp_ref[...]
        h_ref[...] = h0_ref[...]

    gate = u_ref[:, 0:C_WIDTH]
    xb = u_ref[:, C_WIDTH:2 * C_WIDTH]
    tm = xb.shape[0]
    carry = carry_ref[...]
    row8 = lax.broadcasted_iota(jnp.int32, (8, C_WIDTH), 0)
    xc = cb_ref[...] + cw_ref[C_CONV - 1:C_CONV, :] * xb
    for s in range(1, C_CONV):
        rolled = pltpu.roll(xb, s, axis=0)
        head = jnp.where(row8 < s, pltpu.roll(carry, s, axis=0), rolled[0:8, :])
        shifted = jnp.concatenate([head, rolled[8:, :]], axis=0)
        xc = xc + cw_ref[C_CONV - 1 - s:C_CONV - s, :] * shifted
    carry_ref[...] = xb[tm - 8:tm, :]
    tail_ref[...] = xb[tm - 8:tm, :]

    a, b = _lru_gates(xc, wa_ref, wx_ref, ba_ref, bx_ref, lam_ref)
    a_s[...] = a
    b_s[...] = b

    def body(t, h):
        h = a_s[pl.ds(t, 1), :] * h + b_s[pl.ds(t, 1), :]
        b_s[pl.ds(t, 1), :] = h
        return h

    h_last = lax.fori_loop(0, tm, body, h_ref[...], unroll=8)
    h_ref[...] = h_last
    hT_ref[...] = h_last
    y_ref[...] = _gelu_tanh(gate) * b_s[...]


def _lru_seq(u, n_seq, conv_prev8, h0, conv_w, conv_b, wa, wx, ba, bx, lam, tm):
    rows = u.shape[0]
    per_seq = rows // n_seq // tm
    const = lambda shape: pl.BlockSpec(shape, lambda b, t: (0,) * len(shape))
    return pl.pallas_call(
        _lru_seq_kernel,
        grid=(n_seq, per_seq),
        in_specs=[
            pl.BlockSpec((tm, 2 * C_WIDTH), lambda b, t: (b * per_seq + t, 0)),
            pl.BlockSpec((None, 8, C_WIDTH), lambda b, t: (b, 0, 0)),
            pl.BlockSpec((None, 1, C_WIDTH), lambda b, t: (b, 0, 0)),
            const((C_CONV, C_WIDTH)), const((1, C_WIDTH)),
            const((C_BLOCKS, C_BLOCK, C_BLOCK)), const((C_BLOCKS, C_BLOCK, C_BLOCK)),
            const((1, C_WIDTH)), const((1, C_WIDTH)), const((1, C_WIDTH)),
        ],
        out_specs=[
            pl.BlockSpec((tm, C_WIDTH), lambda b, t: (b * per_seq + t, 0)),
            pl.BlockSpec((None, 8, C_WIDTH), lambda b, t: (b, 0, 0)),
            pl.BlockSpec((None, 1, C_WIDTH), lambda b, t: (b, 0, 0)),
        ],
        out_shape=[jax.ShapeDtypeStruct((rows, C_WIDTH), F32),
                   jax.ShapeDtypeStruct((n_seq, 8, C_WIDTH), F32),
                   jax.ShapeDtypeStruct((n_seq, 1, C_WIDTH), F32)],
        scratch_shapes=[pltpu.VMEM((8, C_WIDTH), F32), pltpu.VMEM((1, C_WIDTH), F32),
                        pltpu.VMEM((tm, C_WIDTH), F32), pltpu.VMEM((tm, C_WIDTH), F32)],
        compiler_params=_params("arbitrary", "arbitrary"),
        name="lru_seq",
    )(u, conv_prev8, h0, conv_w, conv_b, wa, wx, ba, bx, lam)


def _lru_step_kernel(u_ref, cp_ref, h0_ref, cw_ref, cb_ref, wa_ref, wx_ref, ba_ref, bx_ref, lam_ref,
                     y_ref, cn_ref, h_ref):
    gate = u_ref[:, 0:C_WIDTH]
    xb = u_ref[:, C_WIDTH:2 * C_WIDTH]
    xc = cb_ref[...] + cw_ref[C_CONV - 1:C_CONV, :] * xb
    for i in range(C_CONV - 1):
        xc = xc + cw_ref[i:i + 1, :] * cp_ref[i]
    a, b = _lru_gates(xc, wa_ref, wx_ref, ba_ref, bx_ref, lam_ref)
    h = a * h0_ref[...] + b
    h_ref[...] = h
    y_ref[...] = _gelu_tanh(gate) * h
    for i in range(C_CONV - 2):
        cn_ref[i] = cp_ref[i + 1]
    cn_ref[C_CONV - 2] = xb


def _lru_step(u, conv_states, h0, j, conv_w, conv_b, wa, wx, ba, bx, lam):
    rows = u.shape[0]
    full = lambda shape: pl.BlockSpec(shape, lambda i: (0,) * len(shape))
    return pl.pallas_call(
        _lru_step_kernel,
        grid=(1,),
        in_specs=[full((rows, 2 * C_WIDTH)),
                  pl.BlockSpec((None, C_CONV - 1, rows, C_WIDTH), lambda i: (j, 0, 0, 0)),
                  full((rows, C_WIDTH)),
                  full((C_CONV, C_WIDTH)), full((1, C_WIDTH)),
                  full((C_BLOCKS, C_BLOCK, C_BLOCK)), full((C_BLOCKS, C_BLOCK, C_BLOCK)),
                  full((1, C_WIDTH)), full((1, C_WIDTH)), full((1, C_WIDTH))],
        out_specs=[full((rows, C_WIDTH)), full((C_CONV - 1, rows, C_WIDTH)), full((rows, C_WIDTH))],
        out_shape=[jax.ShapeDtypeStruct((rows, C_WIDTH), F32),
                   jax.ShapeDtypeStruct((C_CONV - 1, rows, C_WIDTH), F32),
                   jax.ShapeDtypeStruct((rows, C_WIDTH), F32)],
        compiler_params=_params("arbitrary"),
        name="lru_step",
    )(u, conv_states, h0, conv_w, conv_b, wa, wx, ba, bx, lam)


def _pad_rows(w, first, total):
    return jnp.zeros((total, w.shape[1]), w.dtype).at[first:first + w.shape[0]].set(w)


class _Group:
    def __init__(self, x, mod, n_seq, seq_len, states, tm, tf):
        self.x, self.mod, self.n_seq, self.seq_len, self.states, self.tm, self.tf = x, mod, n_seq, seq_len, states, tm, tf
        self.prompt = seq_len > 1
        self.out_shift, self.out_wkv, self.out_gla, self.out_conv, self.out_lru = [], [], [], [], []
        self.new_wkv_t = self.new_gla = None

    def new_states(self):
        wkv = jnp.stack(self.out_wkv) if self.new_wkv_t is None else jnp.transpose(self.new_wkv_t, (0, 4, 1, 2, 3))
        gla = jnp.stack(self.out_gla) if self.new_gla is None else self.new_gla
        return jnp.stack(self.out_shift), wkv, gla, jnp.stack(self.out_conv), jnp.stack(self.out_lru)


def _mixer(grp, l, wts, mxu):
    (w_ffn_up, w_ffn_down, w_in_even, w_out_even, a_mu, a_w_up, a_a_up, a_g_up, a_vec, b_alpha_up,
     b_alpha_b, b_norm_w, w_in_odd, w_out_odd, c_conv_w, c_conv_b, c_wa, c_ba, c_wx, c_bx, c_lam,
     final_norm_w) = wts
    x, mod, n_seq, seq_len, tm, prompt = grp.x, grp.mod, grp.n_seq, grp.seq_len, grp.tm, grp.prompt
    st_shift, st_wkv, st_gla, st_conv, st_lru = grp.states
    rows = x.shape[0]
    j = l // 2
    tn_even, tn_odd = EVEN_COL_TILE, ODD_COL_TILE
    seg = jnp.kron(jnp.eye(A_HEADS, dtype=F32), jnp.ones((A_HEAD_DIM, A_HEAD_DIM), F32)).astype(BF16)
    if l % 2 == 0:
        if prompt:
            u = _inproj(x, mod, l, mxu['in', l], None, tm, tn_even)
        else:
            u, mxu['in', l] = _inproj(x, mod, l, jnp.swapaxes(w_in_even, 1, 2), j, tm, tn_even,
                                      transposed=True)
        wup_pad = _pad_rows(a_w_up[j], 0, LANES)
        aup_pad = _pad_rows(a_a_up[j], A_DECAY_RANK, LANES)
        vec8 = _pad_rows(a_vec[j], 0, SUBLANES)
        alpha_pad = _pad_rows(b_alpha_up[j], 0, LANES)
        alpha_b = b_alpha_b[j].reshape(1, B_KEY_WIDTH)
        mu = a_mu[j].reshape(1, A_PROJ)
        norm_w = b_norm_w[j].reshape(1, B_WIDTH)
        if prompt:
            r, lw, k2, v, kk, al, g = _rwkv_prep(u, st_shift[j].reshape(n_seq, 1, A_PROJ), mu, wup_pad,
                                                 aup_pad, a_g_up[j], vec8, seg, n_seq, TOKENWISE_ROW_TILE, True)
            y_a, h_t = _rwkv_chunk(r, lw, k2, v, kk, al, g, vec8, n_seq, _pair_states(st_wkv[j]))
            s_wkv = _unpair_states(h_t)
            s_shift = u.reshape(n_seq, seq_len, EVEN_PROJ)[:, -1, :A_PROJ]
            y_b, s_gla = _gla_chunk(u, n_seq, alpha_pad, alpha_b, norm_w, st_gla[j])
        else:
            r, lw, k2, v, kk, al, g, bonus = _rwkv_prep(u, st_shift[j], mu, wup_pad, aup_pad, a_g_up[j],
                                                        vec8, seg, rows, rows, False)
            st_wkv_t = jnp.transpose(st_wkv, (0, 2, 3, 4, 1))
            y_raw, grp.new_wkv_t = _rwkv_step(r, lw, k2, v, kk, al, st_wkv_t, j,
                                              jnp.zeros_like(st_wkv_t) if grp.new_wkv_t is None else grp.new_wkv_t)
            y_a = _rwkv_post(y_raw, bonus, g, vec8, seg, rows)
            s_shift = u[:, :A_PROJ]
            y_b, grp.new_gla = _gla_step(u, alpha_pad, alpha_b, norm_w, st_gla, j, STEP_ROWS,
                                         jnp.zeros_like(st_gla) if grp.new_gla is None else grp.new_gla)
        if prompt:
            x = _outproj([y_a, y_b], mxu['out', l], None, x, mod, l, OUTPROJ_ROW_TILE)
        else:
            x, mxu['out', l] = _outproj([y_a, y_b], w_out_even, j, x, mod, l, tm)
        grp.out_shift.append(s_shift)
        if prompt:
            grp.out_wkv.append(s_wkv)
            grp.out_gla.append(s_gla)
    else:
        if prompt:
            u = _inproj(x, mod, l, mxu['in', l], None, tm, tn_odd)
        else:
            u, mxu['in', l] = _inproj(x, mod, l, w_in_odd, j, tm, tn_odd)
        lru_w = (c_conv_w[j], c_conv_b[j].reshape(1, C_WIDTH), c_wa[j], c_wx[j],
                 c_ba[j].reshape(1, C_WIDTH), c_bx[j].reshape(1, C_WIDTH), c_lam[j].reshape(1, C_WIDTH))
        if prompt:
            cp8 = jnp.concatenate([jnp.zeros((n_seq, 8 - (C_CONV - 1), C_WIDTH), F32), st_conv[j]], axis=1)
            y_c, tail, h_t = _lru_seq(u, n_seq, cp8, st_lru[j].reshape(n_seq, 1, C_WIDTH), *lru_w,
                                      TOKENWISE_ROW_TILE)
            s_conv = tail[:, 8 - (C_CONV - 1):, :]
            s_lru = h_t.reshape(n_seq, C_WIDTH)
        else:
            y_c, cn, s_lru = _lru_step(u, jnp.swapaxes(st_conv, 1, 2), st_lru[j], j, *lru_w)
            s_conv = jnp.swapaxes(cn, 0, 1)
        if prompt:
            x = _outproj([y_c], mxu['out', l], None, x, mod, l, OUTPROJ_ROW_TILE)
        else:
            x, mxu['out', l] = _outproj([y_c], w_out_odd, j, x, mod, l, tm)
        grp.out_conv.append(s_conv)
        grp.out_lru.append(s_lru)
    grp.x = x


def _trunk(sample, prompt, wts):
    w_ffn_up, w_ffn_down, final_norm_w = wts[0], wts[1], wts[-1]
    subs = [(l, which) for l in range(DEPTH) for which in (0, 1)]
    mxu = {('ffn', 0, 0): (w_ffn_up[0, 0].astype(BF16), w_ffn_down[0, 0].astype(BF16))}
    for idx, (l, which) in enumerate(subs):
        final_w = final_norm_w if idx == len(subs) - 1 else None
        weights = mxu['ffn', l, which]
        sample.x = _ffn(sample.x, sample.mod, l, which, weights, sample.tm, sample.tf, final_w=final_w)
        if idx + 1 < len(subs):
            nxt = subs[idx + 1]
            prompt.x, mxu[('ffn',) + nxt] = _ffn(prompt.x, prompt.mod, l, which, weights, prompt.tm, prompt.tf,
                                                 convert=(w_ffn_up, w_ffn_down) + nxt)
        else:
            prompt.x = _ffn(prompt.x, prompt.mod, l, which, weights, prompt.tm, prompt.tf, final_w=final_w)
        if which == 0:
            _mixer(sample, l, wts, mxu)
            _mixer(prompt, l, wts, mxu)


def kernel(x_prompt, x_sample, c_prompt, c_sample, state_rwkv_shift, state_rwkv_wkv, state_gla, state_conv, state_lru, w_ada, b_ada, w_ffn_up, w_ffn_down, w_in_even, w_out_even, a_mu, a_w_up, a_a_up, a_g_up, a_vec, b_alpha_up, b_alpha_b, b_norm_w, w_in_odd, w_out_odd, c_conv_w, c_conv_b, c_wa, c_ba, c_wx, c_bx, c_lam, final_norm_w):
    wts = (w_ffn_up, w_ffn_down, w_in_even, w_out_even, a_mu, a_w_up, a_a_up, a_g_up, a_vec, b_alpha_up,
           b_alpha_b, b_norm_w, w_in_odd, w_out_odd, c_conv_w, c_conv_b, c_wa, c_ba, c_wx, c_bx, c_lam,
           final_norm_w)
    n_p, t_p, _ = x_prompt.shape
    n_s, t_s, _ = x_sample.shape
    assert t_s == 1

    c_rows = n_p + n_s
    pad = (-c_rows) % 16
    c_all = jnp.concatenate([c_sample, c_prompt, jnp.zeros((pad, D_MODEL), F32)], axis=0)
    mod_all = _ada(c_all, w_ada, b_ada)
    mod_p = _Mod(mod_all[:, :, n_s:c_rows].reshape(DEPTH, N_MOD, n_p, 1, D_MODEL), False, t_p)
    mod_s = _Mod(mod_all, True, 1)

    fresh = lambda s: jnp.zeros((s.shape[0], n_p) + s.shape[2:], s.dtype)
    states = (state_rwkv_shift, state_rwkv_wkv, state_gla, state_conv, state_lru)
    sample = _Group(x_sample.reshape(n_s, D_MODEL), mod_s, n_s, 1, states, n_s, FFN_COL_TILE)
    prompt = _Group(x_prompt.reshape(n_p * t_p, D_MODEL), mod_p, n_p, t_p, tuple(fresh(s) for s in states),
                    PROMPT_ROW_TILE, FFN_COL_TILE)
    _trunk(sample, prompt, wts)
    return (prompt.x.reshape(n_p, t_p, D_MODEL), sample.x.reshape(n_s, 1, D_MODEL),
            *prompt.new_states(), *sample.new_states())
```

```python
import functools

import jax
import jax.numpy as jnp
from jax import lax
from jax.experimental import pallas as pl
from jax.experimental.pallas import tpu as pltpu

F32 = jnp.float32
BF16 = jnp.bfloat16

D_MODEL = 2048
DEPTH = 4
N_MOD = 9
D_FF = 5632
NORM_EPS = 1e-6

A_HEADS = 16
A_HEAD_DIM = 64
A_WIDTH = A_HEADS * A_HEAD_DIM
A_DECAY_RANK = 64
A_ICL_RANK = 64
A_GATE_RANK = 128
A_PROJ = 3 * A_WIDTH + A_DECAY_RANK + A_ICL_RANK + A_GATE_RANK
A_GN_EPS = 64e-5

B_HEADS = 4
B_KEY_DIM = 128
B_VAL_DIM = 256
B_KEY_WIDTH = B_HEADS * B_KEY_DIM
B_WIDTH = B_HEADS * B_VAL_DIM
B_ALPHA_RANK = 16
B_TAU = 16.0
B_PROJ = 2 * B_KEY_WIDTH + 2 * B_WIDTH + B_ALPHA_RANK
EVEN_PROJ = A_PROJ + B_PROJ

C_WIDTH = D_MODEL
C_BLOCKS = 8
C_BLOCK = C_WIDTH // C_BLOCKS
C_CONV = 4
C_POW = 8.0

LANES = 128
SUBLANES = 8
VMEM_LIMIT = 60 * 1024 * 1024

CHUNK = 64
RWKV_STEP_CHUNKS = 2
GLA_STEP_CHUNKS = 2
GLA_SUB = SUBLANES
PROMPT_ROW_TILE = 1024
OUTPROJ_ROW_TILE = 512
TOKENWISE_ROW_TILE = 256
FFN_COL_TILE = 512
EVEN_COL_TILE = 13 * LANES
ODD_COL_TILE = 8 * LANES
STEP_ROWS = SUBLANES


def _params(*sem):
    return pltpu.CompilerParams(dimension_semantics=sem, vmem_limit_bytes=VMEM_LIMIT)


def _bdot(a, b):
    return jnp.dot(a.astype(BF16), b.astype(BF16), preferred_element_type=F32)


def _split3(x):
    hi = x.astype(BF16)
    r1 = x - hi.astype(F32)
    mid = r1.astype(BF16)
    lo = (r1 - mid.astype(F32)).astype(BF16)
    return hi, mid, lo


def _dot_exact_rhs(a_bf16, x):
    hi, mid, lo = _split3(x)
    f = lambda y: jnp.dot(a_bf16, y, preferred_element_type=F32)
    return f(hi) + f(mid) + f(lo)


def _dot_exact_lhs(x, b_bf16):
    hi, mid, lo = _split3(x)
    f = lambda y: jnp.dot(y, b_bf16, preferred_element_type=F32)
    return f(hi) + f(mid) + f(lo)


def _sigmoid(x):
    return jax.nn.sigmoid(x)


def _softplus(x):
    return jnp.maximum(x, 0.0) + jnp.log1p(jnp.exp(-jnp.abs(x)))


def _gelu_tanh(x):
    return 0.5 * x * (1.0 + jnp.tanh(0.7978845608028654 * (x + 0.044715 * (x * x * x))))


def _norm_mod(x, shift, scale):
    ms = jnp.mean(x * x, axis=-1, keepdims=True)
    return x * lax.rsqrt(ms + NORM_EPS) * (1.0 + scale) + shift


def _ada_kernel(c_ref, w_ref, b_ref, o_ref):
    c = c_ref[...]
    act = (c * _sigmoid(c)).astype(BF16)
    o_ref[...] = jnp.dot(act, w_ref[...].astype(BF16), preferred_element_type=F32) + b_ref[...]


def _ada(c_all, w_ada, b_ada):
    rows = c_all.shape[0]
    tn = 1024
    per = D_MODEL // tn
    b4 = b_ada.reshape(DEPTH, N_MOD * per, 1, tn)
    return pl.pallas_call(
        _ada_kernel,
        grid=(DEPTH, N_MOD * per),
        in_specs=[
            pl.BlockSpec((rows, D_MODEL), lambda l, j: (0, 0)),
            pl.BlockSpec((None, D_MODEL, tn), lambda l, j: (l, 0, j)),
            pl.BlockSpec((None, None, 1, tn), lambda l, j: (l, j, 0, 0)),
        ],
        out_specs=pl.BlockSpec((None, None, rows, tn), lambda l, j: (l, j // per, 0, j % per)),
        out_shape=jax.ShapeDtypeStruct((DEPTH, N_MOD, rows, D_MODEL), F32),
        compiler_params=_params("arbitrary", "arbitrary"),
        name="ada",
    )(c_all, w_ada, b4)


class _Mod:
    def __init__(self, arr, per_row, seq_len):
        self.arr = arr
        self.per_row = per_row
        self.seq_len = seq_len

    def spec(self, layer, m, tm):
        if self.per_row:
            return pl.BlockSpec((None, None, tm, D_MODEL), lambda i, *_: (layer, m, i, 0))
        per_seq = self.seq_len // tm
        return pl.BlockSpec((None, None, None, 1, D_MODEL), lambda i, *_: (layer, m, i // per_seq, 0, 0))


def _ffn_kernel(x_ref, sh_ref, sc_ref, g_ref, wg_ref, wu_ref, wd_ref, *rest, n_j, sub, final, convert):
    rest = list(rest)
    fw_ref = rest.pop(0) if final else None
    f32_tiles = [rest.pop(0) for _ in range(2)] if convert else []
    o_ref = rest.pop(0)
    bf16_tiles = [rest.pop(0) for _ in range(2)] if convert else []
    h_ref, = rest
    j = pl.program_id(1)
    tm = x_ref.shape[0]

    @pl.when(j == 0)
    def _():
        def body(s, carry):
            rows = pl.ds(pl.multiple_of(s * sub, sub), sub)
            sh = sh_ref[...] if sh_ref.shape[0] == 1 else sh_ref[rows, :]
            sc = sc_ref[...] if sc_ref.shape[0] == 1 else sc_ref[rows, :]
            h_ref[rows, :] = _norm_mod(x_ref[rows, :], sh, sc).astype(BF16)
            o_ref[rows, :] = jnp.zeros((sub, D_MODEL), F32)
            return carry
        lax.fori_loop(0, tm // sub, body, 0)

    for src, dst in zip(f32_tiles, bf16_tiles):
        dst[...] = src[...].astype(BF16)

    h = h_ref[...]
    gt = jnp.dot(h, wg_ref[...], preferred_element_type=F32)
    ut = jnp.dot(h, wu_ref[...], preferred_element_type=F32)
    act = (gt * _sigmoid(gt) * ut).astype(BF16)
    o_ref[...] += jnp.dot(act, wd_ref[...], preferred_element_type=F32)

    @pl.when(j == n_j - 1)
    def _():
        def body(s, carry):
            rows = pl.ds(pl.multiple_of(s * sub, sub), sub)
            g = g_ref[...] if g_ref.shape[0] == 1 else g_ref[rows, :]
            y = x_ref[rows, :] + 0.5 * g * o_ref[rows, :]
            if final:
                ms = jnp.mean(y * y, axis=-1, keepdims=True)
                y = y * lax.rsqrt(ms + NORM_EPS) * fw_ref[...]
            o_ref[rows, :] = y
            return carry
        lax.fori_loop(0, tm // sub, body, 0)


def _ffn(x, mod, layer, which, weights, tm, tf, final_w=None, convert=None):
    m_rows = x.shape[0]
    n_i = m_rows // tm
    n_j = D_FF // tf
    m0 = 6 * which
    sub = min(tm, 128)
    final = final_w is not None
    in_specs = [
        pl.BlockSpec((tm, D_MODEL), lambda i, j: (i, 0)),
        mod.spec(layer, m0, tm), mod.spec(layer, m0 + 1, tm), mod.spec(layer, m0 + 2, tm),
        pl.BlockSpec((D_MODEL, tf), lambda i, j: (0, j)),
        pl.BlockSpec((D_MODEL, tf), lambda i, j: (0, j + n_j)),
        pl.BlockSpec((tf, D_MODEL), lambda i, j: (j, 0)),
    ]
    args = [x, mod.arr, mod.arr, mod.arr, weights[0], weights[0], weights[1]]
    if final:
        in_specs.append(pl.BlockSpec((1, D_MODEL), lambda i, j: (0, 0)))
        args.append(final_w.reshape(1, D_MODEL))
    out_specs = [pl.BlockSpec((tm, D_MODEL), lambda i, j: (i, 0))]
    out_shape = [jax.ShapeDtypeStruct((m_rows, D_MODEL), F32)]
    if convert is not None:
        w_up32, w_down32, l2, s2 = convert
        up_tile = (D_MODEL // n_i, 2 * D_FF // n_j)
        down_tile = (D_FF // n_j, D_MODEL // n_i)
        assert up_tile[0] * n_i == D_MODEL and up_tile[1] * n_j == 2 * D_FF and up_tile[1] % LANES == 0
        assert down_tile[0] * n_j == D_FF and down_tile[1] * n_i == D_MODEL and down_tile[1] % LANES == 0
        in_specs += [pl.BlockSpec((None, None) + up_tile, lambda i, j: (l2, s2, i, j)),
                     pl.BlockSpec((None, None) + down_tile, lambda i, j: (l2, s2, j, i))]
        args += [w_up32, w_down32]
        out_specs += [pl.BlockSpec(up_tile, lambda i, j: (i, j)), pl.BlockSpec(down_tile, lambda i, j: (j, i))]
        out_shape += [jax.ShapeDtypeStruct((D_MODEL, 2 * D_FF), BF16), jax.ShapeDtypeStruct((D_FF, D_MODEL), BF16)]
    out = pl.pallas_call(
        functools.partial(_ffn_kernel, n_j=n_j, sub=sub, final=final, convert=convert is not None),
        grid=(n_i, n_j),
        in_specs=in_specs,
        out_specs=out_specs,
        out_shape=out_shape,
        scratch_shapes=[pltpu.VMEM((tm, D_MODEL), BF16)],
        compiler_params=_params("arbitrary", "arbitrary"),
        name="ffn",
    )(*args)
    return (out[0], (out[1], out[2])) if convert is not None else out[0]


def _inproj_kernel(x_ref, sh_ref, sc_ref, w_ref, o_ref, *rest, sub, emit, transposed):
    h_ref = rest[-1]
    j = pl.program_id(1)
    tm = x_ref.shape[0]

    @pl.when(j == 0)
    def _():
        def body(s, carry):
            rows = pl.ds(pl.multiple_of(s * sub, sub), sub)
            sh = sh_ref[...] if sh_ref.shape[0] == 1 else sh_ref[rows, :]
            sc = sc_ref[...] if sc_ref.shape[0] == 1 else sc_ref[rows, :]
            h_ref[rows, :] = _norm_mod(x_ref[rows, :], sh, sc).astype(BF16)
            return carry
        lax.fori_loop(0, tm // sub, body, 0)

    w = w_ref[...]
    if transposed:
        w = w.T
    if emit:
        w = w.astype(BF16)
        rest[0][...] = w
    o_ref[...] = jnp.dot(h_ref[...], w, preferred_element_type=F32)


def _inproj(x, mod, layer, w, widx, tm, tn, transposed=False):
    m_rows = x.shape[0]
    n_out = w.shape[-2] if transposed else w.shape[-1]
    sub = min(tm, 128)
    emit = widx is not None
    if transposed:
        w_spec = pl.BlockSpec((None, tn, D_MODEL), lambda i, j: (widx, j, 0))
    elif emit:
        w_spec = pl.BlockSpec((None, D_MODEL, tn), lambda i, j: (widx, 0, j))
    else:
        w_spec = pl.BlockSpec((D_MODEL, tn), lambda i, j: (0, j))
    out_specs = [pl.BlockSpec((tm, tn), lambda i, j: (i, j))]
    out_shape = [jax.ShapeDtypeStruct((m_rows, n_out), F32)]
    if emit:
        assert m_rows == tm
        out_specs.append(pl.BlockSpec((D_MODEL, tn), lambda i, j: (0, j)))
        out_shape.append(jax.ShapeDtypeStruct((D_MODEL, n_out), BF16))
    out = pl.pallas_call(
        functools.partial(_inproj_kernel, sub=sub, emit=emit, transposed=transposed),
        grid=(m_rows // tm, pl.cdiv(n_out, tn)),
        in_specs=[
            pl.BlockSpec((tm, D_MODEL), lambda i, j: (i, 0)),
            mod.spec(layer, 3, tm), mod.spec(layer, 4, tm),
            w_spec,
        ],
        out_specs=out_specs,
        out_shape=out_shape,
        scratch_shapes=[pltpu.VMEM((tm, D_MODEL), BF16)],
        compiler_params=_params("arbitrary", "arbitrary"),
        name="inproj",
    )(x, mod.arr, mod.arr, w)
    return tuple(out) if emit else out[0]


def _outproj_kernel(*refs, n_in, emit):
    y_refs = refs[:n_in]
    w_refs = refs[n_in:2 * n_in]
    x_ref, g_ref, o_ref = refs[2 * n_in:2 * n_in + 3]
    acc = None
    for k, (y_ref, w_ref) in enumerate(zip(y_refs, w_refs)):
        w = w_ref[...]
        if emit:
            w = w.astype(BF16)
            refs[2 * n_in + 3 + k][...] = w
        part = jnp.dot(y_ref[...].astype(BF16), w, preferred_element_type=F32)
        acc = part if acc is None else acc + part
    o_ref[...] = x_ref[...] + g_ref[...] * acc


def _outproj(ys, w, widx, x, mod, layer, tm):
    m_rows = x.shape[0]
    n_in = len(ys)
    kw = ys[0].shape[1]
    emit = widx is not None
    in_specs = [pl.BlockSpec((tm, kw), lambda i: (i, 0)) for _ in ys]
    if emit:
        assert m_rows == tm
        in_specs += [pl.BlockSpec((None, kw, D_MODEL), lambda i, k=k: (widx, k, 0)) for k in range(n_in)]
        w_args = [w] * n_in
    else:
        in_specs += [pl.BlockSpec((kw, D_MODEL), lambda i: (0, 0))] * n_in
        w_args = list(w)
    in_specs += [pl.BlockSpec((tm, D_MODEL), lambda i: (i, 0)), mod.spec(layer, 5, tm)]
    out_specs = [pl.BlockSpec((tm, D_MODEL), lambda i: (i, 0))]
    out_shape = [jax.ShapeDtypeStruct((m_rows, D_MODEL), F32)]
    if emit:
        out_specs += [pl.BlockSpec((kw, D_MODEL), lambda i: (0, 0))] * n_in
        out_shape += [jax.ShapeDtypeStruct((kw, D_MODEL), BF16)] * n_in
    out = pl.pallas_call(
        functools.partial(_outproj_kernel, n_in=n_in, emit=emit),
        grid=(m_rows // tm,),
        in_specs=in_specs,
        out_specs=out_specs,
        out_shape=out_shape,
        compiler_params=_params("arbitrary"),
        name="outproj",
    )(*ys, *w_args, x, mod.arr)
    return (out[0], tuple(out[1:])) if emit else out[0]


def _rwkv_prep_kernel(u_ref, prev_ref, mu_ref, wup_ref, aup_ref, gup_ref, vec_ref, *rest, seq_mode):
    if seq_mode:
        r_ref, lw_ref, k_ref, v_ref, kk_ref, a_ref, g_ref, carry_ref = rest
    else:
        seg_ref, r_ref, lw_ref, k_ref, v_ref, kk_ref, a_ref, g_ref, bonus_ref = rest
    u = u_ref[...]
    tm = u.shape[0]
    if seq_mode:
        @pl.when(pl.program_id(1) == 0)
        def _():
            carry_ref[...] = prev_ref[...]
        rolled = pltpu.roll(u, 1, axis=0)
        row = lax.broadcasted_iota(jnp.int32, u.shape, 0)
        prev = jnp.where(row == 0, carry_ref[...], rolled)
        carry_ref[...] = u[tm - 1:tm, :]
    else:
        prev = prev_ref[...]
    xs = u + mu_ref[...] * (prev - u)
    r = xs[:, 0:A_WIDTH]
    k = xs[:, A_WIDTH:2 * A_WIDTH]
    v = xs[:, 2 * A_WIDTH:3 * A_WIDTH]
    wa_lo = xs[:, 3 * A_WIDTH:3 * A_WIDTH + LANES]
    g_lo = xs[:, 3 * A_WIDTH + LANES:]
    w0, a0, k_k, k_a, r_k = (vec_ref[i:i + 1, :] for i in range(5))
    w_raw = -_softplus(-(w0 + _bdot(jnp.tanh(wa_lo), wup_ref[...]))) - 0.5
    lw = -jnp.exp(w_raw)
    a = _sigmoid(a0 + _bdot(wa_lo, aup_ref[...]))
    g = _bdot(_sigmoid(g_lo), gup_ref[...])
    kk = k * k_k
    k2 = k * (1.0 + (a - 1.0) * k_a)
    if not seq_mode:
        seg = seg_ref[...]
        kk = kk * lax.rsqrt(jnp.maximum(_dot_exact_lhs(kk * kk, seg), 1e-24))
        bonus_ref[...] = _dot_exact_lhs(r * k2 * r_k, seg) * v
    r_ref[...] = r
    lw_ref[...] = lw
    k_ref[...] = k2
    v_ref[...] = v
    kk_ref[...] = kk
    a_ref[...] = a
    g_ref[...] = g


def _rwkv_prep(u, prev, mu, wup_pad, aup_pad, g_up, vec, seg, n_seq, tm, seq_mode):
    rows = u.shape[0]
    per_seq = rows // n_seq // tm if seq_mode else 1
    grid = (n_seq, per_seq) if seq_mode else (rows // tm, 1)
    rowmap = (lambda b, t: (b * per_seq + t, 0)) if seq_mode else (lambda b, t: (b, 0))
    prev_spec = (pl.BlockSpec((None, 1, A_PROJ), lambda b, t: (b, 0, 0)) if seq_mode
                 else pl.BlockSpec((tm, A_PROJ), rowmap))
    const = lambda shape: pl.BlockSpec(shape, lambda b, t: (0,) * len(shape))
    out = jax.ShapeDtypeStruct((rows, A_WIDTH), F32)
    in_specs = [pl.BlockSpec((tm, A_PROJ), rowmap), prev_spec,
                const((1, A_PROJ)), const((LANES, A_WIDTH)), const((LANES, A_WIDTH)),
                const((A_GATE_RANK, A_WIDTH)), const((8, A_WIDTH))]
    args = [u, prev, mu, wup_pad, aup_pad, g_up, vec]
    if not seq_mode:
        in_specs.append(const((A_WIDTH, A_WIDTH)))
        args.append(seg)
    n_out = 7 if seq_mode else 8
    return pl.pallas_call(
        functools.partial(_rwkv_prep_kernel, seq_mode=seq_mode),
        grid=grid,
        in_specs=in_specs,
        out_specs=[pl.BlockSpec((tm, A_WIDTH), rowmap)] * n_out,
        out_shape=[out] * n_out,
        scratch_shapes=[pltpu.VMEM((1, A_PROJ), F32)] if seq_mode else [],
        compiler_params=_params("arbitrary", "arbitrary"),
        name="rwkv_prep",
    )(*args)


def _rwkv_post_kernel(y_ref, bonus_ref, g_ref, vec_ref, seg_ref, o_ref):
    y = y_ref[...]
    seg = seg_ref[...]
    inv_n = 1.0 / A_HEAD_DIM
    yc = y - _dot_exact_lhs(y, seg) * inv_n
    var = _dot_exact_lhs(yc * yc, seg) * inv_n
    yn = yc * lax.rsqrt(var + A_GN_EPS) * vec_ref[5:6, :] + vec_ref[6:7, :]
    o_ref[...] = (yn + bonus_ref[...]) * g_ref[...]


def _rwkv_post(y, bonus, g, vec, seg, tm):
    rows = y.shape[0]
    spec = pl.BlockSpec((tm, A_WIDTH), lambda i: (i, 0))
    return pl.pallas_call(
        _rwkv_post_kernel,
        grid=(rows // tm,),
        in_specs=[spec, spec, spec,
                  pl.BlockSpec((8, A_WIDTH), lambda i: (0, 0)),
                  pl.BlockSpec((A_WIDTH, A_WIDTH), lambda i: (0, 0))],
        out_specs=spec,
        out_shape=jax.ShapeDtypeStruct((rows, A_WIDTH), F32),
        compiler_params=_params("arbitrary"),
        name="rwkv_post",
    )(y, bonus, g, vec, seg)


def _rwkv_chunk_kernel(r_ref, lw_ref, k_ref, v_ref, kk_ref, a_ref, g_ref, vec_ref, h0_ref, y_ref, hT_ref, h_ref):
    ci = pl.program_id(1)

    @pl.when(ci == 0)
    def _():
        h_ref[...] = h0_ref[...]

    cs = CHUNK
    n2 = 2 * cs
    npair = A_WIDTH // LANES
    pairs = lambda x: jnp.stack([x[:, p * LANES:(p + 1) * LANES] for p in range(npair)])
    vec = vec_ref[...]
    vrow = lambda i: pairs(vec[i:i + 1, :])
    lane = lax.broadcasted_iota(jnp.int32, (1, 1, LANES), 2)
    m0 = (lane < A_HEAD_DIM).astype(F32)
    m1 = 1.0 - m0
    split = lambda x: jnp.concatenate([x * m0, x * m1], axis=1)
    dup = lambda x: jnp.concatenate([x, x], axis=1)
    own = jnp.concatenate([jnp.broadcast_to(m0, (1, cs, LANES)), jnp.broadcast_to(m1, (1, cs, LANES))], axis=1)
    bf = lambda x: x.astype(BF16)

    def bmm(x, y):
        return jnp.einsum('hab,hbc->hac', x, y, preferred_element_type=F32)

    def bmm_nt(x, y):
        return jnp.einsum('han,hbn->hab', x, y, preferred_element_type=F32)

    def bmm_tn(x, y):
        return jnp.einsum('hca,hcb->hab', x, y, preferred_element_type=F32)

    row = lax.broadcasted_iota(jnp.int32, (cs, cs), 0)
    col = lax.broadcasted_iota(jnp.int32, (cs, cs), 1)
    tri = (col <= row).astype(BF16)
    row2 = lax.broadcasted_iota(jnp.int32, (n2, n2), 0)
    col2 = lax.broadcasted_iota(jnp.int32, (n2, n2), 1)
    same = (row2 >= cs) == (col2 >= cs)
    tok_r = jnp.where(row2 >= cs, row2 - cs, row2)
    tok_c = jnp.where(col2 >= cs, col2 - cs, col2)
    strict = jnp.where(same, (tok_c < tok_r).astype(F32), 0.0)
    incl = jnp.where(same, (tok_c <= tok_r).astype(F32), 0.0)
    eye2 = (row2 == col2).astype(F32)
    nrow = lax.broadcasted_iota(jnp.int32, (LANES, LANES), 0)
    ncol = lax.broadcasted_iota(jnp.int32, (LANES, LANES), 1)
    eye_n = (nrow == ncol).astype(F32)

    h = h_ref[...]
    for sub in range(r_ref.shape[0] // cs):
        rows = slice(sub * cs, (sub + 1) * cs)
        lw_all = lw_ref[rows, :]
        c = pairs(_dot_exact_rhs(tri, lw_all))
        lw = pairs(lw_all)
        c_end = c[:, cs - 1:cs, :]
        e_pos = dup(jnp.exp(c))
        e_neg = dup(jnp.exp(-c))
        e_prev = dup(jnp.exp(c - lw))
        e_end = dup(jnp.exp(c_end - c))
        g_end = jnp.exp(c_end)

        kk = split(pairs(kk_ref[rows, :]))
        kk = kk * lax.rsqrt(jnp.maximum(jnp.sum(kk * kk, axis=-1, keepdims=True), 1e-24))
        b_vec = kk * dup(pairs(a_ref[rows, :]))
        r2 = split(pairs(r_ref[rows, :]))
        k2 = split(pairs(k_ref[rows, :]))
        v2 = split(pairs(v_ref[rows, :]))
        bonus = jnp.sum(r2 * k2 * vrow(4), axis=-1, keepdims=True) * v2
        rt = r2 * e_pos
        at = bf(-kk * e_prev)
        bt = bf(b_vec * e_neg)
        kt = bf(k2 * e_neg)
        bh = bf(b_vec * e_end)
        kh = bf(k2 * e_end)
        v2b = bf(v2)

        gram = bmm_nt(jnp.concatenate([at, bf(rt)], axis=1), jnp.concatenate([bt, kt], axis=1))
        a_ab = gram[:, :n2, :n2] * strict
        a_ak = gram[:, :n2, n2:] * strict
        m_b = gram[:, n2:, :n2] * incl
        m_k = gram[:, n2:, n2:] * incl

        tinv = eye2 + a_ab
        a_b = bf(a_ab)
        apow = bmm(a_b, a_b)
        span = 2
        while 2 * span < cs:
            ap_b = bf(apow)
            both = bmm(jnp.concatenate([bf(tinv), ap_b], axis=1), ap_b)
            tinv = tinv + both[:, :n2]
            apow = both[:, n2:]
            span *= 2
        tinv = tinv + bmm(bf(tinv), bf(apow))

        akv = bmm(bf(a_ak), v2b)
        pq = bf(bmm(bf(tinv), jnp.concatenate([at, bf(akv)], axis=-1)))
        mpq = bmm(bf(m_b), pq)
        p2 = rt + mpq[..., :LANES]
        y0 = mpq[..., LANES:] + bmm(bf(m_k), v2b)
        tpq = bmm_tn(bh, pq)
        gm = eye_n * g_end + tpq[..., :LANES]
        hadd = tpq[..., LANES:] + bmm_tn(kh, v2b)

        yh = bmm(jnp.concatenate([bf(p2), bf(gm)], axis=1), bf(h))
        y2 = yh[:, :n2] + y0
        h = yh[:, n2:] + hadd

        inv_n = 1.0 / A_HEAD_DIM
        yc = (y2 - jnp.sum(y2, axis=-1, keepdims=True) * inv_n) * own
        var = jnp.sum(yc * yc, axis=-1, keepdims=True) * inv_n
        z = yc * lax.rsqrt(var + A_GN_EPS) * vrow(5) + vrow(6) * own + bonus
        out = (z[:, :cs] + z[:, cs:]) * pairs(g_ref[rows, :])
        for p in range(npair):
            y_ref[rows, p * LANES:(p + 1) * LANES] = out[p]

    h_ref[...] = h

    @pl.when(ci == pl.num_programs(1) - 1)
    def _():
        hT_ref[...] = h


def _rwkv_chunk(r, lw, k, v, kk, a, g, vec, n_seq, h0):
    rows = r.shape[0]
    step_rows = RWKV_STEP_CHUNKS * CHUNK
    nc = rows // n_seq // step_rows
    npair = A_WIDTH // LANES
    spec = pl.BlockSpec((step_rows, A_WIDTH), lambda b, c: (b * nc + c, 0))
    hspec = pl.BlockSpec((None, npair, LANES, LANES), lambda b, c: (b, 0, 0, 0))
    return pl.pallas_call(
        _rwkv_chunk_kernel,
        grid=(n_seq, nc),
        in_specs=[spec] * 7 + [pl.BlockSpec((8, A_WIDTH), lambda b, c: (0, 0)), hspec],
        out_specs=[spec, hspec],
        out_shape=[jax.ShapeDtypeStruct((rows, A_WIDTH), F32),
                   jax.ShapeDtypeStruct((n_seq, npair, LANES, LANES), F32)],
        scratch_shapes=[pltpu.VMEM((npair, LANES, LANES), F32)],
        compiler_params=_params("arbitrary", "arbitrary"),
        name="rwkv_chunk",
    )(r, lw, k, v, kk, a, g, vec, h0)


def _pair_states(s):
    n = s.shape[0]
    nd = A_HEAD_DIM
    h = jnp.swapaxes(s, -1, -2).reshape(n, A_HEADS // 2, 2, nd, nd)
    out = jnp.zeros((n, A_HEADS // 2, 2 * nd, 2 * nd), s.dtype)
    return out.at[:, :, :nd, :nd].set(h[:, :, 0]).at[:, :, nd:, nd:].set(h[:, :, 1])


def _unpair_states(hp):
    n = hp.shape[0]
    nd = A_HEAD_DIM
    h = jnp.stack([hp[:, :, :nd, :nd], hp[:, :, nd:, nd:]], axis=2).reshape(n, A_HEADS, nd, nd)
    return jnp.swapaxes(h, -1, -2)


def _rwkv_step_kernel(r_ref, lw_ref, k_ref, v_ref, kk_ref, a_ref, s_ref, acc_ref, y_ref, so_ref,
                      r_t, w_t, k_t, v_t, nkk_t, b_t, y_t):
    h = pl.program_id(0)
    nd = A_HEAD_DIM

    @pl.when(h == 0)
    def _():
        kk = kk_ref[...]
        r_t[...] = r_ref[...].T
        w_t[...] = jnp.exp(lw_ref[...]).T
        k_t[...] = k_ref[...].T
        v_t[...] = v_ref[...].T
        nkk_t[...] = (-kk).T
        b_t[...] = (kk * a_ref[...]).T

    first = pl.multiple_of(h * nd, nd)
    rows = pl.ds(first, nd)
    r, w, k, nkk, b = r_t[rows, :], w_t[rows, :], k_t[rows, :], nkk_t[rows, :], b_t[rows, :]

    def body(i, carry):
        s = s_ref[i]
        s_kk = jnp.sum(s * nkk, axis=0, keepdims=True)
        s_new = s * w + s_kk * b + v_t[pl.ds(first + i, 1), :] * k
        so_ref[i] = s_new
        y_t[pl.ds(first + i, 1), :] = jnp.sum(s_new * r, axis=0, keepdims=True)
        return carry

    lax.fori_loop(0, nd, body, 0, unroll=4)

    @pl.when(h == pl.num_programs(0) - 1)
    def _():
        y_ref[...] = y_t[...].T


def _rwkv_step(r, lw, k, v, kk, a, states_t, j, new_states):
    n = r.shape[0]
    nd = A_HEAD_DIM
    rspec = pl.BlockSpec((n, A_WIDTH), lambda h: (0, 0))
    sspec = pl.BlockSpec((None, None, nd, nd, n), lambda h: (j, h, 0, 0, 0))
    return pl.pallas_call(
        _rwkv_step_kernel,
        grid=(A_HEADS,),
        in_specs=[rspec] * 6 + [sspec, pl.BlockSpec(memory_space=pl.ANY)],
        out_specs=[rspec, sspec],
        out_shape=[jax.ShapeDtypeStruct((n, A_WIDTH), F32), jax.ShapeDtypeStruct(states_t.shape, F32)],
        input_output_aliases={7: 1},
        scratch_shapes=[pltpu.VMEM((A_WIDTH, n), F32)] * 7,
        compiler_params=_params("arbitrary"),
        name="rwkv_step",
    )(r, lw, k, v, kk, a, states_t, new_states)


def _gla_log_alpha(a128, aup_ref, ab_ref):
    lane = lax.broadcasted_iota(jnp.int32, a128.shape, 1)
    a_lo = jnp.where(lane < B_ALPHA_RANK, a128, 0.0)
    x = _bdot(a_lo, aup_ref[...]) + ab_ref[...]
    return -_softplus(-x) * (1.0 / B_TAU)


def _gla_block(u_ref, aup_ref, ab_ref, nw_ref, o_ref, s_ref):
    ub = u_ref[...]
    cs = ub.shape[0]
    q_all = ub[:, 0:B_KEY_WIDTH] * (B_KEY_DIM ** -0.5)
    k_all = ub[:, B_KEY_WIDTH:2 * B_KEY_WIDTH]
    v_all = ub[:, 2 * B_KEY_WIDTH:2 * B_KEY_WIDTH + B_WIDTH]
    g_all = ub[:, 2 * B_KEY_WIDTH + B_WIDTH:2 * B_KEY_WIDTH + 2 * B_WIDTH]
    a128 = ub[:, 2 * B_KEY_WIDTH + 2 * B_WIDTH:2 * B_KEY_WIDTH + 2 * B_WIDTH + LANES]
    log_a = _gla_log_alpha(a128, aup_ref, ab_ref)

    row = lax.broadcasted_iota(jnp.int32, (cs, cs), 0)
    col = lax.broadcasted_iota(jnp.int32, (cs, cs), 1)
    cum = _dot_exact_rhs((col <= row).astype(BF16), log_a)
    e_pos = jnp.exp(cum)
    cum_end = cum[cs - 1:cs, :]
    e_end = jnp.exp(cum_end - cum)
    g_end = jnp.exp(cum_end)
    nk = B_KEY_DIM
    eye_k = (lax.broadcasted_iota(jnp.int32, (nk, nk), 0) == lax.broadcasted_iota(jnp.int32, (nk, nk), 1)).astype(F32)

    nblk = cs // GLA_SUB
    kw = B_KEY_WIDTH
    cum3 = cum.reshape(nblk, GLA_SUB, kw)
    before = jnp.concatenate([jnp.zeros((1, 1, kw), F32), cum3[:nblk - 1, GLA_SUB - 1:GLA_SUB, :]], axis=0)
    q_rel = q_all * jnp.exp(cum - jnp.broadcast_to(before, (nblk, GLA_SUB, kw)).reshape(cs, kw))
    tok = lax.broadcasted_iota(jnp.int32, (cs, 1), 0)
    q_parts, k_parts = [], []
    for blk in range(1, nblk):
        first = blk * GLA_SUB
        k_parts.append(jnp.where(tok < first, k_all * jnp.exp(jnp.minimum(before[blk] - cum, 0.0)), 0.0))
        q_parts.append(jnp.where((tok >= first) & (tok < first + GLA_SUB), q_rel, 0.0))

    q3 = q_all.reshape(nblk, GLA_SUB, kw)
    k3 = k_all.reshape(nblk, GLA_SUB, kw)
    v3 = v_all.reshape(nblk, GLA_SUB, B_WIDTH)
    sub_i = lax.broadcasted_iota(jnp.int32, (1, GLA_SUB, 1), 1)
    o_diag = [jnp.zeros((nblk, GLA_SUB, B_VAL_DIM), F32) for _ in range(B_HEADS)]
    for jj in range(GLA_SUB):
        pair = q3 * k3[:, jj:jj + 1, :] * jnp.exp(jnp.minimum(cum3 - cum3[:, jj:jj + 1, :], 0.0))
        pair = jnp.where(sub_i >= jj, pair, 0.0)
        for h in range(B_HEADS):
            score = jnp.sum(pair[:, :, h * nk:(h + 1) * nk], axis=-1, keepdims=True)
            o_diag[h] = o_diag[h] + score * v3[:, jj:jj + 1, h * B_VAL_DIM:(h + 1) * B_VAL_DIM]

    for h in range(B_HEADS):
        ks = slice(h * B_KEY_DIM, (h + 1) * B_KEY_DIM)
        vs = slice(h * B_VAL_DIM, (h + 1) * B_VAL_DIM)
        qe = q_all[:, ks] * e_pos[:, ks]
        ke = k_all[:, ks] * e_end[:, ks]
        vh = v_all[:, vs]
        s = s_ref[h]
        q_cat = jnp.concatenate([part[:, ks] for part in q_parts], axis=-1).astype(BF16)
        k_cat = jnp.concatenate([part[:, ks] for part in k_parts], axis=-1).astype(BF16)
        scores = jnp.einsum('id,jd->ij', q_cat, k_cat, preferred_element_type=F32)
        o = _bdot(scores, vh) + _bdot(qe, s) + o_diag[h].reshape(cs, B_VAL_DIM)
        g_col = jnp.sum(eye_k * g_end[:, ks], axis=-1, keepdims=True)
        s_ref[h] = s * g_col + jnp.einsum('jd,jv->dv', ke.astype(BF16), vh.astype(BF16),
                                           preferred_element_type=F32)
        ms = jnp.mean(o * o, axis=-1, keepdims=True)
        gh = g_all[:, vs]
        o_ref[:, vs] = o * lax.rsqrt(ms + NORM_EPS) * nw_ref[:, vs] * (gh * _sigmoid(gh))


def _gla_chunk_kernel(u_ref, aup_ref, ab_ref, nw_ref, s0_ref, o_ref, sT_ref, s_ref):
    ci = pl.program_id(1)

    @pl.when(ci == 0)
    def _():
        s_ref[...] = s0_ref[...]

    for sub in range(u_ref.shape[0] // CHUNK):
        rows = pl.ds(sub * CHUNK, CHUNK)
        _gla_block(u_ref.at[rows, :], aup_ref, ab_ref, nw_ref, o_ref.at[rows, :], s_ref)

    @pl.when(ci == pl.num_programs(1) - 1)
    def _():
        sT_ref[...] = s_ref[...]


def _gla_chunk(u, n_seq, alpha_up_pad, alpha_b, norm_w, s0):
    rows = u.shape[0]
    step_rows = GLA_STEP_CHUNKS * CHUNK
    nc = rows // n_seq // step_rows
    return pl.pallas_call(
        _gla_chunk_kernel,
        grid=(n_seq, nc),
        in_specs=[
            pl.BlockSpec((step_rows, A_PROJ), lambda b, c: (b * nc + c, 1)),
            pl.BlockSpec((LANES, B_KEY_WIDTH), lambda b, c: (0, 0)),
            pl.BlockSpec((1, B_KEY_WIDTH), lambda b, c: (0, 0)),
            pl.BlockSpec((1, B_WIDTH), lambda b, c: (0, 0)),
            pl.BlockSpec((None, B_HEADS, B_KEY_DIM, B_VAL_DIM), lambda b, c: (b, 0, 0, 0)),
        ],
        out_specs=[
            pl.BlockSpec((step_rows, B_WIDTH), lambda b, c: (b * nc + c, 0)),
            pl.BlockSpec((None, B_HEADS, B_KEY_DIM, B_VAL_DIM), lambda b, c: (b, 0, 0, 0)),
        ],
        out_shape=[jax.ShapeDtypeStruct((rows, B_WIDTH), F32),
                   jax.ShapeDtypeStruct((n_seq, B_HEADS, B_KEY_DIM, B_VAL_DIM), F32)],
        scratch_shapes=[pltpu.VMEM((B_HEADS, B_KEY_DIM, B_VAL_DIM), F32)],
        compiler_params=_params("arbitrary", "arbitrary"),
        name="gla_chunk",
    )(u, alpha_up_pad, alpha_b, norm_w, s0)


def _gla_step_kernel(u_ref, aup_ref, ab_ref, nw_ref, s_ref, acc_ref, o_ref, so_ref, la_ref):
    nb = u_ref.shape[0]
    nk = B_KEY_DIM
    off_k, off_v, off_g = B_KEY_WIDTH, 2 * B_KEY_WIDTH, 2 * B_KEY_WIDTH + B_WIDTH
    off_a = 2 * B_KEY_WIDTH + 2 * B_WIDTH
    la_ref[...] = _gla_log_alpha(u_ref[:, off_a:off_a + LANES], aup_ref, ab_ref)
    eye = (lax.broadcasted_iota(jnp.int32, (nk, nk), 0) == lax.broadcasted_iota(jnp.int32, (nk, nk), 1)).astype(F32)
    col = lambda x: jnp.sum(eye * x, axis=-1, keepdims=True)

    def body(n, carry):
        row = pl.ds(n, 1)
        ur = u_ref[row, :]
        dec_all = jnp.exp(la_ref[row, :])
        outs = []
        for h in range(B_HEADS):
            q = ur[:, h * nk:(h + 1) * nk] * (B_KEY_DIM ** -0.5)
            k = ur[:, off_k + h * nk:off_k + (h + 1) * nk]
            v = ur[:, off_v + h * B_VAL_DIM:off_v + (h + 1) * B_VAL_DIM]
            g = ur[:, off_g + h * B_VAL_DIM:off_g + (h + 1) * B_VAL_DIM]
            dec = dec_all[:, h * nk:(h + 1) * nk]
            s = s_ref[n, h]
            qk = jnp.sum(q * k, axis=-1, keepdims=True)
            o = qk * v + jnp.sum(col(q * dec) * s, axis=0, keepdims=True)
            so_ref[n, h] = s * col(dec) + col(k) * v
            ms = jnp.mean(o * o, axis=-1, keepdims=True)
            outs.append(o * lax.rsqrt(ms + NORM_EPS) * nw_ref[:, h * B_VAL_DIM:(h + 1) * B_VAL_DIM]
                        * (g * _sigmoid(g)))
        o_ref[row, :] = jnp.concatenate(outs, axis=-1)
        return carry

    lax.fori_loop(0, nb, body, 0)


def _gla_step(u, alpha_up_pad, alpha_b, norm_w, states, j, nb, new_states):
    n = u.shape[0]
    sspec = pl.BlockSpec((None, nb, B_HEADS, B_KEY_DIM, B_VAL_DIM), lambda i: (j, i, 0, 0, 0))
    return pl.pallas_call(
        _gla_step_kernel,
        grid=(n // nb,),
        in_specs=[pl.BlockSpec((nb, A_PROJ), lambda i: (i, 1)),
                  pl.BlockSpec((LANES, B_KEY_WIDTH), lambda i: (0, 0)),
                  pl.BlockSpec((1, B_KEY_WIDTH), lambda i: (0, 0)),
                  pl.BlockSpec((1, B_WIDTH), lambda i: (0, 0)),
                  sspec, pl.BlockSpec(memory_space=pl.ANY)],
        out_specs=[pl.BlockSpec((nb, B_WIDTH), lambda i: (i, 0)), sspec],
        out_shape=[jax.ShapeDtypeStruct((n, B_WIDTH), F32), jax.ShapeDtypeStruct(states.shape, F32)],
        input_output_aliases={5: 1},
        scratch_shapes=[pltpu.VMEM((nb, B_KEY_WIDTH), F32)],
        compiler_params=_params("arbitrary"),
        name="gla_step",
    )(u, alpha_up_pad, alpha_b, norm_w, states, new_states)


def _lru_gates(xc, wa_ref, wx_ref, ba_ref, bx_ref, lam_ref):
    ra, ix = [], []
    for nb in range(C_BLOCKS):
        xb = xc[:, nb * C_BLOCK:(nb + 1) * C_BLOCK].astype(BF16)
        ra.append(jnp.dot(xb, wa_ref[nb].astype(BF16), preferred_element_type=F32))
        ix.append(jnp.dot(xb, wx_ref[nb].astype(BF16), preferred_element_type=F32))
    r = _sigmoid(jnp.concatenate(ra, axis=-1) + ba_ref[...])
    i_g = _sigmoid(jnp.concatenate(ix, axis=-1) + bx_ref[...])
    log_a = -C_POW * r * _softplus(-lam_ref[...])
    a = jnp.exp(log_a)
    b = jnp.sqrt(1.0 - a * a) * (i_g * xc)
    return a, b


def _lru_seq_kernel(u_ref, cp_ref, h0_ref, cw_ref, cb_ref, wa_ref, wx_ref, ba_ref, bx_ref, lam_ref,
                    y_ref, tail_ref, hT_ref, carry_ref, h_ref, a_s, b_s):
    ti = pl.program_id(1)

    @pl.when(ti == 0)
    def _():
        carry_ref[...] = cp_ref[...]
        h_ref[...] = h0_ref[...]

    gate = u_ref[:, 0:C_WIDTH]
    xb = u_ref[:, C_WIDTH:2 * C_WIDTH]
    tm = xb.shape[0]
    carry = carry_ref[...]
    row8 = lax.broadcasted_iota(jnp.int32, (8, C_WIDTH), 0)
    xc = cb_ref[...] + cw_ref[C_CONV - 1:C_CONV, :] * xb
    for s in range(1, C_CONV):
        rolled = pltpu.roll(xb, s, axis=0)
        head = jnp.where(row8 < s, pltpu.roll(carry, s, axis=0), rolled[0:8, :])
        shifted = jnp.concatenate([head, rolled[8:, :]], axis=0)
        xc = xc + cw_ref[C_CONV - 1 - s:C_CONV - s, :] * shifted
    carry_ref[...] = xb[tm - 8:tm, :]
    tail_ref[...] = xb[tm - 8:tm, :]

    a, b = _lru_gates(xc, wa_ref, wx_ref, ba_ref, bx_ref, lam_ref)
    a_s[...] = a
    b_s[...] = b

    def body(t, h):
        h = a_s[pl.ds(t, 1), :] * h + b_s[pl.ds(t, 1), :]
        b_s[pl.ds(t, 1), :] = h
        return h

    h_last = lax.fori_loop(0, tm, body, h_ref[...], unroll=8)
    h_ref[...] = h_last
    hT_ref[...] = h_last
    y_ref[...] = _gelu_tanh(gate) * b_s[...]


def _lru_seq(u, n_seq, conv_prev8, h0, conv_w, conv_b, wa, wx, ba, bx, lam, tm):
    rows = u.shape[0]
    per_seq = rows // n_seq // tm
    const = lambda shape: pl.BlockSpec(shape, lambda b, t: (0,) * len(shape))
    return pl.pallas_call(
        _lru_seq_kernel,
        grid=(n_seq, per_seq),
        in_specs=[
            pl.BlockSpec((tm, 2 * C_WIDTH), lambda b, t: (b * per_seq + t, 0)),
            pl.BlockSpec((None, 8, C_WIDTH), lambda b, t: (b, 0, 0)),
            pl.BlockSpec((None, 1, C_WIDTH), lambda b, t: (b, 0, 0)),
            const((C_CONV, C_WIDTH)), const((1, C_WIDTH)),
            const((C_BLOCKS, C_BLOCK, C_BLOCK)), const((C_BLOCKS, C_BLOCK, C_BLOCK)),
            const((1, C_WIDTH)), const((1, C_WIDTH)), const((1, C_WIDTH)),
        ],
        out_specs=[
            pl.BlockSpec((tm, C_WIDTH), lambda b, t: (b * per_seq + t, 0)),
            pl.BlockSpec((None, 8, C_WIDTH), lambda b, t: (b, 0, 0)),
            pl.BlockSpec((None, 1, C_WIDTH), lambda b, t: (b, 0, 0)),
        ],
        out_shape=[jax.ShapeDtypeStruct((rows, C_WIDTH), F32),
                   jax.ShapeDtypeStruct((n_seq, 8, C_WIDTH), F32),
                   jax.ShapeDtypeStruct((n_seq, 1, C_WIDTH), F32)],
        scratch_shapes=[pltpu.VMEM((8, C_WIDTH), F32), pltpu.VMEM((1, C_WIDTH), F32),
                        pltpu.VMEM((tm, C_WIDTH), F32), pltpu.VMEM((tm, C_WIDTH), F32)],
        compiler_params=_params("arbitrary", "arbitrary"),
        name="lru_seq",
    )(u, conv_prev8, h0, conv_w, conv_b, wa, wx, ba, bx, lam)


def _lru_step_kernel(u_ref, cp_ref, h0_ref, cw_ref, cb_ref, wa_ref, wx_ref, ba_ref, bx_ref, lam_ref,
                     y_ref, cn_ref, h_ref):
    gate = u_ref[:, 0:C_WIDTH]
    xb = u_ref[:, C_WIDTH:2 * C_WIDTH]
    xc = cb_ref[...] + cw_ref[C_CONV - 1:C_CONV, :] * xb
    for i in range(C_CONV - 1):
        xc = xc + cw_ref[i:i + 1, :] * cp_ref[i]
    a, b = _lru_gates(xc, wa_ref, wx_ref, ba_ref, bx_ref, lam_ref)
    h = a * h0_ref[...] + b
    h_ref[...] = h
    y_ref[...] = _gelu_tanh(gate) * h
    for i in range(C_CONV - 2):
        cn_ref[i] = cp_ref[i + 1]
    cn_ref[C_CONV - 2] = xb


def _lru_step(u, conv_states, h0, j, conv_w, conv_b, wa, wx, ba, bx, lam):
    rows = u.shape[0]
    full = lambda shape: pl.BlockSpec(shape, lambda i: (0,) * len(shape))
    return pl.pallas_call(
        _lru_step_kernel,
        grid=(1,),
        in_specs=[full((rows, 2 * C_WIDTH)),
                  pl.BlockSpec((None, C_CONV - 1, rows, C_WIDTH), lambda i: (j, 0, 0, 0)),
                  full((rows, C_WIDTH)),
                  full((C_CONV, C_WIDTH)), full((1, C_WIDTH)),
                  full((C_BLOCKS, C_BLOCK, C_BLOCK)), full((C_BLOCKS, C_BLOCK, C_BLOCK)),
                  full((1, C_WIDTH)), full((1, C_WIDTH)), full((1, C_WIDTH))],
        out_specs=[full((rows, C_WIDTH)), full((C_CONV - 1, rows, C_WIDTH)), full((rows, C_WIDTH))],
        out_shape=[jax.ShapeDtypeStruct((rows, C_WIDTH), F32),
                   jax.ShapeDtypeStruct((C_CONV - 1, rows, C_WIDTH), F32),
                   jax.ShapeDtypeStruct((rows, C_WIDTH), F32)],
        compiler_params=_params("arbitrary"),
        name="lru_step",
    )(u, conv_states, h0, conv_w, conv_b, wa, wx, ba, bx, lam)


def _pad_rows(w, first, total):
    return jnp.zeros((total, w.shape[1]), w.dtype).at[first:first + w.shape[0]].set(w)


class _Group:
    def __init__(self, x, mod, n_seq, seq_len, states, tm, tf):
        self.x, self.mod, self.n_seq, self.seq_len, self.states, self.tm, self.tf = x, mod, n_seq, seq_len, states, tm, tf
        self.prompt = seq_len > 1
        self.out_shift, self.out_wkv, self.out_gla, self.out_conv, self.out_lru = [], [], [], [], []
        self.new_wkv_t = self.new_gla = None

    def new_states(self):
        wkv = jnp.stack(self.out_wkv) if self.new_wkv_t is None else jnp.transpose(self.new_wkv_t, (0, 4, 1, 2, 3))
        gla = jnp.stack(self.out_gla) if self.new_gla is None else self.new_gla
        return jnp.stack(self.out_shift), wkv, gla, jnp.stack(self.out_conv), jnp.stack(self.out_lru)


def _mixer(grp, l, wts, mxu):
    (w_ffn_up, w_ffn_down, w_in_even, w_out_even, a_mu, a_w_up, a_a_up, a_g_up, a_vec, b_alpha_up,
     b_alpha_b, b_norm_w, w_in_odd, w_out_odd, c_conv_w, c_conv_b, c_wa, c_ba, c_wx, c_bx, c_lam,
     final_norm_w) = wts
    x, mod, n_seq, seq_len, tm, prompt = grp.x, grp.mod, grp.n_seq, grp.seq_len, grp.tm, grp.prompt
    st_shift, st_wkv, st_gla, st_conv, st_lru = grp.states
    rows = x.shape[0]
    j = l // 2
    tn_even, tn_odd = EVEN_COL_TILE, ODD_COL_TILE
    seg = jnp.kron(jnp.eye(A_HEADS, dtype=F32), jnp.ones((A_HEAD_DIM, A_HEAD_DIM), F32)).astype(BF16)
    if l % 2 == 0:
        if prompt:
            u = _inproj(x, mod, l, mxu['in', l], None, tm, tn_even)
        else:
            u, mxu['in', l] = _inproj(x, mod, l, jnp.swapaxes(w_in_even, 1, 2), j, tm, tn_even,
                                      transposed=True)
        wup_pad = _pad_rows(a_w_up[j], 0, LANES)
        aup_pad = _pad_rows(a_a_up[j], A_DECAY_RANK, LANES)
        vec8 = _pad_rows(a_vec[j], 0, SUBLANES)
        alpha_pad = _pad_rows(b_alpha_up[j], 0, LANES)
        alpha_b = b_alpha_b[j].reshape(1, B_KEY_WIDTH)
        mu = a_mu[j].reshape(1, A_PROJ)
        norm_w = b_norm_w[j].reshape(1, B_WIDTH)
        if prompt:
            r, lw, k2, v, kk, al, g = _rwkv_prep(u, st_shift[j].reshape(n_seq, 1, A_PROJ), mu, wup_pad,
                                                 aup_pad, a_g_up[j], vec8, seg, n_seq, TOKENWISE_ROW_TILE, True)
            y_a, h_t = _rwkv_chunk(r, lw, k2, v, kk, al, g, vec8, n_seq, _pair_states(st_wkv[j]))
            s_wkv = _unpair_states(h_t)
            s_shift = u.reshape(n_seq, seq_len, EVEN_PROJ)[:, -1, :A_PROJ]
            y_b, s_gla = _gla_chunk(u, n_seq, alpha_pad, alpha_b, norm_w, st_gla[j])
        else:
            r, lw, k2, v, kk, al, g, bonus = _rwkv_prep(u, st_shift[j], mu, wup_pad, aup_pad, a_g_up[j],
                                                        vec8, seg, rows, rows, False)
            st_wkv_t = jnp.transpose(st_wkv, (0, 2, 3, 4, 1))
            y_raw, grp.new_wkv_t = _rwkv_step(r, lw, k2, v, kk, al, st_wkv_t, j,
                                              jnp.zeros_like(st_wkv_t) if grp.new_wkv_t is None else grp.new_wkv_t)
            y_a = _rwkv_post(y_raw, bonus, g, vec8, seg, rows)
            s_shift = u[:, :A_PROJ]
            y_b, grp.new_gla = _gla_step(u, alpha_pad, alpha_b, norm_w, st_gla, j, STEP_ROWS,
                                         jnp.zeros_like(st_gla) if grp.new_gla is None else grp.new_gla)
        if prompt:
            x = _outproj([y_a, y_b], mxu['out', l], None, x, mod, l, OUTPROJ_ROW_TILE)
        else:
            x, mxu['out', l] = _outproj([y_a, y_b], w_out_even, j, x, mod, l, tm)
        grp.out_shift.append(s_shift)
        if prompt:
            grp.out_wkv.append(s_wkv)
            grp.out_gla.append(s_gla)
    else:
        if prompt:
            u = _inproj(x, mod, l, mxu['in', l], None, tm, tn_odd)
        else:
            u, mxu['in', l] = _inproj(x, mod, l, w_in_odd, j, tm, tn_odd)
        lru_w = (c_conv_w[j], c_conv_b[j].reshape(1, C_WIDTH), c_wa[j], c_wx[j],
                 c_ba[j].reshape(1, C_WIDTH), c_bx[j].reshape(1, C_WIDTH), c_lam[j].reshape(1, C_WIDTH))
        if prompt:
            cp8 = jnp.concatenate([jnp.zeros((n_seq, 8 - (C_CONV - 1), C_WIDTH), F32), st_conv[j]], axis=1)
            y_c, tail, h_t = _lru_seq(u, n_seq, cp8, st_lru[j].reshape(n_seq, 1, C_WIDTH), *lru_w,
                                      TOKENWISE_ROW_TILE)
            s_conv = tail[:, 8 - (C_CONV - 1):, :]
            s_lru = h_t.reshape(n_seq, C_WIDTH)
        else:
            y_c, cn, s_lru = _lru_step(u, jnp.swapaxes(st_conv, 1, 2), st_lru[j], j, *lru_w)
            s_conv = jnp.swapaxes(cn, 0, 1)
        if prompt:
            x = _outproj([y_c], mxu['out', l], None, x, mod, l, OUTPROJ_ROW_TILE)
        else:
            x, mxu['out', l] = _outproj([y_c], w_out_odd, j, x, mod, l, tm)
        grp.out_conv.append(s_conv)
        grp.out_lru.append(s_lru)
    grp.x = x


def _trunk(sample, prompt, wts):
    w_ffn_up, w_ffn_down, final_norm_w = wts[0], wts[1], wts[-1]
    subs = [(l, which) for l in range(DEPTH) for which in (0, 1)]
    mxu = {('ffn', 0, 0): (w_ffn_up[0, 0].astype(BF16), w_ffn_down[0, 0].astype(BF16))}
    for idx, (l, which) in enumerate(subs):
        final_w = final_norm_w if idx == len(subs) - 1 else None
        weights = mxu['ffn', l, which]
        sample.x = _ffn(sample.x, sample.mod, l, which, weights, sample.tm, sample.tf, final_w=final_w)
        if idx + 1 < len(subs):
            nxt = subs[idx + 1]
            prompt.x, mxu[('ffn',) + nxt] = _ffn(prompt.x, prompt.mod, l, which, weights, prompt.tm, prompt.tf,
                                                 convert=(w_ffn_up, w_ffn_down) + nxt)
        else:
            prompt.x = _ffn(prompt.x, prompt.mod, l, which, weights, prompt.tm, prompt.tf, final_w=final_w)
        if which == 0:
            _mixer(sample, l, wts, mxu)
            _mixer(prompt, l, wts, mxu)


def kernel(x_prompt, x_sample, c_prompt, c_sample, state_rwkv_shift, state_rwkv_wkv, state_gla, state_conv, state_lru, w_ada, b_ada, w_ffn_up, w_ffn_down, w_in_even, w_out_even, a_mu, a_w_up, a_a_up, a_g_up, a_vec, b_alpha_up, b_alpha_b, b_norm_w, w_in_odd, w_out_odd, c_conv_w, c_conv_b, c_wa, c_ba, c_wx, c_bx, c_lam, final_norm_w):
    wts = (w_ffn_up, w_ffn_down, w_in_even, w_out_even, a_mu, a_w_up, a_a_up, a_g_up, a_vec, b_alpha_up,
           b_alpha_b, b_norm_w, w_in_odd, w_out_odd, c_conv_w, c_conv_b, c_wa, c_ba, c_wx, c_bx, c_lam,
           final_norm_w)
    n_p, t_p, _ = x_prompt.shape
    n_s, t_s, _ = x_sample.shape
    assert t_s == 1

    c_rows = n_p + n_s
    pad = (-c_rows) % 16
    c_all = jnp.concatenate([c_sample, c_prompt, jnp.zeros((pad, D_MODEL), F32)], axis=0)
    mod_all = _ada(c_all, w_ada, b_ada)
    mod_p = _Mod(mod_all[:, :, n_s:c_rows].reshape(DEPTH, N_MOD, n_p, 1, D_MODEL), False, t_p)
    mod_s = _Mod(mod_all, True, 1)

    fresh = lambda s: jnp.zeros((s.shape[0], n_p) + s.shape[2:], s.dtype)
    states = (state_rwkv_shift, state_rwkv_wkv, state_gla, state_conv, state_lru)
    sample = _Group(x_sample.reshape(n_s, D_MODEL), mod_s, n_s, 1, states, n_s, FFN_COL_TILE)
    prompt = _Group(x_prompt.reshape(n_p * t_p, D_MODEL), mod_p, n_p, t_p, tuple(fresh(s) for s in states),
                    PROMPT_ROW_TILE, FFN_COL_TILE)
    _trunk(sample, prompt, wts)
    return (prompt.x.reshape(n_p, t_p, D_MODEL), sample.x.reshape(n_s, 1, D_MODEL),
            *prompt.new_states(), *sample.new_states())
```

```python
import functools

import jax
import jax.numpy as jnp
from jax import lax
from jax.experimental import pallas as pl
from jax.experimental.pallas import tpu as pltpu

F32 = jnp.float32
BF16 = jnp.bfloat16

D_MODEL = 2048
DEPTH = 4
N_MOD = 9
D_FF = 5632
NORM_EPS = 1e-6

A_HEADS = 16
A_HEAD_DIM = 64
A_WIDTH = A_HEADS * A_HEAD_DIM
A_DECAY_RANK = 64
A_ICL_RANK = 64
A_GATE_RANK = 128
A_PROJ = 3 * A_WIDTH + A_DECAY_RANK + A_ICL_RANK + A_GATE_RANK
A_GN_EPS = 64e-5

B_HEADS = 4
B_KEY_DIM = 128
B_VAL_DIM = 256
B_KEY_WIDTH = B_HEADS * B_KEY_DIM
B_WIDTH = B_HEADS * B_VAL_DIM
B_ALPHA_RANK = 16
B_TAU = 16.0
B_PROJ = 2 * B_KEY_WIDTH + 2 * B_WIDTH + B_ALPHA_RANK
EVEN_PROJ = A_PROJ + B_PROJ

C_WIDTH = D_MODEL
C_BLOCKS = 8
C_BLOCK = C_WIDTH // C_BLOCKS
C_CONV = 4
C_POW = 8.0

LANES = 128
SUBLANES = 8
VMEM_LIMIT = 60 * 1024 * 1024

CHUNK = 64
RWKV_STEP_CHUNKS = 2
GLA_STEP_CHUNKS = 2
GLA_SUB = SUBLANES
PROMPT_ROW_TILE = 1024
OUTPROJ_ROW_TILE = 512
TOKENWISE_ROW_TILE = 256
FFN_COL_TILE = 512
SAMPLE_FFN_COL_TILE = 1408
EVEN_COL_TILE = 13 * LANES
ODD_COL_TILE = 8 * LANES
STEP_ROWS = 2 * SUBLANES


def _params(*sem):
    return pltpu.CompilerParams(dimension_semantics=sem, vmem_limit_bytes=VMEM_LIMIT)


def _bdot(a, b):
    return jnp.dot(a.astype(BF16), b.astype(BF16), preferred_element_type=F32)


def _split3(x):
    hi = x.astype(BF16)
    r1 = x - hi.astype(F32)
    mid = r1.astype(BF16)
    lo = (r1 - mid.astype(F32)).astype(BF16)
    return hi, mid, lo


def _dot_exact_rhs(a_bf16, x):
    hi, mid, lo = _split3(x)
    f = lambda y: jnp.dot(a_bf16, y, preferred_element_type=F32)
    return f(hi) + f(mid) + f(lo)


def _dot_exact_lhs(x, b_bf16):
    hi, mid, lo = _split3(x)
    f = lambda y: jnp.dot(y, b_bf16, preferred_element_type=F32)
    return f(hi) + f(mid) + f(lo)


def _sigmoid(x):
    return jax.nn.sigmoid(x)


def _softplus(x):
    return jnp.maximum(x, 0.0) + jnp.log1p(jnp.exp(-jnp.abs(x)))


def _gelu_tanh(x):
    return 0.5 * x * (1.0 + jnp.tanh(0.7978845608028654 * (x + 0.044715 * (x * x * x))))


def _norm_mod(x, shift, scale):
    ms = jnp.mean(x * x, axis=-1, keepdims=True)
    return x * lax.rsqrt(ms + NORM_EPS) * (1.0 + scale) + shift


def _ada_kernel(c_ref, w_ref, b_ref, o_ref):
    c = c_ref[...]
    act = (c * _sigmoid(c)).astype(BF16)
    o_ref[...] = jnp.dot(act, w_ref[...].astype(BF16), preferred_element_type=F32) + b_ref[...]


def _ada(c_all, w_ada, b_ada):
    rows = c_all.shape[0]
    tn = 1024
    per = D_MODEL // tn
    b4 = b_ada.reshape(DEPTH, N_MOD * per, 1, tn)
    return pl.pallas_call(
        _ada_kernel,
        grid=(DEPTH, N_MOD * per),
        in_specs=[
            pl.BlockSpec((rows, D_MODEL), lambda l, j: (0, 0)),
            pl.BlockSpec((None, D_MODEL, tn), lambda l, j: (l, 0, j)),
            pl.BlockSpec((None, None, 1, tn), lambda l, j: (l, j, 0, 0)),
        ],
        out_specs=pl.BlockSpec((None, None, rows, tn), lambda l, j: (l, j // per, 0, j % per)),
        out_shape=jax.ShapeDtypeStruct((DEPTH, N_MOD, rows, D_MODEL), F32),
        compiler_params=_params("arbitrary", "arbitrary"),
        name="ada",
    )(c_all, w_ada, b4)


class _Mod:
    def __init__(self, arr, per_row, seq_len):
        self.arr = arr
        self.per_row = per_row
        self.seq_len = seq_len

    def spec(self, layer, m, tm):
        if self.per_row:
            return pl.BlockSpec((None, None, tm, D_MODEL), lambda i, *_: (layer, m, i, 0))
        per_seq = self.seq_len // tm
        return pl.BlockSpec((None, None, None, 1, D_MODEL), lambda i, *_: (layer, m, i // per_seq, 0, 0))


def _ffn_kernel(x_ref, sh_ref, sc_ref, g_ref, wg_ref, wu_ref, wd_ref, *rest, n_j, sub, final, convert):
    rest = list(rest)
    fw_ref = rest.pop(0) if final else None
    f32_tiles = [rest.pop(0) for _ in range(2)] if convert else []
    o_ref = rest.pop(0)
    bf16_tiles = [rest.pop(0) for _ in range(2)] if convert else []
    h_ref, = rest
    j = pl.program_id(1)
    tm = x_ref.shape[0]

    @pl.when(j == 0)
    def _():
        def body(s, carry):
            rows = pl.ds(pl.multiple_of(s * sub, sub), sub)
            sh = sh_ref[...] if sh_ref.shape[0] == 1 else sh_ref[rows, :]
            sc = sc_ref[...] if sc_ref.shape[0] == 1 else sc_ref[rows, :]
            h_ref[rows, :] = _norm_mod(x_ref[rows, :], sh, sc).astype(BF16)
            o_ref[rows, :] = jnp.zeros((sub, D_MODEL), F32)
            return carry
        lax.fori_loop(0, tm // sub, body, 0)

    for src, dst in zip(f32_tiles, bf16_tiles):
        dst[...] = src[...].astype(BF16)

    h = h_ref[...]
    gt = jnp.dot(h, wg_ref[...], preferred_element_type=F32)
    ut = jnp.dot(h, wu_ref[...], preferred_element_type=F32)
    act = (gt * _sigmoid(gt) * ut).astype(BF16)
    o_ref[...] += jnp.dot(act, wd_ref[...], preferred_element_type=F32)

    @pl.when(j == n_j - 1)
    def _():
        def body(s, carry):
            rows = pl.ds(pl.multiple_of(s * sub, sub), sub)
            g = g_ref[...] if g_ref.shape[0] == 1 else g_ref[rows, :]
            y = x_ref[rows, :] + 0.5 * g * o_ref[rows, :]
            if final:
                ms = jnp.mean(y * y, axis=-1, keepdims=True)
                y = y * lax.rsqrt(ms + NORM_EPS) * fw_ref[...]
            o_ref[rows, :] = y
            return carry
        lax.fori_loop(0, tm // sub, body, 0)


def _ffn(x, mod, layer, which, weights, tm, tf, final_w=None, convert=None):
    m_rows = x.shape[0]
    n_i = m_rows // tm
    n_j = D_FF // tf
    m0 = 6 * which
    sub = min(tm, 128)
    final = final_w is not None
    in_specs = [
        pl.BlockSpec((tm, D_MODEL), lambda i, j: (i, 0)),
        mod.spec(layer, m0, tm), mod.spec(layer, m0 + 1, tm), mod.spec(layer, m0 + 2, tm),
        pl.BlockSpec((D_MODEL, tf), lambda i, j: (0, j)),
        pl.BlockSpec((D_MODEL, tf), lambda i, j: (0, j + n_j)),
        pl.BlockSpec((tf, D_MODEL), lambda i, j: (j, 0)),
    ]
    args = [x, mod.arr, mod.arr, mod.arr, weights[0], weights[0], weights[1]]
    if final:
        in_specs.append(pl.BlockSpec((1, D_MODEL), lambda i, j: (0, 0)))
        args.append(final_w.reshape(1, D_MODEL))
    out_specs = [pl.BlockSpec((tm, D_MODEL), lambda i, j: (i, 0))]
    out_shape = [jax.ShapeDtypeStruct((m_rows, D_MODEL), F32)]
    if convert is not None:
        w_up32, w_down32, l2, s2 = convert
        up_tile = (D_MODEL // n_i, 2 * D_FF // n_j)
        down_tile = (D_FF // n_j, D_MODEL // n_i)
        assert up_tile[0] * n_i == D_MODEL and up_tile[1] * n_j == 2 * D_FF and up_tile[1] % LANES == 0
        assert down_tile[0] * n_j == D_FF and down_tile[1] * n_i == D_MODEL and down_tile[1] % LANES == 0
        in_specs += [pl.BlockSpec((None, None) + up_tile, lambda i, j: (l2, s2, i, j)),
                     pl.BlockSpec((None, None) + down_tile, lambda i, j: (l2, s2, j, i))]
        args += [w_up32, w_down32]
        out_specs += [pl.BlockSpec(up_tile, lambda i, j: (i, j)), pl.BlockSpec(down_tile, lambda i, j: (j, i))]
        out_shape += [jax.ShapeDtypeStruct((D_MODEL, 2 * D_FF), BF16), jax.ShapeDtypeStruct((D_FF, D_MODEL), BF16)]
    out = pl.pallas_call(
        functools.partial(_ffn_kernel, n_j=n_j, sub=sub, final=final, convert=convert is not None),
        grid=(n_i, n_j),
        in_specs=in_specs,
        out_specs=out_specs,
        out_shape=out_shape,
        scratch_shapes=[pltpu.VMEM((tm, D_MODEL), BF16)],
        compiler_params=_params("arbitrary", "arbitrary"),
        name="ffn",
    )(*args)
    return (out[0], (out[1], out[2])) if convert is not None else out[0]


def _inproj_kernel(x_ref, sh_ref, sc_ref, w_ref, o_ref, *rest, sub, emit, transposed):
    h_ref = rest[-1]
    j = pl.program_id(1)
    tm = x_ref.shape[0]

    @pl.when(j == 0)
    def _():
        def body(s, carry):
            rows = pl.ds(pl.multiple_of(s * sub, sub), sub)
            sh = sh_ref[...] if sh_ref.shape[0] == 1 else sh_ref[rows, :]
            sc = sc_ref[...] if sc_ref.shape[0] == 1 else sc_ref[rows, :]
            h_ref[rows, :] = _norm_mod(x_ref[rows, :], sh, sc).astype(BF16)
            return carry
        lax.fori_loop(0, tm // sub, body, 0)

    w = w_ref[...]
    if transposed:
        w = w.T
    if emit:
        w = w.astype(BF16)
        rest[0][...] = w
    o_ref[...] = jnp.dot(h_ref[...], w, preferred_element_type=F32)


def _inproj(x, mod, layer, w, widx, tm, tn, transposed=False):
    m_rows = x.shape[0]
    n_out = w.shape[-2] if transposed else w.shape[-1]
    sub = min(tm, 128)
    emit = widx is not None
    if transposed:
        w_spec = pl.BlockSpec((None, tn, D_MODEL), lambda i, j: (widx, j, 0))
    elif emit:
        w_spec = pl.BlockSpec((None, D_MODEL, tn), lambda i, j: (widx, 0, j))
    else:
        w_spec = pl.BlockSpec((D_MODEL, tn), lambda i, j: (0, j))
    out_specs = [pl.BlockSpec((tm, tn), lambda i, j: (i, j))]
    out_shape = [jax.ShapeDtypeStruct((m_rows, n_out), F32)]
    if emit:
        assert m_rows == tm
        out_specs.append(pl.BlockSpec((D_MODEL, tn), lambda i, j: (0, j)))
        out_shape.append(jax.ShapeDtypeStruct((D_MODEL, n_out), BF16))
    out = pl.pallas_call(
        functools.partial(_inproj_kernel, sub=sub, emit=emit, transposed=transposed),
        grid=(m_rows // tm, pl.cdiv(n_out, tn)),
        in_specs=[
            pl.BlockSpec((tm, D_MODEL), lambda i, j: (i, 0)),
            mod.spec(layer, 3, tm), mod.spec(layer, 4, tm),
            w_spec,
        ],
        out_specs=out_specs,
        out_shape=out_shape,
        scratch_shapes=[pltpu.VMEM((tm, D_MODEL), BF16)],
        compiler_params=_params("arbitrary", "arbitrary"),
        name="inproj",
    )(x, mod.arr, mod.arr, w)
    return tuple(out) if emit else out[0]


def _outproj_kernel(*refs, n_in, emit):
    y_refs = refs[:n_in]
    w_refs = refs[n_in:2 * n_in]
    x_ref, g_ref, o_ref = refs[2 * n_in:2 * n_in + 3]
    acc = None
    for k, (y_ref, w_ref) in enumerate(zip(y_refs, w_refs)):
        w = w_ref[...]
        if emit:
            w = w.astype(BF16)
            refs[2 * n_in + 3 + k][...] = w
        part = jnp.dot(y_ref[...].astype(BF16), w, preferred_element_type=F32)
        acc = part if acc is None else acc + part
    o_ref[...] = x_ref[...] + g_ref[...] * acc


def _outproj(ys, w, widx, x, mod, layer, tm):
    m_rows = x.shape[0]
    n_in = len(ys)
    kw = ys[0].shape[1]
    emit = widx is not None
    in_specs = [pl.BlockSpec((tm, kw), lambda i: (i, 0)) for _ in ys]
    if emit:
        assert m_rows == tm
        in_specs += [pl.BlockSpec((None, kw, D_MODEL), lambda i, k=k: (widx, k, 0)) for k in range(n_in)]
        w_args = [w] * n_in
    else:
        in_specs += [pl.BlockSpec((kw, D_MODEL), lambda i: (0, 0))] * n_in
        w_args = list(w)
    in_specs += [pl.BlockSpec((tm, D_MODEL), lambda i: (i, 0)), mod.spec(layer, 5, tm)]
    out_specs = [pl.BlockSpec((tm, D_MODEL), lambda i: (i, 0))]
    out_shape = [jax.ShapeDtypeStruct((m_rows, D_MODEL), F32)]
    if emit:
        out_specs += [pl.BlockSpec((kw, D_MODEL), lambda i: (0, 0))] * n_in
        out_shape += [jax.ShapeDtypeStruct((kw, D_MODEL), BF16)] * n_in
    out = pl.pallas_call(
        functools.partial(_outproj_kernel, n_in=n_in, emit=emit),
        grid=(m_rows // tm,),
        in_specs=in_specs,
        out_specs=out_specs,
        out_shape=out_shape,
        compiler_params=_params("arbitrary"),
        name="outproj",
    )(*ys, *w_args, x, mod.arr)
    return (out[0], tuple(out[1:])) if emit else out[0]


def _rwkv_prep_kernel(u_ref, prev_ref, mu_ref, wup_ref, aup_ref, gup_ref, vec_ref, *rest, seq_mode):
    if seq_mode:
        r_ref, lw_ref, k_ref, v_ref, kk_ref, a_ref, g_ref, carry_ref = rest
    else:
        seg_ref, r_ref, lw_ref, k_ref, v_ref, kk_ref, a_ref, g_ref, bonus_ref = rest
    u = u_ref[...]
    tm = u.shape[0]
    if seq_mode:
        @pl.when(pl.program_id(1) == 0)
        def _():
            carry_ref[...] = prev_ref[...]
        rolled = pltpu.roll(u, 1, axis=0)
        row = lax.broadcasted_iota(jnp.int32, u.shape, 0)
        prev = jnp.where(row == 0, carry_ref[...], rolled)
        carry_ref[...] = u[tm - 1:tm, :]
    else:
        prev = prev_ref[...]
    xs = u + mu_ref[...] * (prev - u)
    r = xs[:, 0:A_WIDTH]
    k = xs[:, A_WIDTH:2 * A_WIDTH]
    v = xs[:, 2 * A_WIDTH:3 * A_WIDTH]
    wa_lo = xs[:, 3 * A_WIDTH:3 * A_WIDTH + LANES]
    g_lo = xs[:, 3 * A_WIDTH + LANES:]
    w0, a0, k_k, k_a, r_k = (vec_ref[i:i + 1, :] for i in range(5))
    w_raw = -_softplus(-(w0 + _bdot(jnp.tanh(wa_lo), wup_ref[...]))) - 0.5
    lw = -jnp.exp(w_raw)
    a = _sigmoid(a0 + _bdot(wa_lo, aup_ref[...]))
    g = _bdot(_sigmoid(g_lo), gup_ref[...])
    kk = k * k_k
    k2 = k * (1.0 + (a - 1.0) * k_a)
    if not seq_mode:
        seg = seg_ref[...]
        kk = kk * lax.rsqrt(jnp.maximum(_dot_exact_lhs(kk * kk, seg), 1e-24))
        bonus_ref[...] = _dot_exact_lhs(r * k2 * r_k, seg) * v
    r_ref[...] = r
    lw_ref[...] = lw
    k_ref[...] = k2
    v_ref[...] = v
    kk_ref[...] = kk
    a_ref[...] = a
    g_ref[...] = g


def _rwkv_prep(u, prev, mu, wup_pad, aup_pad, g_up, vec, seg, n_seq, tm, seq_mode):
    rows = u.shape[0]
    per_seq = rows // n_seq // tm if seq_mode else 1
    grid = (n_seq, per_seq) if seq_mode else (rows // tm, 1)
    rowmap = (lambda b, t: (b * per_seq + t, 0)) if seq_mode else (lambda b, t: (b, 0))
    prev_spec = (pl.BlockSpec((None, 1, A_PROJ), lambda b, t: (b, 0, 0)) if seq_mode
                 else pl.BlockSpec((tm, A_PROJ), rowmap))
    const = lambda shape: pl.BlockSpec(shape, lambda b, t: (0,) * len(shape))
    out = jax.ShapeDtypeStruct((rows, A_WIDTH), F32)
    in_specs = [pl.BlockSpec((tm, A_PROJ), rowmap), prev_spec,
                const((1, A_PROJ)), const((LANES, A_WIDTH)), const((LANES, A_WIDTH)),
                const((A_GATE_RANK, A_WIDTH)), const((8, A_WIDTH))]
    args = [u, prev, mu, wup_pad, aup_pad, g_up, vec]
    if not seq_mode:
        in_specs.append(const((A_WIDTH, A_WIDTH)))
        args.append(seg)
    n_out = 7 if seq_mode else 8
    return pl.pallas_call(
        functools.partial(_rwkv_prep_kernel, seq_mode=seq_mode),
        grid=grid,
        in_specs=in_specs,
        out_specs=[pl.BlockSpec((tm, A_WIDTH), rowmap)] * n_out,
        out_shape=[out] * n_out,
        scratch_shapes=[pltpu.VMEM((1, A_PROJ), F32)] if seq_mode else [],
        compiler_params=_params("arbitrary", "arbitrary"),
        name="rwkv_prep",
    )(*args)


def _rwkv_post_kernel(y_ref, bonus_ref, g_ref, vec_ref, seg_ref, o_ref):
    y = y_ref[...]
    seg = seg_ref[...]
    inv_n = 1.0 / A_HEAD_DIM
    yc = y - _dot_exact_lhs(y, seg) * inv_n
    var = _dot_exact_lhs(yc * yc, seg) * inv_n
    yn = yc * lax.rsqrt(var + A_GN_EPS) * vec_ref[5:6, :] + vec_ref[6:7, :]
    o_ref[...] = (yn + bonus_ref[...]) * g_ref[...]


def _rwkv_post(y, bonus, g, vec, seg, tm):
    rows = y.shape[0]
    spec = pl.BlockSpec((tm, A_WIDTH), lambda i: (i, 0))
    return pl.pallas_call(
        _rwkv_post_kernel,
        grid=(rows // tm,),
        in_specs=[spec, spec, spec,
                  pl.BlockSpec((8, A_WIDTH), lambda i: (0, 0)),
                  pl.BlockSpec((A_WIDTH, A_WIDTH), lambda i: (0, 0))],
        out_specs=spec,
        out_shape=jax.ShapeDtypeStruct((rows, A_WIDTH), F32),
        compiler_params=_params("arbitrary"),
        name="rwkv_post",
    )(y, bonus, g, vec, seg)


def _rwkv_chunk_kernel(r_ref, lw_ref, k_ref, v_ref, kk_ref, a_ref, g_ref, vec_ref, h0_ref, y_ref, hT_ref, h_ref):
    ci = pl.program_id(1)

    @pl.when(ci == 0)
    def _():
        h_ref[...] = h0_ref[...]

    cs = CHUNK
    n2 = 2 * cs
    npair = A_WIDTH // LANES
    pairs = lambda x: jnp.stack([x[:, p * LANES:(p + 1) * LANES] for p in range(npair)])
    vec = vec_ref[...]
    vrow = lambda i: pairs(vec[i:i + 1, :])
    lane = lax.broadcasted_iota(jnp.int32, (1, 1, LANES), 2)
    m0 = (lane < A_HEAD_DIM).astype(F32)
    m1 = 1.0 - m0
    split = lambda x: jnp.concatenate([x * m0, x * m1], axis=1)
    dup = lambda x: jnp.concatenate([x, x], axis=1)
    own = jnp.concatenate([jnp.broadcast_to(m0, (1, cs, LANES)), jnp.broadcast_to(m1, (1, cs, LANES))], axis=1)
    bf = lambda x: x.astype(BF16)

    def bmm(x, y):
        return jnp.einsum('hab,hbc->hac', x, y, preferred_element_type=F32)

    def bmm_nt(x, y):
        return jnp.einsum('han,hbn->hab', x, y, preferred_element_type=F32)

    def bmm_tn(x, y):
        return jnp.einsum('hca,hcb->hab', x, y, preferred_element_type=F32)

    row = lax.broadcasted_iota(jnp.int32, (cs, cs), 0)
    col = lax.broadcasted_iota(jnp.int32, (cs, cs), 1)
    tri = (col <= row).astype(BF16)
    row2 = lax.broadcasted_iota(jnp.int32, (n2, n2), 0)
    col2 = lax.broadcasted_iota(jnp.int32, (n2, n2), 1)
    same = (row2 >= cs) == (col2 >= cs)
    tok_r = jnp.where(row2 >= cs, row2 - cs, row2)
    tok_c = jnp.where(col2 >= cs, col2 - cs, col2)
    strict = jnp.where(same, (tok_c < tok_r).astype(F32), 0.0)
    incl = jnp.where(same, (tok_c <= tok_r).astype(F32), 0.0)
    eye2 = (row2 == col2).astype(F32)
    nrow = lax.broadcasted_iota(jnp.int32, (LANES, LANES), 0)
    ncol = lax.broadcasted_iota(jnp.int32, (LANES, LANES), 1)
    eye_n = (nrow == ncol).astype(F32)

    h = h_ref[...]
    for sub in range(r_ref.shape[0] // cs):
        rows = slice(sub * cs, (sub + 1) * cs)
        lw_all = lw_ref[rows, :]
        c = pairs(_dot_exact_rhs(tri, lw_all))
        lw = pairs(lw_all)
        c_end = c[:, cs - 1:cs, :]
        e_pos = dup(jnp.exp(c))
        e_neg = dup(jnp.exp(-c))
        e_prev = dup(jnp.exp(c - lw))
        e_end = dup(jnp.exp(c_end - c))
        g_end = jnp.exp(c_end)

        kk = split(pairs(kk_ref[rows, :]))
        kk = kk * lax.rsqrt(jnp.maximum(jnp.sum(kk * kk, axis=-1, keepdims=True), 1e-24))
        b_vec = kk * dup(pairs(a_ref[rows, :]))
        r2 = split(pairs(r_ref[rows, :]))
        k2 = split(pairs(k_ref[rows, :]))
        v2 = split(pairs(v_ref[rows, :]))
        bonus = jnp.sum(r2 * k2 * vrow(4), axis=-1, keepdims=True) * v2
        rt = r2 * e_pos
        at = bf(-kk * e_prev)
        bt = bf(b_vec * e_neg)
        kt = bf(k2 * e_neg)
        bh = bf(b_vec * e_end)
        kh = bf(k2 * e_end)
        v2b = bf(v2)

        gram = bmm_nt(jnp.concatenate([at, bf(rt)], axis=1), jnp.concatenate([bt, kt], axis=1))
        a_ab = gram[:, :n2, :n2] * strict
        a_ak = gram[:, :n2, n2:] * strict
        m_b = gram[:, n2:, :n2] * incl
        m_k = gram[:, n2:, n2:] * incl

        tinv = eye2 + a_ab
        a_b = bf(a_ab)
        apow = bmm(a_b, a_b)
        span = 2
        while 2 * span < cs:
            ap_b = bf(apow)
            both = bmm(jnp.concatenate([bf(tinv), ap_b], axis=1), ap_b)
            tinv = tinv + both[:, :n2]
            apow = both[:, n2:]
            span *= 2
        tinv = tinv + bmm(bf(tinv), bf(apow))

        akv = bmm(bf(a_ak), v2b)
        pq = bf(bmm(bf(tinv), jnp.concatenate([at, bf(akv)], axis=-1)))
        mpq = bmm(bf(m_b), pq)
        p2 = rt + mpq[..., :LANES]
        y0 = mpq[..., LANES:] + bmm(bf(m_k), v2b)
        tpq = bmm_tn(bh, pq)
        gm = eye_n * g_end + tpq[..., :LANES]
        hadd = tpq[..., LANES:] + bmm_tn(kh, v2b)

        yh = bmm(jnp.concatenate([bf(p2), bf(gm)], axis=1), bf(h))
        y2 = yh[:, :n2] + y0
        h = yh[:, n2:] + hadd

        inv_n = 1.0 / A_HEAD_DIM
        yc = (y2 - jnp.sum(y2, axis=-1, keepdims=True) * inv_n) * own
        var = jnp.sum(yc * yc, axis=-1, keepdims=True) * inv_n
        z = yc * lax.rsqrt(var + A_GN_EPS) * vrow(5) + vrow(6) * own + bonus
        out = (z[:, :cs] + z[:, cs:]) * pairs(g_ref[rows, :])
        for p in range(npair):
            y_ref[rows, p * LANES:(p + 1) * LANES] = out[p]

    h_ref[...] = h

    @pl.when(ci == pl.num_programs(1) - 1)
    def _():
        hT_ref[...] = h


def _rwkv_chunk(r, lw, k, v, kk, a, g, vec, n_seq, h0):
    rows = r.shape[0]
    step_rows = RWKV_STEP_CHUNKS * CHUNK
    nc = rows // n_seq // step_rows
    npair = A_WIDTH // LANES
    spec = pl.BlockSpec((step_rows, A_WIDTH), lambda b, c: (b * nc + c, 0))
    hspec = pl.BlockSpec((None, npair, LANES, LANES), lambda b, c: (b, 0, 0, 0))
    return pl.pallas_call(
        _rwkv_chunk_kernel,
        grid=(n_seq, nc),
        in_specs=[spec] * 7 + [pl.BlockSpec((8, A_WIDTH), lambda b, c: (0, 0)), hspec],
        out_specs=[spec, hspec],
        out_shape=[jax.ShapeDtypeStruct((rows, A_WIDTH), F32),
                   jax.ShapeDtypeStruct((n_seq, npair, LANES, LANES), F32)],
        scratch_shapes=[pltpu.VMEM((npair, LANES, LANES), F32)],
        compiler_params=_params("arbitrary", "arbitrary"),
        name="rwkv_chunk",
    )(r, lw, k, v, kk, a, g, vec, h0)


def _pair_states(s):
    n = s.shape[0]
    nd = A_HEAD_DIM
    h = jnp.swapaxes(s, -1, -2).reshape(n, A_HEADS // 2, 2, nd, nd)
    out = jnp.zeros((n, A_HEADS // 2, 2 * nd, 2 * nd), s.dtype)
    return out.at[:, :, :nd, :nd].set(h[:, :, 0]).at[:, :, nd:, nd:].set(h[:, :, 1])


def _unpair_states(hp):
    n = hp.shape[0]
    nd = A_HEAD_DIM
    h = jnp.stack([hp[:, :, :nd, :nd], hp[:, :, nd:, nd:]], axis=2).reshape(n, A_HEADS, nd, nd)
    return jnp.swapaxes(h, -1, -2)


def _rwkv_step_kernel(r_ref, lw_ref, k_ref, v_ref, kk_ref, a_ref, s_ref, acc_ref, y_ref, so_ref,
                      r_t, w_t, k_t, v_t, nkk_t, b_t, y_t):
    h = pl.program_id(0)
    nd = A_HEAD_DIM

    @pl.when(h == 0)
    def _():
        kk = kk_ref[...]
        r_t[...] = r_ref[...].T
        w_t[...] = jnp.exp(lw_ref[...]).T
        k_t[...] = k_ref[...].T
        v_t[...] = v_ref[...].T
        nkk_t[...] = (-kk).T
        b_t[...] = (kk * a_ref[...]).T

    first = pl.multiple_of(h * nd, nd)
    rows = pl.ds(first, nd)
    r, w, k, nkk, b = r_t[rows, :], w_t[rows, :], k_t[rows, :], nkk_t[rows, :], b_t[rows, :]

    def body(i, carry):
        s = s_ref[i]
        s_kk = jnp.sum(s * nkk, axis=0, keepdims=True)
        s_new = s * w + s_kk * b + v_t[pl.ds(first + i, 1), :] * k
        so_ref[i] = s_new
        y_t[pl.ds(first + i, 1), :] = jnp.sum(s_new * r, axis=0, keepdims=True)
        return carry

    lax.fori_loop(0, nd, body, 0, unroll=4)

    @pl.when(h == pl.num_programs(0) - 1)
    def _():
        y_ref[...] = y_t[...].T


def _rwkv_step(r, lw, k, v, kk, a, states_t, j, new_states):
    n = r.shape[0]
    nd = A_HEAD_DIM
    rspec = pl.BlockSpec((n, A_WIDTH), lambda h: (0, 0))
    sspec = pl.BlockSpec((None, None, nd, nd, n), lambda h: (j, h, 0, 0, 0))
    return pl.pallas_call(
        _rwkv_step_kernel,
        grid=(A_HEADS,),
        in_specs=[rspec] * 6 + [sspec, pl.BlockSpec(memory_space=pl.ANY)],
        out_specs=[rspec, sspec],
        out_shape=[jax.ShapeDtypeStruct((n, A_WIDTH), F32), jax.ShapeDtypeStruct(states_t.shape, F32)],
        input_output_aliases={7: 1},
        scratch_shapes=[pltpu.VMEM((A_WIDTH, n), F32)] * 7,
        compiler_params=_params("arbitrary"),
        name="rwkv_step",
    )(r, lw, k, v, kk, a, states_t, new_states)


def _gla_log_alpha(a128, aup_ref, ab_ref):
    lane = lax.broadcasted_iota(jnp.int32, a128.shape, 1)
    a_lo = jnp.where(lane < B_ALPHA_RANK, a128, 0.0)
    x = _bdot(a_lo, aup_ref[...]) + ab_ref[...]
    return -_softplus(-x) * (1.0 / B_TAU)


def _gla_block(u_ref, aup_ref, ab_ref, nw_ref, o_ref, s_ref):
    ub = u_ref[...]
    cs = ub.shape[0]
    q_all = ub[:, 0:B_KEY_WIDTH] * (B_KEY_DIM ** -0.5)
    k_all = ub[:, B_KEY_WIDTH:2 * B_KEY_WIDTH]
    v_all = ub[:, 2 * B_KEY_WIDTH:2 * B_KEY_WIDTH + B_WIDTH]
    g_all = ub[:, 2 * B_KEY_WIDTH + B_WIDTH:2 * B_KEY_WIDTH + 2 * B_WIDTH]
    a128 = ub[:, 2 * B_KEY_WIDTH + 2 * B_WIDTH:2 * B_KEY_WIDTH + 2 * B_WIDTH + LANES]
    log_a = _gla_log_alpha(a128, aup_ref, ab_ref)

    row = lax.broadcasted_iota(jnp.int32, (cs, cs), 0)
    col = lax.broadcasted_iota(jnp.int32, (cs, cs), 1)
    cum = _dot_exact_rhs((col <= row).astype(BF16), log_a)
    e_pos = jnp.exp(cum)
    cum_end = cum[cs - 1:cs, :]
    e_end = jnp.exp(cum_end - cum)
    g_end = jnp.exp(cum_end)
    nk = B_KEY_DIM
    eye_k = (lax.broadcasted_iota(jnp.int32, (nk, nk), 0) == lax.broadcasted_iota(jnp.int32, (nk, nk), 1)).astype(F32)

    nblk = cs // GLA_SUB
    kw = B_KEY_WIDTH
    cum3 = cum.reshape(nblk, GLA_SUB, kw)
    before = jnp.concatenate([jnp.zeros((1, 1, kw), F32), cum3[:nblk - 1, GLA_SUB - 1:GLA_SUB, :]], axis=0)
    q_rel = q_all * jnp.exp(cum - jnp.broadcast_to(before, (nblk, GLA_SUB, kw)).reshape(cs, kw))
    tok = lax.broadcasted_iota(jnp.int32, (cs, 1), 0)
    q_parts, k_parts = [], []
    for blk in range(1, nblk):
        first = blk * GLA_SUB
        k_parts.append(jnp.where(tok < first, k_all * jnp.exp(jnp.minimum(before[blk] - cum, 0.0)), 0.0))
        q_parts.append(jnp.where((tok >= first) & (tok < first + GLA_SUB), q_rel, 0.0))

    q3 = q_all.reshape(nblk, GLA_SUB, kw)
    k3 = k_all.reshape(nblk, GLA_SUB, kw)
    v3 = v_all.reshape(nblk, GLA_SUB, B_WIDTH)
    sub_i = lax.broadcasted_iota(jnp.int32, (1, GLA_SUB, 1), 1)
    o_diag = [jnp.zeros((nblk, GLA_SUB, B_VAL_DIM), F32) for _ in range(B_HEADS)]
    for jj in range(GLA_SUB):
        pair = q3 * k3[:, jj:jj + 1, :] * jnp.exp(jnp.minimum(cum3 - cum3[:, jj:jj + 1, :], 0.0))
        pair = jnp.where(sub_i >= jj, pair, 0.0)
        for h in range(B_HEADS):
            score = jnp.sum(pair[:, :, h * nk:(h + 1) * nk], axis=-1, keepdims=True)
            o_diag[h] = o_diag[h] + score * v3[:, jj:jj + 1, h * B_VAL_DIM:(h + 1) * B_VAL_DIM]

    for h in range(B_HEADS):
        ks = slice(h * B_KEY_DIM, (h + 1) * B_KEY_DIM)
        vs = slice(h * B_VAL_DIM, (h + 1) * B_VAL_DIM)
        qe = q_all[:, ks] * e_pos[:, ks]
        ke = k_all[:, ks] * e_end[:, ks]
        vh = v_all[:, vs]
        s = s_ref[h]
        q_cat = jnp.concatenate([part[:, ks] for part in q_parts], axis=-1).astype(BF16)
        k_cat = jnp.concatenate([part[:, ks] for part in k_parts], axis=-1).astype(BF16)
        scores = jnp.einsum('id,jd->ij', q_cat, k_cat, preferred_element_type=F32)
        o = _bdot(scores, vh) + _bdot(qe, s) + o_diag[h].reshape(cs, B_VAL_DIM)
        g_col = jnp.sum(eye_k * g_end[:, ks], axis=-1, keepdims=True)
        s_ref[h] = s * g_col + jnp.einsum('jd,jv->dv', ke.astype(BF16), vh.astype(BF16),
                                           preferred_element_type=F32)
        ms = jnp.mean(o * o, axis=-1, keepdims=True)
        gh = g_all[:, vs]
        o_ref[:, vs] = o * lax.rsqrt(ms + NORM_EPS) * nw_ref[:, vs] * (gh * _sigmoid(gh))


def _gla_chunk_kernel(u_ref, aup_ref, ab_ref, nw_ref, s0_ref, o_ref, sT_ref, s_ref):
    ci = pl.program_id(1)

    @pl.when(ci == 0)
    def _():
        s_ref[...] = s0_ref[...]

    for sub in range(u_ref.shape[0] // CHUNK):
        rows = pl.ds(sub * CHUNK, CHUNK)
        _gla_block(u_ref.at[rows, :], aup_ref, ab_ref, nw_ref, o_ref.at[rows, :], s_ref)

    @pl.when(ci == pl.num_programs(1) - 1)
    def _():
        sT_ref[...] = s_ref[...]


def _gla_chunk(u, n_seq, alpha_up_pad, alpha_b, norm_w, s0):
    rows = u.shape[0]
    step_rows = GLA_STEP_CHUNKS * CHUNK
    nc = rows // n_seq // step_rows
    return pl.pallas_call(
        _gla_chunk_kernel,
        grid=(n_seq, nc),
        in_specs=[
            pl.BlockSpec((step_rows, A_PROJ), lambda b, c: (b * nc + c, 1)),
            pl.BlockSpec((LANES, B_KEY_WIDTH), lambda b, c: (0, 0)),
            pl.BlockSpec((1, B_KEY_WIDTH), lambda b, c: (0, 0)),
            pl.BlockSpec((1, B_WIDTH), lambda b, c: (0, 0)),
            pl.BlockSpec((None, B_HEADS, B_KEY_DIM, B_VAL_DIM), lambda b, c: (b, 0, 0, 0)),
        ],
        out_specs=[
            pl.BlockSpec((step_rows, B_WIDTH), lambda b, c: (b * nc + c, 0)),
            pl.BlockSpec((None, B_HEADS, B_KEY_DIM, B_VAL_DIM), lambda b, c: (b, 0, 0, 0)),
        ],
        out_shape=[jax.ShapeDtypeStruct((rows, B_WIDTH), F32),
                   jax.ShapeDtypeStruct((n_seq, B_HEADS, B_KEY_DIM, B_VAL_DIM), F32)],
        scratch_shapes=[pltpu.VMEM((B_HEADS, B_KEY_DIM, B_VAL_DIM), F32)],
        compiler_params=_params("arbitrary", "arbitrary"),
        name="gla_chunk",
    )(u, alpha_up_pad, alpha_b, norm_w, s0)


def _gla_step_kernel(u_ref, aup_ref, ab_ref, nw_ref, s_ref, acc_ref, o_ref, so_ref, la_ref):
    nb = u_ref.shape[0]
    nk = B_KEY_DIM
    off_k, off_v, off_g = B_KEY_WIDTH, 2 * B_KEY_WIDTH, 2 * B_KEY_WIDTH + B_WIDTH
    off_a = 2 * B_KEY_WIDTH + 2 * B_WIDTH
    la_ref[...] = _gla_log_alpha(u_ref[:, off_a:off_a + LANES], aup_ref, ab_ref)
    eye = (lax.broadcasted_iota(jnp.int32, (nk, nk), 0) == lax.broadcasted_iota(jnp.int32, (nk, nk), 1)).astype(F32)
    col = lambda x: jnp.sum(eye * x, axis=-1, keepdims=True)

    def body(n, carry):
        row = pl.ds(n, 1)
        ur = u_ref[row, :]
        dec_all = jnp.exp(la_ref[row, :])
        outs = []
        for h in range(B_HEADS):
            q = ur[:, h * nk:(h + 1) * nk] * (B_KEY_DIM ** -0.5)
            k = ur[:, off_k + h * nk:off_k + (h + 1) * nk]
            v = ur[:, off_v + h * B_VAL_DIM:off_v + (h + 1) * B_VAL_DIM]
            g = ur[:, off_g + h * B_VAL_DIM:off_g + (h + 1) * B_VAL_DIM]
            dec = dec_all[:, h * nk:(h + 1) * nk]
            s = s_ref[n, h]
            qk = jnp.sum(q * k, axis=-1, keepdims=True)
            o = qk * v + jnp.sum(col(q * dec) * s, axis=0, keepdims=True)
            so_ref[n, h] = s * col(dec) + col(k) * v
            ms = jnp.mean(o * o, axis=-1, keepdims=True)
            outs.append(o * lax.rsqrt(ms + NORM_EPS) * nw_ref[:, h * B_VAL_DIM:(h + 1) * B_VAL_DIM]
                        * (g * _sigmoid(g)))
        o_ref[row, :] = jnp.concatenate(outs, axis=-1)
        return carry

    lax.fori_loop(0, nb, body, 0)


def _gla_step(u, alpha_up_pad, alpha_b, norm_w, states, j, nb, new_states):
    n = u.shape[0]
    sspec = pl.BlockSpec((None, nb, B_HEADS, B_KEY_DIM, B_VAL_DIM), lambda i: (j, i, 0, 0, 0))
    return pl.pallas_call(
        _gla_step_kernel,
        grid=(n // nb,),
        in_specs=[pl.BlockSpec((nb, A_PROJ), lambda i: (i, 1)),
                  pl.BlockSpec((LANES, B_KEY_WIDTH), lambda i: (0, 0)),
                  pl.BlockSpec((1, B_KEY_WIDTH), lambda i: (0, 0)),
                  pl.BlockSpec((1, B_WIDTH), lambda i: (0, 0)),
                  sspec, pl.BlockSpec(memory_space=pl.ANY)],
        out_specs=[pl.BlockSpec((nb, B_WIDTH), lambda i: (i, 0)), sspec],
        out_shape=[jax.ShapeDtypeStruct((n, B_WIDTH), F32), jax.ShapeDtypeStruct(states.shape, F32)],
        input_output_aliases={5: 1},
        scratch_shapes=[pltpu.VMEM((nb, B_KEY_WIDTH), F32)],
        compiler_params=_params("arbitrary"),
        name="gla_step",
    )(u, alpha_up_pad, alpha_b, norm_w, states, new_states)


def _lru_gates(xc, wa_ref, wx_ref, ba_ref, bx_ref, lam_ref):
    ra, ix = [], []
    for nb in range(C_BLOCKS):
        xb = xc[:, nb * C_BLOCK:(nb + 1) * C_BLOCK].astype(BF16)
        ra.append(jnp.dot(xb, wa_ref[nb].astype(BF16), preferred_element_type=F32))
        ix.append(jnp.dot(xb, wx_ref[nb].astype(BF16), preferred_element_type=F32))
    r = _sigmoid(jnp.concatenate(ra, axis=-1) + ba_ref[...])
    i_g = _sigmoid(jnp.concatenate(ix, axis=-1) + bx_ref[...])
    log_a = -C_POW * r * _softplus(-lam_ref[...])
    a = jnp.exp(log_a)
    b = jnp.sqrt(1.0 - a * a) * (i_g * xc)
    return a, b


def _lru_seq_kernel(u_ref, cp_ref, h0_ref, cw_ref, cb_ref, wa_ref, wx_ref, ba_ref, bx_ref, lam_ref,
                    y_ref, tail_ref, hT_ref, carry_ref, h_ref, a_s, b_s):
    ti = pl.program_id(1)

    @pl.when(ti == 0)
    def _():
        carry_ref[...] = cp_ref[...]
        h_ref[...] = h0_ref[...]

    gate = u_ref[:, 0:C_WIDTH]
    xb = u_ref[:, C_WIDTH:2 * C_WIDTH]
    tm = xb.shape[0]
    carry = carry_ref[...]
    row8 = lax.broadcasted_iota(jnp.int32, (8, C_WIDTH), 0)
    xc = cb_ref[...] + cw_ref[C_CONV - 1:C_CONV, :] * xb
    for s in range(1, C_CONV):
        rolled = pltpu.roll(xb, s, axis=0)
        head = jnp.where(row8 < s, pltpu.roll(carry, s, axis=0), rolled[0:8, :])
        shifted = jnp.concatenate([head, rolled[8:, :]], axis=0)
        xc = xc + cw_ref[C_CONV - 1 - s:C_CONV - s, :] * shifted
    carry_ref[...] = xb[tm - 8:tm, :]
    tail_ref[...] = xb[tm - 8:tm, :]

    a, b = _lru_gates(xc, wa_ref, wx_ref, ba_ref, bx_ref, lam_ref)
    a_s[...] = a
    b_s[...] = b

    def body(t, h):
        h = a_s[pl.ds(t, 1), :] * h + b_s[pl.ds(t, 1), :]
        b_s[pl.ds(t, 1), :] = h
        return h

    h_last = lax.fori_loop(0, tm, body, h_ref[...], unroll=8)
    h_ref[...] = h_last
    hT_ref[...] = h_last
    y_ref[...] = _gelu_tanh(gate) * b_s[...]


def _lru_seq(u, n_seq, conv_prev8, h0, conv_w, conv_b, wa, wx, ba, bx, lam, tm):
    rows = u.shape[0]
    per_seq = rows // n_seq // tm
    const = lambda shape: pl.BlockSpec(shape, lambda b, t: (0,) * len(shape))
    return pl.pallas_call(
        _lru_seq_kernel,
        grid=(n_seq, per_seq),
        in_specs=[
            pl.BlockSpec((tm, 2 * C_WIDTH), lambda b, t: (b * per_seq + t, 0)),
            pl.BlockSpec((None, 8, C_WIDTH), lambda b, t: (b, 0, 0)),
            pl.BlockSpec((None, 1, C_WIDTH), lambda b, t: (b, 0, 0)),
            const((C_CONV, C_WIDTH)), const((1, C_WIDTH)),
            const((C_BLOCKS, C_BLOCK, C_BLOCK)), const((C_BLOCKS, C_BLOCK, C_BLOCK)),
            const((1, C_WIDTH)), const((1, C_WIDTH)), const((1, C_WIDTH)),
        ],
        out_specs=[
            pl.BlockSpec((tm, C_WIDTH), lambda b, t: (b * per_seq + t, 0)),
            pl.BlockSpec((None, 8, C_WIDTH), lambda b, t: (b, 0, 0)),
            pl.BlockSpec((None, 1, C_WIDTH), lambda b, t: (b, 0, 0)),
        ],
        out_shape=[jax.ShapeDtypeStruct((rows, C_WIDTH), F32),
                   jax.ShapeDtypeStruct((n_seq, 8, C_WIDTH), F32),
                   jax.ShapeDtypeStruct((n_seq, 1, C_WIDTH), F32)],
        scratch_shapes=[pltpu.VMEM((8, C_WIDTH), F32), pltpu.VMEM((1, C_WIDTH), F32),
                        pltpu.VMEM((tm, C_WIDTH), F32), pltpu.VMEM((tm, C_WIDTH), F32)],
        compiler_params=_params("arbitrary", "arbitrary"),
        name="lru_seq",
    )(u, conv_prev8, h0, conv_w, conv_b, wa, wx, ba, bx, lam)


def _lru_step_kernel(u_ref, cp_ref, h0_ref, cw_ref, cb_ref, wa_ref, wx_ref, ba_ref, bx_ref, lam_ref,
                     y_ref, cn_ref, h_ref):
    gate = u_ref[:, 0:C_WIDTH]
    xb = u_ref[:, C_WIDTH:2 * C_WIDTH]
    xc = cb_ref[...] + cw_ref[C_CONV - 1:C_CONV, :] * xb
    for i in range(C_CONV - 1):
        xc = xc + cw_ref[i:i + 1, :] * cp_ref[i]
    a, b = _lru_gates(xc, wa_ref, wx_ref, ba_ref, bx_ref, lam_ref)
    h = a * h0_ref[...] + b
    h_ref[...] = h
    y_ref[...] = _gelu_tanh(gate) * h
    for i in range(C_CONV - 2):
        cn_ref[i] = cp_ref[i + 1]
    cn_ref[C_CONV - 2] = xb


def _lru_step(u, conv_states, h0, j, conv_w, conv_b, wa, wx, ba, bx, lam):
    rows = u.shape[0]
    full = lambda shape: pl.BlockSpec(shape, lambda i: (0,) * len(shape))
    return pl.pallas_call(
        _lru_step_kernel,
        grid=(1,),
        in_specs=[full((rows, 2 * C_WIDTH)),
                  pl.BlockSpec((None, C_CONV - 1, rows, C_WIDTH), lambda i: (j, 0, 0, 0)),
                  full((rows, C_WIDTH)),
                  full((C_CONV, C_WIDTH)), full((1, C_WIDTH)),
                  full((C_BLOCKS, C_BLOCK, C_BLOCK)), full((C_BLOCKS, C_BLOCK, C_BLOCK)),
                  full((1, C_WIDTH)), full((1, C_WIDTH)), full((1, C_WIDTH))],
        out_specs=[full((rows, C_WIDTH)), full((C_CONV - 1, rows, C_WIDTH)), full((rows, C_WIDTH))],
        out_shape=[jax.ShapeDtypeStruct((rows, C_WIDTH), F32),
                   jax.ShapeDtypeStruct((C_CONV - 1, rows, C_WIDTH), F32),
                   jax.ShapeDtypeStruct((rows, C_WIDTH), F32)],
        compiler_params=_params("arbitrary"),
        name="lru_step",
    )(u, conv_states, h0, conv_w, conv_b, wa, wx, ba, bx, lam)


def _pad_rows(w, first, total):
    return jnp.zeros((total, w.shape[1]), w.dtype).at[first:first + w.shape[0]].set(w)


class _Group:
    def __init__(self, x, mod, n_seq, seq_len, states, tm, tf):
        self.x, self.mod, self.n_seq, self.seq_len, self.states, self.tm, self.tf = x, mod, n_seq, seq_len, states, tm, tf
        self.prompt = seq_len > 1
        self.out_shift, self.out_wkv, self.out_gla, self.out_conv, self.out_lru = [], [], [], [], []
        self.new_wkv_t = self.new_gla = None

    def new_states(self):
        wkv = jnp.stack(self.out_wkv) if self.new_wkv_t is None else jnp.transpose(self.new_wkv_t, (0, 4, 1, 2, 3))
        gla = jnp.stack(self.out_gla) if self.new_gla is None else self.new_gla
        return jnp.stack(self.out_shift), wkv, gla, jnp.stack(self.out_conv), jnp.stack(self.out_lru)


def _mixer(grp, l, wts, mxu):
    (w_ffn_up, w_ffn_down, w_in_even, w_out_even, a_mu, a_w_up, a_a_up, a_g_up, a_vec, b_alpha_up,
     b_alpha_b, b_norm_w, w_in_odd, w_out_odd, c_conv_w, c_conv_b, c_wa, c_ba, c_wx, c_bx, c_lam,
     final_norm_w) = wts
    x, mod, n_seq, seq_len, tm, prompt = grp.x, grp.mod, grp.n_seq, grp.seq_len, grp.tm, grp.prompt
    st_shift, st_wkv, st_gla, st_conv, st_lru = grp.states
    rows = x.shape[0]
    j = l // 2
    tn_even, tn_odd = EVEN_COL_TILE, ODD_COL_TILE
    seg = jnp.kron(jnp.eye(A_HEADS, dtype=F32), jnp.ones((A_HEAD_DIM, A_HEAD_DIM), F32)).astype(BF16)
    if l % 2 == 0:
        if prompt:
            u = _inproj(x, mod, l, mxu['in', l], None, tm, tn_even)
        else:
            u, mxu['in', l] = _inproj(x, mod, l, jnp.swapaxes(w_in_even, 1, 2), j, tm, tn_even,
                                      transposed=True)
        wup_pad = _pad_rows(a_w_up[j], 0, LANES)
        aup_pad = _pad_rows(a_a_up[j], A_DECAY_RANK, LANES)
        vec8 = _pad_rows(a_vec[j], 0, SUBLANES)
        alpha_pad = _pad_rows(b_alpha_up[j], 0, LANES)
        alpha_b = b_alpha_b[j].reshape(1, B_KEY_WIDTH)
        mu = a_mu[j].reshape(1, A_PROJ)
        norm_w = b_norm_w[j].reshape(1, B_WIDTH)
        if prompt:
            r, lw, k2, v, kk, al, g = _rwkv_prep(u, st_shift[j].reshape(n_seq, 1, A_PROJ), mu, wup_pad,
                                                 aup_pad, a_g_up[j], vec8, seg, n_seq, TOKENWISE_ROW_TILE, True)
            y_a, h_t = _rwkv_chunk(r, lw, k2, v, kk, al, g, vec8, n_seq, _pair_states(st_wkv[j]))
            s_wkv = _unpair_states(h_t)
            s_shift = u.reshape(n_seq, seq_len, EVEN_PROJ)[:, -1, :A_PROJ]
            y_b, s_gla = _gla_chunk(u, n_seq, alpha_pad, alpha_b, norm_w, st_gla[j])
        else:
            r, lw, k2, v, kk, al, g, bonus = _rwkv_prep(u, st_shift[j], mu, wup_pad, aup_pad, a_g_up[j],
                                                        vec8, seg, rows, rows, False)
            st_wkv_t = jnp.transpose(st_wkv, (0, 2, 3, 4, 1))
            y_raw, grp.new_wkv_t = _rwkv_step(r, lw, k2, v, kk, al, st_wkv_t, j,
                                              jnp.zeros_like(st_wkv_t) if grp.new_wkv_t is None else grp.new_wkv_t)
            y_a = _rwkv_post(y_raw, bonus, g, vec8, seg, rows)
            s_shift = u[:, :A_PROJ]
            y_b, grp.new_gla = _gla_step(u, alpha_pad, alpha_b, norm_w, st_gla, j, STEP_ROWS,
                                         jnp.zeros_like(st_gla) if grp.new_gla is None else grp.new_gla)
        if prompt:
            x = _outproj([y_a, y_b], mxu['out', l], None, x, mod, l, OUTPROJ_ROW_TILE)
        else:
            x, mxu['out', l] = _outproj([y_a, y_b], w_out_even, j, x, mod, l, tm)
        grp.out_shift.append(s_shift)
        if prompt:
            grp.out_wkv.append(s_wkv)
            grp.out_gla.append(s_gla)
    else:
        if prompt:
            u = _inproj(x, mod, l, mxu['in', l], None, tm, tn_odd)
        else:
            u, mxu['in', l] = _inproj(x, mod, l, w_in_odd, j, tm, tn_odd)
        lru_w = (c_conv_w[j], c_conv_b[j].reshape(1, C_WIDTH), c_wa[j], c_wx[j],
                 c_ba[j].reshape(1, C_WIDTH), c_bx[j].reshape(1, C_WIDTH), c_lam[j].reshape(1, C_WIDTH))
        if prompt:
            cp8 = jnp.concatenate([jnp.zeros((n_seq, 8 - (C_CONV - 1), C_WIDTH), F32), st_conv[j]], axis=1)
            y_c, tail, h_t = _lru_seq(u, n_seq, cp8, st_lru[j].reshape(n_seq, 1, C_WIDTH), *lru_w,
                                      TOKENWISE_ROW_TILE)
            s_conv = tail[:, 8 - (C_CONV - 1):, :]
            s_lru = h_t.reshape(n_seq, C_WIDTH)
        else:
            y_c, cn, s_lru = _lru_step(u, jnp.swapaxes(st_conv, 1, 2), st_lru[j], j, *lru_w)
            s_conv = jnp.swapaxes(cn, 0, 1)
        if prompt:
            x = _outproj([y_c], mxu['out', l], None, x, mod, l, OUTPROJ_ROW_TILE)
        else:
            x, mxu['out', l] = _outproj([y_c], w_out_odd, j, x, mod, l, tm)
        grp.out_conv.append(s_conv)
        grp.out_lru.append(s_lru)
    grp.x = x


def _trunk(sample, prompt, wts):
    w_ffn_up, w_ffn_down, final_norm_w = wts[0], wts[1], wts[-1]
    subs = [(l, which) for l in range(DEPTH) for which in (0, 1)]
    mxu = {('ffn', 0, 0): (w_ffn_up[0, 0].astype(BF16), w_ffn_down[0, 0].astype(BF16))}
    for idx, (l, which) in enumerate(subs):
        final_w = final_norm_w if idx == len(subs) - 1 else None
        weights = mxu['ffn', l, which]
        sample.x = _ffn(sample.x, sample.mod, l, which, weights, sample.tm, sample.tf, final_w=final_w)
        if idx + 1 < len(subs):
            nxt = subs[idx + 1]
            prompt.x, mxu[('ffn',) + nxt] = _ffn(prompt.x, prompt.mod, l, which, weights, prompt.tm, prompt.tf,
                                                 convert=(w_ffn_up, w_ffn_down) + nxt)
        else:
            prompt.x = _ffn(prompt.x, prompt.mod, l, which, weights, prompt.tm, prompt.tf, final_w=final_w)
        if which == 0:
            _mixer(sample, l, wts, mxu)
            _mixer(prompt, l, wts, mxu)


def kernel(x_prompt, x_sample, c_prompt, c_sample, state_rwkv_shift, state_rwkv_wkv, state_gla, state_conv, state_lru, w_ada, b_ada, w_ffn_up, w_ffn_down, w_in_even, w_out_even, a_mu, a_w_up, a_a_up, a_g_up, a_vec, b_alpha_up, b_alpha_b, b_norm_w, w_in_odd, w_out_odd, c_conv_w, c_conv_b, c_wa, c_ba, c_wx, c_bx, c_lam, final_norm_w):
    wts = (w_ffn_up, w_ffn_down, w_in_even, w_out_even, a_mu, a_w_up, a_a_up, a_g_up, a_vec, b_alpha_up,
           b_alpha_b, b_norm_w, w_in_odd, w_out_odd, c_conv_w, c_conv_b, c_wa, c_ba, c_wx, c_bx, c_lam,
           final_norm_w)
    n_p, t_p, _ = x_prompt.shape
    n_s, t_s, _ = x_sample.shape
    assert t_s == 1

    c_rows = n_p + n_s
    pad = (-c_rows) % 16
    c_all = jnp.concatenate([c_sample, c_prompt, jnp.zeros((pad, D_MODEL), F32)], axis=0)
    mod_all = _ada(c_all, w_ada, b_ada)
    mod_p = _Mod(mod_all[:, :, n_s:c_rows].reshape(DEPTH, N_MOD, n_p, 1, D_MODEL), False, t_p)
    mod_s = _Mod(mod_all, True, 1)

    fresh = lambda s: jnp.zeros((s.shape[0], n_p) + s.shape[2:], s.dtype)
    states = (state_rwkv_shift, state_rwkv_wkv, state_gla, state_conv, state_lru)
    sample = _Group(x_sample.reshape(n_s, D_MODEL), mod_s, n_s, 1, states, n_s, SAMPLE_FFN_COL_TILE)
    prompt = _Group(x_prompt.reshape(n_p * t_p, D_MODEL), mod_p, n_p, t_p, tuple(fresh(s) for s in states),
                    PROMPT_ROW_TILE, FFN_COL_TILE)
    _trunk(sample, prompt, wts)
    return (prompt.x.reshape(n_p, t_p, D_MODEL), sample.x.reshape(n_s, 1, D_MODEL),
            *prompt.new_states(), *sample.new_states())
```

```python
import functools

import jax
import jax.numpy as jnp
from jax import lax
from jax.experimental import pallas as pl
from jax.experimental.pallas import tpu as pltpu

F32 = jnp.float32
BF16 = jnp.bfloat16

D_MODEL = 2048
DEPTH = 4
N_MOD = 9
D_FF = 5632
NORM_EPS = 1e-6

A_HEADS = 16
A_HEAD_DIM = 64
A_WIDTH = A_HEADS * A_HEAD_DIM
A_DECAY_RANK = 64
A_ICL_RANK = 64
A_GATE_RANK = 128
A_PROJ = 3 * A_WIDTH + A_DECAY_RANK + A_ICL_RANK + A_GATE_RANK
A_GN_EPS = 64e-5

B_HEADS = 4
B_KEY_DIM = 128
B_VAL_DIM = 256
B_KEY_WIDTH = B_HEADS * B_KEY_DIM
B_WIDTH = B_HEADS * B_VAL_DIM
B_ALPHA_RANK = 16
B_TAU = 16.0
B_PROJ = 2 * B_KEY_WIDTH + 2 * B_WIDTH + B_ALPHA_RANK
EVEN_PROJ = A_PROJ + B_PROJ

C_WIDTH = D_MODEL
C_BLOCKS = 8
C_BLOCK = C_WIDTH // C_BLOCKS
C_CONV = 4
C_POW = 8.0

LANES = 128
SUBLANES = 8
VMEM_LIMIT = 56 * 1024 * 1024
FFN_VMEM_LIMIT = 60 * 1024 * 1024

CHUNK = 64
RWKV_STEP_CHUNKS = 2
GLA_STEP_CHUNKS = 2
GLA_SUB = SUBLANES
PROMPT_ROW_TILE = 1024
OUTPROJ_ROW_TILE = 512
TOKENWISE_ROW_TILE = 256
FFN_COL_TILE = 512
SAMPLE_FFN_COL_TILE = 1408
EVEN_COL_TILE = 13 * LANES
ODD_COL_TILE = 8 * LANES
STEP_ROWS = 2 * SUBLANES
ADA_COL_TILE = 1024
NORM_ROWS = LANES


def _params(*sem, vmem=VMEM_LIMIT):
    return pltpu.CompilerParams(dimension_semantics=sem, vmem_limit_bytes=vmem)


def _bdot(a, b):
    return jnp.dot(a.astype(BF16), b.astype(BF16), preferred_element_type=F32)


def _split3(x):
    hi = x.astype(BF16)
    r1 = x - hi.astype(F32)
    mid = r1.astype(BF16)
    lo = (r1 - mid.astype(F32)).astype(BF16)
    return hi, mid, lo


def _dot_exact_rhs(a_bf16, x):
    hi, mid, lo = _split3(x)
    f = lambda y: jnp.dot(a_bf16, y, preferred_element_type=F32)
    return f(hi) + f(mid) + f(lo)


def _dot_exact_lhs(x, b_bf16):
    hi, mid, lo = _split3(x)
    f = lambda y: jnp.dot(y, b_bf16, preferred_element_type=F32)
    return f(hi) + f(mid) + f(lo)


def _sigmoid(x):
    return jax.nn.sigmoid(x)


def _softplus(x):
    return jnp.maximum(x, 0.0) + jnp.log1p(jnp.exp(-jnp.abs(x)))


def _gelu_tanh(x):
    return 0.5 * x * (1.0 + jnp.tanh(0.7978845608028654 * (x + 0.044715 * (x * x * x))))


def _norm_mod(x, shift, scale):
    ms = jnp.mean(x * x, axis=-1, keepdims=True)
    return x * lax.rsqrt(ms + NORM_EPS) * (1.0 + scale) + shift


def _ada_kernel(c_ref, w_ref, b_ref, o_ref):
    c = c_ref[...]
    act = (c * _sigmoid(c)).astype(BF16)
    o_ref[...] = jnp.dot(act, w_ref[...].astype(BF16), preferred_element_type=F32) + b_ref[...]


def _ada(c_all, w_ada, b_ada):
    rows = c_all.shape[0]
    tn = ADA_COL_TILE
    per = D_MODEL // tn
    b4 = b_ada.reshape(DEPTH, N_MOD * per, 1, tn)
    return pl.pallas_call(
        _ada_kernel,
        grid=(DEPTH, N_MOD * per),
        in_specs=[
            pl.BlockSpec((rows, D_MODEL), lambda l, j: (0, 0)),
            pl.BlockSpec((None, D_MODEL, tn), lambda l, j: (l, 0, j)),
            pl.BlockSpec((None, None, 1, tn), lambda l, j: (l, j, 0, 0)),
        ],
        out_specs=pl.BlockSpec((None, None, rows, tn), lambda l, j: (l, j // per, 0, j % per)),
        out_shape=jax.ShapeDtypeStruct((DEPTH, N_MOD, rows, D_MODEL), F32),
        compiler_params=_params("arbitrary", "arbitrary"),
        name="ada",
    )(c_all, w_ada, b4)


class _Mod:
    def __init__(self, arr, per_row, seq_len):
        self.arr = arr
        self.per_row = per_row
        self.seq_len = seq_len

    def spec(self, layer, m, tm):
        if self.per_row:
            return pl.BlockSpec((None, None, tm, D_MODEL), lambda i, *_: (layer, m, i, 0))
        per_seq = self.seq_len // tm
        return pl.BlockSpec((None, None, None, 1, D_MODEL), lambda i, *_: (layer, m, i // per_seq, 0, 0))


def _ffn_kernel(x_ref, sh_ref, sc_ref, g_ref, wg_ref, wu_ref, wd_ref, *rest, n_j, sub, final, convert):
    rest = list(rest)
    fw_ref = rest.pop(0) if final else None
    f32_tiles = [rest.pop(0) for _ in range(2)] if convert else []
    o_ref = rest.pop(0)
    bf16_tiles = [rest.pop(0) for _ in range(2)] if convert else []
    h_ref, = rest
    j = pl.program_id(1)
    tm = x_ref.shape[0]

    @pl.when(j == 0)
    def _():
        def body(s, carry):
            rows = pl.ds(pl.multiple_of(s * sub, sub), sub)
            sh = sh_ref[...] if sh_ref.shape[0] == 1 else sh_ref[rows, :]
            sc = sc_ref[...] if sc_ref.shape[0] == 1 else sc_ref[rows, :]
            h_ref[rows, :] = _norm_mod(x_ref[rows, :], sh, sc).astype(BF16)
            o_ref[rows, :] = jnp.zeros((sub, D_MODEL), F32)
            return carry
        lax.fori_loop(0, tm // sub, body, 0)

    for src, dst in zip(f32_tiles, bf16_tiles):
        dst[...] = src[...].astype(BF16)

    h = h_ref[...]
    gt = jnp.dot(h, wg_ref[...], preferred_element_type=F32)
    ut = jnp.dot(h, wu_ref[...], preferred_element_type=F32)
    act = (gt * _sigmoid(gt) * ut).astype(BF16)
    o_ref[...] += jnp.dot(act, wd_ref[...], preferred_element_type=F32)

    @pl.when(j == n_j - 1)
    def _():
        def body(s, carry):
            rows = pl.ds(pl.multiple_of(s * sub, sub), sub)
            g = g_ref[...] if g_ref.shape[0] == 1 else g_ref[rows, :]
            y = x_ref[rows, :] + 0.5 * g * o_ref[rows, :]
            if final:
                ms = jnp.mean(y * y, axis=-1, keepdims=True)
                y = y * lax.rsqrt(ms + NORM_EPS) * fw_ref[...]
            o_ref[rows, :] = y
            return carry
        lax.fori_loop(0, tm // sub, body, 0)


def _ffn(x, mod, layer, which, weights, tm, tf, final_w=None, convert=None):
    m_rows = x.shape[0]
    n_i = m_rows // tm
    n_j = D_FF // tf
    m0 = 6 * which
    sub = min(tm, NORM_ROWS)
    final = final_w is not None
    in_specs = [
        pl.BlockSpec((tm, D_MODEL), lambda i, j: (i, 0)),
        mod.spec(layer, m0, tm), mod.spec(layer, m0 + 1, tm), mod.spec(layer, m0 + 2, tm),
        pl.BlockSpec((D_MODEL, tf), lambda i, j: (0, j)),
        pl.BlockSpec((D_MODEL, tf), lambda i, j: (0, j + n_j)),
        pl.BlockSpec((tf, D_MODEL), lambda i, j: (j, 0)),
    ]
    args = [x, mod.arr, mod.arr, mod.arr, weights[0], weights[0], weights[1]]
    if final:
        in_specs.append(pl.BlockSpec((1, D_MODEL), lambda i, j: (0, 0)))
        args.append(final_w.reshape(1, D_MODEL))
    out_specs = [pl.BlockSpec((tm, D_MODEL), lambda i, j: (i, 0))]
    out_shape = [jax.ShapeDtypeStruct((m_rows, D_MODEL), F32)]
    if convert is not None:
        w_up32, w_down32, l2, s2 = convert
        up_tile = (D_MODEL // n_i, 2 * D_FF // n_j)
        down_tile = (D_FF // n_j, D_MODEL // n_i)
        assert up_tile[0] * n_i == D_MODEL and up_tile[1] * n_j == 2 * D_FF and up_tile[1] % LANES == 0
        assert down_tile[0] * n_j == D_FF and down_tile[1] * n_i == D_MODEL and down_tile[1] % LANES == 0
        in_specs += [pl.BlockSpec((None, None) + up_tile, lambda i, j: (l2, s2, i, j)),
                     pl.BlockSpec((None, None) + down_tile, lambda i, j: (l2, s2, j, i))]
        args += [w_up32, w_down32]
        out_specs += [pl.BlockSpec(up_tile, lambda i, j: (i, j)), pl.BlockSpec(down_tile, lambda i, j: (j, i))]
        out_shape += [jax.ShapeDtypeStruct((D_MODEL, 2 * D_FF), BF16), jax.ShapeDtypeStruct((D_FF, D_MODEL), BF16)]
    out = pl.pallas_call(
        functools.partial(_ffn_kernel, n_j=n_j, sub=sub, final=final, convert=convert is not None),
        grid=(n_i, n_j),
        in_specs=in_specs,
        out_specs=out_specs,
        out_shape=out_shape,
        scratch_shapes=[pltpu.VMEM((tm, D_MODEL), BF16)],
        compiler_params=_params("arbitrary", "arbitrary", vmem=FFN_VMEM_LIMIT),
        name="ffn",
    )(*args)
    return (out[0], (out[1], out[2])) if convert is not None else out[0]


def _inproj_kernel(x_ref, sh_ref, sc_ref, w_ref, o_ref, *rest, sub, emit, transposed):
    h_ref = rest[-1]
    j = pl.program_id(1)
    tm = x_ref.shape[0]

    @pl.when(j == 0)
    def _():
        def body(s, carry):
            rows = pl.ds(pl.multiple_of(s * sub, sub), sub)
            sh = sh_ref[...] if sh_ref.shape[0] == 1 else sh_ref[rows, :]
            sc = sc_ref[...] if sc_ref.shape[0] == 1 else sc_ref[rows, :]
            h_ref[rows, :] = _norm_mod(x_ref[rows, :], sh, sc).astype(BF16)
            return carry
        lax.fori_loop(0, tm // sub, body, 0)

    w = w_ref[...]
    if transposed:
        w = w.T
    if emit:
        w = w.astype(BF16)
        rest[0][...] = w
    o_ref[...] = jnp.dot(h_ref[...], w, preferred_element_type=F32)


def _inproj(x, mod, layer, w, widx, tm, tn, transposed=False):
    m_rows = x.shape[0]
    n_out = w.shape[-2] if transposed else w.shape[-1]
    sub = min(tm, NORM_ROWS)
    emit = widx is not None
    if transposed:
        w_spec = pl.BlockSpec((None, tn, D_MODEL), lambda i, j: (widx, j, 0))
    elif emit:
        w_spec = pl.BlockSpec((None, D_MODEL, tn), lambda i, j: (widx, 0, j))
    else:
        w_spec = pl.BlockSpec((D_MODEL, tn), lambda i, j: (0, j))
    out_specs = [pl.BlockSpec((tm, tn), lambda i, j: (i, j))]
    out_shape = [jax.ShapeDtypeStruct((m_rows, n_out), F32)]
    if emit:
        assert m_rows == tm
        out_specs.append(pl.BlockSpec((D_MODEL, tn), lambda i, j: (0, j)))
        out_shape.append(jax.ShapeDtypeStruct((D_MODEL, n_out), BF16))
    out = pl.pallas_call(
        functools.partial(_inproj_kernel, sub=sub, emit=emit, transposed=transposed),
        grid=(m_rows // tm, pl.cdiv(n_out, tn)),
        in_specs=[
            pl.BlockSpec((tm, D_MODEL), lambda i, j: (i, 0)),
            mod.spec(layer, 3, tm), mod.spec(layer, 4, tm),
            w_spec,
        ],
        out_specs=out_specs,
        out_shape=out_shape,
        scratch_shapes=[pltpu.VMEM((tm, D_MODEL), BF16)],
        compiler_params=_params("arbitrary", "arbitrary"),
        name="inproj",
    )(x, mod.arr, mod.arr, w)
    return tuple(out) if emit else out[0]


def _outproj_kernel(*refs, n_in, emit):
    y_refs = refs[:n_in]
    w_refs = refs[n_in:2 * n_in]
    x_ref, g_ref, o_ref = refs[2 * n_in:2 * n_in + 3]
    acc = None
    for k, (y_ref, w_ref) in enumerate(zip(y_refs, w_refs)):
        w = w_ref[...]
        if emit:
            w = w.astype(BF16)
            refs[2 * n_in + 3 + k][...] = w
        part = jnp.dot(y_ref[...].astype(BF16), w, preferred_element_type=F32)
        acc = part if acc is None else acc + part
    o_ref[...] = x_ref[...] + g_ref[...] * acc


def _outproj(ys, w, widx, x, mod, layer, tm):
    m_rows = x.shape[0]
    n_in = len(ys)
    kw = ys[0].shape[1]
    emit = widx is not None
    in_specs = [pl.BlockSpec((tm, kw), lambda i: (i, 0)) for _ in ys]
    if emit:
        assert m_rows == tm
        in_specs += [pl.BlockSpec((None, kw, D_MODEL), lambda i, k=k: (widx, k, 0)) for k in range(n_in)]
        w_args = [w] * n_in
    else:
        in_specs += [pl.BlockSpec((kw, D_MODEL), lambda i: (0, 0))] * n_in
        w_args = list(w)
    in_specs += [pl.BlockSpec((tm, D_MODEL), lambda i: (i, 0)), mod.spec(layer, 5, tm)]
    out_specs = [pl.BlockSpec((tm, D_MODEL), lambda i: (i, 0))]
    out_shape = [jax.ShapeDtypeStruct((m_rows, D_MODEL), F32)]
    if emit:
        out_specs += [pl.BlockSpec((kw, D_MODEL), lambda i: (0, 0))] * n_in
        out_shape += [jax.ShapeDtypeStruct((kw, D_MODEL), BF16)] * n_in
    out = pl.pallas_call(
        functools.partial(_outproj_kernel, n_in=n_in, emit=emit),
        grid=(m_rows // tm,),
        in_specs=in_specs,
        out_specs=out_specs,
        out_shape=out_shape,
        compiler_params=_params("arbitrary"),
        name="outproj",
    )(*ys, *w_args, x, mod.arr)
    return (out[0], tuple(out[1:])) if emit else out[0]


def _rwkv_prep_kernel(u_ref, prev_ref, mu_ref, wup_ref, aup_ref, gup_ref, vec_ref, *rest, seq_mode):
    if seq_mode:
        r_ref, lw_ref, k_ref, v_ref, kk_ref, a_ref, g_ref, carry_ref = rest
    else:
        seg_ref, r_ref, lw_ref, k_ref, v_ref, kk_ref, a_ref, g_ref, bonus_ref = rest
    u = u_ref[...]
    tm = u.shape[0]
    if seq_mode:
        @pl.when(pl.program_id(1) == 0)
        def _():
            carry_ref[...] = prev_ref[...]
        rolled = pltpu.roll(u, 1, axis=0)
        row = lax.broadcasted_iota(jnp.int32, u.shape, 0)
        prev = jnp.where(row == 0, carry_ref[...], rolled)
        carry_ref[...] = u[tm - 1:tm, :]
    else:
        prev = prev_ref[...]
    xs = u + mu_ref[...] * (prev - u)
    r = xs[:, 0:A_WIDTH]
    k = xs[:, A_WIDTH:2 * A_WIDTH]
    v = xs[:, 2 * A_WIDTH:3 * A_WIDTH]
    wa_lo = xs[:, 3 * A_WIDTH:3 * A_WIDTH + LANES]
    g_lo = xs[:, 3 * A_WIDTH + LANES:]
    w0, a0, k_k, k_a, r_k = (vec_ref[i:i + 1, :] for i in range(5))
    w_raw = -_softplus(-(w0 + _bdot(jnp.tanh(wa_lo), wup_ref[...]))) - 0.5
    lw = -jnp.exp(w_raw)
    a = _sigmoid(a0 + _bdot(wa_lo, aup_ref[...]))
    g = _bdot(_sigmoid(g_lo), gup_ref[...])
    kk = k * k_k
    k2 = k * (1.0 + (a - 1.0) * k_a)
    if not seq_mode:
        seg = seg_ref[...]
        kk = kk * lax.rsqrt(jnp.maximum(_dot_exact_lhs(kk * kk, seg), 1e-24))
        bonus_ref[...] = _dot_exact_lhs(r * k2 * r_k, seg) * v
    r_ref[...] = r
    lw_ref[...] = lw
    k_ref[...] = k2
    v_ref[...] = v
    kk_ref[...] = kk
    a_ref[...] = a
    g_ref[...] = g


def _rwkv_prep(u, prev, mu, wup_pad, aup_pad, g_up, vec, seg, n_seq, tm, seq_mode):
    rows = u.shape[0]
    per_seq = rows // n_seq // tm if seq_mode else 1
    grid = (n_seq, per_seq) if seq_mode else (rows // tm, 1)
    rowmap = (lambda b, t: (b * per_seq + t, 0)) if seq_mode else (lambda b, t: (b, 0))
    prev_spec = (pl.BlockSpec((None, 1, A_PROJ), lambda b, t: (b, 0, 0)) if seq_mode
                 else pl.BlockSpec((tm, A_PROJ), rowmap))
    const = lambda shape: pl.BlockSpec(shape, lambda b, t: (0,) * len(shape))
    out = jax.ShapeDtypeStruct((rows, A_WIDTH), F32)
    in_specs = [pl.BlockSpec((tm, A_PROJ), rowmap), prev_spec,
                const((1, A_PROJ)), const((LANES, A_WIDTH)), const((LANES, A_WIDTH)),
                const((A_GATE_RANK, A_WIDTH)), const((SUBLANES, A_WIDTH))]
    args = [u, prev, mu, wup_pad, aup_pad, g_up, vec]
    if not seq_mode:
        in_specs.append(const((A_WIDTH, A_WIDTH)))
        args.append(seg)
    n_out = 7 if seq_mode else 8
    return pl.pallas_call(
        functools.partial(_rwkv_prep_kernel, seq_mode=seq_mode),
        grid=grid,
        in_specs=in_specs,
        out_specs=[pl.BlockSpec((tm, A_WIDTH), rowmap)] * n_out,
        out_shape=[out] * n_out,
        scratch_shapes=[pltpu.VMEM((1, A_PROJ), F32)] if seq_mode else [],
        compiler_params=_params("arbitrary", "arbitrary"),
        name="rwkv_prep",
    )(*args)


def _rwkv_post_kernel(y_ref, bonus_ref, g_ref, vec_ref, seg_ref, o_ref):
    y = y_ref[...]
    seg = seg_ref[...]
    inv_n = 1.0 / A_HEAD_DIM
    yc = y - _dot_exact_lhs(y, seg) * inv_n
    var = _dot_exact_lhs(yc * yc, seg) * inv_n
    yn = yc * lax.rsqrt(var + A_GN_EPS) * vec_ref[5:6, :] + vec_ref[6:7, :]
    o_ref[...] = (yn + bonus_ref[...]) * g_ref[...]


def _rwkv_post(y, bonus, g, vec, seg, tm):
    rows = y.shape[0]
    spec = pl.BlockSpec((tm, A_WIDTH), lambda i: (i, 0))
    return pl.pallas_call(
        _rwkv_post_kernel,
        grid=(rows // tm,),
        in_specs=[spec, spec, spec,
                  pl.BlockSpec((SUBLANES, A_WIDTH), lambda i: (0, 0)),
                  pl.BlockSpec((A_WIDTH, A_WIDTH), lambda i: (0, 0))],
        out_specs=spec,
        out_shape=jax.ShapeDtypeStruct((rows, A_WIDTH), F32),
        compiler_params=_params("arbitrary"),
        name="rwkv_post",
    )(y, bonus, g, vec, seg)


def _rwkv_chunk_kernel(r_ref, lw_ref, k_ref, v_ref, kk_ref, a_ref, g_ref, vec_ref, h0_ref, y_ref, hT_ref, h_ref):
    ci = pl.program_id(1)

    @pl.when(ci == 0)
    def _():
        h_ref[...] = h0_ref[...]

    cs = CHUNK
    n2 = 2 * cs
    npair = A_WIDTH // LANES
    pairs = lambda x: jnp.stack([x[:, p * LANES:(p + 1) * LANES] for p in range(npair)])
    vec = vec_ref[...]
    vrow = lambda i: pairs(vec[i:i + 1, :])
    lane = lax.broadcasted_iota(jnp.int32, (1, 1, LANES), 2)
    m0 = (lane < A_HEAD_DIM).astype(F32)
    m1 = 1.0 - m0
    split = lambda x: jnp.concatenate([x * m0, x * m1], axis=1)
    dup = lambda x: jnp.concatenate([x, x], axis=1)
    own = jnp.concatenate([jnp.broadcast_to(m0, (1, cs, LANES)), jnp.broadcast_to(m1, (1, cs, LANES))], axis=1)
    bf = lambda x: x.astype(BF16)

    def bmm(x, y):
        return jnp.einsum('hab,hbc->hac', x, y, preferred_element_type=F32)

    def bmm_nt(x, y):
        return jnp.einsum('han,hbn->hab', x, y, preferred_element_type=F32)

    def bmm_tn(x, y):
        return jnp.einsum('hca,hcb->hab', x, y, preferred_element_type=F32)

    row = lax.broadcasted_iota(jnp.int32, (cs, cs), 0)
    col = lax.broadcasted_iota(jnp.int32, (cs, cs), 1)
    tri = (col <= row).astype(BF16)
    row2 = lax.broadcasted_iota(jnp.int32, (n2, n2), 0)
    col2 = lax.broadcasted_iota(jnp.int32, (n2, n2), 1)
    same = (row2 >= cs) == (col2 >= cs)
    tok_r = jnp.where(row2 >= cs, row2 - cs, row2)
    tok_c = jnp.where(col2 >= cs, col2 - cs, col2)
    strict = jnp.where(same, (tok_c < tok_r).astype(F32), 0.0)
    incl = jnp.where(same, (tok_c <= tok_r).astype(F32), 0.0)
    eye2 = (row2 == col2).astype(F32)
    nrow = lax.broadcasted_iota(jnp.int32, (LANES, LANES), 0)
    ncol = lax.broadcasted_iota(jnp.int32, (LANES, LANES), 1)
    eye_n = (nrow == ncol).astype(F32)

    h = h_ref[...]
    for sub in range(r_ref.shape[0] // cs):
        rows = slice(sub * cs, (sub + 1) * cs)
        lw_all = lw_ref[rows, :]
        c = pairs(_dot_exact_rhs(tri, lw_all))
        lw = pairs(lw_all)
        c_end = c[:, cs - 1:cs, :]
        e_pos = dup(jnp.exp(c))
        e_neg = dup(jnp.exp(-c))
        e_prev = dup(jnp.exp(c - lw))
        e_end = dup(jnp.exp(c_end - c))
        g_end = jnp.exp(c_end)

        kk = split(pairs(kk_ref[rows, :]))
        kk = kk * lax.rsqrt(jnp.maximum(jnp.sum(kk * kk, axis=-1, keepdims=True), 1e-24))
        b_vec = kk * dup(pairs(a_ref[rows, :]))
        r2 = split(pairs(r_ref[rows, :]))
        k2 = split(pairs(k_ref[rows, :]))
        v2 = split(pairs(v_ref[rows, :]))
        bonus = jnp.sum(r2 * k2 * vrow(4), axis=-1, keepdims=True) * v2
        rt = r2 * e_pos
        at = bf(-kk * e_prev)
        bt = bf(b_vec * e_neg)
        kt = bf(k2 * e_neg)
        bh = bf(b_vec * e_end)
        kh = bf(k2 * e_end)
        v2b = bf(v2)

        gram = bmm_nt(jnp.concatenate([at, bf(rt)], axis=1), jnp.concatenate([bt, kt], axis=1))
        a_ab = gram[:, :n2, :n2] * strict
        a_ak = gram[:, :n2, n2:] * strict
        m_b = gram[:, n2:, :n2] * incl
        m_k = gram[:, n2:, n2:] * incl

        tinv = eye2 + a_ab
        a_b = bf(a_ab)
        apow = bmm(a_b, a_b)
        span = 2
        while 2 * span < cs:
            ap_b = bf(apow)
            both = bmm(jnp.concatenate([bf(tinv), ap_b], axis=1), ap_b)
            tinv = tinv + both[:, :n2]
            apow = both[:, n2:]
            span *= 2
        tinv = tinv + bmm(bf(tinv), bf(apow))

        akv = bmm(bf(a_ak), v2b)
        pq = bf(bmm(bf(tinv), jnp.concatenate([at, bf(akv)], axis=-1)))
        mpq = bmm(bf(m_b), pq)
        p2 = rt + mpq[..., :LANES]
        y0 = mpq[..., LANES:] + bmm(bf(m_k), v2b)
        tpq = bmm_tn(bh, pq)
        gm = eye_n * g_end + tpq[..., :LANES]
        hadd = tpq[..., LANES:] + bmm_tn(kh, v2b)

        yh = bmm(jnp.concatenate([bf(p2), bf(gm)], axis=1), bf(h))
        y2 = yh[:, :n2] + y0
        h = yh[:, n2:] + hadd

        inv_n = 1.0 / A_HEAD_DIM
        yc = (y2 - jnp.sum(y2, axis=-1, keepdims=True) * inv_n) * own
        var = jnp.sum(yc * yc, axis=-1, keepdims=True) * inv_n
        z = yc * lax.rsqrt(var + A_GN_EPS) * vrow(5) + vrow(6) * own + bonus
        out = (z[:, :cs] + z[:, cs:]) * pairs(g_ref[rows, :])
        for p in range(npair):
            y_ref[rows, p * LANES:(p + 1) * LANES] = out[p]

    h_ref[...] = h

    @pl.when(ci == pl.num_programs(1) - 1)
    def _():
        hT_ref[...] = h


def _rwkv_chunk(r, lw, k, v, kk, a, g, vec, n_seq, h0):
    rows = r.shape[0]
    step_rows = RWKV_STEP_CHUNKS * CHUNK
    nc = rows // n_seq // step_rows
    npair = A_WIDTH // LANES
    spec = pl.BlockSpec((step_rows, A_WIDTH), lambda b, c: (b * nc + c, 0))
    hspec = pl.BlockSpec((None, npair, LANES, LANES), lambda b, c: (b, 0, 0, 0))
    return pl.pallas_call(
        _rwkv_chunk_kernel,
        grid=(n_seq, nc),
        in_specs=[spec] * 7 + [pl.BlockSpec((SUBLANES, A_WIDTH), lambda b, c: (0, 0)), hspec],
        out_specs=[spec, hspec],
        out_shape=[jax.ShapeDtypeStruct((rows, A_WIDTH), F32),
                   jax.ShapeDtypeStruct((n_seq, npair, LANES, LANES), F32)],
        scratch_shapes=[pltpu.VMEM((npair, LANES, LANES), F32)],
        compiler_params=_params("arbitrary", "arbitrary"),
        name="rwkv_chunk",
    )(r, lw, k, v, kk, a, g, vec, h0)


def _pair_states(s):
    n = s.shape[0]
    nd = A_HEAD_DIM
    h = jnp.swapaxes(s, -1, -2).reshape(n, A_HEADS // 2, 2, nd, nd)
    out = jnp.zeros((n, A_HEADS // 2, 2 * nd, 2 * nd), s.dtype)
    return out.at[:, :, :nd, :nd].set(h[:, :, 0]).at[:, :, nd:, nd:].set(h[:, :, 1])


def _unpair_states(hp):
    n = hp.shape[0]
    nd = A_HEAD_DIM
    h = jnp.stack([hp[:, :, :nd, :nd], hp[:, :, nd:, nd:]], axis=2).reshape(n, A_HEADS, nd, nd)
    return jnp.swapaxes(h, -1, -2)


def _rwkv_step_kernel(r_ref, lw_ref, k_ref, v_ref, kk_ref, a_ref, s_ref, acc_ref, y_ref, so_ref,
                      r_t, w_t, k_t, v_t, nkk_t, b_t, y_t):
    h = pl.program_id(0)
    nd = A_HEAD_DIM

    @pl.when(h == 0)
    def _():
        kk = kk_ref[...]
        r_t[...] = r_ref[...].T
        w_t[...] = jnp.exp(lw_ref[...]).T
        k_t[...] = k_ref[...].T
        v_t[...] = v_ref[...].T
        nkk_t[...] = (-kk).T
        b_t[...] = (kk * a_ref[...]).T

    first = pl.multiple_of(h * nd, nd)
    rows = pl.ds(first, nd)
    r, w, k, nkk, b = r_t[rows, :], w_t[rows, :], k_t[rows, :], nkk_t[rows, :], b_t[rows, :]

    def body(i, carry):
        s = s_ref[i]
        s_kk = jnp.sum(s * nkk, axis=0, keepdims=True)
        s_new = s * w + s_kk * b + v_t[pl.ds(first + i, 1), :] * k
        so_ref[i] = s_new
        y_t[pl.ds(first + i, 1), :] = jnp.sum(s_new * r, axis=0, keepdims=True)
        return carry

    lax.fori_loop(0, nd, body, 0, unroll=4)

    @pl.when(h == pl.num_programs(0) - 1)
    def _():
        y_ref[...] = y_t[...].T


def _rwkv_step(r, lw, k, v, kk, a, states_t, j, new_states):
    n = r.shape[0]
    nd = A_HEAD_DIM
    rspec = pl.BlockSpec((n, A_WIDTH), lambda h: (0, 0))
    sspec = pl.BlockSpec((None, None, nd, nd, n), lambda h: (j, h, 0, 0, 0))
    return pl.pallas_call(
        _rwkv_step_kernel,
        grid=(A_HEADS,),
        in_specs=[rspec] * 6 + [sspec, pl.BlockSpec(memory_space=pl.ANY)],
        out_specs=[rspec, sspec],
        out_shape=[jax.ShapeDtypeStruct((n, A_WIDTH), F32), jax.ShapeDtypeStruct(states_t.shape, F32)],
        input_output_aliases={7: 1},
        scratch_shapes=[pltpu.VMEM((A_WIDTH, n), F32)] * 7,
        compiler_params=_params("arbitrary"),
        name="rwkv_step",
    )(r, lw, k, v, kk, a, states_t, new_states)


def _gla_log_alpha(a128, aup_ref, ab_ref):
    lane = lax.broadcasted_iota(jnp.int32, a128.shape, 1)
    a_lo = jnp.where(lane < B_ALPHA_RANK, a128, 0.0)
    x = _bdot(a_lo, aup_ref[...]) + ab_ref[...]
    return -_softplus(-x) * (1.0 / B_TAU)


def _gla_block(u_ref, aup_ref, ab_ref, nw_ref, o_ref, s_ref):
    ub = u_ref[...]
    cs = ub.shape[0]
    q_all = ub[:, 0:B_KEY_WIDTH] * (B_KEY_DIM ** -0.5)
    k_all = ub[:, B_KEY_WIDTH:2 * B_KEY_WIDTH]
    v_all = ub[:, 2 * B_KEY_WIDTH:2 * B_KEY_WIDTH + B_WIDTH]
    g_all = ub[:, 2 * B_KEY_WIDTH + B_WIDTH:2 * B_KEY_WIDTH + 2 * B_WIDTH]
    a128 = ub[:, 2 * B_KEY_WIDTH + 2 * B_WIDTH:2 * B_KEY_WIDTH + 2 * B_WIDTH + LANES]
    log_a = _gla_log_alpha(a128, aup_ref, ab_ref)

    row = lax.broadcasted_iota(jnp.int32, (cs, cs), 0)
    col = lax.broadcasted_iota(jnp.int32, (cs, cs), 1)
    cum = _dot_exact_rhs((col <= row).astype(BF16), log_a)
    e_pos = jnp.exp(cum)
    cum_end = cum[cs - 1:cs, :]
    e_end = jnp.exp(cum_end - cum)
    g_end = jnp.exp(cum_end)
    nk = B_KEY_DIM
    eye_k = (lax.broadcasted_iota(jnp.int32, (nk, nk), 0) == lax.broadcasted_iota(jnp.int32, (nk, nk), 1)).astype(F32)

    nblk = cs // GLA_SUB
    kw = B_KEY_WIDTH
    cum3 = cum.reshape(nblk, GLA_SUB, kw)
    before = jnp.concatenate([jnp.zeros((1, 1, kw), F32), cum3[:nblk - 1, GLA_SUB - 1:GLA_SUB, :]], axis=0)
    q_rel = q_all * jnp.exp(cum - jnp.broadcast_to(before, (nblk, GLA_SUB, kw)).reshape(cs, kw))
    tok = lax.broadcasted_iota(jnp.int32, (cs, 1), 0)
    q_parts, k_parts = [], []
    for blk in range(1, nblk):
        first = blk * GLA_SUB
        k_parts.append(jnp.where(tok < first, k_all * jnp.exp(jnp.minimum(before[blk] - cum, 0.0)), 0.0))
        q_parts.append(jnp.where((tok >= first) & (tok < first + GLA_SUB), q_rel, 0.0))

    q3 = q_all.reshape(nblk, GLA_SUB, kw)
    k3 = k_all.reshape(nblk, GLA_SUB, kw)
    v3 = v_all.reshape(nblk, GLA_SUB, B_WIDTH)
    sub_i = lax.broadcasted_iota(jnp.int32, (1, GLA_SUB, 1), 1)
    o_diag = [jnp.zeros((nblk, GLA_SUB, B_VAL_DIM), F32) for _ in range(B_HEADS)]
    for jj in range(GLA_SUB):
        pair = q3 * k3[:, jj:jj + 1, :] * jnp.exp(jnp.minimum(cum3 - cum3[:, jj:jj + 1, :], 0.0))
        pair = jnp.where(sub_i >= jj, pair, 0.0)
        for h in range(B_HEADS):
            score = jnp.sum(pair[:, :, h * nk:(h + 1) * nk], axis=-1, keepdims=True)
            o_diag[h] = o_diag[h] + score * v3[:, jj:jj + 1, h * B_VAL_DIM:(h + 1) * B_VAL_DIM]

    for h in range(B_HEADS):
        ks = slice(h * B_KEY_DIM, (h + 1) * B_KEY_DIM)
        vs = slice(h * B_VAL_DIM, (h + 1) * B_VAL_DIM)
        qe = q_all[:, ks] * e_pos[:, ks]
        ke = k_all[:, ks] * e_end[:, ks]
        vh = v_all[:, vs]
        s = s_ref[h]
        q_cat = jnp.concatenate([part[:, ks] for part in q_parts], axis=-1).astype(BF16)
        k_cat = jnp.concatenate([part[:, ks] for part in k_parts], axis=-1).astype(BF16)
        scores = jnp.einsum('id,jd->ij', q_cat, k_cat, preferred_element_type=F32)
        o = _bdot(scores, vh) + _bdot(qe, s) + o_diag[h].reshape(cs, B_VAL_DIM)
        g_col = jnp.sum(eye_k * g_end[:, ks], axis=-1, keepdims=True)
        s_ref[h] = s * g_col + jnp.einsum('jd,jv->dv', ke.astype(BF16), vh.astype(BF16),
                                           preferred_element_type=F32)
        ms = jnp.mean(o * o, axis=-1, keepdims=True)
        gh = g_all[:, vs]
        o_ref[:, vs] = o * lax.rsqrt(ms + NORM_EPS) * nw_ref[:, vs] * (gh * _sigmoid(gh))


def _gla_chunk_kernel(u_ref, aup_ref, ab_ref, nw_ref, s0_ref, o_ref, sT_ref, s_ref):
    ci = pl.program_id(1)

    @pl.when(ci == 0)
    def _():
        s_ref[...] = s0_ref[...]

    for sub in range(u_ref.shape[0] // CHUNK):
        rows = pl.ds(sub * CHUNK, CHUNK)
        _gla_block(u_ref.at[rows, :], aup_ref, ab_ref, nw_ref, o_ref.at[rows, :], s_ref)

    @pl.when(ci == pl.num_programs(1) - 1)
    def _():
        sT_ref[...] = s_ref[...]


def _gla_chunk(u, n_seq, alpha_up_pad, alpha_b, norm_w, s0):
    rows = u.shape[0]
    step_rows = GLA_STEP_CHUNKS * CHUNK
    nc = rows // n_seq // step_rows
    return pl.pallas_call(
        _gla_chunk_kernel,
        grid=(n_seq, nc),
        in_specs=[
            pl.BlockSpec((step_rows, A_PROJ), lambda b, c: (b * nc + c, 1)),
            pl.BlockSpec((LANES, B_KEY_WIDTH), lambda b, c: (0, 0)),
            pl.BlockSpec((1, B_KEY_WIDTH), lambda b, c: (0, 0)),
            pl.BlockSpec((1, B_WIDTH), lambda b, c: (0, 0)),
            pl.BlockSpec((None, B_HEADS, B_KEY_DIM, B_VAL_DIM), lambda b, c: (b, 0, 0, 0)),
        ],
        out_specs=[
            pl.BlockSpec((step_rows, B_WIDTH), lambda b, c: (b * nc + c, 0)),
            pl.BlockSpec((None, B_HEADS, B_KEY_DIM, B_VAL_DIM), lambda b, c: (b, 0, 0, 0)),
        ],
        out_shape=[jax.ShapeDtypeStruct((rows, B_WIDTH), F32),
                   jax.ShapeDtypeStruct((n_seq, B_HEADS, B_KEY_DIM, B_VAL_DIM), F32)],
        scratch_shapes=[pltpu.VMEM((B_HEADS, B_KEY_DIM, B_VAL_DIM), F32)],
        compiler_params=_params("arbitrary", "arbitrary"),
        name="gla_chunk",
    )(u, alpha_up_pad, alpha_b, norm_w, s0)


def _gla_step_kernel(u_ref, aup_ref, ab_ref, nw_ref, s_ref, acc_ref, o_ref, so_ref, la_ref):
    nb = u_ref.shape[0]
    nk = B_KEY_DIM
    off_k, off_v, off_g = B_KEY_WIDTH, 2 * B_KEY_WIDTH, 2 * B_KEY_WIDTH + B_WIDTH
    off_a = 2 * B_KEY_WIDTH + 2 * B_WIDTH
    la_ref[...] = _gla_log_alpha(u_ref[:, off_a:off_a + LANES], aup_ref, ab_ref)
    eye = (lax.broadcasted_iota(jnp.int32, (nk, nk), 0) == lax.broadcasted_iota(jnp.int32, (nk, nk), 1)).astype(F32)
    col = lambda x: jnp.sum(eye * x, axis=-1, keepdims=True)

    def body(n, carry):
        row = pl.ds(n, 1)
        ur = u_ref[row, :]
        dec_all = jnp.exp(la_ref[row, :])
        outs = []
        for h in range(B_HEADS):
            q = ur[:, h * nk:(h + 1) * nk] * (B_KEY_DIM ** -0.5)
            k = ur[:, off_k + h * nk:off_k + (h + 1) * nk]
            v = ur[:, off_v + h * B_VAL_DIM:off_v + (h + 1) * B_VAL_DIM]
            g = ur[:, off_g + h * B_VAL_DIM:off_g + (h + 1) * B_VAL_DIM]
            dec = dec_all[:, h * nk:(h + 1) * nk]
            s = s_ref[n, h]
            qk = jnp.sum(q * k, axis=-1, keepdims=True)
            o = qk * v + jnp.sum(col(q * dec) * s, axis=0, keepdims=True)
            so_ref[n, h] = s * col(dec) + col(k) * v
            ms = jnp.mean(o * o, axis=-1, keepdims=True)
            outs.append(o * lax.rsqrt(ms + NORM_EPS) * nw_ref[:, h * B_VAL_DIM:(h + 1) * B_VAL_DIM]
                        * (g * _sigmoid(g)))
        o_ref[row, :] = jnp.concatenate(outs, axis=-1)
        return carry

    lax.fori_loop(0, nb, body, 0)


def _gla_step(u, alpha_up_pad, alpha_b, norm_w, states, j, nb, new_states):
    n = u.shape[0]
    sspec = pl.BlockSpec((None, nb, B_HEADS, B_KEY_DIM, B_VAL_DIM), lambda i: (j, i, 0, 0, 0))
    return pl.pallas_call(
        _gla_step_kernel,
        grid=(n // nb,),
        in_specs=[pl.BlockSpec((nb, A_PROJ), lambda i: (i, 1)),
                  pl.BlockSpec((LANES, B_KEY_WIDTH), lambda i: (0, 0)),
                  pl.BlockSpec((1, B_KEY_WIDTH), lambda i: (0, 0)),
                  pl.BlockSpec((1, B_WIDTH), lambda i: (0, 0)),
                  sspec, pl.BlockSpec(memory_space=pl.ANY)],
        out_specs=[pl.BlockSpec((nb, B_WIDTH), lambda i: (i, 0)), sspec],
        out_shape=[jax.ShapeDtypeStruct((n, B_WIDTH), F32), jax.ShapeDtypeStruct(states.shape, F32)],
        input_output_aliases={5: 1},
        scratch_shapes=[pltpu.VMEM((nb, B_KEY_WIDTH), F32)],
        compiler_params=_params("arbitrary"),
        name="gla_step",
    )(u, alpha_up_pad, alpha_b, norm_w, states, new_states)


def _lru_gates(xc, wa_ref, wx_ref, ba_ref, bx_ref, lam_ref):
    ra, ix = [], []
    for nb in range(C_BLOCKS):
        xb = xc[:, nb * C_BLOCK:(nb + 1) * C_BLOCK].astype(BF16)
        ra.append(jnp.dot(xb, wa_ref[nb].astype(BF16), preferred_element_type=F32))
        ix.append(jnp.dot(xb, wx_ref[nb].astype(BF16), preferred_element_type=F32))
    r = _sigmoid(jnp.concatenate(ra, axis=-1) + ba_ref[...])
    i_g = _sigmoid(jnp.concatenate(ix, axis=-1) + bx_ref[...])
    log_a = -C_POW * r * _softplus(-lam_ref[...])
    a = jnp.exp(log_a)
    b = jnp.sqrt(1.0 - a * a) * (i_g * xc)
    return a, b


def _lru_seq_kernel(u_ref, cp_ref, h0_ref, cw_ref, cb_ref, wa_ref, wx_ref, ba_ref, bx_ref, lam_ref,
                    y_ref, tail_ref, hT_ref, carry_ref, h_ref, a_s, b_s):
    ti = pl.program_id(1)

    @pl.when(ti == 0)
    def _():
        carry_ref[...] = cp_ref[...]
        h_ref[...] = h0_ref[...]

    gate = u_ref[:, 0:C_WIDTH]
    xb = u_ref[:, C_WIDTH:2 * C_WIDTH]
    tm = xb.shape[0]
    carry = carry_ref[...]
    row8 = lax.broadcasted_iota(jnp.int32, (SUBLANES, C_WIDTH), 0)
    xc = cb_ref[...] + cw_ref[C_CONV - 1:C_CONV, :] * xb
    for s in range(1, C_CONV):
        rolled = pltpu.roll(xb, s, axis=0)
        head = jnp.where(row8 < s, pltpu.roll(carry, s, axis=0), rolled[0:SUBLANES, :])
        shifted = jnp.concatenate([head, rolled[SUBLANES:, :]], axis=0)
        xc = xc + cw_ref[C_CONV - 1 - s:C_CONV - s, :] * shifted
    carry_ref[...] = xb[tm - SUBLANES:tm, :]
    tail_ref[...] = xb[tm - SUBLANES:tm, :]

    a, b = _lru_gates(xc, wa_ref, wx_ref, ba_ref, bx_ref, lam_ref)
    a_s[...] = a
    b_s[...] = b

    def body(t, h):
        h = a_s[pl.ds(t, 1), :] * h + b_s[pl.ds(t, 1), :]
        b_s[pl.ds(t, 1), :] = h
        return h

    h_last = lax.fori_loop(0, tm, body, h_ref[...], unroll=8)
    h_ref[...] = h_last
    hT_ref[...] = h_last
    y_ref[...] = _gelu_tanh(gate) * b_s[...]


def _lru_seq(u, n_seq, conv_prev8, h0, conv_w, conv_b, wa, wx, ba, bx, lam, tm):
    rows = u.shape[0]
    per_seq = rows // n_seq // tm
    const = lambda shape: pl.BlockSpec(shape, lambda b, t: (0,) * len(shape))
    return pl.pallas_call(
        _lru_seq_kernel,
        grid=(n_seq, per_seq),
        in_specs=[
            pl.BlockSpec((tm, 2 * C_WIDTH), lambda b, t: (b * per_seq + t, 0)),
            pl.BlockSpec((None, SUBLANES, C_WIDTH), lambda b, t: (b, 0, 0)),
            pl.BlockSpec((None, 1, C_WIDTH), lambda b, t: (b, 0, 0)),
            const((C_CONV, C_WIDTH)), const((1, C_WIDTH)),
            const((C_BLOCKS, C_BLOCK, C_BLOCK)), const((C_BLOCKS, C_BLOCK, C_BLOCK)),
            const((1, C_WIDTH)), const((1, C_WIDTH)), const((1, C_WIDTH)),
        ],
        out_specs=[
            pl.BlockSpec((tm, C_WIDTH), lambda b, t: (b * per_seq + t, 0)),
            pl.BlockSpec((None, SUBLANES, C_WIDTH), lambda b, t: (b, 0, 0)),
            pl.BlockSpec((None, 1, C_WIDTH), lambda b, t: (b, 0, 0)),
        ],
        out_shape=[jax.ShapeDtypeStruct((rows, C_WIDTH), F32),
                   jax.ShapeDtypeStruct((n_seq, SUBLANES, C_WIDTH), F32),
                   jax.ShapeDtypeStruct((n_seq, 1, C_WIDTH), F32)],
        scratch_shapes=[pltpu.VMEM((SUBLANES, C_WIDTH), F32), pltpu.VMEM((1, C_WIDTH), F32),
                        pltpu.VMEM((tm, C_WIDTH), F32), pltpu.VMEM((tm, C_WIDTH), F32)],
        compiler_params=_params("arbitrary", "arbitrary"),
        name="lru_seq",
    )(u, conv_prev8, h0, conv_w, conv_b, wa, wx, ba, bx, lam)


def _lru_step_kernel(u_ref, cp_ref, h0_ref, cw_ref, cb_ref, wa_ref, wx_ref, ba_ref, bx_ref, lam_ref,
                     y_ref, cn_ref, h_ref):
    gate = u_ref[:, 0:C_WIDTH]
    xb = u_ref[:, C_WIDTH:2 * C_WIDTH]
    xc = cb_ref[...] + cw_ref[C_CONV - 1:C_CONV, :] * xb
    for i in range(C_CONV - 1):
        xc = xc + cw_ref[i:i + 1, :] * cp_ref[i]
    a, b = _lru_gates(xc, wa_ref, wx_ref, ba_ref, bx_ref, lam_ref)
    h = a * h0_ref[...] + b
    h_ref[...] = h
    y_ref[...] = _gelu_tanh(gate) * h
    for i in range(C_CONV - 2):
        cn_ref[i] = cp_ref[i + 1]
    cn_ref[C_CONV - 2] = xb


def _lru_step(u, conv_states, h0, j, conv_w, conv_b, wa, wx, ba, bx, lam):
    rows = u.shape[0]
    full = lambda shape: pl.BlockSpec(shape, lambda i: (0,) * len(shape))
    return pl.pallas_call(
        _lru_step_kernel,
        grid=(1,),
        in_specs=[full((rows, 2 * C_WIDTH)),
                  pl.BlockSpec((None, C_CONV - 1, rows, C_WIDTH), lambda i: (j, 0, 0, 0)),
                  full((rows, C_WIDTH)),
                  full((C_CONV, C_WIDTH)), full((1, C_WIDTH)),
                  full((C_BLOCKS, C_BLOCK, C_BLOCK)), full((C_BLOCKS, C_BLOCK, C_BLOCK)),
                  full((1, C_WIDTH)), full((1, C_WIDTH)), full((1, C_WIDTH))],
        out_specs=[full((rows, C_WIDTH)), full((C_CONV - 1, rows, C_WIDTH)), full((rows, C_WIDTH))],
        out_shape=[jax.ShapeDtypeStruct((rows, C_WIDTH), F32),
                   jax.ShapeDtypeStruct((C_CONV - 1, rows, C_WIDTH), F32),
                   jax.ShapeDtypeStruct((rows, C_WIDTH), F32)],
        compiler_params=_params("arbitrary"),
        name="lru_step",
    )(u, conv_states, h0, conv_w, conv_b, wa, wx, ba, bx, lam)


def _pad_rows(w, first, total):
    return jnp.zeros((total, w.shape[1]), w.dtype).at[first:first + w.shape[0]].set(w)


class _Group:
    def __init__(self, x, mod, n_seq, seq_len, states, tm, tf):
        self.x, self.mod, self.n_seq, self.seq_len, self.states, self.tm, self.tf = x, mod, n_seq, seq_len, states, tm, tf
        self.prompt = seq_len > 1
        self.out_shift, self.out_wkv, self.out_gla, self.out_conv, self.out_lru = [], [], [], [], []
        self.new_wkv_t = self.new_gla = None

    def new_states(self):
        wkv = jnp.stack(self.out_wkv) if self.new_wkv_t is None else jnp.transpose(self.new_wkv_t, (0, 4, 1, 2, 3))
        gla = jnp.stack(self.out_gla) if self.new_gla is None else self.new_gla
        return jnp.stack(self.out_shift), wkv, gla, jnp.stack(self.out_conv), jnp.stack(self.out_lru)


def _mixer(grp, l, wts, mxu):
    (w_ffn_up, w_ffn_down, w_in_even, w_out_even, a_mu, a_w_up, a_a_up, a_g_up, a_vec, b_alpha_up,
     b_alpha_b, b_norm_w, w_in_odd, w_out_odd, c_conv_w, c_conv_b, c_wa, c_ba, c_wx, c_bx, c_lam,
     final_norm_w) = wts
    x, mod, n_seq, seq_len, tm, prompt = grp.x, grp.mod, grp.n_seq, grp.seq_len, grp.tm, grp.prompt
    st_shift, st_wkv, st_gla, st_conv, st_lru = grp.states
    rows = x.shape[0]
    j = l // 2
    tn_even, tn_odd = EVEN_COL_TILE, ODD_COL_TILE
    seg = jnp.kron(jnp.eye(A_HEADS, dtype=F32), jnp.ones((A_HEAD_DIM, A_HEAD_DIM), F32)).astype(BF16)
    if l % 2 == 0:
        if prompt:
            u = _inproj(x, mod, l, mxu['in', l], None, tm, tn_even)
        else:
            u, mxu['in', l] = _inproj(x, mod, l, jnp.swapaxes(w_in_even, 1, 2), j, tm, tn_even,
                                      transposed=True)
        wup_pad = _pad_rows(a_w_up[j], 0, LANES)
        aup_pad = _pad_rows(a_a_up[j], A_DECAY_RANK, LANES)
        vec8 = _pad_rows(a_vec[j], 0, SUBLANES)
        alpha_pad = _pad_rows(b_alpha_up[j], 0, LANES)
        alpha_b = b_alpha_b[j].reshape(1, B_KEY_WIDTH)
        mu = a_mu[j].reshape(1, A_PROJ)
        norm_w = b_norm_w[j].reshape(1, B_WIDTH)
        if prompt:
            r, lw, k2, v, kk, al, g = _rwkv_prep(u, st_shift[j].reshape(n_seq, 1, A_PROJ), mu, wup_pad,
                                                 aup_pad, a_g_up[j], vec8, seg, n_seq, TOKENWISE_ROW_TILE, True)
            y_a, h_t = _rwkv_chunk(r, lw, k2, v, kk, al, g, vec8, n_seq, _pair_states(st_wkv[j]))
            s_wkv = _unpair_states(h_t)
            s_shift = u.reshape(n_seq, seq_len, EVEN_PROJ)[:, -1, :A_PROJ]
            y_b, s_gla = _gla_chunk(u, n_seq, alpha_pad, alpha_b, norm_w, st_gla[j])
        else:
            r, lw, k2, v, kk, al, g, bonus = _rwkv_prep(u, st_shift[j], mu, wup_pad, aup_pad, a_g_up[j],
                                                        vec8, seg, rows, rows, False)
            st_wkv_t = jnp.transpose(st_wkv, (0, 2, 3, 4, 1))
            y_raw, grp.new_wkv_t = _rwkv_step(r, lw, k2, v, kk, al, st_wkv_t, j,
                                              jnp.zeros_like(st_wkv_t) if grp.new_wkv_t is None else grp.new_wkv_t)
            y_a = _rwkv_post(y_raw, bonus, g, vec8, seg, rows)
            s_shift = u[:, :A_PROJ]
            y_b, grp.new_gla = _gla_step(u, alpha_pad, alpha_b, norm_w, st_gla, j, STEP_ROWS,
                                         jnp.zeros_like(st_gla) if grp.new_gla is None else grp.new_gla)
        if prompt:
            x = _outproj([y_a, y_b], mxu['out', l], None, x, mod, l, OUTPROJ_ROW_TILE)
        else:
            x, mxu['out', l] = _outproj([y_a, y_b], w_out_even, j, x, mod, l, tm)
        grp.out_shift.append(s_shift)
        if prompt:
            grp.out_wkv.append(s_wkv)
            grp.out_gla.append(s_gla)
    else:
        if prompt:
            u = _inproj(x, mod, l, mxu['in', l], None, tm, tn_odd)
        else:
            u, mxu['in', l] = _inproj(x, mod, l, w_in_odd, j, tm, tn_odd)
        lru_w = (c_conv_w[j], c_conv_b[j].reshape(1, C_WIDTH), c_wa[j], c_wx[j],
                 c_ba[j].reshape(1, C_WIDTH), c_bx[j].reshape(1, C_WIDTH), c_lam[j].reshape(1, C_WIDTH))
        if prompt:
            cp8 = jnp.concatenate([jnp.zeros((n_seq, SUBLANES - (C_CONV - 1), C_WIDTH), F32), st_conv[j]], axis=1)
            y_c, tail, h_t = _lru_seq(u, n_seq, cp8, st_lru[j].reshape(n_seq, 1, C_WIDTH), *lru_w,
                                      TOKENWISE_ROW_TILE)
            s_conv = tail[:, SUBLANES - (C_CONV - 1):, :]
            s_lru = h_t.reshape(n_seq, C_WIDTH)
        else:
            y_c, cn, s_lru = _lru_step(u, jnp.swapaxes(st_conv, 1, 2), st_lru[j], j, *lru_w)
            s_conv = jnp.swapaxes(cn, 0, 1)
        if prompt:
            x = _outproj([y_c], mxu['out', l], None, x, mod, l, OUTPROJ_ROW_TILE)
        else:
            x, mxu['out', l] = _outproj([y_c], w_out_odd, j, x, mod, l, tm)
        grp.out_conv.append(s_conv)
        grp.out_lru.append(s_lru)
    grp.x = x


def _trunk(sample, prompt, wts):
    w_ffn_up, w_ffn_down, final_norm_w = wts[0], wts[1], wts[-1]
    subs = [(l, which) for l in range(DEPTH) for which in (0, 1)]
    mxu = {('ffn', 0, 0): (w_ffn_up[0, 0].astype(BF16), w_ffn_down[0, 0].astype(BF16))}
    for idx, (l, which) in enumerate(subs):
        final_w = final_norm_w if idx == len(subs) - 1 else None
        weights = mxu['ffn', l, which]
        sample.x = _ffn(sample.x, sample.mod, l, which, weights, sample.tm, sample.tf, final_w=final_w)
        if idx + 1 < len(subs):
            nxt = subs[idx + 1]
            prompt.x, mxu[('ffn',) + nxt] = _ffn(prompt.x, prompt.mod, l, which, weights, prompt.tm, prompt.tf,
                                                 convert=(w_ffn_up, w_ffn_down) + nxt)
        else:
            prompt.x = _ffn(prompt.x, prompt.mod, l, which, weights, prompt.tm, prompt.tf, final_w=final_w)
        if which == 0:
            _mixer(sample, l, wts, mxu)
            _mixer(prompt, l, wts, mxu)


def kernel(x_prompt, x_sample, c_prompt, c_sample, state_rwkv_shift, state_rwkv_wkv, state_gla, state_conv, state_lru, w_ada, b_ada, w_ffn_up, w_ffn_down, w_in_even, w_out_even, a_mu, a_w_up, a_a_up, a_g_up, a_vec, b_alpha_up, b_alpha_b, b_norm_w, w_in_odd, w_out_odd, c_conv_w, c_conv_b, c_wa, c_ba, c_wx, c_bx, c_lam, final_norm_w):
    wts = (w_ffn_up, w_ffn_down, w_in_even, w_out_even, a_mu, a_w_up, a_a_up, a_g_up, a_vec, b_alpha_up,
           b_alpha_b, b_norm_w, w_in_odd, w_out_odd, c_conv_w, c_conv_b, c_wa, c_ba, c_wx, c_bx, c_lam,
           final_norm_w)
    n_p, t_p, _ = x_prompt.shape
    n_s, t_s, _ = x_sample.shape
    assert t_s == 1

    c_rows = n_p + n_s
    pad = (-c_rows) % 16
    c_all = jnp.concatenate([c_sample, c_prompt, jnp.zeros((pad, D_MODEL), F32)], axis=0)
    mod_all = _ada(c_all, w_ada, b_ada)
    mod_p = _Mod(mod_all[:, :, n_s:c_rows].reshape(DEPTH, N_MOD, n_p, 1, D_MODEL), False, t_p)
    mod_s = _Mod(mod_all, True, 1)

    fresh = lambda s: jnp.zeros((s.shape[0], n_p) + s.shape[2:], s.dtype)
    states = (state_rwkv_shift, state_rwkv_wkv, state_gla, state_conv, state_lru)
    sample = _Group(x_sample.reshape(n_s, D_MODEL), mod_s, n_s, 1, states, n_s, SAMPLE_FFN_COL_TILE)
    prompt = _Group(x_prompt.reshape(n_p * t_p, D_MODEL), mod_p, n_p, t_p, tuple(fresh(s) for s in states),
                    PROMPT_ROW_TILE, FFN_COL_TILE)
    _trunk(sample, prompt, wts)
    return (prompt.x.reshape(n_p, t_p, D_MODEL), sample.x.reshape(n_s, 1, D_MODEL),
            *prompt.new_states(), *sample.new_states())
```

```python
import functools

import jax
import jax.numpy as jnp
from jax import lax
from jax.experimental import pallas as pl
from jax.experimental.pallas import tpu as pltpu

F32 = jnp.float32
BF16 = jnp.bfloat16

D_MODEL = 2048
DEPTH = 4
N_MOD = 9
D_FF = 5632
NORM_EPS = 1e-6

A_HEADS = 16
A_HEAD_DIM = 64
A_WIDTH = A_HEADS * A_HEAD_DIM
A_DECAY_RANK = 64
A_ICL_RANK = 64
A_GATE_RANK = 128
A_PROJ = 3 * A_WIDTH + A_DECAY_RANK + A_ICL_RANK + A_GATE_RANK
A_GN_EPS = 64e-5

B_HEADS = 4
B_KEY_DIM = 128
B_VAL_DIM = 256
B_KEY_WIDTH = B_HEADS * B_KEY_DIM
B_WIDTH = B_HEADS * B_VAL_DIM
B_ALPHA_RANK = 16
B_TAU = 16.0
B_PROJ = 2 * B_KEY_WIDTH + 2 * B_WIDTH + B_ALPHA_RANK
EVEN_PROJ = A_PROJ + B_PROJ

C_WIDTH = D_MODEL
C_BLOCKS = 8
C_BLOCK = C_WIDTH // C_BLOCKS
C_CONV = 4
C_POW = 8.0

LANES = 128
SUBLANES = 8
VMEM_LIMIT = 56 * 1024 * 1024
FFN_VMEM_LIMIT = 60 * 1024 * 1024

CHUNK = 64
RWKV_STEP_CHUNKS = 2
GLA_STEP_CHUNKS = 2
GLA_SUB = SUBLANES
PROMPT_ROW_TILE = 1024
OUTPROJ_ROW_TILE = 512
TOKENWISE_ROW_TILE = 256
FFN_COL_TILE = 512
SAMPLE_FFN_COL_TILE = 1408
EVEN_COL_TILE = 13 * LANES
ODD_COL_TILE = 8 * LANES
STEP_ROWS = 2 * SUBLANES
ADA_COL_TILE = 1024
NORM_ROWS = 32
NORM_UNROLL = 4
EPILOGUE_ROWS = 128


def _params(*sem, vmem=VMEM_LIMIT):
    return pltpu.CompilerParams(dimension_semantics=sem, vmem_limit_bytes=vmem)


def _bdot(a, b):
    return jnp.dot(a.astype(BF16), b.astype(BF16), preferred_element_type=F32)


def _split3(x):
    hi = x.astype(BF16)
    r1 = x - hi.astype(F32)
    mid = r1.astype(BF16)
    lo = (r1 - mid.astype(F32)).astype(BF16)
    return hi, mid, lo


def _dot_exact_rhs(a_bf16, x):
    hi, mid, lo = _split3(x)
    f = lambda y: jnp.dot(a_bf16, y, preferred_element_type=F32)
    return f(hi) + f(mid) + f(lo)


def _dot_exact_lhs(x, b_bf16):
    hi, mid, lo = _split3(x)
    f = lambda y: jnp.dot(y, b_bf16, preferred_element_type=F32)
    return f(hi) + f(mid) + f(lo)


def _sigmoid(x):
    return jax.nn.sigmoid(x)


def _softplus(x):
    return jnp.maximum(x, 0.0) + jnp.log1p(jnp.exp(-jnp.abs(x)))


def _gelu_tanh(x):
    return 0.5 * x * (1.0 + jnp.tanh(0.7978845608028654 * (x + 0.044715 * (x * x * x))))


def _norm_mod(x, shift, scale):
    ms = jnp.mean(x * x, axis=-1, keepdims=True)
    return x * lax.rsqrt(ms + NORM_EPS) * (1.0 + scale) + shift


def _ada_kernel(c_ref, w_ref, b_ref, o_ref):
    c = c_ref[...]
    act = (c * _sigmoid(c)).astype(BF16)
    o_ref[...] = jnp.dot(act, w_ref[...].astype(BF16), preferred_element_type=F32) + b_ref[...]


def _ada(c_all, w_ada, b_ada):
    rows = c_all.shape[0]
    tn = ADA_COL_TILE
    per = D_MODEL // tn
    b4 = b_ada.reshape(DEPTH, N_MOD * per, 1, tn)
    return pl.pallas_call(
        _ada_kernel,
        grid=(DEPTH, N_MOD * per),
        in_specs=[
            pl.BlockSpec((rows, D_MODEL), lambda l, j: (0, 0)),
            pl.BlockSpec((None, D_MODEL, tn), lambda l, j: (l, 0, j)),
            pl.BlockSpec((None, None, 1, tn), lambda l, j: (l, j, 0, 0)),
        ],
        out_specs=pl.BlockSpec((None, None, rows, tn), lambda l, j: (l, j // per, 0, j % per)),
        out_shape=jax.ShapeDtypeStruct((DEPTH, N_MOD, rows, D_MODEL), F32),
        compiler_params=_params("arbitrary", "arbitrary"),
        name="ada",
    )(c_all, w_ada, b4)


class _Mod:
    def __init__(self, arr, per_row, seq_len):
        self.arr = arr
        self.per_row = per_row
        self.seq_len = seq_len

    def spec(self, layer, m, tm):
        if self.per_row:
            return pl.BlockSpec((None, None, tm, D_MODEL), lambda i, *_: (layer, m, i, 0))
        per_seq = self.seq_len // tm
        return pl.BlockSpec((None, None, None, 1, D_MODEL), lambda i, *_: (layer, m, i // per_seq, 0, 0))


def _ffn_kernel(x_ref, sh_ref, sc_ref, g_ref, wg_ref, wu_ref, wd_ref, *rest, n_j, psub, sub, final, convert):
    rest = list(rest)
    fw_ref = rest.pop(0) if final else None
    f32_tiles = [rest.pop(0) for _ in range(2)] if convert else []
    o_ref = rest.pop(0)
    bf16_tiles = [rest.pop(0) for _ in range(2)] if convert else []
    h_ref, = rest
    j = pl.program_id(1)
    tm = x_ref.shape[0]

    @pl.when(j == 0)
    def _():
        def body(s, carry):
            rows = pl.ds(pl.multiple_of(s * psub, psub), psub)
            sh = sh_ref[...] if sh_ref.shape[0] == 1 else sh_ref[rows, :]
            sc = sc_ref[...] if sc_ref.shape[0] == 1 else sc_ref[rows, :]
            h_ref[rows, :] = _norm_mod(x_ref[rows, :], sh, sc).astype(BF16)
            o_ref[rows, :] = jnp.zeros((psub, D_MODEL), F32)
            return carry
        lax.fori_loop(0, tm // psub, body, 0, unroll=NORM_UNROLL)

    for src, dst in zip(f32_tiles, bf16_tiles):
        dst[...] = src[...].astype(BF16)

    h = h_ref[...]
    gt = jnp.dot(h, wg_ref[...], preferred_element_type=F32)
    ut = jnp.dot(h, wu_ref[...], preferred_element_type=F32)
    act = (gt * _sigmoid(gt) * ut).astype(BF16)
    o_ref[...] += jnp.dot(act, wd_ref[...], preferred_element_type=F32)

    @pl.when(j == n_j - 1)
    def _():
        def body(s, carry):
            rows = pl.ds(pl.multiple_of(s * sub, sub), sub)
            g = g_ref[...] if g_ref.shape[0] == 1 else g_ref[rows, :]
            y = x_ref[rows, :] + 0.5 * g * o_ref[rows, :]
            if final:
                ms = jnp.mean(y * y, axis=-1, keepdims=True)
                y = y * lax.rsqrt(ms + NORM_EPS) * fw_ref[...]
            o_ref[rows, :] = y
            return carry
        lax.fori_loop(0, tm // sub, body, 0)


def _ffn(x, mod, layer, which, weights, tm, tf, final_w=None, convert=None):
    m_rows = x.shape[0]
    n_i = m_rows // tm
    n_j = D_FF // tf
    m0 = 6 * which
    sub = min(tm, EPILOGUE_ROWS)
    final = final_w is not None
    in_specs = [
        pl.BlockSpec((tm, D_MODEL), lambda i, j: (i, 0)),
        mod.spec(layer, m0, tm), mod.spec(layer, m0 + 1, tm), mod.spec(layer, m0 + 2, tm),
        pl.BlockSpec((D_MODEL, tf), lambda i, j: (0, j)),
        pl.BlockSpec((D_MODEL, tf), lambda i, j: (0, j + n_j)),
        pl.BlockSpec((tf, D_MODEL), lambda i, j: (j, 0)),
    ]
    args = [x, mod.arr, mod.arr, mod.arr, weights[0], weights[0], weights[1]]
    if final:
        in_specs.append(pl.BlockSpec((1, D_MODEL), lambda i, j: (0, 0)))
        args.append(final_w.reshape(1, D_MODEL))
    out_specs = [pl.BlockSpec((tm, D_MODEL), lambda i, j: (i, 0))]
    out_shape = [jax.ShapeDtypeStruct((m_rows, D_MODEL), F32)]
    if convert is not None:
        w_up32, w_down32, l2, s2 = convert
        up_tile = (D_MODEL // n_i, 2 * D_FF // n_j)
        down_tile = (D_FF // n_j, D_MODEL // n_i)
        assert up_tile[0] * n_i == D_MODEL and up_tile[1] * n_j == 2 * D_FF and up_tile[1] % LANES == 0
        assert down_tile[0] * n_j == D_FF and down_tile[1] * n_i == D_MODEL and down_tile[1] % LANES == 0
        in_specs += [pl.BlockSpec((None, None) + up_tile, lambda i, j: (l2, s2, i, j)),
                     pl.BlockSpec((None, None) + down_tile, lambda i, j: (l2, s2, j, i))]
        args += [w_up32, w_down32]
        out_specs += [pl.BlockSpec(up_tile, lambda i, j: (i, j)), pl.BlockSpec(down_tile, lambda i, j: (j, i))]
        out_shape += [jax.ShapeDtypeStruct((D_MODEL, 2 * D_FF), BF16), jax.ShapeDtypeStruct((D_FF, D_MODEL), BF16)]
    out = pl.pallas_call(
        functools.partial(_ffn_kernel, n_j=n_j, psub=min(tm, NORM_ROWS), sub=sub, final=final,
                          convert=convert is not None),
        grid=(n_i, n_j),
        in_specs=in_specs,
        out_specs=out_specs,
        out_shape=out_shape,
        scratch_shapes=[pltpu.VMEM((tm, D_MODEL), BF16)],
        compiler_params=_params("arbitrary", "arbitrary", vmem=FFN_VMEM_LIMIT),
        name="ffn",
    )(*args)
    return (out[0], (out[1], out[2])) if convert is not None else out[0]


def _inproj_kernel(x_ref, sh_ref, sc_ref, w_ref, o_ref, *rest, sub, emit, transposed):
    h_ref = rest[-1]
    j = pl.program_id(1)
    tm = x_ref.shape[0]

    @pl.when(j == 0)
    def _():
        def body(s, carry):
            rows = pl.ds(pl.multiple_of(s * sub, sub), sub)
            sh = sh_ref[...] if sh_ref.shape[0] == 1 else sh_ref[rows, :]
            sc = sc_ref[...] if sc_ref.shape[0] == 1 else sc_ref[rows, :]
            h_ref[rows, :] = _norm_mod(x_ref[rows, :], sh, sc).astype(BF16)
            return carry
        lax.fori_loop(0, tm // sub, body, 0, unroll=NORM_UNROLL)

    w = w_ref[...]
    if transposed:
        w = w.T
    if emit:
        w = w.astype(BF16)
        rest[0][...] = w
    o_ref[...] = jnp.dot(h_ref[...], w, preferred_element_type=F32)


def _inproj(x, mod, layer, w, widx, tm, tn, transposed=False):
    m_rows = x.shape[0]
    n_out = w.shape[-2] if transposed else w.shape[-1]
    sub = min(tm, NORM_ROWS)
    emit = widx is not None
    if transposed:
        w_spec = pl.BlockSpec((None, tn, D_MODEL), lambda i, j: (widx, j, 0))
    elif emit:
        w_spec = pl.BlockSpec((None, D_MODEL, tn), lambda i, j: (widx, 0, j))
    else:
        w_spec = pl.BlockSpec((D_MODEL, tn), lambda i, j: (0, j))
    out_specs = [pl.BlockSpec((tm, tn), lambda i, j: (i, j))]
    out_shape = [jax.ShapeDtypeStruct((m_rows, n_out), F32)]
    if emit:
        assert m_rows == tm
        out_specs.append(pl.BlockSpec((D_MODEL, tn), lambda i, j: (0, j)))
        out_shape.append(jax.ShapeDtypeStruct((D_MODEL, n_out), BF16))
    out = pl.pallas_call(
        functools.partial(_inproj_kernel, sub=sub, emit=emit, transposed=transposed),
        grid=(m_rows // tm, pl.cdiv(n_out, tn)),
        in_specs=[
            pl.BlockSpec((tm, D_MODEL), lambda i, j: (i, 0)),
            mod.spec(layer, 3, tm), mod.spec(layer, 4, tm),
            w_spec,
        ],
        out_specs=out_specs,
        out_shape=out_shape,
        scratch_shapes=[pltpu.VMEM((tm, D_MODEL), BF16)],
        compiler_params=_params("arbitrary", "arbitrary"),
        name="inproj",
    )(x, mod.arr, mod.arr, w)
    return tuple(out) if emit else out[0]


def _outproj_kernel(*refs, n_in, emit):
    y_refs = refs[:n_in]
    w_refs = refs[n_in:2 * n_in]
    x_ref, g_ref, o_ref = refs[2 * n_in:2 * n_in + 3]
    acc = None
    for k, (y_ref, w_ref) in enumerate(zip(y_refs, w_refs)):
        w = w_ref[...]
        if emit:
            w = w.astype(BF16)
            refs[2 * n_in + 3 + k][...] = w
        part = jnp.dot(y_ref[...].astype(BF16), w, preferred_element_type=F32)
        acc = part if acc is None else acc + part
    o_ref[...] = x_ref[...] + g_ref[...] * acc


def _outproj(ys, w, widx, x, mod, layer, tm):
    m_rows = x.shape[0]
    n_in = len(ys)
    kw = ys[0].shape[1]
    emit = widx is not None
    in_specs = [pl.BlockSpec((tm, kw), lambda i: (i, 0)) for _ in ys]
    if emit:
        assert m_rows == tm
        in_specs += [pl.BlockSpec((None, kw, D_MODEL), lambda i, k=k: (widx, k, 0)) for k in range(n_in)]
        w_args = [w] * n_in
    else:
        in_specs += [pl.BlockSpec((kw, D_MODEL), lambda i: (0, 0))] * n_in
        w_args = list(w)
    in_specs += [pl.BlockSpec((tm, D_MODEL), lambda i: (i, 0)), mod.spec(layer, 5, tm)]
    out_specs = [pl.BlockSpec((tm, D_MODEL), lambda i: (i, 0))]
    out_shape = [jax.ShapeDtypeStruct((m_rows, D_MODEL), F32)]
    if emit:
        out_specs += [pl.BlockSpec((kw, D_MODEL), lambda i: (0, 0))] * n_in
        out_shape += [jax.ShapeDtypeStruct((kw, D_MODEL), BF16)] * n_in
    out = pl.pallas_call(
        functools.partial(_outproj_kernel, n_in=n_in, emit=emit),
        grid=(m_rows // tm,),
        in_specs=in_specs,
        out_specs=out_specs,
        out_shape=out_shape,
        compiler_params=_params("arbitrary"),
        name="outproj",
    )(*ys, *w_args, x, mod.arr)
    return (out[0], tuple(out[1:])) if emit else out[0]


def _rwkv_prep_kernel(u_ref, prev_ref, mu_ref, wup_ref, aup_ref, gup_ref, vec_ref, *rest, seq_mode):
    if seq_mode:
        r_ref, lw_ref, k_ref, v_ref, kk_ref, a_ref, g_ref, carry_ref = rest
    else:
        seg_ref, r_ref, lw_ref, k_ref, v_ref, kk_ref, a_ref, g_ref, bonus_ref = rest
    u = u_ref[...]
    tm = u.shape[0]
    if seq_mode:
        @pl.when(pl.program_id(1) == 0)
        def _():
            carry_ref[...] = prev_ref[...]
        rolled = pltpu.roll(u, 1, axis=0)
        row = lax.broadcasted_iota(jnp.int32, u.shape, 0)
        prev = jnp.where(row == 0, carry_ref[...], rolled)
        carry_ref[...] = u[tm - 1:tm, :]
    else:
        prev = prev_ref[...]
    xs = u + mu_ref[...] * (prev - u)
    r = xs[:, 0:A_WIDTH]
    k = xs[:, A_WIDTH:2 * A_WIDTH]
    v = xs[:, 2 * A_WIDTH:3 * A_WIDTH]
    wa_lo = xs[:, 3 * A_WIDTH:3 * A_WIDTH + LANES]
    g_lo = xs[:, 3 * A_WIDTH + LANES:]
    w0, a0, k_k, k_a, r_k = (vec_ref[i:i + 1, :] for i in range(5))
    w_raw = -_softplus(-(w0 + _bdot(jnp.tanh(wa_lo), wup_ref[...]))) - 0.5
    lw = -jnp.exp(w_raw)
    a = _sigmoid(a0 + _bdot(wa_lo, aup_ref[...]))
    g = _bdot(_sigmoid(g_lo), gup_ref[...])
    kk = k * k_k
    k2 = k * (1.0 + (a - 1.0) * k_a)
    if not seq_mode:
        seg = seg_ref[...]
        kk = kk * lax.rsqrt(jnp.maximum(_dot_exact_lhs(kk * kk, seg), 1e-24))
        bonus_ref[...] = _dot_exact_lhs(r * k2 * r_k, seg) * v
    r_ref[...] = r
    lw_ref[...] = lw
    k_ref[...] = k2
    v_ref[...] = v
    kk_ref[...] = kk
    a_ref[...] = a
    g_ref[...] = g


def _rwkv_prep(u, prev, mu, wup_pad, aup_pad, g_up, vec, seg, n_seq, tm, seq_mode):
    rows = u.shape[0]
    per_seq = rows // n_seq // tm if seq_mode else 1
    grid = (n_seq, per_seq) if seq_mode else (rows // tm, 1)
    rowmap = (lambda b, t: (b * per_seq + t, 0)) if seq_mode else (lambda b, t: (b, 0))
    prev_spec = (pl.BlockSpec((None, 1, A_PROJ), lambda b, t: (b, 0, 0)) if seq_mode
                 else pl.BlockSpec((tm, A_PROJ), rowmap))
    const = lambda shape: pl.BlockSpec(shape, lambda b, t: (0,) * len(shape))
    out = jax.ShapeDtypeStruct((rows, A_WIDTH), F32)
    in_specs = [pl.BlockSpec((tm, A_PROJ), rowmap), prev_spec,
                const((1, A_PROJ)), const((LANES, A_WIDTH)), const((LANES, A_WIDTH)),
                const((A_GATE_RANK, A_WIDTH)), const((SUBLANES, A_WIDTH))]
    args = [u, prev, mu, wup_pad, aup_pad, g_up, vec]
    if not seq_mode:
        in_specs.append(const((A_WIDTH, A_WIDTH)))
        args.append(seg)
    n_out = 7 if seq_mode else 8
    return pl.pallas_call(
        functools.partial(_rwkv_prep_kernel, seq_mode=seq_mode),
        grid=grid,
        in_specs=in_specs,
        out_specs=[pl.BlockSpec((tm, A_WIDTH), rowmap)] * n_out,
        out_shape=[out] * n_out,
        scratch_shapes=[pltpu.VMEM((1, A_PROJ), F32)] if seq_mode else [],
        compiler_params=_params("arbitrary", "arbitrary"),
        name="rwkv_prep",
    )(*args)


def _rwkv_post_kernel(y_ref, bonus_ref, g_ref, vec_ref, seg_ref, o_ref):
    y = y_ref[...]
    seg = seg_ref[...]
    inv_n = 1.0 / A_HEAD_DIM
    yc = y - _dot_exact_lhs(y, seg) * inv_n
    var = _dot_exact_lhs(yc * yc, seg) * inv_n
    yn = yc * lax.rsqrt(var + A_GN_EPS) * vec_ref[5:6, :] + vec_ref[6:7, :]
    o_ref[...] = (yn + bonus_ref[...]) * g_ref[...]


def _rwkv_post(y, bonus, g, vec, seg, tm):
    rows = y.shape[0]
    spec = pl.BlockSpec((tm, A_WIDTH), lambda i: (i, 0))
    return pl.pallas_call(
        _rwkv_post_kernel,
        grid=(rows // tm,),
        in_specs=[spec, spec, spec,
                  pl.BlockSpec((SUBLANES, A_WIDTH), lambda i: (0, 0)),
                  pl.BlockSpec((A_WIDTH, A_WIDTH), lambda i: (0, 0))],
        out_specs=spec,
        out_shape=jax.ShapeDtypeStruct((rows, A_WIDTH), F32),
        compiler_params=_params("arbitrary"),
        name="rwkv_post",
    )(y, bonus, g, vec, seg)


def _rwkv_chunk_kernel(r_ref, lw_ref, k_ref, v_ref, kk_ref, a_ref, g_ref, vec_ref, h0_ref, y_ref, hT_ref, h_ref):
    ci = pl.program_id(1)

    @pl.when(ci == 0)
    def _():
        h_ref[...] = h0_ref[...]

    cs = CHUNK
    n2 = 2 * cs
    npair = A_WIDTH // LANES
    pairs = lambda x: jnp.stack([x[:, p * LANES:(p + 1) * LANES] for p in range(npair)])
    vec = vec_ref[...]
    vrow = lambda i: pairs(vec[i:i + 1, :])
    lane = lax.broadcasted_iota(jnp.int32, (1, 1, LANES), 2)
    m0 = (lane < A_HEAD_DIM).astype(F32)
    m1 = 1.0 - m0
    split = lambda x: jnp.concatenate([x * m0, x * m1], axis=1)
    dup = lambda x: jnp.concatenate([x, x], axis=1)
    own = jnp.concatenate([jnp.broadcast_to(m0, (1, cs, LANES)), jnp.broadcast_to(m1, (1, cs, LANES))], axis=1)
    bf = lambda x: x.astype(BF16)

    def bmm(x, y):
        return jnp.einsum('hab,hbc->hac', x, y, preferred_element_type=F32)

    def bmm_nt(x, y):
        return jnp.einsum('han,hbn->hab', x, y, preferred_element_type=F32)

    def bmm_tn(x, y):
        return jnp.einsum('hca,hcb->hab', x, y, preferred_element_type=F32)

    row = lax.broadcasted_iota(jnp.int32, (cs, cs), 0)
    col = lax.broadcasted_iota(jnp.int32, (cs, cs), 1)
    tri = (col <= row).astype(BF16)
    row2 = lax.broadcasted_iota(jnp.int32, (n2, n2), 0)
    col2 = lax.broadcasted_iota(jnp.int32, (n2, n2), 1)
    same = (row2 >= cs) == (col2 >= cs)
    tok_r = jnp.where(row2 >= cs, row2 - cs, row2)
    tok_c = jnp.where(col2 >= cs, col2 - cs, col2)
    strict = jnp.where(same, (tok_c < tok_r).astype(F32), 0.0)
    incl = jnp.where(same, (tok_c <= tok_r).astype(F32), 0.0)
    eye2 = (row2 == col2).astype(F32)
    nrow = lax.broadcasted_iota(jnp.int32, (LANES, LANES), 0)
    ncol = lax.broadcasted_iota(jnp.int32, (LANES, LANES), 1)
    eye_n = (nrow == ncol).astype(F32)

    h = h_ref[...]
    for sub in range(r_ref.shape[0] // cs):
        rows = slice(sub * cs, (sub + 1) * cs)
        lw_all = lw_ref[rows, :]
        c = pairs(_dot_exact_rhs(tri, lw_all))
        lw = pairs(lw_all)
        c_end = c[:, cs - 1:cs, :]
        e_pos = dup(jnp.exp(c))
        e_neg = dup(jnp.exp(-c))
        e_prev = dup(jnp.exp(c - lw))
        e_end = dup(jnp.exp(c_end - c))
        g_end = jnp.exp(c_end)

        kk = split(pairs(kk_ref[rows, :]))
        kk = kk * lax.rsqrt(jnp.maximum(jnp.sum(kk * kk, axis=-1, keepdims=True), 1e-24))
        b_vec = kk * dup(pairs(a_ref[rows, :]))
        r2 = split(pairs(r_ref[rows, :]))
        k2 = split(pairs(k_ref[rows, :]))
        v2 = split(pairs(v_ref[rows, :]))
        bonus = jnp.sum(r2 * k2 * vrow(4), axis=-1, keepdims=True) * v2
        rt = r2 * e_pos
        at = bf(-kk * e_prev)
        bt = bf(b_vec * e_neg)
        kt = bf(k2 * e_neg)
        bh = bf(b_vec * e_end)
        kh = bf(k2 * e_end)
        v2b = bf(v2)

        gram = bmm_nt(jnp.concatenate([at, bf(rt)], axis=1), jnp.concatenate([bt, kt], axis=1))
        a_ab = gram[:, :n2, :n2] * strict
        a_ak = gram[:, :n2, n2:] * strict
        m_b = gram[:, n2:, :n2] * incl
        m_k = gram[:, n2:, n2:] * incl

        tinv = eye2 + a_ab
        a_b = bf(a_ab)
        apow = bmm(a_b, a_b)
        span = 2
        while 2 * span < cs:
            ap_b = bf(apow)
            both = bmm(jnp.concatenate([bf(tinv), ap_b], axis=1), ap_b)
            tinv = tinv + both[:, :n2]
            apow = both[:, n2:]
            span *= 2
        tinv = tinv + bmm(bf(tinv), bf(apow))

        akv = bmm(bf(a_ak), v2b)
        pq = bf(bmm(bf(tinv), jnp.concatenate([at, bf(akv)], axis=-1)))
        mpq = bmm(bf(m_b), pq)
        p2 = rt + mpq[..., :LANES]
        y0 = mpq[..., LANES:] + bmm(bf(m_k), v2b)
        tpq = bmm_tn(bh, pq)
        gm = eye_n * g_end + tpq[..., :LANES]
        hadd = tpq[..., LANES:] + bmm_tn(kh, v2b)

        yh = bmm(jnp.concatenate([bf(p2), bf(gm)], axis=1), bf(h))
        y2 = yh[:, :n2] + y0
        h = yh[:, n2:] + hadd

        inv_n = 1.0 / A_HEAD_DIM
        yc = (y2 - jnp.sum(y2, axis=-1, keepdims=True) * inv_n) * own
        var = jnp.sum(yc * yc, axis=-1, keepdims=True) * inv_n
        z = yc * lax.rsqrt(var + A_GN_EPS) * vrow(5) + vrow(6) * own + bonus
        out = (z[:, :cs] + z[:, cs:]) * pairs(g_ref[rows, :])
        for p in range(npair):
            y_ref[rows, p * LANES:(p + 1) * LANES] = out[p]

    h_ref[...] = h

    @pl.when(ci == pl.num_programs(1) - 1)
    def _():
        hT_ref[...] = h


def _rwkv_chunk(r, lw, k, v, kk, a, g, vec, n_seq, h0):
    rows = r.shape[0]
    step_rows = RWKV_STEP_CHUNKS * CHUNK
    nc = rows // n_seq // step_rows
    npair = A_WIDTH // LANES
    spec = pl.BlockSpec((step_rows, A_WIDTH), lambda b, c: (b * nc + c, 0))
    hspec = pl.BlockSpec((None, npair, LANES, LANES), lambda b, c: (b, 0, 0, 0))
    return pl.pallas_call(
        _rwkv_chunk_kernel,
        grid=(n_seq, nc),
        in_specs=[spec] * 7 + [pl.BlockSpec((SUBLANES, A_WIDTH), lambda b, c: (0, 0)), hspec],
        out_specs=[spec, hspec],
        out_shape=[jax.ShapeDtypeStruct((rows, A_WIDTH), F32),
                   jax.ShapeDtypeStruct((n_seq, npair, LANES, LANES), F32)],
        scratch_shapes=[pltpu.VMEM((npair, LANES, LANES), F32)],
        compiler_params=_params("arbitrary", "arbitrary"),
        name="rwkv_chunk",
    )(r, lw, k, v, kk, a, g, vec, h0)


def _pair_states(s):
    n = s.shape[0]
    nd = A_HEAD_DIM
    h = jnp.swapaxes(s, -1, -2).reshape(n, A_HEADS // 2, 2, nd, nd)
    out = jnp.zeros((n, A_HEADS // 2, 2 * nd, 2 * nd), s.dtype)
    return out.at[:, :, :nd, :nd].set(h[:, :, 0]).at[:, :, nd:, nd:].set(h[:, :, 1])


def _unpair_states(hp):
    n = hp.shape[0]
    nd = A_HEAD_DIM
    h = jnp.stack([hp[:, :, :nd, :nd], hp[:, :, nd:, nd:]], axis=2).reshape(n, A_HEADS, nd, nd)
    return jnp.swapaxes(h, -1, -2)


def _rwkv_step_kernel(r_ref, lw_ref, k_ref, v_ref, kk_ref, a_ref, s_ref, acc_ref, y_ref, so_ref,
                      r_t, w_t, k_t, v_t, nkk_t, b_t, y_t):
    h = pl.program_id(0)
    nd = A_HEAD_DIM

    @pl.when(h == 0)
    def _():
        kk = kk_ref[...]
        r_t[...] = r_ref[...].T
        w_t[...] = jnp.exp(lw_ref[...]).T
        k_t[...] = k_ref[...].T
        v_t[...] = v_ref[...].T
        nkk_t[...] = (-kk).T
        b_t[...] = (kk * a_ref[...]).T

    first = pl.multiple_of(h * nd, nd)
    rows = pl.ds(first, nd)
    r, w, k, nkk, b = r_t[rows, :], w_t[rows, :], k_t[rows, :], nkk_t[rows, :], b_t[rows, :]

    def body(i, carry):
        s = s_ref[i]
        s_kk = jnp.sum(s * nkk, axis=0, keepdims=True)
        s_new = s * w + s_kk * b + v_t[pl.ds(first + i, 1), :] * k
        so_ref[i] = s_new
        y_t[pl.ds(first + i, 1), :] = jnp.sum(s_new * r, axis=0, keepdims=True)
        return carry

    lax.fori_loop(0, nd, body, 0, unroll=4)

    @pl.when(h == pl.num_programs(0) - 1)
    def _():
        y_ref[...] = y_t[...].T


def _rwkv_step(r, lw, k, v, kk, a, states_t, j, new_states):
    n = r.shape[0]
    nd = A_HEAD_DIM
    rspec = pl.BlockSpec((n, A_WIDTH), lambda h: (0, 0))
    sspec = pl.BlockSpec((None, None, nd, nd, n), lambda h: (j, h, 0, 0, 0))
    return pl.pallas_call(
        _rwkv_step_kernel,
        grid=(A_HEADS,),
        in_specs=[rspec] * 6 + [sspec, pl.BlockSpec(memory_space=pl.ANY)],
        out_specs=[rspec, sspec],
        out_shape=[jax.ShapeDtypeStruct((n, A_WIDTH), F32), jax.ShapeDtypeStruct(states_t.shape, F32)],
        input_output_aliases={7: 1},
        scratch_shapes=[pltpu.VMEM((A_WIDTH, n), F32)] * 7,
        compiler_params=_params("arbitrary"),
        name="rwkv_step",
    )(r, lw, k, v, kk, a, states_t, new_states)


def _gla_log_alpha(a128, aup_ref, ab_ref):
    lane = lax.broadcasted_iota(jnp.int32, a128.shape, 1)
    a_lo = jnp.where(lane < B_ALPHA_RANK, a128, 0.0)
    x = _bdot(a_lo, aup_ref[...]) + ab_ref[...]
    return -_softplus(-x) * (1.0 / B_TAU)


def _gla_block(u_ref, aup_ref, ab_ref, nw_ref, o_ref, s_ref):
    ub = u_ref[...]
    cs = ub.shape[0]
    q_all = ub[:, 0:B_KEY_WIDTH] * (B_KEY_DIM ** -0.5)
    k_all = ub[:, B_KEY_WIDTH:2 * B_KEY_WIDTH]
    v_all = ub[:, 2 * B_KEY_WIDTH:2 * B_KEY_WIDTH + B_WIDTH]
    g_all = ub[:, 2 * B_KEY_WIDTH + B_WIDTH:2 * B_KEY_WIDTH + 2 * B_WIDTH]
    a128 = ub[:, 2 * B_KEY_WIDTH + 2 * B_WIDTH:2 * B_KEY_WIDTH + 2 * B_WIDTH + LANES]
    log_a = _gla_log_alpha(a128, aup_ref, ab_ref)

    row = lax.broadcasted_iota(jnp.int32, (cs, cs), 0)
    col = lax.broadcasted_iota(jnp.int32, (cs, cs), 1)
    cum = _dot_exact_rhs((col <= row).astype(BF16), log_a)
    e_pos = jnp.exp(cum)
    cum_end = cum[cs - 1:cs, :]
    e_end = jnp.exp(cum_end - cum)
    g_end = jnp.exp(cum_end)
    nk = B_KEY_DIM
    eye_k = (lax.broadcasted_iota(jnp.int32, (nk, nk), 0) == lax.broadcasted_iota(jnp.int32, (nk, nk), 1)).astype(F32)

    nblk = cs // GLA_SUB
    kw = B_KEY_WIDTH
    cum3 = cum.reshape(nblk, GLA_SUB, kw)
    before = jnp.concatenate([jnp.zeros((1, 1, kw), F32), cum3[:nblk - 1, GLA_SUB - 1:GLA_SUB, :]], axis=0)
    q_rel = q_all * jnp.exp(cum - jnp.broadcast_to(before, (nblk, GLA_SUB, kw)).reshape(cs, kw))
    tok = lax.broadcasted_iota(jnp.int32, (cs, 1), 0)
    q_parts, k_parts = [], []
    for blk in range(1, nblk):
        first = blk * GLA_SUB
        k_parts.append(jnp.where(tok < first, k_all * jnp.exp(jnp.minimum(before[blk] - cum, 0.0)), 0.0))
        q_parts.append(jnp.where((tok >= first) & (tok < first + GLA_SUB), q_rel, 0.0))

    q3 = q_all.reshape(nblk, GLA_SUB, kw)
    k3 = k_all.reshape(nblk, GLA_SUB, kw)
    v3 = v_all.reshape(nblk, GLA_SUB, B_WIDTH)
    sub_i = lax.broadcasted_iota(jnp.int32, (1, GLA_SUB, 1), 1)
    o_diag = [jnp.zeros((nblk, GLA_SUB, B_VAL_DIM), F32) for _ in range(B_HEADS)]
    for jj in range(GLA_SUB):
        pair = q3 * k3[:, jj:jj + 1, :] * jnp.exp(jnp.minimum(cum3 - cum3[:, jj:jj + 1, :], 0.0))
        pair = jnp.where(sub_i >= jj, pair, 0.0)
        for h in range(B_HEADS):
            score = jnp.sum(pair[:, :, h * nk:(h + 1) * nk], axis=-1, keepdims=True)
            o_diag[h] = o_diag[h] + score * v3[:, jj:jj + 1, h * B_VAL_DIM:(h + 1) * B_VAL_DIM]

    for h in range(B_HEADS):
        ks = slice(h * B_KEY_DIM, (h + 1) * B_KEY_DIM)
        vs = slice(h * B_VAL_DIM, (h + 1) * B_VAL_DIM)
        qe = q_all[:, ks] * e_pos[:, ks]
        ke = k_all[:, ks] * e_end[:, ks]
        vh = v_all[:, vs]
        s = s_ref[h]
        q_cat = jnp.concatenate([part[:, ks] for part in q_parts], axis=-1).astype(BF16)
        k_cat = jnp.concatenate([part[:, ks] for part in k_parts], axis=-1).astype(BF16)
        scores = jnp.einsum('id,jd->ij', q_cat, k_cat, preferred_element_type=F32)
        o = _bdot(scores, vh) + _bdot(qe, s) + o_diag[h].reshape(cs, B_VAL_DIM)
        g_col = jnp.sum(eye_k * g_end[:, ks], axis=-1, keepdims=True)
        s_ref[h] = s * g_col + jnp.einsum('jd,jv->dv', ke.astype(BF16), vh.astype(BF16),
                                           preferred_element_type=F32)
        ms = jnp.mean(o * o, axis=-1, keepdims=True)
        gh = g_all[:, vs]
        o_ref[:, vs] = o * lax.rsqrt(ms + NORM_EPS) * nw_ref[:, vs] * (gh * _sigmoid(gh))


def _gla_chunk_kernel(u_ref, aup_ref, ab_ref, nw_ref, s0_ref, o_ref, sT_ref, s_ref):
    ci = pl.program_id(1)

    @pl.when(ci == 0)
    def _():
        s_ref[...] = s0_ref[...]

    for sub in range(u_ref.shape[0] // CHUNK):
        rows = pl.ds(sub * CHUNK, CHUNK)
        _gla_block(u_ref.at[rows, :], aup_ref, ab_ref, nw_ref, o_ref.at[rows, :], s_ref)

    @pl.when(ci == pl.num_programs(1) - 1)
    def _():
        sT_ref[...] = s_ref[...]


def _gla_chunk(u, n_seq, alpha_up_pad, alpha_b, norm_w, s0):
    rows = u.shape[0]
    step_rows = GLA_STEP_CHUNKS * CHUNK
    nc = rows // n_seq // step_rows
    return pl.pallas_call(
        _gla_chunk_kernel,
        grid=(n_seq, nc),
        in_specs=[
            pl.BlockSpec((step_rows, A_PROJ), lambda b, c: (b * nc + c, 1)),
            pl.BlockSpec((LANES, B_KEY_WIDTH), lambda b, c: (0, 0)),
            pl.BlockSpec((1, B_KEY_WIDTH), lambda b, c: (0, 0)),
            pl.BlockSpec((1, B_WIDTH), lambda b, c: (0, 0)),
            pl.BlockSpec((None, B_HEADS, B_KEY_DIM, B_VAL_DIM), lambda b, c: (b, 0, 0, 0)),
        ],
        out_specs=[
            pl.BlockSpec((step_rows, B_WIDTH), lambda b, c: (b * nc + c, 0)),
            pl.BlockSpec((None, B_HEADS, B_KEY_DIM, B_VAL_DIM), lambda b, c: (b, 0, 0, 0)),
        ],
        out_shape=[jax.ShapeDtypeStruct((rows, B_WIDTH), F32),
                   jax.ShapeDtypeStruct((n_seq, B_HEADS, B_KEY_DIM, B_VAL_DIM), F32)],
        scratch_shapes=[pltpu.VMEM((B_HEADS, B_KEY_DIM, B_VAL_DIM), F32)],
        compiler_params=_params("arbitrary", "arbitrary"),
        name="gla_chunk",
    )(u, alpha_up_pad, alpha_b, norm_w, s0)


def _gla_step_kernel(u_ref, aup_ref, ab_ref, nw_ref, s_ref, acc_ref, o_ref, so_ref, la_ref):
    nb = u_ref.shape[0]
    nk = B_KEY_DIM
    off_k, off_v, off_g = B_KEY_WIDTH, 2 * B_KEY_WIDTH, 2 * B_KEY_WIDTH + B_WIDTH
    off_a = 2 * B_KEY_WIDTH + 2 * B_WIDTH
    la_ref[...] = _gla_log_alpha(u_ref[:, off_a:off_a + LANES], aup_ref, ab_ref)
    eye = (lax.broadcasted_iota(jnp.int32, (nk, nk), 0) == lax.broadcasted_iota(jnp.int32, (nk, nk), 1)).astype(F32)
    col = lambda x: jnp.sum(eye * x, axis=-1, keepdims=True)

    def body(n, carry):
        row = pl.ds(n, 1)
        ur = u_ref[row, :]
        dec_all = jnp.exp(la_ref[row, :])
        outs = []
        for h in range(B_HEADS):
            q = ur[:, h * nk:(h + 1) * nk] * (B_KEY_DIM ** -0.5)
            k = ur[:, off_k + h * nk:off_k + (h + 1) * nk]
            v = ur[:, off_v + h * B_VAL_DIM:off_v + (h + 1) * B_VAL_DIM]
            g = ur[:, off_g + h * B_VAL_DIM:off_g + (h + 1) * B_VAL_DIM]
            dec = dec_all[:, h * nk:(h + 1) * nk]
            s = s_ref[n, h]
            qk = jnp.sum(q * k, axis=-1, keepdims=True)
            o = qk * v + jnp.sum(col(q * dec) * s, axis=0, keepdims=True)
            so_ref[n, h] = s * col(dec) + col(k) * v
            ms = jnp.mean(o * o, axis=-1, keepdims=True)
            outs.append(o * lax.rsqrt(ms + NORM_EPS) * nw_ref[:, h * B_VAL_DIM:(h + 1) * B_VAL_DIM]
                        * (g * _sigmoid(g)))
        o_ref[row, :] = jnp.concatenate(outs, axis=-1)
        return carry

    lax.fori_loop(0, nb, body, 0)


def _gla_step(u, alpha_up_pad, alpha_b, norm_w, states, j, nb, new_states):
    n = u.shape[0]
    sspec = pl.BlockSpec((None, nb, B_HEADS, B_KEY_DIM, B_VAL_DIM), lambda i: (j, i, 0, 0, 0))
    return pl.pallas_call(
        _gla_step_kernel,
        grid=(n // nb,),
        in_specs=[pl.BlockSpec((nb, A_PROJ), lambda i: (i, 1)),
                  pl.BlockSpec((LANES, B_KEY_WIDTH), lambda i: (0, 0)),
                  pl.BlockSpec((1, B_KEY_WIDTH), lambda i: (0, 0)),
                  pl.BlockSpec((1, B_WIDTH), lambda i: (0, 0)),
                  sspec, pl.BlockSpec(memory_space=pl.ANY)],
        out_specs=[pl.BlockSpec((nb, B_WIDTH), lambda i: (i, 0)), sspec],
        out_shape=[jax.ShapeDtypeStruct((n, B_WIDTH), F32), jax.ShapeDtypeStruct(states.shape, F32)],
        input_output_aliases={5: 1},
        scratch_shapes=[pltpu.VMEM((nb, B_KEY_WIDTH), F32)],
        compiler_params=_params("arbitrary"),
        name="gla_step",
    )(u, alpha_up_pad, alpha_b, norm_w, states, new_states)


def _lru_gates(xc, wa_ref, wx_ref, ba_ref, bx_ref, lam_ref):
    ra, ix = [], []
    for nb in range(C_BLOCKS):
        xb = xc[:, nb * C_BLOCK:(nb + 1) * C_BLOCK].astype(BF16)
        ra.append(jnp.dot(xb, wa_ref[nb].astype(BF16), preferred_element_type=F32))
        ix.append(jnp.dot(xb, wx_ref[nb].astype(BF16), preferred_element_type=F32))
    r = _sigmoid(jnp.concatenate(ra, axis=-1) + ba_ref[...])
    i_g = _sigmoid(jnp.concatenate(ix, axis=-1) + bx_ref[...])
    log_a = -C_POW * r * _softplus(-lam_ref[...])
    a = jnp.exp(log_a)
    b = jnp.sqrt(1.0 - a * a) * (i_g * xc)
    return a, b


def _lru_seq_kernel(u_ref, cp_ref, h0_ref, cw_ref, cb_ref, wa_ref, wx_ref, ba_ref, bx_ref, lam_ref,
                    y_ref, tail_ref, hT_ref, carry_ref, h_ref, a_s, b_s):
    ti = pl.program_id(1)

    @pl.when(ti == 0)
    def _():
        carry_ref[...] = cp_ref[...]
        h_ref[...] = h0_ref[...]

    gate = u_ref[:, 0:C_WIDTH]
    xb = u_ref[:, C_WIDTH:2 * C_WIDTH]
    tm = xb.shape[0]
    carry = carry_ref[...]
    row8 = lax.broadcasted_iota(jnp.int32, (SUBLANES, C_WIDTH), 0)
    xc = cb_ref[...] + cw_ref[C_CONV - 1:C_CONV, :] * xb
    for s in range(1, C_CONV):
        rolled = pltpu.roll(xb, s, axis=0)
        head = jnp.where(row8 < s, pltpu.roll(carry, s, axis=0), rolled[0:SUBLANES, :])
        shifted = jnp.concatenate([head, rolled[SUBLANES:, :]], axis=0)
        xc = xc + cw_ref[C_CONV - 1 - s:C_CONV - s, :] * shifted
    carry_ref[...] = xb[tm - SUBLANES:tm, :]
    tail_ref[...] = xb[tm - SUBLANES:tm, :]

    a, b = _lru_gates(xc, wa_ref, wx_ref, ba_ref, bx_ref, lam_ref)
    a_s[...] = a
    b_s[...] = b

    def body(t, h):
        h = a_s[pl.ds(t, 1), :] * h + b_s[pl.ds(t, 1), :]
        b_s[pl.ds(t, 1), :] = h
        return h

    h_last = lax.fori_loop(0, tm, body, h_ref[...], unroll=8)
    h_ref[...] = h_last
    hT_ref[...] = h_last
    y_ref[...] = _gelu_tanh(gate) * b_s[...]


def _lru_seq(u, n_seq, conv_prev8, h0, conv_w, conv_b, wa, wx, ba, bx, lam, tm):
    rows = u.shape[0]
    per_seq = rows // n_seq // tm
    const = lambda shape: pl.BlockSpec(shape, lambda b, t: (0,) * len(shape))
    return pl.pallas_call(
        _lru_seq_kernel,
        grid=(n_seq, per_seq),
        in_specs=[
            pl.BlockSpec((tm, 2 * C_WIDTH), lambda b, t: (b * per_seq + t, 0)),
            pl.BlockSpec((None, SUBLANES, C_WIDTH), lambda b, t: (b, 0, 0)),
            pl.BlockSpec((None, 1, C_WIDTH), lambda b, t: (b, 0, 0)),
            const((C_CONV, C_WIDTH)), const((1, C_WIDTH)),
            const((C_BLOCKS, C_BLOCK, C_BLOCK)), const((C_BLOCKS, C_BLOCK, C_BLOCK)),
            const((1, C_WIDTH)), const((1, C_WIDTH)), const((1, C_WIDTH)),
        ],
        out_specs=[
            pl.BlockSpec((tm, C_WIDTH), lambda b, t: (b * per_seq + t, 0)),
            pl.BlockSpec((None, SUBLANES, C_WIDTH), lambda b, t: (b, 0, 0)),
            pl.BlockSpec((None, 1, C_WIDTH), lambda b, t: (b, 0, 0)),
        ],
        out_shape=[jax.ShapeDtypeStruct((rows, C_WIDTH), F32),
                   jax.ShapeDtypeStruct((n_seq, SUBLANES, C_WIDTH), F32),
                   jax.ShapeDtypeStruct((n_seq, 1, C_WIDTH), F32)],
        scratch_shapes=[pltpu.VMEM((SUBLANES, C_WIDTH), F32), pltpu.VMEM((1, C_WIDTH), F32),
                        pltpu.VMEM((tm, C_WIDTH), F32), pltpu.VMEM((tm, C_WIDTH), F32)],
        compiler_params=_params("arbitrary", "arbitrary"),
        name="lru_seq",
    )(u, conv_prev8, h0, conv_w, conv_b, wa, wx, ba, bx, lam)


def _lru_step_kernel(u_ref, cp_ref, h0_ref, cw_ref, cb_ref, wa_ref, wx_ref, ba_ref, bx_ref, lam_ref,
                     y_ref, cn_ref, h_ref):
    gate = u_ref[:, 0:C_WIDTH]
    xb = u_ref[:, C_WIDTH:2 * C_WIDTH]
    xc = cb_ref[...] + cw_ref[C_CONV - 1:C_CONV, :] * xb
    for i in range(C_CONV - 1):
        xc = xc + cw_ref[i:i + 1, :] * cp_ref[i]
    a, b = _lru_gates(xc, wa_ref, wx_ref, ba_ref, bx_ref, lam_ref)
    h = a * h0_ref[...] + b
    h_ref[...] = h
    y_ref[...] = _gelu_tanh(gate) * h
    for i in range(C_CONV - 2):
        cn_ref[i] = cp_ref[i + 1]
    cn_ref[C_CONV - 2] = xb


def _lru_step(u, conv_states, h0, j, conv_w, conv_b, wa, wx, ba, bx, lam):
    rows = u.shape[0]
    full = lambda shape: pl.BlockSpec(shape, lambda i: (0,) * len(shape))
    return pl.pallas_call(
        _lru_step_kernel,
        grid=(1,),
        in_specs=[full((rows, 2 * C_WIDTH)),
                  pl.BlockSpec((None, C_CONV - 1, rows, C_WIDTH), lambda i: (j, 0, 0, 0)),
                  full((rows, C_WIDTH)),
                  full((C_CONV, C_WIDTH)), full((1, C_WIDTH)),
                  full((C_BLOCKS, C_BLOCK, C_BLOCK)), full((C_BLOCKS, C_BLOCK, C_BLOCK)),
                  full((1, C_WIDTH)), full((1, C_WIDTH)), full((1, C_WIDTH))],
        out_specs=[full((rows, C_WIDTH)), full((C_CONV - 1, rows, C_WIDTH)), full((rows, C_WIDTH))],
        out_shape=[jax.ShapeDtypeStruct((rows, C_WIDTH), F32),
                   jax.ShapeDtypeStruct((C_CONV - 1, rows, C_WIDTH), F32),
                   jax.ShapeDtypeStruct((rows, C_WIDTH), F32)],
        compiler_params=_params("arbitrary"),
        name="lru_step",
    )(u, conv_states, h0, conv_w, conv_b, wa, wx, ba, bx, lam)


def _pad_rows(w, first, total):
    return jnp.zeros((total, w.shape[1]), w.dtype).at[first:first + w.shape[0]].set(w)


class _Group:
    def __init__(self, x, mod, n_seq, seq_len, states, tm, tf):
        self.x, self.mod, self.n_seq, self.seq_len, self.states, self.tm, self.tf = x, mod, n_seq, seq_len, states, tm, tf
        self.prompt = seq_len > 1
        self.out_shift, self.out_wkv, self.out_gla, self.out_conv, self.out_lru = [], [], [], [], []
        self.new_wkv_t = self.new_gla = None

    def new_states(self):
        wkv = jnp.stack(self.out_wkv) if self.new_wkv_t is None else jnp.transpose(self.new_wkv_t, (0, 4, 1, 2, 3))
        gla = jnp.stack(self.out_gla) if self.new_gla is None else self.new_gla
        return jnp.stack(self.out_shift), wkv, gla, jnp.stack(self.out_conv), jnp.stack(self.out_lru)


def _mixer(grp, l, wts, mxu):
    (w_ffn_up, w_ffn_down, w_in_even, w_out_even, a_mu, a_w_up, a_a_up, a_g_up, a_vec, b_alpha_up,
     b_alpha_b, b_norm_w, w_in_odd, w_out_odd, c_conv_w, c_conv_b, c_wa, c_ba, c_wx, c_bx, c_lam,
     final_norm_w) = wts
    x, mod, n_seq, seq_len, tm, prompt = grp.x, grp.mod, grp.n_seq, grp.seq_len, grp.tm, grp.prompt
    st_shift, st_wkv, st_gla, st_conv, st_lru = grp.states
    rows = x.shape[0]
    j = l // 2
    tn_even, tn_odd = EVEN_COL_TILE, ODD_COL_TILE
    seg = jnp.kron(jnp.eye(A_HEADS, dtype=F32), jnp.ones((A_HEAD_DIM, A_HEAD_DIM), F32)).astype(BF16)
    if l % 2 == 0:
        if prompt:
            u = _inproj(x, mod, l, mxu['in', l], None, tm, tn_even)
        else:
            u, mxu['in', l] = _inproj(x, mod, l, jnp.swapaxes(w_in_even, 1, 2), j, tm, tn_even,
                                      transposed=True)
        wup_pad = _pad_rows(a_w_up[j], 0, LANES)
        aup_pad = _pad_rows(a_a_up[j], A_DECAY_RANK, LANES)
        vec8 = _pad_rows(a_vec[j], 0, SUBLANES)
        alpha_pad = _pad_rows(b_alpha_up[j], 0, LANES)
        alpha_b = b_alpha_b[j].reshape(1, B_KEY_WIDTH)
        mu = a_mu[j].reshape(1, A_PROJ)
        norm_w = b_norm_w[j].reshape(1, B_WIDTH)
        if prompt:
            r, lw, k2, v, kk, al, g = _rwkv_prep(u, st_shift[j].reshape(n_seq, 1, A_PROJ), mu, wup_pad,
                                                 aup_pad, a_g_up[j], vec8, seg, n_seq, TOKENWISE_ROW_TILE, True)
            y_a, h_t = _rwkv_chunk(r, lw, k2, v, kk, al, g, vec8, n_seq, _pair_states(st_wkv[j]))
            s_wkv = _unpair_states(h_t)
            s_shift = u.reshape(n_seq, seq_len, EVEN_PROJ)[:, -1, :A_PROJ]
            y_b, s_gla = _gla_chunk(u, n_seq, alpha_pad, alpha_b, norm_w, st_gla[j])
        else:
            r, lw, k2, v, kk, al, g, bonus = _rwkv_prep(u, st_shift[j], mu, wup_pad, aup_pad, a_g_up[j],
                                                        vec8, seg, rows, rows, False)
            st_wkv_t = jnp.transpose(st_wkv, (0, 2, 3, 4, 1))
            y_raw, grp.new_wkv_t = _rwkv_step(r, lw, k2, v, kk, al, st_wkv_t, j,
                                              jnp.zeros_like(st_wkv_t) if grp.new_wkv_t is None else grp.new_wkv_t)
            y_a = _rwkv_post(y_raw, bonus, g, vec8, seg, rows)
            s_shift = u[:, :A_PROJ]
            y_b, grp.new_gla = _gla_step(u, alpha_pad, alpha_b, norm_w, st_gla, j, STEP_ROWS,
                                         jnp.zeros_like(st_gla) if grp.new_gla is None else grp.new_gla)
        if prompt:
            x = _outproj([y_a, y_b], mxu['out', l], None, x, mod, l, OUTPROJ_ROW_TILE)
        else:
            x, mxu['out', l] = _outproj([y_a, y_b], w_out_even, j, x, mod, l, tm)
        grp.out_shift.append(s_shift)
        if prompt:
            grp.out_wkv.append(s_wkv)
            grp.out_gla.append(s_gla)
    else:
        if prompt:
            u = _inproj(x, mod, l, mxu['in', l], None, tm, tn_odd)
        else:
            u, mxu['in', l] = _inproj(x, mod, l, w_in_odd, j, tm, tn_odd)
        lru_w = (c_conv_w[j], c_conv_b[j].reshape(1, C_WIDTH), c_wa[j], c_wx[j],
                 c_ba[j].reshape(1, C_WIDTH), c_bx[j].reshape(1, C_WIDTH), c_lam[j].reshape(1, C_WIDTH))
        if prompt:
            cp8 = jnp.concatenate([jnp.zeros((n_seq, SUBLANES - (C_CONV - 1), C_WIDTH), F32), st_conv[j]], axis=1)
            y_c, tail, h_t = _lru_seq(u, n_seq, cp8, st_lru[j].reshape(n_seq, 1, C_WIDTH), *lru_w,
                                      TOKENWISE_ROW_TILE)
            s_conv = tail[:, SUBLANES - (C_CONV - 1):, :]
            s_lru = h_t.reshape(n_seq, C_WIDTH)
        else:
            y_c, cn, s_lru = _lru_step(u, jnp.swapaxes(st_conv, 1, 2), st_lru[j], j, *lru_w)
            s_conv = jnp.swapaxes(cn, 0, 1)
        if prompt:
            x = _outproj([y_c], mxu['out', l], None, x, mod, l, OUTPROJ_ROW_TILE)
        else:
            x, mxu['out', l] = _outproj([y_c], w_out_odd, j, x, mod, l, tm)
        grp.out_conv.append(s_conv)
        grp.out_lru.append(s_lru)
    grp.x = x


def _trunk(sample, prompt, wts):
    w_ffn_up, w_ffn_down, final_norm_w = wts[0], wts[1], wts[-1]
    subs = [(l, which) for l in range(DEPTH) for which in (0, 1)]
    mxu = {('ffn', 0, 0): (w_ffn_up[0, 0].astype(BF16), w_ffn_down[0, 0].astype(BF16))}
    for idx, (l, which) in enumerate(subs):
        final_w = final_norm_w if idx == len(subs) - 1 else None
        weights = mxu['ffn', l, which]
        sample.x = _ffn(sample.x, sample.mod, l, which, weights, sample.tm, sample.tf, final_w=final_w)
        if idx + 1 < len(subs):
            nxt = subs[idx + 1]
            prompt.x, mxu[('ffn',) + nxt] = _ffn(prompt.x, prompt.mod, l, which, weights, prompt.tm, prompt.tf,
                                                 convert=(w_ffn_up, w_ffn_down) + nxt)
        else:
            prompt.x = _ffn(prompt.x, prompt.mod, l, which, weights, prompt.tm, prompt.tf, final_w=final_w)
        if which == 0:
            _mixer(sample, l, wts, mxu)
            _mixer(prompt, l, wts, mxu)


def kernel(x_prompt, x_sample, c_prompt, c_sample, state_rwkv_shift, state_rwkv_wkv, state_gla, state_conv, state_lru, w_ada, b_ada, w_ffn_up, w_ffn_down, w_in_even, w_out_even, a_mu, a_w_up, a_a_up, a_g_up, a_vec, b_alpha_up, b_alpha_b, b_norm_w, w_in_odd, w_out_odd, c_conv_w, c_conv_b, c_wa, c_ba, c_wx, c_bx, c_lam, final_norm_w):
    wts = (w_ffn_up, w_ffn_down, w_in_even, w_out_even, a_mu, a_w_up, a_a_up, a_g_up, a_vec, b_alpha_up,
           b_alpha_b, b_norm_w, w_in_odd, w_out_odd, c_conv_w, c_conv_b, c_wa, c_ba, c_wx, c_bx, c_lam,
           final_norm_w)
    n_p, t_p, _ = x_prompt.shape
    n_s, t_s, _ = x_sample.shape
    assert t_s == 1

    c_rows = n_p + n_s
    pad = (-c_rows) % 16
    c_all = jnp.concatenate([c_sample, c_prompt, jnp.zeros((pad, D_MODEL), F32)], axis=0)
    mod_all = _ada(c_all, w_ada, b_ada)
    mod_p = _Mod(mod_all[:, :, n_s:c_rows].reshape(DEPTH, N_MOD, n_p, 1, D_MODEL), False, t_p)
    mod_s = _Mod(mod_all, True, 1)

    fresh = lambda s: jnp.zeros((s.shape[0], n_p) + s.shape[2:], s.dtype)
    states = (state_rwkv_shift, state_rwkv_wkv, state_gla, state_conv, state_lru)
    sample = _Group(x_sample.reshape(n_s, D_MODEL), mod_s, n_s, 1, states, n_s, SAMPLE_FFN_COL_TILE)
    prompt = _Group(x_prompt.reshape(n_p * t_p, D_MODEL), mod_p, n_p, t_p, tuple(fresh(s) for s in states),
                    PROMPT_ROW_TILE, FFN_COL_TILE)
    _trunk(sample, prompt, wts)
    return (prompt.x.reshape(n_p, t_p, D_MODEL), sample.x.reshape(n_s, 1, D_MODEL),
            *prompt.new_states(), *sample.new_states())
```

```python
import functools

import jax
import jax.numpy as jnp
from jax import lax
from jax.experimental import pallas as pl
from jax.experimental.pallas import tpu as pltpu

F32 = jnp.float32
BF16 = jnp.bfloat16

D_MODEL = 2048
DEPTH = 4
N_MOD = 9
D_FF = 5632
NORM_EPS = 1e-6

A_HEADS = 16
A_HEAD_DIM = 64
A_WIDTH = A_HEADS * A_HEAD_DIM
A_DECAY_RANK = 64
A_ICL_RANK = 64
A_GATE_RANK = 128
A_PROJ = 3 * A_WIDTH + A_DECAY_RANK + A_ICL_RANK + A_GATE_RANK
A_GN_EPS = 64e-5

B_HEADS = 4
B_KEY_DIM = 128
B_VAL_DIM = 256
B_KEY_WIDTH = B_HEADS * B_KEY_DIM
B_WIDTH = B_HEADS * B_VAL_DIM
B_ALPHA_RANK = 16
B_TAU = 16.0
B_PROJ = 2 * B_KEY_WIDTH + 2 * B_WIDTH + B_ALPHA_RANK
EVEN_PROJ = A_PROJ + B_PROJ

C_WIDTH = D_MODEL
C_BLOCKS = 8
C_BLOCK = C_WIDTH // C_BLOCKS
C_CONV = 4
C_POW = 8.0

LANES = 128
SUBLANES = 8
VMEM_LIMIT = 56 * 1024 * 1024
FFN_VMEM_LIMIT = 60 * 1024 * 1024

CHUNK = 64
RWKV_STEP_CHUNKS = 4
GLA_STEP_CHUNKS = 4
GLA_SUB = SUBLANES
PROMPT_ROW_TILE = 1024
OUTPROJ_ROW_TILE = 512
TOKENWISE_ROW_TILE = 256
FFN_COL_TILE = 512
SAMPLE_FFN_COL_TILE = 1408
EVEN_COL_TILE = 13 * LANES
ODD_COL_TILE = 8 * LANES
STEP_ROWS = 2 * SUBLANES
ADA_COL_TILE = 1024
NORM_ROWS = 32
NORM_UNROLL = 4
EPILOGUE_ROWS = 128


def _params(*sem, vmem=VMEM_LIMIT):
    return pltpu.CompilerParams(dimension_semantics=sem, vmem_limit_bytes=vmem)


def _bdot(a, b):
    return jnp.dot(a.astype(BF16), b.astype(BF16), preferred_element_type=F32)


def _split3(x):
    hi = x.astype(BF16)
    r1 = x - hi.astype(F32)
    mid = r1.astype(BF16)
    lo = (r1 - mid.astype(F32)).astype(BF16)
    return hi, mid, lo


def _dot_exact_rhs(a_bf16, x):
    hi, mid, lo = _split3(x)
    f = lambda y: jnp.dot(a_bf16, y, preferred_element_type=F32)
    return f(hi) + f(mid) + f(lo)


def _dot_exact_lhs(x, b_bf16):
    hi, mid, lo = _split3(x)
    f = lambda y: jnp.dot(y, b_bf16, preferred_element_type=F32)
    return f(hi) + f(mid) + f(lo)


def _sigmoid(x):
    return jax.nn.sigmoid(x)


def _softplus(x):
    return jnp.maximum(x, 0.0) + jnp.log1p(jnp.exp(-jnp.abs(x)))


def _gelu_tanh(x):
    return 0.5 * x * (1.0 + jnp.tanh(0.7978845608028654 * (x + 0.044715 * (x * x * x))))


def _norm_mod(x, shift, scale):
    ms = jnp.mean(x * x, axis=-1, keepdims=True)
    return x * lax.rsqrt(ms + NORM_EPS) * (1.0 + scale) + shift


def _ada_kernel(c_ref, w_ref, b_ref, o_ref):
    c = c_ref[...]
    act = (c * _sigmoid(c)).astype(BF16)
    o_ref[...] = jnp.dot(act, w_ref[...].astype(BF16), preferred_element_type=F32) + b_ref[...]


def _ada(c_all, w_ada, b_ada):
    rows = c_all.shape[0]
    tn = ADA_COL_TILE
    per = D_MODEL // tn
    b4 = b_ada.reshape(DEPTH, N_MOD * per, 1, tn)
    return pl.pallas_call(
        _ada_kernel,
        grid=(DEPTH, N_MOD * per),
        in_specs=[
            pl.BlockSpec((rows, D_MODEL), lambda l, j: (0, 0)),
            pl.BlockSpec((None, D_MODEL, tn), lambda l, j: (l, 0, j)),
            pl.BlockSpec((None, None, 1, tn), lambda l, j: (l, j, 0, 0)),
        ],
        out_specs=pl.BlockSpec((None, None, rows, tn), lambda l, j: (l, j // per, 0, j % per)),
        out_shape=jax.ShapeDtypeStruct((DEPTH, N_MOD, rows, D_MODEL), F32),
        compiler_params=_params("arbitrary", "arbitrary"),
        name="ada",
    )(c_all, w_ada, b4)


class _Mod:
    def __init__(self, arr, per_row, seq_len):
        self.arr = arr
        self.per_row = per_row
        self.seq_len = seq_len

    def spec(self, layer, m, tm):
        if self.per_row:
            return pl.BlockSpec((None, None, tm, D_MODEL), lambda i, *_: (layer, m, i, 0))
        per_seq = self.seq_len // tm
        return pl.BlockSpec((None, None, None, 1, D_MODEL), lambda i, *_: (layer, m, i // per_seq, 0, 0))


def _ffn_kernel(x_ref, sh_ref, sc_ref, g_ref, wg_ref, wu_ref, wd_ref, *rest, n_j, psub, sub, final, convert):
    rest = list(rest)
    fw_ref = rest.pop(0) if final else None
    f32_tiles = [rest.pop(0) for _ in range(2)] if convert else []
    o_ref = rest.pop(0)
    bf16_tiles = [rest.pop(0) for _ in range(2)] if convert else []
    h_ref, = rest
    j = pl.program_id(1)
    tm = x_ref.shape[0]

    @pl.when(j == 0)
    def _():
        def body(s, carry):
            rows = pl.ds(pl.multiple_of(s * psub, psub), psub)
            sh = sh_ref[...] if sh_ref.shape[0] == 1 else sh_ref[rows, :]
            sc = sc_ref[...] if sc_ref.shape[0] == 1 else sc_ref[rows, :]
            h_ref[rows, :] = _norm_mod(x_ref[rows, :], sh, sc).astype(BF16)
            o_ref[rows, :] = jnp.zeros((psub, D_MODEL), F32)
            return carry
        lax.fori_loop(0, tm // psub, body, 0, unroll=NORM_UNROLL)

    for src, dst in zip(f32_tiles, bf16_tiles):
        dst[...] = src[...].astype(BF16)

    h = h_ref[...]
    gt = jnp.dot(h, wg_ref[...], preferred_element_type=F32)
    ut = jnp.dot(h, wu_ref[...], preferred_element_type=F32)
    act = (gt * _sigmoid(gt) * ut).astype(BF16)
    o_ref[...] += jnp.dot(act, wd_ref[...], preferred_element_type=F32)

    @pl.when(j == n_j - 1)
    def _():
        def body(s, carry):
            rows = pl.ds(pl.multiple_of(s * sub, sub), sub)
            g = g_ref[...] if g_ref.shape[0] == 1 else g_ref[rows, :]
            y = x_ref[rows, :] + 0.5 * g * o_ref[rows, :]
            if final:
                ms = jnp.mean(y * y, axis=-1, keepdims=True)
                y = y * lax.rsqrt(ms + NORM_EPS) * fw_ref[...]
            o_ref[rows, :] = y
            return carry
        lax.fori_loop(0, tm // sub, body, 0)


def _ffn(x, mod, layer, which, weights, tm, tf, final_w=None, convert=None):
    m_rows = x.shape[0]
    n_i = m_rows // tm
    n_j = D_FF // tf
    m0 = 6 * which
    sub = min(tm, EPILOGUE_ROWS)
    final = final_w is not None
    in_specs = [
        pl.BlockSpec((tm, D_MODEL), lambda i, j: (i, 0)),
        mod.spec(layer, m0, tm), mod.spec(layer, m0 + 1, tm), mod.spec(layer, m0 + 2, tm),
        pl.BlockSpec((D_MODEL, tf), lambda i, j: (0, j)),
        pl.BlockSpec((D_MODEL, tf), lambda i, j: (0, j + n_j)),
        pl.BlockSpec((tf, D_MODEL), lambda i, j: (j, 0)),
    ]
    args = [x, mod.arr, mod.arr, mod.arr, weights[0], weights[0], weights[1]]
    if final:
        in_specs.append(pl.BlockSpec((1, D_MODEL), lambda i, j: (0, 0)))
        args.append(final_w.reshape(1, D_MODEL))
    out_specs = [pl.BlockSpec((tm, D_MODEL), lambda i, j: (i, 0))]
    out_shape = [jax.ShapeDtypeStruct((m_rows, D_MODEL), F32)]
    if convert is not None:
        w_up32, w_down32, l2, s2 = convert
        up_tile = (D_MODEL // n_i, 2 * D_FF // n_j)
        down_tile = (D_FF // n_j, D_MODEL // n_i)
        assert up_tile[0] * n_i == D_MODEL and up_tile[1] * n_j == 2 * D_FF and up_tile[1] % LANES == 0
        assert down_tile[0] * n_j == D_FF and down_tile[1] * n_i == D_MODEL and down_tile[1] % LANES == 0
        in_specs += [pl.BlockSpec((None, None) + up_tile, lambda i, j: (l2, s2, i, j)),
                     pl.BlockSpec((None, None) + down_tile, lambda i, j: (l2, s2, j, i))]
        args += [w_up32, w_down32]
        out_specs += [pl.BlockSpec(up_tile, lambda i, j: (i, j)), pl.BlockSpec(down_tile, lambda i, j: (j, i))]
        out_shape += [jax.ShapeDtypeStruct((D_MODEL, 2 * D_FF), BF16), jax.ShapeDtypeStruct((D_FF, D_MODEL), BF16)]
    out = pl.pallas_call(
        functools.partial(_ffn_kernel, n_j=n_j, psub=min(tm, NORM_ROWS), sub=sub, final=final,
                          convert=convert is not None),
        grid=(n_i, n_j),
        in_specs=in_specs,
        out_specs=out_specs,
        out_shape=out_shape,
        scratch_shapes=[pltpu.VMEM((tm, D_MODEL), BF16)],
        compiler_params=_params("arbitrary", "arbitrary", vmem=FFN_VMEM_LIMIT),
        name="ffn",
    )(*args)
    return (out[0], (out[1], out[2])) if convert is not None else out[0]


def _inproj_kernel(x_ref, sh_ref, sc_ref, w_ref, o_ref, *rest, sub, emit, transposed):
    h_ref = rest[-1]
    j = pl.program_id(1)
    tm = x_ref.shape[0]

    @pl.when(j == 0)
    def _():
        def body(s, carry):
            rows = pl.ds(pl.multiple_of(s * sub, sub), sub)
            sh = sh_ref[...] if sh_ref.shape[0] == 1 else sh_ref[rows, :]
            sc = sc_ref[...] if sc_ref.shape[0] == 1 else sc_ref[rows, :]
            h_ref[rows, :] = _norm_mod(x_ref[rows, :], sh, sc).astype(BF16)
            return carry
        lax.fori_loop(0, tm // sub, body, 0, unroll=NORM_UNROLL)

    w = w_ref[...]
    if transposed:
        w = w.T
    if emit:
        w = w.astype(BF16)
        rest[0][...] = w
    o_ref[...] = jnp.dot(h_ref[...], w, preferred_element_type=F32)


def _inproj(x, mod, layer, w, widx, tm, tn, transposed=False):
    m_rows = x.shape[0]
    n_out = w.shape[-2] if transposed else w.shape[-1]
    sub = min(tm, NORM_ROWS)
    emit = widx is not None
    if transposed:
        w_spec = pl.BlockSpec((None, tn, D_MODEL), lambda i, j: (widx, j, 0))
    elif emit:
        w_spec = pl.BlockSpec((None, D_MODEL, tn), lambda i, j: (widx, 0, j))
    else:
        w_spec = pl.BlockSpec((D_MODEL, tn), lambda i, j: (0, j))
    out_specs = [pl.BlockSpec((tm, tn), lambda i, j: (i, j))]
    out_shape = [jax.ShapeDtypeStruct((m_rows, n_out), F32)]
    if emit:
        assert m_rows == tm
        out_specs.append(pl.BlockSpec((D_MODEL, tn), lambda i, j: (0, j)))
        out_shape.append(jax.ShapeDtypeStruct((D_MODEL, n_out), BF16))
    out = pl.pallas_call(
        functools.partial(_inproj_kernel, sub=sub, emit=emit, transposed=transposed),
        grid=(m_rows // tm, pl.cdiv(n_out, tn)),
        in_specs=[
            pl.BlockSpec((tm, D_MODEL), lambda i, j: (i, 0)),
            mod.spec(layer, 3, tm), mod.spec(layer, 4, tm),
            w_spec,
        ],
        out_specs=out_specs,
        out_shape=out_shape,
        scratch_shapes=[pltpu.VMEM((tm, D_MODEL), BF16)],
        compiler_params=_params("arbitrary", "arbitrary"),
        name="inproj",
    )(x, mod.arr, mod.arr, w)
    return tuple(out) if emit else out[0]


def _outproj_kernel(*refs, n_in, emit):
    y_refs = refs[:n_in]
    w_refs = refs[n_in:2 * n_in]
    x_ref, g_ref, o_ref = refs[2 * n_in:2 * n_in + 3]
    acc = None
    for k, (y_ref, w_ref) in enumerate(zip(y_refs, w_refs)):
        w = w_ref[...]
        if emit:
            w = w.astype(BF16)
            refs[2 * n_in + 3 + k][...] = w
        part = jnp.dot(y_ref[...].astype(BF16), w, preferred_element_type=F32)
        acc = part if acc is None else acc + part
    o_ref[...] = x_ref[...] + g_ref[...] * acc


def _outproj(ys, w, widx, x, mod, layer, tm):
    m_rows = x.shape[0]
    n_in = len(ys)
    kw = ys[0].shape[1]
    emit = widx is not None
    in_specs = [pl.BlockSpec((tm, kw), lambda i: (i, 0)) for _ in ys]
    if emit:
        assert m_rows == tm
        in_specs += [pl.BlockSpec((None, kw, D_MODEL), lambda i, k=k: (widx, k, 0)) for k in range(n_in)]
        w_args = [w] * n_in
    else:
        in_specs += [pl.BlockSpec((kw, D_MODEL), lambda i: (0, 0))] * n_in
        w_args = list(w)
    in_specs += [pl.BlockSpec((tm, D_MODEL), lambda i: (i, 0)), mod.spec(layer, 5, tm)]
    out_specs = [pl.BlockSpec((tm, D_MODEL), lambda i: (i, 0))]
    out_shape = [jax.ShapeDtypeStruct((m_rows, D_MODEL), F32)]
    if emit:
        out_specs += [pl.BlockSpec((kw, D_MODEL), lambda i: (0, 0))] * n_in
        out_shape += [jax.ShapeDtypeStruct((kw, D_MODEL), BF16)] * n_in
    out = pl.pallas_call(
        functools.partial(_outproj_kernel, n_in=n_in, emit=emit),
        grid=(m_rows // tm,),
        in_specs=in_specs,
        out_specs=out_specs,
        out_shape=out_shape,
        compiler_params=_params("arbitrary"),
        name="outproj",
    )(*ys, *w_args, x, mod.arr)
    return (out[0], tuple(out[1:])) if emit else out[0]


def _rwkv_prep_kernel(u_ref, prev_ref, mu_ref, wup_ref, aup_ref, gup_ref, vec_ref, *rest, seq_mode):
    if seq_mode:
        r_ref, lw_ref, k_ref, v_ref, kk_ref, a_ref, g_ref, carry_ref = rest
    else:
        seg_ref, r_ref, lw_ref, k_ref, v_ref, kk_ref, a_ref, g_ref, bonus_ref = rest
    u = u_ref[...]
    tm = u.shape[0]
    if seq_mode:
        @pl.when(pl.program_id(1) == 0)
        def _():
            carry_ref[...] = prev_ref[...]
        rolled = pltpu.roll(u, 1, axis=0)
        row = lax.broadcasted_iota(jnp.int32, u.shape, 0)
        prev = jnp.where(row == 0, carry_ref[...], rolled)
        carry_ref[...] = u[tm - 1:tm, :]
    else:
        prev = prev_ref[...]
    xs = u + mu_ref[...] * (prev - u)
    r = xs[:, 0:A_WIDTH]
    k = xs[:, A_WIDTH:2 * A_WIDTH]
    v = xs[:, 2 * A_WIDTH:3 * A_WIDTH]
    wa_lo = xs[:, 3 * A_WIDTH:3 * A_WIDTH + LANES]
    g_lo = xs[:, 3 * A_WIDTH + LANES:]
    w0, a0, k_k, k_a, r_k = (vec_ref[i:i + 1, :] for i in range(5))
    w_raw = -_softplus(-(w0 + _bdot(jnp.tanh(wa_lo), wup_ref[...]))) - 0.5
    lw = -jnp.exp(w_raw)
    a = _sigmoid(a0 + _bdot(wa_lo, aup_ref[...]))
    g = _bdot(_sigmoid(g_lo), gup_ref[...])
    kk = k * k_k
    k2 = k * (1.0 + (a - 1.0) * k_a)
    if not seq_mode:
        seg = seg_ref[...]
        kk = kk * lax.rsqrt(jnp.maximum(_dot_exact_lhs(kk * kk, seg), 1e-24))
        bonus_ref[...] = _dot_exact_lhs(r * k2 * r_k, seg) * v
    r_ref[...] = r
    lw_ref[...] = lw
    k_ref[...] = k2
    v_ref[...] = v
    kk_ref[...] = kk
    a_ref[...] = a
    g_ref[...] = g


def _rwkv_prep(u, prev, mu, wup_pad, aup_pad, g_up, vec, seg, n_seq, tm, seq_mode):
    rows = u.shape[0]
    per_seq = rows // n_seq // tm if seq_mode else 1
    grid = (n_seq, per_seq) if seq_mode else (rows // tm, 1)
    rowmap = (lambda b, t: (b * per_seq + t, 0)) if seq_mode else (lambda b, t: (b, 0))
    prev_spec = (pl.BlockSpec((None, 1, A_PROJ), lambda b, t: (b, 0, 0)) if seq_mode
                 else pl.BlockSpec((tm, A_PROJ), rowmap))
    const = lambda shape: pl.BlockSpec(shape, lambda b, t: (0,) * len(shape))
    out = jax.ShapeDtypeStruct((rows, A_WIDTH), F32)
    in_specs = [pl.BlockSpec((tm, A_PROJ), rowmap), prev_spec,
                const((1, A_PROJ)), const((LANES, A_WIDTH)), const((LANES, A_WIDTH)),
                const((A_GATE_RANK, A_WIDTH)), const((SUBLANES, A_WIDTH))]
    args = [u, prev, mu, wup_pad, aup_pad, g_up, vec]
    if not seq_mode:
        in_specs.append(const((A_WIDTH, A_WIDTH)))
        args.append(seg)
    n_out = 7 if seq_mode else 8
    return pl.pallas_call(
        functools.partial(_rwkv_prep_kernel, seq_mode=seq_mode),
        grid=grid,
        in_specs=in_specs,
        out_specs=[pl.BlockSpec((tm, A_WIDTH), rowmap)] * n_out,
        out_shape=[out] * n_out,
        scratch_shapes=[pltpu.VMEM((1, A_PROJ), F32)] if seq_mode else [],
        compiler_params=_params("arbitrary", "arbitrary"),
        name="rwkv_prep",
    )(*args)


def _rwkv_post_kernel(y_ref, bonus_ref, g_ref, vec_ref, seg_ref, o_ref):
    y = y_ref[...]
    seg = seg_ref[...]
    inv_n = 1.0 / A_HEAD_DIM
    yc = y - _dot_exact_lhs(y, seg) * inv_n
    var = _dot_exact_lhs(yc * yc, seg) * inv_n
    yn = yc * lax.rsqrt(var + A_GN_EPS) * vec_ref[5:6, :] + vec_ref[6:7, :]
    o_ref[...] = (yn + bonus_ref[...]) * g_ref[...]


def _rwkv_post(y, bonus, g, vec, seg, tm):
    rows = y.shape[0]
    spec = pl.BlockSpec((tm, A_WIDTH), lambda i: (i, 0))
    return pl.pallas_call(
        _rwkv_post_kernel,
        grid=(rows // tm,),
        in_specs=[spec, spec, spec,
                  pl.BlockSpec((SUBLANES, A_WIDTH), lambda i: (0, 0)),
                  pl.BlockSpec((A_WIDTH, A_WIDTH), lambda i: (0, 0))],
        out_specs=spec,
        out_shape=jax.ShapeDtypeStruct((rows, A_WIDTH), F32),
        compiler_params=_params("arbitrary"),
        name="rwkv_post",
    )(y, bonus, g, vec, seg)


def _rwkv_chunk_kernel(r_ref, lw_ref, k_ref, v_ref, kk_ref, a_ref, g_ref, vec_ref, h0_ref, y_ref, hT_ref, h_ref):
    ci = pl.program_id(1)

    @pl.when(ci == 0)
    def _():
        h_ref[...] = h0_ref[...]

    cs = CHUNK
    n2 = 2 * cs
    npair = A_WIDTH // LANES
    pairs = lambda x: jnp.stack([x[:, p * LANES:(p + 1) * LANES] for p in range(npair)])
    vec = vec_ref[...]
    vrow = lambda i: pairs(vec[i:i + 1, :])
    lane = lax.broadcasted_iota(jnp.int32, (1, 1, LANES), 2)
    m0 = (lane < A_HEAD_DIM).astype(F32)
    m1 = 1.0 - m0
    split = lambda x: jnp.concatenate([x * m0, x * m1], axis=1)
    dup = lambda x: jnp.concatenate([x, x], axis=1)
    own = jnp.concatenate([jnp.broadcast_to(m0, (1, cs, LANES)), jnp.broadcast_to(m1, (1, cs, LANES))], axis=1)
    bf = lambda x: x.astype(BF16)

    def bmm(x, y):
        return jnp.einsum('hab,hbc->hac', x, y, preferred_element_type=F32)

    def bmm_nt(x, y):
        return jnp.einsum('han,hbn->hab', x, y, preferred_element_type=F32)

    def bmm_tn(x, y):
        return jnp.einsum('hca,hcb->hab', x, y, preferred_element_type=F32)

    row = lax.broadcasted_iota(jnp.int32, (cs, cs), 0)
    col = lax.broadcasted_iota(jnp.int32, (cs, cs), 1)
    tri = (col <= row).astype(BF16)
    row2 = lax.broadcasted_iota(jnp.int32, (n2, n2), 0)
    col2 = lax.broadcasted_iota(jnp.int32, (n2, n2), 1)
    same = (row2 >= cs) == (col2 >= cs)
    tok_r = jnp.where(row2 >= cs, row2 - cs, row2)
    tok_c = jnp.where(col2 >= cs, col2 - cs, col2)
    strict = jnp.where(same, (tok_c < tok_r).astype(F32), 0.0)
    incl = jnp.where(same, (tok_c <= tok_r).astype(F32), 0.0)
    eye2 = (row2 == col2).astype(F32)
    nrow = lax.broadcasted_iota(jnp.int32, (LANES, LANES), 0)
    ncol = lax.broadcasted_iota(jnp.int32, (LANES, LANES), 1)
    eye_n = (nrow == ncol).astype(F32)

    h = h_ref[...]
    for sub in range(r_ref.shape[0] // cs):
        rows = slice(sub * cs, (sub + 1) * cs)
        lw_all = lw_ref[rows, :]
        c = pairs(_dot_exact_rhs(tri, lw_all))
        lw = pairs(lw_all)
        c_end = c[:, cs - 1:cs, :]
        e_pos = dup(jnp.exp(c))
        e_neg = dup(jnp.exp(-c))
        e_prev = dup(jnp.exp(c - lw))
        e_end = dup(jnp.exp(c_end - c))
        g_end = jnp.exp(c_end)

        kk = split(pairs(kk_ref[rows, :]))
        kk = kk * lax.rsqrt(jnp.maximum(jnp.sum(kk * kk, axis=-1, keepdims=True), 1e-24))
        b_vec = kk * dup(pairs(a_ref[rows, :]))
        r2 = split(pairs(r_ref[rows, :]))
        k2 = split(pairs(k_ref[rows, :]))
        v2 = split(pairs(v_ref[rows, :]))
        bonus = jnp.sum(r2 * k2 * vrow(4), axis=-1, keepdims=True) * v2
        rt = r2 * e_pos
        at = bf(-kk * e_prev)
        bt = bf(b_vec * e_neg)
        kt = bf(k2 * e_neg)
        bh = bf(b_vec * e_end)
        kh = bf(k2 * e_end)
        v2b = bf(v2)

        gram = bmm_nt(jnp.concatenate([at, bf(rt)], axis=1), jnp.concatenate([bt, kt], axis=1))
        a_ab = gram[:, :n2, :n2] * strict
        a_ak = gram[:, :n2, n2:] * strict
        m_b = gram[:, n2:, :n2] * incl
        m_k = gram[:, n2:, n2:] * incl

        tinv = eye2 + a_ab
        a_b = bf(a_ab)
        apow = bmm(a_b, a_b)
        span = 2
        while 2 * span < cs:
            ap_b = bf(apow)
            both = bmm(jnp.concatenate([bf(tinv), ap_b], axis=1), ap_b)
            tinv = tinv + both[:, :n2]
            apow = both[:, n2:]
            span *= 2
        tinv = tinv + bmm(bf(tinv), bf(apow))

        akv = bmm(bf(a_ak), v2b)
        pq = bf(bmm(bf(tinv), jnp.concatenate([at, bf(akv)], axis=-1)))
        mpq = bmm(bf(m_b), pq)
        p2 = rt + mpq[..., :LANES]
        y0 = mpq[..., LANES:] + bmm(bf(m_k), v2b)
        tpq = bmm_tn(bh, pq)
        gm = eye_n * g_end + tpq[..., :LANES]
        hadd = tpq[..., LANES:] + bmm_tn(kh, v2b)

        yh = bmm(jnp.concatenate([bf(p2), bf(gm)], axis=1), bf(h))
        y2 = yh[:, :n2] + y0
        h = yh[:, n2:] + hadd

        inv_n = 1.0 / A_HEAD_DIM
        yc = (y2 - jnp.sum(y2, axis=-1, keepdims=True) * inv_n) * own
        var = jnp.sum(yc * yc, axis=-1, keepdims=True) * inv_n
        z = yc * lax.rsqrt(var + A_GN_EPS) * vrow(5) + vrow(6) * own + bonus
        out = (z[:, :cs] + z[:, cs:]) * pairs(g_ref[rows, :])
        for p in range(npair):
            y_ref[rows, p * LANES:(p + 1) * LANES] = out[p]

    h_ref[...] = h

    @pl.when(ci == pl.num_programs(1) - 1)
    def _():
        hT_ref[...] = h


def _rwkv_chunk(r, lw, k, v, kk, a, g, vec, n_seq, h0):
    rows = r.shape[0]
    step_rows = RWKV_STEP_CHUNKS * CHUNK
    nc = rows // n_seq // step_rows
    npair = A_WIDTH // LANES
    spec = pl.BlockSpec((step_rows, A_WIDTH), lambda b, c: (b * nc + c, 0))
    hspec = pl.BlockSpec((None, npair, LANES, LANES), lambda b, c: (b, 0, 0, 0))
    return pl.pallas_call(
        _rwkv_chunk_kernel,
        grid=(n_seq, nc),
        in_specs=[spec] * 7 + [pl.BlockSpec((SUBLANES, A_WIDTH), lambda b, c: (0, 0)), hspec],
        out_specs=[spec, hspec],
        out_shape=[jax.ShapeDtypeStruct((rows, A_WIDTH), F32),
                   jax.ShapeDtypeStruct((n_seq, npair, LANES, LANES), F32)],
        scratch_shapes=[pltpu.VMEM((npair, LANES, LANES), F32)],
        compiler_params=_params("arbitrary", "arbitrary"),
        name="rwkv_chunk",
    )(r, lw, k, v, kk, a, g, vec, h0)


def _pair_states(s):
    n = s.shape[0]
    nd = A_HEAD_DIM
    h = jnp.swapaxes(s, -1, -2).reshape(n, A_HEADS // 2, 2, nd, nd)
    out = jnp.zeros((n, A_HEADS // 2, 2 * nd, 2 * nd), s.dtype)
    return out.at[:, :, :nd, :nd].set(h[:, :, 0]).at[:, :, nd:, nd:].set(h[:, :, 1])


def _unpair_states(hp):
    n = hp.shape[0]
    nd = A_HEAD_DIM
    h = jnp.stack([hp[:, :, :nd, :nd], hp[:, :, nd:, nd:]], axis=2).reshape(n, A_HEADS, nd, nd)
    return jnp.swapaxes(h, -1, -2)


def _rwkv_step_kernel(r_ref, lw_ref, k_ref, v_ref, kk_ref, a_ref, s_ref, acc_ref, y_ref, so_ref,
                      r_t, w_t, k_t, v_t, nkk_t, b_t, y_t):
    h = pl.program_id(0)
    nd = A_HEAD_DIM

    @pl.when(h == 0)
    def _():
        kk = kk_ref[...]
        r_t[...] = r_ref[...].T
        w_t[...] = jnp.exp(lw_ref[...]).T
        k_t[...] = k_ref[...].T
        v_t[...] = v_ref[...].T
        nkk_t[...] = (-kk).T
        b_t[...] = (kk * a_ref[...]).T

    first = pl.multiple_of(h * nd, nd)
    rows = pl.ds(first, nd)
    r, w, k, nkk, b = r_t[rows, :], w_t[rows, :], k_t[rows, :], nkk_t[rows, :], b_t[rows, :]

    def body(i, carry):
        s = s_ref[i]
        s_kk = jnp.sum(s * nkk, axis=0, keepdims=True)
        s_new = s * w + s_kk * b + v_t[pl.ds(first + i, 1), :] * k
        so_ref[i] = s_new
        y_t[pl.ds(first + i, 1), :] = jnp.sum(s_new * r, axis=0, keepdims=True)
        return carry

    lax.fori_loop(0, nd, body, 0, unroll=4)

    @pl.when(h == pl.num_programs(0) - 1)
    def _():
        y_ref[...] = y_t[...].T


def _rwkv_step(r, lw, k, v, kk, a, states_t, j, new_states):
    n = r.shape[0]
    nd = A_HEAD_DIM
    rspec = pl.BlockSpec((n, A_WIDTH), lambda h: (0, 0))
    sspec = pl.BlockSpec((None, None, nd, nd, n), lambda h: (j, h, 0, 0, 0))
    return pl.pallas_call(
        _rwkv_step_kernel,
        grid=(A_HEADS,),
        in_specs=[rspec] * 6 + [sspec, pl.BlockSpec(memory_space=pl.ANY)],
        out_specs=[rspec, sspec],
        out_shape=[jax.ShapeDtypeStruct((n, A_WIDTH), F32), jax.ShapeDtypeStruct(states_t.shape, F32)],
        input_output_aliases={7: 1},
        scratch_shapes=[pltpu.VMEM((A_WIDTH, n), F32)] * 7,
        compiler_params=_params("arbitrary"),
        name="rwkv_step",
    )(r, lw, k, v, kk, a, states_t, new_states)


def _gla_log_alpha(a128, aup_ref, ab_ref):
    lane = lax.broadcasted_iota(jnp.int32, a128.shape, 1)
    a_lo = jnp.where(lane < B_ALPHA_RANK, a128, 0.0)
    x = _bdot(a_lo, aup_ref[...]) + ab_ref[...]
    return -_softplus(-x) * (1.0 / B_TAU)


def _gla_block(u_ref, aup_ref, ab_ref, nw_ref, o_ref, s_ref):
    ub = u_ref[...]
    cs = ub.shape[0]
    q_all = ub[:, 0:B_KEY_WIDTH] * (B_KEY_DIM ** -0.5)
    k_all = ub[:, B_KEY_WIDTH:2 * B_KEY_WIDTH]
    v_all = ub[:, 2 * B_KEY_WIDTH:2 * B_KEY_WIDTH + B_WIDTH]
    g_all = ub[:, 2 * B_KEY_WIDTH + B_WIDTH:2 * B_KEY_WIDTH + 2 * B_WIDTH]
    a128 = ub[:, 2 * B_KEY_WIDTH + 2 * B_WIDTH:2 * B_KEY_WIDTH + 2 * B_WIDTH + LANES]
    log_a = _gla_log_alpha(a128, aup_ref, ab_ref)

    row = lax.broadcasted_iota(jnp.int32, (cs, cs), 0)
    col = lax.broadcasted_iota(jnp.int32, (cs, cs), 1)
    cum = _dot_exact_rhs((col <= row).astype(BF16), log_a)
    e_pos = jnp.exp(cum)
    cum_end = cum[cs - 1:cs, :]
    e_end = jnp.exp(cum_end - cum)
    g_end = jnp.exp(cum_end)
    nk = B_KEY_DIM
    eye_k = (lax.broadcasted_iota(jnp.int32, (nk, nk), 0) == lax.broadcasted_iota(jnp.int32, (nk, nk), 1)).astype(F32)

    nblk = cs // GLA_SUB
    kw = B_KEY_WIDTH
    cum3 = cum.reshape(nblk, GLA_SUB, kw)
    before = jnp.concatenate([jnp.zeros((1, 1, kw), F32), cum3[:nblk - 1, GLA_SUB - 1:GLA_SUB, :]], axis=0)
    q_rel = q_all * jnp.exp(cum - jnp.broadcast_to(before, (nblk, GLA_SUB, kw)).reshape(cs, kw))
    tok = lax.broadcasted_iota(jnp.int32, (cs, 1), 0)
    q_parts, k_parts = [], []
    for blk in range(1, nblk):
        first = blk * GLA_SUB
        k_parts.append(jnp.where(tok < first, k_all * jnp.exp(jnp.minimum(before[blk] - cum, 0.0)), 0.0))
        q_parts.append(jnp.where((tok >= first) & (tok < first + GLA_SUB), q_rel, 0.0))

    q3 = q_all.reshape(nblk, GLA_SUB, kw)
    k3 = k_all.reshape(nblk, GLA_SUB, kw)
    v3 = v_all.reshape(nblk, GLA_SUB, B_WIDTH)
    sub_i = lax.broadcasted_iota(jnp.int32, (1, GLA_SUB, 1), 1)
    o_diag = [jnp.zeros((nblk, GLA_SUB, B_VAL_DIM), F32) for _ in range(B_HEADS)]
    for jj in range(GLA_SUB):
        pair = q3 * k3[:, jj:jj + 1, :] * jnp.exp(jnp.minimum(cum3 - cum3[:, jj:jj + 1, :], 0.0))
        pair = jnp.where(sub_i >= jj, pair, 0.0)
        for h in range(B_HEADS):
            score = jnp.sum(pair[:, :, h * nk:(h + 1) * nk], axis=-1, keepdims=True)
            o_diag[h] = o_diag[h] + score * v3[:, jj:jj + 1, h * B_VAL_DIM:(h + 1) * B_VAL_DIM]

    for h in range(B_HEADS):
        ks = slice(h * B_KEY_DIM, (h + 1) * B_KEY_DIM)
        vs = slice(h * B_VAL_DIM, (h + 1) * B_VAL_DIM)
        qe = q_all[:, ks] * e_pos[:, ks]
        ke = k_all[:, ks] * e_end[:, ks]
        vh = v_all[:, vs]
        s = s_ref[h]
        q_cat = jnp.concatenate([part[:, ks] for part in q_parts], axis=-1).astype(BF16)
        k_cat = jnp.concatenate([part[:, ks] for part in k_parts], axis=-1).astype(BF16)
        scores = jnp.einsum('id,jd->ij', q_cat, k_cat, preferred_element_type=F32)
        o = _bdot(scores, vh) + _bdot(qe, s) + o_diag[h].reshape(cs, B_VAL_DIM)
        g_col = jnp.sum(eye_k * g_end[:, ks], axis=-1, keepdims=True)
        s_ref[h] = s * g_col + jnp.einsum('jd,jv->dv', ke.astype(BF16), vh.astype(BF16),
                                           preferred_element_type=F32)
        ms = jnp.mean(o * o, axis=-1, keepdims=True)
        gh = g_all[:, vs]
        o_ref[:, vs] = o * lax.rsqrt(ms + NORM_EPS) * nw_ref[:, vs] * (gh * _sigmoid(gh))


def _gla_chunk_kernel(u_ref, aup_ref, ab_ref, nw_ref, s0_ref, o_ref, sT_ref, s_ref):
    ci = pl.program_id(1)

    @pl.when(ci == 0)
    def _():
        s_ref[...] = s0_ref[...]

    for sub in range(u_ref.shape[0] // CHUNK):
        rows = pl.ds(sub * CHUNK, CHUNK)
        _gla_block(u_ref.at[rows, :], aup_ref, ab_ref, nw_ref, o_ref.at[rows, :], s_ref)

    @pl.when(ci == pl.num_programs(1) - 1)
    def _():
        sT_ref[...] = s_ref[...]


def _gla_chunk(u, n_seq, alpha_up_pad, alpha_b, norm_w, s0):
    rows = u.shape[0]
    step_rows = GLA_STEP_CHUNKS * CHUNK
    nc = rows // n_seq // step_rows
    return pl.pallas_call(
        _gla_chunk_kernel,
        grid=(n_seq, nc),
        in_specs=[
            pl.BlockSpec((step_rows, A_PROJ), lambda b, c: (b * nc + c, 1)),
            pl.BlockSpec((LANES, B_KEY_WIDTH), lambda b, c: (0, 0)),
            pl.BlockSpec((1, B_KEY_WIDTH), lambda b, c: (0, 0)),
            pl.BlockSpec((1, B_WIDTH), lambda b, c: (0, 0)),
            pl.BlockSpec((None, B_HEADS, B_KEY_DIM, B_VAL_DIM), lambda b, c: (b, 0, 0, 0)),
        ],
        out_specs=[
            pl.BlockSpec((step_rows, B_WIDTH), lambda b, c: (b * nc + c, 0)),
            pl.BlockSpec((None, B_HEADS, B_KEY_DIM, B_VAL_DIM), lambda b, c: (b, 0, 0, 0)),
        ],
        out_shape=[jax.ShapeDtypeStruct((rows, B_WIDTH), F32),
                   jax.ShapeDtypeStruct((n_seq, B_HEADS, B_KEY_DIM, B_VAL_DIM), F32)],
        scratch_shapes=[pltpu.VMEM((B_HEADS, B_KEY_DIM, B_VAL_DIM), F32)],
        compiler_params=_params("arbitrary", "arbitrary"),
        name="gla_chunk",
    )(u, alpha_up_pad, alpha_b, norm_w, s0)


def _gla_step_kernel(u_ref, aup_ref, ab_ref, nw_ref, s_ref, acc_ref, o_ref, so_ref, la_ref):
    nb = u_ref.shape[0]
    nk = B_KEY_DIM
    off_k, off_v, off_g = B_KEY_WIDTH, 2 * B_KEY_WIDTH, 2 * B_KEY_WIDTH + B_WIDTH
    off_a = 2 * B_KEY_WIDTH + 2 * B_WIDTH
    la_ref[...] = _gla_log_alpha(u_ref[:, off_a:off_a + LANES], aup_ref, ab_ref)
    eye = (lax.broadcasted_iota(jnp.int32, (nk, nk), 0) == lax.broadcasted_iota(jnp.int32, (nk, nk), 1)).astype(F32)
    col = lambda x: jnp.sum(eye * x, axis=-1, keepdims=True)

    def body(n, carry):
        row = pl.ds(n, 1)
        ur = u_ref[row, :]
        dec_all = jnp.exp(la_ref[row, :])
        outs = []
        for h in range(B_HEADS):
            q = ur[:, h * nk:(h + 1) * nk] * (B_KEY_DIM ** -0.5)
            k = ur[:, off_k + h * nk:off_k + (h + 1) * nk]
            v = ur[:, off_v + h * B_VAL_DIM:off_v + (h + 1) * B_VAL_DIM]
            g = ur[:, off_g + h * B_VAL_DIM:off_g + (h + 1) * B_VAL_DIM]
            dec = dec_all[:, h * nk:(h + 1) * nk]
            s = s_ref[n, h]
            qk = jnp.sum(q * k, axis=-1, keepdims=True)
            o = qk * v + jnp.sum(col(q * dec) * s, axis=0, keepdims=True)
            so_ref[n, h] = s * col(dec) + col(k) * v
            ms = jnp.mean(o * o, axis=-1, keepdims=True)
            outs.append(o * lax.rsqrt(ms + NORM_EPS) * nw_ref[:, h * B_VAL_DIM:(h + 1) * B_VAL_DIM]
                        * (g * _sigmoid(g)))
        o_ref[row, :] = jnp.concatenate(outs, axis=-1)
        return carry

    lax.fori_loop(0, nb, body, 0)


def _gla_step(u, alpha_up_pad, alpha_b, norm_w, states, j, nb, new_states):
    n = u.shape[0]
    sspec = pl.BlockSpec((None, nb, B_HEADS, B_KEY_DIM, B_VAL_DIM), lambda i: (j, i, 0, 0, 0))
    return pl.pallas_call(
        _gla_step_kernel,
        grid=(n // nb,),
        in_specs=[pl.BlockSpec((nb, A_PROJ), lambda i: (i, 1)),
                  pl.BlockSpec((LANES, B_KEY_WIDTH), lambda i: (0, 0)),
                  pl.BlockSpec((1, B_KEY_WIDTH), lambda i: (0, 0)),
                  pl.BlockSpec((1, B_WIDTH), lambda i: (0, 0)),
                  sspec, pl.BlockSpec(memory_space=pl.ANY)],
        out_specs=[pl.BlockSpec((nb, B_WIDTH), lambda i: (i, 0)), sspec],
        out_shape=[jax.ShapeDtypeStruct((n, B_WIDTH), F32), jax.ShapeDtypeStruct(states.shape, F32)],
        input_output_aliases={5: 1},
        scratch_shapes=[pltpu.VMEM((nb, B_KEY_WIDTH), F32)],
        compiler_params=_params("arbitrary"),
        name="gla_step",
    )(u, alpha_up_pad, alpha_b, norm_w, states, new_states)


def _lru_gates(xc, wa_ref, wx_ref, ba_ref, bx_ref, lam_ref):
    ra, ix = [], []
    for nb in range(C_BLOCKS):
        xb = xc[:, nb * C_BLOCK:(nb + 1) * C_BLOCK].astype(BF16)
        ra.append(jnp.dot(xb, wa_ref[nb].astype(BF16), preferred_element_type=F32))
        ix.append(jnp.dot(xb, wx_ref[nb].astype(BF16), preferred_element_type=F32))
    r = _sigmoid(jnp.concatenate(ra, axis=-1) + ba_ref[...])
    i_g = _sigmoid(jnp.concatenate(ix, axis=-1) + bx_ref[...])
    log_a = -C_POW * r * _softplus(-lam_ref[...])
    a = jnp.exp(log_a)
    b = jnp.sqrt(1.0 - a * a) * (i_g * xc)
    return a, b


def _lru_seq_kernel(u_ref, cp_ref, h0_ref, cw_ref, cb_ref, wa_ref, wx_ref, ba_ref, bx_ref, lam_ref,
                    y_ref, tail_ref, hT_ref, carry_ref, h_ref, a_s, b_s):
    ti = pl.program_id(1)

    @pl.when(ti == 0)
    def _():
        carry_ref[...] = cp_ref[...]
        h_ref[...] = h0_ref[...]

    gate = u_ref[:, 0:C_WIDTH]
    xb = u_ref[:, C_WIDTH:2 * C_WIDTH]
    tm = xb.shape[0]
    carry = carry_ref[...]
    row8 = lax.broadcasted_iota(jnp.int32, (SUBLANES, C_WIDTH), 0)
    xc = cb_ref[...] + cw_ref[C_CONV - 1:C_CONV, :] * xb
    for s in range(1, C_CONV):
        rolled = pltpu.roll(xb, s, axis=0)
        head = jnp.where(row8 < s, pltpu.roll(carry, s, axis=0), rolled[0:SUBLANES, :])
        shifted = jnp.concatenate([head, rolled[SUBLANES:, :]], axis=0)
        xc = xc + cw_ref[C_CONV - 1 - s:C_CONV - s, :] * shifted
    carry_ref[...] = xb[tm - SUBLANES:tm, :]
    tail_ref[...] = xb[tm - SUBLANES:tm, :]

    a, b = _lru_gates(xc, wa_ref, wx_ref, ba_ref, bx_ref, lam_ref)
    a_s[...] = a
    b_s[...] = b

    def body(t, h):
        h = a_s[pl.ds(t, 1), :] * h + b_s[pl.ds(t, 1), :]
        b_s[pl.ds(t, 1), :] = h
        return h

    h_last = lax.fori_loop(0, tm, body, h_ref[...], unroll=8)
    h_ref[...] = h_last
    hT_ref[...] = h_last
    y_ref[...] = _gelu_tanh(gate) * b_s[...]


def _lru_seq(u, n_seq, conv_prev8, h0, conv_w, conv_b, wa, wx, ba, bx, lam, tm):
    rows = u.shape[0]
    per_seq = rows // n_seq // tm
    const = lambda shape: pl.BlockSpec(shape, lambda b, t: (0,) * len(shape))
    return pl.pallas_call(
        _lru_seq_kernel,
        grid=(n_seq, per_seq),
        in_specs=[
            pl.BlockSpec((tm, 2 * C_WIDTH), lambda b, t: (b * per_seq + t, 0)),
            pl.BlockSpec((None, SUBLANES, C_WIDTH), lambda b, t: (b, 0, 0)),
            pl.BlockSpec((None, 1, C_WIDTH), lambda b, t: (b, 0, 0)),
            const((C_CONV, C_WIDTH)), const((1, C_WIDTH)),
            const((C_BLOCKS, C_BLOCK, C_BLOCK)), const((C_BLOCKS, C_BLOCK, C_BLOCK)),
            const((1, C_WIDTH)), const((1, C_WIDTH)), const((1, C_WIDTH)),
        ],
        out_specs=[
            pl.BlockSpec((tm, C_WIDTH), lambda b, t: (b * per_seq + t, 0)),
            pl.BlockSpec((None, SUBLANES, C_WIDTH), lambda b, t: (b, 0, 0)),
            pl.BlockSpec((None, 1, C_WIDTH), lambda b, t: (b, 0, 0)),
        ],
        out_shape=[jax.ShapeDtypeStruct((rows, C_WIDTH), F32),
                   jax.ShapeDtypeStruct((n_seq, SUBLANES, C_WIDTH), F32),
                   jax.ShapeDtypeStruct((n_seq, 1, C_WIDTH), F32)],
        scratch_shapes=[pltpu.VMEM((SUBLANES, C_WIDTH), F32), pltpu.VMEM((1, C_WIDTH), F32),
                        pltpu.VMEM((tm, C_WIDTH), F32), pltpu.VMEM((tm, C_WIDTH), F32)],
        compiler_params=_params("arbitrary", "arbitrary"),
        name="lru_seq",
    )(u, conv_prev8, h0, conv_w, conv_b, wa, wx, ba, bx, lam)


def _lru_step_kernel(u_ref, cp_ref, h0_ref, cw_ref, cb_ref, wa_ref, wx_ref, ba_ref, bx_ref, lam_ref,
                     y_ref, cn_ref, h_ref):
    gate = u_ref[:, 0:C_WIDTH]
    xb = u_ref[:, C_WIDTH:2 * C_WIDTH]
    xc = cb_ref[...] + cw_ref[C_CONV - 1:C_CONV, :] * xb
    for i in range(C_CONV - 1):
        xc = xc + cw_ref[i:i + 1, :] * cp_ref[i]
    a, b = _lru_gates(xc, wa_ref, wx_ref, ba_ref, bx_ref, lam_ref)
    h = a * h0_ref[...] + b
    h_ref[...] = h
    y_ref[...] = _gelu_tanh(gate) * h
    for i in range(C_CONV - 2):
        cn_ref[i] = cp_ref[i + 1]
    cn_ref[C_CONV - 2] = xb


def _lru_step(u, conv_states, h0, j, conv_w, conv_b, wa, wx, ba, bx, lam):
    rows = u.shape[0]
    full = lambda shape: pl.BlockSpec(shape, lambda i: (0,) * len(shape))
    return pl.pallas_call(
        _lru_step_kernel,
        grid=(1,),
        in_specs=[full((rows, 2 * C_WIDTH)),
                  pl.BlockSpec((None, C_CONV - 1, rows, C_WIDTH), lambda i: (j, 0, 0, 0)),
                  full((rows, C_WIDTH)),
                  full((C_CONV, C_WIDTH)), full((1, C_WIDTH)),
                  full((C_BLOCKS, C_BLOCK, C_BLOCK)), full((C_BLOCKS, C_BLOCK, C_BLOCK)),
                  full((1, C_WIDTH)), full((1, C_WIDTH)), full((1, C_WIDTH))],
        out_specs=[full((rows, C_WIDTH)), full((C_CONV - 1, rows, C_WIDTH)), full((rows, C_WIDTH))],
        out_shape=[jax.ShapeDtypeStruct((rows, C_WIDTH), F32),
                   jax.ShapeDtypeStruct((C_CONV - 1, rows, C_WIDTH), F32),
                   jax.ShapeDtypeStruct((rows, C_WIDTH), F32)],
        compiler_params=_params("arbitrary"),
        name="lru_step",
    )(u, conv_states, h0, conv_w, conv_b, wa, wx, ba, bx, lam)


def _pad_rows(w, first, total):
    return jnp.zeros((total, w.shape[1]), w.dtype).at[first:first + w.shape[0]].set(w)


class _Group:
    def __init__(self, x, mod, n_seq, seq_len, states, tm, tf):
        self.x, self.mod, self.n_seq, self.seq_len, self.states, self.tm, self.tf = x, mod, n_seq, seq_len, states, tm, tf
        self.prompt = seq_len > 1
        self.out_shift, self.out_wkv, self.out_gla, self.out_conv, self.out_lru = [], [], [], [], []
        self.new_wkv_t = self.new_gla = None

    def new_states(self):
        wkv = jnp.stack(self.out_wkv) if self.new_wkv_t is None else jnp.transpose(self.new_wkv_t, (0, 4, 1, 2, 3))
        gla = jnp.stack(self.out_gla) if self.new_gla is None else self.new_gla
        return jnp.stack(self.out_shift), wkv, gla, jnp.stack(self.out_conv), jnp.stack(self.out_lru)


def _mixer(grp, l, wts, mxu):
    (w_ffn_up, w_ffn_down, w_in_even, w_out_even, a_mu, a_w_up, a_a_up, a_g_up, a_vec, b_alpha_up,
     b_alpha_b, b_norm_w, w_in_odd, w_out_odd, c_conv_w, c_conv_b, c_wa, c_ba, c_wx, c_bx, c_lam,
     final_norm_w) = wts
    x, mod, n_seq, seq_len, tm, prompt = grp.x, grp.mod, grp.n_seq, grp.seq_len, grp.tm, grp.prompt
    st_shift, st_wkv, st_gla, st_conv, st_lru = grp.states
    rows = x.shape[0]
    j = l // 2
    tn_even, tn_odd = EVEN_COL_TILE, ODD_COL_TILE
    seg = jnp.kron(jnp.eye(A_HEADS, dtype=F32), jnp.ones((A_HEAD_DIM, A_HEAD_DIM), F32)).astype(BF16)
    if l % 2 == 0:
        if prompt:
            u = _inproj(x, mod, l, mxu['in', l], None, tm, tn_even)
        else:
            u, mxu['in', l] = _inproj(x, mod, l, jnp.swapaxes(w_in_even, 1, 2), j, tm, tn_even,
                                      transposed=True)
        wup_pad = _pad_rows(a_w_up[j], 0, LANES)
        aup_pad = _pad_rows(a_a_up[j], A_DECAY_RANK, LANES)
        vec8 = _pad_rows(a_vec[j], 0, SUBLANES)
        alpha_pad = _pad_rows(b_alpha_up[j], 0, LANES)
        alpha_b = b_alpha_b[j].reshape(1, B_KEY_WIDTH)
        mu = a_mu[j].reshape(1, A_PROJ)
        norm_w = b_norm_w[j].reshape(1, B_WIDTH)
        if prompt:
            r, lw, k2, v, kk, al, g = _rwkv_prep(u, st_shift[j].reshape(n_seq, 1, A_PROJ), mu, wup_pad,
                                                 aup_pad, a_g_up[j], vec8, seg, n_seq, TOKENWISE_ROW_TILE, True)
            y_a, h_t = _rwkv_chunk(r, lw, k2, v, kk, al, g, vec8, n_seq, _pair_states(st_wkv[j]))
            s_wkv = _unpair_states(h_t)
            s_shift = u.reshape(n_seq, seq_len, EVEN_PROJ)[:, -1, :A_PROJ]
            y_b, s_gla = _gla_chunk(u, n_seq, alpha_pad, alpha_b, norm_w, st_gla[j])
        else:
            r, lw, k2, v, kk, al, g, bonus = _rwkv_prep(u, st_shift[j], mu, wup_pad, aup_pad, a_g_up[j],
                                                        vec8, seg, rows, rows, False)
            st_wkv_t = jnp.transpose(st_wkv, (0, 2, 3, 4, 1))
            y_raw, grp.new_wkv_t = _rwkv_step(r, lw, k2, v, kk, al, st_wkv_t, j,
                                              jnp.zeros_like(st_wkv_t) if grp.new_wkv_t is None else grp.new_wkv_t)
            y_a = _rwkv_post(y_raw, bonus, g, vec8, seg, rows)
            s_shift = u[:, :A_PROJ]
            y_b, grp.new_gla = _gla_step(u, alpha_pad, alpha_b, norm_w, st_gla, j, STEP_ROWS,
                                         jnp.zeros_like(st_gla) if grp.new_gla is None else grp.new_gla)
        if prompt:
            x = _outproj([y_a, y_b], mxu['out', l], None, x, mod, l, OUTPROJ_ROW_TILE)
        else:
            x, mxu['out', l] = _outproj([y_a, y_b], w_out_even, j, x, mod, l, tm)
        grp.out_shift.append(s_shift)
        if prompt:
            grp.out_wkv.append(s_wkv)
            grp.out_gla.append(s_gla)
    else:
        if prompt:
            u = _inproj(x, mod, l, mxu['in', l], None, tm, tn_odd)
        else:
            u, mxu['in', l] = _inproj(x, mod, l, w_in_odd, j, tm, tn_odd)
        lru_w = (c_conv_w[j], c_conv_b[j].reshape(1, C_WIDTH), c_wa[j], c_wx[j],
                 c_ba[j].reshape(1, C_WIDTH), c_bx[j].reshape(1, C_WIDTH), c_lam[j].reshape(1, C_WIDTH))
        if prompt:
            cp8 = jnp.concatenate([jnp.zeros((n_seq, SUBLANES - (C_CONV - 1), C_WIDTH), F32), st_conv[j]], axis=1)
            y_c, tail, h_t = _lru_seq(u, n_seq, cp8, st_lru[j].reshape(n_seq, 1, C_WIDTH), *lru_w,
                                      TOKENWISE_ROW_TILE)
            s_conv = tail[:, SUBLANES - (C_CONV - 1):, :]
            s_lru = h_t.reshape(n_seq, C_WIDTH)
        else:
            y_c, cn, s_lru = _lru_step(u, jnp.swapaxes(st_conv, 1, 2), st_lru[j], j, *lru_w)
            s_conv = jnp.swapaxes(cn, 0, 1)
        if prompt:
            x = _outproj([y_c], mxu['out', l], None, x, mod, l, OUTPROJ_ROW_TILE)
        else:
            x, mxu['out', l] = _outproj([y_c], w_out_odd, j, x, mod, l, tm)
        grp.out_conv.append(s_conv)
        grp.out_lru.append(s_lru)
    grp.x = x


def _trunk(sample, prompt, wts):
    w_ffn_up, w_ffn_down, final_norm_w = wts[0], wts[1], wts[-1]
    subs = [(l, which) for l in range(DEPTH) for which in (0, 1)]
    mxu = {('ffn', 0, 0): (w_ffn_up[0, 0].astype(BF16), w_ffn_down[0, 0].astype(BF16))}
    for idx, (l, which) in enumerate(subs):
        final_w = final_norm_w if idx == len(subs) - 1 else None
        weights = mxu['ffn', l, which]
        sample.x = _ffn(sample.x, sample.mod, l, which, weights, sample.tm, sample.tf, final_w=final_w)
        if idx + 1 < len(subs):
            nxt = subs[idx + 1]
            prompt.x, mxu[('ffn',) + nxt] = _ffn(prompt.x, prompt.mod, l, which, weights, prompt.tm, prompt.tf,
                                                 convert=(w_ffn_up, w_ffn_down) + nxt)
        else:
            prompt.x = _ffn(prompt.x, prompt.mod, l, which, weights, prompt.tm, prompt.tf, final_w=final_w)
        if which == 0:
            _mixer(sample, l, wts, mxu)
            _mixer(prompt, l, wts, mxu)


def kernel(x_prompt, x_sample, c_prompt, c_sample, state_rwkv_shift, state_rwkv_wkv, state_gla, state_conv, state_lru, w_ada, b_ada, w_ffn_up, w_ffn_down, w_in_even, w_out_even, a_mu, a_w_up, a_a_up, a_g_up, a_vec, b_alpha_up, b_alpha_b, b_norm_w, w_in_odd, w_out_odd, c_conv_w, c_conv_b, c_wa, c_ba, c_wx, c_bx, c_lam, final_norm_w):
    wts = (w_ffn_up, w_ffn_down, w_in_even, w_out_even, a_mu, a_w_up, a_a_up, a_g_up, a_vec, b_alpha_up,
           b_alpha_b, b_norm_w, w_in_odd, w_out_odd, c_conv_w, c_conv_b, c_wa, c_ba, c_wx, c_bx, c_lam,
           final_norm_w)
    n_p, t_p, _ = x_prompt.shape
    n_s, t_s, _ = x_sample.shape
    assert t_s == 1

    c_rows = n_p + n_s
    pad = (-c_rows) % 16
    c_all = jnp.concatenate([c_sample, c_prompt, jnp.zeros((pad, D_MODEL), F32)], axis=0)
    mod_all = _ada(c_all, w_ada, b_ada)
    mod_p = _Mod(mod_all[:, :, n_s:c_rows].reshape(DEPTH, N_MOD, n_p, 1, D_MODEL), False, t_p)
    mod_s = _Mod(mod_all, True, 1)

    fresh = lambda s: jnp.zeros((s.shape[0], n_p) + s.shape[2:], s.dtype)
    states = (state_rwkv_shift, state_rwkv_wkv, state_gla, state_conv, state_lru)
    sample = _Group(x_sample.reshape(n_s, D_MODEL), mod_s, n_s, 1, states, n_s, SAMPLE_FFN_COL_TILE)
    prompt = _Group(x_prompt.reshape(n_p * t_p, D_MODEL), mod_p, n_p, t_p, tuple(fresh(s) for s in states),
                    PROMPT_ROW_TILE, FFN_COL_TILE)
    _trunk(sample, prompt, wts)
    return (prompt.x.reshape(n_p, t_p, D_MODEL), sample.x.reshape(n_s, 1, D_MODEL),
            *prompt.new_states(), *sample.new_states())
```

```python
import functools

import jax
import jax.numpy as jnp
from jax import lax
from jax.experimental import pallas as pl
from jax.experimental.pallas import tpu as pltpu

F32 = jnp.float32
BF16 = jnp.bfloat16

D_MODEL = 2048
DEPTH = 4
N_MOD = 9
D_FF = 5632
NORM_EPS = 1e-6

A_HEADS = 16
A_HEAD_DIM = 64
A_WIDTH = A_HEADS * A_HEAD_DIM
A_DECAY_RANK = 64
A_ICL_RANK = 64
A_GATE_RANK = 128
A_PROJ = 3 * A_WIDTH + A_DECAY_RANK + A_ICL_RANK + A_GATE_RANK
A_GN_EPS = 64e-5

B_HEADS = 4
B_KEY_DIM = 128
B_VAL_DIM = 256
B_KEY_WIDTH = B_HEADS * B_KEY_DIM
B_WIDTH = B_HEADS * B_VAL_DIM
B_ALPHA_RANK = 16
B_TAU = 16.0
B_PROJ = 2 * B_KEY_WIDTH + 2 * B_WIDTH + B_ALPHA_RANK
EVEN_PROJ = A_PROJ + B_PROJ

C_WIDTH = D_MODEL
C_BLOCKS = 8
C_BLOCK = C_WIDTH // C_BLOCKS
C_CONV = 4
C_POW = 8.0

LANES = 128
SUBLANES = 8
VMEM_LIMIT = 56 * 1024 * 1024
FFN_VMEM_LIMIT = 60 * 1024 * 1024

CHUNK = 64
RWKV_STEP_CHUNKS = 4
GLA_STEP_CHUNKS = 4
GLA_SUB = SUBLANES
PROMPT_ROW_TILE = 1024
OUTPROJ_ROW_TILE = 512
TOKENWISE_ROW_TILE = 256
FFN_COL_TILE = 512
SAMPLE_FFN_COL_TILE = 1408
EVEN_COL_TILE = 13 * LANES
ODD_COL_TILE = 8 * LANES
STEP_ROWS = 2 * SUBLANES
ADA_COL_TILE = 1024
ADA_BUFFERS = 3
NORM_ROWS = 32
NORM_UNROLL = 4
EPILOGUE_ROWS = 128


def _params(*sem, vmem=VMEM_LIMIT):
    return pltpu.CompilerParams(dimension_semantics=sem, vmem_limit_bytes=vmem)


def _bdot(a, b):
    return jnp.dot(a.astype(BF16), b.astype(BF16), preferred_element_type=F32)


def _split3(x):
    hi = x.astype(BF16)
    r1 = x - hi.astype(F32)
    mid = r1.astype(BF16)
    lo = (r1 - mid.astype(F32)).astype(BF16)
    return hi, mid, lo


def _dot_exact_rhs(a_bf16, x):
    hi, mid, lo = _split3(x)
    f = lambda y: jnp.dot(a_bf16, y, preferred_element_type=F32)
    return f(hi) + f(mid) + f(lo)


def _dot_exact_lhs(x, b_bf16):
    hi, mid, lo = _split3(x)
    f = lambda y: jnp.dot(y, b_bf16, preferred_element_type=F32)
    return f(hi) + f(mid) + f(lo)


def _sigmoid(x):
    return jax.nn.sigmoid(x)


def _softplus(x):
    return jnp.maximum(x, 0.0) + jnp.log1p(jnp.exp(-jnp.abs(x)))


def _gelu_tanh(x):
    return 0.5 * x * (1.0 + jnp.tanh(0.7978845608028654 * (x + 0.044715 * (x * x * x))))


def _norm_mod(x, shift, scale):
    ms = jnp.mean(x * x, axis=-1, keepdims=True)
    return x * lax.rsqrt(ms + NORM_EPS) * (1.0 + scale) + shift


def _ada_kernel(c_ref, w_hbm, b_ref, o_ref, wbuf, sem, *, tn):
    n_col = pl.num_programs(1)
    n_steps = pl.num_programs(0) * n_col
    s = pl.program_id(0) * n_col + pl.program_id(1)

    def tile_copy(step, slot):
        col = pl.multiple_of((step % n_col) * tn, tn)
        return pltpu.make_async_copy(w_hbm.at[step // n_col, :, pl.ds(col, tn)], wbuf.at[slot], sem.at[slot])

    @pl.when(s == 0)
    def _():
        for k in range(ADA_BUFFERS - 1):
            tile_copy(k, k).start()

    ahead = s + (ADA_BUFFERS - 1)

    @pl.when(ahead < n_steps)
    def _():
        tile_copy(ahead, ahead % ADA_BUFFERS).start()

    slot = s % ADA_BUFFERS
    tile_copy(s, slot).wait()
    c = c_ref[...]
    act = (c * _sigmoid(c)).astype(BF16)
    o_ref[...] = jnp.dot(act, wbuf[slot].astype(BF16), preferred_element_type=F32) + b_ref[...]


def _ada(c_all, w_ada, b_ada):
    rows = c_all.shape[0]
    tn = ADA_COL_TILE
    per = D_MODEL // tn
    b4 = b_ada.reshape(DEPTH, N_MOD * per, 1, tn)
    assert DEPTH * N_MOD * per >= ADA_BUFFERS
    return pl.pallas_call(
        functools.partial(_ada_kernel, tn=tn),
        grid=(DEPTH, N_MOD * per),
        in_specs=[
            pl.BlockSpec((rows, D_MODEL), lambda l, j: (0, 0)),
            pl.BlockSpec(memory_space=pl.ANY),
            pl.BlockSpec((None, None, 1, tn), lambda l, j: (l, j, 0, 0)),
        ],
        out_specs=pl.BlockSpec((None, None, rows, tn), lambda l, j: (l, j // per, 0, j % per)),
        out_shape=jax.ShapeDtypeStruct((DEPTH, N_MOD, rows, D_MODEL), F32),
        scratch_shapes=[pltpu.VMEM((ADA_BUFFERS, D_MODEL, tn), F32), pltpu.SemaphoreType.DMA((ADA_BUFFERS,))],
        compiler_params=_params("arbitrary", "arbitrary"),
        name="ada",
    )(c_all, w_ada, b4)


class _Mod:
    def __init__(self, arr, per_row, seq_len):
        self.arr = arr
        self.per_row = per_row
        self.seq_len = seq_len

    def spec(self, layer, m, tm):
        if self.per_row:
            return pl.BlockSpec((None, None, tm, D_MODEL), lambda i, *_: (layer, m, i, 0))
        per_seq = self.seq_len // tm
        return pl.BlockSpec((None, None, None, 1, D_MODEL), lambda i, *_: (layer, m, i // per_seq, 0, 0))


def _ffn_kernel(x_ref, sh_ref, sc_ref, g_ref, wg_ref, wu_ref, wd_ref, *rest, n_j, psub, sub, final, convert):
    rest = list(rest)
    fw_ref = rest.pop(0) if final else None
    f32_tiles = [rest.pop(0) for _ in range(2)] if convert else []
    o_ref = rest.pop(0)
    bf16_tiles = [rest.pop(0) for _ in range(2)] if convert else []
    h_ref, = rest
    j = pl.program_id(1)
    tm = x_ref.shape[0]

    @pl.when(j == 0)
    def _():
        def body(s, carry):
            rows = pl.ds(pl.multiple_of(s * psub, psub), psub)
            sh = sh_ref[...] if sh_ref.shape[0] == 1 else sh_ref[rows, :]
            sc = sc_ref[...] if sc_ref.shape[0] == 1 else sc_ref[rows, :]
            h_ref[rows, :] = _norm_mod(x_ref[rows, :], sh, sc).astype(BF16)
            o_ref[rows, :] = jnp.zeros((psub, D_MODEL), F32)
            return carry
        lax.fori_loop(0, tm // psub, body, 0, unroll=NORM_UNROLL)

    for src, dst in zip(f32_tiles, bf16_tiles):
        dst[...] = src[...].astype(BF16)

    h = h_ref[...]
    gt = jnp.dot(h, wg_ref[...], preferred_element_type=F32)
    ut = jnp.dot(h, wu_ref[...], preferred_element_type=F32)
    act = (gt * _sigmoid(gt) * ut).astype(BF16)
    o_ref[...] += jnp.dot(act, wd_ref[...], preferred_element_type=F32)

    @pl.when(j == n_j - 1)
    def _():
        def body(s, carry):
            rows = pl.ds(pl.multiple_of(s * sub, sub), sub)
            g = g_ref[...] if g_ref.shape[0] == 1 else g_ref[rows, :]
            y = x_ref[rows, :] + 0.5 * g * o_ref[rows, :]
            if final:
                ms = jnp.mean(y * y, axis=-1, keepdims=True)
                y = y * lax.rsqrt(ms + NORM_EPS) * fw_ref[...]
            o_ref[rows, :] = y
            return carry
        lax.fori_loop(0, tm // sub, body, 0)


def _ffn(x, mod, layer, which, weights, tm, tf, final_w=None, convert=None):
    m_rows = x.shape[0]
    n_i = m_rows // tm
    n_j = D_FF // tf
    m0 = 6 * which
    sub = min(tm, EPILOGUE_ROWS)
    final = final_w is not None
    in_specs = [
        pl.BlockSpec((tm, D_MODEL), lambda i, j: (i, 0)),
        mod.spec(layer, m0, tm), mod.spec(layer, m0 + 1, tm), mod.spec(layer, m0 + 2, tm),
        pl.BlockSpec((D_MODEL, tf), lambda i, j: (0, j)),
        pl.BlockSpec((D_MODEL, tf), lambda i, j: (0, j + n_j)),
        pl.BlockSpec((tf, D_MODEL), lambda i, j: (j, 0)),
    ]
    args = [x, mod.arr, mod.arr, mod.arr, weights[0], weights[0], weights[1]]
    if final:
        in_specs.append(pl.BlockSpec((1, D_MODEL), lambda i, j: (0, 0)))
        args.append(final_w.reshape(1, D_MODEL))
    out_specs = [pl.BlockSpec((tm, D_MODEL), lambda i, j: (i, 0))]
    out_shape = [jax.ShapeDtypeStruct((m_rows, D_MODEL), F32)]
    if convert is not None:
        w_up32, w_down32, l2, s2 = convert
        up_tile = (D_MODEL // n_i, 2 * D_FF // n_j)
        down_tile = (D_FF // n_j, D_MODEL // n_i)
        assert up_tile[0] * n_i == D_MODEL and up_tile[1] * n_j == 2 * D_FF and up_tile[1] % LANES == 0
        assert down_tile[0] * n_j == D_FF and down_tile[1] * n_i == D_MODEL and down_tile[1] % LANES == 0
        in_specs += [pl.BlockSpec((None, None) + up_tile, lambda i, j: (l2, s2, i, j)),
                     pl.BlockSpec((None, None) + down_tile, lambda i, j: (l2, s2, j, i))]
        args += [w_up32, w_down32]
        out_specs += [pl.BlockSpec(up_tile, lambda i, j: (i, j)), pl.BlockSpec(down_tile, lambda i, j: (j, i))]
        out_shape += [jax.ShapeDtypeStruct((D_MODEL, 2 * D_FF), BF16), jax.ShapeDtypeStruct((D_FF, D_MODEL), BF16)]
    out = pl.pallas_call(
        functools.partial(_ffn_kernel, n_j=n_j, psub=min(tm, NORM_ROWS), sub=sub, final=final,
                          convert=convert is not None),
        grid=(n_i, n_j),
        in_specs=in_specs,
        out_specs=out_specs,
        out_shape=out_shape,
        scratch_shapes=[pltpu.VMEM((tm, D_MODEL), BF16)],
        compiler_params=_params("arbitrary", "arbitrary", vmem=FFN_VMEM_LIMIT),
        name="ffn",
    )(*args)
    return (out[0], (out[1], out[2])) if convert is not None else out[0]


def _inproj_kernel(x_ref, sh_ref, sc_ref, w_ref, o_ref, *rest, sub, emit, transposed):
    h_ref = rest[-1]
    j = pl.program_id(1)
    tm = x_ref.shape[0]

    @pl.when(j == 0)
    def _():
        def body(s, carry):
            rows = pl.ds(pl.multiple_of(s * sub, sub), sub)
            sh = sh_ref[...] if sh_ref.shape[0] == 1 else sh_ref[rows, :]
            sc = sc_ref[...] if sc_ref.shape[0] == 1 else sc_ref[rows, :]
            h_ref[rows, :] = _norm_mod(x_ref[rows, :], sh, sc).astype(BF16)
            return carry
        lax.fori_loop(0, tm // sub, body, 0, unroll=NORM_UNROLL)

    w = w_ref[...]
    if transposed:
        w = w.T
    if emit:
        w = w.astype(BF16)
        rest[0][...] = w
    o_ref[...] = jnp.dot(h_ref[...], w, preferred_element_type=F32)


def _inproj(x, mod, layer, w, widx, tm, tn, transposed=False):
    m_rows = x.shape[0]
    n_out = w.shape[-2] if transposed else w.shape[-1]
    sub = min(tm, NORM_ROWS)
    emit = widx is not None
    if transposed:
        w_spec = pl.BlockSpec((None, tn, D_MODEL), lambda i, j: (widx, j, 0))
    elif emit:
        w_spec = pl.BlockSpec((None, D_MODEL, tn), lambda i, j: (widx, 0, j))
    else:
        w_spec = pl.BlockSpec((D_MODEL, tn), lambda i, j: (0, j))
    out_specs = [pl.BlockSpec((tm, tn), lambda i, j: (i, j))]
    out_shape = [jax.ShapeDtypeStruct((m_rows, n_out), F32)]
    if emit:
        assert m_rows == tm
        out_specs.append(pl.BlockSpec((D_MODEL, tn), lambda i, j: (0, j)))
        out_shape.append(jax.ShapeDtypeStruct((D_MODEL, n_out), BF16))
    out = pl.pallas_call(
        functools.partial(_inproj_kernel, sub=sub, emit=emit, transposed=transposed),
        grid=(m_rows // tm, pl.cdiv(n_out, tn)),
        in_specs=[
            pl.BlockSpec((tm, D_MODEL), lambda i, j: (i, 0)),
            mod.spec(layer, 3, tm), mod.spec(layer, 4, tm),
            w_spec,
        ],
        out_specs=out_specs,
        out_shape=out_shape,
        scratch_shapes=[pltpu.VMEM((tm, D_MODEL), BF16)],
        compiler_params=_params("arbitrary", "arbitrary"),
        name="inproj",
    )(x, mod.arr, mod.arr, w)
    return tuple(out) if emit else out[0]


def _outproj_kernel(*refs, n_in, emit):
    y_refs = refs[:n_in]
    w_refs = refs[n_in:2 * n_in]
    x_ref, g_ref, o_ref = refs[2 * n_in:2 * n_in + 3]
    acc = None
    for k, (y_ref, w_ref) in enumerate(zip(y_refs, w_refs)):
        w = w_ref[...]
        if emit:
            w = w.astype(BF16)
            refs[2 * n_in + 3 + k][...] = w
        part = jnp.dot(y_ref[...].astype(BF16), w, preferred_element_type=F32)
        acc = part if acc is None else acc + part
    o_ref[...] = x_ref[...] + g_ref[...] * acc


def _outproj(ys, w, widx, x, mod, layer, tm):
    m_rows = x.shape[0]
    n_in = len(ys)
    kw = ys[0].shape[1]
    emit = widx is not None
    in_specs = [pl.BlockSpec((tm, kw), lambda i: (i, 0)) for _ in ys]
    if emit:
        assert m_rows == tm
        in_specs += [pl.BlockSpec((None, kw, D_MODEL), lambda i, k=k: (widx, k, 0)) for k in range(n_in)]
        w_args = [w] * n_in
    else:
        in_specs += [pl.BlockSpec((kw, D_MODEL), lambda i: (0, 0))] * n_in
        w_args = list(w)
    in_specs += [pl.BlockSpec((tm, D_MODEL), lambda i: (i, 0)), mod.spec(layer, 5, tm)]
    out_specs = [pl.BlockSpec((tm, D_MODEL), lambda i: (i, 0))]
    out_shape = [jax.ShapeDtypeStruct((m_rows, D_MODEL), F32)]
    if emit:
        out_specs += [pl.BlockSpec((kw, D_MODEL), lambda i: (0, 0))] * n_in
        out_shape += [jax.ShapeDtypeStruct((kw, D_MODEL), BF16)] * n_in
    out = pl.pallas_call(
        functools.partial(_outproj_kernel, n_in=n_in, emit=emit),
        grid=(m_rows // tm,),
        in_specs=in_specs,
        out_specs=out_specs,
        out_shape=out_shape,
        compiler_params=_params("arbitrary"),
        name="outproj",
    )(*ys, *w_args, x, mod.arr)
    return (out[0], tuple(out[1:])) if emit else out[0]


def _rwkv_prep_kernel(u_ref, prev_ref, mu_ref, wup_ref, aup_ref, gup_ref, vec_ref, *rest, seq_mode):
    if seq_mode:
        r_ref, lw_ref, k_ref, v_ref, kk_ref, a_ref, g_ref, carry_ref = rest
    else:
        seg_ref, r_ref, lw_ref, k_ref, v_ref, kk_ref, a_ref, g_ref, bonus_ref = rest
    u = u_ref[...]
    tm = u.shape[0]
    if seq_mode:
        @pl.when(pl.program_id(1) == 0)
        def _():
            carry_ref[...] = prev_ref[...]
        rolled = pltpu.roll(u, 1, axis=0)
        row = lax.broadcasted_iota(jnp.int32, u.shape, 0)
        prev = jnp.where(row == 0, carry_ref[...], rolled)
        carry_ref[...] = u[tm - 1:tm, :]
    else:
        prev = prev_ref[...]
    xs = u + mu_ref[...] * (prev - u)
    r = xs[:, 0:A_WIDTH]
    k = xs[:, A_WIDTH:2 * A_WIDTH]
    v = xs[:, 2 * A_WIDTH:3 * A_WIDTH]
    wa_lo = xs[:, 3 * A_WIDTH:3 * A_WIDTH + LANES]
    g_lo = xs[:, 3 * A_WIDTH + LANES:]
    w0, a0, k_k, k_a, r_k = (vec_ref[i:i + 1, :] for i in range(5))
    w_raw = -_softplus(-(w0 + _bdot(jnp.tanh(wa_lo), wup_ref[...]))) - 0.5
    lw = -jnp.exp(w_raw)
    a = _sigmoid(a0 + _bdot(wa_lo, aup_ref[...]))
    g = _bdot(_sigmoid(g_lo), gup_ref[...])
    kk = k * k_k
    k2 = k * (1.0 + (a - 1.0) * k_a)
    if not seq_mode:
        seg = seg_ref[...]
        kk = kk * lax.rsqrt(jnp.maximum(_dot_exact_lhs(kk * kk, seg), 1e-24))
        bonus_ref[...] = _dot_exact_lhs(r * k2 * r_k, seg) * v
    r_ref[...] = r
    lw_ref[...] = lw
    k_ref[...] = k2
    v_ref[...] = v
    kk_ref[...] = kk
    a_ref[...] = a
    g_ref[...] = g


def _rwkv_prep(u, prev, mu, wup_pad, aup_pad, g_up, vec, seg, n_seq, tm, seq_mode):
    rows = u.shape[0]
    per_seq = rows // n_seq // tm if seq_mode else 1
    grid = (n_seq, per_seq) if seq_mode else (rows // tm, 1)
    rowmap = (lambda b, t: (b * per_seq + t, 0)) if seq_mode else (lambda b, t: (b, 0))
    prev_spec = (pl.BlockSpec((None, 1, A_PROJ), lambda b, t: (b, 0, 0)) if seq_mode
                 else pl.BlockSpec((tm, A_PROJ), rowmap))
    const = lambda shape: pl.BlockSpec(shape, lambda b, t: (0,) * len(shape))
    out = jax.ShapeDtypeStruct((rows, A_WIDTH), F32)
    in_specs = [pl.BlockSpec((tm, A_PROJ), rowmap), prev_spec,
                const((1, A_PROJ)), const((LANES, A_WIDTH)), const((LANES, A_WIDTH)),
                const((A_GATE_RANK, A_WIDTH)), const((SUBLANES, A_WIDTH))]
    args = [u, prev, mu, wup_pad, aup_pad, g_up, vec]
    if not seq_mode:
        in_specs.append(const((A_WIDTH, A_WIDTH)))
        args.append(seg)
    n_out = 7 if seq_mode else 8
    return pl.pallas_call(
        functools.partial(_rwkv_prep_kernel, seq_mode=seq_mode),
        grid=grid,
        in_specs=in_specs,
        out_specs=[pl.BlockSpec((tm, A_WIDTH), rowmap)] * n_out,
        out_shape=[out] * n_out,
        scratch_shapes=[pltpu.VMEM((1, A_PROJ), F32)] if seq_mode else [],
        compiler_params=_params("arbitrary", "arbitrary"),
        name="rwkv_prep",
    )(*args)


def _rwkv_post_kernel(y_ref, bonus_ref, g_ref, vec_ref, seg_ref, o_ref):
    y = y_ref[...]
    seg = seg_ref[...]
    inv_n = 1.0 / A_HEAD_DIM
    yc = y - _dot_exact_lhs(y, seg) * inv_n
    var = _dot_exact_lhs(yc * yc, seg) * inv_n
    yn = yc * lax.rsqrt(var + A_GN_EPS) * vec_ref[5:6, :] + vec_ref[6:7, :]
    o_ref[...] = (yn + bonus_ref[...]) * g_ref[...]


def _rwkv_post(y, bonus, g, vec, seg, tm):
    rows = y.shape[0]
    spec = pl.BlockSpec((tm, A_WIDTH), lambda i: (i, 0))
    return pl.pallas_call(
        _rwkv_post_kernel,
        grid=(rows // tm,),
        in_specs=[spec, spec, spec,
                  pl.BlockSpec((SUBLANES, A_WIDTH), lambda i: (0, 0)),
                  pl.BlockSpec((A_WIDTH, A_WIDTH), lambda i: (0, 0))],
        out_specs=spec,
        out_shape=jax.ShapeDtypeStruct((rows, A_WIDTH), F32),
        compiler_params=_params("arbitrary"),
        name="rwkv_post",
    )(y, bonus, g, vec, seg)


def _rwkv_chunk_kernel(r_ref, lw_ref, k_ref, v_ref, kk_ref, a_ref, g_ref, vec_ref, h0_ref, y_ref, hT_ref, h_ref):
    ci = pl.program_id(1)

    @pl.when(ci == 0)
    def _():
        h_ref[...] = h0_ref[...]

    cs = CHUNK
    n2 = 2 * cs
    npair = A_WIDTH // LANES
    pairs = lambda x: jnp.stack([x[:, p * LANES:(p + 1) * LANES] for p in range(npair)])
    vec = vec_ref[...]
    vrow = lambda i: pairs(vec[i:i + 1, :])
    lane = lax.broadcasted_iota(jnp.int32, (1, 1, LANES), 2)
    m0 = (lane < A_HEAD_DIM).astype(F32)
    m1 = 1.0 - m0
    split = lambda x: jnp.concatenate([x * m0, x * m1], axis=1)
    dup = lambda x: jnp.concatenate([x, x], axis=1)
    own = jnp.concatenate([jnp.broadcast_to(m0, (1, cs, LANES)), jnp.broadcast_to(m1, (1, cs, LANES))], axis=1)
    bf = lambda x: x.astype(BF16)

    def bmm(x, y):
        return jnp.einsum('hab,hbc->hac', x, y, preferred_element_type=F32)

    def bmm_nt(x, y):
        return jnp.einsum('han,hbn->hab', x, y, preferred_element_type=F32)

    def bmm_tn(x, y):
        return jnp.einsum('hca,hcb->hab', x, y, preferred_element_type=F32)

    row = lax.broadcasted_iota(jnp.int32, (cs, cs), 0)
    col = lax.broadcasted_iota(jnp.int32, (cs, cs), 1)
    tri = (col <= row).astype(BF16)
    row2 = lax.broadcasted_iota(jnp.int32, (n2, n2), 0)
    col2 = lax.broadcasted_iota(jnp.int32, (n2, n2), 1)
    same = (row2 >= cs) == (col2 >= cs)
    tok_r = jnp.where(row2 >= cs, row2 - cs, row2)
    tok_c = jnp.where(col2 >= cs, col2 - cs, col2)
    strict = jnp.where(same, (tok_c < tok_r).astype(F32), 0.0)
    incl = jnp.where(same, (tok_c <= tok_r).astype(F32), 0.0)
    eye2 = (row2 == col2).astype(F32)
    nrow = lax.broadcasted_iota(jnp.int32, (LANES, LANES), 0)
    ncol = lax.broadcasted_iota(jnp.int32, (LANES, LANES), 1)
    eye_n = (nrow == ncol).astype(F32)

    h = h_ref[...]
    for sub in range(r_ref.shape[0] // cs):
        rows = slice(sub * cs, (sub + 1) * cs)
        lw_all = lw_ref[rows, :]
        c = pairs(_dot_exact_rhs(tri, lw_all))
        lw = pairs(lw_all)
        c_end = c[:, cs - 1:cs, :]
        e_pos = dup(jnp.exp(c))
        e_neg = dup(jnp.exp(-c))
        e_prev = dup(jnp.exp(c - lw))
        e_end = dup(jnp.exp(c_end - c))
        g_end = jnp.exp(c_end)

        kk = split(pairs(kk_ref[rows, :]))
        kk = kk * lax.rsqrt(jnp.maximum(jnp.sum(kk * kk, axis=-1, keepdims=True), 1e-24))
        b_vec = kk * dup(pairs(a_ref[rows, :]))
        r2 = split(pairs(r_ref[rows, :]))
        k2 = split(pairs(k_ref[rows, :]))
        v2 = split(pairs(v_ref[rows, :]))
        bonus = jnp.sum(r2 * k2 * vrow(4), axis=-1, keepdims=True) * v2
        rt = r2 * e_pos
        at = bf(-kk * e_prev)
        bt = bf(b_vec * e_neg)
        kt = bf(k2 * e_neg)
        bh = bf(b_vec * e_end)
        kh = bf(k2 * e_end)
        v2b = bf(v2)

        gram = bmm_nt(jnp.concatenate([at, bf(rt)], axis=1), jnp.concatenate([bt, kt], axis=1))
        a_ab = gram[:, :n2, :n2] * strict
        a_ak = gram[:, :n2, n2:] * strict
        m_b = gram[:, n2:, :n2] * incl
        m_k = gram[:, n2:, n2:] * incl

        tinv = eye2 + a_ab
        a_b = bf(a_ab)
        apow = bmm(a_b, a_b)
        span = 2
        while 2 * span < cs:
            ap_b = bf(apow)
            both = bmm(jnp.concatenate([bf(tinv), ap_b], axis=1), ap_b)
            tinv = tinv + both[:, :n2]
            apow = both[:, n2:]
            span *= 2
        tinv = tinv + bmm(bf(tinv), bf(apow))

        akv = bmm(bf(a_ak), v2b)
        pq = bf(bmm(bf(tinv), jnp.concatenate([at, bf(akv)], axis=-1)))
        mpq = bmm(bf(m_b), pq)
        p2 = rt + mpq[..., :LANES]
        y0 = mpq[..., LANES:] + bmm(bf(m_k), v2b)
        tpq = bmm_tn(bh, pq)
        gm = eye_n * g_end + tpq[..., :LANES]
        hadd = tpq[..., LANES:] + bmm_tn(kh, v2b)

        yh = bmm(jnp.concatenate([bf(p2), bf(gm)], axis=1), bf(h))
        y2 = yh[:, :n2] + y0
        h = yh[:, n2:] + hadd

        inv_n = 1.0 / A_HEAD_DIM
        yc = (y2 - jnp.sum(y2, axis=-1, keepdims=True) * inv_n) * own
        var = jnp.sum(yc * yc, axis=-1, keepdims=True) * inv_n
        z = yc * lax.rsqrt(var + A_GN_EPS) * vrow(5) + vrow(6) * own + bonus
        out = (z[:, :cs] + z[:, cs:]) * pairs(g_ref[rows, :])
        for p in range(npair):
            y_ref[rows, p * LANES:(p + 1) * LANES] = out[p]

    h_ref[...] = h

    @pl.when(ci == pl.num_programs(1) - 1)
    def _():
        hT_ref[...] = h


def _rwkv_chunk(r, lw, k, v, kk, a, g, vec, n_seq, h0):
    rows = r.shape[0]
    step_rows = RWKV_STEP_CHUNKS * CHUNK
    nc = rows // n_seq // step_rows
    npair = A_WIDTH // LANES
    spec = pl.BlockSpec((step_rows, A_WIDTH), lambda b, c: (b * nc + c, 0))
    hspec = pl.BlockSpec((None, npair, LANES, LANES), lambda b, c: (b, 0, 0, 0))
    return pl.pallas_call(
        _rwkv_chunk_kernel,
        grid=(n_seq, nc),
        in_specs=[spec] * 7 + [pl.BlockSpec((SUBLANES, A_WIDTH), lambda b, c: (0, 0)), hspec],
        out_specs=[spec, hspec],
        out_shape=[jax.ShapeDtypeStruct((rows, A_WIDTH), F32),
                   jax.ShapeDtypeStruct((n_seq, npair, LANES, LANES), F32)],
        scratch_shapes=[pltpu.VMEM((npair, LANES, LANES), F32)],
        compiler_params=_params("arbitrary", "arbitrary"),
        name="rwkv_chunk",
    )(r, lw, k, v, kk, a, g, vec, h0)


def _pair_states(s):
    n = s.shape[0]
    nd = A_HEAD_DIM
    h = jnp.swapaxes(s, -1, -2).reshape(n, A_HEADS // 2, 2, nd, nd)
    out = jnp.zeros((n, A_HEADS // 2, 2 * nd, 2 * nd), s.dtype)
    return out.at[:, :, :nd, :nd].set(h[:, :, 0]).at[:, :, nd:, nd:].set(h[:, :, 1])


def _unpair_states(hp):
    n = hp.shape[0]
    nd = A_HEAD_DIM
    h = jnp.stack([hp[:, :, :nd, :nd], hp[:, :, nd:, nd:]], axis=2).reshape(n, A_HEADS, nd, nd)
    return jnp.swapaxes(h, -1, -2)


def _rwkv_step_kernel(r_ref, lw_ref, k_ref, v_ref, kk_ref, a_ref, s_ref, acc_ref, y_ref, so_ref,
                      r_t, w_t, k_t, v_t, nkk_t, b_t, y_t):
    h = pl.program_id(0)
    nd = A_HEAD_DIM

    @pl.when(h == 0)
    def _():
        kk = kk_ref[...]
        r_t[...] = r_ref[...].T
        w_t[...] = jnp.exp(lw_ref[...]).T
        k_t[...] = k_ref[...].T
        v_t[...] = v_ref[...].T
        nkk_t[...] = (-kk).T
        b_t[...] = (kk * a_ref[...]).T

    first = pl.multiple_of(h * nd, nd)
    rows = pl.ds(first, nd)
    r, w, k, nkk, b = r_t[rows, :], w_t[rows, :], k_t[rows, :], nkk_t[rows, :], b_t[rows, :]

    def body(i, carry):
        s = s_ref[i]
        s_kk = jnp.sum(s * nkk, axis=0, keepdims=True)
        s_new = s * w + s_kk * b + v_t[pl.ds(first + i, 1), :] * k
        so_ref[i] = s_new
        y_t[pl.ds(first + i, 1), :] = jnp.sum(s_new * r, axis=0, keepdims=True)
        return carry

    lax.fori_loop(0, nd, body, 0, unroll=4)

    @pl.when(h == pl.num_programs(0) - 1)
    def _():
        y_ref[...] = y_t[...].T


def _rwkv_step(r, lw, k, v, kk, a, states_t, j, new_states):
    n = r.shape[0]
    nd = A_HEAD_DIM
    rspec = pl.BlockSpec((n, A_WIDTH), lambda h: (0, 0))
    sspec = pl.BlockSpec((None, None, nd, nd, n), lambda h: (j, h, 0, 0, 0))
    return pl.pallas_call(
        _rwkv_step_kernel,
        grid=(A_HEADS,),
        in_specs=[rspec] * 6 + [sspec, pl.BlockSpec(memory_space=pl.ANY)],
        out_specs=[rspec, sspec],
        out_shape=[jax.ShapeDtypeStruct((n, A_WIDTH), F32), jax.ShapeDtypeStruct(states_t.shape, F32)],
        input_output_aliases={7: 1},
        scratch_shapes=[pltpu.VMEM((A_WIDTH, n), F32)] * 7,
        compiler_params=_params("arbitrary"),
        name="rwkv_step",
    )(r, lw, k, v, kk, a, states_t, new_states)


def _gla_log_alpha(a128, aup_ref, ab_ref):
    lane = lax.broadcasted_iota(jnp.int32, a128.shape, 1)
    a_lo = jnp.where(lane < B_ALPHA_RANK, a128, 0.0)
    x = _bdot(a_lo, aup_ref[...]) + ab_ref[...]
    return -_softplus(-x) * (1.0 / B_TAU)


def _gla_block(u_ref, aup_ref, ab_ref, nw_ref, o_ref, s_ref):
    ub = u_ref[...]
    cs = ub.shape[0]
    q_all = ub[:, 0:B_KEY_WIDTH] * (B_KEY_DIM ** -0.5)
    k_all = ub[:, B_KEY_WIDTH:2 * B_KEY_WIDTH]
    v_all = ub[:, 2 * B_KEY_WIDTH:2 * B_KEY_WIDTH + B_WIDTH]
    g_all = ub[:, 2 * B_KEY_WIDTH + B_WIDTH:2 * B_KEY_WIDTH + 2 * B_WIDTH]
    a128 = ub[:, 2 * B_KEY_WIDTH + 2 * B_WIDTH:2 * B_KEY_WIDTH + 2 * B_WIDTH + LANES]
    log_a = _gla_log_alpha(a128, aup_ref, ab_ref)

    row = lax.broadcasted_iota(jnp.int32, (cs, cs), 0)
    col = lax.broadcasted_iota(jnp.int32, (cs, cs), 1)
    cum = _dot_exact_rhs((col <= row).astype(BF16), log_a)
    e_pos = jnp.exp(cum)
    cum_end = cum[cs - 1:cs, :]
    e_end = jnp.exp(cum_end - cum)
    g_end = jnp.exp(cum_end)
    nk = B_KEY_DIM
    eye_k = (lax.broadcasted_iota(jnp.int32, (nk, nk), 0) == lax.broadcasted_iota(jnp.int32, (nk, nk), 1)).astype(F32)

    nblk = cs // GLA_SUB
    kw = B_KEY_WIDTH
    cum3 = cum.reshape(nblk, GLA_SUB, kw)
    before = jnp.concatenate([jnp.zeros((1, 1, kw), F32), cum3[:nblk - 1, GLA_SUB - 1:GLA_SUB, :]], axis=0)
    q_rel = q_all * jnp.exp(cum - jnp.broadcast_to(before, (nblk, GLA_SUB, kw)).reshape(cs, kw))
    tok = lax.broadcasted_iota(jnp.int32, (cs, 1), 0)
    q_parts, k_parts = [], []
    for blk in range(1, nblk):
        first = blk * GLA_SUB
        k_parts.append(jnp.where(tok < first, k_all * jnp.exp(jnp.minimum(before[blk] - cum, 0.0)), 0.0))
        q_parts.append(jnp.where((tok >= first) & (tok < first + GLA_SUB), q_rel, 0.0))

    q3 = q_all.reshape(nblk, GLA_SUB, kw)
    k3 = k_all.reshape(nblk, GLA_SUB, kw)
    v3 = v_all.reshape(nblk, GLA_SUB, B_WIDTH)
    sub_i = lax.broadcasted_iota(jnp.int32, (1, GLA_SUB, 1), 1)
    o_diag = [jnp.zeros((nblk, GLA_SUB, B_VAL_DIM), F32) for _ in range(B_HEADS)]
    for jj in range(GLA_SUB):
        pair = q3 * k3[:, jj:jj + 1, :] * jnp.exp(jnp.minimum(cum3 - cum3[:, jj:jj + 1, :], 0.0))
        pair = jnp.where(sub_i >= jj, pair, 0.0)
        for h in range(B_HEADS):
            score = jnp.sum(pair[:, :, h * nk:(h + 1) * nk], axis=-1, keepdims=True)
            o_diag[h] = o_diag[h] + score * v3[:, jj:jj + 1, h * B_VAL_DIM:(h + 1) * B_VAL_DIM]

    for h in range(B_HEADS):
        ks = slice(h * B_KEY_DIM, (h + 1) * B_KEY_DIM)
        vs = slice(h * B_VAL_DIM, (h + 1) * B_VAL_DIM)
        qe = q_all[:, ks] * e_pos[:, ks]
        ke = k_all[:, ks] * e_end[:, ks]
        vh = v_all[:, vs]
        s = s_ref[h]
        q_cat = jnp.concatenate([part[:, ks] for part in q_parts], axis=-1).astype(BF16)
        k_cat = jnp.concatenate([part[:, ks] for part in k_parts], axis=-1).astype(BF16)
        scores = jnp.einsum('id,jd->ij', q_cat, k_cat, preferred_element_type=F32)
        o = _bdot(scores, vh) + _bdot(qe, s) + o_diag[h].reshape(cs, B_VAL_DIM)
        g_col = jnp.sum(eye_k * g_end[:, ks], axis=-1, keepdims=True)
        s_ref[h] = s * g_col + jnp.einsum('jd,jv->dv', ke.astype(BF16), vh.astype(BF16),
                                           preferred_element_type=F32)
        ms = jnp.mean(o * o, axis=-1, keepdims=True)
        gh = g_all[:, vs]
        o_ref[:, vs] = o * lax.rsqrt(ms + NORM_EPS) * nw_ref[:, vs] * (gh * _sigmoid(gh))


def _gla_chunk_kernel(u_ref, aup_ref, ab_ref, nw_ref, s0_ref, o_ref, sT_ref, s_ref):
    ci = pl.program_id(1)

    @pl.when(ci == 0)
    def _():
        s_ref[...] = s0_ref[...]

    for sub in range(u_ref.shape[0] // CHUNK):
        rows = pl.ds(sub * CHUNK, CHUNK)
        _gla_block(u_ref.at[rows, :], aup_ref, ab_ref, nw_ref, o_ref.at[rows, :], s_ref)

    @pl.when(ci == pl.num_programs(1) - 1)
    def _():
        sT_ref[...] = s_ref[...]


def _gla_chunk(u, n_seq, alpha_up_pad, alpha_b, norm_w, s0):
    rows = u.shape[0]
    step_rows = GLA_STEP_CHUNKS * CHUNK
    nc = rows // n_seq // step_rows
    return pl.pallas_call(
        _gla_chunk_kernel,
        grid=(n_seq, nc),
        in_specs=[
            pl.BlockSpec((step_rows, A_PROJ), lambda b, c: (b * nc + c, 1)),
            pl.BlockSpec((LANES, B_KEY_WIDTH), lambda b, c: (0, 0)),
            pl.BlockSpec((1, B_KEY_WIDTH), lambda b, c: (0, 0)),
            pl.BlockSpec((1, B_WIDTH), lambda b, c: (0, 0)),
            pl.BlockSpec((None, B_HEADS, B_KEY_DIM, B_VAL_DIM), lambda b, c: (b, 0, 0, 0)),
        ],
        out_specs=[
            pl.BlockSpec((step_rows, B_WIDTH), lambda b, c: (b * nc + c, 0)),
            pl.BlockSpec((None, B_HEADS, B_KEY_DIM, B_VAL_DIM), lambda b, c: (b, 0, 0, 0)),
        ],
        out_shape=[jax.ShapeDtypeStruct((rows, B_WIDTH), F32),
                   jax.ShapeDtypeStruct((n_seq, B_HEADS, B_KEY_DIM, B_VAL_DIM), F32)],
        scratch_shapes=[pltpu.VMEM((B_HEADS, B_KEY_DIM, B_VAL_DIM), F32)],
        compiler_params=_params("arbitrary", "arbitrary"),
        name="gla_chunk",
    )(u, alpha_up_pad, alpha_b, norm_w, s0)


def _gla_step_kernel(u_ref, aup_ref, ab_ref, nw_ref, s_ref, acc_ref, o_ref, so_ref, la_ref):
    nb = u_ref.shape[0]
    nk = B_KEY_DIM
    off_k, off_v, off_g = B_KEY_WIDTH, 2 * B_KEY_WIDTH, 2 * B_KEY_WIDTH + B_WIDTH
    off_a = 2 * B_KEY_WIDTH + 2 * B_WIDTH
    la_ref[...] = _gla_log_alpha(u_ref[:, off_a:off_a + LANES], aup_ref, ab_ref)
    eye = (lax.broadcasted_iota(jnp.int32, (nk, nk), 0) == lax.broadcasted_iota(jnp.int32, (nk, nk), 1)).astype(F32)
    col = lambda x: jnp.sum(eye * x, axis=-1, keepdims=True)

    def body(n, carry):
        row = pl.ds(n, 1)
        ur = u_ref[row, :]
        dec_all = jnp.exp(la_ref[row, :])
        outs = []
        for h in range(B_HEADS):
            q = ur[:, h * nk:(h + 1) * nk] * (B_KEY_DIM ** -0.5)
            k = ur[:, off_k + h * nk:off_k + (h + 1) * nk]
            v = ur[:, off_v + h * B_VAL_DIM:off_v + (h + 1) * B_VAL_DIM]
            g = ur[:, off_g + h * B_VAL_DIM:off_g + (h + 1) * B_VAL_DIM]
            dec = dec_all[:, h * nk:(h + 1) * nk]
            s = s_ref[n, h]
            qk = jnp.sum(q * k, axis=-1, keepdims=True)
            o = qk * v + jnp.sum(col(q * dec) * s, axis=0, keepdims=True)
            so_ref[n, h] = s * col(dec) + col(k) * v
            ms = jnp.mean(o * o, axis=-1, keepdims=True)
            outs.append(o * lax.rsqrt(ms + NORM_EPS) * nw_ref[:, h * B_VAL_DIM:(h + 1) * B_VAL_DIM]
                        * (g * _sigmoid(g)))
        o_ref[row, :] = jnp.concatenate(outs, axis=-1)
        return carry

    lax.fori_loop(0, nb, body, 0)


def _gla_step(u, alpha_up_pad, alpha_b, norm_w, states, j, nb, new_states):
    n = u.shape[0]
    sspec = pl.BlockSpec((None, nb, B_HEADS, B_KEY_DIM, B_VAL_DIM), lambda i: (j, i, 0, 0, 0))
    return pl.pallas_call(
        _gla_step_kernel,
        grid=(n // nb,),
        in_specs=[pl.BlockSpec((nb, A_PROJ), lambda i: (i, 1)),
                  pl.BlockSpec((LANES, B_KEY_WIDTH), lambda i: (0, 0)),
                  pl.BlockSpec((1, B_KEY_WIDTH), lambda i: (0, 0)),
                  pl.BlockSpec((1, B_WIDTH), lambda i: (0, 0)),
                  sspec, pl.BlockSpec(memory_space=pl.ANY)],
        out_specs=[pl.BlockSpec((nb, B_WIDTH), lambda i: (i, 0)), sspec],
        out_shape=[jax.ShapeDtypeStruct((n, B_WIDTH), F32), jax.ShapeDtypeStruct(states.shape, F32)],
        input_output_aliases={5: 1},
        scratch_shapes=[pltpu.VMEM((nb, B_KEY_WIDTH), F32)],
        compiler_params=_params("arbitrary"),
        name="gla_step",
    )(u, alpha_up_pad, alpha_b, norm_w, states, new_states)


def _lru_gates(xc, wa_ref, wx_ref, ba_ref, bx_ref, lam_ref):
    ra, ix = [], []
    for nb in range(C_BLOCKS):
        xb = xc[:, nb * C_BLOCK:(nb + 1) * C_BLOCK].astype(BF16)
        ra.append(jnp.dot(xb, wa_ref[nb].astype(BF16), preferred_element_type=F32))
        ix.append(jnp.dot(xb, wx_ref[nb].astype(BF16), preferred_element_type=F32))
    r = _sigmoid(jnp.concatenate(ra, axis=-1) + ba_ref[...])
    i_g = _sigmoid(jnp.concatenate(ix, axis=-1) + bx_ref[...])
    log_a = -C_POW * r * _softplus(-lam_ref[...])
    a = jnp.exp(log_a)
    b = jnp.sqrt(1.0 - a * a) * (i_g * xc)
    return a, b


def _lru_seq_kernel(u_ref, cp_ref, h0_ref, cw_ref, cb_ref, wa_ref, wx_ref, ba_ref, bx_ref, lam_ref,
                    y_ref, tail_ref, hT_ref, carry_ref, h_ref, a_s, b_s):
    ti = pl.program_id(1)

    @pl.when(ti == 0)
    def _():
        carry_ref[...] = cp_ref[...]
        h_ref[...] = h0_ref[...]

    gate = u_ref[:, 0:C_WIDTH]
    xb = u_ref[:, C_WIDTH:2 * C_WIDTH]
    tm = xb.shape[0]
    carry = carry_ref[...]
    row8 = lax.broadcasted_iota(jnp.int32, (SUBLANES, C_WIDTH), 0)
    xc = cb_ref[...] + cw_ref[C_CONV - 1:C_CONV, :] * xb
    for s in range(1, C_CONV):
        rolled = pltpu.roll(xb, s, axis=0)
        head = jnp.where(row8 < s, pltpu.roll(carry, s, axis=0), rolled[0:SUBLANES, :])
        shifted = jnp.concatenate([head, rolled[SUBLANES:, :]], axis=0)
        xc = xc + cw_ref[C_CONV - 1 - s:C_CONV - s, :] * shifted
    carry_ref[...] = xb[tm - SUBLANES:tm, :]
    tail_ref[...] = xb[tm - SUBLANES:tm, :]

    a, b = _lru_gates(xc, wa_ref, wx_ref, ba_ref, bx_ref, lam_ref)
    a_s[...] = a
    b_s[...] = b

    def body(t, h):
        h = a_s[pl.ds(t, 1), :] * h + b_s[pl.ds(t, 1), :]
        b_s[pl.ds(t, 1), :] = h
        return h

    h_last = lax.fori_loop(0, tm, body, h_ref[...], unroll=8)
    h_ref[...] = h_last
    hT_ref[...] = h_last
    y_ref[...] = _gelu_tanh(gate) * b_s[...]


def _lru_seq(u, n_seq, conv_prev8, h0, conv_w, conv_b, wa, wx, ba, bx, lam, tm):
    rows = u.shape[0]
    per_seq = rows // n_seq // tm
    const = lambda shape: pl.BlockSpec(shape, lambda b, t: (0,) * len(shape))
    return pl.pallas_call(
        _lru_seq_kernel,
        grid=(n_seq, per_seq),
        in_specs=[
            pl.BlockSpec((tm, 2 * C_WIDTH), lambda b, t: (b * per_seq + t, 0)),
            pl.BlockSpec((None, SUBLANES, C_WIDTH), lambda b, t: (b, 0, 0)),
            pl.BlockSpec((None, 1, C_WIDTH), lambda b, t: (b, 0, 0)),
            const((C_CONV, C_WIDTH)), const((1, C_WIDTH)),
            const((C_BLOCKS, C_BLOCK, C_BLOCK)), const((C_BLOCKS, C_BLOCK, C_BLOCK)),
            const((1, C_WIDTH)), const((1, C_WIDTH)), const((1, C_WIDTH)),
        ],
        out_specs=[
            pl.BlockSpec((tm, C_WIDTH), lambda b, t: (b * per_seq + t, 0)),
            pl.BlockSpec((None, SUBLANES, C_WIDTH), lambda b, t: (b, 0, 0)),
            pl.BlockSpec((None, 1, C_WIDTH), lambda b, t: (b, 0, 0)),
        ],
        out_shape=[jax.ShapeDtypeStruct((rows, C_WIDTH), F32),
                   jax.ShapeDtypeStruct((n_seq, SUBLANES, C_WIDTH), F32),
                   jax.ShapeDtypeStruct((n_seq, 1, C_WIDTH), F32)],
        scratch_shapes=[pltpu.VMEM((SUBLANES, C_WIDTH), F32), pltpu.VMEM((1, C_WIDTH), F32),
                        pltpu.VMEM((tm, C_WIDTH), F32), pltpu.VMEM((tm, C_WIDTH), F32)],
        compiler_params=_params("arbitrary", "arbitrary"),
        name="lru_seq",
    )(u, conv_prev8, h0, conv_w, conv_b, wa, wx, ba, bx, lam)


def _lru_step_kernel(u_ref, cp_ref, h0_ref, cw_ref, cb_ref, wa_ref, wx_ref, ba_ref, bx_ref, lam_ref,
                     y_ref, cn_ref, h_ref):
    gate = u_ref[:, 0:C_WIDTH]
    xb = u_ref[:, C_WIDTH:2 * C_WIDTH]
    xc = cb_ref[...] + cw_ref[C_CONV - 1:C_CONV, :] * xb
    for i in range(C_CONV - 1):
        xc = xc + cw_ref[i:i + 1, :] * cp_ref[i]
    a, b = _lru_gates(xc, wa_ref, wx_ref, ba_ref, bx_ref, lam_ref)
    h = a * h0_ref[...] + b
    h_ref[...] = h
    y_ref[...] = _gelu_tanh(gate) * h
    for i in range(C_CONV - 2):
        cn_ref[i] = cp_ref[i + 1]
    cn_ref[C_CONV - 2] = xb


def _lru_step(u, conv_states, h0, j, conv_w, conv_b, wa, wx, ba, bx, lam):
    rows = u.shape[0]
    full = lambda shape: pl.BlockSpec(shape, lambda i: (0,) * len(shape))
    return pl.pallas_call(
        _lru_step_kernel,
        grid=(1,),
        in_specs=[full((rows, 2 * C_WIDTH)),
                  pl.BlockSpec((None, C_CONV - 1, rows, C_WIDTH), lambda i: (j, 0, 0, 0)),
                  full((rows, C_WIDTH)),
                  full((C_CONV, C_WIDTH)), full((1, C_WIDTH)),
                  full((C_BLOCKS, C_BLOCK, C_BLOCK)), full((C_BLOCKS, C_BLOCK, C_BLOCK)),
                  full((1, C_WIDTH)), full((1, C_WIDTH)), full((1, C_WIDTH))],
        out_specs=[full((rows, C_WIDTH)), full((C_CONV - 1, rows, C_WIDTH)), full((rows, C_WIDTH))],
        out_shape=[jax.ShapeDtypeStruct((rows, C_WIDTH), F32),
                   jax.ShapeDtypeStruct((C_CONV - 1, rows, C_WIDTH), F32),
                   jax.ShapeDtypeStruct((rows, C_WIDTH), F32)],
        compiler_params=_params("arbitrary"),
        name="lru_step",
    )(u, conv_states, h0, conv_w, conv_b, wa, wx, ba, bx, lam)


def _pad_rows(w, first, total):
    return jnp.zeros((total, w.shape[1]), w.dtype).at[first:first + w.shape[0]].set(w)


class _Group:
    def __init__(self, x, mod, n_seq, seq_len, states, tm, tf):
        self.x, self.mod, self.n_seq, self.seq_len, self.states, self.tm, self.tf = x, mod, n_seq, seq_len, states, tm, tf
        self.prompt = seq_len > 1
        self.out_shift, self.out_wkv, self.out_gla, self.out_conv, self.out_lru = [], [], [], [], []
        self.new_wkv_t = self.new_gla = None

    def new_states(self):
        wkv = jnp.stack(self.out_wkv) if self.new_wkv_t is None else jnp.transpose(self.new_wkv_t, (0, 4, 1, 2, 3))
        gla = jnp.stack(self.out_gla) if self.new_gla is None else self.new_gla
        return jnp.stack(self.out_shift), wkv, gla, jnp.stack(self.out_conv), jnp.stack(self.out_lru)


def _mixer(grp, l, wts, mxu):
    (w_ffn_up, w_ffn_down, w_in_even, w_out_even, a_mu, a_w_up, a_a_up, a_g_up, a_vec, b_alpha_up,
     b_alpha_b, b_norm_w, w_in_odd, w_out_odd, c_conv_w, c_conv_b, c_wa, c_ba, c_wx, c_bx, c_lam,
     final_norm_w) = wts
    x, mod, n_seq, seq_len, tm, prompt = grp.x, grp.mod, grp.n_seq, grp.seq_len, grp.tm, grp.prompt
    st_shift, st_wkv, st_gla, st_conv, st_lru = grp.states
    rows = x.shape[0]
    j = l // 2
    tn_even, tn_odd = EVEN_COL_TILE, ODD_COL_TILE
    seg = jnp.kron(jnp.eye(A_HEADS, dtype=F32), jnp.ones((A_HEAD_DIM, A_HEAD_DIM), F32)).astype(BF16)
    if l % 2 == 0:
        if prompt:
            u = _inproj(x, mod, l, mxu['in', l], None, tm, tn_even)
        else:
            u, mxu['in', l] = _inproj(x, mod, l, jnp.swapaxes(w_in_even, 1, 2), j, tm, tn_even,
                                      transposed=True)
        wup_pad = _pad_rows(a_w_up[j], 0, LANES)
        aup_pad = _pad_rows(a_a_up[j], A_DECAY_RANK, LANES)
        vec8 = _pad_rows(a_vec[j], 0, SUBLANES)
        alpha_pad = _pad_rows(b_alpha_up[j], 0, LANES)
        alpha_b = b_alpha_b[j].reshape(1, B_KEY_WIDTH)
        mu = a_mu[j].reshape(1, A_PROJ)
        norm_w = b_norm_w[j].reshape(1, B_WIDTH)
        if prompt:
            r, lw, k2, v, kk, al, g = _rwkv_prep(u, st_shift[j].reshape(n_seq, 1, A_PROJ), mu, wup_pad,
                                                 aup_pad, a_g_up[j], vec8, seg, n_seq, TOKENWISE_ROW_TILE, True)
            y_a, h_t = _rwkv_chunk(r, lw, k2, v, kk, al, g, vec8, n_seq, _pair_states(st_wkv[j]))
            s_wkv = _unpair_states(h_t)
            s_shift = u.reshape(n_seq, seq_len, EVEN_PROJ)[:, -1, :A_PROJ]
            y_b, s_gla = _gla_chunk(u, n_seq, alpha_pad, alpha_b, norm_w, st_gla[j])
        else:
            r, lw, k2, v, kk, al, g, bonus = _rwkv_prep(u, st_shift[j], mu, wup_pad, aup_pad, a_g_up[j],
                                                        vec8, seg, rows, rows, False)
            st_wkv_t = jnp.transpose(st_wkv, (0, 2, 3, 4, 1))
            y_raw, grp.new_wkv_t = _rwkv_step(r, lw, k2, v, kk, al, st_wkv_t, j,
                                              jnp.zeros_like(st_wkv_t) if grp.new_wkv_t is None else grp.new_wkv_t)
            y_a = _rwkv_post(y_raw, bonus, g, vec8, seg, rows)
            s_shift = u[:, :A_PROJ]
            y_b, grp.new_gla = _gla_step(u, alpha_pad, alpha_b, norm_w, st_gla, j, STEP_ROWS,
                                         jnp.zeros_like(st_gla) if grp.new_gla is None else grp.new_gla)
        if prompt:
            x = _outproj([y_a, y_b], mxu['out', l], None, x, mod, l, OUTPROJ_ROW_TILE)
        else:
            x, mxu['out', l] = _outproj([y_a, y_b], w_out_even, j, x, mod, l, tm)
        grp.out_shift.append(s_shift)
        if prompt:
            grp.out_wkv.append(s_wkv)
            grp.out_gla.append(s_gla)
    else:
        if prompt:
            u = _inproj(x, mod, l, mxu['in', l], None, tm, tn_odd)
        else:
            u, mxu['in', l] = _inproj(x, mod, l, w_in_odd, j, tm, tn_odd)
        lru_w = (c_conv_w[j], c_conv_b[j].reshape(1, C_WIDTH), c_wa[j], c_wx[j],
                 c_ba[j].reshape(1, C_WIDTH), c_bx[j].reshape(1, C_WIDTH), c_lam[j].reshape(1, C_WIDTH))
        if prompt:
            cp8 = jnp.concatenate([jnp.zeros((n_seq, SUBLANES - (C_CONV - 1), C_WIDTH), F32), st_conv[j]], axis=1)
            y_c, tail, h_t = _lru_seq(u, n_seq, cp8, st_lru[j].reshape(n_seq, 1, C_WIDTH), *lru_w,
                                      TOKENWISE_ROW_TILE)
            s_conv = tail[:, SUBLANES - (C_CONV - 1):, :]
            s_lru = h_t.reshape(n_seq, C_WIDTH)
        else:
            y_c, cn, s_lru = _lru_step(u, jnp.swapaxes(st_conv, 1, 2), st_lru[j], j, *lru_w)
            s_conv = jnp.swapaxes(cn, 0, 1)
        if prompt:
            x = _outproj([y_c], mxu['out', l], None, x, mod, l, OUTPROJ_ROW_TILE)
        else:
            x, mxu['out', l] = _outproj([y_c], w_out_odd, j, x, mod, l, tm)
        grp.out_conv.append(s_conv)
        grp.out_lru.append(s_lru)
    grp.x = x


def _trunk(sample, prompt, wts):
    w_ffn_up, w_ffn_down, final_norm_w = wts[0], wts[1], wts[-1]
    subs = [(l, which) for l in range(DEPTH) for which in (0, 1)]
    mxu = {('ffn', 0, 0): (w_ffn_up[0, 0].astype(BF16), w_ffn_down[0, 0].astype(BF16))}
    for idx, (l, which) in enumerate(subs):
        final_w = final_norm_w if idx == len(subs) - 1 else None
        weights = mxu['ffn', l, which]
        sample.x = _ffn(sample.x, sample.mod, l, which, weights, sample.tm, sample.tf, final_w=final_w)
        if idx + 1 < len(subs):
            nxt = subs[idx + 1]
            prompt.x, mxu[('ffn',) + nxt] = _ffn(prompt.x, prompt.mod, l, which, weights, prompt.tm, prompt.tf,
                                                 convert=(w_ffn_up, w_ffn_down) + nxt)
        else:
            prompt.x = _ffn(prompt.x, prompt.mod, l, which, weights, prompt.tm, prompt.tf, final_w=final_w)
        if which == 0:
            _mixer(sample, l, wts, mxu)
            _mixer(prompt, l, wts, mxu)


def kernel(x_prompt, x_sample, c_prompt, c_sample, state_rwkv_shift, state_rwkv_wkv, state_gla, state_conv, state_lru, w_ada, b_ada, w_ffn_up, w_ffn_down, w_in_even, w_out_even, a_mu, a_w_up, a_a_up, a_g_up, a_vec, b_alpha_up, b_alpha_b, b_norm_w, w_in_odd, w_out_odd, c_conv_w, c_conv_b, c_wa, c_ba, c_wx, c_bx, c_lam, final_norm_w):
    wts = (w_ffn_up, w_ffn_down, w_in_even, w_out_even, a_mu, a_w_up, a_a_up, a_g_up, a_vec, b_alpha_up,
           b_alpha_b, b_norm_w, w_in_odd, w_out_odd, c_conv_w, c_conv_b, c_wa, c_ba, c_wx, c_bx, c_lam,
           final_norm_w)
    n_p, t_p, _ = x_prompt.shape
    n_s, t_s, _ = x_sample.shape
    assert t_s == 1

    c_rows = n_p + n_s
    pad = (-c_rows) % 16
    c_all = jnp.concatenate([c_sample, c_prompt, jnp.zeros((pad, D_MODEL), F32)], axis=0)
    mod_all = _ada(c_all, w_ada, b_ada)
    mod_p = _Mod(mod_all[:, :, n_s:c_rows].reshape(DEPTH, N_MOD, n_p, 1, D_MODEL), False, t_p)
    mod_s = _Mod(mod_all, True, 1)

    fresh = lambda s: jnp.zeros((s.shape[0], n_p) + s.shape[2:], s.dtype)
    states = (state_rwkv_shift, state_rwkv_wkv, state_gla, state_conv, state_lru)
    sample = _Group(x_sample.reshape(n_s, D_MODEL), mod_s, n_s, 1, states, n_s, SAMPLE_FFN_COL_TILE)
    prompt = _Group(x_prompt.reshape(n_p * t_p, D_MODEL), mod_p, n_p, t_p, tuple(fresh(s) for s in states),
                    PROMPT_ROW_TILE, FFN_COL_TILE)
    _trunk(sample, prompt, wts)
    return (prompt.x.reshape(n_p, t_p, D_MODEL), sample.x.reshape(n_s, 1, D_MODEL),
            *prompt.new_states(), *sample.new_states())
```
